```python
import jax, jax.numpy as jnp
from jax import lax
import numpy as np

D_MODEL = 2048
BATCH = 8
SEQ = 4096
DEPTH = 2

CHUNK = 64
N_META = 16
D_LRU = D_MODEL // 2
D_CONV = D_MODEL // 2
D_MIX = D_LRU + D_CONV
LRU_HEADS = 16
LRU_HEAD_DIM = D_LRU // LRU_HEADS
CONV_GROUPS = 16
LRU_CONV_W = 4
SHORT_CONV_W = 3
LRU_C = 8.0
RMS_EPS = 1e-6
SPLIT_SIZES = (D_LRU, D_LRU, D_CONV, D_CONV, D_CONV, D_CONV)
D_IN = sum(SPLIT_SIZES)
SPLIT_IDX = tuple(int(v) for v in np.cumsum(SPLIT_SIZES)[:-1])

kernel_name = "hymba_rglru_shortconv_trunk"


def rmsnorm(x, g):
    xf = x.astype(jnp.float32)
    y = xf * lax.rsqrt(jnp.mean(xf * xf, axis=-1, keepdims=True) + RMS_EPS)
    return (y * g.astype(jnp.float32)).astype(x.dtype)


def causal_depthwise_conv(x, w):
    k, c = w.shape
    return lax.conv_general_dilated(
        x, w[:, None, :].astype(x.dtype), window_strides=(1,),
        padding=((k - 1, 0),), dimension_numbers=("NWC", "WIO", "NWC"),
        feature_group_count=c)


def rg_lru(x, wr, br, wi, bi, lam):
    bsz, length, _ = x.shape
    xf = x.astype(jnp.float32)
    xh = xf.reshape(bsz, length, LRU_HEADS, LRU_HEAD_DIM)
    r = jax.nn.sigmoid(jnp.einsum("blhi,hij->blhj", xh, wr.astype(jnp.float32))
                       .reshape(bsz, length, D_LRU) + br.astype(jnp.float32))
    i = jax.nn.sigmoid(jnp.einsum("blhi,hij->blhj", xh, wi.astype(jnp.float32))
                       .reshape(bsz, length, D_LRU) + bi.astype(jnp.float32))
    log_a = -LRU_C * r * jax.nn.softplus(-lam.astype(jnp.float32))
    a = jnp.exp(log_a)
    b = jnp.sqrt(-jnp.expm1(2.0 * log_a)) * (i * xf)

    def combine(left, right):
        a1, b1 = left
        a2, b2 = right
        return a1 * a2, a2 * b1 + b2

    _, h = lax.associative_scan(combine, (a, b), axis=1)
    return h.astype(x.dtype)


def hybrid_layer(x, norm_g, w_in, conv_a_w, conv_a_b, lru_wr, lru_br, lru_wi,
                 lru_bi, lru_lambda, conv_b_w, w_out):
    h = rmsnorm(x, norm_g)
    u = jnp.einsum("bld,de->ble", h, w_in.astype(h.dtype))
    xa, ga, gate_b, gate_c, xb, gb = jnp.split(u, SPLIT_IDX, axis=-1)
    xa = causal_depthwise_conv(xa, conv_a_w) + conv_a_b.astype(xa.dtype)
    ya = rg_lru(xa, lru_wr, lru_br, lru_wi, lru_bi, lru_lambda) * jax.nn.silu(ga)
    yb = gate_b * causal_depthwise_conv(gate_c * xb, conv_b_w) * jax.nn.silu(gb)
    y = jnp.concatenate([ya, yb], axis=-1)
    return x + jnp.einsum("ble,ed->bld", y, w_out.astype(y.dtype))


def _fwd_setup_inputs(seed: int = 0) -> dict:
    key = jax.random.key(seed)
    ks = jax.random.split(key, 16)
    f32 = jnp.float32
    x = jax.random.normal(ks[0], (BATCH, SEQ, D_MODEL), f32)
    meta = jax.random.normal(ks[1], (N_META, D_MODEL), f32)
    norm_g = 1.0 + 0.01 * jax.random.normal(ks[2], (DEPTH, D_MODEL), f32)
    w_in = jax.random.normal(ks[3], (DEPTH, D_MODEL, D_IN), f32) * D_MODEL ** -0.5
    conv_a_w = jax.random.normal(ks[4], (DEPTH, LRU_CONV_W, D_LRU), f32) * LRU_CONV_W ** -0.5
    conv_a_b = 0.01 * jax.random.normal(ks[5], (DEPTH, D_LRU), f32)
    lru_wr = jax.random.normal(ks[6], (DEPTH, LRU_HEADS, LRU_HEAD_DIM, LRU_HEAD_DIM), f32) * LRU_HEAD_DIM ** -0.5
    lru_br = 0.01 * jax.random.normal(ks[7], (DEPTH, D_LRU), f32)
    lru_wi = jax.random.normal(ks[8], (DEPTH, LRU_HEADS, LRU_HEAD_DIM, LRU_HEAD_DIM), f32) * LRU_HEAD_DIM ** -0.5
    lru_bi = 0.01 * jax.random.normal(ks[9], (DEPTH, D_LRU), f32)
    a_c = jax.random.uniform(ks[10], (DEPTH, D_LRU), f32, 0.9, 0.999)
    a0 = a_c ** (1.0 / LRU_C)
    lru_lambda = jnp.log(a0) - jnp.log1p(-a0)
    conv_b_w = jax.random.normal(ks[11], (DEPTH, SHORT_CONV_W, D_CONV), f32) * SHORT_CONV_W ** -0.5
    w_out = jax.random.normal(ks[12], (DEPTH, D_MIX, D_MODEL), f32) * D_MIX ** -0.5
    final_g = 1.0 + 0.01 * jax.random.normal(ks[13], (D_MODEL,), f32)
    return {"x": x, "meta": meta, "norm_g": norm_g, "w_in": w_in,
            "conv_a_w": conv_a_w, "conv_a_b": conv_a_b, "lru_wr": lru_wr,
            "lru_br": lru_br, "lru_wi": lru_wi, "lru_bi": lru_bi,
            "lru_lambda": lru_lambda, "conv_b_w": conv_b_w, "w_out": w_out,
            "final_g": final_g}


def _fwd_reference(x, meta, norm_g, w_in, conv_a_w, conv_a_b, lru_wr, lru_br, lru_wi,
              lru_bi, lru_lambda, conv_b_w, w_out, final_g):
    bsz = x.shape[0]
    m = jnp.broadcast_to(meta.astype(x.dtype)[None], (bsz, N_META, D_MODEL))
    h = jnp.concatenate([m, x], axis=1)
    for layer in range(DEPTH):
        h = hybrid_layer(h, norm_g[layer], w_in[layer], conv_a_w[layer],
                         conv_a_b[layer], lru_wr[layer], lru_br[layer],
                         lru_wi[layer], lru_bi[layer], lru_lambda[layer],
                         conv_b_w[layer], w_out[layer])
    return rmsnorm(h[:, N_META:], final_g)


import jax as _jax
import jax.numpy as _jnp

TWIN_FORMAT = 'train_step'
FWD_PARAMS = ['x', 'meta', 'norm_g', 'w_in', 'conv_a_w', 'conv_a_b', 'lru_wr', 'lru_br', 'lru_wi', 'lru_bi', 'lru_lambda', 'conv_b_w', 'w_out', 'final_g']
TWIN_WEIGHTS = ['meta', 'norm_g', 'w_in', 'conv_a_w', 'conv_a_b', 'lru_wr', 'lru_br', 'lru_wi', 'lru_bi', 'lru_lambda', 'conv_b_w', 'w_out', 'final_g']
TWIN_DIFF_INPUT = 'x'
TWIN_INPUTS = ['x', 'meta', 'norm_g', 'w_in', 'conv_a_w', 'conv_a_b', 'lru_wr', 'lru_br', 'lru_wi', 'lru_bi', 'lru_lambda', 'conv_b_w', 'w_out', 'final_g', 'loss_target', 'm_meta', 'm_norm_g', 'm_w_in', 'm_conv_a_w', 'm_conv_a_b', 'm_lru_wr', 'm_lru_br', 'm_lru_wi', 'm_lru_bi', 'm_lru_lambda', 'm_conv_b_w', 'm_w_out', 'm_final_g', 'v_meta', 'v_norm_g', 'v_w_in', 'v_conv_a_w', 'v_conv_a_b', 'v_lru_wr', 'v_lru_br', 'v_lru_wi', 'v_lru_bi', 'v_lru_lambda', 'v_conv_b_w', 'v_w_out', 'v_final_g']
TWIN_OUTPUTS = ['loss', 'grad_x', 'grad_meta', 'grad_norm_g', 'grad_w_in', 'grad_conv_a_w', 'grad_conv_a_b', 'grad_lru_wr', 'grad_lru_br', 'grad_lru_wi', 'grad_lru_bi', 'grad_lru_lambda', 'grad_conv_b_w', 'grad_w_out', 'grad_final_g', 'delta_meta', 'delta_norm_g', 'delta_w_in', 'delta_conv_a_w', 'delta_conv_a_b', 'delta_lru_wr', 'delta_lru_br', 'delta_lru_wi', 'delta_lru_bi', 'delta_lru_lambda', 'delta_conv_b_w', 'delta_w_out', 'delta_final_g', 'new_m_meta', 'new_m_norm_g', 'new_m_w_in', 'new_m_conv_a_w', 'new_m_conv_a_b', 'new_m_lru_wr', 'new_m_lru_br', 'new_m_lru_wi', 'new_m_lru_bi', 'new_m_lru_lambda', 'new_m_conv_b_w', 'new_m_w_out', 'new_m_final_g', 'new_v_meta', 'new_v_norm_g', 'new_v_w_in', 'new_v_conv_a_w', 'new_v_conv_a_b', 'new_v_lru_wr', 'new_v_lru_br', 'new_v_lru_wi', 'new_v_lru_bi', 'new_v_lru_lambda', 'new_v_conv_b_w', 'new_v_w_out', 'new_v_final_g']
TWIN_LEAF_KINDS = {'loss': 'loss', 'grad_x': 'grad_x', 'grad_meta': 'grad_w', 'grad_norm_g': 'grad_w', 'grad_w_in': 'grad_w', 'grad_conv_a_w': 'grad_w', 'grad_conv_a_b': 'grad_w', 'grad_lru_wr': 'grad_w', 'grad_lru_br': 'grad_w', 'grad_lru_wi': 'grad_w', 'grad_lru_bi': 'grad_w', 'grad_lru_lambda': 'grad_w', 'grad_conv_b_w': 'grad_w', 'grad_w_out': 'grad_w', 'grad_final_g': 'grad_w', 'delta_meta': 'delta_w', 'delta_norm_g': 'delta_w', 'delta_w_in': 'delta_w', 'delta_conv_a_w': 'delta_w', 'delta_conv_a_b': 'delta_w', 'delta_lru_wr': 'delta_w', 'delta_lru_br': 'delta_w', 'delta_lru_wi': 'delta_w', 'delta_lru_bi': 'delta_w', 'delta_lru_lambda': 'delta_w', 'delta_conv_b_w': 'delta_w', 'delta_w_out': 'delta_w', 'delta_final_g': 'delta_w', 'new_m_meta': 'new_m', 'new_m_norm_g': 'new_m', 'new_m_w_in': 'new_m', 'new_m_conv_a_w': 'new_m', 'new_m_conv_a_b': 'new_m', 'new_m_lru_wr': 'new_m', 'new_m_lru_br': 'new_m', 'new_m_lru_wi': 'new_m', 'new_m_lru_bi': 'new_m', 'new_m_lru_lambda': 'new_m', 'new_m_conv_b_w': 'new_m', 'new_m_w_out': 'new_m', 'new_m_final_g': 'new_m', 'new_v_meta': 'new_v', 'new_v_norm_g': 'new_v', 'new_v_w_in': 'new_v', 'new_v_conv_a_w': 'new_v', 'new_v_conv_a_b': 'new_v', 'new_v_lru_wr': 'new_v', 'new_v_lru_br': 'new_v', 'new_v_lru_wi': 'new_v', 'new_v_lru_bi': 'new_v', 'new_v_lru_lambda': 'new_v', 'new_v_conv_b_w': 'new_v', 'new_v_w_out': 'new_v', 'new_v_final_g': 'new_v'}


def _forward(args):
    return _fwd_reference(*[args[k] for k in FWD_PARAMS])


def _output_shape():
    def fwd():
        inp = _fwd_setup_inputs(0)
        return _fwd_reference(*[inp[k] for k in FWD_PARAMS])
    out = _jax.eval_shape(fwd)
    return out.shape, out.dtype

N_MICROBATCH = 1
ADAM_LR = 0.001
ADAM_B1 = 0.9
ADAM_B2 = 0.999
ADAM_EPS = 1e-08
ADAM_WD = 0.01
ADAM_STEP = 10
PER_EXAMPLE_BATCH_AXIS = {'x': 0, 'loss_target': 0}
SHARED_INPUTS = []
_WEIGHT_DTYPES = {'meta': _jnp.float32, 'norm_g': _jnp.float32, 'w_in': _jnp.float32, 'conv_a_w': _jnp.float32, 'conv_a_b': _jnp.float32, 'lru_wr': _jnp.float32, 'lru_br': _jnp.float32, 'lru_wi': _jnp.float32, 'lru_bi': _jnp.float32, 'lru_lambda': _jnp.float32, 'conv_b_w': _jnp.float32, 'w_out': _jnp.float32, 'final_g': _jnp.float32}
MOMENT_SCALE = {'meta': 2.221024e-03, 'norm_g': 7.992477e-02, 'w_in': 4.520475e-02, 'conv_a_w': 3.938800e-02, 'conv_a_b': 3.931354e-01, 'lru_wr': 1.527207e-02, 'lru_br': 9.347455e-03, 'lru_wi': 2.823441e-02, 'lru_bi': 1.413601e-02, 'lru_lambda': 1.878242e-02, 'conv_b_w': 5.041531e-02, 'w_out': 4.456015e-02, 'final_g': 1.599501e+01}


def _to_microbatches(a, axis):
    t = _jnp.moveaxis(a, axis, 0)
    t = t.reshape((N_MICROBATCH, t.shape[0] // N_MICROBATCH) + t.shape[1:])
    return _jnp.moveaxis(t, 1, axis + 1)


def setup_inputs(seed: int = 0) -> dict:
    inp = _fwd_setup_inputs(seed)
    key = _jax.random.fold_in(_jax.random.key(seed), 7919)
    shape, _ = _output_shape()
    out = dict(inp)
    out["loss_target"] = _jax.random.normal(_jax.random.fold_in(key, 0), shape, _jnp.float32)
    for i, name in enumerate(TWIN_WEIGHTS):
        w = inp[name].astype(_jnp.float32)
        if MOMENT_SCALE is None:
            s = _jnp.sqrt(_jnp.mean(_jnp.square(w)) + 1e-30)
        else:
            s = MOMENT_SCALE[name]
        km, kv = _jax.random.split(_jax.random.fold_in(key, i + 1))
        out[name] = w
        out["m_" + name] = s * _jax.random.normal(km, w.shape, _jnp.float32)
        out["v_" + name] = (s * s) * _jax.random.uniform(kv, w.shape, _jnp.float32, 0.5, 1.5)
    if N_MICROBATCH > 1:
        for name, axis in PER_EXAMPLE_BATCH_AXIS.items():
            out[name] = _to_microbatches(out[name], axis)
    return {'x': out['x'], 'meta': out['meta'], 'norm_g': out['norm_g'], 'w_in': out['w_in'], 'conv_a_w': out['conv_a_w'], 'conv_a_b': out['conv_a_b'], 'lru_wr': out['lru_wr'], 'lru_br': out['lru_br'], 'lru_wi': out['lru_wi'], 'lru_bi': out['lru_bi'], 'lru_lambda': out['lru_lambda'], 'conv_b_w': out['conv_b_w'], 'w_out': out['w_out'], 'final_g': out['final_g'], 'loss_target': out['loss_target'], 'm_meta': out['m_meta'], 'm_norm_g': out['m_norm_g'], 'm_w_in': out['m_w_in'], 'm_conv_a_w': out['m_conv_a_w'], 'm_conv_a_b': out['m_conv_a_b'], 'm_lru_wr': out['m_lru_wr'], 'm_lru_br': out['m_lru_br'], 'm_lru_wi': out['m_lru_wi'], 'm_lru_bi': out['m_lru_bi'], 'm_lru_lambda': out['m_lru_lambda'], 'm_conv_b_w': out['m_conv_b_w'], 'm_w_out': out['m_w_out'], 'm_final_g': out['m_final_g'], 'v_meta': out['v_meta'], 'v_norm_g': out['v_norm_g'], 'v_w_in': out['v_w_in'], 'v_conv_a_w': out['v_conv_a_w'], 'v_conv_a_b': out['v_conv_a_b'], 'v_lru_wr': out['v_lru_wr'], 'v_lru_br': out['v_lru_br'], 'v_lru_wi': out['v_lru_wi'], 'v_lru_bi': out['v_lru_bi'], 'v_lru_lambda': out['v_lru_lambda'], 'v_conv_b_w': out['v_conv_b_w'], 'v_w_out': out['v_w_out'], 'v_final_g': out['v_final_g']}


def _loss(weights, diff, rest, loss_target):
    with _jax.named_scope("forward"):
        args = {**rest, TWIN_DIFF_INPUT: diff, **{k: w.astype(_WEIGHT_DTYPES[k]) for k, w in weights.items()}}
        y = _forward(args)
    with _jax.named_scope("loss_head"):
        err = _jnp.square(y.astype(_jnp.float32) - loss_target)
        return 0.5 * _jnp.sum(_jnp.mean(err, axis=-1)) if err.ndim else 0.5 * err


def _adamw(w, g, m, v):
    m = ADAM_B1 * m + (1.0 - ADAM_B1) * g
    v = ADAM_B2 * v + (1.0 - ADAM_B2) * _jnp.square(g)
    m_hat = m / (1.0 - ADAM_B1 ** ADAM_STEP)
    v_hat = v / (1.0 - ADAM_B2 ** ADAM_STEP)
    delta = -ADAM_LR * (m_hat / (_jnp.sqrt(v_hat) + ADAM_EPS) + ADAM_WD * w)
    return delta, m, v


def reference(x, meta, norm_g, w_in, conv_a_w, conv_a_b, lru_wr, lru_br, lru_wi, lru_bi, lru_lambda, conv_b_w, w_out, final_g, loss_target, m_meta, m_norm_g, m_w_in, m_conv_a_w, m_conv_a_b, m_lru_wr, m_lru_br, m_lru_wi, m_lru_bi, m_lru_lambda, m_conv_b_w, m_w_out, m_final_g, v_meta, v_norm_g, v_w_in, v_conv_a_w, v_conv_a_b, v_lru_wr, v_lru_br, v_lru_wi, v_lru_bi, v_lru_lambda, v_conv_b_w, v_w_out, v_final_g):
    given = dict(x=x, meta=meta, norm_g=norm_g, w_in=w_in, conv_a_w=conv_a_w, conv_a_b=conv_a_b, lru_wr=lru_wr, lru_br=lru_br, lru_wi=lru_wi, lru_bi=lru_bi, lru_lambda=lru_lambda, conv_b_w=conv_b_w, w_out=w_out, final_g=final_g, loss_target=loss_target, m_meta=m_meta, m_norm_g=m_norm_g, m_w_in=m_w_in, m_conv_a_w=m_conv_a_w, m_conv_a_b=m_conv_a_b, m_lru_wr=m_lru_wr, m_lru_br=m_lru_br, m_lru_wi=m_lru_wi, m_lru_bi=m_lru_bi, m_lru_lambda=m_lru_lambda, m_conv_b_w=m_conv_b_w, m_w_out=m_w_out, m_final_g=m_final_g, v_meta=v_meta, v_norm_g=v_norm_g, v_w_in=v_w_in, v_conv_a_w=v_conv_a_w, v_conv_a_b=v_conv_a_b, v_lru_wr=v_lru_wr, v_lru_br=v_lru_br, v_lru_wi=v_lru_wi, v_lru_bi=v_lru_bi, v_lru_lambda=v_lru_lambda, v_conv_b_w=v_conv_b_w, v_w_out=v_w_out, v_final_g=v_final_g)
    weights = {n: given[n] for n in TWIN_WEIGHTS}
    shared = {n: given[n] for n in SHARED_INPUTS}
    per_example = {n: given[n] for n in ['x']}
    grad_fn = _jax.value_and_grad(_loss, argnums=(0, 1))

    def one_microbatch(ex, loss_target):
        ex = dict(ex)
        diff = ex.pop(TWIN_DIFF_INPUT)
        return grad_fn(weights, diff, {**shared, **ex}, loss_target)

    if N_MICROBATCH == 1:
        loss, (grad_w, grad_x) = one_microbatch(per_example, given["loss_target"])
    else:
        def body(carry, xs):
            loss_sum, grad_sum = carry
            l_k, (gw_k, gx_k) = one_microbatch(xs[0], xs[1])
            with _jax.named_scope("update"):
                return (loss_sum + l_k, _jax.tree.map(_jnp.add, grad_sum, gw_k)), gx_k

        init = (_jnp.zeros((), _jnp.float32), _jax.tree.map(_jnp.zeros_like, weights))
        (loss, grad_w), grad_x = _jax.lax.scan(body, init, (per_example, given["loss_target"]))
    with _jax.named_scope("update"):
        delta_w, new_m, new_v = {}, {}, {}
        for n in TWIN_WEIGHTS:
            delta_w[n], new_m[n], new_v[n] = _adamw(weights[n], grad_w[n], given["m_" + n], given["v_" + n])
    return (loss, grad_x, *[grad_w[n] for n in TWIN_WEIGHTS], *[delta_w[n] for n in TWIN_WEIGHTS],
            *[new_m[n] for n in TWIN_WEIGHTS], *[new_v[n] for n in TWIN_WEIGHTS])
```

```python
import functools

import jax
import jax.numpy as jnp
from jax import lax
from jax.experimental import pallas as pl
from jax.experimental.pallas import tpu as pltpu

F32 = jnp.float32
BF16 = jnp.bfloat16
MESH = pl.DeviceIdType.MESH

N_DEV = 8
RMS_EPS = 1e-6
LRU_C = 8.0
ADAM_LR = 0.001
ADAM_B1 = 0.9
ADAM_B2 = 0.999
ADAM_EPS = 1e-08
ADAM_WD = 0.01
ADAM_STEP = 10

V7X_VMEM_LIMIT = 52 * 1024 * 1024
LANES = 128
SUBLANES = 8
TOKEN_TILE = 384
MIX_ROWS = 128
GATE_BLOCK = 256


def _params(sem):
    return pltpu.CompilerParams(dimension_semantics=sem, vmem_limit_bytes=V7X_VMEM_LIMIT)


def _tile(n, target, align=LANES):
    best = None
    for t in range(align, min(n, target) + 1, align):
        if n % t == 0:
            best = t
    return n if best is None else best


def _sigmoid(z):
    return 1.0 / (1.0 + jnp.exp(-z))


def _softplus(z):
    e = jnp.exp(-jnp.abs(z))
    u = 1.0 + e
    l1p = jnp.where(u == 1.0, e, jnp.log(u) * e / jnp.where(u == 1.0, 1.0, u - 1.0))
    return jnp.maximum(z, 0.0) + l1p


def _matmul(a, b, *, ta=False, tb=False, tm, tn, tk, out_dtype=F32, add=None, name):
    m, k = (a.shape[1], a.shape[0]) if ta else a.shape
    n = b.shape[0] if tb else b.shape[1]
    assert (b.shape[1] if tb else b.shape[0]) == k
    assert m % tm == 0 and n % tn == 0 and k % tk == 0, (m, n, k, tm, tn, tk)
    nk = k // tk
    a_spec = pl.BlockSpec((tk, tm), lambda i, j, q: (q, i)) if ta else pl.BlockSpec((tm, tk), lambda i, j, q: (i, q))
    b_spec = pl.BlockSpec((tn, tk), lambda i, j, q: (j, q)) if tb else pl.BlockSpec((tk, tn), lambda i, j, q: (q, j))
    o_spec = pl.BlockSpec((tm, tn), lambda i, j, q: (i, j))
    dims = (((0 if ta else 1,), (1 if tb else 0,)), ((), ()))
    has_add = add is not None

    def body(*refs):
        if has_add:
            a_ref, b_ref, add_ref, o_ref, acc_ref = refs
        else:
            a_ref, b_ref, o_ref, acc_ref = refs
        q = pl.program_id(2)
        part = lax.dot_general(a_ref[...], b_ref[...], dims, preferred_element_type=F32)

        def finish(acc):
            if has_add:
                acc = acc + add_ref[...]
            o_ref[...] = acc.astype(out_dtype)

        if nk == 1:
            finish(part)
        else:
            @pl.when(q == 0)
            def _():
                acc_ref[...] = part

            @pl.when(jnp.logical_and(q > 0, q < nk - 1))
            def _():
                acc_ref[...] += part

            @pl.when(q == nk - 1)
            def _():
                finish(acc_ref[...] + part)

    in_specs = [a_spec, b_spec] + ([o_spec] if has_add else [])
    args = (a, b) + ((add,) if has_add else ())
    acc_shape = (tm, tn) if nk > 1 else (SUBLANES, LANES)
    return pl.pallas_call(
        body, name=name,
        grid=(m // tm, n // tn, nk),
        in_specs=in_specs, out_specs=o_spec,
        out_shape=jax.ShapeDtypeStruct((m, n), out_dtype),
        scratch_shapes=[pltpu.VMEM(acc_shape, F32)],
        compiler_params=_params(("parallel", "parallel", "arbitrary")),
    )(*args)


def _rms_fwd(h, g, *, name):
    tp, d = h.shape
    tr = _tile(tp, 512, SUBLANES)

    def body(h_ref, g_ref, o_ref):
        hv = h_ref[...]
        rstd = lax.rsqrt(jnp.mean(hv * hv, axis=-1, keepdims=True) + RMS_EPS)
        o_ref[...] = (hv * rstd * g_ref[...]).astype(BF16)

    return pl.pallas_call(
        body, name=name, grid=(tp // tr,),
        in_specs=[pl.BlockSpec((tr, d), lambda i: (i, 0)), pl.BlockSpec((1, d), lambda i: (0, 0))],
        out_specs=pl.BlockSpec((tr, d), lambda i: (i, 0)),
        out_shape=jax.ShapeDtypeStruct((tp, d), BF16),
        compiler_params=_params(("parallel",)),
    )(h, g.reshape(1, d))


def _rms_bwd(h, dhn, dout, g, *, name):
    tp, d = h.shape
    tr = _tile(tp, 384, SUBLANES)

    def body(h_ref, dhn_ref, dout_ref, g_ref, dh_ref, dhb_ref, dg_ref):
        hv = h_ref[...]
        rstd = lax.rsqrt(jnp.mean(hv * hv, axis=-1, keepdims=True) + RMS_EPS)
        xhat = hv * rstd
        dn = dhn_ref[...]
        dxhat = dn * g_ref[...]
        dh = dout_ref[...] + rstd * (dxhat - xhat * jnp.mean(dxhat * xhat, axis=-1, keepdims=True))
        dh_ref[...] = dh
        dhb_ref[...] = dh.astype(BF16)
        part = jnp.sum(dn * xhat, axis=0, keepdims=True)

        @pl.when(pl.program_id(0) == 0)
        def _():
            dg_ref[...] = part

        @pl.when(pl.program_id(0) > 0)
        def _():
            dg_ref[...] += part

    row = pl.BlockSpec((tr, d), lambda i: (i, 0))
    vec = pl.BlockSpec((1, d), lambda i: (0, 0))
    return pl.pallas_call(
        body, name=name, grid=(tp // tr,),
        in_specs=[row, row, row, vec],
        out_specs=[row, row, vec],
        out_shape=[jax.ShapeDtypeStruct((tp, d), F32), jax.ShapeDtypeStruct((tp, d), BF16),
                   jax.ShapeDtypeStruct((1, d), F32)],
        compiler_params=_params(("arbitrary",)),
    )(h, dhn, dout, g.reshape(1, d))


def _loss_head(h, tgt, g, *, n_meta, n_tok, name):
    tp, d = h.shape
    tr = _tile(tp, 384, SUBLANES)

    def body(h_ref, t_ref, g_ref, dh_ref, dhb_ref, dg_ref, loss_ref):
        i = pl.program_id(0)
        hv = h_ref[...]
        rstd = lax.rsqrt(jnp.mean(hv * hv, axis=-1, keepdims=True) + RMS_EPS)
        xhat = hv * rstd
        gv = g_ref[...]
        rows = i * tr + lax.broadcasted_iota(jnp.int32, (tr, 1), 0)
        valid = jnp.logical_and(rows >= n_meta, rows < n_tok)
        err = jnp.where(valid, xhat * gv - t_ref[...], 0.0)
        dy = err * (1.0 / d)
        dxhat = dy * gv
        dh = rstd * (dxhat - xhat * jnp.mean(dxhat * xhat, axis=-1, keepdims=True))
        dh_ref[...] = dh
        dhb_ref[...] = dh.astype(BF16)
        dg_part = jnp.sum(dy * xhat, axis=0, keepdims=True)
        per_row = jnp.sum(err * err, axis=-1, keepdims=True) * (1.0 / d)
        loss_part = jnp.broadcast_to(0.5 * jnp.sum(per_row, axis=0, keepdims=True), (SUBLANES, LANES))

        @pl.when(i == 0)
        def _():
            dg_ref[...] = dg_part
            loss_ref[...] = loss_part

        @pl.when(i > 0)
        def _():
            dg_ref[...] += dg_part
            loss_ref[...] += loss_part

    row = pl.BlockSpec((tr, d), lambda i: (i, 0))
    vec = pl.BlockSpec((1, d), lambda i: (0, 0))
    return pl.pallas_call(
        body, name=name, grid=(tp // tr,),
        in_specs=[row, row, vec],
        out_specs=[row, row, vec, pl.BlockSpec((SUBLANES, LANES), lambda i: (0, 0))],
        out_shape=[jax.ShapeDtypeStruct((tp, d), F32), jax.ShapeDtypeStruct((tp, d), BF16),
                   jax.ShapeDtypeStruct((1, d), F32), jax.ShapeDtypeStruct((SUBLANES, LANES), F32)],
        compiler_params=_params(("arbitrary",)),
    )(h, tgt, g.reshape(1, d))


def _shift_down(halo, tile, s):
    if s == 0:
        return tile
    ext = jnp.concatenate([halo, tile], axis=0)
    return pltpu.roll(ext, s, 0)[SUBLANES:]


def _shift_up(tile, head, s):
    if s == 0:
        return tile
    ext = jnp.concatenate([tile, head], axis=0)
    n = ext.shape[0]
    return pltpu.roll(ext, n - s, 0)[: tile.shape[0]]


def _scan_rows_fwd(a, b):
    row = lax.broadcasted_iota(jnp.int32, a.shape, 0)
    for s in (1, 2, 4):
        a_sh = pltpu.roll(a, s, 0)
        b_sh = pltpu.roll(b, s, 0)
        m = row >= s
        b = jnp.where(m, a * b_sh + b, b)
        a = jnp.where(m, a * a_sh, a)
    return a, b


def _scan_rows_bwd(c, d):
    row = lax.broadcasted_iota(jnp.int32, c.shape, 0)
    for s in (1, 2, 4):
        c_sh = pltpu.roll(c, SUBLANES - s, 0)
        d_sh = pltpu.roll(d, SUBLANES - s, 0)
        m = row < SUBLANES - s
        d = jnp.where(m, c * d_sh + d, d)
        c = jnp.where(m, c * c_sh, c)
    return c, d


def _gates(ca, wr, wi, br, bi, sp):
    cab = ca.astype(BF16)
    r = _sigmoid(jnp.dot(cab, wr, preferred_element_type=F32) + br)
    ig = _sigmoid(jnp.dot(cab, wi, preferred_element_type=F32) + bi)
    la = -LRU_C * r * sp
    a = jnp.exp(la)
    mult = jnp.sqrt(-jnp.tanh(la) * (a * a + 1.0))
    return r, ig, a, mult


def _mixer_fwd(u, wa, ba, wr_blk, br, wi_blk, bi, lam, wb, *, name):
    tp, din = u.shape
    dl = din // 6
    tt = MIX_ROWS
    cw = GATE_BLOCK
    nch = dl // cw
    assert tp % tt == 0 and dl % cw == 0

    def body(u_ref, wa_ref, ba_ref, wr_ref, br_ref, wi_ref, bi_ref, lam_ref, wb_ref,
             ca_ref, hs_ref, y_ref, xa_tail, v_tail, h_carry, a_s, b_s):
        @pl.when(pl.program_id(0) == 0)
        def _():
            xa_tail[...] = jnp.zeros_like(xa_tail)
            v_tail[...] = jnp.zeros_like(v_tail)
            h_carry[...] = jnp.zeros_like(h_carry)

        for ch in range(nch):
            cs = slice(ch * cw, (ch + 1) * cw)

            def seg(s):
                return slice(s * dl + ch * cw, s * dl + (ch + 1) * cw)

            xa = u_ref[:, seg(0)]
            halo = xa_tail[:, cs]
            ca = ba_ref[:, cs] + wa_ref[3:4, cs] * xa
            for kk in range(3):
                ca = ca + wa_ref[kk:kk + 1, cs] * _shift_down(halo, xa, 3 - kk)
            xa_tail[:, cs] = xa[tt - SUBLANES:]
            ca_ref[:, cs] = ca
            sp = _softplus(-lam_ref[:, cs])
            _, ig, a, mult = _gates(ca, wr_ref[ch], wi_ref[ch], br_ref[:, cs], bi_ref[:, cs], sp)
            a_s[:, cs] = a
            b_s[:, cs] = mult * (ig * ca)

            bv = u_ref[:, seg(2)]
            v = u_ref[:, seg(3)] * u_ref[:, seg(4)]
            gb = u_ref[:, seg(5)]
            vh = v_tail[:, cs]
            cb = wb_ref[2:3, cs] * v
            for kk in range(2):
                cb = cb + wb_ref[kk:kk + 1, cs] * _shift_down(vh, v, 2 - kk)
            v_tail[:, cs] = v[tt - SUBLANES:]
            y_ref[:, dl + ch * cw: dl + (ch + 1) * cw] = (bv * cb * (gb * _sigmoid(gb))).astype(BF16)

        def group(gi, hprev):
            rows = pl.ds(pl.multiple_of(gi * SUBLANES, SUBLANES), SUBLANES)
            a8, b8 = _scan_rows_fwd(a_s[rows, :], b_s[rows, :])
            h8 = b8 + a8 * hprev
            hs_ref[rows, :] = h8
            return jnp.broadcast_to(h8[SUBLANES - 1:SUBLANES, :], h8.shape)

        h_carry[...] = lax.fori_loop(0, tt // SUBLANES, group, h_carry[...])

        for ch in range(nch):
            cs = slice(ch * cw, (ch + 1) * cw)
            ga = u_ref[:, dl + ch * cw: dl + (ch + 1) * cw]
            y_ref[:, cs] = (hs_ref[:, cs] * (ga * _sigmoid(ga))).astype(BF16)

    row = lambda w: pl.BlockSpec((tt, w), lambda i: (i, 0))
    full = lambda shp: pl.BlockSpec(shp, lambda i: tuple(0 for _ in shp))
    return pl.pallas_call(
        body, name=name, grid=(tp // tt,),
        in_specs=[row(din), full((4, dl)), full((1, dl)), full((nch, cw, cw)), full((1, dl)),
                  full((nch, cw, cw)), full((1, dl)), full((1, dl)), full((3, dl))],
        out_specs=[row(dl), row(dl), row(2 * dl)],
        out_shape=[jax.ShapeDtypeStruct((tp, dl), F32), jax.ShapeDtypeStruct((tp, dl), F32),
                   jax.ShapeDtypeStruct((tp, 2 * dl), BF16)],
        scratch_shapes=[pltpu.VMEM((SUBLANES, dl), F32), pltpu.VMEM((SUBLANES, dl), F32),
                        pltpu.VMEM((SUBLANES, dl), F32), pltpu.VMEM((tt, dl), F32), pltpu.VMEM((tt, dl), F32)],
        compiler_params=_params(("arbitrary",)),
    )(u, wa, ba, wr_blk, br, wi_blk, bi, lam, wb)


SG_WA, SG_BA, SG_BR, SG_BI, SG_LAM, SG_WB, SG_ROWS = 0, 4, 5, 6, 7, 8, 16


def _mixer_bwd(u, ca, hs, dy, wa, wr_blk, br, wi_blk, bi, lam, wb, *, name):
    tp, din = u.shape
    dl = din // 6
    tt = MIX_ROWS
    cw = GATE_BLOCK
    nch = dl // cw
    nt = tp // tt
    hb = tt // SUBLANES
    tn_dims = (((0,), (0,)), ((), ()))
    nt_dims = (((1,), (1,)), ((), ()))

    def body(u_ref, uh_ref, ca_ref, hs_ref, hsh_ref, dy_ref, wa_ref, wr_ref, br_ref, wi_ref, bi_ref, lam_ref, wb_ref,
             du_ref, sg_ref, dwr_ref, dwi_ref,
             g_carry, a_head, dca_head, dcb_head, r_s, i_s, a_s, an_s, d_s, g_s):
        i = pl.program_id(0)
        first_tile = i == nt - 1

        @pl.when(i == 0)
        def _():
            for ref in (g_carry, a_head, dca_head, dcb_head, sg_ref, dwr_ref, dwi_ref):
                ref[...] = jnp.zeros_like(ref)

        def halo_of(x):
            return jnp.where(first_tile, 0.0, x)

        for ch in range(nch):
            cs = slice(ch * cw, (ch + 1) * cw)
            cav = ca_ref[:, cs]
            sp = _softplus(-lam_ref[:, cs])
            r, ig, a, _ = _gates(cav, wr_ref[ch], wi_ref[ch], br_ref[:, cs], bi_ref[:, cs], sp)
            r_s[:, cs] = r
            i_s[:, cs] = ig
            a_s[:, cs] = a
            an_s[:, cs] = _shift_up(a, a_head[:, cs], 1)
            a_head[:, cs] = a[:SUBLANES]
            ga = u_ref[:, dl + ch * cw: dl + (ch + 1) * cw]
            d_s[:, cs] = dy_ref[:, cs] * (ga * _sigmoid(ga))

        def group(k, gnext):
            gi = tt // SUBLANES - 1 - k
            rows = pl.ds(pl.multiple_of(gi * SUBLANES, SUBLANES), SUBLANES)
            c8, d8 = _scan_rows_bwd(an_s[rows, :], d_s[rows, :])
            g8 = d8 + c8 * gnext
            g_s[rows, :] = g8
            return jnp.broadcast_to(g8[0:1, :], g8.shape)

        g_carry[...] = lax.fori_loop(0, tt // SUBLANES, group, g_carry[...])

        def acc_row(r0, val):
            sg_ref[r0:r0 + 1, cs_cur[0]] += jnp.sum(val, axis=0, keepdims=True)

        cs_cur = [None]
        for ch in range(nch):
            cs = slice(ch * cw, (ch + 1) * cw)
            cs_cur[0] = cs

            def seg(s):
                return slice(s * dl + ch * cw, s * dl + (ch + 1) * cw)

            cav = ca_ref[:, cs]
            r = r_s[:, cs]
            ig = i_s[:, cs]
            a = a_s[:, cs]
            g = g_s[:, cs]
            hsv = hs_ref[:, cs]
            lamv = lam_ref[:, cs]
            sp = _softplus(-lamv)
            la = -LRU_C * r * sp
            e2 = a * a
            mult = jnp.sqrt(-jnp.tanh(la) * (e2 + 1.0))
            hprev = _shift_down(halo_of(hsh_ref[:, cs]), hsv, 1)
            dla = g * hprev * a - g * (ig * cav) * e2 / mult
            gm = g * mult
            dzi = gm * cav * ig * (1.0 - ig)
            dca = gm * ig
            dzr = dla * (-LRU_C * sp) * r * (1.0 - r)
            acc_row(SG_LAM, dla * (-LRU_C * r) * (-_sigmoid(-lamv)))
            acc_row(SG_BR, dzr)
            acc_row(SG_BI, dzi)
            dzr_b = dzr.astype(BF16)
            dzi_b = dzi.astype(BF16)
            cab = cav.astype(BF16)
            dca = dca + lax.dot_general(dzr_b, wr_ref[ch], nt_dims, preferred_element_type=F32)
            dca = dca + lax.dot_general(dzi_b, wi_ref[ch], nt_dims, preferred_element_type=F32)
            dwr_ref[ch] += lax.dot_general(cab, dzr_b, tn_dims, preferred_element_type=F32)
            dwi_ref[ch] += lax.dot_general(cab, dzi_b, tn_dims, preferred_element_type=F32)
            acc_row(SG_BA, dca)
            xa = u_ref[:, seg(0)]
            xah = halo_of(uh_ref[:, seg(0)])
            head = dca_head[:, cs]
            dxa = wa_ref[3:4, cs] * dca
            acc_row(SG_WA + 3, dca * xa)
            for kk in range(3):
                acc_row(SG_WA + kk, dca * _shift_down(xah, xa, 3 - kk))
                dxa = dxa + wa_ref[kk:kk + 1, cs] * _shift_up(dca, head, 3 - kk)
            dca_head[:, cs] = dca[:SUBLANES]
            ga = u_ref[:, seg(1)]
            sga = _sigmoid(ga)
            dga = dy_ref[:, cs] * hsv * (sga * (1.0 + ga * (1.0 - sga)))
            du_ref[:, seg(0)] = dxa.astype(BF16)
            du_ref[:, seg(1)] = dga.astype(BF16)

            bv = u_ref[:, seg(2)]
            cv = u_ref[:, seg(3)]
            xb = u_ref[:, seg(4)]
            gb = u_ref[:, seg(5)]
            dyb = dy_ref[:, dl + ch * cw: dl + (ch + 1) * cw]
            v = cv * xb
            vh = halo_of(uh_ref[:, seg(3)] * uh_ref[:, seg(4)])
            v1 = _shift_down(vh, v, 1)
            v2 = _shift_down(vh, v, 2)
            cb = wb_ref[2:3, cs] * v + wb_ref[1:2, cs] * v1 + wb_ref[0:1, cs] * v2
            sgb = _sigmoid(gb)
            sl = gb * sgb
            dcb = dyb * bv * sl
            du_ref[:, seg(2)] = (dyb * cb * sl).astype(BF16)
            du_ref[:, seg(5)] = (dyb * bv * cb * (sgb * (1.0 + gb * (1.0 - sgb)))).astype(BF16)
            acc_row(SG_WB + 2, dcb * v)
            acc_row(SG_WB + 1, dcb * v1)
            acc_row(SG_WB + 0, dcb * v2)
            bhead = dcb_head[:, cs]
            dv = wb_ref[2:3, cs] * dcb + wb_ref[1:2, cs] * _shift_up(dcb, bhead, 1) \
                + wb_ref[0:1, cs] * _shift_up(dcb, bhead, 2)
            dcb_head[:, cs] = dcb[:SUBLANES]
            du_ref[:, seg(3)] = (dv * xb).astype(BF16)
            du_ref[:, seg(4)] = (dv * cv).astype(BF16)

    rev = lambda w: pl.BlockSpec((tt, w), lambda i: (nt - 1 - i, 0))
    halo = lambda w: pl.BlockSpec((SUBLANES, w), lambda i: (jnp.maximum((nt - 1 - i) * hb - 1, 0), 0))
    full = lambda shp: pl.BlockSpec(shp, lambda i: tuple(0 for _ in shp))
    vm = lambda r: pltpu.VMEM((r, dl), F32)
    return pl.pallas_call(
        body, name=name, grid=(nt,),
        in_specs=[rev(din), halo(din), rev(dl), rev(dl), halo(dl), rev(2 * dl), full((4, dl)),
                  full((nch, cw, cw)), full((1, dl)), full((nch, cw, cw)), full((1, dl)), full((1, dl)), full((3, dl))],
        out_specs=[rev(din), full((SG_ROWS, dl)), full((nch, cw, cw)), full((nch, cw, cw))],
        out_shape=[jax.ShapeDtypeStruct((tp, din), BF16), jax.ShapeDtypeStruct((SG_ROWS, dl), F32),
                   jax.ShapeDtypeStruct((nch, cw, cw), F32), jax.ShapeDtypeStruct((nch, cw, cw), F32)],
        scratch_shapes=[vm(SUBLANES), vm(SUBLANES), vm(SUBLANES), vm(SUBLANES),
                        vm(tt), vm(tt), vm(tt), vm(tt), vm(tt), vm(tt)],
        compiler_params=_params(("arbitrary",)),
    )(u, u, ca, hs, hs, dy, wa, wr_blk, br, wi_blk, bi, lam, wb)


def _adamw(w, g, m, v, *, slots, name):
    r, c = w.shape
    tr = _tile(r, 256, SUBLANES)
    bc1 = 1.0 - ADAM_B1 ** ADAM_STEP
    bc2 = 1.0 - ADAM_B2 ** ADAM_STEP

    def body(w_ref, g_ref, m_ref, v_ref, grad_ref, delta_ref, nm_ref, nv_ref):
        if slots:
            gv = g_ref[0]
            for s in range(1, N_DEV):
                gv = gv + g_ref[s]
        else:
            gv = g_ref[...]
        wv = w_ref[...]
        mn = ADAM_B1 * m_ref[...] + (1.0 - ADAM_B1) * gv
        vn = ADAM_B2 * v_ref[...] + (1.0 - ADAM_B2) * (gv * gv)
        m_hat = mn / bc1
        v_hat = vn / bc2
        grad_ref[...] = gv
        delta_ref[...] = -ADAM_LR * (m_hat / (jnp.sqrt(v_hat) + ADAM_EPS) + ADAM_WD * wv)
        nm_ref[...] = mn
        nv_ref[...] = vn

    blk = pl.BlockSpec((tr, c), lambda i: (i, 0))
    g_spec = pl.BlockSpec((N_DEV, tr, c), lambda i: (0, i, 0)) if slots else blk
    shp = jax.ShapeDtypeStruct((r, c), F32)
    return pl.pallas_call(
        body, name=name, grid=(r // tr,),
        in_specs=[blk, g_spec, blk, blk], out_specs=[blk, blk, blk, blk],
        out_shape=[shp, shp, shp, shp],
        compiler_params=_params(("parallel",)),
    )(w, g, m, v)


def _slot_sum(g, *, name):
    _, r, c = g.shape
    tr = _tile(r, 512, SUBLANES)

    def body(g_ref, o_ref):
        gv = g_ref[0]
        for s in range(1, N_DEV):
            gv = gv + g_ref[s]
        o_ref[...] = gv

    return pl.pallas_call(
        body, name=name, grid=(r // tr,),
        in_specs=[pl.BlockSpec((N_DEV, tr, c), lambda i: (0, i, 0))],
        out_specs=pl.BlockSpec((tr, c), lambda i: (i, 0)),
        out_shape=jax.ShapeDtypeStruct((r, c), F32),
        compiler_params=_params(("parallel",)),
    )(g)


def _mesh_pos():
    x, y, c = lax.axis_index("x"), lax.axis_index("y"), lax.axis_index("c")
    return x, y, c, 4 * x + 2 * y + c


ANY = pl.BlockSpec(memory_space=pl.ANY)


def _all_gather(srcs, out_shapes, views, *, name):
    n = len(srcs)

    def body(*refs):
        src = refs[:n]
        dst = refs[n:2 * n]
        send_sems, recv_sems, local_sems = refs[2 * n:]
        x, y, c, me = _mesh_pos()
        sibling = (x, y, 1 - c)
        chips = [(1 - x, y), (x, 1 - y), (1 - x, 1 - y)]

        def dev(px, py, pc):
            return 4 * px + 2 * py + pc

        def copy(a, k, block, to, from_src=False):
            win = views[a](dst[a], dev(*block))
            return pltpu.make_async_remote_copy(
                src_ref=src[a] if from_src else win, dst_ref=win,
                send_sem=send_sems.at[a * 7 + k], recv_sem=recv_sems.at[a * 7 + k],
                device_id=to, device_id_type=MESH)

        mine = [pltpu.make_async_copy(src[a], views[a](dst[a], me), local_sems.at[a]) for a in range(n)]
        started = []
        for a in range(n):
            mine[a].start()
            first = [copy(a, 0, (x, y, c), sibling, True)]
            first += [copy(a, 1 + j, (x, y, c), (*chip, c), True) for j, chip in enumerate(chips)]
            for cp in first:
                cp.start()
            started += first
        for a in range(n):
            for j, chip in enumerate(chips):
                copy(a, 1 + j, (*chip, c), (x, y, c)).wait_recv()
                fwd = copy(a, 4 + j, (*chip, c), sibling)
                fwd.start()
                started.append(fwd)
        for a in range(n):
            copy(a, 0, (x, y, 1 - c), (x, y, c)).wait_recv()
            for j, chip in enumerate(chips):
                copy(a, 4 + j, (*chip, 1 - c), (x, y, c)).wait_recv()
        for cp in started:
            cp.wait_send()
        for a in range(n):
            mine[a].wait()

    return pl.pallas_call(
        body, name=name,
        in_specs=[ANY] * n, out_specs=[ANY] * n,
        out_shape=[jax.ShapeDtypeStruct(s, x.dtype) for s, x in zip(out_shapes, srcs)],
        scratch_shapes=[pltpu.SemaphoreType.DMA((7 * n,)), pltpu.SemaphoreType.DMA((7 * n,)),
                        pltpu.SemaphoreType.DMA((n,))],
    )(*srcs)


def _all_to_all(srcs, land_shapes, views, *, name):
    n = len(srcs)

    def body(*refs):
        src = refs[:n]
        land = refs[n:2 * n]
        send_sems, recv_sems, local_sems = refs[2 * n:]
        x, y, c, me = _mesh_pos()
        mine = [pltpu.make_async_copy(views[a](src[a], me), land[a].at[me], local_sems.at[a]) for a in range(n)]
        for cp in mine:
            cp.start()
        sends = []
        for a in range(n):
            for k in range(1, N_DEV):
                px = 1 - x if k & 4 else x
                py = 1 - y if k & 2 else y
                pc = 1 - c if k & 1 else c
                peer = 4 * px + 2 * py + pc
                cp = pltpu.make_async_remote_copy(
                    src_ref=views[a](src[a], peer), dst_ref=land[a].at[me],
                    send_sem=send_sems.at[a * 7 + k - 1], recv_sem=recv_sems.at[a * 7 + k - 1],
                    device_id=(px, py, pc), device_id_type=MESH)
                cp.start()
                arrive = pltpu.make_async_remote_copy(
                    src_ref=views[a](src[a], peer), dst_ref=land[a].at[peer],
                    send_sem=send_sems.at[a * 7 + k - 1], recv_sem=recv_sems.at[a * 7 + k - 1],
                    device_id=(px, py, pc), device_id_type=MESH)
                sends.append((cp, arrive))
        for cp, arrive in sends:
            arrive.wait_recv()
        for cp, arrive in sends:
            cp.wait_send()
        for cp in mine:
            cp.wait()

    return pl.pallas_call(
        body, name=name,
        in_specs=[ANY] * n, out_specs=[ANY] * n,
        out_shape=[jax.ShapeDtypeStruct(s, x.dtype) for s, x in zip(land_shapes, srcs)],
        scratch_shapes=[pltpu.SemaphoreType.DMA((7 * n,)), pltpu.SemaphoreType.DMA((7 * n,)),
                        pltpu.SemaphoreType.DMA((n,))],
    )(*srcs)


def _lead(ref, d):
    return ref.at[d]


def _col_window(width):
    def view(ref, d):
        return ref.at[:, pl.ds(d * width, width)]
    return view


def _pack(arrs):
    flat = jnp.concatenate([a.reshape(-1).astype(F32) for a in arrs])
    n = flat.shape[0]
    rows = -(-n // (SUBLANES * LANES)) * SUBLANES
    return jnp.pad(flat, (0, rows * LANES - n)).reshape(rows, LANES)


def _unpack(buf, shapes):
    flat = buf.reshape(-1)
    out, off = [], 0
    for s in shapes:
        n = 1
        for q in s:
            n *= q
        out.append(flat[off:off + n].reshape(s))
        off += n
    return out


def _blockdiag(w, cw):
    h, hd, _ = w.shape
    per = cw // hd
    wg = w.reshape(h // per, per, hd, hd)
    eye = jnp.eye(per, dtype=w.dtype)
    blk = jnp.einsum("gpij,pq->gpiqj", wg, eye)
    return blk.reshape(h // per, cw, cw).astype(BF16)


def _blockdiag_extract(g, hd):
    n, cw, _ = g.shape
    per = cw // hd
    g5 = g.reshape(n, per, hd, per, hd)
    idx = jnp.arange(per)
    return g5[:, idx, :, idx, :].transpose(1, 0, 2, 3).reshape(n * per, hd, hd)


def kernel(x, meta, norm_g, w_in, conv_a_w, conv_a_b, lru_wr, lru_br, lru_wi, lru_bi, lru_lambda, conv_b_w, w_out, final_g, loss_target, m_meta, m_norm_g, m_w_in, m_conv_a_w, m_conv_a_b, m_lru_wr, m_lru_br, m_lru_wi, m_lru_bi, m_lru_lambda, m_conv_b_w, m_w_out, m_final_g, v_meta, v_norm_g, v_w_in, v_conv_a_w, v_conv_a_b, v_lru_wr, v_lru_br, v_lru_wi, v_lru_bi, v_lru_lambda, v_conv_b_w, v_w_out, v_final_g):
    _, seq, d = x.shape
    n_meta = meta.shape[0]
    depth = w_in.shape[0]
    din = w_in.shape[2] * N_DEV
    dl = din // 6
    dmix = 2 * dl
    wcol = w_in.shape[2]
    wrow = w_out.shape[1]
    mcol = meta.shape[1]
    ccol = conv_a_w.shape[2]
    heads, hd = lru_wr.shape[1], lru_wr.shape[2]
    n_tok = n_meta + seq
    tp = -(-n_tok // TOKEN_TILE) * TOKEN_TILE
    me = 4 * lax.axis_index("x") + 2 * lax.axis_index("y") + lax.axis_index("c")

    small_mine = _pack([meta, conv_a_w, conv_b_w])
    srcs = [w_in[l].astype(BF16) for l in range(depth)] + [w_out[l].astype(BF16) for l in range(depth)] + [small_mine]
    shapes = [(d, din)] * depth + [(N_DEV, wrow, d)] * depth + [(N_DEV,) + small_mine.shape]
    views = [_col_window(wcol)] * depth + [_lead] * depth + [_lead]
    gathered = _all_gather(srcs, shapes, views, name="gather_weights")
    w_in_full = gathered[:depth]
    w_out_full = [g.reshape(dmix, d) for g in gathered[depth:2 * depth]]
    parts = [_unpack(gathered[2 * depth][s], [meta.shape, conv_a_w.shape, conv_b_w.shape]) for s in range(N_DEV)]
    meta_full = jnp.concatenate([p[0] for p in parts], axis=1)
    wa_full = jnp.concatenate([p[1] for p in parts], axis=2)
    wb_full = jnp.concatenate([p[2] for p in parts], axis=2)

    wr_blk = [_blockdiag(lru_wr[l], GATE_BLOCK) for l in range(depth)]
    wi_blk = [_blockdiag(lru_wi[l], GATE_BLOCK) for l in range(depth)]
    vec = lambda a: a.reshape(1, dl)

    h = jnp.concatenate([meta_full, x[0], jnp.zeros((tp - n_tok, d), F32)], axis=0)
    tgt = jnp.pad(loss_target[0], ((n_meta, tp - n_tok), (0, 0)))
    tm = _tile(tp, 1408)
    saved = []
    for l in range(depth):
        hn = _rms_fwd(h, norm_g[l], name=f"rms_fwd_{l}")
        u = _matmul(hn, w_in_full[l], tm=tm, tn=_tile(din, 768), tk=d, name=f"mm_u_{l}")
        ca, hs, y = _mixer_fwd(u, wa_full[l], vec(conv_a_b[l]), wr_blk[l], vec(lru_br[l]), wi_blk[l], vec(lru_bi[l]),
                               vec(lru_lambda[l]), wb_full[l], name=f"mixer_fwd_{l}")
        h_next = _matmul(y, w_out_full[l], tm=tm, tn=_tile(d, 512), tk=dmix, add=h, name=f"mm_out_{l}")
        saved.append((h, hn, u, ca, hs, y))
        h = h_next

    dh, dhb, dg_final, loss_part = _loss_head(h, tgt, final_g, n_meta=n_meta, n_tok=n_tok, name="loss_head")
    loss = lax.psum(loss_part[0, 0], ("x", "y", "c"))

    small_grads = [None] * depth
    big_landed = [None] * depth
    for l in reversed(range(depth)):
        h_in, hn, u, ca, hs, y = saved[l]
        dy = _matmul(dhb, w_out_full[l], tb=True, tm=tm, tn=_tile(dmix, 512), tk=d, name=f"mm_dy_{l}")
        dw_out = _matmul(y, dhb, ta=True, tm=_tile(dmix, 1024), tn=_tile(d, 1024), tk=tm, name=f"mm_dwout_{l}")
        du, sg, dwr, dwi = _mixer_bwd(u, ca, hs, dy, wa_full[l], wr_blk[l], vec(lru_br[l]), wi_blk[l], vec(lru_bi[l]),
                                      vec(lru_lambda[l]), wb_full[l], name=f"mixer_bwd_{l}")
        dhn = _matmul(du, w_in_full[l], tb=True, tm=tm, tn=_tile(d, 1024), tk=_tile(din, 1536), name=f"mm_dhn_{l}")
        dw_in = _matmul(hn, du, ta=True, tm=_tile(d, 1024), tn=_tile(din, 1536), tk=tm, name=f"mm_dwin_{l}")
        dh, dhb, dg_norm = _rms_bwd(h_in, dhn, dh, norm_g[l], name=f"rms_bwd_{l}")
        big_landed[l] = _all_to_all(
            [dw_in, dw_out.reshape(N_DEV, wrow, d)],
            [(N_DEV, d, wcol), (N_DEV, wrow, d)],
            [_col_window(wcol), _lead], name=f"scatter_grads_{l}")
        small_grads[l] = (dg_norm, sg, dwr, dwi)

    rep_shapes = [norm_g.shape, conv_a_b.shape, lru_wr.shape, lru_br.shape, lru_wi.shape, lru_bi.shape,
                  lru_lambda.shape, final_g.shape]
    shard_full_shapes = [(n_meta, d), (depth, 4, dl), (depth, 3, dl)]
    stack = lambda f: jnp.stack([f(l) for l in range(depth)])
    partial = _pack([
        stack(lambda l: small_grads[l][0][0]),
        stack(lambda l: small_grads[l][1][SG_BA]),
        stack(lambda l: _blockdiag_extract(small_grads[l][2], hd)),
        stack(lambda l: small_grads[l][1][SG_BR]),
        stack(lambda l: _blockdiag_extract(small_grads[l][3], hd)),
        stack(lambda l: small_grads[l][1][SG_BI]),
        stack(lambda l: small_grads[l][1][SG_LAM]),
        dg_final[0],
        dh[:n_meta],
        stack(lambda l: small_grads[l][1][SG_WA:SG_WA + 4]),
        stack(lambda l: small_grads[l][1][SG_WB:SG_WB + 3]),
    ])
    all_partials = _all_gather([partial], [(N_DEV,) + partial.shape], [_lead], name="gather_small_grads")[0]
    summed = _unpack(_slot_sum(all_partials, name="sum_small_grads"), rep_shapes + shard_full_shapes)
    g_rep = summed[:len(rep_shapes)]
    g_meta = lax.dynamic_slice_in_dim(summed[-3], me * mcol, mcol, axis=1)
    g_wa = lax.dynamic_slice_in_dim(summed[-2], me * ccol, ccol, axis=2)
    g_wb = lax.dynamic_slice_in_dim(summed[-1], me * ccol, ccol, axis=2)

    small_w = [norm_g, conv_a_b, lru_wr, lru_br, lru_wi, lru_bi, lru_lambda, final_g, meta, conv_a_w, conv_b_w]
    small_m = [m_norm_g, m_conv_a_b, m_lru_wr, m_lru_br, m_lru_wi, m_lru_bi, m_lru_lambda, m_final_g, m_meta,
               m_conv_a_w, m_conv_b_w]
    small_v = [v_norm_g, v_conv_a_b, v_lru_wr, v_lru_br, v_lru_wi, v_lru_bi, v_lru_lambda, v_final_g, v_meta,
               v_conv_a_w, v_conv_b_w]
    small_g = g_rep + [g_meta, g_wa, g_wb]
    small_out = _adamw(_pack(small_w), _pack(small_g), _pack(small_m), _pack(small_v), slots=False, name="adamw_small")
    small_shapes = [a.shape for a in small_w]
    s_grad, s_delta, s_m, s_v = [_unpack(o, small_shapes) for o in small_out]

    win_out = [_adamw(w_in[l], big_landed[l][0], m_w_in[l], v_w_in[l], slots=True, name=f"adamw_w_in_{l}")
               for l in range(depth)]
    wout_out = [_adamw(w_out[l], big_landed[l][1], m_w_out[l], v_w_out[l], slots=True, name=f"adamw_w_out_{l}")
                for l in range(depth)]

    names = ["norm_g", "conv_a_b", "lru_wr", "lru_br", "lru_wi", "lru_bi", "lru_lambda", "final_g", "meta",
             "conv_a_w", "conv_b_w"]
    order = ["meta", "norm_g", "w_in", "conv_a_w", "conv_a_b", "lru_wr", "lru_br", "lru_wi", "lru_bi", "lru_lambda",
             "conv_b_w", "w_out", "final_g"]

    def family(idx, small):
        table = {nm: small[i] for i, nm in enumerate(names)}
        table["w_in"] = jnp.stack([win_out[l][idx] for l in range(depth)])
        table["w_out"] = jnp.stack([wout_out[l][idx] for l in range(depth)])
        return [table[nm] for nm in order]

    grad_x = dh[n_meta:n_tok][None]
    return (loss, grad_x, *family(0, s_grad), *family(1, s_delta), *family(2, s_m), *family(3, s_v))
```

```python
import functools

import jax
import jax.numpy as jnp
from jax import lax
from jax.experimental import pallas as pl
from jax.experimental.pallas import tpu as pltpu

F32 = jnp.float32
BF16 = jnp.bfloat16
MESH = pl.DeviceIdType.MESH

N_DEV = 8
RMS_EPS = 1e-6
LRU_C = 8.0
ADAM_LR = 0.001
ADAM_B1 = 0.9
ADAM_B2 = 0.999
ADAM_EPS = 1e-08
ADAM_WD = 0.01
ADAM_STEP = 10

V7X_VMEM_LIMIT = 52 * 1024 * 1024
LANES = 128
SUBLANES = 8
TOKEN_TILE = 384
MIX_ROWS = 128
GATE_BLOCK = 256


def _params(sem):
    return pltpu.CompilerParams(dimension_semantics=sem, vmem_limit_bytes=V7X_VMEM_LIMIT)


def _tile(n, target, align=LANES):
    best = None
    for t in range(align, min(n, target) + 1, align):
        if n % t == 0:
            best = t
    return n if best is None else best


def _sigmoid(z):
    return 1.0 / (1.0 + jnp.exp(-z))


def _softplus(z):
    e = jnp.exp(-jnp.abs(z))
    u = 1.0 + e
    l1p = jnp.where(u == 1.0, e, jnp.log(u) * e / jnp.where(u == 1.0, 1.0, u - 1.0))
    return jnp.maximum(z, 0.0) + l1p


def _matmul(a, b, *, ta=False, tb=False, tm, tn, tk, out_dtype=F32, add=None, dep=None, name):
    m, k = (a.shape[1], a.shape[0]) if ta else a.shape
    n = b.shape[0] if tb else b.shape[1]
    assert (b.shape[1] if tb else b.shape[0]) == k
    assert m % tm == 0 and n % tn == 0 and k % tk == 0, (m, n, k, tm, tn, tk)
    nk = k // tk
    a_spec = pl.BlockSpec((tk, tm), lambda i, j, q: (q, i)) if ta else pl.BlockSpec((tm, tk), lambda i, j, q: (i, q))
    b_spec = pl.BlockSpec((tn, tk), lambda i, j, q: (j, q)) if tb else pl.BlockSpec((tk, tn), lambda i, j, q: (q, j))
    o_spec = pl.BlockSpec((tm, tn), lambda i, j, q: (i, j))
    dims = (((0 if ta else 1,), (1 if tb else 0,)), ((), ()))
    has_add = add is not None
    has_dep = dep is not None

    def body(*refs):
        if has_dep:
            refs = refs[:-3] + refs[-2:]
        if has_add:
            a_ref, b_ref, add_ref, o_ref, acc_ref = refs
        else:
            a_ref, b_ref, o_ref, acc_ref = refs
        q = pl.program_id(2)
        part = lax.dot_general(a_ref[...], b_ref[...], dims, preferred_element_type=F32)

        def finish(acc):
            if has_add:
                acc = acc + add_ref[...]
            o_ref[...] = acc.astype(out_dtype)

        if nk == 1:
            finish(part)
        else:
            @pl.when(q == 0)
            def _():
                acc_ref[...] = part

            @pl.when(jnp.logical_and(q > 0, q < nk - 1))
            def _():
                acc_ref[...] += part

            @pl.when(q == nk - 1)
            def _():
                finish(acc_ref[...] + part)

    in_specs = [a_spec, b_spec] + ([o_spec] if has_add else [])
    args = (a, b) + ((add,) if has_add else ())
    if has_dep:
        in_specs.append(pl.BlockSpec((SUBLANES, LANES), lambda i, j, q: (0, 0)))
        args += (dep,)
    acc_shape = (tm, tn) if nk > 1 else (SUBLANES, LANES)
    return pl.pallas_call(
        body, name=name,
        grid=(m // tm, n // tn, nk),
        in_specs=in_specs, out_specs=o_spec,
        out_shape=jax.ShapeDtypeStruct((m, n), out_dtype),
        scratch_shapes=[pltpu.VMEM(acc_shape, F32)],
        compiler_params=_params(("parallel", "parallel", "arbitrary")),
    )(*args)


def _rms_fwd(h, g, *, name):
    tp, d = h.shape
    tr = _tile(tp, 512, SUBLANES)

    def body(h_ref, g_ref, o_ref):
        hv = h_ref[...]
        rstd = lax.rsqrt(jnp.mean(hv * hv, axis=-1, keepdims=True) + RMS_EPS)
        o_ref[...] = (hv * rstd * g_ref[...]).astype(BF16)

    return pl.pallas_call(
        body, name=name, grid=(tp // tr,),
        in_specs=[pl.BlockSpec((tr, d), lambda i: (i, 0)), pl.BlockSpec((1, d), lambda i: (0, 0))],
        out_specs=pl.BlockSpec((tr, d), lambda i: (i, 0)),
        out_shape=jax.ShapeDtypeStruct((tp, d), BF16),
        compiler_params=_params(("parallel",)),
    )(h, g.reshape(1, d))


def _rms_bwd(h, dhn, dout, g, *, name):
    tp, d = h.shape
    tr = _tile(tp, 384, SUBLANES)

    def body(h_ref, dhn_ref, dout_ref, g_ref, dh_ref, dhb_ref, dg_ref):
        hv = h_ref[...]
        rstd = lax.rsqrt(jnp.mean(hv * hv, axis=-1, keepdims=True) + RMS_EPS)
        xhat = hv * rstd
        dn = dhn_ref[...]
        dxhat = dn * g_ref[...]
        dh = dout_ref[...] + rstd * (dxhat - xhat * jnp.mean(dxhat * xhat, axis=-1, keepdims=True))
        dh_ref[...] = dh
        dhb_ref[...] = dh.astype(BF16)
        part = jnp.sum(dn * xhat, axis=0, keepdims=True)

        @pl.when(pl.program_id(0) == 0)
        def _():
            dg_ref[...] = part

        @pl.when(pl.program_id(0) > 0)
        def _():
            dg_ref[...] += part

    row = pl.BlockSpec((tr, d), lambda i: (i, 0))
    vec = pl.BlockSpec((1, d), lambda i: (0, 0))
    return pl.pallas_call(
        body, name=name, grid=(tp // tr,),
        in_specs=[row, row, row, vec],
        out_specs=[row, row, vec],
        out_shape=[jax.ShapeDtypeStruct((tp, d), F32), jax.ShapeDtypeStruct((tp, d), BF16),
                   jax.ShapeDtypeStruct((1, d), F32)],
        compiler_params=_params(("arbitrary",)),
    )(h, dhn, dout, g.reshape(1, d))


def _loss_head(h, tgt, g, *, n_meta, n_tok, name):
    tp, d = h.shape
    tr = _tile(tp, 384, SUBLANES)

    def body(h_ref, t_ref, g_ref, dh_ref, dhb_ref, dg_ref, loss_ref):
        i = pl.program_id(0)
        hv = h_ref[...]
        rstd = lax.rsqrt(jnp.mean(hv * hv, axis=-1, keepdims=True) + RMS_EPS)
        xhat = hv * rstd
        gv = g_ref[...]
        rows = i * tr + lax.broadcasted_iota(jnp.int32, (tr, 1), 0)
        valid = jnp.logical_and(rows >= n_meta, rows < n_tok)
        err = jnp.where(valid, xhat * gv - t_ref[...], 0.0)
        dy = err * (1.0 / d)
        dxhat = dy * gv
        dh = rstd * (dxhat - xhat * jnp.mean(dxhat * xhat, axis=-1, keepdims=True))
        dh_ref[...] = dh
        dhb_ref[...] = dh.astype(BF16)
        dg_part = jnp.sum(dy * xhat, axis=0, keepdims=True)
        per_row = jnp.sum(err * err, axis=-1, keepdims=True) * (1.0 / d)
        loss_part = jnp.broadcast_to(0.5 * jnp.sum(per_row, axis=0, keepdims=True), (SUBLANES, LANES))

        @pl.when(i == 0)
        def _():
            dg_ref[...] = dg_part
            loss_ref[...] = loss_part

        @pl.when(i > 0)
        def _():
            dg_ref[...] += dg_part
            loss_ref[...] += loss_part

    row = pl.BlockSpec((tr, d), lambda i: (i, 0))
    vec = pl.BlockSpec((1, d), lambda i: (0, 0))
    return pl.pallas_call(
        body, name=name, grid=(tp // tr,),
        in_specs=[row, row, vec],
        out_specs=[row, row, vec, pl.BlockSpec((SUBLANES, LANES), lambda i: (0, 0))],
        out_shape=[jax.ShapeDtypeStruct((tp, d), F32), jax.ShapeDtypeStruct((tp, d), BF16),
                   jax.ShapeDtypeStruct((1, d), F32), jax.ShapeDtypeStruct((SUBLANES, LANES), F32)],
        compiler_params=_params(("arbitrary",)),
    )(h, tgt, g.reshape(1, d))


def _shift_down(halo, tile, s):
    if s == 0:
        return tile
    ext = jnp.concatenate([halo, tile], axis=0)
    return pltpu.roll(ext, s, 0)[SUBLANES:]


def _shift_up(tile, head, s):
    if s == 0:
        return tile
    ext = jnp.concatenate([tile, head], axis=0)
    n = ext.shape[0]
    return pltpu.roll(ext, n - s, 0)[: tile.shape[0]]


def _scan_rows_fwd(a, b):
    row = lax.broadcasted_iota(jnp.int32, a.shape, 0)
    for s in (1, 2, 4):
        a_sh = pltpu.roll(a, s, 0)
        b_sh = pltpu.roll(b, s, 0)
        m = row >= s
        b = jnp.where(m, a * b_sh + b, b)
        a = jnp.where(m, a * a_sh, a)
    return a, b


def _scan_rows_bwd(c, d):
    row = lax.broadcasted_iota(jnp.int32, c.shape, 0)
    for s in (1, 2, 4):
        c_sh = pltpu.roll(c, SUBLANES - s, 0)
        d_sh = pltpu.roll(d, SUBLANES - s, 0)
        m = row < SUBLANES - s
        d = jnp.where(m, c * d_sh + d, d)
        c = jnp.where(m, c * c_sh, c)
    return c, d


def _gates(ca, wr, wi, br, bi, sp):
    cab = ca.astype(BF16)
    r = _sigmoid(jnp.dot(cab, wr, preferred_element_type=F32) + br)
    ig = _sigmoid(jnp.dot(cab, wi, preferred_element_type=F32) + bi)
    la = -LRU_C * r * sp
    a = jnp.exp(la)
    mult = jnp.sqrt(-jnp.tanh(la) * (a * a + 1.0))
    return r, ig, a, mult


def _mixer_fwd(u, wa, ba, wr_blk, br, wi_blk, bi, lam, wb, *, name):
    tp, din = u.shape
    dl = din // 6
    tt = MIX_ROWS
    cw = GATE_BLOCK
    nch = dl // cw
    assert tp % tt == 0 and dl % cw == 0

    def body(u_ref, wa_ref, ba_ref, wr_ref, br_ref, wi_ref, bi_ref, lam_ref, wb_ref,
             ca_ref, hs_ref, y_ref, xa_tail, v_tail, h_carry, a_s, b_s):
        @pl.when(pl.program_id(0) == 0)
        def _():
            xa_tail[...] = jnp.zeros_like(xa_tail)
            v_tail[...] = jnp.zeros_like(v_tail)
            h_carry[...] = jnp.zeros_like(h_carry)

        for ch in range(nch):
            cs = slice(ch * cw, (ch + 1) * cw)

            def seg(s):
                return slice(s * dl + ch * cw, s * dl + (ch + 1) * cw)

            xa = u_ref[:, seg(0)]
            halo = xa_tail[:, cs]
            ca = ba_ref[:, cs] + wa_ref[3:4, cs] * xa
            for kk in range(3):
                ca = ca + wa_ref[kk:kk + 1, cs] * _shift_down(halo, xa, 3 - kk)
            xa_tail[:, cs] = xa[tt - SUBLANES:]
            ca_ref[:, cs] = ca
            sp = _softplus(-lam_ref[:, cs])
            _, ig, a, mult = _gates(ca, wr_ref[ch], wi_ref[ch], br_ref[:, cs], bi_ref[:, cs], sp)
            a_s[:, cs] = a
            b_s[:, cs] = mult * (ig * ca)

            bv = u_ref[:, seg(2)]
            v = u_ref[:, seg(3)] * u_ref[:, seg(4)]
            gb = u_ref[:, seg(5)]
            vh = v_tail[:, cs]
            cb = wb_ref[2:3, cs] * v
            for kk in range(2):
                cb = cb + wb_ref[kk:kk + 1, cs] * _shift_down(vh, v, 2 - kk)
            v_tail[:, cs] = v[tt - SUBLANES:]
            y_ref[:, dl + ch * cw: dl + (ch + 1) * cw] = (bv * cb * (gb * _sigmoid(gb))).astype(BF16)

        def group(gi, hprev):
            rows = pl.ds(pl.multiple_of(gi * SUBLANES, SUBLANES), SUBLANES)
            a8, b8 = _scan_rows_fwd(a_s[rows, :], b_s[rows, :])
            h8 = b8 + a8 * hprev
            hs_ref[rows, :] = h8
            return jnp.broadcast_to(h8[SUBLANES - 1:SUBLANES, :], h8.shape)

        h_carry[...] = lax.fori_loop(0, tt // SUBLANES, group, h_carry[...])

        for ch in range(nch):
            cs = slice(ch * cw, (ch + 1) * cw)
            ga = u_ref[:, dl + ch * cw: dl + (ch + 1) * cw]
            y_ref[:, cs] = (hs_ref[:, cs] * (ga * _sigmoid(ga))).astype(BF16)

    row = lambda w: pl.BlockSpec((tt, w), lambda i: (i, 0))
    full = lambda shp: pl.BlockSpec(shp, lambda i: tuple(0 for _ in shp))
    return pl.pallas_call(
        body, name=name, grid=(tp // tt,),
        in_specs=[row(din), full((4, dl)), full((1, dl)), full((nch, cw, cw)), full((1, dl)),
                  full((nch, cw, cw)), full((1, dl)), full((1, dl)), full((3, dl))],
        out_specs=[row(dl), row(dl), row(2 * dl)],
        out_shape=[jax.ShapeDtypeStruct((tp, dl), F32), jax.ShapeDtypeStruct((tp, dl), F32),
                   jax.ShapeDtypeStruct((tp, 2 * dl), BF16)],
        scratch_shapes=[pltpu.VMEM((SUBLANES, dl), F32), pltpu.VMEM((SUBLANES, dl), F32),
                        pltpu.VMEM((SUBLANES, dl), F32), pltpu.VMEM((tt, dl), F32), pltpu.VMEM((tt, dl), F32)],
        compiler_params=_params(("arbitrary",)),
    )(u, wa, ba, wr_blk, br, wi_blk, bi, lam, wb)


SG_WA, SG_BA, SG_BR, SG_BI, SG_LAM, SG_WB, SG_ROWS = 0, 4, 5, 6, 7, 8, 16


def _mixer_bwd(u, ca, hs, dy, wa, wr_blk, br, wi_blk, bi, lam, wb, *, name):
    tp, din = u.shape
    dl = din // 6
    tt = MIX_ROWS
    cw = GATE_BLOCK
    nch = dl // cw
    nt = tp // tt
    hb = tt // SUBLANES
    tn_dims = (((0,), (0,)), ((), ()))
    nt_dims = (((1,), (1,)), ((), ()))

    def body(u_ref, uh_ref, ca_ref, hs_ref, hsh_ref, dy_ref, wa_ref, wr_ref, br_ref, wi_ref, bi_ref, lam_ref, wb_ref,
             du_ref, sg_ref, dwr_ref, dwi_ref,
             g_carry, a_head, dca_head, dcb_head, r_s, i_s, a_s, an_s, d_s, g_s):
        i = pl.program_id(0)
        first_tile = i == nt - 1

        @pl.when(i == 0)
        def _():
            for ref in (g_carry, a_head, dca_head, dcb_head, sg_ref, dwr_ref, dwi_ref):
                ref[...] = jnp.zeros_like(ref)

        def halo_of(x):
            return jnp.where(first_tile, 0.0, x)

        for ch in range(nch):
            cs = slice(ch * cw, (ch + 1) * cw)
            cav = ca_ref[:, cs]
            sp = _softplus(-lam_ref[:, cs])
            r, ig, a, _ = _gates(cav, wr_ref[ch], wi_ref[ch], br_ref[:, cs], bi_ref[:, cs], sp)
            r_s[:, cs] = r
            i_s[:, cs] = ig
            a_s[:, cs] = a
            an_s[:, cs] = _shift_up(a, a_head[:, cs], 1)
            a_head[:, cs] = a[:SUBLANES]
            ga = u_ref[:, dl + ch * cw: dl + (ch + 1) * cw]
            d_s[:, cs] = dy_ref[:, cs] * (ga * _sigmoid(ga))

        def group(k, gnext):
            gi = tt // SUBLANES - 1 - k
            rows = pl.ds(pl.multiple_of(gi * SUBLANES, SUBLANES), SUBLANES)
            c8, d8 = _scan_rows_bwd(an_s[rows, :], d_s[rows, :])
            g8 = d8 + c8 * gnext
            g_s[rows, :] = g8
            return jnp.broadcast_to(g8[0:1, :], g8.shape)

        g_carry[...] = lax.fori_loop(0, tt // SUBLANES, group, g_carry[...])

        def acc_row(r0, val):
            sg_ref[r0:r0 + 1, cs_cur[0]] += jnp.sum(val, axis=0, keepdims=True)

        cs_cur = [None]
        for ch in range(nch):
            cs = slice(ch * cw, (ch + 1) * cw)
            cs_cur[0] = cs

            def seg(s):
                return slice(s * dl + ch * cw, s * dl + (ch + 1) * cw)

            cav = ca_ref[:, cs]
            r = r_s[:, cs]
            ig = i_s[:, cs]
            a = a_s[:, cs]
            g = g_s[:, cs]
            hsv = hs_ref[:, cs]
            lamv = lam_ref[:, cs]
            sp = _softplus(-lamv)
            la = -LRU_C * r * sp
            e2 = a * a
            mult = jnp.sqrt(-jnp.tanh(la) * (e2 + 1.0))
            hprev = _shift_down(halo_of(hsh_ref[:, cs]), hsv, 1)
            dla = g * hprev * a - g * (ig * cav) * e2 / mult
            gm = g * mult
            dzi = gm * cav * ig * (1.0 - ig)
            dca = gm * ig
            dzr = dla * (-LRU_C * sp) * r * (1.0 - r)
            acc_row(SG_LAM, dla * (-LRU_C * r) * (-_sigmoid(-lamv)))
            acc_row(SG_BR, dzr)
            acc_row(SG_BI, dzi)
            dzr_b = dzr.astype(BF16)
            dzi_b = dzi.astype(BF16)
            cab = cav.astype(BF16)
            dca = dca + lax.dot_general(dzr_b, wr_ref[ch], nt_dims, preferred_element_type=F32)
            dca = dca + lax.dot_general(dzi_b, wi_ref[ch], nt_dims, preferred_element_type=F32)
            dwr_ref[ch] += lax.dot_general(cab, dzr_b, tn_dims, preferred_element_type=F32)
            dwi_ref[ch] += lax.dot_general(cab, dzi_b, tn_dims, preferred_element_type=F32)
            acc_row(SG_BA, dca)
            xa = u_ref[:, seg(0)]
            xah = halo_of(uh_ref[:, seg(0)])
            head = dca_head[:, cs]
            dxa = wa_ref[3:4, cs] * dca
            acc_row(SG_WA + 3, dca * xa)
            for kk in range(3):
                acc_row(SG_WA + kk, dca * _shift_down(xah, xa, 3 - kk))
                dxa = dxa + wa_ref[kk:kk + 1, cs] * _shift_up(dca, head, 3 - kk)
            dca_head[:, cs] = dca[:SUBLANES]
            ga = u_ref[:, seg(1)]
            sga = _sigmoid(ga)
            dga = dy_ref[:, cs] * hsv * (sga * (1.0 + ga * (1.0 - sga)))
            du_ref[:, seg(0)] = dxa.astype(BF16)
            du_ref[:, seg(1)] = dga.astype(BF16)

            bv = u_ref[:, seg(2)]
            cv = u_ref[:, seg(3)]
            xb = u_ref[:, seg(4)]
            gb = u_ref[:, seg(5)]
            dyb = dy_ref[:, dl + ch * cw: dl + (ch + 1) * cw]
            v = cv * xb
            vh = halo_of(uh_ref[:, seg(3)] * uh_ref[:, seg(4)])
            v1 = _shift_down(vh, v, 1)
            v2 = _shift_down(vh, v, 2)
            cb = wb_ref[2:3, cs] * v + wb_ref[1:2, cs] * v1 + wb_ref[0:1, cs] * v2
            sgb = _sigmoid(gb)
            sl = gb * sgb
            dcb = dyb * bv * sl
            du_ref[:, seg(2)] = (dyb * cb * sl).astype(BF16)
            du_ref[:, seg(5)] = (dyb * bv * cb * (sgb * (1.0 + gb * (1.0 - sgb)))).astype(BF16)
            acc_row(SG_WB + 2, dcb * v)
            acc_row(SG_WB + 1, dcb * v1)
            acc_row(SG_WB + 0, dcb * v2)
            bhead = dcb_head[:, cs]
            dv = wb_ref[2:3, cs] * dcb + wb_ref[1:2, cs] * _shift_up(dcb, bhead, 1) \
                + wb_ref[0:1, cs] * _shift_up(dcb, bhead, 2)
            dcb_head[:, cs] = dcb[:SUBLANES]
            du_ref[:, seg(3)] = (dv * xb).astype(BF16)
            du_ref[:, seg(4)] = (dv * cv).astype(BF16)

    rev = lambda w: pl.BlockSpec((tt, w), lambda i: (nt - 1 - i, 0))
    halo = lambda w: pl.BlockSpec((SUBLANES, w), lambda i: (jnp.maximum((nt - 1 - i) * hb - 1, 0), 0))
    full = lambda shp: pl.BlockSpec(shp, lambda i: tuple(0 for _ in shp))
    vm = lambda r: pltpu.VMEM((r, dl), F32)
    return pl.pallas_call(
        body, name=name, grid=(nt,),
        in_specs=[rev(din), halo(din), rev(dl), rev(dl), halo(dl), rev(2 * dl), full((4, dl)),
                  full((nch, cw, cw)), full((1, dl)), full((nch, cw, cw)), full((1, dl)), full((1, dl)), full((3, dl))],
        out_specs=[rev(din), full((SG_ROWS, dl)), full((nch, cw, cw)), full((nch, cw, cw))],
        out_shape=[jax.ShapeDtypeStruct((tp, din), BF16), jax.ShapeDtypeStruct((SG_ROWS, dl), F32),
                   jax.ShapeDtypeStruct((nch, cw, cw), F32), jax.ShapeDtypeStruct((nch, cw, cw), F32)],
        scratch_shapes=[vm(SUBLANES), vm(SUBLANES), vm(SUBLANES), vm(SUBLANES),
                        vm(tt), vm(tt), vm(tt), vm(tt), vm(tt), vm(tt)],
        compiler_params=_params(("arbitrary",)),
    )(u, u, ca, hs, hs, dy, wa, wr_blk, br, wi_blk, bi, lam, wb)


def _adamw(w, g, m, v, *, name, landed=None):
    r, c = w.shape
    tr = _tile(r, 256, 2 * SUBLANES)
    bc1 = 1.0 - ADAM_B1 ** ADAM_STEP
    bc2 = 1.0 - ADAM_B2 ** ADAM_STEP
    slots = landed is not None

    def body(*refs):
        if slots:
            w_ref, g_ref, l_ref, m_ref, v_ref, grad_ref, delta_ref, nm_ref, nv_ref = refs
            gv = g_ref[...].astype(F32)
            for s in range(N_DEV - 1):
                gv = gv + l_ref[s].astype(F32)
        else:
            w_ref, g_ref, m_ref, v_ref, grad_ref, delta_ref, nm_ref, nv_ref = refs
            gv = g_ref[...]
        wv = w_ref[...]
        mn = ADAM_B1 * m_ref[...] + (1.0 - ADAM_B1) * gv
        vn = ADAM_B2 * v_ref[...] + (1.0 - ADAM_B2) * (gv * gv)
        m_hat = mn / bc1
        v_hat = vn / bc2
        grad_ref[...] = gv
        delta_ref[...] = -ADAM_LR * (m_hat / (jnp.sqrt(v_hat) + ADAM_EPS) + ADAM_WD * wv)
        nm_ref[...] = mn
        nv_ref[...] = vn

    blk = pl.BlockSpec((tr, c), lambda i: (i, 0))
    l_spec = [pl.BlockSpec((N_DEV - 1, tr, c), lambda i: (0, i, 0))] if slots else []
    shp = jax.ShapeDtypeStruct((r, c), F32)
    args = (w, g, landed, m, v) if slots else (w, g, m, v)
    return pl.pallas_call(
        body, name=name, grid=(r // tr,),
        in_specs=[blk, blk] + l_spec + [blk, blk], out_specs=[blk, blk, blk, blk],
        out_shape=[shp, shp, shp, shp],
        compiler_params=_params(("parallel",)),
    )(*args)


def _slot_sum(g, *, name):
    _, r, c = g.shape
    tr = _tile(r, 512, SUBLANES)

    def body(g_ref, o_ref):
        gv = g_ref[0]
        for s in range(1, N_DEV):
            gv = gv + g_ref[s]
        o_ref[...] = gv

    return pl.pallas_call(
        body, name=name, grid=(r // tr,),
        in_specs=[pl.BlockSpec((N_DEV, tr, c), lambda i: (0, i, 0))],
        out_specs=pl.BlockSpec((tr, c), lambda i: (i, 0)),
        out_shape=jax.ShapeDtypeStruct((r, c), F32),
        compiler_params=_params(("parallel",)),
    )(g)


def _mesh_pos():
    x, y, c = lax.axis_index("x"), lax.axis_index("y"), lax.axis_index("c")
    return x, y, c, 4 * x + 2 * y + c


ANY = pl.BlockSpec(memory_space=pl.ANY)


def _all_gather(srcs, out_shapes, views, *, name):
    n = len(srcs)

    def body(*refs):
        src = refs[:n]
        dst = refs[n:2 * n]
        send_sems, recv_sems, local_sems = refs[2 * n:]
        x, y, c, me = _mesh_pos()
        sibling = (x, y, 1 - c)
        chips = [(1 - x, y), (x, 1 - y), (1 - x, 1 - y)]

        def dev(px, py, pc):
            return 4 * px + 2 * py + pc

        def copy(a, k, block, to, from_src=False):
            win = views[a](dst[a], dev(*block))
            return pltpu.make_async_remote_copy(
                src_ref=src[a] if from_src else win, dst_ref=win,
                send_sem=send_sems.at[a * 7 + k], recv_sem=recv_sems.at[a * 7 + k],
                device_id=to, device_id_type=MESH)

        mine = [pltpu.make_async_copy(src[a], views[a](dst[a], me), local_sems.at[a]) for a in range(n)]
        started = []
        for a in range(n):
            mine[a].start()
            first = [copy(a, 0, (x, y, c), sibling, True)]
            first += [copy(a, 1 + j, (x, y, c), (*chip, c), True) for j, chip in enumerate(chips)]
            for cp in first:
                cp.start()
            started += first
        for a in range(n):
            for j, chip in enumerate(chips):
                copy(a, 1 + j, (*chip, c), (x, y, c)).wait_recv()
                fwd = copy(a, 4 + j, (*chip, c), sibling)
                fwd.start()
                started.append(fwd)
        for a in range(n):
            copy(a, 0, (x, y, 1 - c), (x, y, c)).wait_recv()
            for j, chip in enumerate(chips):
                copy(a, 4 + j, (*chip, 1 - c), (x, y, c)).wait_recv()
        for cp in started:
            cp.wait_send()
        for a in range(n):
            mine[a].wait()

    return pl.pallas_call(
        body, name=name,
        in_specs=[ANY] * n, out_specs=[ANY] * n,
        out_shape=[jax.ShapeDtypeStruct(s, x.dtype) for s, x in zip(out_shapes, srcs)],
        scratch_shapes=[pltpu.SemaphoreType.DMA((7 * n,)), pltpu.SemaphoreType.DMA((7 * n,)),
                        pltpu.SemaphoreType.DMA((n,))],
    )(*srcs)


HBM = pl.BlockSpec(memory_space=pltpu.HBM)
SEM = pl.BlockSpec(memory_space=pltpu.SEMAPHORE)
EFFECT = pltpu.SideEffectType.DATAFLOW_SIDE_EFFECTING


def _peer_copies(n, wins, src, land, send_sems, recv_sems):
    x, y, c, me = _mesh_pos()
    out = []
    for a in range(n):
        for k in range(1, N_DEV):
            px = 1 - x if k & 4 else x
            py = 1 - y if k & 2 else y
            pc = 1 - c if k & 1 else c
            s_win, d_win = wins[a](src[a], land[a], me, 4 * px + 2 * py + pc, k)
            out.append(pltpu.make_async_remote_copy(
                src_ref=s_win, dst_ref=d_win,
                send_sem=send_sems.at[a * 7 + k - 1], recv_sem=recv_sems.at[a * 7 + k - 1],
                device_id=(px, py, pc), device_id_type=MESH))
    return out


def _push_start(srcs, lands, wins, *, name):
    n = len(srcs)

    def body(*refs):
        src = refs[:n]
        land = refs[n:2 * n]
        send_sems, recv_sems = refs[2 * n], refs[2 * n + 1]
        token = refs[-1]
        for cp in _peer_copies(n, wins, src, land, send_sems, recv_sems):
            cp.start()
        token[...] = jnp.zeros_like(token)

    bufs = (*srcs, *lands)
    return pl.pallas_call(
        body, name=name,
        out_shape=(pltpu.SemaphoreType.DMA((7 * n,)), pltpu.SemaphoreType.DMA((7 * n,)),
                   *[pltpu.HBM(v.shape, v.dtype) for v in bufs], jax.ShapeDtypeStruct((SUBLANES, LANES), F32)),
        in_specs=[HBM] * (2 * n),
        out_specs=(SEM, SEM, *[HBM] * (2 * n), pl.BlockSpec(memory_space=pltpu.VMEM)),
        input_output_aliases={i: 2 + i for i in range(2 * n)},
        compiler_params=pltpu.CompilerParams(has_side_effects=EFFECT),
    )(*[pltpu.with_memory_space_constraint(v, pltpu.HBM) for v in bufs])


def _push_wait(handle, wins, after, *, name):
    send_sems, recv_sems, *bufs, _ = handle
    n = len(bufs) // 2

    def body(*refs):
        src = refs[:n]
        land = refs[n:2 * n]
        for cp in _peer_copies(n, wins, src, land, refs[2 * n], refs[2 * n + 1]):
            cp.wait_send()
            cp.wait_recv()

    outs = pl.pallas_call(
        body, name=name,
        out_shape=tuple(pltpu.HBM(v.shape, v.dtype) for v in bufs),
        in_specs=[HBM] * (2 * n) + [SEM, SEM, ANY],
        out_specs=tuple([HBM] * (2 * n)),
        input_output_aliases={i: i for i in range(2 * n)},
        compiler_params=pltpu.CompilerParams(has_side_effects=EFFECT),
    )(*bufs, send_sems, recv_sems, after)
    return outs[:n], outs[n:]


def _gather_lead(src, land, me, peer, k):
    return src, land.at[me]


def _gather_cols(width):
    def win(src, land, me, peer, k):
        return src, land.at[:, pl.ds(me * width, width)]
    return win


def _scatter_lead(src, land, me, peer, k):
    return src.at[peer], land.at[k - 1]


def _scatter_cols(width):
    def win(src, land, me, peer, k):
        return src.at[:, pl.ds(peer * width, width)], land.at[k - 1]
    return win


def _dep(x, token):
    return x + token[0, 0].astype(x.dtype)


def _lead(ref, d):
    return ref.at[d]


def _col_window(width):
    def view(ref, d):
        return ref.at[:, pl.ds(d * width, width)]
    return view


def _pack(arrs):
    flat = jnp.concatenate([a.reshape(-1).astype(F32) for a in arrs])
    n = flat.shape[0]
    rows = -(-n // (SUBLANES * LANES)) * SUBLANES
    return jnp.pad(flat, (0, rows * LANES - n)).reshape(rows, LANES)


def _unpack(buf, shapes):
    flat = buf.reshape(-1)
    out, off = [], 0
    for s in shapes:
        n = 1
        for q in s:
            n *= q
        out.append(flat[off:off + n].reshape(s))
        off += n
    return out


def _blockdiag(w, cw):
    h, hd, _ = w.shape
    per = cw // hd
    wg = w.reshape(h // per, per, hd, hd)
    eye = jnp.eye(per, dtype=w.dtype)
    blk = jnp.einsum("gpij,pq->gpiqj", wg, eye)
    return blk.reshape(h // per, cw, cw).astype(BF16)


def _blockdiag_extract(g, hd):
    n, cw, _ = g.shape
    per = cw // hd
    g5 = g.reshape(n, per, hd, per, hd)
    idx = jnp.arange(per)
    return g5[:, idx, :, idx, :].transpose(1, 0, 2, 3).reshape(n * per, hd, hd)


def kernel(x, meta, norm_g, w_in, conv_a_w, conv_a_b, lru_wr, lru_br, lru_wi, lru_bi, lru_lambda, conv_b_w, w_out, final_g, loss_target, m_meta, m_norm_g, m_w_in, m_conv_a_w, m_conv_a_b, m_lru_wr, m_lru_br, m_lru_wi, m_lru_bi, m_lru_lambda, m_conv_b_w, m_w_out, m_final_g, v_meta, v_norm_g, v_w_in, v_conv_a_w, v_conv_a_b, v_lru_wr, v_lru_br, v_lru_wi, v_lru_bi, v_lru_lambda, v_conv_b_w, v_w_out, v_final_g):
    _, seq, d = x.shape
    n_meta = meta.shape[0]
    depth = w_in.shape[0]
    din = w_in.shape[2] * N_DEV
    dl = din // 6
    dmix = 2 * dl
    wcol = w_in.shape[2]
    wrow = w_out.shape[1]
    mcol = meta.shape[1]
    ccol = conv_a_w.shape[2]
    heads, hd = lru_wr.shape[1], lru_wr.shape[2]
    n_tok = n_meta + seq
    tp = -(-n_tok // TOKEN_TILE) * TOKEN_TILE
    me = 4 * lax.axis_index("x") + 2 * lax.axis_index("y") + lax.axis_index("c")

    bf = lambda a: a.astype(BF16)
    small_mine = _pack([meta, conv_a_w, conv_b_w])
    first = _all_gather([bf(w_in[0]), small_mine], [(d, din), (N_DEV,) + small_mine.shape],
                        [_col_window(wcol), _lead], name="gather_first")
    w_in_full = [first[0]] + [None] * (depth - 1)
    w_out_full = [None] * depth
    parts = [_unpack(first[1][s], [meta.shape, conv_a_w.shape, conv_b_w.shape]) for s in range(N_DEV)]
    meta_full = jnp.concatenate([p[0] for p in parts], axis=1)
    wa_full = jnp.concatenate([p[1] for p in parts], axis=2)
    wb_full = jnp.concatenate([p[2] for p in parts], axis=2)

    def own_cols(own):
        return lax.dynamic_update_slice(jnp.zeros((d, din), BF16), own, (0, me * wcol))

    def own_rows(own):
        return lax.dynamic_update_slice(jnp.zeros((N_DEV, wrow, d), BF16), own[None], (me, 0, 0))

    wout0 = _push_start([bf(w_out[0])], [own_rows(bf(w_out[0]))], [_gather_lead], name="gather_wout0_start")
    later_src = [_dep(bf(w_in[l]), wout0[-1]) for l in range(1, depth)] + [bf(w_out[l]) for l in range(1, depth)]
    later_land = [own_cols(s) for s in later_src[:depth - 1]] + [own_rows(s) for s in later_src[depth - 1:]]
    later_wins = [_gather_cols(wcol)] * (depth - 1) + [_gather_lead] * (depth - 1)
    later = _push_start(later_src, later_land, later_wins, name="gather_later_start")

    wr_blk = [_blockdiag(lru_wr[l], GATE_BLOCK) for l in range(depth)]
    wi_blk = [_blockdiag(lru_wi[l], GATE_BLOCK) for l in range(depth)]
    vec = lambda a: a.reshape(1, dl)

    h = jnp.concatenate([meta_full, x[0], jnp.zeros((tp - n_tok, d), F32)], axis=0)
    tgt = jnp.pad(loss_target[0], ((n_meta, tp - n_tok), (0, 0)))
    tm = _tile(tp, 1408)
    saved = []
    for l in range(depth):
        gain = _dep(norm_g[l], later[-1]) if l == 0 else norm_g[l]
        hn = _rms_fwd(h, gain, name=f"rms_fwd_{l}")
        u = _matmul(hn, w_in_full[l], tm=tm, tn=_tile(din, 768), tk=d, name=f"mm_u_{l}")
        ca, hs, y = _mixer_fwd(u, wa_full[l], vec(conv_a_b[l]), wr_blk[l], vec(lru_br[l]), wi_blk[l], vec(lru_bi[l]),
                               vec(lru_lambda[l]), wb_full[l], name=f"mixer_fwd_{l}")
        if l == 0:
            _, landed = _push_wait(wout0, [_gather_lead], y, name="gather_wout0_wait")
            w_out_full[0] = landed[0].reshape(dmix, d)
        h_next = _matmul(y, w_out_full[l], tm=tm, tn=_tile(d, 512), tk=dmix, add=h, name=f"mm_out_{l}")
        if l == 0 and depth > 1:
            _, landed = _push_wait(later, later_wins, h_next, name="gather_later_wait")
            for j in range(1, depth):
                w_in_full[j] = landed[j - 1]
                w_out_full[j] = landed[depth - 1 + j - 1].reshape(dmix, d)
        saved.append((h, hn, u, ca, hs, y))
        h = h_next

    dh, dhb, dg_final, loss_part = _loss_head(h, tgt, final_g, n_meta=n_meta, n_tok=n_tok, name="loss_head")
    loss = lax.psum(loss_part[0, 0], ("x", "y", "c"))

    small_grads = [None] * depth
    sent_out = [None] * depth
    sent_in = [None] * depth
    scatter_in = [_scatter_cols(wcol)]
    token = None
    dg_norms = []
    for l in reversed(range(depth)):
        h_in, hn, u, ca, hs, y = saved[l]
        dy = _matmul(dhb, w_out_full[l], tb=True, tm=tm, tn=_tile(dmix, 512), tk=d, dep=token, name=f"mm_dy_{l}")
        dw_out = _matmul(y, dhb, ta=True, tm=_tile(dmix, 1024), tn=_tile(d, 1024), tk=tm, out_dtype=BF16,
                         name=f"mm_dwout_{l}")
        sent_out[l] = _push_start([dw_out.reshape(N_DEV, wrow, d)], [lax.empty((N_DEV - 1, wrow, d), BF16)],
                                  [_scatter_lead], name=f"scatter_wout_{l}_start")
        du, sg, dwr, dwi = _mixer_bwd(u, ca, hs, dy, wa_full[l], wr_blk[l], vec(lru_br[l]), wi_blk[l], vec(lru_bi[l]),
                                      vec(lru_lambda[l]), _dep(wb_full[l], sent_out[l][-1]), name=f"mixer_bwd_{l}")
        small_grads[l] = (sg, dwr, dwi)
        if l == 0:
            early = _pack([
                jnp.stack([small_grads[j][0][SG_BA] for j in range(depth)]),
                jnp.stack([_blockdiag_extract(small_grads[j][1], hd) for j in range(depth)]),
                jnp.stack([small_grads[j][0][SG_BR] for j in range(depth)]),
                jnp.stack([_blockdiag_extract(small_grads[j][2], hd) for j in range(depth)]),
                jnp.stack([small_grads[j][0][SG_BI] for j in range(depth)]),
                jnp.stack([small_grads[j][0][SG_LAM] for j in range(depth)]),
                jnp.stack([small_grads[j][0][SG_WA:SG_WA + 4] for j in range(depth)]),
                jnp.stack([small_grads[j][0][SG_WB:SG_WB + 3] for j in range(depth)]),
                dg_final[0], *dg_norms])
            early_land = lax.dynamic_update_slice(jnp.zeros((N_DEV,) + early.shape, F32), early[None], (me, 0, 0))
            sent_early = _push_start([early], [early_land], [_gather_lead], name="gather_early_grads_start")
        dw_in = _matmul(hn, du, ta=True, tm=_tile(d, 1024), tn=_tile(din, 1536), tk=tm, out_dtype=BF16,
                        dep=sent_early[-1] if l == 0 else None, name=f"mm_dwin_{l}")
        sent_in[l] = _push_start([dw_in], [lax.empty((N_DEV - 1, d, wcol), BF16)], scatter_in,
                                 name=f"scatter_win_{l}_start")
        token = sent_in[l][-1]
        dhn = _matmul(du, w_in_full[l], tb=True, tm=tm, tn=_tile(d, 1024), tk=_tile(din, 1536), dep=token,
                      name=f"mm_dhn_{l}")
        dh, dhb, dg_norm = _rms_bwd(h_in, dhn, dh, norm_g[l], name=f"rms_bwd_{l}")
        if l > 0:
            dg_norms.append(dg_norm[0])

    late = _pack([dg_norm[0], dh[:n_meta]])
    late_all = _all_gather([late], [(N_DEV,) + late.shape], [_lead], name="gather_late_grads")[0]
    late_sum = _unpack(_slot_sum(late_all, name="sum_late_grads"), [(d,), (n_meta, d)])
    _, early_all = _push_wait(sent_early, [_gather_lead], late_sum[0], name="gather_early_grads_wait")
    early_shapes = [conv_a_b.shape, lru_wr.shape, lru_br.shape, lru_wi.shape, lru_bi.shape, lru_lambda.shape,
                    (depth, 4, dl), (depth, 3, dl), final_g.shape] + [(d,)] * (depth - 1)
    e = _unpack(_slot_sum(early_all[0], name="sum_early_grads"), early_shapes)
    g_norm = jnp.stack([late_sum[0]] + e[9:][::-1])
    g_meta = lax.dynamic_slice_in_dim(late_sum[1], me * mcol, mcol, axis=1)
    g_wa = lax.dynamic_slice_in_dim(e[6], me * ccol, ccol, axis=2)
    g_wb = lax.dynamic_slice_in_dim(e[7], me * ccol, ccol, axis=2)

    small_w = [norm_g, conv_a_b, lru_wr, lru_br, lru_wi, lru_bi, lru_lambda, final_g, meta, conv_a_w, conv_b_w]
    small_m = [m_norm_g, m_conv_a_b, m_lru_wr, m_lru_br, m_lru_wi, m_lru_bi, m_lru_lambda, m_final_g, m_meta,
               m_conv_a_w, m_conv_b_w]
    small_v = [v_norm_g, v_conv_a_b, v_lru_wr, v_lru_br, v_lru_wi, v_lru_bi, v_lru_lambda, v_final_g, v_meta,
               v_conv_a_w, v_conv_b_w]
    small_g = [g_norm, e[0], e[1], e[2], e[3], e[4], e[5], e[8], g_meta, g_wa, g_wb]
    small_out = _adamw(_pack(small_w), _pack(small_g), _pack(small_m), _pack(small_v), name="adamw_small")
    small_shapes = [a.shape for a in small_w]
    s_grad, s_delta, s_m, s_v = [_unpack(o, small_shapes) for o in small_out]

    win_out = [None] * depth
    wout_out = [None] * depth
    after = small_out[0]
    for l in reversed(range(depth)):
        src, landed = _push_wait(sent_out[l], [_scatter_lead], after, name=f"scatter_wout_{l}_wait")
        own = lax.dynamic_index_in_dim(src[0], me, 0, keepdims=False)
        wout_out[l] = _adamw(w_out[l], own, m_w_out[l], v_w_out[l], landed=landed[0], name=f"adamw_w_out_{l}")
        src, landed = _push_wait(sent_in[l], scatter_in, wout_out[l][0], name=f"scatter_win_{l}_wait")
        own = lax.dynamic_slice_in_dim(src[0], me * wcol, wcol, axis=1)
        win_out[l] = _adamw(w_in[l], own, m_w_in[l], v_w_in[l], landed=landed[0], name=f"adamw_w_in_{l}")
        after = win_out[l][0]

    names = ["norm_g", "conv_a_b", "lru_wr", "lru_br", "lru_wi", "lru_bi", "lru_lambda", "final_g", "meta",
             "conv_a_w", "conv_b_w"]
    order = ["meta", "norm_g", "w_in", "conv_a_w", "conv_a_b", "lru_wr", "lru_br", "lru_wi", "lru_bi", "lru_lambda",
             "conv_b_w", "w_out", "final_g"]

    def family(idx, small):
        table = {nm: small[i] for i, nm in enumerate(names)}
        table["w_in"] = jnp.stack([win_out[l][idx] for l in range(depth)])
        table["w_out"] = jnp.stack([wout_out[l][idx] for l in range(depth)])
        return [table[nm] for nm in order]

    grad_x = dh[n_meta:n_tok][None]
    return (loss, grad_x, *family(0, s_grad), *family(1, s_delta), *family(2, s_m), *family(3, s_v))
```

```python
import functools

import jax
import jax.numpy as jnp
from jax import lax
from jax.experimental import pallas as pl
from jax.experimental.pallas import tpu as pltpu

F32 = jnp.float32
BF16 = jnp.bfloat16
MESH = pl.DeviceIdType.MESH

N_DEV = 8
RMS_EPS = 1e-6
LRU_C = 8.0
ADAM_LR = 0.001
ADAM_B1 = 0.9
ADAM_B2 = 0.999
ADAM_EPS = 1e-08
ADAM_WD = 0.01
ADAM_STEP = 10

V7X_VMEM_LIMIT = 52 * 1024 * 1024
LANES = 128
SUBLANES = 8
TOKEN_TILE = 384
MIX_ROWS = 128
GATE_BLOCK = 256


def _params(sem):
    return pltpu.CompilerParams(dimension_semantics=sem, vmem_limit_bytes=V7X_VMEM_LIMIT)


def _tile(n, target, align=LANES):
    best = None
    for t in range(align, min(n, target) + 1, align):
        if n % t == 0:
            best = t
    return n if best is None else best


def _sigmoid(z):
    return 1.0 / (1.0 + jnp.exp(-z))


def _softplus(z):
    e = jnp.exp(-jnp.abs(z))
    u = 1.0 + e
    l1p = jnp.where(u == 1.0, e, jnp.log(u) * e / jnp.where(u == 1.0, 1.0, u - 1.0))
    return jnp.maximum(z, 0.0) + l1p


def _matmul(a, b, *, ta=False, tb=False, tm, tn, tk, out_dtype=F32, add=None, dep=None, name):
    m, k = (a.shape[1], a.shape[0]) if ta else a.shape
    n = b.shape[0] if tb else b.shape[1]
    assert (b.shape[1] if tb else b.shape[0]) == k
    assert m % tm == 0 and n % tn == 0 and k % tk == 0, (m, n, k, tm, tn, tk)
    nk = k // tk
    a_spec = pl.BlockSpec((tk, tm), lambda i, j, q: (q, i)) if ta else pl.BlockSpec((tm, tk), lambda i, j, q: (i, q))
    b_spec = pl.BlockSpec((tn, tk), lambda i, j, q: (j, q)) if tb else pl.BlockSpec((tk, tn), lambda i, j, q: (q, j))
    o_spec = pl.BlockSpec((tm, tn), lambda i, j, q: (i, j))
    dims = (((0 if ta else 1,), (1 if tb else 0,)), ((), ()))
    has_add = add is not None
    has_dep = dep is not None

    def body(*refs):
        if has_dep:
            refs = refs[:-3] + refs[-2:]
        if has_add:
            a_ref, b_ref, add_ref, o_ref, acc_ref = refs
        else:
            a_ref, b_ref, o_ref, acc_ref = refs
        q = pl.program_id(2)
        part = lax.dot_general(a_ref[...], b_ref[...], dims, preferred_element_type=F32)

        def finish(acc):
            if has_add:
                acc = acc + add_ref[...]
            o_ref[...] = acc.astype(out_dtype)

        if nk == 1:
            finish(part)
        else:
            @pl.when(q == 0)
            def _():
                acc_ref[...] = part

            @pl.when(jnp.logical_and(q > 0, q < nk - 1))
            def _():
                acc_ref[...] += part

            @pl.when(q == nk - 1)
            def _():
                finish(acc_ref[...] + part)

    in_specs = [a_spec, b_spec] + ([o_spec] if has_add else [])
    args = (a, b) + ((add,) if has_add else ())
    if has_dep:
        in_specs.append(pl.BlockSpec((SUBLANES, LANES), lambda i, j, q: (0, 0)))
        args += (dep,)
    acc_shape = (tm, tn) if nk > 1 else (SUBLANES, LANES)
    return pl.pallas_call(
        body, name=name,
        grid=(m // tm, n // tn, nk),
        in_specs=in_specs, out_specs=o_spec,
        out_shape=jax.ShapeDtypeStruct((m, n), out_dtype),
        scratch_shapes=[pltpu.VMEM(acc_shape, F32)],
        compiler_params=_params(("parallel", "parallel", "arbitrary")),
    )(*args)


def _rms_fwd(h, g, *, name):
    tp, d = h.shape
    tr = _tile(tp, 512, SUBLANES)

    def body(h_ref, g_ref, o_ref):
        hv = h_ref[...]
        rstd = lax.rsqrt(jnp.mean(hv * hv, axis=-1, keepdims=True) + RMS_EPS)
        o_ref[...] = (hv * rstd * g_ref[...]).astype(BF16)

    return pl.pallas_call(
        body, name=name, grid=(tp // tr,),
        in_specs=[pl.BlockSpec((tr, d), lambda i: (i, 0)), pl.BlockSpec((1, d), lambda i: (0, 0))],
        out_specs=pl.BlockSpec((tr, d), lambda i: (i, 0)),
        out_shape=jax.ShapeDtypeStruct((tp, d), BF16),
        compiler_params=_params(("parallel",)),
    )(h, g.reshape(1, d))


def _rms_bwd(h, dhn, dout, g, *, name):
    tp, d = h.shape
    tr = _tile(tp, 384, SUBLANES)

    def body(h_ref, dhn_ref, dout_ref, g_ref, dh_ref, dhb_ref, dg_ref):
        hv = h_ref[...]
        rstd = lax.rsqrt(jnp.mean(hv * hv, axis=-1, keepdims=True) + RMS_EPS)
        xhat = hv * rstd
        dn = dhn_ref[...]
        dxhat = dn * g_ref[...]
        dh = dout_ref[...] + rstd * (dxhat - xhat * jnp.mean(dxhat * xhat, axis=-1, keepdims=True))
        dh_ref[...] = dh
        dhb_ref[...] = dh.astype(BF16)
        part = jnp.sum(dn * xhat, axis=0, keepdims=True)

        @pl.when(pl.program_id(0) == 0)
        def _():
            dg_ref[...] = part

        @pl.when(pl.program_id(0) > 0)
        def _():
            dg_ref[...] += part

    row = pl.BlockSpec((tr, d), lambda i: (i, 0))
    vec = pl.BlockSpec((1, d), lambda i: (0, 0))
    return pl.pallas_call(
        body, name=name, grid=(tp // tr,),
        in_specs=[row, row, row, vec],
        out_specs=[row, row, vec],
        out_shape=[jax.ShapeDtypeStruct((tp, d), F32), jax.ShapeDtypeStruct((tp, d), BF16),
                   jax.ShapeDtypeStruct((1, d), F32)],
        compiler_params=_params(("arbitrary",)),
    )(h, dhn, dout, g.reshape(1, d))


def _loss_head(h, tgt, g, *, n_meta, n_tok, name):
    tp, d = h.shape
    tr = _tile(tp, 384, SUBLANES)

    def body(h_ref, t_ref, g_ref, dh_ref, dhb_ref, dg_ref, loss_ref):
        i = pl.program_id(0)
        hv = h_ref[...]
        rstd = lax.rsqrt(jnp.mean(hv * hv, axis=-1, keepdims=True) + RMS_EPS)
        xhat = hv * rstd
        gv = g_ref[...]
        rows = i * tr + lax.broadcasted_iota(jnp.int32, (tr, 1), 0)
        valid = jnp.logical_and(rows >= n_meta, rows < n_tok)
        err = jnp.where(valid, xhat * gv - t_ref[...], 0.0)
        dy = err * (1.0 / d)
        dxhat = dy * gv
        dh = rstd * (dxhat - xhat * jnp.mean(dxhat * xhat, axis=-1, keepdims=True))
        dh_ref[...] = dh
        dhb_ref[...] = dh.astype(BF16)
        dg_part = jnp.sum(dy * xhat, axis=0, keepdims=True)
        per_row = jnp.sum(err * err, axis=-1, keepdims=True) * (1.0 / d)
        loss_part = jnp.broadcast_to(0.5 * jnp.sum(per_row, axis=0, keepdims=True), (SUBLANES, LANES))

        @pl.when(i == 0)
        def _():
            dg_ref[...] = dg_part
            loss_ref[...] = loss_part

        @pl.when(i > 0)
        def _():
            dg_ref[...] += dg_part
            loss_ref[...] += loss_part

    row = pl.BlockSpec((tr, d), lambda i: (i, 0))
    vec = pl.BlockSpec((1, d), lambda i: (0, 0))
    return pl.pallas_call(
        body, name=name, grid=(tp // tr,),
        in_specs=[row, row, vec],
        out_specs=[row, row, vec, pl.BlockSpec((SUBLANES, LANES), lambda i: (0, 0))],
        out_shape=[jax.ShapeDtypeStruct((tp, d), F32), jax.ShapeDtypeStruct((tp, d), BF16),
                   jax.ShapeDtypeStruct((1, d), F32), jax.ShapeDtypeStruct((SUBLANES, LANES), F32)],
        compiler_params=_params(("arbitrary",)),
    )(h, tgt, g.reshape(1, d))


def _shift_down(halo, tile, s):
    if s == 0:
        return tile
    ext = jnp.concatenate([halo, tile], axis=0)
    return pltpu.roll(ext, s, 0)[SUBLANES:]


def _shift_up(tile, head, s):
    if s == 0:
        return tile
    ext = jnp.concatenate([tile, head], axis=0)
    n = ext.shape[0]
    return pltpu.roll(ext, n - s, 0)[: tile.shape[0]]


def _scan_rows_fwd(a, b):
    row = lax.broadcasted_iota(jnp.int32, a.shape, 0)
    for s in (1, 2, 4):
        a_sh = pltpu.roll(a, s, 0)
        b_sh = pltpu.roll(b, s, 0)
        m = row >= s
        b = jnp.where(m, a * b_sh + b, b)
        a = jnp.where(m, a * a_sh, a)
    return a, b


def _scan_rows_bwd(c, d):
    row = lax.broadcasted_iota(jnp.int32, c.shape, 0)
    for s in (1, 2, 4):
        c_sh = pltpu.roll(c, SUBLANES - s, 0)
        d_sh = pltpu.roll(d, SUBLANES - s, 0)
        m = row < SUBLANES - s
        d = jnp.where(m, c * d_sh + d, d)
        c = jnp.where(m, c * c_sh, c)
    return c, d


def _gates(ca, wr, wi, br, bi, sp):
    cab = ca.astype(BF16)
    r = _sigmoid(jnp.dot(cab, wr, preferred_element_type=F32) + br)
    ig = _sigmoid(jnp.dot(cab, wi, preferred_element_type=F32) + bi)
    la = -LRU_C * r * sp
    a = jnp.exp(la)
    mult = jnp.sqrt(-jnp.tanh(la) * (a * a + 1.0))
    return r, ig, a, mult


def _mixer_fwd(u, wa, ba, wr_blk, br, wi_blk, bi, lam, wb, *, name):
    tp, din = u.shape
    dl = din // 6
    tt = MIX_ROWS
    cw = GATE_BLOCK
    nch = dl // cw
    assert tp % tt == 0 and dl % cw == 0

    def body(u_ref, wa_ref, ba_ref, wr_ref, br_ref, wi_ref, bi_ref, lam_ref, wb_ref,
             ca_ref, hs_ref, y_ref, xa_tail, v_tail, h_carry, a_s, b_s):
        @pl.when(pl.program_id(0) == 0)
        def _():
            xa_tail[...] = jnp.zeros_like(xa_tail)
            v_tail[...] = jnp.zeros_like(v_tail)
            h_carry[...] = jnp.zeros_like(h_carry)

        for ch in range(nch):
            cs = slice(ch * cw, (ch + 1) * cw)

            def seg(s):
                return slice(s * dl + ch * cw, s * dl + (ch + 1) * cw)

            xa = u_ref[:, seg(0)]
            halo = xa_tail[:, cs]
            ca = ba_ref[:, cs] + wa_ref[3:4, cs] * xa
            for kk in range(3):
                ca = ca + wa_ref[kk:kk + 1, cs] * _shift_down(halo, xa, 3 - kk)
            xa_tail[:, cs] = xa[tt - SUBLANES:]
            ca_ref[:, cs] = ca
            sp = _softplus(-lam_ref[:, cs])
            _, ig, a, mult = _gates(ca, wr_ref[ch], wi_ref[ch], br_ref[:, cs], bi_ref[:, cs], sp)
            a_s[:, cs] = a
            b_s[:, cs] = mult * (ig * ca)

            bv = u_ref[:, seg(2)]
            v = u_ref[:, seg(3)] * u_ref[:, seg(4)]
            gb = u_ref[:, seg(5)]
            vh = v_tail[:, cs]
            cb = wb_ref[2:3, cs] * v
            for kk in range(2):
                cb = cb + wb_ref[kk:kk + 1, cs] * _shift_down(vh, v, 2 - kk)
            v_tail[:, cs] = v[tt - SUBLANES:]
            y_ref[:, dl + ch * cw: dl + (ch + 1) * cw] = (bv * cb * (gb * _sigmoid(gb))).astype(BF16)

        def group(gi, hprev):
            rows = pl.ds(pl.multiple_of(gi * SUBLANES, SUBLANES), SUBLANES)
            a8, b8 = _scan_rows_fwd(a_s[rows, :], b_s[rows, :])
            h8 = b8 + a8 * hprev
            hs_ref[rows, :] = h8
            return jnp.broadcast_to(h8[SUBLANES - 1:SUBLANES, :], h8.shape)

        h_carry[...] = lax.fori_loop(0, tt // SUBLANES, group, h_carry[...])

        for ch in range(nch):
            cs = slice(ch * cw, (ch + 1) * cw)
            ga = u_ref[:, dl + ch * cw: dl + (ch + 1) * cw]
            y_ref[:, cs] = (hs_ref[:, cs] * (ga * _sigmoid(ga))).astype(BF16)

    row = lambda w: pl.BlockSpec((tt, w), lambda i: (i, 0))
    full = lambda shp: pl.BlockSpec(shp, lambda i: tuple(0 for _ in shp))
    return pl.pallas_call(
        body, name=name, grid=(tp // tt,),
        in_specs=[row(din), full((4, dl)), full((1, dl)), full((nch, cw, cw)), full((1, dl)),
                  full((nch, cw, cw)), full((1, dl)), full((1, dl)), full((3, dl))],
        out_specs=[row(dl), row(dl), row(2 * dl)],
        out_shape=[jax.ShapeDtypeStruct((tp, dl), F32), jax.ShapeDtypeStruct((tp, dl), F32),
                   jax.ShapeDtypeStruct((tp, 2 * dl), BF16)],
        scratch_shapes=[pltpu.VMEM((SUBLANES, dl), F32), pltpu.VMEM((SUBLANES, dl), F32),
                        pltpu.VMEM((SUBLANES, dl), F32), pltpu.VMEM((tt, dl), F32), pltpu.VMEM((tt, dl), F32)],
        compiler_params=_params(("arbitrary",)),
    )(u, wa, ba, wr_blk, br, wi_blk, bi, lam, wb)


SG_WA, SG_BA, SG_BR, SG_BI, SG_LAM, SG_WB, SG_ROWS = 0, 4, 5, 6, 7, 8, 16


def _mixer_bwd(u, ca, hs, dy, wa, wr_blk, br, wi_blk, bi, lam, wb, *, name):
    tp, din = u.shape
    dl = din // 6
    tt = MIX_ROWS
    cw = GATE_BLOCK
    nch = dl // cw
    nt = tp // tt
    hb = tt // SUBLANES
    tn_dims = (((0,), (0,)), ((), ()))
    nt_dims = (((1,), (1,)), ((), ()))

    def body(u_ref, uh_ref, ca_ref, hs_ref, hsh_ref, dy_ref, wa_ref, wr_ref, br_ref, wi_ref, bi_ref, lam_ref, wb_ref,
             du_ref, sg_ref, dwr_ref, dwi_ref,
             g_carry, a_head, dca_head, dcb_head, r_s, i_s, a_s, an_s, d_s, g_s):
        i = pl.program_id(0)
        first_tile = i == nt - 1

        @pl.when(i == 0)
        def _():
            for ref in (g_carry, a_head, dca_head, dcb_head, sg_ref, dwr_ref, dwi_ref):
                ref[...] = jnp.zeros_like(ref)

        def halo_of(x):
            return jnp.where(first_tile, 0.0, x)

        for ch in range(nch):
            cs = slice(ch * cw, (ch + 1) * cw)
            cav = ca_ref[:, cs]
            sp = _softplus(-lam_ref[:, cs])
            r, ig, a, _ = _gates(cav, wr_ref[ch], wi_ref[ch], br_ref[:, cs], bi_ref[:, cs], sp)
            r_s[:, cs] = r
            i_s[:, cs] = ig
            a_s[:, cs] = a
            an_s[:, cs] = _shift_up(a, a_head[:, cs], 1)
            a_head[:, cs] = a[:SUBLANES]
            ga = u_ref[:, dl + ch * cw: dl + (ch + 1) * cw]
            d_s[:, cs] = dy_ref[:, cs] * (ga * _sigmoid(ga))

        def group(k, gnext):
            gi = tt // SUBLANES - 1 - k
            rows = pl.ds(pl.multiple_of(gi * SUBLANES, SUBLANES), SUBLANES)
            c8, d8 = _scan_rows_bwd(an_s[rows, :], d_s[rows, :])
            g8 = d8 + c8 * gnext
            g_s[rows, :] = g8
            return jnp.broadcast_to(g8[0:1, :], g8.shape)

        g_carry[...] = lax.fori_loop(0, tt // SUBLANES, group, g_carry[...])

        def acc_row(r0, val):
            sg_ref[r0:r0 + 1, cs_cur[0]] += jnp.sum(val, axis=0, keepdims=True)

        cs_cur = [None]
        for ch in range(nch):
            cs = slice(ch * cw, (ch + 1) * cw)
            cs_cur[0] = cs

            def seg(s):
                return slice(s * dl + ch * cw, s * dl + (ch + 1) * cw)

            cav = ca_ref[:, cs]
            r = r_s[:, cs]
            ig = i_s[:, cs]
            a = a_s[:, cs]
            g = g_s[:, cs]
            hsv = hs_ref[:, cs]
            lamv = lam_ref[:, cs]
            sp = _softplus(-lamv)
            la = -LRU_C * r * sp
            e2 = a * a
            mult = jnp.sqrt(-jnp.tanh(la) * (e2 + 1.0))
            hprev = _shift_down(halo_of(hsh_ref[:, cs]), hsv, 1)
            dla = g * hprev * a - g * (ig * cav) * e2 / mult
            gm = g * mult
            dzi = gm * cav * ig * (1.0 - ig)
            dca = gm * ig
            dzr = dla * (-LRU_C * sp) * r * (1.0 - r)
            acc_row(SG_LAM, dla * (-LRU_C * r) * (-_sigmoid(-lamv)))
            acc_row(SG_BR, dzr)
            acc_row(SG_BI, dzi)
            dzr_b = dzr.astype(BF16)
            dzi_b = dzi.astype(BF16)
            cab = cav.astype(BF16)
            dca = dca + lax.dot_general(dzr_b, wr_ref[ch], nt_dims, preferred_element_type=F32)
            dca = dca + lax.dot_general(dzi_b, wi_ref[ch], nt_dims, preferred_element_type=F32)
            dwr_ref[ch] += lax.dot_general(cab, dzr_b, tn_dims, preferred_element_type=F32)
            dwi_ref[ch] += lax.dot_general(cab, dzi_b, tn_dims, preferred_element_type=F32)
            acc_row(SG_BA, dca)
            xa = u_ref[:, seg(0)]
            xah = halo_of(uh_ref[:, seg(0)])
            head = dca_head[:, cs]
            dxa = wa_ref[3:4, cs] * dca
            acc_row(SG_WA + 3, dca * xa)
            for kk in range(3):
                acc_row(SG_WA + kk, dca * _shift_down(xah, xa, 3 - kk))
                dxa = dxa + wa_ref[kk:kk + 1, cs] * _shift_up(dca, head, 3 - kk)
            dca_head[:, cs] = dca[:SUBLANES]
            ga = u_ref[:, seg(1)]
            sga = _sigmoid(ga)
            dga = dy_ref[:, cs] * hsv * (sga * (1.0 + ga * (1.0 - sga)))
            du_ref[:, seg(0)] = dxa.astype(BF16)
            du_ref[:, seg(1)] = dga.astype(BF16)

            bv = u_ref[:, seg(2)]
            cv = u_ref[:, seg(3)]
            xb = u_ref[:, seg(4)]
            gb = u_ref[:, seg(5)]
            dyb = dy_ref[:, dl + ch * cw: dl + (ch + 1) * cw]
            v = cv * xb
            vh = halo_of(uh_ref[:, seg(3)] * uh_ref[:, seg(4)])
            v1 = _shift_down(vh, v, 1)
            v2 = _shift_down(vh, v, 2)
            cb = wb_ref[2:3, cs] * v + wb_ref[1:2, cs] * v1 + wb_ref[0:1, cs] * v2
            sgb = _sigmoid(gb)
            sl = gb * sgb
            dcb = dyb * bv * sl
            du_ref[:, seg(2)] = (dyb * cb * sl).astype(BF16)
            du_ref[:, seg(5)] = (dyb * bv * cb * (sgb * (1.0 + gb * (1.0 - sgb)))).astype(BF16)
            acc_row(SG_WB + 2, dcb * v)
            acc_row(SG_WB + 1, dcb * v1)
            acc_row(SG_WB + 0, dcb * v2)
            bhead = dcb_head[:, cs]
            dv = wb_ref[2:3, cs] * dcb + wb_ref[1:2, cs] * _shift_up(dcb, bhead, 1) \
                + wb_ref[0:1, cs] * _shift_up(dcb, bhead, 2)
            dcb_head[:, cs] = dcb[:SUBLANES]
            du_ref[:, seg(3)] = (dv * xb).astype(BF16)
            du_ref[:, seg(4)] = (dv * cv).astype(BF16)

    rev = lambda w: pl.BlockSpec((tt, w), lambda i: (nt - 1 - i, 0))
    halo = lambda w: pl.BlockSpec((SUBLANES, w), lambda i: (jnp.maximum((nt - 1 - i) * hb - 1, 0), 0))
    full = lambda shp: pl.BlockSpec(shp, lambda i: tuple(0 for _ in shp))
    vm = lambda r: pltpu.VMEM((r, dl), F32)
    return pl.pallas_call(
        body, name=name, grid=(nt,),
        in_specs=[rev(din), halo(din), rev(dl), rev(dl), halo(dl), rev(2 * dl), full((4, dl)),
                  full((nch, cw, cw)), full((1, dl)), full((nch, cw, cw)), full((1, dl)), full((1, dl)), full((3, dl))],
        out_specs=[rev(din), full((SG_ROWS, dl)), full((nch, cw, cw)), full((nch, cw, cw))],
        out_shape=[jax.ShapeDtypeStruct((tp, din), BF16), jax.ShapeDtypeStruct((SG_ROWS, dl), F32),
                   jax.ShapeDtypeStruct((nch, cw, cw), F32), jax.ShapeDtypeStruct((nch, cw, cw), F32)],
        scratch_shapes=[vm(SUBLANES), vm(SUBLANES), vm(SUBLANES), vm(SUBLANES),
                        vm(tt), vm(tt), vm(tt), vm(tt), vm(tt), vm(tt)],
        compiler_params=_params(("arbitrary",)),
    )(u, u, ca, hs, hs, dy, wa, wr_blk, br, wi_blk, bi, lam, wb)


def _adamw(w, g, m, v, *, name, landed=None, layer=None, depth=None, into=None):
    r, c = w.shape
    tr = _tile(r, 256, 2 * SUBLANES)
    bc1 = 1.0 - ADAM_B1 ** ADAM_STEP
    bc2 = 1.0 - ADAM_B2 ** ADAM_STEP
    slots = landed is not None

    def body(*refs):
        if into is not None:
            refs = refs[:-8] + refs[-4:]
        if slots:
            w_ref, g_ref, l_ref, m_ref, v_ref, grad_ref, delta_ref, nm_ref, nv_ref = refs
            gv = g_ref[...].astype(F32)
            for s in range(N_DEV - 1):
                gv = gv + l_ref[s].astype(F32)
        else:
            w_ref, g_ref, m_ref, v_ref, grad_ref, delta_ref, nm_ref, nv_ref = refs
            gv = g_ref[...]
        wv = w_ref[...]
        mn = ADAM_B1 * m_ref[...] + (1.0 - ADAM_B1) * gv
        vn = ADAM_B2 * v_ref[...] + (1.0 - ADAM_B2) * (gv * gv)
        m_hat = mn / bc1
        v_hat = vn / bc2
        grad_ref[...] = gv
        delta_ref[...] = -ADAM_LR * (m_hat / (jnp.sqrt(v_hat) + ADAM_EPS) + ADAM_WD * wv)
        nm_ref[...] = mn
        nv_ref[...] = vn

    blk = pl.BlockSpec((tr, c), lambda i: (i, 0))
    l_spec = [pl.BlockSpec((N_DEV - 1, tr, c), lambda i: (0, i, 0))] if slots else []
    args = (w, g, landed, m, v) if slots else (w, g, m, v)
    in_specs = [blk, blk] + l_spec + [blk, blk]
    if depth is None:
        shp = jax.ShapeDtypeStruct((r, c), F32)
        out_blk = blk
    else:
        shp = jax.ShapeDtypeStruct((depth, r, c), F32)
        out_blk = pl.BlockSpec((None, tr, c), lambda i: (layer, i, 0))
    aliases = {}
    if into is not None:
        aliases = {len(args) + j: j for j in range(4)}
        in_specs = in_specs + [ANY] * 4
        args = args + tuple(into)
    return pl.pallas_call(
        body, name=name, grid=(r // tr,),
        in_specs=in_specs, out_specs=[out_blk] * 4,
        out_shape=[shp] * 4, input_output_aliases=aliases,
        compiler_params=_params(("parallel",)),
    )(*args)


def _slot_sum(g, *, name):
    _, r, c = g.shape
    tr = _tile(r, 512, SUBLANES)

    def body(g_ref, o_ref):
        gv = g_ref[0]
        for s in range(1, N_DEV):
            gv = gv + g_ref[s]
        o_ref[...] = gv

    return pl.pallas_call(
        body, name=name, grid=(r // tr,),
        in_specs=[pl.BlockSpec((N_DEV, tr, c), lambda i: (0, i, 0))],
        out_specs=pl.BlockSpec((tr, c), lambda i: (i, 0)),
        out_shape=jax.ShapeDtypeStruct((r, c), F32),
        compiler_params=_params(("parallel",)),
    )(g)


def _mesh_pos():
    x, y, c = lax.axis_index("x"), lax.axis_index("y"), lax.axis_index("c")
    return x, y, c, 4 * x + 2 * y + c


ANY = pl.BlockSpec(memory_space=pl.ANY)


def _all_gather(srcs, out_shapes, views, *, name):
    n = len(srcs)

    def body(*refs):
        src = refs[:n]
        dst = refs[n:2 * n]
        send_sems, recv_sems, local_sems = refs[2 * n:]
        x, y, c, me = _mesh_pos()
        sibling = (x, y, 1 - c)
        chips = [(1 - x, y), (x, 1 - y), (1 - x, 1 - y)]

        def dev(px, py, pc):
            return 4 * px + 2 * py + pc

        def copy(a, k, block, to, from_src=False):
            win = views[a](dst[a], dev(*block))
            return pltpu.make_async_remote_copy(
                src_ref=src[a] if from_src else win, dst_ref=win,
                send_sem=send_sems.at[a * 7 + k], recv_sem=recv_sems.at[a * 7 + k],
                device_id=to, device_id_type=MESH)

        mine = [pltpu.make_async_copy(src[a], views[a](dst[a], me), local_sems.at[a]) for a in range(n)]
        started = []
        for a in range(n):
            mine[a].start()
            first = [copy(a, 0, (x, y, c), sibling, True)]
            first += [copy(a, 1 + j, (x, y, c), (*chip, c), True) for j, chip in enumerate(chips)]
            for cp in first:
                cp.start()
            started += first
        for a in range(n):
            for j, chip in enumerate(chips):
                copy(a, 1 + j, (*chip, c), (x, y, c)).wait_recv()
                fwd = copy(a, 4 + j, (*chip, c), sibling)
                fwd.start()
                started.append(fwd)
        for a in range(n):
            copy(a, 0, (x, y, 1 - c), (x, y, c)).wait_recv()
            for j, chip in enumerate(chips):
                copy(a, 4 + j, (*chip, 1 - c), (x, y, c)).wait_recv()
        for cp in started:
            cp.wait_send()
        for a in range(n):
            mine[a].wait()

    return pl.pallas_call(
        body, name=name,
        in_specs=[ANY] * n, out_specs=[ANY] * n,
        out_shape=[jax.ShapeDtypeStruct(s, x.dtype) for s, x in zip(out_shapes, srcs)],
        scratch_shapes=[pltpu.SemaphoreType.DMA((7 * n,)), pltpu.SemaphoreType.DMA((7 * n,)),
                        pltpu.SemaphoreType.DMA((n,))],
    )(*srcs)


HBM = pl.BlockSpec(memory_space=pltpu.HBM)
SEM = pl.BlockSpec(memory_space=pltpu.SEMAPHORE)
EFFECT = pltpu.SideEffectType.DATAFLOW_SIDE_EFFECTING


def _peer_copies(n, wins, src, land, send_sems, recv_sems):
    x, y, c, me = _mesh_pos()
    out = []
    for a in range(n):
        for k in range(1, N_DEV):
            px = 1 - x if k & 4 else x
            py = 1 - y if k & 2 else y
            pc = 1 - c if k & 1 else c
            s_win, d_win = wins[a](src[a], land[a], me, 4 * px + 2 * py + pc, k)
            out.append(pltpu.make_async_remote_copy(
                src_ref=s_win, dst_ref=d_win,
                send_sem=send_sems.at[a * 7 + k - 1], recv_sem=recv_sems.at[a * 7 + k - 1],
                device_id=(px, py, pc), device_id_type=MESH))
    return out


def _push_start(srcs, lands, wins, *, name):
    n = len(srcs)

    def body(*refs):
        src = refs[:n]
        land = refs[n:2 * n]
        send_sems, recv_sems = refs[2 * n], refs[2 * n + 1]
        token = refs[-1]
        for cp in _peer_copies(n, wins, src, land, send_sems, recv_sems):
            cp.start()
        token[...] = jnp.zeros_like(token)

    bufs = (*srcs, *lands)
    return pl.pallas_call(
        body, name=name,
        out_shape=(pltpu.SemaphoreType.DMA((7 * n,)), pltpu.SemaphoreType.DMA((7 * n,)),
                   *[pltpu.HBM(v.shape, v.dtype) for v in bufs], jax.ShapeDtypeStruct((SUBLANES, LANES), F32)),
        in_specs=[HBM] * (2 * n),
        out_specs=(SEM, SEM, *[HBM] * (2 * n), pl.BlockSpec(memory_space=pltpu.VMEM)),
        input_output_aliases={i: 2 + i for i in range(2 * n)},
        compiler_params=pltpu.CompilerParams(has_side_effects=EFFECT),
    )(*[pltpu.with_memory_space_constraint(v, pltpu.HBM) for v in bufs])


def _push_wait(handle, wins, after, *, name):
    send_sems, recv_sems, *bufs, _ = handle
    n = len(bufs) // 2

    def body(*refs):
        src = refs[:n]
        land = refs[n:2 * n]
        for cp in _peer_copies(n, wins, src, land, refs[2 * n], refs[2 * n + 1]):
            cp.wait_send()
            cp.wait_recv()

    outs = pl.pallas_call(
        body, name=name,
        out_shape=tuple(pltpu.HBM(v.shape, v.dtype) for v in bufs),
        in_specs=[HBM] * (2 * n) + [SEM, SEM, ANY],
        out_specs=tuple([HBM] * (2 * n)),
        input_output_aliases={i: i for i in range(2 * n)},
        compiler_params=pltpu.CompilerParams(has_side_effects=EFFECT),
    )(*bufs, send_sems, recv_sems, after)
    return outs[:n], outs[n:]


def _gather_lead(src, land, me, peer, k):
    return src, land.at[me]


def _gather_cols(width):
    def win(src, land, me, peer, k):
        return src, land.at[:, pl.ds(me * width, width)]
    return win


def _scatter_lead(src, land, me, peer, k):
    return src.at[peer], land.at[k - 1]


def _scatter_cols(width):
    def win(src, land, me, peer, k):
        return src.at[:, pl.ds(peer * width, width)], land.at[k - 1]
    return win


def _dep(x, token):
    return x + token[0, 0].astype(x.dtype)


def _lead(ref, d):
    return ref.at[d]


def _col_window(width):
    def view(ref, d):
        return ref.at[:, pl.ds(d * width, width)]
    return view


def _pack(arrs):
    flat = jnp.concatenate([a.reshape(-1).astype(F32) for a in arrs])
    n = flat.shape[0]
    rows = -(-n // (SUBLANES * LANES)) * SUBLANES
    return jnp.pad(flat, (0, rows * LANES - n)).reshape(rows, LANES)


def _unpack(buf, shapes):
    flat = buf.reshape(-1)
    out, off = [], 0
    for s in shapes:
        n = 1
        for q in s:
            n *= q
        out.append(flat[off:off + n].reshape(s))
        off += n
    return out


def _blockdiag(w, cw):
    h, hd, _ = w.shape
    per = cw // hd
    wg = w.reshape(h // per, per, hd, hd)
    eye = jnp.eye(per, dtype=w.dtype)
    blk = jnp.einsum("gpij,pq->gpiqj", wg, eye)
    return blk.reshape(h // per, cw, cw).astype(BF16)


def _blockdiag_extract(g, hd):
    n, cw, _ = g.shape
    per = cw // hd
    g5 = g.reshape(n, per, hd, per, hd)
    idx = jnp.arange(per)
    return g5[:, idx, :, idx, :].transpose(1, 0, 2, 3).reshape(n * per, hd, hd)


def kernel(x, meta, norm_g, w_in, conv_a_w, conv_a_b, lru_wr, lru_br, lru_wi, lru_bi, lru_lambda, conv_b_w, w_out, final_g, loss_target, m_meta, m_norm_g, m_w_in, m_conv_a_w, m_conv_a_b, m_lru_wr, m_lru_br, m_lru_wi, m_lru_bi, m_lru_lambda, m_conv_b_w, m_w_out, m_final_g, v_meta, v_norm_g, v_w_in, v_conv_a_w, v_conv_a_b, v_lru_wr, v_lru_br, v_lru_wi, v_lru_bi, v_lru_lambda, v_conv_b_w, v_w_out, v_final_g):
    _, seq, d = x.shape
    n_meta = meta.shape[0]
    depth = w_in.shape[0]
    din = w_in.shape[2] * N_DEV
    dl = din // 6
    dmix = 2 * dl
    wcol = w_in.shape[2]
    wrow = w_out.shape[1]
    mcol = meta.shape[1]
    ccol = conv_a_w.shape[2]
    heads, hd = lru_wr.shape[1], lru_wr.shape[2]
    n_tok = n_meta + seq
    tp = -(-n_tok // TOKEN_TILE) * TOKEN_TILE
    me = 4 * lax.axis_index("x") + 2 * lax.axis_index("y") + lax.axis_index("c")

    bf = lambda a: a.astype(BF16)
    small_mine = _pack([meta, conv_a_w, conv_b_w])
    first = _all_gather([bf(w_in[0]), small_mine], [(d, din), (N_DEV,) + small_mine.shape],
                        [_col_window(wcol), _lead], name="gather_first")
    w_in_full = [first[0]] + [None] * (depth - 1)
    w_out_full = [None] * depth
    parts = [_unpack(first[1][s], [meta.shape, conv_a_w.shape, conv_b_w.shape]) for s in range(N_DEV)]
    meta_full = jnp.concatenate([p[0] for p in parts], axis=1)
    wa_full = jnp.concatenate([p[1] for p in parts], axis=2)
    wb_full = jnp.concatenate([p[2] for p in parts], axis=2)

    def own_cols(own):
        return lax.dynamic_update_slice(lax.empty((d, din), BF16), own, (0, me * wcol))

    def own_rows(own):
        return lax.dynamic_update_slice(lax.empty((N_DEV, wrow, d), BF16), own[None], (me, 0, 0))

    w_in_full[0], wout0_src = lax.optimization_barrier((first[0], bf(w_out[0])))
    wout0 = _push_start([wout0_src], [own_rows(wout0_src)], [_gather_lead], name="gather_wout0_start")
    later_src = [_dep(bf(w_in[l]), wout0[-1]) for l in range(1, depth)] + [bf(w_out[l]) for l in range(1, depth)]
    later_land = [own_cols(s) for s in later_src[:depth - 1]] + [own_rows(s) for s in later_src[depth - 1:]]
    later_wins = [_gather_cols(wcol)] * (depth - 1) + [_gather_lead] * (depth - 1)
    later = _push_start(later_src, later_land, later_wins, name="gather_later_start")

    wr_blk = [_blockdiag(lru_wr[l], GATE_BLOCK) for l in range(depth)]
    wi_blk = [_blockdiag(lru_wi[l], GATE_BLOCK) for l in range(depth)]
    vec = lambda a: a.reshape(1, dl)

    h = jnp.concatenate([meta_full, x[0], jnp.zeros((tp - n_tok, d), F32)], axis=0)
    tgt = jnp.pad(loss_target[0], ((n_meta, tp - n_tok), (0, 0)))
    tm = _tile(tp, 1408)
    saved = []
    for l in range(depth):
        gain = _dep(norm_g[l], later[-1]) if l == 0 else norm_g[l]
        hn = _rms_fwd(h, gain, name=f"rms_fwd_{l}")
        u = _matmul(hn, w_in_full[l], tm=tm, tn=_tile(din, 768), tk=d, name=f"mm_u_{l}")
        ca, hs, y = _mixer_fwd(u, wa_full[l], vec(conv_a_b[l]), wr_blk[l], vec(lru_br[l]), wi_blk[l], vec(lru_bi[l]),
                               vec(lru_lambda[l]), wb_full[l], name=f"mixer_fwd_{l}")
        if l == 0:
            _, landed = _push_wait(wout0, [_gather_lead], y, name="gather_wout0_wait")
            w_out_full[0] = landed[0].reshape(dmix, d)
        h_next = _matmul(y, w_out_full[l], tm=tm, tn=_tile(d, 512), tk=dmix, add=h, name=f"mm_out_{l}")
        if l == 0 and depth > 1:
            _, landed = _push_wait(later, later_wins, h_next, name="gather_later_wait")
            for j in range(1, depth):
                w_in_full[j] = landed[j - 1]
                w_out_full[j] = landed[depth - 1 + j - 1].reshape(dmix, d)
        saved.append((h, hn, u, ca, hs, y))
        h = h_next

    dh, dhb, dg_final, loss_part = _loss_head(h, tgt, final_g, n_meta=n_meta, n_tok=n_tok, name="loss_head")
    loss = lax.psum(loss_part[0, 0], ("x", "y", "c"))

    small_grads = [None] * depth
    sent_out = [None] * depth
    sent_in = [None] * depth
    scatter_in = [_scatter_cols(wcol)]
    token = None
    dg_norms = []
    for l in reversed(range(depth)):
        h_in, hn, u, ca, hs, y = saved[l]
        dy = _matmul(dhb, w_out_full[l], tb=True, tm=tm, tn=_tile(dmix, 512), tk=d, dep=token, name=f"mm_dy_{l}")
        dw_out = _matmul(y, dhb, ta=True, tm=_tile(dmix, 1024), tn=_tile(d, 1024), tk=tm, out_dtype=BF16,
                         name=f"mm_dwout_{l}")
        sent_out[l] = _push_start([dw_out.reshape(N_DEV, wrow, d)], [lax.empty((N_DEV - 1, wrow, d), BF16)],
                                  [_scatter_lead], name=f"scatter_wout_{l}_start")
        du, sg, dwr, dwi = _mixer_bwd(u, ca, hs, dy, wa_full[l], wr_blk[l], vec(lru_br[l]), wi_blk[l], vec(lru_bi[l]),
                                      vec(lru_lambda[l]), _dep(wb_full[l], sent_out[l][-1]), name=f"mixer_bwd_{l}")
        small_grads[l] = (sg, dwr, dwi)
        if l == 0:
            early = _pack([
                jnp.stack([small_grads[j][0][SG_BA] for j in range(depth)]),
                jnp.stack([_blockdiag_extract(small_grads[j][1], hd) for j in range(depth)]),
                jnp.stack([small_grads[j][0][SG_BR] for j in range(depth)]),
                jnp.stack([_blockdiag_extract(small_grads[j][2], hd) for j in range(depth)]),
                jnp.stack([small_grads[j][0][SG_BI] for j in range(depth)]),
                jnp.stack([small_grads[j][0][SG_LAM] for j in range(depth)]),
                jnp.stack([small_grads[j][0][SG_WA:SG_WA + 4] for j in range(depth)]),
                jnp.stack([small_grads[j][0][SG_WB:SG_WB + 3] for j in range(depth)]),
                dg_final[0], *dg_norms])
            early_land = lax.dynamic_update_slice(lax.empty((N_DEV,) + early.shape, F32), early[None], (me, 0, 0))
            sent_early = _push_start([early], [early_land], [_gather_lead], name="gather_early_grads_start")
        dw_in = _matmul(hn, du, ta=True, tm=_tile(d, 1024), tn=_tile(din, 1536), tk=tm, out_dtype=BF16,
                        dep=sent_early[-1] if l == 0 else None, name=f"mm_dwin_{l}")
        sent_in[l] = _push_start([dw_in], [lax.empty((N_DEV - 1, d, wcol), BF16)], scatter_in,
                                 name=f"scatter_win_{l}_start")
        token = sent_in[l][-1]
        dhn = _matmul(du, w_in_full[l], tb=True, tm=tm, tn=_tile(d, 1024), tk=_tile(din, 1536), dep=token,
                      name=f"mm_dhn_{l}")
        dh, dhb, dg_norm = _rms_bwd(h_in, dhn, dh, norm_g[l], name=f"rms_bwd_{l}")
        if l > 0:
            dg_norms.append(dg_norm[0])

    late = _pack([dg_norm[0], dh[:n_meta]])
    late_all = _all_gather([late], [(N_DEV,) + late.shape], [_lead], name="gather_late_grads")[0]
    late_sum = _unpack(_slot_sum(late_all, name="sum_late_grads"), [(d,), (n_meta, d)])
    _, early_all = _push_wait(sent_early, [_gather_lead], late_sum[0], name="gather_early_grads_wait")
    early_shapes = [conv_a_b.shape, lru_wr.shape, lru_br.shape, lru_wi.shape, lru_bi.shape, lru_lambda.shape,
                    (depth, 4, dl), (depth, 3, dl), final_g.shape] + [(d,)] * (depth - 1)
    e = _unpack(_slot_sum(early_all[0], name="sum_early_grads"), early_shapes)
    g_norm = jnp.stack([late_sum[0]] + e[9:][::-1])
    g_meta = lax.dynamic_slice_in_dim(late_sum[1], me * mcol, mcol, axis=1)
    g_wa = lax.dynamic_slice_in_dim(e[6], me * ccol, ccol, axis=2)
    g_wb = lax.dynamic_slice_in_dim(e[7], me * ccol, ccol, axis=2)

    small_w = [norm_g, conv_a_b, lru_wr, lru_br, lru_wi, lru_bi, lru_lambda, final_g, meta, conv_a_w, conv_b_w]
    small_m = [m_norm_g, m_conv_a_b, m_lru_wr, m_lru_br, m_lru_wi, m_lru_bi, m_lru_lambda, m_final_g, m_meta,
               m_conv_a_w, m_conv_b_w]
    small_v = [v_norm_g, v_conv_a_b, v_lru_wr, v_lru_br, v_lru_wi, v_lru_bi, v_lru_lambda, v_final_g, v_meta,
               v_conv_a_w, v_conv_b_w]
    small_g = [g_norm, e[0], e[1], e[2], e[3], e[4], e[5], e[8], g_meta, g_wa, g_wb]
    small_out = _adamw(_pack(small_w), _pack(small_g), _pack(small_m), _pack(small_v), name="adamw_small")
    small_shapes = [a.shape for a in small_w]
    s_grad, s_delta, s_m, s_v = [_unpack(o, small_shapes) for o in small_out]

    win_out = None
    wout_out = None
    after = small_out[0]
    for l in reversed(range(depth)):
        src, landed = _push_wait(sent_out[l], [_scatter_lead], after, name=f"scatter_wout_{l}_wait")
        own = lax.dynamic_index_in_dim(src[0], me, 0, keepdims=False)
        wout_out = _adamw(w_out[l], own, m_w_out[l], v_w_out[l], landed=landed[0], layer=l, depth=depth,
                          into=wout_out, name=f"adamw_w_out_{l}")
        src, landed = _push_wait(sent_in[l], scatter_in, wout_out[0], name=f"scatter_win_{l}_wait")
        own = lax.dynamic_slice_in_dim(src[0], me * wcol, wcol, axis=1)
        win_out = _adamw(w_in[l], own, m_w_in[l], v_w_in[l], landed=landed[0], layer=l, depth=depth,
                         into=win_out, name=f"adamw_w_in_{l}")
        after = win_out[0]

    names = ["norm_g", "conv_a_b", "lru_wr", "lru_br", "lru_wi", "lru_bi", "lru_lambda", "final_g", "meta",
             "conv_a_w", "conv_b_w"]
    order = ["meta", "norm_g", "w_in", "conv_a_w", "conv_a_b", "lru_wr", "lru_br", "lru_wi", "lru_bi", "lru_lambda",
             "conv_b_w", "w_out", "final_g"]

    def family(idx, small):
        table = {nm: small[i] for i, nm in enumerate(names)}
        table["w_in"] = win_out[idx]
        table["w_out"] = wout_out[idx]
        return [table[nm] for nm in order]

    grad_x = dh[n_meta:n_tok][None]
    return (loss, grad_x, *family(0, s_grad), *family(1, s_delta), *family(2, s_m), *family(3, s_v))
```

```python
import functools

import jax
import jax.numpy as jnp
from jax import lax
from jax.experimental import pallas as pl
from jax.experimental.pallas import tpu as pltpu

F32 = jnp.float32
BF16 = jnp.bfloat16
MESH = pl.DeviceIdType.MESH

N_DEV = 8
RMS_EPS = 1e-6
LRU_C = 8.0
ADAM_LR = 0.001
ADAM_B1 = 0.9
ADAM_B2 = 0.999
ADAM_EPS = 1e-08
ADAM_WD = 0.01
ADAM_STEP = 10

V7X_VMEM_LIMIT = 52 * 1024 * 1024
LANES = 128
SUBLANES = 8
TOKEN_TILE = 384
MIX_ROWS = 128
GATE_BLOCK = 256


def _params(sem):
    return pltpu.CompilerParams(dimension_semantics=sem, vmem_limit_bytes=V7X_VMEM_LIMIT)


def _tile(n, target, align=LANES):
    best = None
    for t in range(align, min(n, target) + 1, align):
        if n % t == 0:
            best = t
    return n if best is None else best


def _sigmoid(z):
    return 1.0 / (1.0 + jnp.exp(-z))


def _softplus(z):
    e = jnp.exp(-jnp.abs(z))
    u = 1.0 + e
    l1p = jnp.where(u == 1.0, e, jnp.log(u) * e / jnp.where(u == 1.0, 1.0, u - 1.0))
    return jnp.maximum(z, 0.0) + l1p


def _matmul(a, b, *, ta=False, tb=False, tm, tn, tk, out_dtype=F32, add=None, dep=None, m_part=None, name):
    m, k = (a.shape[1], a.shape[0]) if ta else a.shape
    m_off = 0
    if m_part is not None:
        assert add is None and m % (m_part[1] * tm) == 0
        m //= m_part[1]
        m_off = m_part[0] * (m // tm)
    n = b.shape[0] if tb else b.shape[1]
    assert (b.shape[1] if tb else b.shape[0]) == k
    assert m % tm == 0 and n % tn == 0 and k % tk == 0, (m, n, k, tm, tn, tk)
    nk = k // tk
    a_spec = pl.BlockSpec((tk, tm), lambda i, j, q: (q, i + m_off)) if ta \
        else pl.BlockSpec((tm, tk), lambda i, j, q: (i + m_off, q))
    b_spec = pl.BlockSpec((tn, tk), lambda i, j, q: (j, q)) if tb else pl.BlockSpec((tk, tn), lambda i, j, q: (q, j))
    o_spec = pl.BlockSpec((tm, tn), lambda i, j, q: (i, j))
    dims = (((0 if ta else 1,), (1 if tb else 0,)), ((), ()))
    has_add = add is not None
    has_dep = dep is not None

    def body(*refs):
        if has_dep:
            refs = refs[:-3] + refs[-2:]
        if has_add:
            a_ref, b_ref, add_ref, o_ref, acc_ref = refs
        else:
            a_ref, b_ref, o_ref, acc_ref = refs
        q = pl.program_id(2)
        part = lax.dot_general(a_ref[...], b_ref[...], dims, preferred_element_type=F32)

        def finish(acc):
            if has_add:
                acc = acc + add_ref[...]
            o_ref[...] = acc.astype(out_dtype)

        if nk == 1:
            finish(part)
        else:
            @pl.when(q == 0)
            def _():
                acc_ref[...] = part

            @pl.when(jnp.logical_and(q > 0, q < nk - 1))
            def _():
                acc_ref[...] += part

            @pl.when(q == nk - 1)
            def _():
                finish(acc_ref[...] + part)

    in_specs = [a_spec, b_spec] + ([o_spec] if has_add else [])
    args = (a, b) + ((add,) if has_add else ())
    if has_dep:
        in_specs.append(pl.BlockSpec((SUBLANES, LANES), lambda i, j, q: (0, 0)))
        args += (dep,)
    acc_shape = (tm, tn) if nk > 1 else (SUBLANES, LANES)
    return pl.pallas_call(
        body, name=name,
        grid=(m // tm, n // tn, nk),
        in_specs=in_specs, out_specs=o_spec,
        out_shape=jax.ShapeDtypeStruct((m, n), out_dtype),
        scratch_shapes=[pltpu.VMEM(acc_shape, F32)],
        compiler_params=_params(("parallel", "parallel", "arbitrary")),
    )(*args)


def _rms_fwd(h, g, *, name):
    tp, d = h.shape
    tr = _tile(tp, 512, SUBLANES)

    def body(h_ref, g_ref, o_ref):
        hv = h_ref[...]
        rstd = lax.rsqrt(jnp.mean(hv * hv, axis=-1, keepdims=True) + RMS_EPS)
        o_ref[...] = (hv * rstd * g_ref[...]).astype(BF16)

    return pl.pallas_call(
        body, name=name, grid=(tp // tr,),
        in_specs=[pl.BlockSpec((tr, d), lambda i: (i, 0)), pl.BlockSpec((1, d), lambda i: (0, 0))],
        out_specs=pl.BlockSpec((tr, d), lambda i: (i, 0)),
        out_shape=jax.ShapeDtypeStruct((tp, d), BF16),
        compiler_params=_params(("parallel",)),
    )(h, g.reshape(1, d))


def _rms_bwd(h, dhn, dout, g, *, name):
    tp, d = h.shape
    tr = _tile(tp, 384, SUBLANES)

    def body(h_ref, dhn_ref, dout_ref, g_ref, dh_ref, dhb_ref, dg_ref):
        hv = h_ref[...]
        rstd = lax.rsqrt(jnp.mean(hv * hv, axis=-1, keepdims=True) + RMS_EPS)
        xhat = hv * rstd
        dn = dhn_ref[...]
        dxhat = dn * g_ref[...]
        dh = dout_ref[...] + rstd * (dxhat - xhat * jnp.mean(dxhat * xhat, axis=-1, keepdims=True))
        dh_ref[...] = dh
        dhb_ref[...] = dh.astype(BF16)
        part = jnp.sum(dn * xhat, axis=0, keepdims=True)

        @pl.when(pl.program_id(0) == 0)
        def _():
            dg_ref[...] = part

        @pl.when(pl.program_id(0) > 0)
        def _():
            dg_ref[...] += part

    row = pl.BlockSpec((tr, d), lambda i: (i, 0))
    vec = pl.BlockSpec((1, d), lambda i: (0, 0))
    return pl.pallas_call(
        body, name=name, grid=(tp // tr,),
        in_specs=[row, row, row, vec],
        out_specs=[row, row, vec],
        out_shape=[jax.ShapeDtypeStruct((tp, d), F32), jax.ShapeDtypeStruct((tp, d), BF16),
                   jax.ShapeDtypeStruct((1, d), F32)],
        compiler_params=_params(("arbitrary",)),
    )(h, dhn, dout, g.reshape(1, d))


def _loss_head(h, tgt, g, *, n_meta, n_tok, name):
    tp, d = h.shape
    tr = _tile(tp, 384, SUBLANES)

    def body(h_ref, t_ref, g_ref, dh_ref, dhb_ref, dg_ref, loss_ref):
        i = pl.program_id(0)
        hv = h_ref[...]
        rstd = lax.rsqrt(jnp.mean(hv * hv, axis=-1, keepdims=True) + RMS_EPS)
        xhat = hv * rstd
        gv = g_ref[...]
        rows = i * tr + lax.broadcasted_iota(jnp.int32, (tr, 1), 0)
        valid = jnp.logical_and(rows >= n_meta, rows < n_tok)
        err = jnp.where(valid, xhat * gv - t_ref[...], 0.0)
        dy = err * (1.0 / d)
        dxhat = dy * gv
        dh = rstd * (dxhat - xhat * jnp.mean(dxhat * xhat, axis=-1, keepdims=True))
        dh_ref[...] = dh
        dhb_ref[...] = dh.astype(BF16)
        dg_part = jnp.sum(dy * xhat, axis=0, keepdims=True)
        per_row = jnp.sum(err * err, axis=-1, keepdims=True) * (1.0 / d)
        loss_part = jnp.broadcast_to(0.5 * jnp.sum(per_row, axis=0, keepdims=True), (SUBLANES, LANES))

        @pl.when(i == 0)
        def _():
            dg_ref[...] = dg_part
            loss_ref[...] = loss_part

        @pl.when(i > 0)
        def _():
            dg_ref[...] += dg_part
            loss_ref[...] += loss_part

    row = pl.BlockSpec((tr, d), lambda i: (i, 0))
    vec = pl.BlockSpec((1, d), lambda i: (0, 0))
    return pl.pallas_call(
        body, name=name, grid=(tp // tr,),
        in_specs=[row, row, vec],
        out_specs=[row, row, vec, pl.BlockSpec((SUBLANES, LANES), lambda i: (0, 0))],
        out_shape=[jax.ShapeDtypeStruct((tp, d), F32), jax.ShapeDtypeStruct((tp, d), BF16),
                   jax.ShapeDtypeStruct((1, d), F32), jax.ShapeDtypeStruct((SUBLANES, LANES), F32)],
        compiler_params=_params(("arbitrary",)),
    )(h, tgt, g.reshape(1, d))


def _shift_down(halo, tile, s):
    if s == 0:
        return tile
    ext = jnp.concatenate([halo, tile], axis=0)
    return pltpu.roll(ext, s, 0)[SUBLANES:]


def _shift_up(tile, head, s):
    if s == 0:
        return tile
    ext = jnp.concatenate([tile, head], axis=0)
    n = ext.shape[0]
    return pltpu.roll(ext, n - s, 0)[: tile.shape[0]]


def _scan_rows_fwd(a, b):
    row = lax.broadcasted_iota(jnp.int32, a.shape, 0)
    for s in (1, 2, 4):
        a_sh = pltpu.roll(a, s, 0)
        b_sh = pltpu.roll(b, s, 0)
        m = row >= s
        b = jnp.where(m, a * b_sh + b, b)
        a = jnp.where(m, a * a_sh, a)
    return a, b


def _scan_rows_bwd(c, d):
    row = lax.broadcasted_iota(jnp.int32, c.shape, 0)
    for s in (1, 2, 4):
        c_sh = pltpu.roll(c, SUBLANES - s, 0)
        d_sh = pltpu.roll(d, SUBLANES - s, 0)
        m = row < SUBLANES - s
        d = jnp.where(m, c * d_sh + d, d)
        c = jnp.where(m, c * c_sh, c)
    return c, d


def _gates(ca, wr, wi, br, bi, sp):
    cab = ca.astype(BF16)
    r = _sigmoid(jnp.dot(cab, wr, preferred_element_type=F32) + br)
    ig = _sigmoid(jnp.dot(cab, wi, preferred_element_type=F32) + bi)
    la = -LRU_C * r * sp
    a = jnp.exp(la)
    mult = jnp.sqrt(-jnp.tanh(la) * (a * a + 1.0))
    return r, ig, a, mult


def _mixer_fwd(u, wa, ba, wr_blk, br, wi_blk, bi, lam, wb, *, name):
    tp, din = u.shape
    dl = din // 6
    tt = MIX_ROWS
    cw = GATE_BLOCK
    nch = dl // cw
    assert tp % tt == 0 and dl % cw == 0

    def body(u_ref, wa_ref, ba_ref, wr_ref, br_ref, wi_ref, bi_ref, lam_ref, wb_ref,
             ca_ref, hs_ref, y_ref, xa_tail, v_tail, h_carry, a_s, b_s):
        @pl.when(pl.program_id(0) == 0)
        def _():
            xa_tail[...] = jnp.zeros_like(xa_tail)
            v_tail[...] = jnp.zeros_like(v_tail)
            h_carry[...] = jnp.zeros_like(h_carry)

        for ch in range(nch):
            cs = slice(ch * cw, (ch + 1) * cw)

            def seg(s):
                return slice(s * dl + ch * cw, s * dl + (ch + 1) * cw)

            xa = u_ref[:, seg(0)]
            halo = xa_tail[:, cs]
            ca = ba_ref[:, cs] + wa_ref[3:4, cs] * xa
            for kk in range(3):
                ca = ca + wa_ref[kk:kk + 1, cs] * _shift_down(halo, xa, 3 - kk)
            xa_tail[:, cs] = xa[tt - SUBLANES:]
            ca_ref[:, cs] = ca
            sp = _softplus(-lam_ref[:, cs])
            _, ig, a, mult = _gates(ca, wr_ref[ch], wi_ref[ch], br_ref[:, cs], bi_ref[:, cs], sp)
            a_s[:, cs] = a
            b_s[:, cs] = mult * (ig * ca)

            bv = u_ref[:, seg(2)]
            v = u_ref[:, seg(3)] * u_ref[:, seg(4)]
            gb = u_ref[:, seg(5)]
            vh = v_tail[:, cs]
            cb = wb_ref[2:3, cs] * v
            for kk in range(2):
                cb = cb + wb_ref[kk:kk + 1, cs] * _shift_down(vh, v, 2 - kk)
            v_tail[:, cs] = v[tt - SUBLANES:]
            y_ref[:, dl + ch * cw: dl + (ch + 1) * cw] = (bv * cb * (gb * _sigmoid(gb))).astype(BF16)

        def group(gi, hprev):
            rows = pl.ds(pl.multiple_of(gi * SUBLANES, SUBLANES), SUBLANES)
            a8, b8 = _scan_rows_fwd(a_s[rows, :], b_s[rows, :])
            h8 = b8 + a8 * hprev
            hs_ref[rows, :] = h8
            return jnp.broadcast_to(h8[SUBLANES - 1:SUBLANES, :], h8.shape)

        h_carry[...] = lax.fori_loop(0, tt // SUBLANES, group, h_carry[...])

        for ch in range(nch):
            cs = slice(ch * cw, (ch + 1) * cw)
            ga = u_ref[:, dl + ch * cw: dl + (ch + 1) * cw]
            y_ref[:, cs] = (hs_ref[:, cs] * (ga * _sigmoid(ga))).astype(BF16)

    row = lambda w: pl.BlockSpec((tt, w), lambda i: (i, 0))
    full = lambda shp: pl.BlockSpec(shp, lambda i: tuple(0 for _ in shp))
    return pl.pallas_call(
        body, name=name, grid=(tp // tt,),
        in_specs=[row(din), full((4, dl)), full((1, dl)), full((nch, cw, cw)), full((1, dl)),
                  full((nch, cw, cw)), full((1, dl)), full((1, dl)), full((3, dl))],
        out_specs=[row(dl), row(dl), row(2 * dl)],
        out_shape=[jax.ShapeDtypeStruct((tp, dl), F32), jax.ShapeDtypeStruct((tp, dl), F32),
                   jax.ShapeDtypeStruct((tp, 2 * dl), BF16)],
        scratch_shapes=[pltpu.VMEM((SUBLANES, dl), F32), pltpu.VMEM((SUBLANES, dl), F32),
                        pltpu.VMEM((SUBLANES, dl), F32), pltpu.VMEM((tt, dl), F32), pltpu.VMEM((tt, dl), F32)],
        compiler_params=_params(("arbitrary",)),
    )(u, wa, ba, wr_blk, br, wi_blk, bi, lam, wb)


SG_WA, SG_BA, SG_BR, SG_BI, SG_LAM, SG_WB, SG_ROWS = 0, 4, 5, 6, 7, 8, 16


def _mixer_bwd(u, ca, hs, dy, wa, wr_blk, br, wi_blk, bi, lam, wb, *, name):
    tp, din = u.shape
    dl = din // 6
    tt = MIX_ROWS
    cw = GATE_BLOCK
    nch = dl // cw
    nt = tp // tt
    hb = tt // SUBLANES
    tn_dims = (((0,), (0,)), ((), ()))
    nt_dims = (((1,), (1,)), ((), ()))

    def body(u_ref, uh_ref, ca_ref, hs_ref, hsh_ref, dy_ref, wa_ref, wr_ref, br_ref, wi_ref, bi_ref, lam_ref, wb_ref,
             du_ref, sg_ref, dwr_ref, dwi_ref,
             g_carry, a_head, dca_head, dcb_head, r_s, i_s, a_s, an_s, d_s, g_s):
        i = pl.program_id(0)
        first_tile = i == nt - 1

        @pl.when(i == 0)
        def _():
            for ref in (g_carry, a_head, dca_head, dcb_head, sg_ref, dwr_ref, dwi_ref):
                ref[...] = jnp.zeros_like(ref)

        def halo_of(x):
            return jnp.where(first_tile, 0.0, x)

        for ch in range(nch):
            cs = slice(ch * cw, (ch + 1) * cw)
            cav = ca_ref[:, cs]
            sp = _softplus(-lam_ref[:, cs])
            r, ig, a, _ = _gates(cav, wr_ref[ch], wi_ref[ch], br_ref[:, cs], bi_ref[:, cs], sp)
            r_s[:, cs] = r
            i_s[:, cs] = ig
            a_s[:, cs] = a
            an_s[:, cs] = _shift_up(a, a_head[:, cs], 1)
            a_head[:, cs] = a[:SUBLANES]
            ga = u_ref[:, dl + ch * cw: dl + (ch + 1) * cw]
            d_s[:, cs] = dy_ref[:, cs] * (ga * _sigmoid(ga))

        def group(k, gnext):
            gi = tt // SUBLANES - 1 - k
            rows = pl.ds(pl.multiple_of(gi * SUBLANES, SUBLANES), SUBLANES)
            c8, d8 = _scan_rows_bwd(an_s[rows, :], d_s[rows, :])
            g8 = d8 + c8 * gnext
            g_s[rows, :] = g8
            return jnp.broadcast_to(g8[0:1, :], g8.shape)

        g_carry[...] = lax.fori_loop(0, tt // SUBLANES, group, g_carry[...])

        def acc_row(r0, val):
            sg_ref[r0:r0 + 1, cs_cur[0]] += jnp.sum(val, axis=0, keepdims=True)

        cs_cur = [None]
        for ch in range(nch):
            cs = slice(ch * cw, (ch + 1) * cw)
            cs_cur[0] = cs

            def seg(s):
                return slice(s * dl + ch * cw, s * dl + (ch + 1) * cw)

            cav = ca_ref[:, cs]
            r = r_s[:, cs]
            ig = i_s[:, cs]
            a = a_s[:, cs]
            g = g_s[:, cs]
            hsv = hs_ref[:, cs]
            lamv = lam_ref[:, cs]
            sp = _softplus(-lamv)
            la = -LRU_C * r * sp
            e2 = a * a
            mult = jnp.sqrt(-jnp.tanh(la) * (e2 + 1.0))
            hprev = _shift_down(halo_of(hsh_ref[:, cs]), hsv, 1)
            dla = g * hprev * a - g * (ig * cav) * e2 / mult
            gm = g * mult
            dzi = gm * cav * ig * (1.0 - ig)
            dca = gm * ig
            dzr = dla * (-LRU_C * sp) * r * (1.0 - r)
            acc_row(SG_LAM, dla * (-LRU_C * r) * (-_sigmoid(-lamv)))
            acc_row(SG_BR, dzr)
            acc_row(SG_BI, dzi)
            dzr_b = dzr.astype(BF16)
            dzi_b = dzi.astype(BF16)
            cab = cav.astype(BF16)
            dca = dca + lax.dot_general(dzr_b, wr_ref[ch], nt_dims, preferred_element_type=F32)
            dca = dca + lax.dot_general(dzi_b, wi_ref[ch], nt_dims, preferred_element_type=F32)
            dwr_ref[ch] += lax.dot_general(cab, dzr_b, tn_dims, preferred_element_type=F32)
            dwi_ref[ch] += lax.dot_general(cab, dzi_b, tn_dims, preferred_element_type=F32)
            acc_row(SG_BA, dca)
            xa = u_ref[:, seg(0)]
            xah = halo_of(uh_ref[:, seg(0)])
            head = dca_head[:, cs]
            dxa = wa_ref[3:4, cs] * dca
            acc_row(SG_WA + 3, dca * xa)
            for kk in range(3):
                acc_row(SG_WA + kk, dca * _shift_down(xah, xa, 3 - kk))
                dxa = dxa + wa_ref[kk:kk + 1, cs] * _shift_up(dca, head, 3 - kk)
            dca_head[:, cs] = dca[:SUBLANES]
            ga = u_ref[:, seg(1)]
            sga = _sigmoid(ga)
            dga = dy_ref[:, cs] * hsv * (sga * (1.0 + ga * (1.0 - sga)))
            du_ref[:, seg(0)] = dxa.astype(BF16)
            du_ref[:, seg(1)] = dga.astype(BF16)

            bv = u_ref[:, seg(2)]
            cv = u_ref[:, seg(3)]
            xb = u_ref[:, seg(4)]
            gb = u_ref[:, seg(5)]
            dyb = dy_ref[:, dl + ch * cw: dl + (ch + 1) * cw]
            v = cv * xb
            vh = halo_of(uh_ref[:, seg(3)] * uh_ref[:, seg(4)])
            v1 = _shift_down(vh, v, 1)
            v2 = _shift_down(vh, v, 2)
            cb = wb_ref[2:3, cs] * v + wb_ref[1:2, cs] * v1 + wb_ref[0:1, cs] * v2
            sgb = _sigmoid(gb)
            sl = gb * sgb
            dcb = dyb * bv * sl
            du_ref[:, seg(2)] = (dyb * cb * sl).astype(BF16)
            du_ref[:, seg(5)] = (dyb * bv * cb * (sgb * (1.0 + gb * (1.0 - sgb)))).astype(BF16)
            acc_row(SG_WB + 2, dcb * v)
            acc_row(SG_WB + 1, dcb * v1)
            acc_row(SG_WB + 0, dcb * v2)
            bhead = dcb_head[:, cs]
            dv = wb_ref[2:3, cs] * dcb + wb_ref[1:2, cs] * _shift_up(dcb, bhead, 1) \
                + wb_ref[0:1, cs] * _shift_up(dcb, bhead, 2)
            dcb_head[:, cs] = dcb[:SUBLANES]
            du_ref[:, seg(3)] = (dv * xb).astype(BF16)
            du_ref[:, seg(4)] = (dv * cv).astype(BF16)

    rev = lambda w: pl.BlockSpec((tt, w), lambda i: (nt - 1 - i, 0))
    halo = lambda w: pl.BlockSpec((SUBLANES, w), lambda i: (jnp.maximum((nt - 1 - i) * hb - 1, 0), 0))
    full = lambda shp: pl.BlockSpec(shp, lambda i: tuple(0 for _ in shp))
    vm = lambda r: pltpu.VMEM((r, dl), F32)
    return pl.pallas_call(
        body, name=name, grid=(nt,),
        in_specs=[rev(din), halo(din), rev(dl), rev(dl), halo(dl), rev(2 * dl), full((4, dl)),
                  full((nch, cw, cw)), full((1, dl)), full((nch, cw, cw)), full((1, dl)), full((1, dl)), full((3, dl))],
        out_specs=[rev(din), full((SG_ROWS, dl)), full((nch, cw, cw)), full((nch, cw, cw))],
        out_shape=[jax.ShapeDtypeStruct((tp, din), BF16), jax.ShapeDtypeStruct((SG_ROWS, dl), F32),
                   jax.ShapeDtypeStruct((nch, cw, cw), F32), jax.ShapeDtypeStruct((nch, cw, cw), F32)],
        scratch_shapes=[vm(SUBLANES), vm(SUBLANES), vm(SUBLANES), vm(SUBLANES),
                        vm(tt), vm(tt), vm(tt), vm(tt), vm(tt), vm(tt)],
        compiler_params=_params(("arbitrary",)),
    )(u, u, ca, hs, hs, dy, wa, wr_blk, br, wi_blk, bi, lam, wb)


def _adamw(w, g, m, v, *, name, landed=None, layer=None, depth=None, into=None, row_off=0):
    r, c = w.shape
    rows = g.shape[0]
    tr = _tile(rows, 256, 2 * SUBLANES)
    assert row_off % tr == 0
    boff = row_off // tr
    bc1 = 1.0 - ADAM_B1 ** ADAM_STEP
    bc2 = 1.0 - ADAM_B2 ** ADAM_STEP
    slots = landed is not None

    def body(*refs):
        if into is not None:
            refs = refs[:-8] + refs[-4:]
        if slots:
            w_ref, g_ref, l_ref, m_ref, v_ref, grad_ref, delta_ref, nm_ref, nv_ref = refs
            gv = g_ref[...].astype(F32)
            for s in range(N_DEV - 1):
                gv = gv + l_ref[s].astype(F32)
        else:
            w_ref, g_ref, m_ref, v_ref, grad_ref, delta_ref, nm_ref, nv_ref = refs
            gv = g_ref[...]
        wv = w_ref[...]
        mn = ADAM_B1 * m_ref[...] + (1.0 - ADAM_B1) * gv
        vn = ADAM_B2 * v_ref[...] + (1.0 - ADAM_B2) * (gv * gv)
        m_hat = mn / bc1
        v_hat = vn / bc2
        grad_ref[...] = gv
        delta_ref[...] = -ADAM_LR * (m_hat / (jnp.sqrt(v_hat) + ADAM_EPS) + ADAM_WD * wv)
        nm_ref[...] = mn
        nv_ref[...] = vn

    blk = pl.BlockSpec((tr, c), lambda i: (i + boff, 0))
    g_blk = pl.BlockSpec((tr, c), lambda i: (i, 0))
    l_spec = [pl.BlockSpec((N_DEV - 1, tr, c), lambda i: (0, i, 0))] if slots else []
    args = (w, g, landed, m, v) if slots else (w, g, m, v)
    in_specs = [blk, g_blk] + l_spec + [blk, blk]
    if depth is None:
        shp = jax.ShapeDtypeStruct((r, c), F32)
        out_blk = blk
    else:
        shp = jax.ShapeDtypeStruct((depth, r, c), F32)
        out_blk = pl.BlockSpec((None, tr, c), lambda i: (layer, i + boff, 0))
    aliases = {}
    if into is not None:
        aliases = {len(args) + j: j for j in range(4)}
        in_specs = in_specs + [ANY] * 4
        args = args + tuple(into)
    return pl.pallas_call(
        body, name=name, grid=(rows // tr,),
        in_specs=in_specs, out_specs=[out_blk] * 4,
        out_shape=[shp] * 4, input_output_aliases=aliases,
        compiler_params=_params(("parallel",)),
    )(*args)


def _slot_sum(g, *, name):
    _, r, c = g.shape
    tr = _tile(r, 512, SUBLANES)

    def body(g_ref, o_ref):
        gv = g_ref[0]
        for s in range(1, N_DEV):
            gv = gv + g_ref[s]
        o_ref[...] = gv

    return pl.pallas_call(
        body, name=name, grid=(r // tr,),
        in_specs=[pl.BlockSpec((N_DEV, tr, c), lambda i: (0, i, 0))],
        out_specs=pl.BlockSpec((tr, c), lambda i: (i, 0)),
        out_shape=jax.ShapeDtypeStruct((r, c), F32),
        compiler_params=_params(("parallel",)),
    )(g)


def _mesh_pos():
    x, y, c = lax.axis_index("x"), lax.axis_index("y"), lax.axis_index("c")
    return x, y, c, 4 * x + 2 * y + c


ANY = pl.BlockSpec(memory_space=pl.ANY)


def _all_gather(srcs, out_shapes, views, *, name, place=()):
    n = len(srcs)
    npl = len(place)

    def body(*refs):
        src = refs[:n]
        psrc = refs[n:n + npl]
        dst = refs[n + npl:2 * n + npl]
        pdst = refs[2 * n + npl:2 * (n + npl)]
        send_sems, recv_sems, local_sems = refs[2 * (n + npl):]
        x, y, c, me = _mesh_pos()
        sibling = (x, y, 1 - c)
        chips = [(1 - x, y), (x, 1 - y), (1 - x, 1 - y)]

        def dev(px, py, pc):
            return 4 * px + 2 * py + pc

        def copy(a, k, block, to, from_src=False):
            win = views[a](dst[a], dev(*block))
            return pltpu.make_async_remote_copy(
                src_ref=src[a] if from_src else win, dst_ref=win,
                send_sem=send_sems.at[a * 7 + k], recv_sem=recv_sems.at[a * 7 + k],
                device_id=to, device_id_type=MESH)

        mine = [pltpu.make_async_copy(src[a], views[a](dst[a], me), local_sems.at[a]) for a in range(n)]
        mine += [pltpu.make_async_copy(psrc[j], place[j][2](pdst[j], me), local_sems.at[n + j]) for j in range(npl)]
        for cp in mine[n:]:
            cp.start()
        started = []
        for a in range(n):
            mine[a].start()
            first = [copy(a, 0, (x, y, c), sibling, True)]
            first += [copy(a, 1 + j, (x, y, c), (*chip, c), True) for j, chip in enumerate(chips)]
            for cp in first:
                cp.start()
            started += first
        for a in range(n):
            for j, chip in enumerate(chips):
                copy(a, 1 + j, (*chip, c), (x, y, c)).wait_recv()
                fwd = copy(a, 4 + j, (*chip, c), sibling)
                fwd.start()
                started.append(fwd)
        for a in range(n):
            copy(a, 0, (x, y, 1 - c), (x, y, c)).wait_recv()
            for j, chip in enumerate(chips):
                copy(a, 4 + j, (*chip, 1 - c), (x, y, c)).wait_recv()
        for cp in started:
            cp.wait_send()
        for cp in mine:
            cp.wait()

    return pl.pallas_call(
        body, name=name,
        in_specs=[ANY] * (n + npl), out_specs=[ANY] * (n + npl),
        out_shape=[jax.ShapeDtypeStruct(s, x.dtype) for s, x in zip(out_shapes, srcs)]
        + [jax.ShapeDtypeStruct(shape, arr.dtype) for arr, shape, _ in place],
        scratch_shapes=[pltpu.SemaphoreType.DMA((7 * n,)), pltpu.SemaphoreType.DMA((7 * n,)),
                        pltpu.SemaphoreType.DMA((n + npl,))],
    )(*srcs, *[arr for arr, _, _ in place])


HBM = pl.BlockSpec(memory_space=pltpu.HBM)
SEM = pl.BlockSpec(memory_space=pltpu.SEMAPHORE)
EFFECT = pltpu.SideEffectType.DATAFLOW_SIDE_EFFECTING


def _peer_copies(n, wins, src, land, send_sems, recv_sems):
    x, y, c, me = _mesh_pos()
    out = []
    for a in range(n):
        for k in range(1, N_DEV):
            px = 1 - x if k & 4 else x
            py = 1 - y if k & 2 else y
            pc = 1 - c if k & 1 else c
            s_win, d_win = wins[a](src[a], land[a], me, 4 * px + 2 * py + pc, k)
            out.append(pltpu.make_async_remote_copy(
                src_ref=s_win, dst_ref=d_win,
                send_sem=send_sems.at[a * 7 + k - 1], recv_sem=recv_sems.at[a * 7 + k - 1],
                device_id=(px, py, pc), device_id_type=MESH))
    return out


def _push_start(srcs, lands, wins, *, name):
    n = len(srcs)

    def body(*refs):
        src = refs[:n]
        land = refs[n:2 * n]
        send_sems, recv_sems = refs[2 * n], refs[2 * n + 1]
        token = refs[-1]
        for cp in _peer_copies(n, wins, src, land, send_sems, recv_sems):
            cp.start()
        token[...] = jnp.zeros_like(token)

    bufs = (*srcs, *lands)
    return pl.pallas_call(
        body, name=name,
        out_shape=(pltpu.SemaphoreType.DMA((7 * n,)), pltpu.SemaphoreType.DMA((7 * n,)),
                   *[pltpu.HBM(v.shape, v.dtype) for v in bufs], jax.ShapeDtypeStruct((SUBLANES, LANES), F32)),
        in_specs=[HBM] * (2 * n),
        out_specs=(SEM, SEM, *[HBM] * (2 * n), pl.BlockSpec(memory_space=pltpu.VMEM)),
        input_output_aliases={i: 2 + i for i in range(2 * n)},
        compiler_params=pltpu.CompilerParams(has_side_effects=EFFECT),
    )(*[pltpu.with_memory_space_constraint(v, pltpu.HBM) for v in bufs])


def _push_wait(handle, wins, after, *, name):
    send_sems, recv_sems, *bufs, _ = handle
    n = len(bufs) // 2

    def body(*refs):
        src = refs[:n]
        land = refs[n:2 * n]
        for cp in _peer_copies(n, wins, src, land, refs[2 * n], refs[2 * n + 1]):
            cp.wait_send()
            cp.wait_recv()

    outs = pl.pallas_call(
        body, name=name,
        out_shape=tuple(pltpu.HBM(v.shape, v.dtype) for v in bufs),
        in_specs=[HBM] * (2 * n) + [SEM, SEM, ANY],
        out_specs=tuple([HBM] * (2 * n)),
        input_output_aliases={i: i for i in range(2 * n)},
        compiler_params=pltpu.CompilerParams(has_side_effects=EFFECT),
    )(*bufs, send_sems, recv_sems, after)
    return outs[:n], outs[n:]


def _gather_lead(src, land, me, peer, k):
    return src, land.at[me]


def _gather_cols(width):
    def win(src, land, me, peer, k):
        return src, land.at[:, pl.ds(me * width, width)]
    return win


def _scatter_lead(src, land, me, peer, k):
    return src.at[peer], land.at[k - 1]


def _scatter_cols(width):
    def win(src, land, me, peer, k):
        return src.at[:, pl.ds(peer * width, width)], land.at[k - 1]
    return win


def _dep(x, token):
    return x + token[0, 0].astype(x.dtype)


def _lead(ref, d):
    return ref.at[d]


def _col_window(width):
    def view(ref, d):
        return ref.at[:, pl.ds(d * width, width)]
    return view


def _pack(arrs):
    flat = jnp.concatenate([a.reshape(-1).astype(F32) for a in arrs])
    n = flat.shape[0]
    rows = -(-n // (SUBLANES * LANES)) * SUBLANES
    return jnp.pad(flat, (0, rows * LANES - n)).reshape(rows, LANES)


def _unpack(buf, shapes):
    flat = buf.reshape(-1)
    out, off = [], 0
    for s in shapes:
        n = 1
        for q in s:
            n *= q
        out.append(flat[off:off + n].reshape(s))
        off += n
    return out


def _blockdiag(w, cw):
    h, hd, _ = w.shape
    per = cw // hd
    wg = w.reshape(h // per, per, hd, hd)
    eye = jnp.eye(per, dtype=w.dtype)
    blk = jnp.einsum("gpij,pq->gpiqj", wg, eye)
    return blk.reshape(h // per, cw, cw).astype(BF16)


def _blockdiag_extract(g, hd):
    n, cw, _ = g.shape
    per = cw // hd
    g5 = g.reshape(n, per, hd, per, hd)
    idx = jnp.arange(per)
    return g5[:, idx, :, idx, :].transpose(1, 0, 2, 3).reshape(n * per, hd, hd)


def kernel(x, meta, norm_g, w_in, conv_a_w, conv_a_b, lru_wr, lru_br, lru_wi, lru_bi, lru_lambda, conv_b_w, w_out, final_g, loss_target, m_meta, m_norm_g, m_w_in, m_conv_a_w, m_conv_a_b, m_lru_wr, m_lru_br, m_lru_wi, m_lru_bi, m_lru_lambda, m_conv_b_w, m_w_out, m_final_g, v_meta, v_norm_g, v_w_in, v_conv_a_w, v_conv_a_b, v_lru_wr, v_lru_br, v_lru_wi, v_lru_bi, v_lru_lambda, v_conv_b_w, v_w_out, v_final_g):
    _, seq, d = x.shape
    n_meta = meta.shape[0]
    depth = w_in.shape[0]
    din = w_in.shape[2] * N_DEV
    dl = din // 6
    dmix = 2 * dl
    wcol = w_in.shape[2]
    wrow = w_out.shape[1]
    mcol = meta.shape[1]
    ccol = conv_a_w.shape[2]
    heads, hd = lru_wr.shape[1], lru_wr.shape[2]
    n_tok = n_meta + seq
    tp = -(-n_tok // TOKEN_TILE) * TOKEN_TILE
    me = 4 * lax.axis_index("x") + 2 * lax.axis_index("y") + lax.axis_index("c")

    bf = lambda a: a.astype(BF16)
    small_mine = _pack([meta, conv_a_w, conv_b_w])
    w_out_b = [bf(w_out[l]) for l in range(depth)]
    w_in_b = [bf(w_in[l]) for l in range(depth)]
    place = [(s, (N_DEV, wrow, d), _lead) for s in w_out_b] + [(s, (d, din), _col_window(wcol)) for s in w_in_b[1:]]
    first = _all_gather([w_in_b[0], small_mine], [(d, din), (N_DEV,) + small_mine.shape],
                        [_col_window(wcol), _lead], name="gather_first", place=place)
    w_in_full = [first[0]] + [None] * (depth - 1)
    w_out_full = [None] * depth
    land_out = first[2:2 + depth]
    land_in = [None] + list(first[2 + depth:])
    parts = [_unpack(first[1][s], [meta.shape, conv_a_w.shape, conv_b_w.shape]) for s in range(N_DEV)]
    meta_full = jnp.concatenate([p[0] for p in parts], axis=1)
    wa_full = jnp.concatenate([p[1] for p in parts], axis=2)
    wb_full = jnp.concatenate([p[2] for p in parts], axis=2)

    push_out = [None] * depth
    push_in = [None] * depth
    push_out[0] = _push_start([w_out_b[0]], [land_out[0]], [_gather_lead], name="gather_wout_0_start")
    token = push_out[0][-1]
    for l in range(1, depth):
        push_in[l] = _push_start([_dep(w_in_b[l], token)], [land_in[l]], [_gather_cols(wcol)],
                                 name=f"gather_win_{l}_start")
        push_out[l] = _push_start([_dep(w_out_b[l], push_in[l][-1])], [land_out[l]], [_gather_lead],
                                  name=f"gather_wout_{l}_start")
        token = push_out[l][-1]

    wr_blk = [_blockdiag(lru_wr[l], GATE_BLOCK) for l in range(depth)]
    wi_blk = [_blockdiag(lru_wi[l], GATE_BLOCK) for l in range(depth)]
    vec = lambda a: a.reshape(1, dl)

    h = jnp.concatenate([meta_full, x[0], jnp.zeros((tp - n_tok, d), F32)], axis=0)
    tgt = jnp.pad(loss_target[0], ((n_meta, tp - n_tok), (0, 0)))
    tm = _tile(tp, 1408)
    saved = []
    for l in range(depth):
        gain = _dep(norm_g[l], token) if l == 0 else norm_g[l]
        hn = _rms_fwd(h, gain, name=f"rms_fwd_{l}")
        if l > 0:
            _, landed = _push_wait(push_in[l], [_gather_cols(wcol)], hn, name=f"gather_win_{l}_wait")
            w_in_full[l] = landed[0]
        u = _matmul(hn, w_in_full[l], tm=tm, tn=_tile(din, 768), tk=d, name=f"mm_u_{l}")
        ca, hs, y = _mixer_fwd(u, wa_full[l], vec(conv_a_b[l]), wr_blk[l], vec(lru_br[l]), wi_blk[l], vec(lru_bi[l]),
                               vec(lru_lambda[l]), wb_full[l], name=f"mixer_fwd_{l}")
        _, landed = _push_wait(push_out[l], [_gather_lead], y, name=f"gather_wout_{l}_wait")
        w_out_full[l] = landed[0].reshape(dmix, d)
        h_next = _matmul(y, w_out_full[l], tm=tm, tn=_tile(d, 512), tk=dmix, add=h, name=f"mm_out_{l}")
        saved.append((h, hn, u, ca, hs, y))
        h = h_next

    dh, dhb, dg_final, loss_part = _loss_head(h, tgt, final_g, n_meta=n_meta, n_tok=n_tok, name="loss_head")
    loss = lax.psum(loss_part[0, 0], ("x", "y", "c"))

    small_grads = [None] * depth
    sent_out = [None] * depth
    sent_in = [None] * depth
    scatter_in = [_scatter_cols(wcol)]
    token = None
    dg_norms = []
    for l in reversed(range(depth)):
        h_in, hn, u, ca, hs, y = saved[l]
        dy = _matmul(dhb, w_out_full[l], tb=True, tm=tm, tn=_tile(dmix, 512), tk=d, dep=token, name=f"mm_dy_{l}")
        dw_out = _matmul(y, dhb, ta=True, tm=_tile(dmix, 1024), tn=_tile(d, 1024), tk=tm, out_dtype=BF16,
                         name=f"mm_dwout_{l}")
        sent_out[l] = _push_start([dw_out.reshape(N_DEV, wrow, d)], [lax.empty((N_DEV - 1, wrow, d), BF16)],
                                  [_scatter_lead], name=f"scatter_wout_{l}_start")
        du, sg, dwr, dwi = _mixer_bwd(u, ca, hs, dy, wa_full[l], wr_blk[l], vec(lru_br[l]), wi_blk[l], vec(lru_bi[l]),
                                      vec(lru_lambda[l]), _dep(wb_full[l], sent_out[l][-1]), name=f"mixer_bwd_{l}")
        small_grads[l] = (sg, dwr, dwi)
        if l == 0:
            early = _pack([
                jnp.stack([small_grads[j][0][SG_BA] for j in range(depth)]),
                jnp.stack([_blockdiag_extract(small_grads[j][1], hd) for j in range(depth)]),
                jnp.stack([small_grads[j][0][SG_BR] for j in range(depth)]),
                jnp.stack([_blockdiag_extract(small_grads[j][2], hd) for j in range(depth)]),
                jnp.stack([small_grads[j][0][SG_BI] for j in range(depth)]),
                jnp.stack([small_grads[j][0][SG_LAM] for j in range(depth)]),
                jnp.stack([small_grads[j][0][SG_WA:SG_WA + 4] for j in range(depth)]),
                jnp.stack([small_grads[j][0][SG_WB:SG_WB + 3] for j in range(depth)]),
                dg_final[0], *dg_norms])
            early_land = lax.dynamic_update_slice(lax.empty((N_DEV,) + early.shape, F32), early[None], (me, 0, 0))
            sent_early = _push_start([early], [early_land], [_gather_lead], name="gather_early_grads_start")
        parts = 2 if l == 0 else 1
        token = sent_early[-1] if l == 0 else None
        sent_in[l] = []
        for p in range(parts):
            dw_in = _matmul(hn, du, ta=True, tm=_tile(d // parts, 1024), tn=_tile(din, 1536), tk=tm, out_dtype=BF16,
                            dep=token, m_part=(p, parts), name=f"mm_dwin_{l}_{p}")
            sent_in[l].append(_push_start([dw_in], [lax.empty((N_DEV - 1, d // parts, wcol), BF16)], scatter_in,
                                          name=f"scatter_win_{l}_{p}_start"))
            token = sent_in[l][-1][-1]
        dhn = _matmul(du, w_in_full[l], tb=True, tm=tm, tn=_tile(d, 1024), tk=_tile(din, 1536), dep=token,
                      name=f"mm_dhn_{l}")
        dh, dhb, dg_norm = _rms_bwd(h_in, dhn, dh, norm_g[l], name=f"rms_bwd_{l}")
        if l > 0:
            dg_norms.append(dg_norm[0])

    late = _pack([dg_norm[0], dh[:n_meta]])
    late_all = _all_gather([late], [(N_DEV,) + late.shape], [_lead], name="gather_late_grads")[0]
    late_sum = _unpack(_slot_sum(late_all, name="sum_late_grads"), [(d,), (n_meta, d)])
    _, early_all = _push_wait(sent_early, [_gather_lead], late_sum[0], name="gather_early_grads_wait")
    early_shapes = [conv_a_b.shape, lru_wr.shape, lru_br.shape, lru_wi.shape, lru_bi.shape, lru_lambda.shape,
                    (depth, 4, dl), (depth, 3, dl), final_g.shape] + [(d,)] * (depth - 1)
    e = _unpack(_slot_sum(early_all[0], name="sum_early_grads"), early_shapes)
    g_norm = jnp.stack([late_sum[0]] + e[9:][::-1])
    g_meta = lax.dynamic_slice_in_dim(late_sum[1], me * mcol, mcol, axis=1)
    g_wa = lax.dynamic_slice_in_dim(e[6], me * ccol, ccol, axis=2)
    g_wb = lax.dynamic_slice_in_dim(e[7], me * ccol, ccol, axis=2)

    small_w = [norm_g, conv_a_b, lru_wr, lru_br, lru_wi, lru_bi, lru_lambda, final_g, meta, conv_a_w, conv_b_w]
    small_m = [m_norm_g, m_conv_a_b, m_lru_wr, m_lru_br, m_lru_wi, m_lru_bi, m_lru_lambda, m_final_g, m_meta,
               m_conv_a_w, m_conv_b_w]
    small_v = [v_norm_g, v_conv_a_b, v_lru_wr, v_lru_br, v_lru_wi, v_lru_bi, v_lru_lambda, v_final_g, v_meta,
               v_conv_a_w, v_conv_b_w]
    small_g = [g_norm, e[0], e[1], e[2], e[3], e[4], e[5], e[8], g_meta, g_wa, g_wb]
    small_out = _adamw(_pack(small_w), _pack(small_g), _pack(small_m), _pack(small_v), name="adamw_small")
    small_shapes = [a.shape for a in small_w]
    s_grad, s_delta, s_m, s_v = [_unpack(o, small_shapes) for o in small_out]

    win_out = None
    wout_out = None
    after = small_out[0]
    for l in reversed(range(depth)):
        src, landed = _push_wait(sent_out[l], [_scatter_lead], after, name=f"scatter_wout_{l}_wait")
        own = lax.dynamic_index_in_dim(src[0], me, 0, keepdims=False)
        wout_out = _adamw(w_out[l], own, m_w_out[l], v_w_out[l], landed=landed[0], layer=l, depth=depth,
                          into=wout_out, name=f"adamw_w_out_{l}")
        after = wout_out[0]
        for p, sent in enumerate(sent_in[l]):
            src, landed = _push_wait(sent, scatter_in, after, name=f"scatter_win_{l}_{p}_wait")
            own = lax.dynamic_slice_in_dim(src[0], me * wcol, wcol, axis=1)
            win_out = _adamw(w_in[l], own, m_w_in[l], v_w_in[l], landed=landed[0], layer=l, depth=depth,
                             into=win_out, row_off=p * own.shape[0], name=f"adamw_w_in_{l}_{p}")
            after = win_out[0]

    names = ["norm_g", "conv_a_b", "lru_wr", "lru_br", "lru_wi", "lru_bi", "lru_lambda", "final_g", "meta",
             "conv_a_w", "conv_b_w"]
    order = ["meta", "norm_g", "w_in", "conv_a_w", "conv_a_b", "lru_wr", "lru_br", "lru_wi", "lru_bi", "lru_lambda",
             "conv_b_w", "w_out", "final_g"]

    def family(idx, small):
        table = {nm: small[i] for i, nm in enumerate(names)}
        table["w_in"] = win_out[idx]
        table["w_out"] = wout_out[idx]
        return [table[nm] for nm in order]

    grad_x = dh[n_meta:n_tok][None]
    return (loss, grad_x, *family(0, s_grad), *family(1, s_delta), *family(2, s_m), *family(3, s_v))
```

```python
import functools

import jax
import jax.numpy as jnp
from jax import lax
from jax.experimental import pallas as pl
from jax.experimental.pallas import tpu as pltpu

F32 = jnp.float32
BF16 = jnp.bfloat16
MESH = pl.DeviceIdType.MESH

N_DEV = 8
RMS_EPS = 1e-6
LRU_C = 8.0
ADAM_LR = 0.001
ADAM_B1 = 0.9
ADAM_B2 = 0.999
ADAM_EPS = 1e-08
ADAM_WD = 0.01
ADAM_STEP = 10

V7X_VMEM_LIMIT = 52 * 1024 * 1024
LANES = 128
SUBLANES = 8
TOKEN_TILE = 384
MIX_ROWS = 128
GATE_BLOCK = 256


def _params(sem):
    return pltpu.CompilerParams(dimension_semantics=sem, vmem_limit_bytes=V7X_VMEM_LIMIT)


def _tile(n, target, align=LANES):
    best = None
    for t in range(align, min(n, target) + 1, align):
        if n % t == 0:
            best = t
    return n if best is None else best


def _sigmoid(z):
    return 1.0 / (1.0 + jnp.exp(-z))


def _softplus(z):
    e = jnp.exp(-jnp.abs(z))
    u = 1.0 + e
    l1p = jnp.where(u == 1.0, e, jnp.log(u) * e / jnp.where(u == 1.0, 1.0, u - 1.0))
    return jnp.maximum(z, 0.0) + l1p


def _matmul(a, b, *, ta=False, tb=False, tm, tn, tk, out_dtype=F32, add=None, dep=None, m_part=None,
            b_lead=False, o_lead=False, name):
    m, k = (a.shape[1], a.shape[0]) if ta else a.shape
    m_off = 0
    if m_part is not None:
        assert add is None and m % (m_part[1] * tm) == 0
        m //= m_part[1]
        m_off = m_part[0] * (m // tm)
    if b_lead:
        nb, rows, width = b.shape
        n, kb = (rows, nb * width) if tb else (nb * width, rows)
        assert width == (tk if tb else tn)
    else:
        n, kb = b.shape if tb else b.shape[::-1]
    assert kb == k
    assert m % tm == 0 and n % tn == 0 and k % tk == 0, (m, n, k, tm, tn, tk)
    nk = k // tk
    a_spec = pl.BlockSpec((tk, tm), lambda i, j, q: (q, i + m_off)) if ta \
        else pl.BlockSpec((tm, tk), lambda i, j, q: (i + m_off, q))
    if b_lead:
        b_spec = pl.BlockSpec((None, tn, tk), lambda i, j, q: (q, j, 0)) if tb \
            else pl.BlockSpec((None, tk, tn), lambda i, j, q: (j, q, 0))
    else:
        b_spec = pl.BlockSpec((tn, tk), lambda i, j, q: (j, q)) if tb else pl.BlockSpec((tk, tn), lambda i, j, q: (q, j))
    if o_lead:
        assert add is None
        o_spec = pl.BlockSpec((None, tm, tn), lambda i, j, q: (j, i, 0))
        o_shape = (n // tn, m, tn)
    else:
        o_spec = pl.BlockSpec((tm, tn), lambda i, j, q: (i, j))
        o_shape = (m, n)
    dims = (((0 if ta else 1,), (1 if tb else 0,)), ((), ()))
    has_add = add is not None
    has_dep = dep is not None

    def body(*refs):
        if has_dep:
            refs = refs[:-3] + refs[-2:]
        if has_add:
            a_ref, b_ref, add_ref, o_ref, acc_ref = refs
        else:
            a_ref, b_ref, o_ref, acc_ref = refs
        q = pl.program_id(2)
        part = lax.dot_general(a_ref[...], b_ref[...], dims, preferred_element_type=F32)

        def finish(acc):
            if has_add:
                acc = acc + add_ref[...]
            o_ref[...] = acc.astype(out_dtype)

        if nk == 1:
            finish(part)
        else:
            @pl.when(q == 0)
            def _():
                acc_ref[...] = part

            @pl.when(jnp.logical_and(q > 0, q < nk - 1))
            def _():
                acc_ref[...] += part

            @pl.when(q == nk - 1)
            def _():
                finish(acc_ref[...] + part)

    in_specs = [a_spec, b_spec] + ([o_spec] if has_add else [])
    args = (a, b) + ((add,) if has_add else ())
    if has_dep:
        in_specs.append(pl.BlockSpec((SUBLANES, LANES), lambda i, j, q: (0, 0)))
        args += (dep,)
    acc_shape = (tm, tn) if nk > 1 else (SUBLANES, LANES)
    return pl.pallas_call(
        body, name=name,
        grid=(m // tm, n // tn, nk),
        in_specs=in_specs, out_specs=o_spec,
        out_shape=jax.ShapeDtypeStruct(o_shape, out_dtype),
        scratch_shapes=[pltpu.VMEM(acc_shape, F32)],
        compiler_params=_params(("parallel", "parallel", "arbitrary")),
    )(*args)


def _rms_fwd(h, g, *, name):
    tp, d = h.shape
    tr = _tile(tp, 512, SUBLANES)

    def body(h_ref, g_ref, o_ref):
        hv = h_ref[...]
        rstd = lax.rsqrt(jnp.mean(hv * hv, axis=-1, keepdims=True) + RMS_EPS)
        o_ref[...] = (hv * rstd * g_ref[...]).astype(BF16)

    return pl.pallas_call(
        body, name=name, grid=(tp // tr,),
        in_specs=[pl.BlockSpec((tr, d), lambda i: (i, 0)), pl.BlockSpec((1, d), lambda i: (0, 0))],
        out_specs=pl.BlockSpec((tr, d), lambda i: (i, 0)),
        out_shape=jax.ShapeDtypeStruct((tp, d), BF16),
        compiler_params=_params(("parallel",)),
    )(h, g.reshape(1, d))


def _rms_bwd(h, dhn, dout, g, *, name):
    tp, d = h.shape
    tr = _tile(tp, 384, SUBLANES)

    def body(h_ref, dhn_ref, dout_ref, g_ref, dh_ref, dhb_ref, dg_ref):
        hv = h_ref[...]
        rstd = lax.rsqrt(jnp.mean(hv * hv, axis=-1, keepdims=True) + RMS_EPS)
        xhat = hv * rstd
        dn = dhn_ref[...]
        dxhat = dn * g_ref[...]
        dh = dout_ref[...] + rstd * (dxhat - xhat * jnp.mean(dxhat * xhat, axis=-1, keepdims=True))
        dh_ref[...] = dh
        dhb_ref[...] = dh.astype(BF16)
        part = jnp.sum(dn * xhat, axis=0, keepdims=True)

        @pl.when(pl.program_id(0) == 0)
        def _():
            dg_ref[...] = part

        @pl.when(pl.program_id(0) > 0)
        def _():
            dg_ref[...] += part

    row = pl.BlockSpec((tr, d), lambda i: (i, 0))
    vec = pl.BlockSpec((1, d), lambda i: (0, 0))
    return pl.pallas_call(
        body, name=name, grid=(tp // tr,),
        in_specs=[row, row, row, vec],
        out_specs=[row, row, vec],
        out_shape=[jax.ShapeDtypeStruct((tp, d), F32), jax.ShapeDtypeStruct((tp, d), BF16),
                   jax.ShapeDtypeStruct((1, d), F32)],
        compiler_params=_params(("arbitrary",)),
    )(h, dhn, dout, g.reshape(1, d))


def _loss_head(h, tgt, g, *, n_meta, n_tok, name):
    tp, d = h.shape
    tr = _tile(tp, 384, SUBLANES)

    def body(h_ref, t_ref, g_ref, dh_ref, dhb_ref, dg_ref, loss_ref):
        i = pl.program_id(0)
        hv = h_ref[...]
        rstd = lax.rsqrt(jnp.mean(hv * hv, axis=-1, keepdims=True) + RMS_EPS)
        xhat = hv * rstd
        gv = g_ref[...]
        rows = i * tr + lax.broadcasted_iota(jnp.int32, (tr, 1), 0)
        valid = jnp.logical_and(rows >= n_meta, rows < n_tok)
        err = jnp.where(valid, xhat * gv - t_ref[...], 0.0)
        dy = err * (1.0 / d)
        dxhat = dy * gv
        dh = rstd * (dxhat - xhat * jnp.mean(dxhat * xhat, axis=-1, keepdims=True))
        dh_ref[...] = dh
        dhb_ref[...] = dh.astype(BF16)
        dg_part = jnp.sum(dy * xhat, axis=0, keepdims=True)
        per_row = jnp.sum(err * err, axis=-1, keepdims=True) * (1.0 / d)
        loss_part = jnp.broadcast_to(0.5 * jnp.sum(per_row, axis=0, keepdims=True), (SUBLANES, LANES))

        @pl.when(i == 0)
        def _():
            dg_ref[...] = dg_part
            loss_ref[...] = loss_part

        @pl.when(i > 0)
        def _():
            dg_ref[...] += dg_part
            loss_ref[...] += loss_part

    row = pl.BlockSpec((tr, d), lambda i: (i, 0))
    vec = pl.BlockSpec((1, d), lambda i: (0, 0))
    return pl.pallas_call(
        body, name=name, grid=(tp // tr,),
        in_specs=[row, row, vec],
        out_specs=[row, row, vec, pl.BlockSpec((SUBLANES, LANES), lambda i: (0, 0))],
        out_shape=[jax.ShapeDtypeStruct((tp, d), F32), jax.ShapeDtypeStruct((tp, d), BF16),
                   jax.ShapeDtypeStruct((1, d), F32), jax.ShapeDtypeStruct((SUBLANES, LANES), F32)],
        compiler_params=_params(("arbitrary",)),
    )(h, tgt, g.reshape(1, d))


def _shift_down(halo, tile, s):
    if s == 0:
        return tile
    ext = jnp.concatenate([halo, tile], axis=0)
    return pltpu.roll(ext, s, 0)[SUBLANES:]


def _shift_up(tile, head, s):
    if s == 0:
        return tile
    ext = jnp.concatenate([tile, head], axis=0)
    n = ext.shape[0]
    return pltpu.roll(ext, n - s, 0)[: tile.shape[0]]


def _scan_rows_fwd(a, b):
    row = lax.broadcasted_iota(jnp.int32, a.shape, 0)
    for s in (1, 2, 4):
        a_sh = pltpu.roll(a, s, 0)
        b_sh = pltpu.roll(b, s, 0)
        m = row >= s
        b = jnp.where(m, a * b_sh + b, b)
        a = jnp.where(m, a * a_sh, a)
    return a, b


def _scan_rows_bwd(c, d):
    row = lax.broadcasted_iota(jnp.int32, c.shape, 0)
    for s in (1, 2, 4):
        c_sh = pltpu.roll(c, SUBLANES - s, 0)
        d_sh = pltpu.roll(d, SUBLANES - s, 0)
        m = row < SUBLANES - s
        d = jnp.where(m, c * d_sh + d, d)
        c = jnp.where(m, c * c_sh, c)
    return c, d


def _gates(ca, wr, wi, br, bi, sp):
    cab = ca.astype(BF16)
    r = _sigmoid(jnp.dot(cab, wr, preferred_element_type=F32) + br)
    ig = _sigmoid(jnp.dot(cab, wi, preferred_element_type=F32) + bi)
    la = -LRU_C * r * sp
    a = jnp.exp(la)
    mult = jnp.sqrt(-jnp.tanh(la) * (a * a + 1.0))
    return r, ig, a, mult


def _mixer_fwd(u, wa, ba, wr_blk, br, wi_blk, bi, lam, wb, *, name):
    tp, din = u.shape
    dl = din // 6
    tt = MIX_ROWS
    cw = GATE_BLOCK
    nch = dl // cw
    assert tp % tt == 0 and dl % cw == 0

    def body(u_ref, wa_ref, ba_ref, wr_ref, br_ref, wi_ref, bi_ref, lam_ref, wb_ref,
             ca_ref, hs_ref, y_ref, xa_tail, v_tail, h_carry, a_s, b_s):
        @pl.when(pl.program_id(0) == 0)
        def _():
            xa_tail[...] = jnp.zeros_like(xa_tail)
            v_tail[...] = jnp.zeros_like(v_tail)
            h_carry[...] = jnp.zeros_like(h_carry)

        for ch in range(nch):
            cs = slice(ch * cw, (ch + 1) * cw)

            def seg(s):
                return slice(s * dl + ch * cw, s * dl + (ch + 1) * cw)

            xa = u_ref[:, seg(0)]
            halo = xa_tail[:, cs]
            ca = ba_ref[:, cs] + wa_ref[3:4, cs] * xa
            for kk in range(3):
                ca = ca + wa_ref[kk:kk + 1, cs] * _shift_down(halo, xa, 3 - kk)
            xa_tail[:, cs] = xa[tt - SUBLANES:]
            ca_ref[:, cs] = ca
            sp = _softplus(-lam_ref[:, cs])
            _, ig, a, mult = _gates(ca, wr_ref[ch], wi_ref[ch], br_ref[:, cs], bi_ref[:, cs], sp)
            a_s[:, cs] = a
            b_s[:, cs] = mult * (ig * ca)

            bv = u_ref[:, seg(2)]
            v = u_ref[:, seg(3)] * u_ref[:, seg(4)]
            gb = u_ref[:, seg(5)]
            vh = v_tail[:, cs]
            cb = wb_ref[2:3, cs] * v
            for kk in range(2):
                cb = cb + wb_ref[kk:kk + 1, cs] * _shift_down(vh, v, 2 - kk)
            v_tail[:, cs] = v[tt - SUBLANES:]
            y_ref[:, dl + ch * cw: dl + (ch + 1) * cw] = (bv * cb * (gb * _sigmoid(gb))).astype(BF16)

        def group(gi, hprev):
            rows = pl.ds(pl.multiple_of(gi * SUBLANES, SUBLANES), SUBLANES)
            a8, b8 = _scan_rows_fwd(a_s[rows, :], b_s[rows, :])
            h8 = b8 + a8 * hprev
            hs_ref[rows, :] = h8
            return jnp.broadcast_to(h8[SUBLANES - 1:SUBLANES, :], h8.shape)

        h_carry[...] = lax.fori_loop(0, tt // SUBLANES, group, h_carry[...])

        for ch in range(nch):
            cs = slice(ch * cw, (ch + 1) * cw)
            ga = u_ref[:, dl + ch * cw: dl + (ch + 1) * cw]
            y_ref[:, cs] = (hs_ref[:, cs] * (ga * _sigmoid(ga))).astype(BF16)

    row = lambda w: pl.BlockSpec((tt, w), lambda i: (i, 0))
    full = lambda shp: pl.BlockSpec(shp, lambda i: tuple(0 for _ in shp))
    return pl.pallas_call(
        body, name=name, grid=(tp // tt,),
        in_specs=[row(din), full((4, dl)), full((1, dl)), full((nch, cw, cw)), full((1, dl)),
                  full((nch, cw, cw)), full((1, dl)), full((1, dl)), full((3, dl))],
        out_specs=[row(dl), row(dl), row(2 * dl)],
        out_shape=[jax.ShapeDtypeStruct((tp, dl), F32), jax.ShapeDtypeStruct((tp, dl), F32),
                   jax.ShapeDtypeStruct((tp, 2 * dl), BF16)],
        scratch_shapes=[pltpu.VMEM((SUBLANES, dl), F32), pltpu.VMEM((SUBLANES, dl), F32),
                        pltpu.VMEM((SUBLANES, dl), F32), pltpu.VMEM((tt, dl), F32), pltpu.VMEM((tt, dl), F32)],
        compiler_params=_params(("arbitrary",)),
    )(u, wa, ba, wr_blk, br, wi_blk, bi, lam, wb)


SG_WA, SG_BA, SG_BR, SG_BI, SG_LAM, SG_WB, SG_ROWS = 0, 4, 5, 6, 7, 8, 16


def _mixer_bwd(u, ca, hs, dy, wa, wr_blk, br, wi_blk, bi, lam, wb, *, name):
    tp, din = u.shape
    dl = din // 6
    tt = MIX_ROWS
    cw = GATE_BLOCK
    nch = dl // cw
    nt = tp // tt
    hb = tt // SUBLANES
    tn_dims = (((0,), (0,)), ((), ()))
    nt_dims = (((1,), (1,)), ((), ()))

    def body(u_ref, uh_ref, ca_ref, hs_ref, hsh_ref, dy_ref, wa_ref, wr_ref, br_ref, wi_ref, bi_ref, lam_ref, wb_ref,
             du_ref, sg_ref, dwr_ref, dwi_ref,
             g_carry, a_head, dca_head, dcb_head, r_s, i_s, a_s, an_s, d_s, g_s):
        i = pl.program_id(0)
        first_tile = i == nt - 1

        @pl.when(i == 0)
        def _():
            for ref in (g_carry, a_head, dca_head, dcb_head, sg_ref, dwr_ref, dwi_ref):
                ref[...] = jnp.zeros_like(ref)

        def halo_of(x):
            return jnp.where(first_tile, 0.0, x)

        for ch in range(nch):
            cs = slice(ch * cw, (ch + 1) * cw)
            cav = ca_ref[:, cs]
            sp = _softplus(-lam_ref[:, cs])
            r, ig, a, _ = _gates(cav, wr_ref[ch], wi_ref[ch], br_ref[:, cs], bi_ref[:, cs], sp)
            r_s[:, cs] = r
            i_s[:, cs] = ig
            a_s[:, cs] = a
            an_s[:, cs] = _shift_up(a, a_head[:, cs], 1)
            a_head[:, cs] = a[:SUBLANES]
            ga = u_ref[:, dl + ch * cw: dl + (ch + 1) * cw]
            d_s[:, cs] = dy_ref[:, cs] * (ga * _sigmoid(ga))

        def group(k, gnext):
            gi = tt // SUBLANES - 1 - k
            rows = pl.ds(pl.multiple_of(gi * SUBLANES, SUBLANES), SUBLANES)
            c8, d8 = _scan_rows_bwd(an_s[rows, :], d_s[rows, :])
            g8 = d8 + c8 * gnext
            g_s[rows, :] = g8
            return jnp.broadcast_to(g8[0:1, :], g8.shape)

        g_carry[...] = lax.fori_loop(0, tt // SUBLANES, group, g_carry[...])

        def acc_row(r0, val):
            sg_ref[r0:r0 + 1, cs_cur[0]] += jnp.sum(val, axis=0, keepdims=True)

        cs_cur = [None]
        for ch in range(nch):
            cs = slice(ch * cw, (ch + 1) * cw)
            cs_cur[0] = cs

            def seg(s):
                return slice(s * dl + ch * cw, s * dl + (ch + 1) * cw)

            cav = ca_ref[:, cs]
            r = r_s[:, cs]
            ig = i_s[:, cs]
            a = a_s[:, cs]
            g = g_s[:, cs]
            hsv = hs_ref[:, cs]
            lamv = lam_ref[:, cs]
            sp = _softplus(-lamv)
            la = -LRU_C * r * sp
            e2 = a * a
            mult = jnp.sqrt(-jnp.tanh(la) * (e2 + 1.0))
            hprev = _shift_down(halo_of(hsh_ref[:, cs]), hsv, 1)
            dla = g * hprev * a - g * (ig * cav) * e2 / mult
            gm = g * mult
            dzi = gm * cav * ig * (1.0 - ig)
            dca = gm * ig
            dzr = dla * (-LRU_C * sp) * r * (1.0 - r)
            acc_row(SG_LAM, dla * (-LRU_C * r) * (-_sigmoid(-lamv)))
            acc_row(SG_BR, dzr)
            acc_row(SG_BI, dzi)
            dzr_b = dzr.astype(BF16)
            dzi_b = dzi.astype(BF16)
            cab = cav.astype(BF16)
            dca = dca + lax.dot_general(dzr_b, wr_ref[ch], nt_dims, preferred_element_type=F32)
            dca = dca + lax.dot_general(dzi_b, wi_ref[ch], nt_dims, preferred_element_type=F32)
            dwr_ref[ch] += lax.dot_general(cab, dzr_b, tn_dims, preferred_element_type=F32)
            dwi_ref[ch] += lax.dot_general(cab, dzi_b, tn_dims, preferred_element_type=F32)
            acc_row(SG_BA, dca)
            xa = u_ref[:, seg(0)]
            xah = halo_of(uh_ref[:, seg(0)])
            head = dca_head[:, cs]
            dxa = wa_ref[3:4, cs] * dca
            acc_row(SG_WA + 3, dca * xa)
            for kk in range(3):
                acc_row(SG_WA + kk, dca * _shift_down(xah, xa, 3 - kk))
                dxa = dxa + wa_ref[kk:kk + 1, cs] * _shift_up(dca, head, 3 - kk)
            dca_head[:, cs] = dca[:SUBLANES]
            ga = u_ref[:, seg(1)]
            sga = _sigmoid(ga)
            dga = dy_ref[:, cs] * hsv * (sga * (1.0 + ga * (1.0 - sga)))
            du_ref[:, seg(0)] = dxa.astype(BF16)
            du_ref[:, seg(1)] = dga.astype(BF16)

            bv = u_ref[:, seg(2)]
            cv = u_ref[:, seg(3)]
            xb = u_ref[:, seg(4)]
            gb = u_ref[:, seg(5)]
            dyb = dy_ref[:, dl + ch * cw: dl + (ch + 1) * cw]
            v = cv * xb
            vh = halo_of(uh_ref[:, seg(3)] * uh_ref[:, seg(4)])
            v1 = _shift_down(vh, v, 1)
            v2 = _shift_down(vh, v, 2)
            cb = wb_ref[2:3, cs] * v + wb_ref[1:2, cs] * v1 + wb_ref[0:1, cs] * v2
            sgb = _sigmoid(gb)
            sl = gb * sgb
            dcb = dyb * bv * sl
            du_ref[:, seg(2)] = (dyb * cb * sl).astype(BF16)
            du_ref[:, seg(5)] = (dyb * bv * cb * (sgb * (1.0 + gb * (1.0 - sgb)))).astype(BF16)
            acc_row(SG_WB + 2, dcb * v)
            acc_row(SG_WB + 1, dcb * v1)
            acc_row(SG_WB + 0, dcb * v2)
            bhead = dcb_head[:, cs]
            dv = wb_ref[2:3, cs] * dcb + wb_ref[1:2, cs] * _shift_up(dcb, bhead, 1) \
                + wb_ref[0:1, cs] * _shift_up(dcb, bhead, 2)
            dcb_head[:, cs] = dcb[:SUBLANES]
            du_ref[:, seg(3)] = (dv * xb).astype(BF16)
            du_ref[:, seg(4)] = (dv * cv).astype(BF16)

    rev = lambda w: pl.BlockSpec((tt, w), lambda i: (nt - 1 - i, 0))
    halo = lambda w: pl.BlockSpec((SUBLANES, w), lambda i: (jnp.maximum((nt - 1 - i) * hb - 1, 0), 0))
    full = lambda shp: pl.BlockSpec(shp, lambda i: tuple(0 for _ in shp))
    vm = lambda r: pltpu.VMEM((r, dl), F32)
    return pl.pallas_call(
        body, name=name, grid=(nt,),
        in_specs=[rev(din), halo(din), rev(dl), rev(dl), halo(dl), rev(2 * dl), full((4, dl)),
                  full((nch, cw, cw)), full((1, dl)), full((nch, cw, cw)), full((1, dl)), full((1, dl)), full((3, dl))],
        out_specs=[rev(din), full((SG_ROWS, dl)), full((nch, cw, cw)), full((nch, cw, cw))],
        out_shape=[jax.ShapeDtypeStruct((tp, din), BF16), jax.ShapeDtypeStruct((SG_ROWS, dl), F32),
                   jax.ShapeDtypeStruct((nch, cw, cw), F32), jax.ShapeDtypeStruct((nch, cw, cw), F32)],
        scratch_shapes=[vm(SUBLANES), vm(SUBLANES), vm(SUBLANES), vm(SUBLANES),
                        vm(tt), vm(tt), vm(tt), vm(tt), vm(tt), vm(tt)],
        compiler_params=_params(("arbitrary",)),
    )(u, u, ca, hs, hs, dy, wa, wr_blk, br, wi_blk, bi, lam, wb)


def _adamw(w, g, m, v, *, name, landed=None, layer=None, depth=None, into=None, row_off=0):
    r, c = w.shape
    rows = g.shape[0]
    tr = _tile(rows, 256, 2 * SUBLANES)
    assert row_off % tr == 0
    boff = row_off // tr
    bc1 = 1.0 - ADAM_B1 ** ADAM_STEP
    bc2 = 1.0 - ADAM_B2 ** ADAM_STEP
    slots = landed is not None

    def body(*refs):
        if into is not None:
            refs = refs[:-8] + refs[-4:]
        if slots:
            w_ref, g_ref, l_ref, m_ref, v_ref, grad_ref, delta_ref, nm_ref, nv_ref = refs
            gv = g_ref[...].astype(F32)
            for s in range(N_DEV - 1):
                gv = gv + l_ref[s].astype(F32)
        else:
            w_ref, g_ref, m_ref, v_ref, grad_ref, delta_ref, nm_ref, nv_ref = refs
            gv = g_ref[...]
        wv = w_ref[...]
        mn = ADAM_B1 * m_ref[...] + (1.0 - ADAM_B1) * gv
        vn = ADAM_B2 * v_ref[...] + (1.0 - ADAM_B2) * (gv * gv)
        m_hat = mn / bc1
        v_hat = vn / bc2
        grad_ref[...] = gv
        delta_ref[...] = -ADAM_LR * (m_hat / (jnp.sqrt(v_hat) + ADAM_EPS) + ADAM_WD * wv)
        nm_ref[...] = mn
        nv_ref[...] = vn

    blk = pl.BlockSpec((tr, c), lambda i: (i + boff, 0))
    g_blk = pl.BlockSpec((tr, c), lambda i: (i, 0))
    l_spec = [pl.BlockSpec((N_DEV - 1, tr, c), lambda i: (0, i, 0))] if slots else []
    args = (w, g, landed, m, v) if slots else (w, g, m, v)
    in_specs = [blk, g_blk] + l_spec + [blk, blk]
    if depth is None:
        shp = jax.ShapeDtypeStruct((r, c), F32)
        out_blk = blk
    else:
        shp = jax.ShapeDtypeStruct((depth, r, c), F32)
        out_blk = pl.BlockSpec((None, tr, c), lambda i: (layer, i + boff, 0))
    aliases = {}
    if into is not None:
        aliases = {len(args) + j: j for j in range(4)}
        in_specs = in_specs + [ANY] * 4
        args = args + tuple(into)
    return pl.pallas_call(
        body, name=name, grid=(rows // tr,),
        in_specs=in_specs, out_specs=[out_blk] * 4,
        out_shape=[shp] * 4, input_output_aliases=aliases,
        compiler_params=_params(("parallel",)),
    )(*args)


def _slot_sum(g, *, name):
    _, r, c = g.shape
    tr = _tile(r, 512, SUBLANES)

    def body(g_ref, o_ref):
        gv = g_ref[0]
        for s in range(1, N_DEV):
            gv = gv + g_ref[s]
        o_ref[...] = gv

    return pl.pallas_call(
        body, name=name, grid=(r // tr,),
        in_specs=[pl.BlockSpec((N_DEV, tr, c), lambda i: (0, i, 0))],
        out_specs=pl.BlockSpec((tr, c), lambda i: (i, 0)),
        out_shape=jax.ShapeDtypeStruct((r, c), F32),
        compiler_params=_params(("parallel",)),
    )(g)


def _mesh_pos():
    x, y, c = lax.axis_index("x"), lax.axis_index("y"), lax.axis_index("c")
    return x, y, c, 4 * x + 2 * y + c


ANY = pl.BlockSpec(memory_space=pl.ANY)


def _all_gather(srcs, out_shapes, views, *, name, place=()):
    n = len(srcs)
    npl = len(place)

    def body(*refs):
        src = refs[:n]
        psrc = refs[n:n + npl]
        dst = refs[n + npl:2 * n + npl]
        pdst = refs[2 * n + npl:2 * (n + npl)]
        send_sems, recv_sems, local_sems = refs[2 * (n + npl):]
        x, y, c, me = _mesh_pos()
        sibling = (x, y, 1 - c)
        chips = [(1 - x, y), (x, 1 - y), (1 - x, 1 - y)]

        def dev(px, py, pc):
            return 4 * px + 2 * py + pc

        def copy(a, k, block, to, from_src=False):
            win = views[a](dst[a], dev(*block))
            return pltpu.make_async_remote_copy(
                src_ref=src[a] if from_src else win, dst_ref=win,
                send_sem=send_sems.at[a * 7 + k], recv_sem=recv_sems.at[a * 7 + k],
                device_id=to, device_id_type=MESH)

        mine = [pltpu.make_async_copy(src[a], views[a](dst[a], me), local_sems.at[a]) for a in range(n)]
        mine += [pltpu.make_async_copy(psrc[j], place[j][2](pdst[j], me), local_sems.at[n + j]) for j in range(npl)]
        started = []
        for a in range(n):
            mine[a].start()
            first = [copy(a, 0, (x, y, c), sibling, True)]
            first += [copy(a, 1 + j, (x, y, c), (*chip, c), True) for j, chip in enumerate(chips)]
            for cp in first:
                cp.start()
            started += first
        for cp in mine[n:]:
            cp.start()
        for a in range(n):
            for j, chip in enumerate(chips):
                copy(a, 1 + j, (*chip, c), (x, y, c)).wait_recv()
                fwd = copy(a, 4 + j, (*chip, c), sibling)
                fwd.start()
                started.append(fwd)
        for a in range(n):
            copy(a, 0, (x, y, 1 - c), (x, y, c)).wait_recv()
            for j, chip in enumerate(chips):
                copy(a, 4 + j, (*chip, 1 - c), (x, y, c)).wait_recv()
        for cp in started:
            cp.wait_send()
        for cp in mine:
            cp.wait()

    return pl.pallas_call(
        body, name=name,
        in_specs=[ANY] * (n + npl), out_specs=[ANY] * (n + npl),
        out_shape=[jax.ShapeDtypeStruct(s, x.dtype) for s, x in zip(out_shapes, srcs)]
        + [jax.ShapeDtypeStruct(shape, arr.dtype) for arr, shape, _ in place],
        scratch_shapes=[pltpu.SemaphoreType.DMA((7 * n,)), pltpu.SemaphoreType.DMA((7 * n,)),
                        pltpu.SemaphoreType.DMA((n + npl,))],
    )(*srcs, *[arr for arr, _, _ in place])


HBM = pl.BlockSpec(memory_space=pltpu.HBM)
SEM = pl.BlockSpec(memory_space=pltpu.SEMAPHORE)
EFFECT = pltpu.SideEffectType.DATAFLOW_SIDE_EFFECTING


def _peer_copies(n, wins, src, land, send_sems, recv_sems):
    x, y, c, me = _mesh_pos()
    out = []
    for a in range(n):
        for k in range(1, N_DEV):
            px = 1 - x if k & 4 else x
            py = 1 - y if k & 2 else y
            pc = 1 - c if k & 1 else c
            s_win, d_win = wins[a](src[a], land[a], me, 4 * px + 2 * py + pc, k)
            out.append(pltpu.make_async_remote_copy(
                src_ref=s_win, dst_ref=d_win,
                send_sem=send_sems.at[a * 7 + k - 1], recv_sem=recv_sems.at[a * 7 + k - 1],
                device_id=(px, py, pc), device_id_type=MESH))
    return out


def _push_start(srcs, lands, wins, *, name):
    n = len(srcs)

    def body(*refs):
        src = refs[:n]
        land = refs[n:2 * n]
        send_sems, recv_sems = refs[2 * n], refs[2 * n + 1]
        token = refs[-1]
        for cp in _peer_copies(n, wins, src, land, send_sems, recv_sems):
            cp.start()
        token[...] = jnp.zeros_like(token)

    bufs = (*srcs, *lands)
    return pl.pallas_call(
        body, name=name,
        out_shape=(pltpu.SemaphoreType.DMA((7 * n,)), pltpu.SemaphoreType.DMA((7 * n,)),
                   *[pltpu.HBM(v.shape, v.dtype) for v in bufs], jax.ShapeDtypeStruct((SUBLANES, LANES), F32)),
        in_specs=[HBM] * (2 * n),
        out_specs=(SEM, SEM, *[HBM] * (2 * n), pl.BlockSpec(memory_space=pltpu.VMEM)),
        input_output_aliases={i: 2 + i for i in range(2 * n)},
        compiler_params=pltpu.CompilerParams(has_side_effects=EFFECT),
    )(*[pltpu.with_memory_space_constraint(v, pltpu.HBM) for v in bufs])


def _push_wait(handle, wins, after, *, name):
    send_sems, recv_sems, *bufs, _ = handle
    n = len(bufs) // 2

    def body(*refs):
        src = refs[:n]
        land = refs[n:2 * n]
        for cp in _peer_copies(n, wins, src, land, refs[2 * n], refs[2 * n + 1]):
            cp.wait_send()
            cp.wait_recv()

    outs = pl.pallas_call(
        body, name=name,
        out_shape=tuple(pltpu.HBM(v.shape, v.dtype) for v in bufs),
        in_specs=[HBM] * (2 * n) + [SEM, SEM, ANY],
        out_specs=tuple([HBM] * (2 * n)),
        input_output_aliases={i: i for i in range(2 * n)},
        compiler_params=pltpu.CompilerParams(has_side_effects=EFFECT),
    )(*bufs, send_sems, recv_sems, after)
    return outs[:n], outs[n:]


def _gather_lead(src, land, me, peer, k):
    return src, land.at[me]


def _scatter_lead(src, land, me, peer, k):
    return src.at[peer], land.at[k - 1]


def _dep(x, token):
    return x + token[0, 0].astype(x.dtype)


def _lead(ref, d):
    return ref.at[d]


def _pack(arrs):
    flat = jnp.concatenate([a.reshape(-1).astype(F32) for a in arrs])
    n = flat.shape[0]
    rows = -(-n // (SUBLANES * LANES)) * SUBLANES
    return jnp.pad(flat, (0, rows * LANES - n)).reshape(rows, LANES)


def _unpack(buf, shapes):
    flat = buf.reshape(-1)
    out, off = [], 0
    for s in shapes:
        n = 1
        for q in s:
            n *= q
        out.append(flat[off:off + n].reshape(s))
        off += n
    return out


def _blockdiag(w, cw):
    h, hd, _ = w.shape
    per = cw // hd
    wg = w.reshape(h // per, per, hd, hd)
    eye = jnp.eye(per, dtype=w.dtype)
    blk = jnp.einsum("gpij,pq->gpiqj", wg, eye)
    return blk.reshape(h // per, cw, cw).astype(BF16)


def _blockdiag_extract(g, hd):
    n, cw, _ = g.shape
    per = cw // hd
    g5 = g.reshape(n, per, hd, per, hd)
    idx = jnp.arange(per)
    return g5[:, idx, :, idx, :].transpose(1, 0, 2, 3).reshape(n * per, hd, hd)


def kernel(x, meta, norm_g, w_in, conv_a_w, conv_a_b, lru_wr, lru_br, lru_wi, lru_bi, lru_lambda, conv_b_w, w_out, final_g, loss_target, m_meta, m_norm_g, m_w_in, m_conv_a_w, m_conv_a_b, m_lru_wr, m_lru_br, m_lru_wi, m_lru_bi, m_lru_lambda, m_conv_b_w, m_w_out, m_final_g, v_meta, v_norm_g, v_w_in, v_conv_a_w, v_conv_a_b, v_lru_wr, v_lru_br, v_lru_wi, v_lru_bi, v_lru_lambda, v_conv_b_w, v_w_out, v_final_g):
    _, seq, d = x.shape
    n_meta = meta.shape[0]
    depth = w_in.shape[0]
    din = w_in.shape[2] * N_DEV
    dl = din // 6
    dmix = 2 * dl
    wcol = w_in.shape[2]
    wrow = w_out.shape[1]
    mcol = meta.shape[1]
    ccol = conv_a_w.shape[2]
    heads, hd = lru_wr.shape[1], lru_wr.shape[2]
    n_tok = n_meta + seq
    tp = -(-n_tok // TOKEN_TILE) * TOKEN_TILE
    me = 4 * lax.axis_index("x") + 2 * lax.axis_index("y") + lax.axis_index("c")

    bf = lambda a: a.astype(BF16)
    small_mine = _pack([meta, conv_a_w, conv_b_w])
    w_out_b = [bf(w_out[l]) for l in range(depth)]
    w_in_b = [bf(w_in[l]) for l in range(depth)]
    place = [(s, (N_DEV, wrow, d), _lead) for s in w_out_b] + [(s, (N_DEV, d, wcol), _lead) for s in w_in_b[1:]]
    first = _all_gather([w_in_b[0], small_mine], [(N_DEV, d, wcol), (N_DEV,) + small_mine.shape],
                        [_lead, _lead], name="gather_first", place=place)
    w_in_full = [first[0]] + [None] * (depth - 1)
    w_out_full = [None] * depth
    land_out = first[2:2 + depth]
    land_in = [None] + list(first[2 + depth:])
    parts = [_unpack(first[1][s], [meta.shape, conv_a_w.shape, conv_b_w.shape]) for s in range(N_DEV)]
    meta_full = jnp.concatenate([p[0] for p in parts], axis=1)
    wa_full = jnp.concatenate([p[1] for p in parts], axis=2)
    wb_full = jnp.concatenate([p[2] for p in parts], axis=2)

    push_out = [None] * depth
    push_in = [None] * depth
    push_out[0] = _push_start([w_out_b[0]], [land_out[0]], [_gather_lead], name="gather_wout_0_start")
    token = push_out[0][-1]
    for l in range(1, depth):
        push_in[l] = _push_start([_dep(w_in_b[l], token)], [land_in[l]], [_gather_lead],
                                 name=f"gather_win_{l}_start")
        push_out[l] = _push_start([_dep(w_out_b[l], push_in[l][-1])], [land_out[l]], [_gather_lead],
                                  name=f"gather_wout_{l}_start")
        token = push_out[l][-1]

    wr_blk = [_blockdiag(lru_wr[l], GATE_BLOCK) for l in range(depth)]
    wi_blk = [_blockdiag(lru_wi[l], GATE_BLOCK) for l in range(depth)]
    vec = lambda a: a.reshape(1, dl)

    h = jnp.concatenate([meta_full, x[0], jnp.zeros((tp - n_tok, d), F32)], axis=0)
    tgt = jnp.pad(loss_target[0], ((n_meta, tp - n_tok), (0, 0)))
    tm = _tile(tp, 1408)
    saved = []
    for l in range(depth):
        gain = _dep(norm_g[l], token) if l == 0 else norm_g[l]
        hn = _rms_fwd(h, gain, name=f"rms_fwd_{l}")
        if l > 0:
            _, landed = _push_wait(push_in[l], [_gather_lead], hn, name=f"gather_win_{l}_wait")
            w_in_full[l] = landed[0]
        u = _matmul(hn, w_in_full[l], tm=tm, tn=wcol, tk=d, b_lead=True, name=f"mm_u_{l}")
        ca, hs, y = _mixer_fwd(u, wa_full[l], vec(conv_a_b[l]), wr_blk[l], vec(lru_br[l]), wi_blk[l], vec(lru_bi[l]),
                               vec(lru_lambda[l]), wb_full[l], name=f"mixer_fwd_{l}")
        _, landed = _push_wait(push_out[l], [_gather_lead], y, name=f"gather_wout_{l}_wait")
        w_out_full[l] = landed[0].reshape(dmix, d)
        h_next = _matmul(y, w_out_full[l], tm=tm, tn=_tile(d, 512), tk=dmix, add=h, name=f"mm_out_{l}")
        saved.append((h, hn, u, ca, hs, y))
        h = h_next

    dh, dhb, dg_final, loss_part = _loss_head(h, tgt, final_g, n_meta=n_meta, n_tok=n_tok, name="loss_head")
    loss = lax.psum(loss_part[0, 0], ("x", "y", "c"))

    small_grads = [None] * depth
    sent_out = [None] * depth
    sent_in = [None] * depth
    token = None
    dg_norms = []
    for l in reversed(range(depth)):
        h_in, hn, u, ca, hs, y = saved[l]
        dy = _matmul(dhb, w_out_full[l], tb=True, tm=tm, tn=_tile(dmix, 512), tk=d, dep=token, name=f"mm_dy_{l}")
        dw_out = _matmul(y, dhb, ta=True, tm=_tile(dmix, 1024), tn=_tile(d, 1024), tk=tm, out_dtype=BF16,
                         name=f"mm_dwout_{l}")
        sent_out[l] = _push_start([dw_out.reshape(N_DEV, wrow, d)], [lax.empty((N_DEV - 1, wrow, d), BF16)],
                                  [_scatter_lead], name=f"scatter_wout_{l}_start")
        du, sg, dwr, dwi = _mixer_bwd(u, ca, hs, dy, wa_full[l], wr_blk[l], vec(lru_br[l]), wi_blk[l], vec(lru_bi[l]),
                                      vec(lru_lambda[l]), _dep(wb_full[l], sent_out[l][-1]), name=f"mixer_bwd_{l}")
        small_grads[l] = (sg, dwr, dwi)
        if l == 0:
            early = _pack([
                jnp.stack([small_grads[j][0][SG_BA] for j in range(depth)]),
                jnp.stack([_blockdiag_extract(small_grads[j][1], hd) for j in range(depth)]),
                jnp.stack([small_grads[j][0][SG_BR] for j in range(depth)]),
                jnp.stack([_blockdiag_extract(small_grads[j][2], hd) for j in range(depth)]),
                jnp.stack([small_grads[j][0][SG_BI] for j in range(depth)]),
                jnp.stack([small_grads[j][0][SG_LAM] for j in range(depth)]),
                jnp.stack([small_grads[j][0][SG_WA:SG_WA + 4] for j in range(depth)]),
                jnp.stack([small_grads[j][0][SG_WB:SG_WB + 3] for j in range(depth)]),
                dg_final[0], *dg_norms])
            early_land = lax.dynamic_update_slice(lax.empty((N_DEV,) + early.shape, F32), early[None], (me, 0, 0))
            sent_early = _push_start([early], [early_land], [_gather_lead], name="gather_early_grads_start")
        parts = 2 if l == 0 else 1
        token = sent_early[-1] if l == 0 else None
        sent_in[l] = []
        for p in range(parts):
            dw_in = _matmul(hn, du, ta=True, tm=_tile(d // parts, 1024), tn=wcol, tk=tm, out_dtype=BF16,
                            dep=token, m_part=(p, parts), o_lead=True, name=f"mm_dwin_{l}_{p}")
            sent_in[l].append(_push_start([dw_in], [lax.empty((N_DEV - 1, d // parts, wcol), BF16)], [_scatter_lead],
                                          name=f"scatter_win_{l}_{p}_start"))
            token = sent_in[l][-1][-1]
        dhn = _matmul(du, w_in_full[l], tb=True, tm=tm, tn=_tile(d, 1024), tk=wcol, dep=token, b_lead=True,
                      name=f"mm_dhn_{l}")
        dh, dhb, dg_norm = _rms_bwd(h_in, dhn, dh, norm_g[l], name=f"rms_bwd_{l}")
        if l > 0:
            dg_norms.append(dg_norm[0])

    late = _pack([dg_norm[0], dh[:n_meta]])
    late_all = _all_gather([late], [(N_DEV,) + late.shape], [_lead], name="gather_late_grads")[0]
    late_sum = _unpack(_slot_sum(late_all, name="sum_late_grads"), [(d,), (n_meta, d)])
    _, early_all = _push_wait(sent_early, [_gather_lead], late_sum[0], name="gather_early_grads_wait")
    early_shapes = [conv_a_b.shape, lru_wr.shape, lru_br.shape, lru_wi.shape, lru_bi.shape, lru_lambda.shape,
                    (depth, 4, dl), (depth, 3, dl), final_g.shape] + [(d,)] * (depth - 1)
    e = _unpack(_slot_sum(early_all[0], name="sum_early_grads"), early_shapes)
    g_norm = jnp.stack([late_sum[0]] + e[9:][::-1])
    g_meta = lax.dynamic_slice_in_dim(late_sum[1], me * mcol, mcol, axis=1)
    g_wa = lax.dynamic_slice_in_dim(e[6], me * ccol, ccol, axis=2)
    g_wb = lax.dynamic_slice_in_dim(e[7], me * ccol, ccol, axis=2)

    small_w = [norm_g, conv_a_b, lru_wr, lru_br, lru_wi, lru_bi, lru_lambda, final_g, meta, conv_a_w, conv_b_w]
    small_m = [m_norm_g, m_conv_a_b, m_lru_wr, m_lru_br, m_lru_wi, m_lru_bi, m_lru_lambda, m_final_g, m_meta,
               m_conv_a_w, m_conv_b_w]
    small_v = [v_norm_g, v_conv_a_b, v_lru_wr, v_lru_br, v_lru_wi, v_lru_bi, v_lru_lambda, v_final_g, v_meta,
               v_conv_a_w, v_conv_b_w]
    small_g = [g_norm, e[0], e[1], e[2], e[3], e[4], e[5], e[8], g_meta, g_wa, g_wb]
    small_out = _adamw(_pack(small_w), _pack(small_g), _pack(small_m), _pack(small_v), name="adamw_small")
    small_shapes = [a.shape for a in small_w]
    s_grad, s_delta, s_m, s_v = [_unpack(o, small_shapes) for o in small_out]

    win_out = None
    wout_out = None
    after = small_out[0]
    for l in reversed(range(depth)):
        src, landed = _push_wait(sent_out[l], [_scatter_lead], after, name=f"scatter_wout_{l}_wait")
        own = lax.dynamic_index_in_dim(src[0], me, 0, keepdims=False)
        wout_out = _adamw(w_out[l], own, m_w_out[l], v_w_out[l], landed=landed[0], layer=l, depth=depth,
                          into=wout_out, name=f"adamw_w_out_{l}")
        after = wout_out[0]
        for p, sent in enumerate(sent_in[l]):
            src, landed = _push_wait(sent, [_scatter_lead], after, name=f"scatter_win_{l}_{p}_wait")
            own = lax.dynamic_index_in_dim(src[0], me, 0, keepdims=False)
            win_out = _adamw(w_in[l], own, m_w_in[l], v_w_in[l], landed=landed[0], layer=l, depth=depth,
                             into=win_out, row_off=p * own.shape[0], name=f"adamw_w_in_{l}_{p}")
            after = win_out[0]

    names = ["norm_g", "conv_a_b", "lru_wr", "lru_br", "lru_wi", "lru_bi", "lru_lambda", "final_g", "meta",
             "conv_a_w", "conv_b_w"]
    order = ["meta", "norm_g", "w_in", "conv_a_w", "conv_a_b", "lru_wr", "lru_br", "lru_wi", "lru_bi", "lru_lambda",
             "conv_b_w", "w_out", "final_g"]

    def family(idx, small):
        table = {nm: small[i] for i, nm in enumerate(names)}
        table["w_in"] = win_out[idx]
        table["w_out"] = wout_out[idx]
        return [table[nm] for nm in order]

    grad_x = dh[n_meta:n_tok][None]
    return (loss, grad_x, *family(0, s_grad), *family(1, s_delta), *family(2, s_m), *family(3, s_v))
```

```python
import functools

import jax
import jax.numpy as jnp
from jax import lax
from jax.experimental import pallas as pl
from jax.experimental.pallas import tpu as pltpu

F32 = jnp.float32
BF16 = jnp.bfloat16
MESH = pl.DeviceIdType.MESH

N_DEV = 8
RMS_EPS = 1e-6
LRU_C = 8.0
ADAM_LR = 0.001
ADAM_B1 = 0.9
ADAM_B2 = 0.999
ADAM_EPS = 1e-08
ADAM_WD = 0.01
ADAM_STEP = 10

V7X_VMEM_LIMIT = 52 * 1024 * 1024
LANES = 128
SUBLANES = 8
TOKEN_TILE = 384
MIX_ROWS = 128
GATE_BLOCK = 256


def _params(sem):
    return pltpu.CompilerParams(dimension_semantics=sem, vmem_limit_bytes=V7X_VMEM_LIMIT)


def _tile(n, target, align=LANES):
    best = None
    for t in range(align, min(n, target) + 1, align):
        if n % t == 0:
            best = t
    return n if best is None else best


def _sigmoid(z):
    return 1.0 / (1.0 + jnp.exp(-z))


def _softplus(z):
    e = jnp.exp(-jnp.abs(z))
    u = 1.0 + e
    l1p = jnp.where(u == 1.0, e, jnp.log(u) * e / jnp.where(u == 1.0, 1.0, u - 1.0))
    return jnp.maximum(z, 0.0) + l1p


def _matmul(a, b, *, ta=False, tb=False, tm, tn, tk, out_dtype=F32, add=None, dep=None, m_part=None, name):
    m, k = (a.shape[1], a.shape[0]) if ta else a.shape
    m_off = 0
    if m_part is not None:
        assert add is None and m % (m_part[1] * tm) == 0
        m //= m_part[1]
        m_off = m_part[0] * (m // tm)
    n, kb = b.shape if tb else b.shape[::-1]
    assert kb == k
    assert m % tm == 0 and n % tn == 0 and k % tk == 0, (m, n, k, tm, tn, tk)
    nk = k // tk
    a_spec = pl.BlockSpec((tk, tm), lambda i, j, q: (q, i + m_off)) if ta \
        else pl.BlockSpec((tm, tk), lambda i, j, q: (i + m_off, q))
    b_spec = pl.BlockSpec((tn, tk), lambda i, j, q: (j, q)) if tb else pl.BlockSpec((tk, tn), lambda i, j, q: (q, j))
    o_spec = pl.BlockSpec((tm, tn), lambda i, j, q: (i, j))
    o_shape = (m, n)
    dims = (((0 if ta else 1,), (1 if tb else 0,)), ((), ()))
    has_add = add is not None
    has_dep = dep is not None

    def body(*refs):
        if has_dep:
            refs = refs[:-3] + refs[-2:]
        if has_add:
            a_ref, b_ref, add_ref, o_ref, acc_ref = refs
        else:
            a_ref, b_ref, o_ref, acc_ref = refs
        q = pl.program_id(2)
        part = lax.dot_general(a_ref[...], b_ref[...], dims, preferred_element_type=F32)

        def finish(acc):
            if has_add:
                acc = acc + add_ref[...]
            o_ref[...] = acc.astype(out_dtype)

        if nk == 1:
            finish(part)
        else:
            @pl.when(q == 0)
            def _():
                acc_ref[...] = part

            @pl.when(jnp.logical_and(q > 0, q < nk - 1))
            def _():
                acc_ref[...] += part

            @pl.when(q == nk - 1)
            def _():
                finish(acc_ref[...] + part)

    in_specs = [a_spec, b_spec] + ([o_spec] if has_add else [])
    args = (a, b) + ((add,) if has_add else ())
    if has_dep:
        in_specs.append(pl.BlockSpec((SUBLANES, LANES), lambda i, j, q: (0, 0)))
        args += (dep,)
    acc_shape = (tm, tn) if nk > 1 else (SUBLANES, LANES)
    return pl.pallas_call(
        body, name=name,
        grid=(m // tm, n // tn, nk),
        in_specs=in_specs, out_specs=o_spec,
        out_shape=jax.ShapeDtypeStruct(o_shape, out_dtype),
        scratch_shapes=[pltpu.VMEM(acc_shape, F32)],
        compiler_params=_params(("parallel", "parallel", "arbitrary")),
    )(*args)


def _rms_fwd(h, g, *, name):
    tp, d = h.shape
    tr = _tile(tp, 512, SUBLANES)

    def body(h_ref, g_ref, o_ref):
        hv = h_ref[...]
        rstd = lax.rsqrt(jnp.mean(hv * hv, axis=-1, keepdims=True) + RMS_EPS)
        o_ref[...] = (hv * rstd * g_ref[...]).astype(BF16)

    return pl.pallas_call(
        body, name=name, grid=(tp // tr,),
        in_specs=[pl.BlockSpec((tr, d), lambda i: (i, 0)), pl.BlockSpec((1, d), lambda i: (0, 0))],
        out_specs=pl.BlockSpec((tr, d), lambda i: (i, 0)),
        out_shape=jax.ShapeDtypeStruct((tp, d), BF16),
        compiler_params=_params(("parallel",)),
    )(h, g.reshape(1, d))


def _rms_bwd(h, dhn, dout, g, *, name):
    tp, d = h.shape
    tr = _tile(tp, 384, SUBLANES)

    def body(h_ref, dhn_ref, dout_ref, g_ref, dh_ref, dhb_ref, dg_ref):
        hv = h_ref[...]
        rstd = lax.rsqrt(jnp.mean(hv * hv, axis=-1, keepdims=True) + RMS_EPS)
        xhat = hv * rstd
        dn = dhn_ref[...]
        dxhat = dn * g_ref[...]
        dh = dout_ref[...] + rstd * (dxhat - xhat * jnp.mean(dxhat * xhat, axis=-1, keepdims=True))
        dh_ref[...] = dh
        dhb_ref[...] = dh.astype(BF16)
        part = jnp.sum(dn * xhat, axis=0, keepdims=True)

        @pl.when(pl.program_id(0) == 0)
        def _():
            dg_ref[...] = part

        @pl.when(pl.program_id(0) > 0)
        def _():
            dg_ref[...] += part

    row = pl.BlockSpec((tr, d), lambda i: (i, 0))
    vec = pl.BlockSpec((1, d), lambda i: (0, 0))
    return pl.pallas_call(
        body, name=name, grid=(tp // tr,),
        in_specs=[row, row, row, vec],
        out_specs=[row, row, vec],
        out_shape=[jax.ShapeDtypeStruct((tp, d), F32), jax.ShapeDtypeStruct((tp, d), BF16),
                   jax.ShapeDtypeStruct((1, d), F32)],
        compiler_params=_params(("arbitrary",)),
    )(h, dhn, dout, g.reshape(1, d))


def _loss_head(h, tgt, g, *, n_meta, n_tok, name):
    tp, d = h.shape
    tr = _tile(tp, 384, SUBLANES)

    def body(h_ref, t_ref, g_ref, dh_ref, dhb_ref, dg_ref, loss_ref):
        i = pl.program_id(0)
        hv = h_ref[...]
        rstd = lax.rsqrt(jnp.mean(hv * hv, axis=-1, keepdims=True) + RMS_EPS)
        xhat = hv * rstd
        gv = g_ref[...]
        rows = i * tr + lax.broadcasted_iota(jnp.int32, (tr, 1), 0)
        valid = jnp.logical_and(rows >= n_meta, rows < n_tok)
        err = jnp.where(valid, xhat * gv - t_ref[...], 0.0)
        dy = err * (1.0 / d)
        dxhat = dy * gv
        dh = rstd * (dxhat - xhat * jnp.mean(dxhat * xhat, axis=-1, keepdims=True))
        dh_ref[...] = dh
        dhb_ref[...] = dh.astype(BF16)
        dg_part = jnp.sum(dy * xhat, axis=0, keepdims=True)
        per_row = jnp.sum(err * err, axis=-1, keepdims=True) * (1.0 / d)
        loss_part = jnp.broadcast_to(0.5 * jnp.sum(per_row, axis=0, keepdims=True), (SUBLANES, LANES))

        @pl.when(i == 0)
        def _():
            dg_ref[...] = dg_part
            loss_ref[...] = loss_part

        @pl.when(i > 0)
        def _():
            dg_ref[...] += dg_part
            loss_ref[...] += loss_part

    row = pl.BlockSpec((tr, d), lambda i: (i, 0))
    vec = pl.BlockSpec((1, d), lambda i: (0, 0))
    return pl.pallas_call(
        body, name=name, grid=(tp // tr,),
        in_specs=[row, row, vec],
        out_specs=[row, row, vec, pl.BlockSpec((SUBLANES, LANES), lambda i: (0, 0))],
        out_shape=[jax.ShapeDtypeStruct((tp, d), F32), jax.ShapeDtypeStruct((tp, d), BF16),
                   jax.ShapeDtypeStruct((1, d), F32), jax.ShapeDtypeStruct((SUBLANES, LANES), F32)],
        compiler_params=_params(("arbitrary",)),
    )(h, tgt, g.reshape(1, d))


def _shift_down(halo, tile, s):
    if s == 0:
        return tile
    ext = jnp.concatenate([halo, tile], axis=0)
    return pltpu.roll(ext, s, 0)[SUBLANES:]


def _shift_up(tile, head, s):
    if s == 0:
        return tile
    ext = jnp.concatenate([tile, head], axis=0)
    n = ext.shape[0]
    return pltpu.roll(ext, n - s, 0)[: tile.shape[0]]


def _scan_rows_fwd(a, b):
    row = lax.broadcasted_iota(jnp.int32, a.shape, 0)
    for s in (1, 2, 4):
        a_sh = pltpu.roll(a, s, 0)
        b_sh = pltpu.roll(b, s, 0)
        m = row >= s
        b = jnp.where(m, a * b_sh + b, b)
        a = jnp.where(m, a * a_sh, a)
    return a, b


def _scan_rows_bwd(c, d):
    row = lax.broadcasted_iota(jnp.int32, c.shape, 0)
    for s in (1, 2, 4):
        c_sh = pltpu.roll(c, SUBLANES - s, 0)
        d_sh = pltpu.roll(d, SUBLANES - s, 0)
        m = row < SUBLANES - s
        d = jnp.where(m, c * d_sh + d, d)
        c = jnp.where(m, c * c_sh, c)
    return c, d


def _gates(ca, wr, wi, br, bi, sp):
    cab = ca.astype(BF16)
    r = _sigmoid(jnp.dot(cab, wr, preferred_element_type=F32) + br)
    ig = _sigmoid(jnp.dot(cab, wi, preferred_element_type=F32) + bi)
    la = -LRU_C * r * sp
    a = jnp.exp(la)
    mult = jnp.sqrt(-jnp.tanh(la) * (a * a + 1.0))
    return r, ig, a, mult


def _mixer_fwd(u, wa, ba, wr_blk, br, wi_blk, bi, lam, wb, *, name):
    tp, din = u.shape
    dl = din // 6
    tt = MIX_ROWS
    cw = GATE_BLOCK
    nch = dl // cw
    assert tp % tt == 0 and dl % cw == 0

    def body(u_ref, wa_ref, ba_ref, wr_ref, br_ref, wi_ref, bi_ref, lam_ref, wb_ref,
             ca_ref, hs_ref, y_ref, xa_tail, v_tail, h_carry, a_s, b_s):
        @pl.when(pl.program_id(0) == 0)
        def _():
            xa_tail[...] = jnp.zeros_like(xa_tail)
            v_tail[...] = jnp.zeros_like(v_tail)
            h_carry[...] = jnp.zeros_like(h_carry)

        for ch in range(nch):
            cs = slice(ch * cw, (ch + 1) * cw)

            def seg(s):
                return slice(s * dl + ch * cw, s * dl + (ch + 1) * cw)

            xa = u_ref[:, seg(0)]
            halo = xa_tail[:, cs]
            ca = ba_ref[:, cs] + wa_ref[3:4, cs] * xa
            for kk in range(3):
                ca = ca + wa_ref[kk:kk + 1, cs] * _shift_down(halo, xa, 3 - kk)
            xa_tail[:, cs] = xa[tt - SUBLANES:]
            ca_ref[:, cs] = ca
            sp = _softplus(-lam_ref[:, cs])
            _, ig, a, mult = _gates(ca, wr_ref[ch], wi_ref[ch], br_ref[:, cs], bi_ref[:, cs], sp)
            a_s[:, cs] = a
            b_s[:, cs] = mult * (ig * ca)

            bv = u_ref[:, seg(2)]
            v = u_ref[:, seg(3)] * u_ref[:, seg(4)]
            gb = u_ref[:, seg(5)]
            vh = v_tail[:, cs]
            cb = wb_ref[2:3, cs] * v
            for kk in range(2):
                cb = cb + wb_ref[kk:kk + 1, cs] * _shift_down(vh, v, 2 - kk)
            v_tail[:, cs] = v[tt - SUBLANES:]
            y_ref[:, dl + ch * cw: dl + (ch + 1) * cw] = (bv * cb * (gb * _sigmoid(gb))).astype(BF16)

        def group(gi, hprev):
            rows = pl.ds(pl.multiple_of(gi * SUBLANES, SUBLANES), SUBLANES)
            a8, b8 = _scan_rows_fwd(a_s[rows, :], b_s[rows, :])
            h8 = b8 + a8 * hprev
            hs_ref[rows, :] = h8
            return jnp.broadcast_to(h8[SUBLANES - 1:SUBLANES, :], h8.shape)

        h_carry[...] = lax.fori_loop(0, tt // SUBLANES, group, h_carry[...])

        for ch in range(nch):
            cs = slice(ch * cw, (ch + 1) * cw)
            ga = u_ref[:, dl + ch * cw: dl + (ch + 1) * cw]
            y_ref[:, cs] = (hs_ref[:, cs] * (ga * _sigmoid(ga))).astype(BF16)

    row = lambda w: pl.BlockSpec((tt, w), lambda i: (i, 0))
    full = lambda shp: pl.BlockSpec(shp, lambda i: tuple(0 for _ in shp))
    return pl.pallas_call(
        body, name=name, grid=(tp // tt,),
        in_specs=[row(din), full((4, dl)), full((1, dl)), full((nch, cw, cw)), full((1, dl)),
                  full((nch, cw, cw)), full((1, dl)), full((1, dl)), full((3, dl))],
        out_specs=[row(dl), row(dl), row(2 * dl)],
        out_shape=[jax.ShapeDtypeStruct((tp, dl), F32), jax.ShapeDtypeStruct((tp, dl), F32),
                   jax.ShapeDtypeStruct((tp, 2 * dl), BF16)],
        scratch_shapes=[pltpu.VMEM((SUBLANES, dl), F32), pltpu.VMEM((SUBLANES, dl), F32),
                        pltpu.VMEM((SUBLANES, dl), F32), pltpu.VMEM((tt, dl), F32), pltpu.VMEM((tt, dl), F32)],
        compiler_params=_params(("arbitrary",)),
    )(u, wa, ba, wr_blk, br, wi_blk, bi, lam, wb)


SG_WA, SG_BA, SG_BR, SG_BI, SG_LAM, SG_WB, SG_ROWS = 0, 4, 5, 6, 7, 8, 16


def _mixer_bwd(u, ca, hs, dy, wa, wr_blk, br, wi_blk, bi, lam, wb, *, name):
    tp, din = u.shape
    dl = din // 6
    tt = MIX_ROWS
    cw = GATE_BLOCK
    nch = dl // cw
    nt = tp // tt
    hb = tt // SUBLANES
    tn_dims = (((0,), (0,)), ((), ()))
    nt_dims = (((1,), (1,)), ((), ()))

    def body(u_ref, uh_ref, ca_ref, hs_ref, hsh_ref, dy_ref, wa_ref, wr_ref, br_ref, wi_ref, bi_ref, lam_ref, wb_ref,
             du_ref, sg_ref, dwr_ref, dwi_ref,
             g_carry, a_head, dca_head, dcb_head, r_s, i_s, a_s, an_s, d_s, g_s):
        i = pl.program_id(0)
        first_tile = i == nt - 1

        @pl.when(i == 0)
        def _():
            for ref in (g_carry, a_head, dca_head, dcb_head, sg_ref, dwr_ref, dwi_ref):
                ref[...] = jnp.zeros_like(ref)

        def halo_of(x):
            return jnp.where(first_tile, 0.0, x)

        for ch in range(nch):
            cs = slice(ch * cw, (ch + 1) * cw)
            cav = ca_ref[:, cs]
            sp = _softplus(-lam_ref[:, cs])
            r, ig, a, _ = _gates(cav, wr_ref[ch], wi_ref[ch], br_ref[:, cs], bi_ref[:, cs], sp)
            r_s[:, cs] = r
            i_s[:, cs] = ig
            a_s[:, cs] = a
            an_s[:, cs] = _shift_up(a, a_head[:, cs], 1)
            a_head[:, cs] = a[:SUBLANES]
            ga = u_ref[:, dl + ch * cw: dl + (ch + 1) * cw]
            d_s[:, cs] = dy_ref[:, cs] * (ga * _sigmoid(ga))

        def group(k, gnext):
            gi = tt // SUBLANES - 1 - k
            rows = pl.ds(pl.multiple_of(gi * SUBLANES, SUBLANES), SUBLANES)
            c8, d8 = _scan_rows_bwd(an_s[rows, :], d_s[rows, :])
            g8 = d8 + c8 * gnext
            g_s[rows, :] = g8
            return jnp.broadcast_to(g8[0:1, :], g8.shape)

        g_carry[...] = lax.fori_loop(0, tt // SUBLANES, group, g_carry[...])

        def acc_row(r0, val):
            sg_ref[r0:r0 + 1, cs_cur[0]] += jnp.sum(val, axis=0, keepdims=True)

        cs_cur = [None]
        for ch in range(nch):
            cs = slice(ch * cw, (ch + 1) * cw)
            cs_cur[0] = cs

            def seg(s):
                return slice(s * dl + ch * cw, s * dl + (ch + 1) * cw)

            cav = ca_ref[:, cs]
            r = r_s[:, cs]
            ig = i_s[:, cs]
            a = a_s[:, cs]
            g = g_s[:, cs]
            hsv = hs_ref[:, cs]
            lamv = lam_ref[:, cs]
            sp = _softplus(-lamv)
            la = -LRU_C * r * sp
            e2 = a * a
            mult = jnp.sqrt(-jnp.tanh(la) * (e2 + 1.0))
            hprev = _shift_down(halo_of(hsh_ref[:, cs]), hsv, 1)
            dla = g * hprev * a - g * (ig * cav) * e2 / mult
            gm = g * mult
            dzi = gm * cav * ig * (1.0 - ig)
            dca = gm * ig
            dzr = dla * (-LRU_C * sp) * r * (1.0 - r)
            acc_row(SG_LAM, dla * (-LRU_C * r) * (-_sigmoid(-lamv)))
            acc_row(SG_BR, dzr)
            acc_row(SG_BI, dzi)
            dzr_b = dzr.astype(BF16)
            dzi_b = dzi.astype(BF16)
            cab = cav.astype(BF16)
            dca = dca + lax.dot_general(dzr_b, wr_ref[ch], nt_dims, preferred_element_type=F32)
            dca = dca + lax.dot_general(dzi_b, wi_ref[ch], nt_dims, preferred_element_type=F32)
            dwr_ref[ch] += lax.dot_general(cab, dzr_b, tn_dims, preferred_element_type=F32)
            dwi_ref[ch] += lax.dot_general(cab, dzi_b, tn_dims, preferred_element_type=F32)
            acc_row(SG_BA, dca)
            xa = u_ref[:, seg(0)]
            xah = halo_of(uh_ref[:, seg(0)])
            head = dca_head[:, cs]
            dxa = wa_ref[3:4, cs] * dca
            acc_row(SG_WA + 3, dca * xa)
            for kk in range(3):
                acc_row(SG_WA + kk, dca * _shift_down(xah, xa, 3 - kk))
                dxa = dxa + wa_ref[kk:kk + 1, cs] * _shift_up(dca, head, 3 - kk)
            dca_head[:, cs] = dca[:SUBLANES]
            ga = u_ref[:, seg(1)]
            sga = _sigmoid(ga)
            dga = dy_ref[:, cs] * hsv * (sga * (1.0 + ga * (1.0 - sga)))
            du_ref[:, seg(0)] = dxa.astype(BF16)
            du_ref[:, seg(1)] = dga.astype(BF16)

            bv = u_ref[:, seg(2)]
            cv = u_ref[:, seg(3)]
            xb = u_ref[:, seg(4)]
            gb = u_ref[:, seg(5)]
            dyb = dy_ref[:, dl + ch * cw: dl + (ch + 1) * cw]
            v = cv * xb
            vh = halo_of(uh_ref[:, seg(3)] * uh_ref[:, seg(4)])
            v1 = _shift_down(vh, v, 1)
            v2 = _shift_down(vh, v, 2)
            cb = wb_ref[2:3, cs] * v + wb_ref[1:2, cs] * v1 + wb_ref[0:1, cs] * v2
            sgb = _sigmoid(gb)
            sl = gb * sgb
            dcb = dyb * bv * sl
            du_ref[:, seg(2)] = (dyb * cb * sl).astype(BF16)
            du_ref[:, seg(5)] = (dyb * bv * cb * (sgb * (1.0 + gb * (1.0 - sgb)))).astype(BF16)
            acc_row(SG_WB + 2, dcb * v)
            acc_row(SG_WB + 1, dcb * v1)
            acc_row(SG_WB + 0, dcb * v2)
            bhead = dcb_head[:, cs]
            dv = wb_ref[2:3, cs] * dcb + wb_ref[1:2, cs] * _shift_up(dcb, bhead, 1) \
                + wb_ref[0:1, cs] * _shift_up(dcb, bhead, 2)
            dcb_head[:, cs] = dcb[:SUBLANES]
            du_ref[:, seg(3)] = (dv * xb).astype(BF16)
            du_ref[:, seg(4)] = (dv * cv).astype(BF16)

    rev = lambda w: pl.BlockSpec((tt, w), lambda i: (nt - 1 - i, 0))
    halo = lambda w: pl.BlockSpec((SUBLANES, w), lambda i: (jnp.maximum((nt - 1 - i) * hb - 1, 0), 0))
    full = lambda shp: pl.BlockSpec(shp, lambda i: tuple(0 for _ in shp))
    vm = lambda r: pltpu.VMEM((r, dl), F32)
    return pl.pallas_call(
        body, name=name, grid=(nt,),
        in_specs=[rev(din), halo(din), rev(dl), rev(dl), halo(dl), rev(2 * dl), full((4, dl)),
                  full((nch, cw, cw)), full((1, dl)), full((nch, cw, cw)), full((1, dl)), full((1, dl)), full((3, dl))],
        out_specs=[rev(din), full((SG_ROWS, dl)), full((nch, cw, cw)), full((nch, cw, cw))],
        out_shape=[jax.ShapeDtypeStruct((tp, din), BF16), jax.ShapeDtypeStruct((SG_ROWS, dl), F32),
                   jax.ShapeDtypeStruct((nch, cw, cw), F32), jax.ShapeDtypeStruct((nch, cw, cw), F32)],
        scratch_shapes=[vm(SUBLANES), vm(SUBLANES), vm(SUBLANES), vm(SUBLANES),
                        vm(tt), vm(tt), vm(tt), vm(tt), vm(tt), vm(tt)],
        compiler_params=_params(("arbitrary",)),
    )(u, u, ca, hs, hs, dy, wa, wr_blk, br, wi_blk, bi, lam, wb)


def _adamw(w, g, m, v, *, name, landed=None, layer=None, depth=None, into=None, row_off=0):
    r, c = w.shape
    rows = g.shape[0]
    tr = _tile(rows, 256, 2 * SUBLANES)
    assert row_off % tr == 0
    boff = row_off // tr
    bc1 = 1.0 - ADAM_B1 ** ADAM_STEP
    bc2 = 1.0 - ADAM_B2 ** ADAM_STEP
    slots = landed is not None

    def body(*refs):
        if into is not None:
            refs = refs[:-8] + refs[-4:]
        if slots:
            w_ref, g_ref, l_ref, m_ref, v_ref, grad_ref, delta_ref, nm_ref, nv_ref = refs
            gv = g_ref[...].astype(F32)
            for s in range(N_DEV - 1):
                gv = gv + l_ref[s].astype(F32)
        else:
            w_ref, g_ref, m_ref, v_ref, grad_ref, delta_ref, nm_ref, nv_ref = refs
            gv = g_ref[...]
        wv = w_ref[...]
        mn = ADAM_B1 * m_ref[...] + (1.0 - ADAM_B1) * gv
        vn = ADAM_B2 * v_ref[...] + (1.0 - ADAM_B2) * (gv * gv)
        m_hat = mn / bc1
        v_hat = vn / bc2
        grad_ref[...] = gv
        delta_ref[...] = -ADAM_LR * (m_hat / (jnp.sqrt(v_hat) + ADAM_EPS) + ADAM_WD * wv)
        nm_ref[...] = mn
        nv_ref[...] = vn

    blk = pl.BlockSpec((tr, c), lambda i: (i + boff, 0))
    g_blk = pl.BlockSpec((tr, c), lambda i: (i, 0))
    l_spec = [pl.BlockSpec((N_DEV - 1, tr, c), lambda i: (0, i, 0))] if slots else []
    args = (w, g, landed, m, v) if slots else (w, g, m, v)
    in_specs = [blk, g_blk] + l_spec + [blk, blk]
    if depth is None:
        shp = jax.ShapeDtypeStruct((r, c), F32)
        out_blk = blk
    else:
        shp = jax.ShapeDtypeStruct((depth, r, c), F32)
        out_blk = pl.BlockSpec((None, tr, c), lambda i: (layer, i + boff, 0))
    aliases = {}
    if into is not None:
        aliases = {len(args) + j: j for j in range(4)}
        in_specs = in_specs + [ANY] * 4
        args = args + tuple(into)
    return pl.pallas_call(
        body, name=name, grid=(rows // tr,),
        in_specs=in_specs, out_specs=[out_blk] * 4,
        out_shape=[shp] * 4, input_output_aliases=aliases,
        compiler_params=_params(("parallel",)),
    )(*args)


def _slot_sum(g, *, name):
    _, r, c = g.shape
    tr = _tile(r, 512, SUBLANES)

    def body(g_ref, o_ref):
        gv = g_ref[0]
        for s in range(1, N_DEV):
            gv = gv + g_ref[s]
        o_ref[...] = gv

    return pl.pallas_call(
        body, name=name, grid=(r // tr,),
        in_specs=[pl.BlockSpec((N_DEV, tr, c), lambda i: (0, i, 0))],
        out_specs=pl.BlockSpec((tr, c), lambda i: (i, 0)),
        out_shape=jax.ShapeDtypeStruct((r, c), F32),
        compiler_params=_params(("parallel",)),
    )(g)


def _mesh_pos():
    x, y, c = lax.axis_index("x"), lax.axis_index("y"), lax.axis_index("c")
    return x, y, c, 4 * x + 2 * y + c


ANY = pl.BlockSpec(memory_space=pl.ANY)


def _all_gather(srcs, out_shapes, views, *, name, place=()):
    n = len(srcs)
    npl = len(place)

    def body(*refs):
        src = refs[:n]
        psrc = refs[n:n + npl]
        dst = refs[n + npl:2 * n + npl]
        pdst = refs[2 * n + npl:2 * (n + npl)]
        send_sems, recv_sems, local_sems = refs[2 * (n + npl):]
        x, y, c, me = _mesh_pos()
        sibling = (x, y, 1 - c)
        chips = [(1 - x, y), (x, 1 - y), (1 - x, 1 - y)]

        def dev(px, py, pc):
            return 4 * px + 2 * py + pc

        def copy(a, k, block, to, from_src=False):
            win = views[a](dst[a], dev(*block))
            return pltpu.make_async_remote_copy(
                src_ref=src[a] if from_src else win, dst_ref=win,
                send_sem=send_sems.at[a * 7 + k], recv_sem=recv_sems.at[a * 7 + k],
                device_id=to, device_id_type=MESH)

        mine = [pltpu.make_async_copy(src[a], views[a](dst[a], me), local_sems.at[a]) for a in range(n)]
        mine += [pltpu.make_async_copy(psrc[j], place[j][2](pdst[j], me), local_sems.at[n + j]) for j in range(npl)]
        started = []
        for a in range(n):
            mine[a].start()
            first = [copy(a, 0, (x, y, c), sibling, True)]
            first += [copy(a, 1 + j, (x, y, c), (*chip, c), True) for j, chip in enumerate(chips)]
            for cp in first:
                cp.start()
            started += first
        for cp in mine[n:]:
            cp.start()
        for a in range(n):
            for j, chip in enumerate(chips):
                copy(a, 1 + j, (*chip, c), (x, y, c)).wait_recv()
                fwd = copy(a, 4 + j, (*chip, c), sibling)
                fwd.start()
                started.append(fwd)
        for a in range(n):
            copy(a, 0, (x, y, 1 - c), (x, y, c)).wait_recv()
            for j, chip in enumerate(chips):
                copy(a, 4 + j, (*chip, 1 - c), (x, y, c)).wait_recv()
        for cp in started:
            cp.wait_send()
        for cp in mine:
            cp.wait()

    return pl.pallas_call(
        body, name=name,
        in_specs=[ANY] * (n + npl), out_specs=[ANY] * (n + npl),
        out_shape=[jax.ShapeDtypeStruct(s, x.dtype) for s, x in zip(out_shapes, srcs)]
        + [jax.ShapeDtypeStruct(shape, arr.dtype) for arr, shape, _ in place],
        scratch_shapes=[pltpu.SemaphoreType.DMA((7 * n,)), pltpu.SemaphoreType.DMA((7 * n,)),
                        pltpu.SemaphoreType.DMA((n + npl,))],
    )(*srcs, *[arr for arr, _, _ in place])


HBM = pl.BlockSpec(memory_space=pltpu.HBM)
SEM = pl.BlockSpec(memory_space=pltpu.SEMAPHORE)
EFFECT = pltpu.SideEffectType.DATAFLOW_SIDE_EFFECTING


def _peer_copies(n, wins, src, land, send_sems, recv_sems):
    x, y, c, me = _mesh_pos()
    out = []
    for a in range(n):
        for k in range(1, N_DEV):
            px = 1 - x if k & 4 else x
            py = 1 - y if k & 2 else y
            pc = 1 - c if k & 1 else c
            s_win, d_win = wins[a](src[a], land[a], me, 4 * px + 2 * py + pc, k)
            out.append(pltpu.make_async_remote_copy(
                src_ref=s_win, dst_ref=d_win,
                send_sem=send_sems.at[a * 7 + k - 1], recv_sem=recv_sems.at[a * 7 + k - 1],
                device_id=(px, py, pc), device_id_type=MESH))
    return out


def _push_start(srcs, lands, wins, *, name):
    n = len(srcs)

    def body(*refs):
        src = refs[:n]
        land = refs[n:2 * n]
        send_sems, recv_sems = refs[2 * n], refs[2 * n + 1]
        token = refs[-1]
        for cp in _peer_copies(n, wins, src, land, send_sems, recv_sems):
            cp.start()
        token[...] = jnp.zeros_like(token)

    bufs = (*srcs, *lands)
    return pl.pallas_call(
        body, name=name,
        out_shape=(pltpu.SemaphoreType.DMA((7 * n,)), pltpu.SemaphoreType.DMA((7 * n,)),
                   *[pltpu.HBM(v.shape, v.dtype) for v in bufs], jax.ShapeDtypeStruct((SUBLANES, LANES), F32)),
        in_specs=[HBM] * (2 * n),
        out_specs=(SEM, SEM, *[HBM] * (2 * n), pl.BlockSpec(memory_space=pltpu.VMEM)),
        input_output_aliases={i: 2 + i for i in range(2 * n)},
        compiler_params=pltpu.CompilerParams(has_side_effects=EFFECT),
    )(*[pltpu.with_memory_space_constraint(v, pltpu.HBM) for v in bufs])


def _push_wait(handle, wins, after, *, name):
    send_sems, recv_sems, *bufs, _ = handle
    n = len(bufs) // 2

    def body(*refs):
        src = refs[:n]
        land = refs[n:2 * n]
        for cp in _peer_copies(n, wins, src, land, refs[2 * n], refs[2 * n + 1]):
            cp.wait_send()
            cp.wait_recv()

    outs = pl.pallas_call(
        body, name=name,
        out_shape=tuple(pltpu.HBM(v.shape, v.dtype) for v in bufs),
        in_specs=[HBM] * (2 * n) + [SEM, SEM, ANY],
        out_specs=tuple([HBM] * (2 * n)),
        input_output_aliases={i: i for i in range(2 * n)},
        compiler_params=pltpu.CompilerParams(has_side_effects=EFFECT),
    )(*bufs, send_sems, recv_sems, after)
    return outs[:n], outs[n:]


def _gather_lead(src, land, me, peer, k):
    return src, land.at[me]


def _gather_cols(width):
    def win(src, land, me, peer, k):
        return src, land.at[:, pl.ds(me * width, width)]
    return win


def _scatter_lead(src, land, me, peer, k):
    return src.at[peer], land.at[k - 1]


def _scatter_cols(width):
    def win(src, land, me, peer, k):
        return src.at[:, pl.ds(peer * width, width)], land.at[k - 1]
    return win


def _place_block(own, *, cols, name):
    rows, width = own.shape
    tr = _tile(rows, 512, 2 * SUBLANES)
    _, _, _, me = _mesh_pos()

    def body(me_ref, x_ref, o_ref):
        o_ref[...] = x_ref[...]

    if cols:
        out_spec = pl.BlockSpec((tr, width), lambda i, me_ref: (i, me_ref[0]))
        shape = (rows, N_DEV * width)
    else:
        out_spec = pl.BlockSpec((None, tr, width), lambda i, me_ref: (me_ref[0], i, 0))
        shape = (N_DEV, rows, width)
    return pl.pallas_call(
        body, name=name,
        grid_spec=pltpu.PrefetchScalarGridSpec(
            num_scalar_prefetch=1, grid=(rows // tr,),
            in_specs=[pl.BlockSpec((tr, width), lambda i, me_ref: (i, 0))], out_specs=out_spec),
        out_shape=jax.ShapeDtypeStruct(shape, own.dtype),
        compiler_params=_params(("arbitrary",)),
    )(me.astype(jnp.int32).reshape(1), own)


def _dep(x, token):
    return x + token[0, 0].astype(x.dtype)


def _lead(ref, d):
    return ref.at[d]


def _col_window(width):
    def view(ref, d):
        return ref.at[:, pl.ds(d * width, width)]
    return view


def _pack(arrs):
    flat = jnp.concatenate([a.reshape(-1).astype(F32) for a in arrs])
    n = flat.shape[0]
    rows = -(-n // (SUBLANES * LANES)) * SUBLANES
    return jnp.pad(flat, (0, rows * LANES - n)).reshape(rows, LANES)


def _unpack(buf, shapes):
    flat = buf.reshape(-1)
    out, off = [], 0
    for s in shapes:
        n = 1
        for q in s:
            n *= q
        out.append(flat[off:off + n].reshape(s))
        off += n
    return out


def _blockdiag(w, cw):
    h, hd, _ = w.shape
    per = cw // hd
    wg = w.reshape(h // per, per, hd, hd)
    eye = jnp.eye(per, dtype=w.dtype)
    blk = jnp.einsum("gpij,pq->gpiqj", wg, eye)
    return blk.reshape(h // per, cw, cw).astype(BF16)


def _blockdiag_extract(g, hd):
    n, cw, _ = g.shape
    per = cw // hd
    g5 = g.reshape(n, per, hd, per, hd)
    idx = jnp.arange(per)
    return g5[:, idx, :, idx, :].transpose(1, 0, 2, 3).reshape(n * per, hd, hd)


def kernel(x, meta, norm_g, w_in, conv_a_w, conv_a_b, lru_wr, lru_br, lru_wi, lru_bi, lru_lambda, conv_b_w, w_out, final_g, loss_target, m_meta, m_norm_g, m_w_in, m_conv_a_w, m_conv_a_b, m_lru_wr, m_lru_br, m_lru_wi, m_lru_bi, m_lru_lambda, m_conv_b_w, m_w_out, m_final_g, v_meta, v_norm_g, v_w_in, v_conv_a_w, v_conv_a_b, v_lru_wr, v_lru_br, v_lru_wi, v_lru_bi, v_lru_lambda, v_conv_b_w, v_w_out, v_final_g):
    _, seq, d = x.shape
    n_meta = meta.shape[0]
    depth = w_in.shape[0]
    din = w_in.shape[2] * N_DEV
    dl = din // 6
    dmix = 2 * dl
    wcol = w_in.shape[2]
    wrow = w_out.shape[1]
    mcol = meta.shape[1]
    ccol = conv_a_w.shape[2]
    heads, hd = lru_wr.shape[1], lru_wr.shape[2]
    n_tok = n_meta + seq
    tp = -(-n_tok // TOKEN_TILE) * TOKEN_TILE
    me = 4 * lax.axis_index("x") + 2 * lax.axis_index("y") + lax.axis_index("c")

    bf = lambda a: a.astype(BF16)
    small_mine = _pack([meta, conv_a_w, conv_b_w])
    w_out_b = [bf(w_out[l]) for l in range(depth)]
    w_in_b = [bf(w_in[l]) for l in range(depth)]
    first = _all_gather([w_in_b[0], small_mine], [(d, din), (N_DEV,) + small_mine.shape],
                        [_col_window(wcol), _lead], name="gather_first")
    w_in_full = [first[0]] + [None] * (depth - 1)
    w_out_full = [None] * depth
    land_out = [_place_block(w_out_b[l], cols=False, name=f"place_wout_{l}") for l in range(depth)]
    land_in = [None] + [_place_block(w_in_b[l], cols=True, name=f"place_win_{l}") for l in range(1, depth)]
    parts = [_unpack(first[1][s], [meta.shape, conv_a_w.shape, conv_b_w.shape]) for s in range(N_DEV)]
    meta_full = jnp.concatenate([p[0] for p in parts], axis=1)
    wa_full = jnp.concatenate([p[1] for p in parts], axis=2)
    wb_full = jnp.concatenate([p[2] for p in parts], axis=2)

    push_out = [None] * depth
    push_in = [None] * depth
    w_in_full[0], first_src = lax.optimization_barrier((first[0], w_out_b[0]))
    push_out[0] = _push_start([first_src], [land_out[0]], [_gather_lead], name="gather_wout_0_start")
    token = push_out[0][-1]
    for l in range(1, depth):
        push_in[l] = _push_start([_dep(w_in_b[l], token)], [land_in[l]], [_gather_cols(wcol)],
                                 name=f"gather_win_{l}_start")
        push_out[l] = _push_start([_dep(w_out_b[l], push_in[l][-1])], [land_out[l]], [_gather_lead],
                                  name=f"gather_wout_{l}_start")
        token = push_out[l][-1]

    wr_blk = [_blockdiag(lru_wr[l], GATE_BLOCK) for l in range(depth)]
    wi_blk = [_blockdiag(lru_wi[l], GATE_BLOCK) for l in range(depth)]
    vec = lambda a: a.reshape(1, dl)

    h = jnp.concatenate([meta_full, x[0], jnp.zeros((tp - n_tok, d), F32)], axis=0)
    tgt = jnp.pad(loss_target[0], ((n_meta, tp - n_tok), (0, 0)))
    tm = _tile(tp, 1408)
    saved = []
    for l in range(depth):
        gain = _dep(norm_g[l], token) if l == 0 else norm_g[l]
        hn = _rms_fwd(h, gain, name=f"rms_fwd_{l}")
        if l > 0:
            _, landed = _push_wait(push_in[l], [_gather_cols(wcol)], hn, name=f"gather_win_{l}_wait")
            w_in_full[l] = landed[0]
        u = _matmul(hn, w_in_full[l], tm=tm, tn=_tile(din, 768), tk=d, name=f"mm_u_{l}")
        ca, hs, y = _mixer_fwd(u, wa_full[l], vec(conv_a_b[l]), wr_blk[l], vec(lru_br[l]), wi_blk[l], vec(lru_bi[l]),
                               vec(lru_lambda[l]), wb_full[l], name=f"mixer_fwd_{l}")
        _, landed = _push_wait(push_out[l], [_gather_lead], y, name=f"gather_wout_{l}_wait")
        w_out_full[l] = landed[0].reshape(dmix, d)
        h_next = _matmul(y, w_out_full[l], tm=tm, tn=_tile(d, 512), tk=dmix, add=h, name=f"mm_out_{l}")
        saved.append((h, hn, u, ca, hs, y))
        h = h_next

    dh, dhb, dg_final, loss_part = _loss_head(h, tgt, final_g, n_meta=n_meta, n_tok=n_tok, name="loss_head")
    loss = lax.psum(loss_part[0, 0], ("x", "y", "c"))

    small_grads = [None] * depth
    sent_out = [None] * depth
    sent_in = [None] * depth
    scatter_in = [_scatter_cols(wcol)]
    token = None
    dg_norms = []
    for l in reversed(range(depth)):
        h_in, hn, u, ca, hs, y = saved[l]
        dy = _matmul(dhb, w_out_full[l], tb=True, tm=tm, tn=_tile(dmix, 512), tk=d, dep=token, name=f"mm_dy_{l}")
        dw_out = _matmul(y, dhb, ta=True, tm=_tile(dmix, 1024), tn=_tile(d, 1024), tk=tm, out_dtype=BF16,
                         name=f"mm_dwout_{l}")
        sent_out[l] = _push_start([dw_out.reshape(N_DEV, wrow, d)], [lax.empty((N_DEV - 1, wrow, d), BF16)],
                                  [_scatter_lead], name=f"scatter_wout_{l}_start")
        du, sg, dwr, dwi = _mixer_bwd(u, ca, hs, dy, wa_full[l], wr_blk[l], vec(lru_br[l]), wi_blk[l], vec(lru_bi[l]),
                                      vec(lru_lambda[l]), _dep(wb_full[l], sent_out[l][-1]), name=f"mixer_bwd_{l}")
        small_grads[l] = (sg, dwr, dwi)
        if l == 0:
            early = _pack([
                jnp.stack([small_grads[j][0][SG_BA] for j in range(depth)]),
                jnp.stack([_blockdiag_extract(small_grads[j][1], hd) for j in range(depth)]),
                jnp.stack([small_grads[j][0][SG_BR] for j in range(depth)]),
                jnp.stack([_blockdiag_extract(small_grads[j][2], hd) for j in range(depth)]),
                jnp.stack([small_grads[j][0][SG_BI] for j in range(depth)]),
                jnp.stack([small_grads[j][0][SG_LAM] for j in range(depth)]),
                jnp.stack([small_grads[j][0][SG_WA:SG_WA + 4] for j in range(depth)]),
                jnp.stack([small_grads[j][0][SG_WB:SG_WB + 3] for j in range(depth)]),
                dg_final[0], *dg_norms])
            early_land = lax.dynamic_update_slice(lax.empty((N_DEV,) + early.shape, F32), early[None], (me, 0, 0))
            sent_early = _push_start([early], [early_land], [_gather_lead], name="gather_early_grads_start")
        parts = 2 if l == 0 else 1
        token = sent_early[-1] if l == 0 else None
        sent_in[l] = []
        for p in range(parts):
            dw_in = _matmul(hn, du, ta=True, tm=_tile(d // parts, 1024), tn=_tile(din, 1536), tk=tm, out_dtype=BF16,
                            dep=token, m_part=(p, parts), name=f"mm_dwin_{l}_{p}")
            sent_in[l].append(_push_start([dw_in], [lax.empty((N_DEV - 1, d // parts, wcol), BF16)], scatter_in,
                                          name=f"scatter_win_{l}_{p}_start"))
            token = sent_in[l][-1][-1]
        dhn = _matmul(du, w_in_full[l], tb=True, tm=tm, tn=_tile(d, 1024), tk=_tile(din, 1536), dep=token,
                      name=f"mm_dhn_{l}")
        dh, dhb, dg_norm = _rms_bwd(h_in, dhn, dh, norm_g[l], name=f"rms_bwd_{l}")
        if l > 0:
            dg_norms.append(dg_norm[0])

    late = _pack([dg_norm[0], dh[:n_meta]])
    late_all = _all_gather([late], [(N_DEV,) + late.shape], [_lead], name="gather_late_grads")[0]
    late_sum = _unpack(_slot_sum(late_all, name="sum_late_grads"), [(d,), (n_meta, d)])
    _, early_all = _push_wait(sent_early, [_gather_lead], late_sum[0], name="gather_early_grads_wait")
    early_shapes = [conv_a_b.shape, lru_wr.shape, lru_br.shape, lru_wi.shape, lru_bi.shape, lru_lambda.shape,
                    (depth, 4, dl), (depth, 3, dl), final_g.shape] + [(d,)] * (depth - 1)
    e = _unpack(_slot_sum(early_all[0], name="sum_early_grads"), early_shapes)
    g_norm = jnp.stack([late_sum[0]] + e[9:][::-1])
    g_meta = lax.dynamic_slice_in_dim(late_sum[1], me * mcol, mcol, axis=1)
    g_wa = lax.dynamic_slice_in_dim(e[6], me * ccol, ccol, axis=2)
    g_wb = lax.dynamic_slice_in_dim(e[7], me * ccol, ccol, axis=2)

    small_w = [norm_g, conv_a_b, lru_wr, lru_br, lru_wi, lru_bi, lru_lambda, final_g, meta, conv_a_w, conv_b_w]
    small_m = [m_norm_g, m_conv_a_b, m_lru_wr, m_lru_br, m_lru_wi, m_lru_bi, m_lru_lambda, m_final_g, m_meta,
               m_conv_a_w, m_conv_b_w]
    small_v = [v_norm_g, v_conv_a_b, v_lru_wr, v_lru_br, v_lru_wi, v_lru_bi, v_lru_lambda, v_final_g, v_meta,
               v_conv_a_w, v_conv_b_w]
    small_g = [g_norm, e[0], e[1], e[2], e[3], e[4], e[5], e[8], g_meta, g_wa, g_wb]
    small_out = _adamw(_pack(small_w), _pack(small_g), _pack(small_m), _pack(small_v), name="adamw_small")
    small_shapes = [a.shape for a in small_w]
    s_grad, s_delta, s_m, s_v = [_unpack(o, small_shapes) for o in small_out]

    win_out = None
    wout_out = None
    after = small_out[0]
    for l in reversed(range(depth)):
        src, landed = _push_wait(sent_out[l], [_scatter_lead], after, name=f"scatter_wout_{l}_wait")
        own = lax.dynamic_index_in_dim(src[0], me, 0, keepdims=False)
        wout_out = _adamw(w_out[l], own, m_w_out[l], v_w_out[l], landed=landed[0], layer=l, depth=depth,
                          into=wout_out, name=f"adamw_w_out_{l}")
        after = wout_out[0]
        for p, sent in enumerate(sent_in[l]):
            src, landed = _push_wait(sent, scatter_in, after, name=f"scatter_win_{l}_{p}_wait")
            own = lax.dynamic_slice_in_dim(src[0], me * wcol, wcol, axis=1)
            win_out = _adamw(w_in[l], own, m_w_in[l], v_w_in[l], landed=landed[0], layer=l, depth=depth,
                             into=win_out, row_off=p * own.shape[0], name=f"adamw_w_in_{l}_{p}")
            after = win_out[0]

    names = ["norm_g", "conv_a_b", "lru_wr", "lru_br", "lru_wi", "lru_bi", "lru_lambda", "final_g", "meta",
             "conv_a_w", "conv_b_w"]
    order = ["meta", "norm_g", "w_in", "conv_a_w", "conv_a_b", "lru_wr", "lru_br", "lru_wi", "lru_bi", "lru_lambda",
             "conv_b_w", "w_out", "final_g"]

    def family(idx, small):
        table = {nm: small[i] for i, nm in enumerate(names)}
        table["w_in"] = win_out[idx]
        table["w_out"] = wout_out[idx]
        return [table[nm] for nm in order]

    grad_x = dh[n_meta:n_tok][None]
    return (loss, grad_x, *family(0, s_grad), *family(1, s_delta), *family(2, s_m), *family(3, s_v))
```

```python
import functools

import jax
import jax.numpy as jnp
from jax import lax
from jax.experimental import pallas as pl
from jax.experimental.pallas import tpu as pltpu

F32 = jnp.float32
BF16 = jnp.bfloat16
MESH = pl.DeviceIdType.MESH

N_DEV = 8
RMS_EPS = 1e-6
LRU_C = 8.0
ADAM_LR = 0.001
ADAM_B1 = 0.9
ADAM_B2 = 0.999
ADAM_EPS = 1e-08
ADAM_WD = 0.01
ADAM_STEP = 10

V7X_VMEM_LIMIT = 52 * 1024 * 1024
LANES = 128
SUBLANES = 8
TOKEN_TILE = 384
MIX_ROWS = 128
GATE_BLOCK = 256
SCAN_UNROLL = 4


def _params(sem):
    return pltpu.CompilerParams(dimension_semantics=sem, vmem_limit_bytes=V7X_VMEM_LIMIT)


def _tile(n, target, align=LANES):
    best = None
    for t in range(align, min(n, target) + 1, align):
        if n % t == 0:
            best = t
    return n if best is None else best


def _sigmoid(z):
    return 0.5 * jnp.tanh(0.5 * z) + 0.5


def _softplus(z):
    e = jnp.exp(-jnp.abs(z))
    u = 1.0 + e
    l1p = jnp.where(u == 1.0, e, jnp.log(u) * e / jnp.where(u == 1.0, 1.0, u - 1.0))
    return jnp.maximum(z, 0.0) + l1p


def _matmul(a, b, *, ta=False, tb=False, tm, tn, tk, out_dtype=F32, add=None, dep=None, m_part=None, name):
    m, k = (a.shape[1], a.shape[0]) if ta else a.shape
    m_off = 0
    if m_part is not None:
        assert add is None and m % (m_part[1] * tm) == 0
        m //= m_part[1]
        m_off = m_part[0] * (m // tm)
    n, kb = b.shape if tb else b.shape[::-1]
    assert kb == k
    assert m % tm == 0 and n % tn == 0 and k % tk == 0, (m, n, k, tm, tn, tk)
    nk = k // tk
    a_spec = pl.BlockSpec((tk, tm), lambda i, j, q: (q, i + m_off)) if ta \
        else pl.BlockSpec((tm, tk), lambda i, j, q: (i + m_off, q))
    b_spec = pl.BlockSpec((tn, tk), lambda i, j, q: (j, q)) if tb else pl.BlockSpec((tk, tn), lambda i, j, q: (q, j))
    o_spec = pl.BlockSpec((tm, tn), lambda i, j, q: (i, j))
    o_shape = (m, n)
    dims = (((0 if ta else 1,), (1 if tb else 0,)), ((), ()))
    has_add = add is not None
    has_dep = dep is not None

    def body(*refs):
        if has_dep:
            refs = refs[:-3] + refs[-2:]
        if has_add:
            a_ref, b_ref, add_ref, o_ref, acc_ref = refs
        else:
            a_ref, b_ref, o_ref, acc_ref = refs
        q = pl.program_id(2)
        part = lax.dot_general(a_ref[...], b_ref[...], dims, preferred_element_type=F32)

        def finish(acc):
            if has_add:
                acc = acc + add_ref[...]
            o_ref[...] = acc.astype(out_dtype)

        if nk == 1:
            finish(part)
        else:
            @pl.when(q == 0)
            def _():
                acc_ref[...] = part

            @pl.when(jnp.logical_and(q > 0, q < nk - 1))
            def _():
                acc_ref[...] += part

            @pl.when(q == nk - 1)
            def _():
                finish(acc_ref[...] + part)

    in_specs = [a_spec, b_spec] + ([o_spec] if has_add else [])
    args = (a, b) + ((add,) if has_add else ())
    if has_dep:
        in_specs.append(pl.BlockSpec((SUBLANES, LANES), lambda i, j, q: (0, 0)))
        args += (dep,)
    acc_shape = (tm, tn) if nk > 1 else (SUBLANES, LANES)
    return pl.pallas_call(
        body, name=name,
        grid=(m // tm, n // tn, nk),
        in_specs=in_specs, out_specs=o_spec,
        out_shape=jax.ShapeDtypeStruct(o_shape, out_dtype),
        scratch_shapes=[pltpu.VMEM(acc_shape, F32)],
        compiler_params=_params(("parallel", "parallel", "arbitrary")),
    )(*args)


def _rms_fwd(h, g, *, name):
    tp, d = h.shape
    tr = _tile(tp, 512, SUBLANES)

    def body(h_ref, g_ref, o_ref):
        hv = h_ref[...]
        rstd = lax.rsqrt(jnp.mean(hv * hv, axis=-1, keepdims=True) + RMS_EPS)
        o_ref[...] = (hv * rstd * g_ref[...]).astype(BF16)

    return pl.pallas_call(
        body, name=name, grid=(tp // tr,),
        in_specs=[pl.BlockSpec((tr, d), lambda i: (i, 0)), pl.BlockSpec((1, d), lambda i: (0, 0))],
        out_specs=pl.BlockSpec((tr, d), lambda i: (i, 0)),
        out_shape=jax.ShapeDtypeStruct((tp, d), BF16),
        compiler_params=_params(("parallel",)),
    )(h, g.reshape(1, d))


def _rms_bwd(h, dhn, dout, g, *, name):
    tp, d = h.shape
    tr = _tile(tp, 384, SUBLANES)

    def body(h_ref, dhn_ref, dout_ref, g_ref, dh_ref, dhb_ref, dg_ref):
        hv = h_ref[...]
        rstd = lax.rsqrt(jnp.mean(hv * hv, axis=-1, keepdims=True) + RMS_EPS)
        xhat = hv * rstd
        dn = dhn_ref[...]
        dxhat = dn * g_ref[...]
        dh = dout_ref[...] + rstd * (dxhat - xhat * jnp.mean(dxhat * xhat, axis=-1, keepdims=True))
        dh_ref[...] = dh
        dhb_ref[...] = dh.astype(BF16)
        part = jnp.sum(dn * xhat, axis=0, keepdims=True)

        @pl.when(pl.program_id(0) == 0)
        def _():
            dg_ref[...] = part

        @pl.when(pl.program_id(0) > 0)
        def _():
            dg_ref[...] += part

    row = pl.BlockSpec((tr, d), lambda i: (i, 0))
    vec = pl.BlockSpec((1, d), lambda i: (0, 0))
    return pl.pallas_call(
        body, name=name, grid=(tp // tr,),
        in_specs=[row, row, row, vec],
        out_specs=[row, row, vec],
        out_shape=[jax.ShapeDtypeStruct((tp, d), F32), jax.ShapeDtypeStruct((tp, d), BF16),
                   jax.ShapeDtypeStruct((1, d), F32)],
        compiler_params=_params(("arbitrary",)),
    )(h, dhn, dout, g.reshape(1, d))


def _loss_head(h, tgt, g, *, n_meta, n_tok, name):
    tp, d = h.shape
    tr = _tile(tp, 384, SUBLANES)

    def body(h_ref, t_ref, g_ref, dh_ref, dhb_ref, dg_ref, loss_ref):
        i = pl.program_id(0)
        hv = h_ref[...]
        rstd = lax.rsqrt(jnp.mean(hv * hv, axis=-1, keepdims=True) + RMS_EPS)
        xhat = hv * rstd
        gv = g_ref[...]
        rows = i * tr + lax.broadcasted_iota(jnp.int32, (tr, 1), 0)
        valid = jnp.logical_and(rows >= n_meta, rows < n_tok)
        err = jnp.where(valid, xhat * gv - t_ref[...], 0.0)
        dy = err * (1.0 / d)
        dxhat = dy * gv
        dh = rstd * (dxhat - xhat * jnp.mean(dxhat * xhat, axis=-1, keepdims=True))
        dh_ref[...] = dh
        dhb_ref[...] = dh.astype(BF16)
        dg_part = jnp.sum(dy * xhat, axis=0, keepdims=True)
        per_row = jnp.sum(err * err, axis=-1, keepdims=True) * (1.0 / d)
        loss_part = jnp.broadcast_to(0.5 * jnp.sum(per_row, axis=0, keepdims=True), (SUBLANES, LANES))

        @pl.when(i == 0)
        def _():
            dg_ref[...] = dg_part
            loss_ref[...] = loss_part

        @pl.when(i > 0)
        def _():
            dg_ref[...] += dg_part
            loss_ref[...] += loss_part

    row = pl.BlockSpec((tr, d), lambda i: (i, 0))
    vec = pl.BlockSpec((1, d), lambda i: (0, 0))
    return pl.pallas_call(
        body, name=name, grid=(tp // tr,),
        in_specs=[row, row, vec],
        out_specs=[row, row, vec, pl.BlockSpec((SUBLANES, LANES), lambda i: (0, 0))],
        out_shape=[jax.ShapeDtypeStruct((tp, d), F32), jax.ShapeDtypeStruct((tp, d), BF16),
                   jax.ShapeDtypeStruct((1, d), F32), jax.ShapeDtypeStruct((SUBLANES, LANES), F32)],
        compiler_params=_params(("arbitrary",)),
    )(h, tgt, g.reshape(1, d))


def _shift_down(halo, tile, s):
    if s == 0:
        return tile
    ext = jnp.concatenate([halo, tile], axis=0)
    return pltpu.roll(ext, s, 0)[SUBLANES:]


def _shift_up(tile, head, s):
    if s == 0:
        return tile
    ext = jnp.concatenate([tile, head], axis=0)
    n = ext.shape[0]
    return pltpu.roll(ext, n - s, 0)[: tile.shape[0]]


def _scan_rows_fwd(a, b):
    row = lax.broadcasted_iota(jnp.int32, a.shape, 0)
    for s in (1, 2, 4):
        a_sh = pltpu.roll(a, s, 0)
        b_sh = pltpu.roll(b, s, 0)
        m = row >= s
        b = jnp.where(m, a * b_sh + b, b)
        a = jnp.where(m, a * a_sh, a)
    return a, b


def _scan_rows_bwd(c, d):
    row = lax.broadcasted_iota(jnp.int32, c.shape, 0)
    for s in (1, 2, 4):
        c_sh = pltpu.roll(c, SUBLANES - s, 0)
        d_sh = pltpu.roll(d, SUBLANES - s, 0)
        m = row < SUBLANES - s
        d = jnp.where(m, c * d_sh + d, d)
        c = jnp.where(m, c * c_sh, c)
    return c, d


def _gates(ca, wr, wi, br, bi, sp):
    cab = ca.astype(BF16)
    r = _sigmoid(jnp.dot(cab, wr, preferred_element_type=F32) + br)
    ig = _sigmoid(jnp.dot(cab, wi, preferred_element_type=F32) + bi)
    la = -LRU_C * r * sp
    a = jnp.exp(la)
    mult = jnp.sqrt(-jnp.tanh(la) * (a * a + 1.0))
    return r, ig, a, mult


def _mixer_fwd(u, wa, ba, wr_blk, br, wi_blk, bi, lam, wb, *, name):
    tp, din = u.shape
    dl = din // 6
    tt = MIX_ROWS
    cw = GATE_BLOCK
    nch = dl // cw
    assert tp % tt == 0 and dl % cw == 0

    def body(u_ref, wa_ref, ba_ref, wr_ref, br_ref, wi_ref, bi_ref, lam_ref, wb_ref,
             ca_ref, hs_ref, y_ref, xa_tail, v_tail, h_carry, a_s, b_s):
        @pl.when(pl.program_id(0) == 0)
        def _():
            xa_tail[...] = jnp.zeros_like(xa_tail)
            v_tail[...] = jnp.zeros_like(v_tail)
            h_carry[...] = jnp.zeros_like(h_carry)

        for ch in range(nch):
            cs = slice(ch * cw, (ch + 1) * cw)

            def seg(s):
                return slice(s * dl + ch * cw, s * dl + (ch + 1) * cw)

            xa = u_ref[:, seg(0)]
            halo = xa_tail[:, cs]
            ca = ba_ref[:, cs] + wa_ref[3:4, cs] * xa
            for kk in range(3):
                ca = ca + wa_ref[kk:kk + 1, cs] * _shift_down(halo, xa, 3 - kk)
            xa_tail[:, cs] = xa[tt - SUBLANES:]
            ca_ref[:, cs] = ca
            sp = _softplus(-lam_ref[:, cs])
            _, ig, a, mult = _gates(ca, wr_ref[ch], wi_ref[ch], br_ref[:, cs], bi_ref[:, cs], sp)
            a_s[:, cs] = a
            b_s[:, cs] = mult * (ig * ca)

            bv = u_ref[:, seg(2)]
            v = u_ref[:, seg(3)] * u_ref[:, seg(4)]
            gb = u_ref[:, seg(5)]
            vh = v_tail[:, cs]
            cb = wb_ref[2:3, cs] * v
            for kk in range(2):
                cb = cb + wb_ref[kk:kk + 1, cs] * _shift_down(vh, v, 2 - kk)
            v_tail[:, cs] = v[tt - SUBLANES:]
            y_ref[:, dl + ch * cw: dl + (ch + 1) * cw] = (bv * cb * (gb * _sigmoid(gb))).astype(BF16)

        def group(gi, hprev):
            rows = pl.ds(pl.multiple_of(gi * SUBLANES, SUBLANES), SUBLANES)
            a8, b8 = _scan_rows_fwd(a_s[rows, :], b_s[rows, :])
            h8 = b8 + a8 * hprev
            hs_ref[rows, :] = h8
            return jnp.broadcast_to(h8[SUBLANES - 1:SUBLANES, :], h8.shape)

        h_carry[...] = lax.fori_loop(0, tt // SUBLANES, group, h_carry[...], unroll=SCAN_UNROLL)

        for ch in range(nch):
            cs = slice(ch * cw, (ch + 1) * cw)
            ga = u_ref[:, dl + ch * cw: dl + (ch + 1) * cw]
            y_ref[:, cs] = (hs_ref[:, cs] * (ga * _sigmoid(ga))).astype(BF16)

    row = lambda w: pl.BlockSpec((tt, w), lambda i: (i, 0))
    full = lambda shp: pl.BlockSpec(shp, lambda i: tuple(0 for _ in shp))
    return pl.pallas_call(
        body, name=name, grid=(tp // tt,),
        in_specs=[row(din), full((4, dl)), full((1, dl)), full((nch, cw, cw)), full((1, dl)),
                  full((nch, cw, cw)), full((1, dl)), full((1, dl)), full((3, dl))],
        out_specs=[row(dl), row(dl), row(2 * dl)],
        out_shape=[jax.ShapeDtypeStruct((tp, dl), F32), jax.ShapeDtypeStruct((tp, dl), F32),
                   jax.ShapeDtypeStruct((tp, 2 * dl), BF16)],
        scratch_shapes=[pltpu.VMEM((SUBLANES, dl), F32), pltpu.VMEM((SUBLANES, dl), F32),
                        pltpu.VMEM((SUBLANES, dl), F32), pltpu.VMEM((tt, dl), F32), pltpu.VMEM((tt, dl), F32)],
        compiler_params=_params(("arbitrary",)),
    )(u, wa, ba, wr_blk, br, wi_blk, bi, lam, wb)


SG_WA, SG_BA, SG_BR, SG_BI, SG_LAM, SG_WB, SG_ROWS = 0, 4, 5, 6, 7, 8, 16


def _mixer_bwd(u, ca, hs, dy, wa, wr_blk, br, wi_blk, bi, lam, wb, *, name):
    tp, din = u.shape
    dl = din // 6
    tt = MIX_ROWS
    cw = GATE_BLOCK
    nch = dl // cw
    nt = tp // tt
    hb = tt // SUBLANES
    tn_dims = (((0,), (0,)), ((), ()))
    nt_dims = (((1,), (1,)), ((), ()))

    def body(u_ref, uh_ref, ca_ref, hs_ref, hsh_ref, dy_ref, wa_ref, wr_ref, br_ref, wi_ref, bi_ref, lam_ref, wb_ref,
             du_ref, sg_ref, dwr_ref, dwi_ref,
             g_carry, a_head, dca_head, dcb_head, r_s, i_s, a_s, an_s, d_s, g_s):
        i = pl.program_id(0)
        first_tile = i == nt - 1

        @pl.when(i == 0)
        def _():
            for ref in (g_carry, a_head, dca_head, dcb_head, sg_ref, dwr_ref, dwi_ref):
                ref[...] = jnp.zeros_like(ref)

        def halo_of(x):
            return jnp.where(first_tile, 0.0, x)

        for ch in range(nch):
            cs = slice(ch * cw, (ch + 1) * cw)
            cav = ca_ref[:, cs]
            sp = _softplus(-lam_ref[:, cs])
            r, ig, a, _ = _gates(cav, wr_ref[ch], wi_ref[ch], br_ref[:, cs], bi_ref[:, cs], sp)
            r_s[:, cs] = r
            i_s[:, cs] = ig
            a_s[:, cs] = a
            an_s[:, cs] = _shift_up(a, a_head[:, cs], 1)
            a_head[:, cs] = a[:SUBLANES]
            ga = u_ref[:, dl + ch * cw: dl + (ch + 1) * cw]
            d_s[:, cs] = dy_ref[:, cs] * (ga * _sigmoid(ga))

        def group(k, gnext):
            gi = tt // SUBLANES - 1 - k
            rows = pl.ds(pl.multiple_of(gi * SUBLANES, SUBLANES), SUBLANES)
            c8, d8 = _scan_rows_bwd(an_s[rows, :], d_s[rows, :])
            g8 = d8 + c8 * gnext
            g_s[rows, :] = g8
            return jnp.broadcast_to(g8[0:1, :], g8.shape)

        g_carry[...] = lax.fori_loop(0, tt // SUBLANES, group, g_carry[...], unroll=SCAN_UNROLL)

        def acc_row(r0, val):
            sg_ref[r0:r0 + 1, cs_cur[0]] += jnp.sum(val, axis=0, keepdims=True)

        cs_cur = [None]
        for ch in range(nch):
            cs = slice(ch * cw, (ch + 1) * cw)
            cs_cur[0] = cs

            def seg(s):
                return slice(s * dl + ch * cw, s * dl + (ch + 1) * cw)

            cav = ca_ref[:, cs]
            r = r_s[:, cs]
            ig = i_s[:, cs]
            a = a_s[:, cs]
            g = g_s[:, cs]
            hsv = hs_ref[:, cs]
            lamv = lam_ref[:, cs]
            sp = _softplus(-lamv)
            la = -LRU_C * r * sp
            e2 = a * a
            one_m_e2 = -jnp.tanh(la) * (e2 + 1.0)
            mult = jnp.sqrt(one_m_e2)
            hprev = _shift_down(halo_of(hsh_ref[:, cs]), hsv, 1)
            dla = g * hprev * a - g * (ig * cav) * e2 * lax.rsqrt(one_m_e2)
            gm = g * mult
            dzi = gm * cav * ig * (1.0 - ig)
            dca = gm * ig
            dzr = dla * (-LRU_C * sp) * r * (1.0 - r)
            acc_row(SG_LAM, dla * (-LRU_C * r) * (-_sigmoid(-lamv)))
            acc_row(SG_BR, dzr)
            acc_row(SG_BI, dzi)
            dzr_b = dzr.astype(BF16)
            dzi_b = dzi.astype(BF16)
            cab = cav.astype(BF16)
            dca = dca + lax.dot_general(dzr_b, wr_ref[ch], nt_dims, preferred_element_type=F32)
            dca = dca + lax.dot_general(dzi_b, wi_ref[ch], nt_dims, preferred_element_type=F32)
            dwr_ref[ch] += lax.dot_general(cab, dzr_b, tn_dims, preferred_element_type=F32)
            dwi_ref[ch] += lax.dot_general(cab, dzi_b, tn_dims, preferred_element_type=F32)
            acc_row(SG_BA, dca)
            xa = u_ref[:, seg(0)]
            xah = halo_of(uh_ref[:, seg(0)])
            head = dca_head[:, cs]
            dxa = wa_ref[3:4, cs] * dca
            acc_row(SG_WA + 3, dca * xa)
            for kk in range(3):
                acc_row(SG_WA + kk, dca * _shift_down(xah, xa, 3 - kk))
                dxa = dxa + wa_ref[kk:kk + 1, cs] * _shift_up(dca, head, 3 - kk)
            dca_head[:, cs] = dca[:SUBLANES]
            ga = u_ref[:, seg(1)]
            sga = _sigmoid(ga)
            dga = dy_ref[:, cs] * hsv * (sga * (1.0 + ga * (1.0 - sga)))
            du_ref[:, seg(0)] = dxa.astype(BF16)
            du_ref[:, seg(1)] = dga.astype(BF16)

            bv = u_ref[:, seg(2)]
            cv = u_ref[:, seg(3)]
            xb = u_ref[:, seg(4)]
            gb = u_ref[:, seg(5)]
            dyb = dy_ref[:, dl + ch * cw: dl + (ch + 1) * cw]
            v = cv * xb
            vh = halo_of(uh_ref[:, seg(3)] * uh_ref[:, seg(4)])
            v1 = _shift_down(vh, v, 1)
            v2 = _shift_down(vh, v, 2)
            cb = wb_ref[2:3, cs] * v + wb_ref[1:2, cs] * v1 + wb_ref[0:1, cs] * v2
            sgb = _sigmoid(gb)
            sl = gb * sgb
            dcb = dyb * bv * sl
            du_ref[:, seg(2)] = (dyb * cb * sl).astype(BF16)
            du_ref[:, seg(5)] = (dyb * bv * cb * (sgb * (1.0 + gb * (1.0 - sgb)))).astype(BF16)
            acc_row(SG_WB + 2, dcb * v)
            acc_row(SG_WB + 1, dcb * v1)
            acc_row(SG_WB + 0, dcb * v2)
            bhead = dcb_head[:, cs]
            dv = wb_ref[2:3, cs] * dcb + wb_ref[1:2, cs] * _shift_up(dcb, bhead, 1) \
                + wb_ref[0:1, cs] * _shift_up(dcb, bhead, 2)
            dcb_head[:, cs] = dcb[:SUBLANES]
            du_ref[:, seg(3)] = (dv * xb).astype(BF16)
            du_ref[:, seg(4)] = (dv * cv).astype(BF16)

    rev = lambda w: pl.BlockSpec((tt, w), lambda i: (nt - 1 - i, 0))
    halo = lambda w: pl.BlockSpec((SUBLANES, w), lambda i: (jnp.maximum((nt - 1 - i) * hb - 1, 0), 0))
    full = lambda shp: pl.BlockSpec(shp, lambda i: tuple(0 for _ in shp))
    vm = lambda r: pltpu.VMEM((r, dl), F32)
    return pl.pallas_call(
        body, name=name, grid=(nt,),
        in_specs=[rev(din), halo(din), rev(dl), rev(dl), halo(dl), rev(2 * dl), full((4, dl)),
                  full((nch, cw, cw)), full((1, dl)), full((nch, cw, cw)), full((1, dl)), full((1, dl)), full((3, dl))],
        out_specs=[rev(din), full((SG_ROWS, dl)), full((nch, cw, cw)), full((nch, cw, cw))],
        out_shape=[jax.ShapeDtypeStruct((tp, din), BF16), jax.ShapeDtypeStruct((SG_ROWS, dl), F32),
                   jax.ShapeDtypeStruct((nch, cw, cw), F32), jax.ShapeDtypeStruct((nch, cw, cw), F32)],
        scratch_shapes=[vm(SUBLANES), vm(SUBLANES), vm(SUBLANES), vm(SUBLANES),
                        vm(tt), vm(tt), vm(tt), vm(tt), vm(tt), vm(tt)],
        compiler_params=_params(("arbitrary",)),
    )(u, u, ca, hs, hs, dy, wa, wr_blk, br, wi_blk, bi, lam, wb)


def _adamw(w, g, m, v, *, name, landed=None, layer=None, depth=None, into=None, row_off=0):
    r, c = w.shape
    rows = g.shape[0]
    tr = _tile(rows, 256, 2 * SUBLANES)
    assert row_off % tr == 0
    boff = row_off // tr
    bc1 = 1.0 - ADAM_B1 ** ADAM_STEP
    bc2 = 1.0 - ADAM_B2 ** ADAM_STEP
    slots = landed is not None

    def body(*refs):
        if into is not None:
            refs = refs[:-8] + refs[-4:]
        if slots:
            w_ref, g_ref, l_ref, m_ref, v_ref, grad_ref, delta_ref, nm_ref, nv_ref = refs
            gv = g_ref[...].astype(F32)
            for s in range(N_DEV - 1):
                gv = gv + l_ref[s].astype(F32)
        else:
            w_ref, g_ref, m_ref, v_ref, grad_ref, delta_ref, nm_ref, nv_ref = refs
            gv = g_ref[...]
        wv = w_ref[...]
        mn = ADAM_B1 * m_ref[...] + (1.0 - ADAM_B1) * gv
        vn = ADAM_B2 * v_ref[...] + (1.0 - ADAM_B2) * (gv * gv)
        m_hat = mn / bc1
        v_hat = vn / bc2
        grad_ref[...] = gv
        delta_ref[...] = -ADAM_LR * (m_hat / (jnp.sqrt(v_hat) + ADAM_EPS) + ADAM_WD * wv)
        nm_ref[...] = mn
        nv_ref[...] = vn

    blk = pl.BlockSpec((tr, c), lambda i: (i + boff, 0))
    g_blk = pl.BlockSpec((tr, c), lambda i: (i, 0))
    l_spec = [pl.BlockSpec((N_DEV - 1, tr, c), lambda i: (0, i, 0))] if slots else []
    args = (w, g, landed, m, v) if slots else (w, g, m, v)
    in_specs = [blk, g_blk] + l_spec + [blk, blk]
    if depth is None:
        shp = jax.ShapeDtypeStruct((r, c), F32)
        out_blk = blk
    else:
        shp = jax.ShapeDtypeStruct((depth, r, c), F32)
        out_blk = pl.BlockSpec((None, tr, c), lambda i: (layer, i + boff, 0))
    aliases = {}
    if into is not None:
        aliases = {len(args) + j: j for j in range(4)}
        in_specs = in_specs + [ANY] * 4
        args = args + tuple(into)
    return pl.pallas_call(
        body, name=name, grid=(rows // tr,),
        in_specs=in_specs, out_specs=[out_blk] * 4,
        out_shape=[shp] * 4, input_output_aliases=aliases,
        compiler_params=_params(("parallel",)),
    )(*args)


def _slot_sum(g, *, name):
    _, r, c = g.shape
    tr = _tile(r, 512, SUBLANES)

    def body(g_ref, o_ref):
        gv = g_ref[0]
        for s in range(1, N_DEV):
            gv = gv + g_ref[s]
        o_ref[...] = gv

    return pl.pallas_call(
        body, name=name, grid=(r // tr,),
        in_specs=[pl.BlockSpec((N_DEV, tr, c), lambda i: (0, i, 0))],
        out_specs=pl.BlockSpec((tr, c), lambda i: (i, 0)),
        out_shape=jax.ShapeDtypeStruct((r, c), F32),
        compiler_params=_params(("parallel",)),
    )(g)


def _mesh_pos():
    x, y, c = lax.axis_index("x"), lax.axis_index("y"), lax.axis_index("c")
    return x, y, c, 4 * x + 2 * y + c


ANY = pl.BlockSpec(memory_space=pl.ANY)


def _all_gather(srcs, out_shapes, views, *, name, place=()):
    n = len(srcs)
    npl = len(place)

    def body(*refs):
        src = refs[:n]
        psrc = refs[n:n + npl]
        dst = refs[n + npl:2 * n + npl]
        pdst = refs[2 * n + npl:2 * (n + npl)]
        send_sems, recv_sems, local_sems = refs[2 * (n + npl):]
        x, y, c, me = _mesh_pos()
        sibling = (x, y, 1 - c)
        chips = [(1 - x, y), (x, 1 - y), (1 - x, 1 - y)]

        def dev(px, py, pc):
            return 4 * px + 2 * py + pc

        def copy(a, k, block, to, from_src=False):
            win = views[a](dst[a], dev(*block))
            return pltpu.make_async_remote_copy(
                src_ref=src[a] if from_src else win, dst_ref=win,
                send_sem=send_sems.at[a * 7 + k], recv_sem=recv_sems.at[a * 7 + k],
                device_id=to, device_id_type=MESH)

        mine = [pltpu.make_async_copy(src[a], views[a](dst[a], me), local_sems.at[a]) for a in range(n)]
        mine += [pltpu.make_async_copy(psrc[j], place[j][2](pdst[j], me), local_sems.at[n + j]) for j in range(npl)]
        started = []
        for a in range(n):
            mine[a].start()
            first = [copy(a, 0, (x, y, c), sibling, True)]
            first += [copy(a, 1 + j, (x, y, c), (*chip, c), True) for j, chip in enumerate(chips)]
            for cp in first:
                cp.start()
            started += first
        for cp in mine[n:]:
            cp.start()
        for a in range(n):
            for j, chip in enumerate(chips):
                copy(a, 1 + j, (*chip, c), (x, y, c)).wait_recv()
                fwd = copy(a, 4 + j, (*chip, c), sibling)
                fwd.start()
                started.append(fwd)
        for a in range(n):
            copy(a, 0, (x, y, 1 - c), (x, y, c)).wait_recv()
            for j, chip in enumerate(chips):
                copy(a, 4 + j, (*chip, 1 - c), (x, y, c)).wait_recv()
        for cp in started:
            cp.wait_send()
        for cp in mine:
            cp.wait()

    return pl.pallas_call(
        body, name=name,
        in_specs=[ANY] * (n + npl), out_specs=[ANY] * (n + npl),
        out_shape=[jax.ShapeDtypeStruct(s, x.dtype) for s, x in zip(out_shapes, srcs)]
        + [jax.ShapeDtypeStruct(shape, arr.dtype) for arr, shape, _ in place],
        scratch_shapes=[pltpu.SemaphoreType.DMA((7 * n,)), pltpu.SemaphoreType.DMA((7 * n,)),
                        pltpu.SemaphoreType.DMA((n + npl,))],
    )(*srcs, *[arr for arr, _, _ in place])


HBM = pl.BlockSpec(memory_space=pltpu.HBM)
SEM = pl.BlockSpec(memory_space=pltpu.SEMAPHORE)
EFFECT = pltpu.SideEffectType.DATAFLOW_SIDE_EFFECTING


def _peer_copies(n, wins, src, land, send_sems, recv_sems):
    x, y, c, me = _mesh_pos()
    out = []
    for a in range(n):
        for k in range(1, N_DEV):
            px = 1 - x if k & 4 else x
            py = 1 - y if k & 2 else y
            pc = 1 - c if k & 1 else c
            s_win, d_win = wins[a](src[a], land[a], me, 4 * px + 2 * py + pc, k)
            out.append(pltpu.make_async_remote_copy(
                src_ref=s_win, dst_ref=d_win,
                send_sem=send_sems.at[a * 7 + k - 1], recv_sem=recv_sems.at[a * 7 + k - 1],
                device_id=(px, py, pc), device_id_type=MESH))
    return out


def _push_start(srcs, lands, wins, *, name):
    n = len(srcs)

    def body(*refs):
        src = refs[:n]
        land = refs[n:2 * n]
        send_sems, recv_sems = refs[2 * n], refs[2 * n + 1]
        token = refs[-1]
        for cp in _peer_copies(n, wins, src, land, send_sems, recv_sems):
            cp.start()
        token[...] = jnp.zeros_like(token)

    bufs = (*srcs, *lands)
    return pl.pallas_call(
        body, name=name,
        out_shape=(pltpu.SemaphoreType.DMA((7 * n,)), pltpu.SemaphoreType.DMA((7 * n,)),
                   *[pltpu.HBM(v.shape, v.dtype) for v in bufs], jax.ShapeDtypeStruct((SUBLANES, LANES), F32)),
        in_specs=[HBM] * (2 * n),
        out_specs=(SEM, SEM, *[HBM] * (2 * n), pl.BlockSpec(memory_space=pltpu.VMEM)),
        input_output_aliases={i: 2 + i for i in range(2 * n)},
        compiler_params=pltpu.CompilerParams(has_side_effects=EFFECT),
    )(*[pltpu.with_memory_space_constraint(v, pltpu.HBM) for v in bufs])


def _push_wait(handle, wins, after, *, name):
    send_sems, recv_sems, *bufs, _ = handle
    n = len(bufs) // 2

    def body(*refs):
        src = refs[:n]
        land = refs[n:2 * n]
        for cp in _peer_copies(n, wins, src, land, refs[2 * n], refs[2 * n + 1]):
            cp.wait_send()
            cp.wait_recv()

    outs = pl.pallas_call(
        body, name=name,
        out_shape=tuple(pltpu.HBM(v.shape, v.dtype) for v in bufs),
        in_specs=[HBM] * (2 * n) + [SEM, SEM, ANY],
        out_specs=tuple([HBM] * (2 * n)),
        input_output_aliases={i: i for i in range(2 * n)},
        compiler_params=pltpu.CompilerParams(has_side_effects=EFFECT),
    )(*bufs, send_sems, recv_sems, after)
    return outs[:n], outs[n:]


def _gather_lead(src, land, me, peer, k):
    return src, land.at[me]


def _gather_cols(width):
    def win(src, land, me, peer, k):
        return src, land.at[:, pl.ds(me * width, width)]
    return win


def _scatter_lead(src, land, me, peer, k):
    return src.at[peer], land.at[k - 1]


def _scatter_cols(width):
    def win(src, land, me, peer, k):
        return src.at[:, pl.ds(peer * width, width)], land.at[k - 1]
    return win


def _place_block(own, *, cols, name):
    rows, width = own.shape
    tr = _tile(rows, 512, 2 * SUBLANES)
    _, _, _, me = _mesh_pos()

    def body(me_ref, x_ref, o_ref):
        o_ref[...] = x_ref[...]

    if cols:
        out_spec = pl.BlockSpec((tr, width), lambda i, me_ref: (i, me_ref[0]))
        shape = (rows, N_DEV * width)
    else:
        out_spec = pl.BlockSpec((None, tr, width), lambda i, me_ref: (me_ref[0], i, 0))
        shape = (N_DEV, rows, width)
    return pl.pallas_call(
        body, name=name,
        grid_spec=pltpu.PrefetchScalarGridSpec(
            num_scalar_prefetch=1, grid=(rows // tr,),
            in_specs=[pl.BlockSpec((tr, width), lambda i, me_ref: (i, 0))], out_specs=out_spec),
        out_shape=jax.ShapeDtypeStruct(shape, own.dtype),
        compiler_params=_params(("arbitrary",)),
    )(me.astype(jnp.int32).reshape(1), own)


def _dep(x, token):
    return x + token[0, 0].astype(x.dtype)


def _lead(ref, d):
    return ref.at[d]


def _col_window(width):
    def view(ref, d):
        return ref.at[:, pl.ds(d * width, width)]
    return view


def _pack(arrs):
    flat = jnp.concatenate([a.reshape(-1).astype(F32) for a in arrs])
    n = flat.shape[0]
    rows = -(-n // (SUBLANES * LANES)) * SUBLANES
    return jnp.pad(flat, (0, rows * LANES - n)).reshape(rows, LANES)


def _unpack(buf, shapes):
    flat = buf.reshape(-1)
    out, off = [], 0
    for s in shapes:
        n = 1
        for q in s:
            n *= q
        out.append(flat[off:off + n].reshape(s))
        off += n
    return out


def _blockdiag(w, cw):
    h, hd, _ = w.shape
    per = cw // hd
    wg = w.reshape(h // per, per, hd, hd)
    eye = jnp.eye(per, dtype=w.dtype)
    blk = jnp.einsum("gpij,pq->gpiqj", wg, eye)
    return blk.reshape(h // per, cw, cw).astype(BF16)


def _blockdiag_extract(g, hd):
    n, cw, _ = g.shape
    per = cw // hd
    g5 = g.reshape(n, per, hd, per, hd)
    idx = jnp.arange(per)
    return g5[:, idx, :, idx, :].transpose(1, 0, 2, 3).reshape(n * per, hd, hd)


def kernel(x, meta, norm_g, w_in, conv_a_w, conv_a_b, lru_wr, lru_br, lru_wi, lru_bi, lru_lambda, conv_b_w, w_out, final_g, loss_target, m_meta, m_norm_g, m_w_in, m_conv_a_w, m_conv_a_b, m_lru_wr, m_lru_br, m_lru_wi, m_lru_bi, m_lru_lambda, m_conv_b_w, m_w_out, m_final_g, v_meta, v_norm_g, v_w_in, v_conv_a_w, v_conv_a_b, v_lru_wr, v_lru_br, v_lru_wi, v_lru_bi, v_lru_lambda, v_conv_b_w, v_w_out, v_final_g):
    _, seq, d = x.shape
    n_meta = meta.shape[0]
    depth = w_in.shape[0]
    din = w_in.shape[2] * N_DEV
    dl = din // 6
    dmix = 2 * dl
    wcol = w_in.shape[2]
    wrow = w_out.shape[1]
    mcol = meta.shape[1]
    ccol = conv_a_w.shape[2]
    heads, hd = lru_wr.shape[1], lru_wr.shape[2]
    n_tok = n_meta + seq
    tp = -(-n_tok // TOKEN_TILE) * TOKEN_TILE
    me = 4 * lax.axis_index("x") + 2 * lax.axis_index("y") + lax.axis_index("c")

    bf = lambda a: a.astype(BF16)
    small_mine = _pack([meta, conv_a_w, conv_b_w])
    w_out_b = [bf(w_out[l]) for l in range(depth)]
    w_in_b = [bf(w_in[l]) for l in range(depth)]
    first = _all_gather([w_in_b[0], small_mine], [(d, din), (N_DEV,) + small_mine.shape],
                        [_col_window(wcol), _lead], name="gather_first")
    w_in_full = [first[0]] + [None] * (depth - 1)
    w_out_full = [None] * depth
    land_out = [_place_block(w_out_b[l], cols=False, name=f"place_wout_{l}") for l in range(depth)]
    land_in = [None] + [_place_block(w_in_b[l], cols=True, name=f"place_win_{l}") for l in range(1, depth)]
    parts = [_unpack(first[1][s], [meta.shape, conv_a_w.shape, conv_b_w.shape]) for s in range(N_DEV)]
    meta_full = jnp.concatenate([p[0] for p in parts], axis=1)
    wa_full = jnp.concatenate([p[1] for p in parts], axis=2)
    wb_full = jnp.concatenate([p[2] for p in parts], axis=2)

    push_out = [None] * depth
    push_in = [None] * depth
    w_in_full[0], first_src = lax.optimization_barrier((first[0], w_out_b[0]))
    push_out[0] = _push_start([first_src], [land_out[0]], [_gather_lead], name="gather_wout_0_start")
    token = push_out[0][-1]
    for l in range(1, depth):
        push_in[l] = _push_start([_dep(w_in_b[l], token)], [land_in[l]], [_gather_cols(wcol)],
                                 name=f"gather_win_{l}_start")
        push_out[l] = _push_start([_dep(w_out_b[l], push_in[l][-1])], [land_out[l]], [_gather_lead],
                                  name=f"gather_wout_{l}_start")
        token = push_out[l][-1]

    wr_blk = [_blockdiag(lru_wr[l], GATE_BLOCK) for l in range(depth)]
    wi_blk = [_blockdiag(lru_wi[l], GATE_BLOCK) for l in range(depth)]
    vec = lambda a: a.reshape(1, dl)

    h = jnp.concatenate([meta_full, x[0], jnp.zeros((tp - n_tok, d), F32)], axis=0)
    tgt = jnp.pad(loss_target[0], ((n_meta, tp - n_tok), (0, 0)))
    tm = _tile(tp, 1408)
    saved = []
    for l in range(depth):
        gain = _dep(norm_g[l], token) if l == 0 else norm_g[l]
        hn = _rms_fwd(h, gain, name=f"rms_fwd_{l}")
        if l > 0:
            _, landed = _push_wait(push_in[l], [_gather_cols(wcol)], hn, name=f"gather_win_{l}_wait")
            w_in_full[l] = landed[0]
        u = _matmul(hn, w_in_full[l], tm=tm, tn=_tile(din, 768), tk=d, name=f"mm_u_{l}")
        ca, hs, y = _mixer_fwd(u, wa_full[l], vec(conv_a_b[l]), wr_blk[l], vec(lru_br[l]), wi_blk[l], vec(lru_bi[l]),
                               vec(lru_lambda[l]), wb_full[l], name=f"mixer_fwd_{l}")
        _, landed = _push_wait(push_out[l], [_gather_lead], y, name=f"gather_wout_{l}_wait")
        w_out_full[l] = landed[0].reshape(dmix, d)
        h_next = _matmul(y, w_out_full[l], tm=tm, tn=_tile(d, 512), tk=dmix, add=h, name=f"mm_out_{l}")
        saved.append((h, hn, u, ca, hs, y))
        h = h_next

    dh, dhb, dg_final, loss_part = _loss_head(h, tgt, final_g, n_meta=n_meta, n_tok=n_tok, name="loss_head")
    loss = lax.psum(loss_part[0, 0], ("x", "y", "c"))

    small_grads = [None] * depth
    sent_out = [None] * depth
    sent_in = [None] * depth
    scatter_in = [_scatter_cols(wcol)]
    token = None
    dg_norms = []
    for l in reversed(range(depth)):
        h_in, hn, u, ca, hs, y = saved[l]
        dy = _matmul(dhb, w_out_full[l], tb=True, tm=tm, tn=_tile(dmix, 512), tk=d, dep=token, name=f"mm_dy_{l}")
        dw_out = _matmul(y, dhb, ta=True, tm=_tile(dmix, 1024), tn=_tile(d, 1024), tk=tm, out_dtype=BF16,
                         name=f"mm_dwout_{l}")
        sent_out[l] = _push_start([dw_out.reshape(N_DEV, wrow, d)], [lax.empty((N_DEV - 1, wrow, d), BF16)],
                                  [_scatter_lead], name=f"scatter_wout_{l}_start")
        du, sg, dwr, dwi = _mixer_bwd(u, ca, hs, dy, wa_full[l], wr_blk[l], vec(lru_br[l]), wi_blk[l], vec(lru_bi[l]),
                                      vec(lru_lambda[l]), _dep(wb_full[l], sent_out[l][-1]), name=f"mixer_bwd_{l}")
        small_grads[l] = (sg, dwr, dwi)
        if l == 0:
            early = _pack([
                jnp.stack([small_grads[j][0][SG_BA] for j in range(depth)]),
                jnp.stack([_blockdiag_extract(small_grads[j][1], hd) for j in range(depth)]),
                jnp.stack([small_grads[j][0][SG_BR] for j in range(depth)]),
                jnp.stack([_blockdiag_extract(small_grads[j][2], hd) for j in range(depth)]),
                jnp.stack([small_grads[j][0][SG_BI] for j in range(depth)]),
                jnp.stack([small_grads[j][0][SG_LAM] for j in range(depth)]),
                jnp.stack([small_grads[j][0][SG_WA:SG_WA + 4] for j in range(depth)]),
                jnp.stack([small_grads[j][0][SG_WB:SG_WB + 3] for j in range(depth)]),
                dg_final[0], *dg_norms])
            early_land = lax.dynamic_update_slice(lax.empty((N_DEV,) + early.shape, F32), early[None], (me, 0, 0))
            sent_early = _push_start([early], [early_land], [_gather_lead], name="gather_early_grads_start")
        parts = 2 if l == 0 else 1
        token = sent_early[-1] if l == 0 else None
        sent_in[l] = []
        for p in range(parts):
            dw_in = _matmul(hn, du, ta=True, tm=_tile(d // parts, 1024), tn=_tile(din, 1536), tk=tm, out_dtype=BF16,
                            dep=token, m_part=(p, parts), name=f"mm_dwin_{l}_{p}")
            sent_in[l].append(_push_start([dw_in], [lax.empty((N_DEV - 1, d // parts, wcol), BF16)], scatter_in,
                                          name=f"scatter_win_{l}_{p}_start"))
            token = sent_in[l][-1][-1]
        dhn = _matmul(du, w_in_full[l], tb=True, tm=tm, tn=_tile(d, 1024), tk=_tile(din, 1536), dep=token,
                      name=f"mm_dhn_{l}")
        dh, dhb, dg_norm = _rms_bwd(h_in, dhn, dh, norm_g[l], name=f"rms_bwd_{l}")
        if l > 0:
            dg_norms.append(dg_norm[0])

    late = _pack([dg_norm[0], dh[:n_meta]])
    late_all = _all_gather([late], [(N_DEV,) + late.shape], [_lead], name="gather_late_grads")[0]
    late_sum = _unpack(_slot_sum(late_all, name="sum_late_grads"), [(d,), (n_meta, d)])
    _, early_all = _push_wait(sent_early, [_gather_lead], late_sum[0], name="gather_early_grads_wait")
    early_shapes = [conv_a_b.shape, lru_wr.shape, lru_br.shape, lru_wi.shape, lru_bi.shape, lru_lambda.shape,
                    (depth, 4, dl), (depth, 3, dl), final_g.shape] + [(d,)] * (depth - 1)
    e = _unpack(_slot_sum(early_all[0], name="sum_early_grads"), early_shapes)
    g_norm = jnp.stack([late_sum[0]] + e[9:][::-1])
    g_meta = lax.dynamic_slice_in_dim(late_sum[1], me * mcol, mcol, axis=1)
    g_wa = lax.dynamic_slice_in_dim(e[6], me * ccol, ccol, axis=2)
    g_wb = lax.dynamic_slice_in_dim(e[7], me * ccol, ccol, axis=2)

    small_w = [norm_g, conv_a_b, lru_wr, lru_br, lru_wi, lru_bi, lru_lambda, final_g, meta, conv_a_w, conv_b_w]
    small_m = [m_norm_g, m_conv_a_b, m_lru_wr, m_lru_br, m_lru_wi, m_lru_bi, m_lru_lambda, m_final_g, m_meta,
               m_conv_a_w, m_conv_b_w]
    small_v = [v_norm_g, v_conv_a_b, v_lru_wr, v_lru_br, v_lru_wi, v_lru_bi, v_lru_lambda, v_final_g, v_meta,
               v_conv_a_w, v_conv_b_w]
    small_g = [g_norm, e[0], e[1], e[2], e[3], e[4], e[5], e[8], g_meta, g_wa, g_wb]
    small_out = _adamw(_pack(small_w), _pack(small_g), _pack(small_m), _pack(small_v), name="adamw_small")
    small_shapes = [a.shape for a in small_w]
    s_grad, s_delta, s_m, s_v = [_unpack(o, small_shapes) for o in small_out]

    win_out = None
    wout_out = None
    after = small_out[0]
    for l in reversed(range(depth)):
        src, landed = _push_wait(sent_out[l], [_scatter_lead], after, name=f"scatter_wout_{l}_wait")
        own = lax.dynamic_index_in_dim(src[0], me, 0, keepdims=False)
        wout_out = _adamw(w_out[l], own, m_w_out[l], v_w_out[l], landed=landed[0], layer=l, depth=depth,
                          into=wout_out, name=f"adamw_w_out_{l}")
        after = wout_out[0]
        for p, sent in enumerate(sent_in[l]):
            src, landed = _push_wait(sent, scatter_in, after, name=f"scatter_win_{l}_{p}_wait")
            own = lax.dynamic_slice_in_dim(src[0], me * wcol, wcol, axis=1)
            win_out = _adamw(w_in[l], own, m_w_in[l], v_w_in[l], landed=landed[0], layer=l, depth=depth,
                             into=win_out, row_off=p * own.shape[0], name=f"adamw_w_in_{l}_{p}")
            after = win_out[0]

    names = ["norm_g", "conv_a_b", "lru_wr", "lru_br", "lru_wi", "lru_bi", "lru_lambda", "final_g", "meta",
             "conv_a_w", "conv_b_w"]
    order = ["meta", "norm_g", "w_in", "conv_a_w", "conv_a_b", "lru_wr", "lru_br", "lru_wi", "lru_bi", "lru_lambda",
             "conv_b_w", "w_out", "final_g"]

    def family(idx, small):
        table = {nm: small[i] for i, nm in enumerate(names)}
        table["w_in"] = win_out[idx]
        table["w_out"] = wout_out[idx]
        return [table[nm] for nm in order]

    grad_x = dh[n_meta:n_tok][None]
    return (loss, grad_x, *family(0, s_grad), *family(1, s_delta), *family(2, s_m), *family(3, s_v))
```

```python
import functools

import jax
import jax.numpy as jnp
from jax import lax
from jax.experimental import pallas as pl
from jax.experimental.pallas import tpu as pltpu

F32 = jnp.float32
BF16 = jnp.bfloat16
MESH = pl.DeviceIdType.MESH

N_DEV = 8
RMS_EPS = 1e-6
LRU_C = 8.0
ADAM_LR = 0.001
ADAM_B1 = 0.9
ADAM_B2 = 0.999
ADAM_EPS = 1e-08
ADAM_WD = 0.01
ADAM_STEP = 10

V7X_VMEM_LIMIT = 52 * 1024 * 1024
LANES = 128
SUBLANES = 8
TOKEN_TILE = 384
MIX_ROWS = 128
GATE_BLOCK = 256
SCAN_UNROLL = 4


def _params(sem):
    return pltpu.CompilerParams(dimension_semantics=sem, vmem_limit_bytes=V7X_VMEM_LIMIT)


def _tile(n, target, align=LANES):
    best = None
    for t in range(align, min(n, target) + 1, align):
        if n % t == 0:
            best = t
    return n if best is None else best


def _sigmoid(z):
    return 0.5 * jnp.tanh(0.5 * z) + 0.5


def _softplus(z):
    e = jnp.exp(-jnp.abs(z))
    u = 1.0 + e
    l1p = jnp.where(u == 1.0, e, jnp.log(u) * e / jnp.where(u == 1.0, 1.0, u - 1.0))
    return jnp.maximum(z, 0.0) + l1p


def _matmul(a, b, *, ta=False, tb=False, tm, tn, tk, out_dtype=F32, add=None, dep=None, m_part=None, name):
    m, k = (a.shape[1], a.shape[0]) if ta else a.shape
    m_off = 0
    if m_part is not None:
        assert add is None and m % (m_part[1] * tm) == 0
        m //= m_part[1]
        m_off = m_part[0] * (m // tm)
    n, kb = b.shape if tb else b.shape[::-1]
    assert kb == k
    assert m % tm == 0 and n % tn == 0 and k % tk == 0, (m, n, k, tm, tn, tk)
    nk = k // tk
    a_spec = pl.BlockSpec((tk, tm), lambda i, j, q: (q, i + m_off)) if ta \
        else pl.BlockSpec((tm, tk), lambda i, j, q: (i + m_off, q))
    b_spec = pl.BlockSpec((tn, tk), lambda i, j, q: (j, q)) if tb else pl.BlockSpec((tk, tn), lambda i, j, q: (q, j))
    o_spec = pl.BlockSpec((tm, tn), lambda i, j, q: (i, j))
    o_shape = (m, n)
    dims = (((0 if ta else 1,), (1 if tb else 0,)), ((), ()))
    has_add = add is not None
    has_dep = dep is not None

    def body(*refs):
        if has_dep:
            refs = refs[:-3] + refs[-2:]
        if has_add:
            a_ref, b_ref, add_ref, o_ref, acc_ref = refs
        else:
            a_ref, b_ref, o_ref, acc_ref = refs
        q = pl.program_id(2)
        part = lax.dot_general(a_ref[...], b_ref[...], dims, preferred_element_type=F32)

        def finish(acc):
            if has_add:
                acc = acc + add_ref[...]
            o_ref[...] = acc.astype(out_dtype)

        if nk == 1:
            finish(part)
        else:
            @pl.when(q == 0)
            def _():
                acc_ref[...] = part

            @pl.when(jnp.logical_and(q > 0, q < nk - 1))
            def _():
                acc_ref[...] += part

            @pl.when(q == nk - 1)
            def _():
                finish(acc_ref[...] + part)

    in_specs = [a_spec, b_spec] + ([o_spec] if has_add else [])
    args = (a, b) + ((add,) if has_add else ())
    if has_dep:
        in_specs.append(pl.BlockSpec((SUBLANES, LANES), lambda i, j, q: (0, 0)))
        args += (dep,)
    acc_shape = (tm, tn) if nk > 1 else (SUBLANES, LANES)
    return pl.pallas_call(
        body, name=name,
        grid=(m // tm, n // tn, nk),
        in_specs=in_specs, out_specs=o_spec,
        out_shape=jax.ShapeDtypeStruct(o_shape, out_dtype),
        scratch_shapes=[pltpu.VMEM(acc_shape, F32)],
        compiler_params=_params(("parallel", "parallel", "arbitrary")),
    )(*args)


def _rms_fwd(h, g, *, name):
    tp, d = h.shape
    tr = _tile(tp, 512, SUBLANES)

    def body(h_ref, g_ref, o_ref):
        hv = h_ref[...]
        rstd = lax.rsqrt(jnp.mean(hv * hv, axis=-1, keepdims=True) + RMS_EPS)
        o_ref[...] = (hv * rstd * g_ref[...]).astype(BF16)

    return pl.pallas_call(
        body, name=name, grid=(tp // tr,),
        in_specs=[pl.BlockSpec((tr, d), lambda i: (i, 0)), pl.BlockSpec((1, d), lambda i: (0, 0))],
        out_specs=pl.BlockSpec((tr, d), lambda i: (i, 0)),
        out_shape=jax.ShapeDtypeStruct((tp, d), BF16),
        compiler_params=_params(("parallel",)),
    )(h, g.reshape(1, d))


def _rms_bwd(h, dhn, dout, g, *, name):
    tp, d = h.shape
    tr = _tile(tp, 384, SUBLANES)

    def body(h_ref, dhn_ref, dout_ref, g_ref, dh_ref, dhb_ref, dg_ref):
        hv = h_ref[...]
        rstd = lax.rsqrt(jnp.mean(hv * hv, axis=-1, keepdims=True) + RMS_EPS)
        xhat = hv * rstd
        dn = dhn_ref[...]
        dxhat = dn * g_ref[...]
        dh = dout_ref[...] + rstd * (dxhat - xhat * jnp.mean(dxhat * xhat, axis=-1, keepdims=True))
        dh_ref[...] = dh
        dhb_ref[...] = dh.astype(BF16)
        part = jnp.sum(dn * xhat, axis=0, keepdims=True)

        @pl.when(pl.program_id(0) == 0)
        def _():
            dg_ref[...] = part

        @pl.when(pl.program_id(0) > 0)
        def _():
            dg_ref[...] += part

    row = pl.BlockSpec((tr, d), lambda i: (i, 0))
    vec = pl.BlockSpec((1, d), lambda i: (0, 0))
    return pl.pallas_call(
        body, name=name, grid=(tp // tr,),
        in_specs=[row, row, row, vec],
        out_specs=[row, row, vec],
        out_shape=[jax.ShapeDtypeStruct((tp, d), F32), jax.ShapeDtypeStruct((tp, d), BF16),
                   jax.ShapeDtypeStruct((1, d), F32)],
        compiler_params=_params(("arbitrary",)),
    )(h, dhn, dout, g.reshape(1, d))


def _loss_head(h, tgt, g, *, n_meta, n_tok, name):
    tp, d = h.shape
    tr = _tile(tp, 384, SUBLANES)

    def body(h_ref, t_ref, g_ref, dh_ref, dhb_ref, dg_ref, loss_ref):
        i = pl.program_id(0)
        hv = h_ref[...]
        rstd = lax.rsqrt(jnp.mean(hv * hv, axis=-1, keepdims=True) + RMS_EPS)
        xhat = hv * rstd
        gv = g_ref[...]
        rows = i * tr + lax.broadcasted_iota(jnp.int32, (tr, 1), 0)
        valid = jnp.logical_and(rows >= n_meta, rows < n_tok)
        err = jnp.where(valid, xhat * gv - t_ref[...], 0.0)
        dy = err * (1.0 / d)
        dxhat = dy * gv
        dh = rstd * (dxhat - xhat * jnp.mean(dxhat * xhat, axis=-1, keepdims=True))
        dh_ref[...] = dh
        dhb_ref[...] = dh.astype(BF16)
        dg_part = jnp.sum(dy * xhat, axis=0, keepdims=True)
        per_row = jnp.sum(err * err, axis=-1, keepdims=True) * (1.0 / d)
        loss_part = jnp.broadcast_to(0.5 * jnp.sum(per_row, axis=0, keepdims=True), (SUBLANES, LANES))

        @pl.when(i == 0)
        def _():
            dg_ref[...] = dg_part
            loss_ref[...] = loss_part

        @pl.when(i > 0)
        def _():
            dg_ref[...] += dg_part
            loss_ref[...] += loss_part

    row = pl.BlockSpec((tr, d), lambda i: (i, 0))
    vec = pl.BlockSpec((1, d), lambda i: (0, 0))
    return pl.pallas_call(
        body, name=name, grid=(tp // tr,),
        in_specs=[row, row, vec],
        out_specs=[row, row, vec, pl.BlockSpec((SUBLANES, LANES), lambda i: (0, 0))],
        out_shape=[jax.ShapeDtypeStruct((tp, d), F32), jax.ShapeDtypeStruct((tp, d), BF16),
                   jax.ShapeDtypeStruct((1, d), F32), jax.ShapeDtypeStruct((SUBLANES, LANES), F32)],
        compiler_params=_params(("arbitrary",)),
    )(h, tgt, g.reshape(1, d))


def _shift_down(halo, tile, s):
    if s == 0:
        return tile
    ext = jnp.concatenate([halo, tile], axis=0)
    return pltpu.roll(ext, s, 0)[SUBLANES:]


def _shift_up(tile, head, s):
    if s == 0:
        return tile
    ext = jnp.concatenate([tile, head], axis=0)
    n = ext.shape[0]
    return pltpu.roll(ext, n - s, 0)[: tile.shape[0]]


def _scan_rows_fwd(a, b):
    row = lax.broadcasted_iota(jnp.int32, a.shape, 0)
    for s in (1, 2, 4):
        a_sh = pltpu.roll(a, s, 0)
        b_sh = pltpu.roll(b, s, 0)
        m = row >= s
        b = jnp.where(m, a * b_sh + b, b)
        a = jnp.where(m, a * a_sh, a)
    return a, b


def _scan_rows_bwd(c, d):
    row = lax.broadcasted_iota(jnp.int32, c.shape, 0)
    for s in (1, 2, 4):
        c_sh = pltpu.roll(c, SUBLANES - s, 0)
        d_sh = pltpu.roll(d, SUBLANES - s, 0)
        m = row < SUBLANES - s
        d = jnp.where(m, c * d_sh + d, d)
        c = jnp.where(m, c * c_sh, c)
    return c, d


def _gates(ca, wr, wi, br, bi, sp):
    cab = ca.astype(BF16)
    r = _sigmoid(jnp.dot(cab, wr, preferred_element_type=F32) + br)
    ig = _sigmoid(jnp.dot(cab, wi, preferred_element_type=F32) + bi)
    la = -LRU_C * r * sp
    a = jnp.exp(la)
    mult = jnp.sqrt(-jnp.tanh(la) * (a * a + 1.0))
    return r, ig, a, mult


def _mixer_fwd(u, wa, ba, wr_blk, br, wi_blk, bi, lam, wb, *, name):
    tp, din = u.shape
    dl = din // 6
    tt = MIX_ROWS
    cw = GATE_BLOCK
    nch = dl // cw
    assert tp % tt == 0 and dl % cw == 0

    def body(u_ref, wa_ref, ba_ref, wr_ref, br_ref, wi_ref, bi_ref, lam_ref, wb_ref,
             ca_ref, hs_ref, y_ref, xa_tail, v_tail, h_carry, a_s, b_s):
        @pl.when(pl.program_id(0) == 0)
        def _():
            xa_tail[...] = jnp.zeros_like(xa_tail)
            v_tail[...] = jnp.zeros_like(v_tail)
            h_carry[...] = jnp.zeros_like(h_carry)

        for ch in range(nch):
            cs = slice(ch * cw, (ch + 1) * cw)

            def seg(s):
                return slice(s * dl + ch * cw, s * dl + (ch + 1) * cw)

            xa = u_ref[:, seg(0)]
            halo = xa_tail[:, cs]
            ca = ba_ref[:, cs] + wa_ref[3:4, cs] * xa
            for kk in range(3):
                ca = ca + wa_ref[kk:kk + 1, cs] * _shift_down(halo, xa, 3 - kk)
            xa_tail[:, cs] = xa[tt - SUBLANES:]
            ca_ref[:, cs] = ca
            sp = _softplus(-lam_ref[:, cs])
            _, ig, a, mult = _gates(ca, wr_ref[ch], wi_ref[ch], br_ref[:, cs], bi_ref[:, cs], sp)
            a_s[:, cs] = a
            b_s[:, cs] = mult * (ig * ca)

            bv = u_ref[:, seg(2)]
            v = u_ref[:, seg(3)] * u_ref[:, seg(4)]
            gb = u_ref[:, seg(5)]
            vh = v_tail[:, cs]
            cb = wb_ref[2:3, cs] * v
            for kk in range(2):
                cb = cb + wb_ref[kk:kk + 1, cs] * _shift_down(vh, v, 2 - kk)
            v_tail[:, cs] = v[tt - SUBLANES:]
            y_ref[:, dl + ch * cw: dl + (ch + 1) * cw] = (bv * cb * (gb * _sigmoid(gb))).astype(BF16)

        def group(gi, hprev):
            rows = pl.ds(pl.multiple_of(gi * SUBLANES, SUBLANES), SUBLANES)
            a8, b8 = _scan_rows_fwd(a_s[rows, :], b_s[rows, :])
            h8 = b8 + a8 * hprev
            hs_ref[rows, :] = h8
            return jnp.broadcast_to(h8[SUBLANES - 1:SUBLANES, :], h8.shape)

        h_carry[...] = lax.fori_loop(0, tt // SUBLANES, group, h_carry[...], unroll=SCAN_UNROLL)

        for ch in range(nch):
            cs = slice(ch * cw, (ch + 1) * cw)
            ga = u_ref[:, dl + ch * cw: dl + (ch + 1) * cw]
            y_ref[:, cs] = (hs_ref[:, cs] * (ga * _sigmoid(ga))).astype(BF16)

    row = lambda w: pl.BlockSpec((tt, w), lambda i: (i, 0))
    full = lambda shp: pl.BlockSpec(shp, lambda i: tuple(0 for _ in shp))
    return pl.pallas_call(
        body, name=name, grid=(tp // tt,),
        in_specs=[row(din), full((4, dl)), full((1, dl)), full((nch, cw, cw)), full((1, dl)),
                  full((nch, cw, cw)), full((1, dl)), full((1, dl)), full((3, dl))],
        out_specs=[row(dl), row(dl), row(2 * dl)],
        out_shape=[jax.ShapeDtypeStruct((tp, dl), F32), jax.ShapeDtypeStruct((tp, dl), F32),
                   jax.ShapeDtypeStruct((tp, 2 * dl), BF16)],
        scratch_shapes=[pltpu.VMEM((SUBLANES, dl), F32), pltpu.VMEM((SUBLANES, dl), F32),
                        pltpu.VMEM((SUBLANES, dl), F32), pltpu.VMEM((tt, dl), F32), pltpu.VMEM((tt, dl), F32)],
        compiler_params=_params(("arbitrary",)),
    )(u, wa, ba, wr_blk, br, wi_blk, bi, lam, wb)


SG_WA, SG_BA, SG_BR, SG_BI, SG_LAM, SG_WB, SG_ROWS = 0, 4, 5, 6, 7, 8, 16


def _mixer_bwd(u, ca, hs, dy, wa, wr_blk, br, wi_blk, bi, lam, wb, *, name):
    tp, din = u.shape
    dl = din // 6
    tt = MIX_ROWS
    cw = GATE_BLOCK
    nch = dl // cw
    nt = tp // tt
    hb = tt // SUBLANES
    tn_dims = (((0,), (0,)), ((), ()))
    nt_dims = (((1,), (1,)), ((), ()))

    def body(u_ref, uh_ref, ca_ref, hs_ref, hsh_ref, dy_ref, wa_ref, wr_ref, br_ref, wi_ref, bi_ref, lam_ref, wb_ref,
             du_ref, sg_ref, dwr_ref, dwi_ref,
             g_carry, a_head, dca_head, dcb_head, r_s, i_s, a_s, an_s, d_s, g_s):
        i = pl.program_id(0)
        first_tile = i == nt - 1

        @pl.when(i == 0)
        def _():
            for ref in (g_carry, a_head, dca_head, dcb_head, sg_ref, dwr_ref, dwi_ref):
                ref[...] = jnp.zeros_like(ref)

        def halo_of(x):
            return jnp.where(first_tile, 0.0, x)

        for ch in range(nch):
            cs = slice(ch * cw, (ch + 1) * cw)
            cav = ca_ref[:, cs]
            sp = _softplus(-lam_ref[:, cs])
            r, ig, a, _ = _gates(cav, wr_ref[ch], wi_ref[ch], br_ref[:, cs], bi_ref[:, cs], sp)
            r_s[:, cs] = r
            i_s[:, cs] = ig
            a_s[:, cs] = a
            an_s[:, cs] = _shift_up(a, a_head[:, cs], 1)
            a_head[:, cs] = a[:SUBLANES]
            ga = u_ref[:, dl + ch * cw: dl + (ch + 1) * cw]
            d_s[:, cs] = dy_ref[:, cs] * (ga * _sigmoid(ga))

        def group(k, gnext):
            gi = tt // SUBLANES - 1 - k
            rows = pl.ds(pl.multiple_of(gi * SUBLANES, SUBLANES), SUBLANES)
            c8, d8 = _scan_rows_bwd(an_s[rows, :], d_s[rows, :])
            g8 = d8 + c8 * gnext
            g_s[rows, :] = g8
            return jnp.broadcast_to(g8[0:1, :], g8.shape)

        g_carry[...] = lax.fori_loop(0, tt // SUBLANES, group, g_carry[...], unroll=SCAN_UNROLL)

        def acc_row(r0, val):
            sg_ref[r0:r0 + 1, cs_cur[0]] += jnp.sum(val, axis=0, keepdims=True)

        cs_cur = [None]
        for ch in range(nch):
            cs = slice(ch * cw, (ch + 1) * cw)
            cs_cur[0] = cs

            def seg(s):
                return slice(s * dl + ch * cw, s * dl + (ch + 1) * cw)

            cav = ca_ref[:, cs]
            r = r_s[:, cs]
            ig = i_s[:, cs]
            a = a_s[:, cs]
            g = g_s[:, cs]
            hsv = hs_ref[:, cs]
            lamv = lam_ref[:, cs]
            sp = _softplus(-lamv)
            la = -LRU_C * r * sp
            e2 = a * a
            one_m_e2 = -jnp.tanh(la) * (e2 + 1.0)
            mult = jnp.sqrt(one_m_e2)
            hprev = _shift_down(halo_of(hsh_ref[:, cs]), hsv, 1)
            dla = g * hprev * a - g * (ig * cav) * e2 * lax.rsqrt(one_m_e2)
            gm = g * mult
            dzi = gm * cav * ig * (1.0 - ig)
            dca = gm * ig
            dzr = dla * (-LRU_C * sp) * r * (1.0 - r)
            acc_row(SG_LAM, dla * (-LRU_C * r) * (-_sigmoid(-lamv)))
            acc_row(SG_BR, dzr)
            acc_row(SG_BI, dzi)
            dzr_b = dzr.astype(BF16)
            dzi_b = dzi.astype(BF16)
            cab = cav.astype(BF16)
            dca = dca + lax.dot_general(dzr_b, wr_ref[ch], nt_dims, preferred_element_type=F32)
            dca = dca + lax.dot_general(dzi_b, wi_ref[ch], nt_dims, preferred_element_type=F32)
            dwr_ref[ch] += lax.dot_general(cab, dzr_b, tn_dims, preferred_element_type=F32)
            dwi_ref[ch] += lax.dot_general(cab, dzi_b, tn_dims, preferred_element_type=F32)
            acc_row(SG_BA, dca)
            xa = u_ref[:, seg(0)]
            xah = halo_of(uh_ref[:, seg(0)])
            head = dca_head[:, cs]
            dxa = wa_ref[3:4, cs] * dca
            acc_row(SG_WA + 3, dca * xa)
            for kk in range(3):
                acc_row(SG_WA + kk, dca * _shift_down(xah, xa, 3 - kk))
                dxa = dxa + wa_ref[kk:kk + 1, cs] * _shift_up(dca, head, 3 - kk)
            dca_head[:, cs] = dca[:SUBLANES]
            ga = u_ref[:, seg(1)]
            sga = _sigmoid(ga)
            dga = dy_ref[:, cs] * hsv * (sga * (1.0 + ga * (1.0 - sga)))
            du_ref[:, seg(0)] = dxa.astype(BF16)
            du_ref[:, seg(1)] = dga.astype(BF16)

            bv = u_ref[:, seg(2)]
            cv = u_ref[:, seg(3)]
            xb = u_ref[:, seg(4)]
            gb = u_ref[:, seg(5)]
            dyb = dy_ref[:, dl + ch * cw: dl + (ch + 1) * cw]
            v = cv * xb
            vh = halo_of(uh_ref[:, seg(3)] * uh_ref[:, seg(4)])
            v1 = _shift_down(vh, v, 1)
            v2 = _shift_down(vh, v, 2)
            cb = wb_ref[2:3, cs] * v + wb_ref[1:2, cs] * v1 + wb_ref[0:1, cs] * v2
            sgb = _sigmoid(gb)
            sl = gb * sgb
            dcb = dyb * bv * sl
            du_ref[:, seg(2)] = (dyb * cb * sl).astype(BF16)
            du_ref[:, seg(5)] = (dyb * bv * cb * (sgb * (1.0 + gb * (1.0 - sgb)))).astype(BF16)
            acc_row(SG_WB + 2, dcb * v)
            acc_row(SG_WB + 1, dcb * v1)
            acc_row(SG_WB + 0, dcb * v2)
            bhead = dcb_head[:, cs]
            dv = wb_ref[2:3, cs] * dcb + wb_ref[1:2, cs] * _shift_up(dcb, bhead, 1) \
                + wb_ref[0:1, cs] * _shift_up(dcb, bhead, 2)
            dcb_head[:, cs] = dcb[:SUBLANES]
            du_ref[:, seg(3)] = (dv * xb).astype(BF16)
            du_ref[:, seg(4)] = (dv * cv).astype(BF16)

    rev = lambda w: pl.BlockSpec((tt, w), lambda i: (nt - 1 - i, 0))
    halo = lambda w: pl.BlockSpec((SUBLANES, w), lambda i: (jnp.maximum((nt - 1 - i) * hb - 1, 0), 0))
    full = lambda shp: pl.BlockSpec(shp, lambda i: tuple(0 for _ in shp))
    vm = lambda r: pltpu.VMEM((r, dl), F32)
    return pl.pallas_call(
        body, name=name, grid=(nt,),
        in_specs=[rev(din), halo(din), rev(dl), rev(dl), halo(dl), rev(2 * dl), full((4, dl)),
                  full((nch, cw, cw)), full((1, dl)), full((nch, cw, cw)), full((1, dl)), full((1, dl)), full((3, dl))],
        out_specs=[rev(din), full((SG_ROWS, dl)), full((nch, cw, cw)), full((nch, cw, cw))],
        out_shape=[jax.ShapeDtypeStruct((tp, din), BF16), jax.ShapeDtypeStruct((SG_ROWS, dl), F32),
                   jax.ShapeDtypeStruct((nch, cw, cw), F32), jax.ShapeDtypeStruct((nch, cw, cw), F32)],
        scratch_shapes=[vm(SUBLANES), vm(SUBLANES), vm(SUBLANES), vm(SUBLANES),
                        vm(tt), vm(tt), vm(tt), vm(tt), vm(tt), vm(tt)],
        compiler_params=_params(("arbitrary",)),
    )(u, u, ca, hs, hs, dy, wa, wr_blk, br, wi_blk, bi, lam, wb)


def _adamw(w, g, m, v, *, name, landed=None, layer=None, depth=None, into=None, row_off=0):
    r, c = w.shape
    rows = g.shape[0]
    tr = _tile(rows, 256, 2 * SUBLANES)
    assert row_off % tr == 0
    boff = row_off // tr
    bc1 = 1.0 - ADAM_B1 ** ADAM_STEP
    bc2 = 1.0 - ADAM_B2 ** ADAM_STEP
    slots = landed is not None

    def body(*refs):
        if into is not None:
            refs = refs[:-8] + refs[-4:]
        if slots:
            w_ref, g_ref, l_ref, m_ref, v_ref, grad_ref, delta_ref, nm_ref, nv_ref = refs
            gv = g_ref[...].astype(F32)
            for s in range(N_DEV - 1):
                gv = gv + l_ref[s].astype(F32)
        else:
            w_ref, g_ref, m_ref, v_ref, grad_ref, delta_ref, nm_ref, nv_ref = refs
            gv = g_ref[...]
        wv = w_ref[...]
        mn = ADAM_B1 * m_ref[...] + (1.0 - ADAM_B1) * gv
        vn = ADAM_B2 * v_ref[...] + (1.0 - ADAM_B2) * (gv * gv)
        m_hat = mn / bc1
        v_hat = vn / bc2
        grad_ref[...] = gv
        delta_ref[...] = -ADAM_LR * (m_hat / (jnp.sqrt(v_hat) + ADAM_EPS) + ADAM_WD * wv)
        nm_ref[...] = mn
        nv_ref[...] = vn

    blk = pl.BlockSpec((tr, c), lambda i: (i + boff, 0))
    g_blk = pl.BlockSpec((tr, c), lambda i: (i, 0))
    l_spec = [pl.BlockSpec((N_DEV - 1, tr, c), lambda i: (0, i, 0))] if slots else []
    args = (w, g, landed, m, v) if slots else (w, g, m, v)
    in_specs = [blk, g_blk] + l_spec + [blk, blk]
    if depth is None:
        shp = jax.ShapeDtypeStruct((r, c), F32)
        out_blk = blk
    else:
        shp = jax.ShapeDtypeStruct((depth, r, c), F32)
        out_blk = pl.BlockSpec((None, tr, c), lambda i: (layer, i + boff, 0))
    aliases = {}
    if into is not None:
        aliases = {len(args) + j: j for j in range(4)}
        in_specs = in_specs + [ANY] * 4
        args = args + tuple(into)
    return pl.pallas_call(
        body, name=name, grid=(rows // tr,),
        in_specs=in_specs, out_specs=[out_blk] * 4,
        out_shape=[shp] * 4, input_output_aliases=aliases,
        compiler_params=_params(("parallel",)),
    )(*args)


def _slot_sum(g, *, name):
    _, r, c = g.shape
    tr = _tile(r, 512, SUBLANES)

    def body(g_ref, o_ref):
        gv = g_ref[0]
        for s in range(1, N_DEV):
            gv = gv + g_ref[s]
        o_ref[...] = gv

    return pl.pallas_call(
        body, name=name, grid=(r // tr,),
        in_specs=[pl.BlockSpec((N_DEV, tr, c), lambda i: (0, i, 0))],
        out_specs=pl.BlockSpec((tr, c), lambda i: (i, 0)),
        out_shape=jax.ShapeDtypeStruct((r, c), F32),
        compiler_params=_params(("parallel",)),
    )(g)


def _mesh_pos():
    x, y, c = lax.axis_index("x"), lax.axis_index("y"), lax.axis_index("c")
    return x, y, c, 4 * x + 2 * y + c


ANY = pl.BlockSpec(memory_space=pl.ANY)


def _all_gather(srcs, out_shapes, views, *, name, place=()):
    n = len(srcs)
    npl = len(place)

    def body(*refs):
        src = refs[:n]
        psrc = refs[n:n + npl]
        dst = refs[n + npl:2 * n + npl]
        pdst = refs[2 * n + npl:2 * (n + npl)]
        send_sems, recv_sems, local_sems = refs[2 * (n + npl):]
        x, y, c, me = _mesh_pos()
        sibling = (x, y, 1 - c)
        chips = [(1 - x, y), (x, 1 - y), (1 - x, 1 - y)]

        def dev(px, py, pc):
            return 4 * px + 2 * py + pc

        def copy(a, k, block, to, from_src=False):
            win = views[a](dst[a], dev(*block))
            return pltpu.make_async_remote_copy(
                src_ref=src[a] if from_src else win, dst_ref=win,
                send_sem=send_sems.at[a * 7 + k], recv_sem=recv_sems.at[a * 7 + k],
                device_id=to, device_id_type=MESH)

        mine = [pltpu.make_async_copy(src[a], views[a](dst[a], me), local_sems.at[a]) for a in range(n)]
        mine += [pltpu.make_async_copy(psrc[j], place[j][2](pdst[j], me), local_sems.at[n + j]) for j in range(npl)]
        started = []
        for a in range(n):
            mine[a].start()
            first = [copy(a, 0, (x, y, c), sibling, True)]
            first += [copy(a, 1 + j, (x, y, c), (*chip, c), True) for j, chip in enumerate(chips)]
            for cp in first:
                cp.start()
            started += first
        for cp in mine[n:]:
            cp.start()
        for a in range(n):
            for j, chip in enumerate(chips):
                copy(a, 1 + j, (*chip, c), (x, y, c)).wait_recv()
                fwd = copy(a, 4 + j, (*chip, c), sibling)
                fwd.start()
                started.append(fwd)
        for a in range(n):
            copy(a, 0, (x, y, 1 - c), (x, y, c)).wait_recv()
            for j, chip in enumerate(chips):
                copy(a, 4 + j, (*chip, 1 - c), (x, y, c)).wait_recv()
        for cp in started:
            cp.wait_send()
        for cp in mine:
            cp.wait()

    return pl.pallas_call(
        body, name=name,
        in_specs=[ANY] * (n + npl), out_specs=[ANY] * (n + npl),
        out_shape=[jax.ShapeDtypeStruct(s, x.dtype) for s, x in zip(out_shapes, srcs)]
        + [jax.ShapeDtypeStruct(shape, arr.dtype) for arr, shape, _ in place],
        scratch_shapes=[pltpu.SemaphoreType.DMA((7 * n,)), pltpu.SemaphoreType.DMA((7 * n,)),
                        pltpu.SemaphoreType.DMA((n + npl,))],
    )(*srcs, *[arr for arr, _, _ in place])


HBM = pl.BlockSpec(memory_space=pltpu.HBM)
SEM = pl.BlockSpec(memory_space=pltpu.SEMAPHORE)
EFFECT = pltpu.SideEffectType.DATAFLOW_SIDE_EFFECTING


def _peer_of(x, y, c, k):
    return (1 - x if k & 4 else x, 1 - y if k & 2 else y, 1 - c if k & 1 else c)


def _peer_copies(n, wins, src, land, send_sems, recv_sems):
    x, y, c, me = _mesh_pos()
    out = []
    for a in range(n):
        for k in range(1, N_DEV):
            px, py, pc = _peer_of(x, y, c, k)
            plan = wins[a](src[a], land[a], me, 4 * px + 2 * py + pc, k)
            if plan is None:
                continue
            target = _peer_of(x, y, c, plan[2]) if len(plan) == 3 else (px, py, pc)
            out.append(pltpu.make_async_remote_copy(
                src_ref=plan[0], dst_ref=plan[1],
                send_sem=send_sems.at[a * 7 + k - 1], recv_sem=recv_sems.at[a * 7 + k - 1],
                device_id=target, device_id_type=MESH))
    return out


def _push_start(srcs, lands, wins, *, name):
    n = len(srcs)

    def body(*refs):
        src = refs[:n]
        land = refs[n:2 * n]
        send_sems, recv_sems = refs[2 * n], refs[2 * n + 1]
        token = refs[-1]
        for cp in _peer_copies(n, wins, src, land, send_sems, recv_sems):
            cp.start()
        token[...] = jnp.zeros_like(token)

    bufs = (*srcs, *lands)
    return pl.pallas_call(
        body, name=name,
        out_shape=(pltpu.SemaphoreType.DMA((7 * n,)), pltpu.SemaphoreType.DMA((7 * n,)),
                   *[pltpu.HBM(v.shape, v.dtype) for v in bufs], jax.ShapeDtypeStruct((SUBLANES, LANES), F32)),
        in_specs=[HBM] * (2 * n),
        out_specs=(SEM, SEM, *[HBM] * (2 * n), pl.BlockSpec(memory_space=pltpu.VMEM)),
        input_output_aliases={i: 2 + i for i in range(2 * n)},
        compiler_params=pltpu.CompilerParams(has_side_effects=EFFECT),
    )(*[pltpu.with_memory_space_constraint(v, pltpu.HBM) for v in bufs])


def _push_wait(handle, wins, after, *, name):
    send_sems, recv_sems, *bufs, _ = handle
    n = len(bufs) // 2

    def body(*refs):
        src = refs[:n]
        land = refs[n:2 * n]
        for cp in _peer_copies(n, wins, src, land, refs[2 * n], refs[2 * n + 1]):
            cp.wait_send()
            cp.wait_recv()

    outs = pl.pallas_call(
        body, name=name,
        out_shape=tuple(pltpu.HBM(v.shape, v.dtype) for v in bufs),
        in_specs=[HBM] * (2 * n) + [SEM, SEM, ANY],
        out_specs=tuple([HBM] * (2 * n)),
        input_output_aliases={i: i for i in range(2 * n)},
        compiler_params=pltpu.CompilerParams(has_side_effects=EFFECT),
    )(*bufs, send_sems, recv_sems, after)
    return outs[:n], outs[n:]


def _gather_lead(src, land, me, peer, k):
    return src, land.at[me]


SAME_CORE_PEERS = (2, 4, 6)
SIBLING = 1


def _gather_cols(width, ks=range(1, N_DEV)):
    def win(src, land, me, peer, k):
        return (src, land.at[:, pl.ds(me * width, width)]) if k in ks else None
    return win


def _forward_cols(width):
    def win(src, land, me, peer, k):
        block = land.at[:, pl.ds(peer * width, width)]
        return (block, block, SIBLING) if k in SAME_CORE_PEERS else None
    return win


def _scatter_lead(src, land, me, peer, k):
    return src.at[peer], land.at[k - 1]


def _scatter_cols(width):
    def win(src, land, me, peer, k):
        return src.at[:, pl.ds(peer * width, width)], land.at[k - 1]
    return win


def _place_block(own, *, cols, name):
    rows, width = own.shape
    tr = _tile(rows, 512, 2 * SUBLANES)
    _, _, _, me = _mesh_pos()

    def body(me_ref, x_ref, o_ref):
        o_ref[...] = x_ref[...]

    if cols:
        out_spec = pl.BlockSpec((tr, width), lambda i, me_ref: (i, me_ref[0]))
        shape = (rows, N_DEV * width)
    else:
        out_spec = pl.BlockSpec((None, tr, width), lambda i, me_ref: (me_ref[0], i, 0))
        shape = (N_DEV, rows, width)
    return pl.pallas_call(
        body, name=name,
        grid_spec=pltpu.PrefetchScalarGridSpec(
            num_scalar_prefetch=1, grid=(rows // tr,),
            in_specs=[pl.BlockSpec((tr, width), lambda i, me_ref: (i, 0))], out_specs=out_spec),
        out_shape=jax.ShapeDtypeStruct(shape, own.dtype),
        compiler_params=_params(("arbitrary",)),
    )(me.astype(jnp.int32).reshape(1), own)


def _dep(x, token):
    return x + token[0, 0].astype(x.dtype)


def _lead(ref, d):
    return ref.at[d]


def _col_window(width):
    def view(ref, d):
        return ref.at[:, pl.ds(d * width, width)]
    return view


def _pack(arrs):
    flat = jnp.concatenate([a.reshape(-1).astype(F32) for a in arrs])
    n = flat.shape[0]
    rows = -(-n // (SUBLANES * LANES)) * SUBLANES
    return jnp.pad(flat, (0, rows * LANES - n)).reshape(rows, LANES)


def _unpack(buf, shapes):
    flat = buf.reshape(-1)
    out, off = [], 0
    for s in shapes:
        n = 1
        for q in s:
            n *= q
        out.append(flat[off:off + n].reshape(s))
        off += n
    return out


def _blockdiag(w, cw):
    h, hd, _ = w.shape
    per = cw // hd
    wg = w.reshape(h // per, per, hd, hd)
    eye = jnp.eye(per, dtype=w.dtype)
    blk = jnp.einsum("gpij,pq->gpiqj", wg, eye)
    return blk.reshape(h // per, cw, cw).astype(BF16)


def _blockdiag_extract(g, hd):
    n, cw, _ = g.shape
    per = cw // hd
    g5 = g.reshape(n, per, hd, per, hd)
    idx = jnp.arange(per)
    return g5[:, idx, :, idx, :].transpose(1, 0, 2, 3).reshape(n * per, hd, hd)


def kernel(x, meta, norm_g, w_in, conv_a_w, conv_a_b, lru_wr, lru_br, lru_wi, lru_bi, lru_lambda, conv_b_w, w_out, final_g, loss_target, m_meta, m_norm_g, m_w_in, m_conv_a_w, m_conv_a_b, m_lru_wr, m_lru_br, m_lru_wi, m_lru_bi, m_lru_lambda, m_conv_b_w, m_w_out, m_final_g, v_meta, v_norm_g, v_w_in, v_conv_a_w, v_conv_a_b, v_lru_wr, v_lru_br, v_lru_wi, v_lru_bi, v_lru_lambda, v_conv_b_w, v_w_out, v_final_g):
    _, seq, d = x.shape
    n_meta = meta.shape[0]
    depth = w_in.shape[0]
    din = w_in.shape[2] * N_DEV
    dl = din // 6
    dmix = 2 * dl
    wcol = w_in.shape[2]
    wrow = w_out.shape[1]
    mcol = meta.shape[1]
    ccol = conv_a_w.shape[2]
    heads, hd = lru_wr.shape[1], lru_wr.shape[2]
    n_tok = n_meta + seq
    tp = -(-n_tok // TOKEN_TILE) * TOKEN_TILE
    me = 4 * lax.axis_index("x") + 2 * lax.axis_index("y") + lax.axis_index("c")

    bf = lambda a: a.astype(BF16)
    small_mine = _pack([meta, conv_a_w, conv_b_w])
    w_out_b = [bf(w_out[l]) for l in range(depth)]
    w_in_b = [bf(w_in[l]) for l in range(depth)]
    small_all = _all_gather([small_mine], [(N_DEV,) + small_mine.shape], [_lead], name="gather_small")[0]
    parts = [_unpack(small_all[s], [meta.shape, conv_a_w.shape, conv_b_w.shape]) for s in range(N_DEV)]
    meta_full = jnp.concatenate([p[0] for p in parts], axis=1)
    wa_full = jnp.concatenate([p[1] for p in parts], axis=2)
    wb_full = jnp.concatenate([p[2] for p in parts], axis=2)
    w_in_full = [None] * depth
    w_out_full = [None] * depth
    land_out = [_place_block(w_out_b[l], cols=False, name=f"place_wout_{l}") for l in range(depth)]
    land_in = [_place_block(w_in_b[l], cols=True, name=f"place_win_{l}") for l in range(depth)]

    first_wins = [_gather_cols(wcol, ks=(SIBLING,) + SAME_CORE_PEERS)]
    first_push = _push_start([w_in_b[0]], [land_in[0]], first_wins, name="gather_win_0_start")
    h = jnp.concatenate([meta_full, x[0], jnp.zeros((tp - n_tok, d), F32)], axis=0)
    hn_first = _rms_fwd(h, _dep(norm_g[0], first_push[-1]), name="rms_fwd_0")
    first_src, first_land = _push_wait(first_push, first_wins, hn_first, name="gather_win_0_wait")
    forward = _push_start(first_src, first_land, [_forward_cols(wcol)], name="forward_win_0_start")
    tgt = _dep(jnp.pad(loss_target[0], ((n_meta, tp - n_tok), (0, 0))), forward[-1])
    _, first_land = _push_wait(forward, [_forward_cols(wcol)], tgt, name="forward_win_0_wait")

    push_out = [None] * depth
    push_in = [None] * depth
    w_in_full[0], later_src = lax.optimization_barrier((first_land[0], w_out_b[0]))
    push_out[0] = _push_start([later_src], [land_out[0]], [_gather_lead], name="gather_wout_0_start")
    token = push_out[0][-1]
    for l in range(1, depth):
        push_in[l] = _push_start([_dep(w_in_b[l], token)], [land_in[l]], [_gather_cols(wcol)],
                                 name=f"gather_win_{l}_start")
        push_out[l] = _push_start([_dep(w_out_b[l], push_in[l][-1])], [land_out[l]], [_gather_lead],
                                  name=f"gather_wout_{l}_start")
        token = push_out[l][-1]

    wr_blk = [_blockdiag(lru_wr[l], GATE_BLOCK) for l in range(depth)]
    wi_blk = [_blockdiag(lru_wi[l], GATE_BLOCK) for l in range(depth)]
    vec = lambda a: a.reshape(1, dl)

    tm = _tile(tp, 1408)
    saved = []
    for l in range(depth):
        hn = hn_first if l == 0 else _rms_fwd(h, norm_g[l], name=f"rms_fwd_{l}")
        if l > 0:
            _, landed = _push_wait(push_in[l], [_gather_cols(wcol)], hn, name=f"gather_win_{l}_wait")
            w_in_full[l] = landed[0]
        u = _matmul(hn, w_in_full[l], tm=tm, tn=_tile(din, 768), tk=d, dep=token if l == 0 else None,
                    name=f"mm_u_{l}")
        ca, hs, y = _mixer_fwd(u, wa_full[l], vec(conv_a_b[l]), wr_blk[l], vec(lru_br[l]), wi_blk[l], vec(lru_bi[l]),
                               vec(lru_lambda[l]), wb_full[l], name=f"mixer_fwd_{l}")
        _, landed = _push_wait(push_out[l], [_gather_lead], y, name=f"gather_wout_{l}_wait")
        w_out_full[l] = landed[0].reshape(dmix, d)
        h_next = _matmul(y, w_out_full[l], tm=tm, tn=_tile(d, 512), tk=dmix, add=h, name=f"mm_out_{l}")
        saved.append((h, hn, u, ca, hs, y))
        h = h_next

    dh, dhb, dg_final, loss_part = _loss_head(h, tgt, final_g, n_meta=n_meta, n_tok=n_tok, name="loss_head")
    loss = lax.psum(loss_part[0, 0], ("x", "y", "c"))

    small_grads = [None] * depth
    sent_out = [None] * depth
    sent_in = [None] * depth
    scatter_in = [_scatter_cols(wcol)]
    token = None
    dg_norms = []
    for l in reversed(range(depth)):
        h_in, hn, u, ca, hs, y = saved[l]
        dy = _matmul(dhb, w_out_full[l], tb=True, tm=tm, tn=_tile(dmix, 512), tk=d, dep=token, name=f"mm_dy_{l}")
        dw_out = _matmul(y, dhb, ta=True, tm=_tile(dmix, 1024), tn=_tile(d, 1024), tk=tm, out_dtype=BF16,
                         name=f"mm_dwout_{l}")
        sent_out[l] = _push_start([dw_out.reshape(N_DEV, wrow, d)], [lax.empty((N_DEV - 1, wrow, d), BF16)],
                                  [_scatter_lead], name=f"scatter_wout_{l}_start")
        du, sg, dwr, dwi = _mixer_bwd(u, ca, hs, dy, wa_full[l], wr_blk[l], vec(lru_br[l]), wi_blk[l], vec(lru_bi[l]),
                                      vec(lru_lambda[l]), _dep(wb_full[l], sent_out[l][-1]), name=f"mixer_bwd_{l}")
        small_grads[l] = (sg, dwr, dwi)
        if l == 0:
            early = _pack([
                jnp.stack([small_grads[j][0][SG_BA] for j in range(depth)]),
                jnp.stack([_blockdiag_extract(small_grads[j][1], hd) for j in range(depth)]),
                jnp.stack([small_grads[j][0][SG_BR] for j in range(depth)]),
                jnp.stack([_blockdiag_extract(small_grads[j][2], hd) for j in range(depth)]),
                jnp.stack([small_grads[j][0][SG_BI] for j in range(depth)]),
                jnp.stack([small_grads[j][0][SG_LAM] for j in range(depth)]),
                jnp.stack([small_grads[j][0][SG_WA:SG_WA + 4] for j in range(depth)]),
                jnp.stack([small_grads[j][0][SG_WB:SG_WB + 3] for j in range(depth)]),
                dg_final[0], *dg_norms])
            early_land = lax.dynamic_update_slice(lax.empty((N_DEV,) + early.shape, F32), early[None], (me, 0, 0))
            sent_early = _push_start([early], [early_land], [_gather_lead], name="gather_early_grads_start")
        parts = 2 if l == 0 else 1
        token = sent_early[-1] if l == 0 else None
        sent_in[l] = []
        for p in range(parts):
            dw_in = _matmul(hn, du, ta=True, tm=_tile(d // parts, 1024), tn=_tile(din, 1536), tk=tm, out_dtype=BF16,
                            dep=token, m_part=(p, parts), name=f"mm_dwin_{l}_{p}")
            sent_in[l].append(_push_start([dw_in], [lax.empty((N_DEV - 1, d // parts, wcol), BF16)], scatter_in,
                                          name=f"scatter_win_{l}_{p}_start"))
            token = sent_in[l][-1][-1]
        dhn = _matmul(du, w_in_full[l], tb=True, tm=tm, tn=_tile(d, 1024), tk=_tile(din, 1536), dep=token,
                      name=f"mm_dhn_{l}")
        dh, dhb, dg_norm = _rms_bwd(h_in, dhn, dh, norm_g[l], name=f"rms_bwd_{l}")
        if l > 0:
            dg_norms.append(dg_norm[0])

    late = _pack([dg_norm[0], dh[:n_meta]])
    late_all = _all_gather([late], [(N_DEV,) + late.shape], [_lead], name="gather_late_grads")[0]
    late_sum = _unpack(_slot_sum(late_all, name="sum_late_grads"), [(d,), (n_meta, d)])
    _, early_all = _push_wait(sent_early, [_gather_lead], late_sum[0], name="gather_early_grads_wait")
    early_shapes = [conv_a_b.shape, lru_wr.shape, lru_br.shape, lru_wi.shape, lru_bi.shape, lru_lambda.shape,
                    (depth, 4, dl), (depth, 3, dl), final_g.shape] + [(d,)] * (depth - 1)
    e = _unpack(_slot_sum(early_all[0], name="sum_early_grads"), early_shapes)
    g_norm = jnp.stack([late_sum[0]] + e[9:][::-1])
    g_meta = lax.dynamic_slice_in_dim(late_sum[1], me * mcol, mcol, axis=1)
    g_wa = lax.dynamic_slice_in_dim(e[6], me * ccol, ccol, axis=2)
    g_wb = lax.dynamic_slice_in_dim(e[7], me * ccol, ccol, axis=2)

    small_w = [norm_g, conv_a_b, lru_wr, lru_br, lru_wi, lru_bi, lru_lambda, final_g, meta, conv_a_w, conv_b_w]
    small_m = [m_norm_g, m_conv_a_b, m_lru_wr, m_lru_br, m_lru_wi, m_lru_bi, m_lru_lambda, m_final_g, m_meta,
               m_conv_a_w, m_conv_b_w]
    small_v = [v_norm_g, v_conv_a_b, v_lru_wr, v_lru_br, v_lru_wi, v_lru_bi, v_lru_lambda, v_final_g, v_meta,
               v_conv_a_w, v_conv_b_w]
    small_g = [g_norm, e[0], e[1], e[2], e[3], e[4], e[5], e[8], g_meta, g_wa, g_wb]
    small_out = _adamw(_pack(small_w), _pack(small_g), _pack(small_m), _pack(small_v), name="adamw_small")
    small_shapes = [a.shape for a in small_w]
    s_grad, s_delta, s_m, s_v = [_unpack(o, small_shapes) for o in small_out]

    win_out = None
    wout_out = None
    after = small_out[0]
    for l in reversed(range(depth)):
        src, landed = _push_wait(sent_out[l], [_scatter_lead], after, name=f"scatter_wout_{l}_wait")
        own = lax.dynamic_index_in_dim(src[0], me, 0, keepdims=False)
        wout_out = _adamw(w_out[l], own, m_w_out[l], v_w_out[l], landed=landed[0], layer=l, depth=depth,
                          into=wout_out, name=f"adamw_w_out_{l}")
        after = wout_out[0]
        for p, sent in enumerate(sent_in[l]):
            src, landed = _push_wait(sent, scatter_in, after, name=f"scatter_win_{l}_{p}_wait")
            own = lax.dynamic_slice_in_dim(src[0], me * wcol, wcol, axis=1)
            win_out = _adamw(w_in[l], own, m_w_in[l], v_w_in[l], landed=landed[0], layer=l, depth=depth,
                             into=win_out, row_off=p * own.shape[0], name=f"adamw_w_in_{l}_{p}")
            after = win_out[0]

    names = ["norm_g", "conv_a_b", "lru_wr", "lru_br", "lru_wi", "lru_bi", "lru_lambda", "final_g", "meta",
             "conv_a_w", "conv_b_w"]
    order = ["meta", "norm_g", "w_in", "conv_a_w", "conv_a_b", "lru_wr", "lru_br", "lru_wi", "lru_bi", "lru_lambda",
             "conv_b_w", "w_out", "final_g"]

    def family(idx, small):
        table = {nm: small[i] for i, nm in enumerate(names)}
        table["w_in"] = win_out[idx]
        table["w_out"] = wout_out[idx]
        return [table[nm] for nm in order]

    grad_x = dh[n_meta:n_tok][None]
    return (loss, grad_x, *family(0, s_grad), *family(1, s_delta), *family(2, s_m), *family(3, s_v))
```

```python
import functools

import jax
import jax.numpy as jnp
from jax import lax
from jax.experimental import pallas as pl
from jax.experimental.pallas import tpu as pltpu

F32 = jnp.float32
BF16 = jnp.bfloat16
MESH = pl.DeviceIdType.MESH

N_DEV = 8
RMS_EPS = 1e-6
LRU_C = 8.0
ADAM_LR = 0.001
ADAM_B1 = 0.9
ADAM_B2 = 0.999
ADAM_EPS = 1e-08
ADAM_WD = 0.01
ADAM_STEP = 10

V7X_VMEM_LIMIT = 52 * 1024 * 1024
LANES = 128
SUBLANES = 8
TOKEN_TILE = 384
MIX_ROWS = 128
GATE_BLOCK = 256
SCAN_UNROLL = 4


def _params(sem):
    return pltpu.CompilerParams(dimension_semantics=sem, vmem_limit_bytes=V7X_VMEM_LIMIT)


def _tile(n, target, align=LANES):
    best = None
    for t in range(align, min(n, target) + 1, align):
        if n % t == 0:
            best = t
    return n if best is None else best


def _sigmoid(z):
    return 0.5 * jnp.tanh(0.5 * z) + 0.5


def _softplus(z):
    e = jnp.exp(-jnp.abs(z))
    u = 1.0 + e
    l1p = jnp.where(u == 1.0, e, jnp.log(u) * e / jnp.where(u == 1.0, 1.0, u - 1.0))
    return jnp.maximum(z, 0.0) + l1p


def _matmul(a, b, *, ta=False, tb=False, tm, tn, tk, out_dtype=F32, add=None, dep=None, m_part=None, name):
    m, k = (a.shape[1], a.shape[0]) if ta else a.shape
    m_off = 0
    if m_part is not None:
        assert add is None and m % (m_part[1] * tm) == 0
        m //= m_part[1]
        m_off = m_part[0] * (m // tm)
    n, kb = b.shape if tb else b.shape[::-1]
    assert kb == k
    assert m % tm == 0 and n % tn == 0 and k % tk == 0, (m, n, k, tm, tn, tk)
    nk = k // tk
    a_spec = pl.BlockSpec((tk, tm), lambda i, j, q: (q, i + m_off)) if ta \
        else pl.BlockSpec((tm, tk), lambda i, j, q: (i + m_off, q))
    b_spec = pl.BlockSpec((tn, tk), lambda i, j, q: (j, q)) if tb else pl.BlockSpec((tk, tn), lambda i, j, q: (q, j))
    o_spec = pl.BlockSpec((tm, tn), lambda i, j, q: (i, j))
    o_shape = (m, n)
    dims = (((0 if ta else 1,), (1 if tb else 0,)), ((), ()))
    has_add = add is not None
    has_dep = dep is not None

    def body(*refs):
        if has_dep:
            refs = refs[:-3] + refs[-2:]
        if has_add:
            a_ref, b_ref, add_ref, o_ref, acc_ref = refs
        else:
            a_ref, b_ref, o_ref, acc_ref = refs
        q = pl.program_id(2)
        part = lax.dot_general(a_ref[...], b_ref[...], dims, preferred_element_type=F32)

        def finish(acc):
            if has_add:
                acc = acc + add_ref[...]
            o_ref[...] = acc.astype(out_dtype)

        if nk == 1:
            finish(part)
        else:
            @pl.when(q == 0)
            def _():
                acc_ref[...] = part

            @pl.when(jnp.logical_and(q > 0, q < nk - 1))
            def _():
                acc_ref[...] += part

            @pl.when(q == nk - 1)
            def _():
                finish(acc_ref[...] + part)

    in_specs = [a_spec, b_spec] + ([o_spec] if has_add else [])
    args = (a, b) + ((add,) if has_add else ())
    if has_dep:
        in_specs.append(pl.BlockSpec((SUBLANES, LANES), lambda i, j, q: (0, 0)))
        args += (dep,)
    acc_shape = (tm, tn) if nk > 1 else (SUBLANES, LANES)
    return pl.pallas_call(
        body, name=name,
        grid=(m // tm, n // tn, nk),
        in_specs=in_specs, out_specs=o_spec,
        out_shape=jax.ShapeDtypeStruct(o_shape, out_dtype),
        scratch_shapes=[pltpu.VMEM(acc_shape, F32)],
        compiler_params=_params(("parallel", "parallel", "arbitrary")),
    )(*args)


def _rms_fwd(h, g, *, name):
    tp, d = h.shape
    tr = _tile(tp, 512, SUBLANES)

    def body(h_ref, g_ref, o_ref):
        hv = h_ref[...]
        rstd = lax.rsqrt(jnp.mean(hv * hv, axis=-1, keepdims=True) + RMS_EPS)
        o_ref[...] = (hv * rstd * g_ref[...]).astype(BF16)

    return pl.pallas_call(
        body, name=name, grid=(tp // tr,),
        in_specs=[pl.BlockSpec((tr, d), lambda i: (i, 0)), pl.BlockSpec((1, d), lambda i: (0, 0))],
        out_specs=pl.BlockSpec((tr, d), lambda i: (i, 0)),
        out_shape=jax.ShapeDtypeStruct((tp, d), BF16),
        compiler_params=_params(("parallel",)),
    )(h, g.reshape(1, d))


def _rms_bwd(h, dhn, dout, g, *, name):
    tp, d = h.shape
    tr = _tile(tp, 384, SUBLANES)

    def body(h_ref, dhn_ref, dout_ref, g_ref, dh_ref, dhb_ref, dg_ref):
        hv = h_ref[...]
        rstd = lax.rsqrt(jnp.mean(hv * hv, axis=-1, keepdims=True) + RMS_EPS)
        xhat = hv * rstd
        dn = dhn_ref[...]
        dxhat = dn * g_ref[...]
        dh = dout_ref[...] + rstd * (dxhat - xhat * jnp.mean(dxhat * xhat, axis=-1, keepdims=True))
        dh_ref[...] = dh
        dhb_ref[...] = dh.astype(BF16)
        part = jnp.sum(dn * xhat, axis=0, keepdims=True)

        @pl.when(pl.program_id(0) == 0)
        def _():
            dg_ref[...] = part

        @pl.when(pl.program_id(0) > 0)
        def _():
            dg_ref[...] += part

    row = pl.BlockSpec((tr, d), lambda i: (i, 0))
    vec = pl.BlockSpec((1, d), lambda i: (0, 0))
    return pl.pallas_call(
        body, name=name, grid=(tp // tr,),
        in_specs=[row, row, row, vec],
        out_specs=[row, row, vec],
        out_shape=[jax.ShapeDtypeStruct((tp, d), F32), jax.ShapeDtypeStruct((tp, d), BF16),
                   jax.ShapeDtypeStruct((1, d), F32)],
        compiler_params=_params(("arbitrary",)),
    )(h, dhn, dout, g.reshape(1, d))


def _loss_head(h, tgt, g, *, n_meta, n_tok, name):
    tp, d = h.shape
    tr = _tile(tp, 384, SUBLANES)

    def body(h_ref, t_ref, g_ref, dh_ref, dhb_ref, dg_ref, loss_ref):
        i = pl.program_id(0)
        hv = h_ref[...]
        rstd = lax.rsqrt(jnp.mean(hv * hv, axis=-1, keepdims=True) + RMS_EPS)
        xhat = hv * rstd
        gv = g_ref[...]
        rows = i * tr + lax.broadcasted_iota(jnp.int32, (tr, 1), 0)
        valid = jnp.logical_and(rows >= n_meta, rows < n_tok)
        err = jnp.where(valid, xhat * gv - t_ref[...], 0.0)
        dy = err * (1.0 / d)
        dxhat = dy * gv
        dh = rstd * (dxhat - xhat * jnp.mean(dxhat * xhat, axis=-1, keepdims=True))
        dh_ref[...] = dh
        dhb_ref[...] = dh.astype(BF16)
        dg_part = jnp.sum(dy * xhat, axis=0, keepdims=True)
        per_row = jnp.sum(err * err, axis=-1, keepdims=True) * (1.0 / d)
        loss_part = jnp.broadcast_to(0.5 * jnp.sum(per_row, axis=0, keepdims=True), (SUBLANES, LANES))

        @pl.when(i == 0)
        def _():
            dg_ref[...] = dg_part
            loss_ref[...] = loss_part

        @pl.when(i > 0)
        def _():
            dg_ref[...] += dg_part
            loss_ref[...] += loss_part

    row = pl.BlockSpec((tr, d), lambda i: (i, 0))
    vec = pl.BlockSpec((1, d), lambda i: (0, 0))
    return pl.pallas_call(
        body, name=name, grid=(tp // tr,),
        in_specs=[row, row, vec],
        out_specs=[row, row, vec, pl.BlockSpec((SUBLANES, LANES), lambda i: (0, 0))],
        out_shape=[jax.ShapeDtypeStruct((tp, d), F32), jax.ShapeDtypeStruct((tp, d), BF16),
                   jax.ShapeDtypeStruct((1, d), F32), jax.ShapeDtypeStruct((SUBLANES, LANES), F32)],
        compiler_params=_params(("arbitrary",)),
    )(h, tgt, g.reshape(1, d))


def _shift_down(halo, tile, s):
    if s == 0:
        return tile
    ext = jnp.concatenate([halo, tile], axis=0)
    return pltpu.roll(ext, s, 0)[SUBLANES:]


def _shift_up(tile, head, s):
    if s == 0:
        return tile
    ext = jnp.concatenate([tile, head], axis=0)
    n = ext.shape[0]
    return pltpu.roll(ext, n - s, 0)[: tile.shape[0]]


def _scan_rows_fwd(a, b):
    row = lax.broadcasted_iota(jnp.int32, a.shape, 0)
    for s in (1, 2, 4):
        a_sh = pltpu.roll(a, s, 0)
        b_sh = pltpu.roll(b, s, 0)
        m = row >= s
        b = jnp.where(m, a * b_sh + b, b)
        a = jnp.where(m, a * a_sh, a)
    return a, b


def _scan_rows_bwd(c, d):
    row = lax.broadcasted_iota(jnp.int32, c.shape, 0)
    for s in (1, 2, 4):
        c_sh = pltpu.roll(c, SUBLANES - s, 0)
        d_sh = pltpu.roll(d, SUBLANES - s, 0)
        m = row < SUBLANES - s
        d = jnp.where(m, c * d_sh + d, d)
        c = jnp.where(m, c * c_sh, c)
    return c, d


def _gates(ca, wr, wi, br, bi, sp):
    cab = ca.astype(BF16)
    r = _sigmoid(jnp.dot(cab, wr, preferred_element_type=F32) + br)
    ig = _sigmoid(jnp.dot(cab, wi, preferred_element_type=F32) + bi)
    la = -LRU_C * r * sp
    a = jnp.exp(la)
    mult = jnp.sqrt(-jnp.tanh(la) * (a * a + 1.0))
    return r, ig, a, mult


def _mixer_fwd(u, wa, ba, wr_blk, br, wi_blk, bi, lam, wb, *, name):
    tp, din = u.shape
    dl = din // 6
    tt = MIX_ROWS
    cw = GATE_BLOCK
    nch = dl // cw
    assert tp % tt == 0 and dl % cw == 0

    def body(u_ref, wa_ref, ba_ref, wr_ref, br_ref, wi_ref, bi_ref, lam_ref, wb_ref,
             ca_ref, hs_ref, y_ref, xa_tail, v_tail, h_carry, a_s, b_s):
        @pl.when(pl.program_id(0) == 0)
        def _():
            xa_tail[...] = jnp.zeros_like(xa_tail)
            v_tail[...] = jnp.zeros_like(v_tail)
            h_carry[...] = jnp.zeros_like(h_carry)

        for ch in range(nch):
            cs = slice(ch * cw, (ch + 1) * cw)

            def seg(s):
                return slice(s * dl + ch * cw, s * dl + (ch + 1) * cw)

            xa = u_ref[:, seg(0)]
            halo = xa_tail[:, cs]
            ca = ba_ref[:, cs] + wa_ref[3:4, cs] * xa
            for kk in range(3):
                ca = ca + wa_ref[kk:kk + 1, cs] * _shift_down(halo, xa, 3 - kk)
            xa_tail[:, cs] = xa[tt - SUBLANES:]
            ca_ref[:, cs] = ca
            sp = _softplus(-lam_ref[:, cs])
            _, ig, a, mult = _gates(ca, wr_ref[ch], wi_ref[ch], br_ref[:, cs], bi_ref[:, cs], sp)
            a_s[:, cs] = a
            b_s[:, cs] = mult * (ig * ca)

            bv = u_ref[:, seg(2)]
            v = u_ref[:, seg(3)] * u_ref[:, seg(4)]
            gb = u_ref[:, seg(5)]
            vh = v_tail[:, cs]
            cb = wb_ref[2:3, cs] * v
            for kk in range(2):
                cb = cb + wb_ref[kk:kk + 1, cs] * _shift_down(vh, v, 2 - kk)
            v_tail[:, cs] = v[tt - SUBLANES:]
            y_ref[:, dl + ch * cw: dl + (ch + 1) * cw] = (bv * cb * (gb * _sigmoid(gb))).astype(BF16)

        def group(gi, hprev):
            rows = pl.ds(pl.multiple_of(gi * SUBLANES, SUBLANES), SUBLANES)
            a8, b8 = _scan_rows_fwd(a_s[rows, :], b_s[rows, :])
            h8 = b8 + a8 * hprev
            hs_ref[rows, :] = h8
            return jnp.broadcast_to(h8[SUBLANES - 1:SUBLANES, :], h8.shape)

        h_carry[...] = lax.fori_loop(0, tt // SUBLANES, group, h_carry[...], unroll=SCAN_UNROLL)

        for ch in range(nch):
            cs = slice(ch * cw, (ch + 1) * cw)
            ga = u_ref[:, dl + ch * cw: dl + (ch + 1) * cw]
            y_ref[:, cs] = (hs_ref[:, cs] * (ga * _sigmoid(ga))).astype(BF16)

    row = lambda w: pl.BlockSpec((tt, w), lambda i: (i, 0))
    full = lambda shp: pl.BlockSpec(shp, lambda i: tuple(0 for _ in shp))
    return pl.pallas_call(
        body, name=name, grid=(tp // tt,),
        in_specs=[row(din), full((4, dl)), full((1, dl)), full((nch, cw, cw)), full((1, dl)),
                  full((nch, cw, cw)), full((1, dl)), full((1, dl)), full((3, dl))],
        out_specs=[row(dl), row(dl), row(2 * dl)],
        out_shape=[jax.ShapeDtypeStruct((tp, dl), F32), jax.ShapeDtypeStruct((tp, dl), F32),
                   jax.ShapeDtypeStruct((tp, 2 * dl), BF16)],
        scratch_shapes=[pltpu.VMEM((SUBLANES, dl), F32), pltpu.VMEM((SUBLANES, dl), F32),
                        pltpu.VMEM((SUBLANES, dl), F32), pltpu.VMEM((tt, dl), F32), pltpu.VMEM((tt, dl), F32)],
        compiler_params=_params(("arbitrary",)),
    )(u, wa, ba, wr_blk, br, wi_blk, bi, lam, wb)


SG_WA, SG_BA, SG_BR, SG_BI, SG_LAM, SG_WB, SG_ROWS = 0, 4, 5, 6, 7, 8, 16


def _mixer_bwd(u, ca, hs, dy, wa, wr_blk, br, wi_blk, bi, lam, wb, *, name):
    tp, din = u.shape
    dl = din // 6
    tt = MIX_ROWS
    cw = GATE_BLOCK
    nch = dl // cw
    nt = tp // tt
    hb = tt // SUBLANES
    tn_dims = (((0,), (0,)), ((), ()))
    nt_dims = (((1,), (1,)), ((), ()))

    def body(u_ref, uh_ref, ca_ref, hs_ref, hsh_ref, dy_ref, wa_ref, wr_ref, br_ref, wi_ref, bi_ref, lam_ref, wb_ref,
             du_ref, sg_ref, dwr_ref, dwi_ref,
             g_carry, a_head, dca_head, dcb_head, r_s, i_s, a_s, an_s, d_s, g_s):
        i = pl.program_id(0)
        first_tile = i == nt - 1

        @pl.when(i == 0)
        def _():
            for ref in (g_carry, a_head, dca_head, dcb_head, sg_ref, dwr_ref, dwi_ref):
                ref[...] = jnp.zeros_like(ref)

        def halo_of(x):
            return jnp.where(first_tile, 0.0, x)

        for ch in range(nch):
            cs = slice(ch * cw, (ch + 1) * cw)
            cav = ca_ref[:, cs]
            sp = _softplus(-lam_ref[:, cs])
            r, ig, a, _ = _gates(cav, wr_ref[ch], wi_ref[ch], br_ref[:, cs], bi_ref[:, cs], sp)
            r_s[:, cs] = r
            i_s[:, cs] = ig
            a_s[:, cs] = a
            an_s[:, cs] = _shift_up(a, a_head[:, cs], 1)
            a_head[:, cs] = a[:SUBLANES]
            ga = u_ref[:, dl + ch * cw: dl + (ch + 1) * cw]
            d_s[:, cs] = dy_ref[:, cs] * (ga * _sigmoid(ga))

        def group(k, gnext):
            gi = tt // SUBLANES - 1 - k
            rows = pl.ds(pl.multiple_of(gi * SUBLANES, SUBLANES), SUBLANES)
            c8, d8 = _scan_rows_bwd(an_s[rows, :], d_s[rows, :])
            g8 = d8 + c8 * gnext
            g_s[rows, :] = g8
            return jnp.broadcast_to(g8[0:1, :], g8.shape)

        g_carry[...] = lax.fori_loop(0, tt // SUBLANES, group, g_carry[...], unroll=SCAN_UNROLL)

        def acc_row(r0, val):
            sg_ref[r0:r0 + 1, cs_cur[0]] += jnp.sum(val, axis=0, keepdims=True)

        cs_cur = [None]
        for ch in range(nch):
            cs = slice(ch * cw, (ch + 1) * cw)
            cs_cur[0] = cs

            def seg(s):
                return slice(s * dl + ch * cw, s * dl + (ch + 1) * cw)

            cav = ca_ref[:, cs]
            r = r_s[:, cs]
            ig = i_s[:, cs]
            a = a_s[:, cs]
            g = g_s[:, cs]
            hsv = hs_ref[:, cs]
            lamv = lam_ref[:, cs]
            sp = _softplus(-lamv)
            la = -LRU_C * r * sp
            e2 = a * a
            one_m_e2 = -jnp.tanh(la) * (e2 + 1.0)
            mult = jnp.sqrt(one_m_e2)
            hprev = _shift_down(halo_of(hsh_ref[:, cs]), hsv, 1)
            icav = ig * cav
            dla = g * (hprev * a - icav * (e2 * lax.rsqrt(one_m_e2)))
            gm = g * mult
            dzi = gm * icav * (1.0 - ig)
            dca = gm * ig
            dla_r = dla * r
            dzr = dla_r * (1.0 - r) * (-LRU_C * sp)
            sg_ref[SG_LAM:SG_LAM + 1, cs] += jnp.sum(dla_r, axis=0, keepdims=True) * (LRU_C * _sigmoid(-lamv))
            acc_row(SG_BR, dzr)
            acc_row(SG_BI, dzi)
            dzr_b = dzr.astype(BF16)
            dzi_b = dzi.astype(BF16)
            cab = cav.astype(BF16)
            dca = dca + lax.dot_general(dzr_b, wr_ref[ch], nt_dims, preferred_element_type=F32)
            dca = dca + lax.dot_general(dzi_b, wi_ref[ch], nt_dims, preferred_element_type=F32)
            dwr_ref[ch] += lax.dot_general(cab, dzr_b, tn_dims, preferred_element_type=F32)
            dwi_ref[ch] += lax.dot_general(cab, dzi_b, tn_dims, preferred_element_type=F32)
            acc_row(SG_BA, dca)
            xa = u_ref[:, seg(0)]
            head = dca_head[:, cs]
            dxa = wa_ref[3:4, cs] * dca
            acc_row(SG_WA + 3, dca * xa)
            for kk in range(3):
                later = _shift_up(dca, head, 3 - kk)
                acc_row(SG_WA + kk, later * xa)
                dxa = dxa + wa_ref[kk:kk + 1, cs] * later
            dca_head[:, cs] = dca[:SUBLANES]
            ga = u_ref[:, seg(1)]
            sga = _sigmoid(ga)
            dga = dy_ref[:, cs] * hsv * (sga + (ga * sga) * (1.0 - sga))
            du_ref[:, seg(0)] = dxa.astype(BF16)
            du_ref[:, seg(1)] = dga.astype(BF16)

            bv = u_ref[:, seg(2)]
            cv = u_ref[:, seg(3)]
            xb = u_ref[:, seg(4)]
            gb = u_ref[:, seg(5)]
            dyb = dy_ref[:, dl + ch * cw: dl + (ch + 1) * cw]
            v = cv * xb
            vh = halo_of(uh_ref[:, seg(3)] * uh_ref[:, seg(4)])
            v1 = _shift_down(vh, v, 1)
            v2 = _shift_down(vh, v, 2)
            cb = wb_ref[2:3, cs] * v + wb_ref[1:2, cs] * v1 + wb_ref[0:1, cs] * v2
            sgb = _sigmoid(gb)
            sl = gb * sgb
            dyb_b = dyb * bv
            dyb_cb = dyb * cb
            dcb = dyb_b * sl
            du_ref[:, seg(2)] = (dyb_cb * sl).astype(BF16)
            du_ref[:, seg(5)] = (dyb_cb * bv * (sgb + sl * (1.0 - sgb))).astype(BF16)
            bhead = dcb_head[:, cs]
            dv = wb_ref[2:3, cs] * dcb
            acc_row(SG_WB + 2, dcb * v)
            for kk in range(2):
                later = _shift_up(dcb, bhead, 2 - kk)
                acc_row(SG_WB + kk, later * v)
                dv = dv + wb_ref[kk:kk + 1, cs] * later
            dcb_head[:, cs] = dcb[:SUBLANES]
            du_ref[:, seg(3)] = (dv * xb).astype(BF16)
            du_ref[:, seg(4)] = (dv * cv).astype(BF16)

    rev = lambda w: pl.BlockSpec((tt, w), lambda i: (nt - 1 - i, 0))
    halo = lambda w: pl.BlockSpec((SUBLANES, w), lambda i: (jnp.maximum((nt - 1 - i) * hb - 1, 0), 0))
    full = lambda shp: pl.BlockSpec(shp, lambda i: tuple(0 for _ in shp))
    vm = lambda r: pltpu.VMEM((r, dl), F32)
    return pl.pallas_call(
        body, name=name, grid=(nt,),
        in_specs=[rev(din), halo(din), rev(dl), rev(dl), halo(dl), rev(2 * dl), full((4, dl)),
                  full((nch, cw, cw)), full((1, dl)), full((nch, cw, cw)), full((1, dl)), full((1, dl)), full((3, dl))],
        out_specs=[rev(din), full((SG_ROWS, dl)), full((nch, cw, cw)), full((nch, cw, cw))],
        out_shape=[jax.ShapeDtypeStruct((tp, din), BF16), jax.ShapeDtypeStruct((SG_ROWS, dl), F32),
                   jax.ShapeDtypeStruct((nch, cw, cw), F32), jax.ShapeDtypeStruct((nch, cw, cw), F32)],
        scratch_shapes=[vm(SUBLANES), vm(SUBLANES), vm(SUBLANES), vm(SUBLANES),
                        vm(tt), vm(tt), vm(tt), vm(tt), vm(tt), vm(tt)],
        compiler_params=_params(("arbitrary",)),
    )(u, u, ca, hs, hs, dy, wa, wr_blk, br, wi_blk, bi, lam, wb)


def _adamw(w, g, m, v, *, name, landed=None, layer=None, depth=None, into=None, row_off=0):
    r, c = w.shape
    rows = g.shape[0]
    tr = _tile(rows, 256, 2 * SUBLANES)
    assert row_off % tr == 0
    boff = row_off // tr
    bc1 = 1.0 - ADAM_B1 ** ADAM_STEP
    bc2 = 1.0 - ADAM_B2 ** ADAM_STEP
    slots = landed is not None

    def body(*refs):
        if into is not None:
            refs = refs[:-8] + refs[-4:]
        if slots:
            w_ref, g_ref, l_ref, m_ref, v_ref, grad_ref, delta_ref, nm_ref, nv_ref = refs
            gv = g_ref[...].astype(F32)
            for s in range(N_DEV - 1):
                gv = gv + l_ref[s].astype(F32)
        else:
            w_ref, g_ref, m_ref, v_ref, grad_ref, delta_ref, nm_ref, nv_ref = refs
            gv = g_ref[...]
        wv = w_ref[...]
        mn = ADAM_B1 * m_ref[...] + (1.0 - ADAM_B1) * gv
        vn = ADAM_B2 * v_ref[...] + (1.0 - ADAM_B2) * (gv * gv)
        m_hat = mn / bc1
        v_hat = vn / bc2
        grad_ref[...] = gv
        delta_ref[...] = -ADAM_LR * (m_hat / (jnp.sqrt(v_hat) + ADAM_EPS) + ADAM_WD * wv)
        nm_ref[...] = mn
        nv_ref[...] = vn

    blk = pl.BlockSpec((tr, c), lambda i: (i + boff, 0))
    g_blk = pl.BlockSpec((tr, c), lambda i: (i, 0))
    l_spec = [pl.BlockSpec((N_DEV - 1, tr, c), lambda i: (0, i, 0))] if slots else []
    args = (w, g, landed, m, v) if slots else (w, g, m, v)
    in_specs = [blk, g_blk] + l_spec + [blk, blk]
    if depth is None:
        shp = jax.ShapeDtypeStruct((r, c), F32)
        out_blk = blk
    else:
        shp = jax.ShapeDtypeStruct((depth, r, c), F32)
        out_blk = pl.BlockSpec((None, tr, c), lambda i: (layer, i + boff, 0))
    aliases = {}
    if into is not None:
        aliases = {len(args) + j: j for j in range(4)}
        in_specs = in_specs + [ANY] * 4
        args = args + tuple(into)
    return pl.pallas_call(
        body, name=name, grid=(rows // tr,),
        in_specs=in_specs, out_specs=[out_blk] * 4,
        out_shape=[shp] * 4, input_output_aliases=aliases,
        compiler_params=_params(("parallel",)),
    )(*args)


def _slot_sum(g, *, name):
    _, r, c = g.shape
    tr = _tile(r, 512, SUBLANES)

    def body(g_ref, o_ref):
        gv = g_ref[0]
        for s in range(1, N_DEV):
            gv = gv + g_ref[s]
        o_ref[...] = gv

    return pl.pallas_call(
        body, name=name, grid=(r // tr,),
        in_specs=[pl.BlockSpec((N_DEV, tr, c), lambda i: (0, i, 0))],
        out_specs=pl.BlockSpec((tr, c), lambda i: (i, 0)),
        out_shape=jax.ShapeDtypeStruct((r, c), F32),
        compiler_params=_params(("parallel",)),
    )(g)


def _mesh_pos():
    x, y, c = lax.axis_index("x"), lax.axis_index("y"), lax.axis_index("c")
    return x, y, c, 4 * x + 2 * y + c


ANY = pl.BlockSpec(memory_space=pl.ANY)


def _all_gather(srcs, out_shapes, views, *, name, place=()):
    n = len(srcs)
    npl = len(place)

    def body(*refs):
        src = refs[:n]
        psrc = refs[n:n + npl]
        dst = refs[n + npl:2 * n + npl]
        pdst = refs[2 * n + npl:2 * (n + npl)]
        send_sems, recv_sems, local_sems = refs[2 * (n + npl):]
        x, y, c, me = _mesh_pos()
        sibling = (x, y, 1 - c)
        chips = [(1 - x, y), (x, 1 - y), (1 - x, 1 - y)]

        def dev(px, py, pc):
            return 4 * px + 2 * py + pc

        def copy(a, k, block, to, from_src=False):
            win = views[a](dst[a], dev(*block))
            return pltpu.make_async_remote_copy(
                src_ref=src[a] if from_src else win, dst_ref=win,
                send_sem=send_sems.at[a * 7 + k], recv_sem=recv_sems.at[a * 7 + k],
                device_id=to, device_id_type=MESH)

        mine = [pltpu.make_async_copy(src[a], views[a](dst[a], me), local_sems.at[a]) for a in range(n)]
        mine += [pltpu.make_async_copy(psrc[j], place[j][2](pdst[j], me), local_sems.at[n + j]) for j in range(npl)]
        started = []
        for a in range(n):
            mine[a].start()
            first = [copy(a, 0, (x, y, c), sibling, True)]
            first += [copy(a, 1 + j, (x, y, c), (*chip, c), True) for j, chip in enumerate(chips)]
            for cp in first:
                cp.start()
            started += first
        for cp in mine[n:]:
            cp.start()
        for a in range(n):
            for j, chip in enumerate(chips):
                copy(a, 1 + j, (*chip, c), (x, y, c)).wait_recv()
                fwd = copy(a, 4 + j, (*chip, c), sibling)
                fwd.start()
                started.append(fwd)
        for a in range(n):
            copy(a, 0, (x, y, 1 - c), (x, y, c)).wait_recv()
            for j, chip in enumerate(chips):
                copy(a, 4 + j, (*chip, 1 - c), (x, y, c)).wait_recv()
        for cp in started:
            cp.wait_send()
        for cp in mine:
            cp.wait()

    return pl.pallas_call(
        body, name=name,
        in_specs=[ANY] * (n + npl), out_specs=[ANY] * (n + npl),
        out_shape=[jax.ShapeDtypeStruct(s, x.dtype) for s, x in zip(out_shapes, srcs)]
        + [jax.ShapeDtypeStruct(shape, arr.dtype) for arr, shape, _ in place],
        scratch_shapes=[pltpu.SemaphoreType.DMA((7 * n,)), pltpu.SemaphoreType.DMA((7 * n,)),
                        pltpu.SemaphoreType.DMA((n + npl,))],
    )(*srcs, *[arr for arr, _, _ in place])


HBM = pl.BlockSpec(memory_space=pltpu.HBM)
SEM = pl.BlockSpec(memory_space=pltpu.SEMAPHORE)
EFFECT = pltpu.SideEffectType.DATAFLOW_SIDE_EFFECTING


def _peer_of(x, y, c, k):
    return (1 - x if k & 4 else x, 1 - y if k & 2 else y, 1 - c if k & 1 else c)


def _peer_copies(n, wins, src, land, send_sems, recv_sems):
    x, y, c, me = _mesh_pos()
    out = []
    for a in range(n):
        for k in range(1, N_DEV):
            px, py, pc = _peer_of(x, y, c, k)
            s_win, d_win = wins[a](src[a], land[a], me, 4 * px + 2 * py + pc, k)
            out.append(pltpu.make_async_remote_copy(
                src_ref=s_win, dst_ref=d_win,
                send_sem=send_sems.at[a * 7 + k - 1], recv_sem=recv_sems.at[a * 7 + k - 1],
                device_id=(px, py, pc), device_id_type=MESH))
    return out


def _push_start(srcs, lands, wins, *, name):
    n = len(srcs)

    def body(*refs):
        src = refs[:n]
        land = refs[n:2 * n]
        send_sems, recv_sems = refs[2 * n], refs[2 * n + 1]
        token = refs[-1]
        for cp in _peer_copies(n, wins, src, land, send_sems, recv_sems):
            cp.start()
        token[...] = jnp.zeros_like(token)

    bufs = (*srcs, *lands)
    return pl.pallas_call(
        body, name=name,
        out_shape=(pltpu.SemaphoreType.DMA((7 * n,)), pltpu.SemaphoreType.DMA((7 * n,)),
                   *[pltpu.HBM(v.shape, v.dtype) for v in bufs], jax.ShapeDtypeStruct((SUBLANES, LANES), F32)),
        in_specs=[HBM] * (2 * n),
        out_specs=(SEM, SEM, *[HBM] * (2 * n), pl.BlockSpec(memory_space=pltpu.VMEM)),
        input_output_aliases={i: 2 + i for i in range(2 * n)},
        compiler_params=pltpu.CompilerParams(has_side_effects=EFFECT),
    )(*[pltpu.with_memory_space_constraint(v, pltpu.HBM) for v in bufs])


def _push_wait(handle, wins, after, *, name):
    send_sems, recv_sems, *bufs, _ = handle
    n = len(bufs) // 2

    def body(*refs):
        src = refs[:n]
        land = refs[n:2 * n]
        for cp in _peer_copies(n, wins, src, land, refs[2 * n], refs[2 * n + 1]):
            cp.wait_send()
            cp.wait_recv()

    outs = pl.pallas_call(
        body, name=name,
        out_shape=tuple(pltpu.HBM(v.shape, v.dtype) for v in bufs),
        in_specs=[HBM] * (2 * n) + [SEM, SEM, ANY],
        out_specs=tuple([HBM] * (2 * n)),
        input_output_aliases={i: i for i in range(2 * n)},
        compiler_params=pltpu.CompilerParams(has_side_effects=EFFECT),
    )(*bufs, send_sems, recv_sems, after)
    return outs[:n], outs[n:]


def _gather_lead(src, land, me, peer, k):
    return src, land.at[me]


def _gather_cols(width):
    def win(src, land, me, peer, k):
        return src, land.at[:, pl.ds(me * width, width)]
    return win


def _scatter_lead(src, land, me, peer, k):
    return src.at[peer], land.at[k - 1]


def _scatter_cols(width):
    def win(src, land, me, peer, k):
        return src.at[:, pl.ds(peer * width, width)], land.at[k - 1]
    return win


def _place_block(own, *, cols, name):
    rows, width = own.shape
    tr = _tile(rows, 512, 2 * SUBLANES)
    _, _, _, me = _mesh_pos()

    def body(me_ref, x_ref, o_ref):
        o_ref[...] = x_ref[...]

    if cols:
        out_spec = pl.BlockSpec((tr, width), lambda i, me_ref: (i, me_ref[0]))
        shape = (rows, N_DEV * width)
    else:
        out_spec = pl.BlockSpec((None, tr, width), lambda i, me_ref: (me_ref[0], i, 0))
        shape = (N_DEV, rows, width)
    return pl.pallas_call(
        body, name=name,
        grid_spec=pltpu.PrefetchScalarGridSpec(
            num_scalar_prefetch=1, grid=(rows // tr,),
            in_specs=[pl.BlockSpec((tr, width), lambda i, me_ref: (i, 0))], out_specs=out_spec),
        out_shape=jax.ShapeDtypeStruct(shape, own.dtype),
        compiler_params=_params(("arbitrary",)),
    )(me.astype(jnp.int32).reshape(1), own)


def _dep(x, token):
    return x + token[0, 0].astype(x.dtype)


def _lead(ref, d):
    return ref.at[d]


def _col_window(width):
    def view(ref, d):
        return ref.at[:, pl.ds(d * width, width)]
    return view


def _pack(arrs):
    flat = jnp.concatenate([a.reshape(-1).astype(F32) for a in arrs])
    n = flat.shape[0]
    rows = -(-n // (SUBLANES * LANES)) * SUBLANES
    return jnp.pad(flat, (0, rows * LANES - n)).reshape(rows, LANES)


def _unpack(buf, shapes):
    flat = buf.reshape(-1)
    out, off = [], 0
    for s in shapes:
        n = 1
        for q in s:
            n *= q
        out.append(flat[off:off + n].reshape(s))
        off += n
    return out


def _blockdiag(w, cw):
    h, hd, _ = w.shape
    per = cw // hd
    wg = w.reshape(h // per, per, hd, hd)
    eye = jnp.eye(per, dtype=w.dtype)
    blk = jnp.einsum("gpij,pq->gpiqj", wg, eye)
    return blk.reshape(h // per, cw, cw).astype(BF16)


def _blockdiag_extract(g, hd):
    n, cw, _ = g.shape
    per = cw // hd
    g5 = g.reshape(n, per, hd, per, hd)
    idx = jnp.arange(per)
    return g5[:, idx, :, idx, :].transpose(1, 0, 2, 3).reshape(n * per, hd, hd)


def kernel(x, meta, norm_g, w_in, conv_a_w, conv_a_b, lru_wr, lru_br, lru_wi, lru_bi, lru_lambda, conv_b_w, w_out, final_g, loss_target, m_meta, m_norm_g, m_w_in, m_conv_a_w, m_conv_a_b, m_lru_wr, m_lru_br, m_lru_wi, m_lru_bi, m_lru_lambda, m_conv_b_w, m_w_out, m_final_g, v_meta, v_norm_g, v_w_in, v_conv_a_w, v_conv_a_b, v_lru_wr, v_lru_br, v_lru_wi, v_lru_bi, v_lru_lambda, v_conv_b_w, v_w_out, v_final_g):
    _, seq, d = x.shape
    n_meta = meta.shape[0]
    depth = w_in.shape[0]
    din = w_in.shape[2] * N_DEV
    dl = din // 6
    dmix = 2 * dl
    wcol = w_in.shape[2]
    wrow = w_out.shape[1]
    mcol = meta.shape[1]
    ccol = conv_a_w.shape[2]
    heads, hd = lru_wr.shape[1], lru_wr.shape[2]
    n_tok = n_meta + seq
    tp = -(-n_tok // TOKEN_TILE) * TOKEN_TILE
    me = 4 * lax.axis_index("x") + 2 * lax.axis_index("y") + lax.axis_index("c")

    bf = lambda a: a.astype(BF16)
    small_mine = _pack([meta, conv_a_w, conv_b_w])
    first = _all_gather([bf(w_in[0]), small_mine], [(d, din), (N_DEV,) + small_mine.shape],
                        [_col_window(wcol), _lead], name="gather_first")
    parts = [_unpack(first[1][s], [meta.shape, conv_a_w.shape, conv_b_w.shape]) for s in range(N_DEV)]
    meta_full = jnp.concatenate([p[0] for p in parts], axis=1)
    wa_full = jnp.concatenate([p[1] for p in parts], axis=2)
    wb_full = jnp.concatenate([p[2] for p in parts], axis=2)
    w_in_full = [None] * depth
    w_out_full = [None] * depth

    push_out = [None] * depth
    push_in = [None] * depth
    w_in_full[0], src = lax.optimization_barrier((first[0], bf(w_out[0])))
    push_out[0] = _push_start([src], [_place_block(src, cols=False, name="place_wout_0")], [_gather_lead],
                              name="gather_wout_0_start")
    token = push_out[0][-1]
    for l in range(1, depth):
        src = bf(_dep(w_in[l], token))
        push_in[l] = _push_start([src], [_place_block(src, cols=True, name=f"place_win_{l}")], [_gather_cols(wcol)],
                                 name=f"gather_win_{l}_start")
        src = bf(_dep(w_out[l], push_in[l][-1]))
        push_out[l] = _push_start([src], [_place_block(src, cols=False, name=f"place_wout_{l}")], [_gather_lead],
                                  name=f"gather_wout_{l}_start")
        token = push_out[l][-1]

    wr_blk = [_blockdiag(lru_wr[l], GATE_BLOCK) for l in range(depth)]
    wi_blk = [_blockdiag(lru_wi[l], GATE_BLOCK) for l in range(depth)]
    vec = lambda a: a.reshape(1, dl)

    h = jnp.concatenate([meta_full, x[0], jnp.zeros((tp - n_tok, d), F32)], axis=0)
    tgt = jnp.pad(loss_target[0], ((n_meta, tp - n_tok), (0, 0)))
    tm = _tile(tp, 1408)
    saved = []
    for l in range(depth):
        hn = _rms_fwd(h, _dep(norm_g[l], token) if l == 0 else norm_g[l], name=f"rms_fwd_{l}")
        if l > 0:
            _, landed = _push_wait(push_in[l], [_gather_cols(wcol)], hn, name=f"gather_win_{l}_wait")
            w_in_full[l] = landed[0]
        u = _matmul(hn, w_in_full[l], tm=tm, tn=_tile(din, 768), tk=d, name=f"mm_u_{l}")
        ca, hs, y = _mixer_fwd(u, wa_full[l], vec(conv_a_b[l]), wr_blk[l], vec(lru_br[l]), wi_blk[l], vec(lru_bi[l]),
                               vec(lru_lambda[l]), wb_full[l], name=f"mixer_fwd_{l}")
        _, landed = _push_wait(push_out[l], [_gather_lead], y, name=f"gather_wout_{l}_wait")
        w_out_full[l] = landed[0].reshape(dmix, d)
        h_next = _matmul(y, w_out_full[l], tm=tm, tn=_tile(d, 512), tk=dmix, add=h, name=f"mm_out_{l}")
        saved.append((h, hn, u, ca, hs, y))
        h = h_next

    dh, dhb, dg_final, loss_part = _loss_head(h, tgt, final_g, n_meta=n_meta, n_tok=n_tok, name="loss_head")
    loss = lax.psum(loss_part[0, 0], ("x", "y", "c"))

    small_grads = [None] * depth
    sent_out = [None] * depth
    sent_in = [None] * depth
    scatter_in = [_scatter_cols(wcol)]
    token = None
    dg_norms = []
    for l in reversed(range(depth)):
        h_in, hn, u, ca, hs, y = saved[l]
        dy = _matmul(dhb, w_out_full[l], tb=True, tm=tm, tn=_tile(dmix, 512), tk=d, dep=token, name=f"mm_dy_{l}")
        dw_out = _matmul(y, dhb, ta=True, tm=_tile(dmix, 1024), tn=_tile(d, 1024), tk=tm, out_dtype=BF16,
                         name=f"mm_dwout_{l}")
        sent_out[l] = _push_start([dw_out.reshape(N_DEV, wrow, d)], [lax.empty((N_DEV - 1, wrow, d), BF16)],
                                  [_scatter_lead], name=f"scatter_wout_{l}_start")
        du, sg, dwr, dwi = _mixer_bwd(u, ca, hs, dy, wa_full[l], wr_blk[l], vec(lru_br[l]), wi_blk[l], vec(lru_bi[l]),
                                      vec(lru_lambda[l]), _dep(wb_full[l], sent_out[l][-1]), name=f"mixer_bwd_{l}")
        small_grads[l] = (sg, dwr, dwi)
        if l == 0:
            early = _pack([
                jnp.stack([small_grads[j][0][SG_BA] for j in range(depth)]),
                jnp.stack([_blockdiag_extract(small_grads[j][1], hd) for j in range(depth)]),
                jnp.stack([small_grads[j][0][SG_BR] for j in range(depth)]),
                jnp.stack([_blockdiag_extract(small_grads[j][2], hd) for j in range(depth)]),
                jnp.stack([small_grads[j][0][SG_BI] for j in range(depth)]),
                jnp.stack([small_grads[j][0][SG_LAM] for j in range(depth)]),
                jnp.stack([small_grads[j][0][SG_WA:SG_WA + 4] for j in range(depth)]),
                jnp.stack([small_grads[j][0][SG_WB:SG_WB + 3] for j in range(depth)]),
                dg_final[0], *dg_norms])
            early_land = lax.dynamic_update_slice(lax.empty((N_DEV,) + early.shape, F32), early[None], (me, 0, 0))
            sent_early = _push_start([early], [early_land], [_gather_lead], name="gather_early_grads_start")
        parts = 2 if l == 0 else 1
        token = sent_early[-1] if l == 0 else None
        sent_in[l] = []
        for p in range(parts):
            dw_in = _matmul(hn, du, ta=True, tm=_tile(d // parts, 1024), tn=_tile(din, 1536), tk=tm, out_dtype=BF16,
                            dep=token, m_part=(p, parts), name=f"mm_dwin_{l}_{p}")
            sent_in[l].append(_push_start([dw_in], [lax.empty((N_DEV - 1, d // parts, wcol), BF16)], scatter_in,
                                          name=f"scatter_win_{l}_{p}_start"))
            token = sent_in[l][-1][-1]
        dhn = _matmul(du, w_in_full[l], tb=True, tm=tm, tn=_tile(d, 1024), tk=_tile(din, 1536), dep=token,
                      name=f"mm_dhn_{l}")
        dh, dhb, dg_norm = _rms_bwd(h_in, dhn, dh, norm_g[l], name=f"rms_bwd_{l}")
        if l > 0:
            dg_norms.append(dg_norm[0])

    late = _pack([dg_norm[0], dh[:n_meta]])
    late_all = _all_gather([late], [(N_DEV,) + late.shape], [_lead], name="gather_late_grads")[0]
    late_sum = _unpack(_slot_sum(late_all, name="sum_late_grads"), [(d,), (n_meta, d)])
    _, early_all = _push_wait(sent_early, [_gather_lead], late_sum[0], name="gather_early_grads_wait")
    early_shapes = [conv_a_b.shape, lru_wr.shape, lru_br.shape, lru_wi.shape, lru_bi.shape, lru_lambda.shape,
                    (depth, 4, dl), (depth, 3, dl), final_g.shape] + [(d,)] * (depth - 1)
    e = _unpack(_slot_sum(early_all[0], name="sum_early_grads"), early_shapes)
    g_norm = jnp.stack([late_sum[0]] + e[9:][::-1])
    g_meta = lax.dynamic_slice_in_dim(late_sum[1], me * mcol, mcol, axis=1)
    g_wa = lax.dynamic_slice_in_dim(e[6], me * ccol, ccol, axis=2)
    g_wb = lax.dynamic_slice_in_dim(e[7], me * ccol, ccol, axis=2)

    small_w = [norm_g, conv_a_b, lru_wr, lru_br, lru_wi, lru_bi, lru_lambda, final_g, meta, conv_a_w, conv_b_w]
    small_m = [m_norm_g, m_conv_a_b, m_lru_wr, m_lru_br, m_lru_wi, m_lru_bi, m_lru_lambda, m_final_g, m_meta,
               m_conv_a_w, m_conv_b_w]
    small_v = [v_norm_g, v_conv_a_b, v_lru_wr, v_lru_br, v_lru_wi, v_lru_bi, v_lru_lambda, v_final_g, v_meta,
               v_conv_a_w, v_conv_b_w]
    small_g = [g_norm, e[0], e[1], e[2], e[3], e[4], e[5], e[8], g_meta, g_wa, g_wb]
    small_out = _adamw(_pack(small_w), _pack(small_g), _pack(small_m), _pack(small_v), name="adamw_small")
    small_shapes = [a.shape for a in small_w]
    s_grad, s_delta, s_m, s_v = [_unpack(o, small_shapes) for o in small_out]

    win_out = None
    wout_out = None
    after = small_out[0]
    for l in reversed(range(depth)):
        src, landed = _push_wait(sent_out[l], [_scatter_lead], after, name=f"scatter_wout_{l}_wait")
        own = lax.dynamic_index_in_dim(src[0], me, 0, keepdims=False)
        wout_out = _adamw(w_out[l], own, m_w_out[l], v_w_out[l], landed=landed[0], layer=l, depth=depth,
                          into=wout_out, name=f"adamw_w_out_{l}")
        after = wout_out[0]
        for p, sent in enumerate(sent_in[l]):
            src, landed = _push_wait(sent, scatter_in, after, name=f"scatter_win_{l}_{p}_wait")
            own = lax.dynamic_slice_in_dim(src[0], me * wcol, wcol, axis=1)
            win_out = _adamw(w_in[l], own, m_w_in[l], v_w_in[l], landed=landed[0], layer=l, depth=depth,
                             into=win_out, row_off=p * own.shape[0], name=f"adamw_w_in_{l}_{p}")
            after = win_out[0]

    names = ["norm_g", "conv_a_b", "lru_wr", "lru_br", "lru_wi", "lru_bi", "lru_lambda", "final_g", "meta",
             "conv_a_w", "conv_b_w"]
    order = ["meta", "norm_g", "w_in", "conv_a_w", "conv_a_b", "lru_wr", "lru_br", "lru_wi", "lru_bi", "lru_lambda",
             "conv_b_w", "w_out", "final_g"]

    def family(idx, small):
        table = {nm: small[i] for i, nm in enumerate(names)}
        table["w_in"] = win_out[idx]
        table["w_out"] = wout_out[idx]
        return [table[nm] for nm in order]

    grad_x = dh[n_meta:n_tok][None]
    return (loss, grad_x, *family(0, s_grad), *family(1, s_delta), *family(2, s_m), *family(3, s_v))
```

```python
import functools

import jax
import jax.numpy as jnp
from jax import lax
from jax.experimental import pallas as pl
from jax.experimental.pallas import tpu as pltpu

F32 = jnp.float32
BF16 = jnp.bfloat16
MESH = pl.DeviceIdType.MESH

N_DEV = 8
RMS_EPS = 1e-6
LRU_C = 8.0
ADAM_LR = 0.001
ADAM_B1 = 0.9
ADAM_B2 = 0.999
ADAM_EPS = 1e-08
ADAM_WD = 0.01
ADAM_STEP = 10

V7X_VMEM_LIMIT = 52 * 1024 * 1024
LANES = 128
SUBLANES = 8
TOKEN_TILE = 384
MIX_ROWS = 128
GATE_BLOCK = 256
SCAN_UNROLL = 4


def _params(sem):
    return pltpu.CompilerParams(dimension_semantics=sem, vmem_limit_bytes=V7X_VMEM_LIMIT)


def _tile(n, target, align=LANES):
    best = None
    for t in range(align, min(n, target) + 1, align):
        if n % t == 0:
            best = t
    return n if best is None else best


def _sigmoid(z):
    return 0.5 * jnp.tanh(0.5 * z) + 0.5


def _softplus(z):
    e = jnp.exp(-jnp.abs(z))
    u = 1.0 + e
    l1p = jnp.where(u == 1.0, e, jnp.log(u) * e / jnp.where(u == 1.0, 1.0, u - 1.0))
    return jnp.maximum(z, 0.0) + l1p


def _matmul(a, b, *, ta=False, tb=False, tm, tn, tk, out_dtype=F32, add=None, dep=None, m_part=None, name):
    m, k = (a.shape[1], a.shape[0]) if ta else a.shape
    m_off = 0
    if m_part is not None:
        assert add is None and m % (m_part[1] * tm) == 0
        m //= m_part[1]
        m_off = m_part[0] * (m // tm)
    n, kb = b.shape if tb else b.shape[::-1]
    assert kb == k
    assert m % tm == 0 and n % tn == 0 and k % tk == 0, (m, n, k, tm, tn, tk)
    nk = k // tk
    a_spec = pl.BlockSpec((tk, tm), lambda i, j, q: (q, i + m_off)) if ta \
        else pl.BlockSpec((tm, tk), lambda i, j, q: (i + m_off, q))
    b_spec = pl.BlockSpec((tn, tk), lambda i, j, q: (j, q)) if tb else pl.BlockSpec((tk, tn), lambda i, j, q: (q, j))
    o_spec = pl.BlockSpec((tm, tn), lambda i, j, q: (i, j))
    o_shape = (m, n)
    dims = (((0 if ta else 1,), (1 if tb else 0,)), ((), ()))
    has_add = add is not None
    has_dep = dep is not None

    def body(*refs):
        if has_dep:
            refs = refs[:-3] + refs[-2:]
        if has_add:
            a_ref, b_ref, add_ref, o_ref, acc_ref = refs
        else:
            a_ref, b_ref, o_ref, acc_ref = refs
        q = pl.program_id(2)
        part = lax.dot_general(a_ref[...], b_ref[...], dims, preferred_element_type=F32)

        def finish(acc):
            if has_add:
                acc = acc + add_ref[...]
            o_ref[...] = acc.astype(out_dtype)

        if nk == 1:
            finish(part)
        else:
            @pl.when(q == 0)
            def _():
                acc_ref[...] = part

            @pl.when(jnp.logical_and(q > 0, q < nk - 1))
            def _():
                acc_ref[...] += part

            @pl.when(q == nk - 1)
            def _():
                finish(acc_ref[...] + part)

    in_specs = [a_spec, b_spec] + ([o_spec] if has_add else [])
    args = (a, b) + ((add,) if has_add else ())
    if has_dep:
        in_specs.append(pl.BlockSpec((SUBLANES, LANES), lambda i, j, q: (0, 0)))
        args += (dep,)
    acc_shape = (tm, tn) if nk > 1 else (SUBLANES, LANES)
    return pl.pallas_call(
        body, name=name,
        grid=(m // tm, n // tn, nk),
        in_specs=in_specs, out_specs=o_spec,
        out_shape=jax.ShapeDtypeStruct(o_shape, out_dtype),
        scratch_shapes=[pltpu.VMEM(acc_shape, F32)],
        compiler_params=_params(("parallel", "parallel", "arbitrary")),
    )(*args)


def _rms_fwd(h, g, *, name):
    tp, d = h.shape
    tr = _tile(tp, 512, SUBLANES)

    def body(h_ref, g_ref, o_ref):
        hv = h_ref[...]
        rstd = lax.rsqrt(jnp.mean(hv * hv, axis=-1, keepdims=True) + RMS_EPS)
        o_ref[...] = (hv * rstd * g_ref[...]).astype(BF16)

    return pl.pallas_call(
        body, name=name, grid=(tp // tr,),
        in_specs=[pl.BlockSpec((tr, d), lambda i: (i, 0)), pl.BlockSpec((1, d), lambda i: (0, 0))],
        out_specs=pl.BlockSpec((tr, d), lambda i: (i, 0)),
        out_shape=jax.ShapeDtypeStruct((tp, d), BF16),
        compiler_params=_params(("parallel",)),
    )(h, g.reshape(1, d))


def _rms_bwd(h, dhn, dout, g, *, name):
    tp, d = h.shape
    tr = _tile(tp, 384, SUBLANES)

    def body(h_ref, dhn_ref, dout_ref, g_ref, dh_ref, dhb_ref, dg_ref):
        hv = h_ref[...]
        rstd = lax.rsqrt(jnp.mean(hv * hv, axis=-1, keepdims=True) + RMS_EPS)
        xhat = hv * rstd
        dn = dhn_ref[...]
        dxhat = dn * g_ref[...]
        dh = dout_ref[...] + rstd * (dxhat - xhat * jnp.mean(dxhat * xhat, axis=-1, keepdims=True))
        dh_ref[...] = dh
        dhb_ref[...] = dh.astype(BF16)
        part = jnp.sum(dn * xhat, axis=0, keepdims=True)

        @pl.when(pl.program_id(0) == 0)
        def _():
            dg_ref[...] = part

        @pl.when(pl.program_id(0) > 0)
        def _():
            dg_ref[...] += part

    row = pl.BlockSpec((tr, d), lambda i: (i, 0))
    vec = pl.BlockSpec((1, d), lambda i: (0, 0))
    return pl.pallas_call(
        body, name=name, grid=(tp // tr,),
        in_specs=[row, row, row, vec],
        out_specs=[row, row, vec],
        out_shape=[jax.ShapeDtypeStruct((tp, d), F32), jax.ShapeDtypeStruct((tp, d), BF16),
                   jax.ShapeDtypeStruct((1, d), F32)],
        compiler_params=_params(("arbitrary",)),
    )(h, dhn, dout, g.reshape(1, d))


def _rms_bwd_first(h, dhn, dout, g, *, n_meta, seq, name):
    tp, d = h.shape
    tr = MIX_ROWS
    assert seq % tr == 0 and tr % n_meta == 0 and tp >= seq + n_meta
    nt = seq // tr
    per = tr // n_meta

    def grads(hv, dn, do, gv):
        rstd = lax.rsqrt(jnp.mean(hv * hv, axis=-1, keepdims=True) + RMS_EPS)
        xhat = hv * rstd
        dxhat = dn * gv
        dh = do + rstd * (dxhat - xhat * jnp.mean(dxhat * xhat, axis=-1, keepdims=True))
        return dh, jnp.sum(dn * xhat, axis=0, keepdims=True)

    def body(h_ref, dhn_ref, dout_ref, hn_ref, dhnn_ref, doutn_ref, g_ref, gx_ref, dmeta_ref, dg_ref):
        i = pl.program_id(0)
        gv = g_ref[...]
        dh, part = grads(h_ref[...], dhn_ref[...], dout_ref[...], gv)
        dh_next, part_next = grads(hn_ref[...], dhnn_ref[...], doutn_ref[...], gv)
        gx_ref[...] = jnp.concatenate([dh[n_meta:], dh_next], axis=0)

        @pl.when(i == 0)
        def _():
            dmeta_ref[...] = dh[:n_meta]
            dg_ref[...] = part

        @pl.when(i > 0)
        def _():
            dg_ref[...] += part

        @pl.when(i == nt - 1)
        def _():
            dg_ref[...] += part_next

    row = pl.BlockSpec((tr, d), lambda i: (i, 0))
    nxt = pl.BlockSpec((n_meta, d), lambda i: ((i + 1) * per, 0))
    vec = pl.BlockSpec((1, d), lambda i: (0, 0))
    return pl.pallas_call(
        body, name=name, grid=(nt,),
        in_specs=[row, row, row, nxt, nxt, nxt, vec],
        out_specs=[pl.BlockSpec((None, tr, d), lambda i: (0, i, 0)), pl.BlockSpec((n_meta, d), lambda i: (0, 0)), vec],
        out_shape=[jax.ShapeDtypeStruct((1, seq, d), F32), jax.ShapeDtypeStruct((n_meta, d), F32),
                   jax.ShapeDtypeStruct((1, d), F32)],
        compiler_params=_params(("arbitrary",)),
    )(h, dhn, dout, h, dhn, dout, g.reshape(1, d))


def _loss_head(h, tgt, g, *, n_meta, n_tok, name):
    tp, d = h.shape
    seq = tgt.shape[0]
    tr = MIX_ROWS
    assert tp % tr == 0 and seq % tr == 0 and tr % n_meta == 0
    per = tr // n_meta

    def body(h_ref, t_ref, tp_ref, g_ref, dh_ref, dhb_ref, dg_ref, loss_ref):
        i = pl.program_id(0)
        hv = h_ref[...]
        rstd = lax.rsqrt(jnp.mean(hv * hv, axis=-1, keepdims=True) + RMS_EPS)
        xhat = hv * rstd
        gv = g_ref[...]
        rows = i * tr + lax.broadcasted_iota(jnp.int32, (tr, 1), 0)
        valid = jnp.logical_and(rows >= n_meta, rows < n_tok)
        target = jnp.concatenate([tp_ref[...], t_ref[:tr - n_meta, :]], axis=0)
        err = jnp.where(valid, xhat * gv - target, 0.0)
        dy = err * (1.0 / d)
        dxhat = dy * gv
        dh = rstd * (dxhat - xhat * jnp.mean(dxhat * xhat, axis=-1, keepdims=True))
        dh_ref[...] = dh
        dhb_ref[...] = dh.astype(BF16)
        dg_part = jnp.sum(dy * xhat, axis=0, keepdims=True)
        per_row = jnp.sum(err * err, axis=-1, keepdims=True) * (1.0 / d)
        loss_part = jnp.broadcast_to(0.5 * jnp.sum(per_row, axis=0, keepdims=True), (SUBLANES, LANES))

        @pl.when(i == 0)
        def _():
            dg_ref[...] = dg_part
            loss_ref[...] = loss_part

        @pl.when(i > 0)
        def _():
            dg_ref[...] += dg_part
            loss_ref[...] += loss_part

    row = pl.BlockSpec((tr, d), lambda i: (i, 0))
    vec = pl.BlockSpec((1, d), lambda i: (0, 0))
    own = pl.BlockSpec((tr, d), lambda i: (jnp.minimum(i, seq // tr - 1), 0))
    before = pl.BlockSpec((n_meta, d), lambda i: (jnp.maximum(i * per - 1, 0), 0))
    return pl.pallas_call(
        body, name=name, grid=(tp // tr,),
        in_specs=[row, own, before, vec],
        out_specs=[row, row, vec, pl.BlockSpec((SUBLANES, LANES), lambda i: (0, 0))],
        out_shape=[jax.ShapeDtypeStruct((tp, d), F32), jax.ShapeDtypeStruct((tp, d), BF16),
                   jax.ShapeDtypeStruct((1, d), F32), jax.ShapeDtypeStruct((SUBLANES, LANES), F32)],
        compiler_params=_params(("arbitrary",)),
    )(h, tgt, tgt, g.reshape(1, d))


def _shift_down(halo, tile, s):
    if s == 0:
        return tile
    ext = jnp.concatenate([halo, tile], axis=0)
    return pltpu.roll(ext, s, 0)[SUBLANES:]


def _shift_up(tile, head, s):
    if s == 0:
        return tile
    ext = jnp.concatenate([tile, head], axis=0)
    n = ext.shape[0]
    return pltpu.roll(ext, n - s, 0)[: tile.shape[0]]


def _scan_rows_fwd(a, b):
    row = lax.broadcasted_iota(jnp.int32, a.shape, 0)
    for s in (1, 2, 4):
        a_sh = pltpu.roll(a, s, 0)
        b_sh = pltpu.roll(b, s, 0)
        m = row >= s
        b = jnp.where(m, a * b_sh + b, b)
        a = jnp.where(m, a * a_sh, a)
    return a, b


def _scan_rows_bwd(c, d):
    row = lax.broadcasted_iota(jnp.int32, c.shape, 0)
    for s in (1, 2, 4):
        c_sh = pltpu.roll(c, SUBLANES - s, 0)
        d_sh = pltpu.roll(d, SUBLANES - s, 0)
        m = row < SUBLANES - s
        d = jnp.where(m, c * d_sh + d, d)
        c = jnp.where(m, c * c_sh, c)
    return c, d


def _gates(ca, wr, wi, br, bi, sp):
    cab = ca.astype(BF16)
    r = _sigmoid(jnp.dot(cab, wr, preferred_element_type=F32) + br)
    ig = _sigmoid(jnp.dot(cab, wi, preferred_element_type=F32) + bi)
    la = -LRU_C * r * sp
    a = jnp.exp(la)
    mult = jnp.sqrt(-jnp.tanh(la) * (a * a + 1.0))
    return r, ig, a, mult


def _mixer_fwd(u, wa, ba, wr_blk, br, wi_blk, bi, lam, wb, *, name):
    tp, din = u.shape
    dl = din // 6
    tt = MIX_ROWS
    cw = GATE_BLOCK
    nch = dl // cw
    assert tp % tt == 0 and dl % cw == 0

    def body(u_ref, wa_ref, ba_ref, wr_ref, br_ref, wi_ref, bi_ref, lam_ref, wb_ref,
             ca_ref, hs_ref, y_ref, xa_tail, v_tail, h_carry, a_s, b_s):
        @pl.when(pl.program_id(0) == 0)
        def _():
            xa_tail[...] = jnp.zeros_like(xa_tail)
            v_tail[...] = jnp.zeros_like(v_tail)
            h_carry[...] = jnp.zeros_like(h_carry)

        for ch in range(nch):
            cs = slice(ch * cw, (ch + 1) * cw)

            def seg(s):
                return slice(s * dl + ch * cw, s * dl + (ch + 1) * cw)

            xa = u_ref[:, seg(0)]
            halo = xa_tail[:, cs]
            ca = ba_ref[:, cs] + wa_ref[3:4, cs] * xa
            for kk in range(3):
                ca = ca + wa_ref[kk:kk + 1, cs] * _shift_down(halo, xa, 3 - kk)
            xa_tail[:, cs] = xa[tt - SUBLANES:]
            ca_ref[:, cs] = ca
            sp = _softplus(-lam_ref[:, cs])
            _, ig, a, mult = _gates(ca, wr_ref[ch], wi_ref[ch], br_ref[:, cs], bi_ref[:, cs], sp)
            a_s[:, cs] = a
            b_s[:, cs] = mult * (ig * ca)

            bv = u_ref[:, seg(2)]
            v = u_ref[:, seg(3)] * u_ref[:, seg(4)]
            gb = u_ref[:, seg(5)]
            vh = v_tail[:, cs]
            cb = wb_ref[2:3, cs] * v
            for kk in range(2):
                cb = cb + wb_ref[kk:kk + 1, cs] * _shift_down(vh, v, 2 - kk)
            v_tail[:, cs] = v[tt - SUBLANES:]
            y_ref[:, dl + ch * cw: dl + (ch + 1) * cw] = (bv * cb * (gb * _sigmoid(gb))).astype(BF16)

        def group(gi, hprev):
            rows = pl.ds(pl.multiple_of(gi * SUBLANES, SUBLANES), SUBLANES)
            a8, b8 = _scan_rows_fwd(a_s[rows, :], b_s[rows, :])
            h8 = b8 + a8 * hprev
            hs_ref[rows, :] = h8
            return jnp.broadcast_to(h8[SUBLANES - 1:SUBLANES, :], h8.shape)

        h_carry[...] = lax.fori_loop(0, tt // SUBLANES, group, h_carry[...], unroll=SCAN_UNROLL)

        for ch in range(nch):
            cs = slice(ch * cw, (ch + 1) * cw)
            ga = u_ref[:, dl + ch * cw: dl + (ch + 1) * cw]
            y_ref[:, cs] = (hs_ref[:, cs] * (ga * _sigmoid(ga))).astype(BF16)

    row = lambda w: pl.BlockSpec((tt, w), lambda i: (i, 0))
    full = lambda shp: pl.BlockSpec(shp, lambda i: tuple(0 for _ in shp))
    return pl.pallas_call(
        body, name=name, grid=(tp // tt,),
        in_specs=[row(din), full((4, dl)), full((1, dl)), full((nch, cw, cw)), full((1, dl)),
                  full((nch, cw, cw)), full((1, dl)), full((1, dl)), full((3, dl))],
        out_specs=[row(dl), row(dl), row(2 * dl)],
        out_shape=[jax.ShapeDtypeStruct((tp, dl), F32), jax.ShapeDtypeStruct((tp, dl), F32),
                   jax.ShapeDtypeStruct((tp, 2 * dl), BF16)],
        scratch_shapes=[pltpu.VMEM((SUBLANES, dl), F32), pltpu.VMEM((SUBLANES, dl), F32),
                        pltpu.VMEM((SUBLANES, dl), F32), pltpu.VMEM((tt, dl), F32), pltpu.VMEM((tt, dl), F32)],
        compiler_params=_params(("arbitrary",)),
    )(u, wa, ba, wr_blk, br, wi_blk, bi, lam, wb)


SG_WA, SG_BA, SG_BR, SG_BI, SG_LAM, SG_WB, SG_ROWS = 0, 4, 5, 6, 7, 8, 16


def _mixer_bwd(u, ca, hs, dy, wa, wr_blk, br, wi_blk, bi, lam, wb, *, name):
    tp, din = u.shape
    dl = din // 6
    tt = MIX_ROWS
    cw = GATE_BLOCK
    nch = dl // cw
    nt = tp // tt
    hb = tt // SUBLANES
    tn_dims = (((0,), (0,)), ((), ()))
    nt_dims = (((1,), (1,)), ((), ()))

    def body(u_ref, uh_ref, ca_ref, hs_ref, hsh_ref, dy_ref, wa_ref, wr_ref, br_ref, wi_ref, bi_ref, lam_ref, wb_ref,
             du_ref, sg_ref, dwr_ref, dwi_ref,
             g_carry, a_head, dca_head, dcb_head, r_s, i_s, a_s, an_s, d_s, g_s):
        i = pl.program_id(0)
        first_tile = i == nt - 1

        @pl.when(i == 0)
        def _():
            for ref in (g_carry, a_head, dca_head, dcb_head, sg_ref, dwr_ref, dwi_ref):
                ref[...] = jnp.zeros_like(ref)

        def halo_of(x):
            return jnp.where(first_tile, 0.0, x)

        for ch in range(nch):
            cs = slice(ch * cw, (ch + 1) * cw)
            cav = ca_ref[:, cs]
            sp = _softplus(-lam_ref[:, cs])
            r, ig, a, _ = _gates(cav, wr_ref[ch], wi_ref[ch], br_ref[:, cs], bi_ref[:, cs], sp)
            r_s[:, cs] = r
            i_s[:, cs] = ig
            a_s[:, cs] = a
            an_s[:, cs] = _shift_up(a, a_head[:, cs], 1)
            a_head[:, cs] = a[:SUBLANES]
            ga = u_ref[:, dl + ch * cw: dl + (ch + 1) * cw]
            d_s[:, cs] = dy_ref[:, cs] * (ga * _sigmoid(ga))

        def group(k, gnext):
            gi = tt // SUBLANES - 1 - k
            rows = pl.ds(pl.multiple_of(gi * SUBLANES, SUBLANES), SUBLANES)
            c8, d8 = _scan_rows_bwd(an_s[rows, :], d_s[rows, :])
            g8 = d8 + c8 * gnext
            g_s[rows, :] = g8
            return jnp.broadcast_to(g8[0:1, :], g8.shape)

        g_carry[...] = lax.fori_loop(0, tt // SUBLANES, group, g_carry[...], unroll=SCAN_UNROLL)

        def acc_row(r0, val):
            sg_ref[r0:r0 + 1, cs_cur[0]] += jnp.sum(val, axis=0, keepdims=True)

        cs_cur = [None]
        for ch in range(nch):
            cs = slice(ch * cw, (ch + 1) * cw)
            cs_cur[0] = cs

            def seg(s):
                return slice(s * dl + ch * cw, s * dl + (ch + 1) * cw)

            cav = ca_ref[:, cs]
            r = r_s[:, cs]
            ig = i_s[:, cs]
            a = a_s[:, cs]
            g = g_s[:, cs]
            hsv = hs_ref[:, cs]
            lamv = lam_ref[:, cs]
            sp = _softplus(-lamv)
            la = -LRU_C * r * sp
            e2 = a * a
            one_m_e2 = -jnp.tanh(la) * (e2 + 1.0)
            mult = jnp.sqrt(one_m_e2)
            hprev = _shift_down(halo_of(hsh_ref[:, cs]), hsv, 1)
            icav = ig * cav
            dla = g * (hprev * a - icav * (e2 * lax.rsqrt(one_m_e2)))
            gm = g * mult
            dzi = gm * icav * (1.0 - ig)
            dca = gm * ig
            dla_r = dla * r
            dzr = dla_r * (1.0 - r) * (-LRU_C * sp)
            sg_ref[SG_LAM:SG_LAM + 1, cs] += jnp.sum(dla_r, axis=0, keepdims=True) * (LRU_C * _sigmoid(-lamv))
            acc_row(SG_BR, dzr)
            acc_row(SG_BI, dzi)
            dzr_b = dzr.astype(BF16)
            dzi_b = dzi.astype(BF16)
            cab = cav.astype(BF16)
            dca = dca + lax.dot_general(dzr_b, wr_ref[ch], nt_dims, preferred_element_type=F32)
            dca = dca + lax.dot_general(dzi_b, wi_ref[ch], nt_dims, preferred_element_type=F32)
            dwr_ref[ch] += lax.dot_general(cab, dzr_b, tn_dims, preferred_element_type=F32)
            dwi_ref[ch] += lax.dot_general(cab, dzi_b, tn_dims, preferred_element_type=F32)
            acc_row(SG_BA, dca)
            xa = u_ref[:, seg(0)]
            head = dca_head[:, cs]
            dxa = wa_ref[3:4, cs] * dca
            acc_row(SG_WA + 3, dca * xa)
            for kk in range(3):
                later = _shift_up(dca, head, 3 - kk)
                acc_row(SG_WA + kk, later * xa)
                dxa = dxa + wa_ref[kk:kk + 1, cs] * later
            dca_head[:, cs] = dca[:SUBLANES]
            ga = u_ref[:, seg(1)]
            sga = _sigmoid(ga)
            dga = dy_ref[:, cs] * hsv * (sga + (ga * sga) * (1.0 - sga))
            du_ref[:, seg(0)] = dxa.astype(BF16)
            du_ref[:, seg(1)] = dga.astype(BF16)

            bv = u_ref[:, seg(2)]
            cv = u_ref[:, seg(3)]
            xb = u_ref[:, seg(4)]
            gb = u_ref[:, seg(5)]
            dyb = dy_ref[:, dl + ch * cw: dl + (ch + 1) * cw]
            v = cv * xb
            vh = halo_of(uh_ref[:, seg(3)] * uh_ref[:, seg(4)])
            v1 = _shift_down(vh, v, 1)
            v2 = _shift_down(vh, v, 2)
            cb = wb_ref[2:3, cs] * v + wb_ref[1:2, cs] * v1 + wb_ref[0:1, cs] * v2
            sgb = _sigmoid(gb)
            sl = gb * sgb
            dyb_b = dyb * bv
            dyb_cb = dyb * cb
            dcb = dyb_b * sl
            du_ref[:, seg(2)] = (dyb_cb * sl).astype(BF16)
            du_ref[:, seg(5)] = (dyb_cb * bv * (sgb + sl * (1.0 - sgb))).astype(BF16)
            bhead = dcb_head[:, cs]
            dv = wb_ref[2:3, cs] * dcb
            acc_row(SG_WB + 2, dcb * v)
            for kk in range(2):
                later = _shift_up(dcb, bhead, 2 - kk)
                acc_row(SG_WB + kk, later * v)
                dv = dv + wb_ref[kk:kk + 1, cs] * later
            dcb_head[:, cs] = dcb[:SUBLANES]
            du_ref[:, seg(3)] = (dv * xb).astype(BF16)
            du_ref[:, seg(4)] = (dv * cv).astype(BF16)

    rev = lambda w: pl.BlockSpec((tt, w), lambda i: (nt - 1 - i, 0))
    halo = lambda w: pl.BlockSpec((SUBLANES, w), lambda i: (jnp.maximum((nt - 1 - i) * hb - 1, 0), 0))
    full = lambda shp: pl.BlockSpec(shp, lambda i: tuple(0 for _ in shp))
    vm = lambda r: pltpu.VMEM((r, dl), F32)
    return pl.pallas_call(
        body, name=name, grid=(nt,),
        in_specs=[rev(din), halo(din), rev(dl), rev(dl), halo(dl), rev(2 * dl), full((4, dl)),
                  full((nch, cw, cw)), full((1, dl)), full((nch, cw, cw)), full((1, dl)), full((1, dl)), full((3, dl))],
        out_specs=[rev(din), full((SG_ROWS, dl)), full((nch, cw, cw)), full((nch, cw, cw))],
        out_shape=[jax.ShapeDtypeStruct((tp, din), BF16), jax.ShapeDtypeStruct((SG_ROWS, dl), F32),
                   jax.ShapeDtypeStruct((nch, cw, cw), F32), jax.ShapeDtypeStruct((nch, cw, cw), F32)],
        scratch_shapes=[vm(SUBLANES), vm(SUBLANES), vm(SUBLANES), vm(SUBLANES),
                        vm(tt), vm(tt), vm(tt), vm(tt), vm(tt), vm(tt)],
        compiler_params=_params(("arbitrary",)),
    )(u, u, ca, hs, hs, dy, wa, wr_blk, br, wi_blk, bi, lam, wb)


def _adamw(w, g, m, v, *, name, landed=None, layer=None, depth=None, into=None, row_off=0):
    r, c = w.shape[-2:]
    rows = g.shape[0]
    tr = _tile(rows, 256, 2 * SUBLANES)
    assert row_off % tr == 0
    boff = row_off // tr
    bc1 = 1.0 - ADAM_B1 ** ADAM_STEP
    bc2 = 1.0 - ADAM_B2 ** ADAM_STEP
    slots = landed is not None

    def body(*refs):
        if into is not None:
            refs = refs[:-8] + refs[-4:]
        if slots:
            w_ref, g_ref, l_ref, m_ref, v_ref, grad_ref, delta_ref, nm_ref, nv_ref = refs
            gv = g_ref[...].astype(F32)
            for s in range(N_DEV - 1):
                gv = gv + l_ref[s].astype(F32)
        else:
            w_ref, g_ref, m_ref, v_ref, grad_ref, delta_ref, nm_ref, nv_ref = refs
            gv = g_ref[...]
        wv = w_ref[...]
        mn = ADAM_B1 * m_ref[...] + (1.0 - ADAM_B1) * gv
        vn = ADAM_B2 * v_ref[...] + (1.0 - ADAM_B2) * (gv * gv)
        m_hat = mn / bc1
        v_hat = vn / bc2
        grad_ref[...] = gv
        delta_ref[...] = -ADAM_LR * (m_hat / (jnp.sqrt(v_hat) + ADAM_EPS) + ADAM_WD * wv)
        nm_ref[...] = mn
        nv_ref[...] = vn

    if depth is None:
        blk = pl.BlockSpec((tr, c), lambda i: (i + boff, 0))
    else:
        blk = pl.BlockSpec((None, tr, c), lambda i: (layer, i + boff, 0))
    g_blk = pl.BlockSpec((tr, c), lambda i: (i, 0))
    l_spec = [pl.BlockSpec((N_DEV - 1, tr, c), lambda i: (0, i, 0))] if slots else []
    args = (w, g, landed, m, v) if slots else (w, g, m, v)
    in_specs = [blk, g_blk] + l_spec + [blk, blk]
    if depth is None:
        shp = jax.ShapeDtypeStruct((r, c), F32)
        out_blk = blk
    else:
        shp = jax.ShapeDtypeStruct((depth, r, c), F32)
        out_blk = pl.BlockSpec((None, tr, c), lambda i: (layer, i + boff, 0))
    aliases = {}
    if into is not None:
        aliases = {len(args) + j: j for j in range(4)}
        in_specs = in_specs + [ANY] * 4
        args = args + tuple(into)
    return pl.pallas_call(
        body, name=name, grid=(rows // tr,),
        in_specs=in_specs, out_specs=[out_blk] * 4,
        out_shape=[shp] * 4, input_output_aliases=aliases,
        compiler_params=_params(("parallel",)),
    )(*args)


def _slot_sum(g, *, name):
    _, r, c = g.shape
    tr = _tile(r, 512, SUBLANES)

    def body(g_ref, o_ref):
        gv = g_ref[0]
        for s in range(1, N_DEV):
            gv = gv + g_ref[s]
        o_ref[...] = gv

    return pl.pallas_call(
        body, name=name, grid=(r // tr,),
        in_specs=[pl.BlockSpec((N_DEV, tr, c), lambda i: (0, i, 0))],
        out_specs=pl.BlockSpec((tr, c), lambda i: (i, 0)),
        out_shape=jax.ShapeDtypeStruct((r, c), F32),
        compiler_params=_params(("parallel",)),
    )(g)


def _mesh_pos():
    x, y, c = lax.axis_index("x"), lax.axis_index("y"), lax.axis_index("c")
    return x, y, c, 4 * x + 2 * y + c


ANY = pl.BlockSpec(memory_space=pl.ANY)


def _all_gather(srcs, out_shapes, views, *, name, place=()):
    n = len(srcs)
    npl = len(place)

    def body(*refs):
        src = refs[:n]
        psrc = refs[n:n + npl]
        dst = refs[n + npl:2 * n + npl]
        pdst = refs[2 * n + npl:2 * (n + npl)]
        send_sems, recv_sems, local_sems = refs[2 * (n + npl):]
        x, y, c, me = _mesh_pos()
        sibling = (x, y, 1 - c)
        chips = [(1 - x, y), (x, 1 - y), (1 - x, 1 - y)]

        def dev(px, py, pc):
            return 4 * px + 2 * py + pc

        def copy(a, k, block, to, from_src=False):
            win = views[a](dst[a], dev(*block))
            return pltpu.make_async_remote_copy(
                src_ref=src[a] if from_src else win, dst_ref=win,
                send_sem=send_sems.at[a * 7 + k], recv_sem=recv_sems.at[a * 7 + k],
                device_id=to, device_id_type=MESH)

        mine = [pltpu.make_async_copy(src[a], views[a](dst[a], me), local_sems.at[a]) for a in range(n)]
        mine += [pltpu.make_async_copy(psrc[j], place[j][2](pdst[j], me), local_sems.at[n + j]) for j in range(npl)]
        started = []
        for a in range(n):
            mine[a].start()
            first = [copy(a, 0, (x, y, c), sibling, True)]
            first += [copy(a, 1 + j, (x, y, c), (*chip, c), True) for j, chip in enumerate(chips)]
            for cp in first:
                cp.start()
            started += first
        for cp in mine[n:]:
            cp.start()
        for a in range(n):
            for j, chip in enumerate(chips):
                copy(a, 1 + j, (*chip, c), (x, y, c)).wait_recv()
                fwd = copy(a, 4 + j, (*chip, c), sibling)
                fwd.start()
                started.append(fwd)
        for a in range(n):
            copy(a, 0, (x, y, 1 - c), (x, y, c)).wait_recv()
            for j, chip in enumerate(chips):
                copy(a, 4 + j, (*chip, 1 - c), (x, y, c)).wait_recv()
        for cp in started:
            cp.wait_send()
        for cp in mine:
            cp.wait()

    return pl.pallas_call(
        body, name=name,
        in_specs=[ANY] * (n + npl), out_specs=[ANY] * (n + npl),
        out_shape=[jax.ShapeDtypeStruct(s, x.dtype) for s, x in zip(out_shapes, srcs)]
        + [jax.ShapeDtypeStruct(shape, arr.dtype) for arr, shape, _ in place],
        scratch_shapes=[pltpu.SemaphoreType.DMA((7 * n,)), pltpu.SemaphoreType.DMA((7 * n,)),
                        pltpu.SemaphoreType.DMA((n + npl,))],
    )(*srcs, *[arr for arr, _, _ in place])


HBM = pl.BlockSpec(memory_space=pltpu.HBM)
SEM = pl.BlockSpec(memory_space=pltpu.SEMAPHORE)
EFFECT = pltpu.SideEffectType.DATAFLOW_SIDE_EFFECTING


def _peer_of(x, y, c, k):
    return (1 - x if k & 4 else x, 1 - y if k & 2 else y, 1 - c if k & 1 else c)


def _peer_copies(n, wins, src, land, send_sems, recv_sems):
    x, y, c, me = _mesh_pos()
    out = []
    for a in range(n):
        for k in range(1, N_DEV):
            px, py, pc = _peer_of(x, y, c, k)
            s_win, d_win = wins[a](src[a], land[a], me, 4 * px + 2 * py + pc, k)
            out.append(pltpu.make_async_remote_copy(
                src_ref=s_win, dst_ref=d_win,
                send_sem=send_sems.at[a * 7 + k - 1], recv_sem=recv_sems.at[a * 7 + k - 1],
                device_id=(px, py, pc), device_id_type=MESH))
    return out


def _push_start(srcs, lands, wins, *, name):
    n = len(srcs)

    def body(*refs):
        src = refs[:n]
        land = refs[n:2 * n]
        send_sems, recv_sems = refs[2 * n], refs[2 * n + 1]
        token = refs[-1]
        for cp in _peer_copies(n, wins, src, land, send_sems, recv_sems):
            cp.start()
        token[...] = jnp.zeros_like(token)

    bufs = (*srcs, *lands)
    return pl.pallas_call(
        body, name=name,
        out_shape=(pltpu.SemaphoreType.DMA((7 * n,)), pltpu.SemaphoreType.DMA((7 * n,)),
                   *[pltpu.HBM(v.shape, v.dtype) for v in bufs], jax.ShapeDtypeStruct((SUBLANES, LANES), F32)),
        in_specs=[HBM] * (2 * n),
        out_specs=(SEM, SEM, *[HBM] * (2 * n), pl.BlockSpec(memory_space=pltpu.VMEM)),
        input_output_aliases={i: 2 + i for i in range(2 * n)},
        compiler_params=pltpu.CompilerParams(has_side_effects=EFFECT),
    )(*[pltpu.with_memory_space_constraint(v, pltpu.HBM) for v in bufs])


def _push_wait(handle, wins, after, *, name):
    send_sems, recv_sems, *bufs, _ = handle
    n = len(bufs) // 2

    def body(*refs):
        src = refs[:n]
        land = refs[n:2 * n]
        for cp in _peer_copies(n, wins, src, land, refs[2 * n], refs[2 * n + 1]):
            cp.wait_send()
            cp.wait_recv()

    outs = pl.pallas_call(
        body, name=name,
        out_shape=tuple(pltpu.HBM(v.shape, v.dtype) for v in bufs),
        in_specs=[HBM] * (2 * n) + [SEM, SEM, ANY],
        out_specs=tuple([HBM] * (2 * n)),
        input_output_aliases={i: i for i in range(2 * n)},
        compiler_params=pltpu.CompilerParams(has_side_effects=EFFECT),
    )(*bufs, send_sems, recv_sems, after)
    return outs[:n], outs[n:]


def _gather_lead(src, land, me, peer, k):
    return src, land.at[me]


def _gather_cols(width):
    def win(src, land, me, peer, k):
        return src, land.at[:, pl.ds(me * width, width)]
    return win


def _scatter_lead(src, land, me, peer, k):
    return src.at[peer], land.at[k - 1]


def _scatter_cols(width):
    def win(src, land, me, peer, k):
        return src.at[:, pl.ds(peer * width, width)], land.at[k - 1]
    return win


def _place_block(own, *, cols, name):
    rows, width = own.shape
    tr = _tile(rows, 512, 2 * SUBLANES)
    _, _, _, me = _mesh_pos()

    def body(me_ref, x_ref, o_ref):
        o_ref[...] = x_ref[...]

    if cols:
        out_spec = pl.BlockSpec((tr, width), lambda i, me_ref: (i, me_ref[0]))
        shape = (rows, N_DEV * width)
    else:
        out_spec = pl.BlockSpec((None, tr, width), lambda i, me_ref: (me_ref[0], i, 0))
        shape = (N_DEV, rows, width)
    return pl.pallas_call(
        body, name=name,
        grid_spec=pltpu.PrefetchScalarGridSpec(
            num_scalar_prefetch=1, grid=(rows // tr,),
            in_specs=[pl.BlockSpec((tr, width), lambda i, me_ref: (i, 0))], out_specs=out_spec),
        out_shape=jax.ShapeDtypeStruct(shape, own.dtype),
        compiler_params=_params(("arbitrary",)),
    )(me.astype(jnp.int32).reshape(1), own)


def _dep(x, token):
    return x + token[0, 0].astype(x.dtype)


def _lead(ref, d):
    return ref.at[d]


def _col_window(width):
    def view(ref, d):
        return ref.at[:, pl.ds(d * width, width)]
    return view


def _pack(arrs):
    flat = jnp.concatenate([a.reshape(-1).astype(F32) for a in arrs])
    n = flat.shape[0]
    rows = -(-n // (SUBLANES * LANES)) * SUBLANES
    return jnp.pad(flat, (0, rows * LANES - n)).reshape(rows, LANES)


def _unpack(buf, shapes):
    flat = buf.reshape(-1)
    out, off = [], 0
    for s in shapes:
        n = 1
        for q in s:
            n *= q
        out.append(flat[off:off + n].reshape(s))
        off += n
    return out


def _blockdiag(w, cw):
    h, hd, _ = w.shape
    per = cw // hd
    wg = w.reshape(h // per, per, hd, hd)
    eye = jnp.eye(per, dtype=w.dtype)
    blk = jnp.einsum("gpij,pq->gpiqj", wg, eye)
    return blk.reshape(h // per, cw, cw).astype(BF16)


def _blockdiag_extract(g, hd):
    n, cw, _ = g.shape
    per = cw // hd
    g5 = g.reshape(n, per, hd, per, hd)
    idx = jnp.arange(per)
    return g5[:, idx, :, idx, :].transpose(1, 0, 2, 3).reshape(n * per, hd, hd)


def kernel(x, meta, norm_g, w_in, conv_a_w, conv_a_b, lru_wr, lru_br, lru_wi, lru_bi, lru_lambda, conv_b_w, w_out, final_g, loss_target, m_meta, m_norm_g, m_w_in, m_conv_a_w, m_conv_a_b, m_lru_wr, m_lru_br, m_lru_wi, m_lru_bi, m_lru_lambda, m_conv_b_w, m_w_out, m_final_g, v_meta, v_norm_g, v_w_in, v_conv_a_w, v_conv_a_b, v_lru_wr, v_lru_br, v_lru_wi, v_lru_bi, v_lru_lambda, v_conv_b_w, v_w_out, v_final_g):
    _, seq, d = x.shape
    n_meta = meta.shape[0]
    depth = w_in.shape[0]
    din = w_in.shape[2] * N_DEV
    dl = din // 6
    dmix = 2 * dl
    wcol = w_in.shape[2]
    wrow = w_out.shape[1]
    mcol = meta.shape[1]
    ccol = conv_a_w.shape[2]
    heads, hd = lru_wr.shape[1], lru_wr.shape[2]
    n_tok = n_meta + seq
    tp = -(-n_tok // TOKEN_TILE) * TOKEN_TILE
    me = 4 * lax.axis_index("x") + 2 * lax.axis_index("y") + lax.axis_index("c")

    bf = lambda a: a.astype(BF16)
    small_mine = _pack([meta, conv_a_w, conv_b_w])
    first = _all_gather([bf(w_in[0]), small_mine], [(d, din), (N_DEV,) + small_mine.shape],
                        [_col_window(wcol), _lead], name="gather_first")
    parts = [_unpack(first[1][s], [meta.shape, conv_a_w.shape, conv_b_w.shape]) for s in range(N_DEV)]
    meta_full = jnp.concatenate([p[0] for p in parts], axis=1)
    wa_full = jnp.concatenate([p[1] for p in parts], axis=2)
    wb_full = jnp.concatenate([p[2] for p in parts], axis=2)
    w_in_full = [None] * depth
    w_out_full = [None] * depth

    push_out = [None] * depth
    push_in = [None] * depth
    w_in_full[0], src = lax.optimization_barrier((first[0], bf(w_out[0])))
    push_out[0] = _push_start([src], [_place_block(src, cols=False, name="place_wout_0")], [_gather_lead],
                              name="gather_wout_0_start")
    token = push_out[0][-1]
    for l in range(1, depth):
        src = bf(_dep(w_in[l], token))
        push_in[l] = _push_start([src], [_place_block(src, cols=True, name=f"place_win_{l}")], [_gather_cols(wcol)],
                                 name=f"gather_win_{l}_start")
        src = bf(_dep(w_out[l], push_in[l][-1]))
        push_out[l] = _push_start([src], [_place_block(src, cols=False, name=f"place_wout_{l}")], [_gather_lead],
                                  name=f"gather_wout_{l}_start")
        token = push_out[l][-1]

    wr_blk = [_blockdiag(lru_wr[l], GATE_BLOCK) for l in range(depth)]
    wi_blk = [_blockdiag(lru_wi[l], GATE_BLOCK) for l in range(depth)]
    vec = lambda a: a.reshape(1, dl)

    h = jnp.concatenate([meta_full, x[0], jnp.zeros((tp - n_tok, d), F32)], axis=0)
    tm = _tile(tp, 1408)
    saved = []
    for l in range(depth):
        hn = _rms_fwd(h, _dep(norm_g[l], token) if l == 0 else norm_g[l], name=f"rms_fwd_{l}")
        if l > 0:
            _, landed = _push_wait(push_in[l], [_gather_cols(wcol)], hn, name=f"gather_win_{l}_wait")
            w_in_full[l] = landed[0]
        u = _matmul(hn, w_in_full[l], tm=tm, tn=_tile(din, 768), tk=d, name=f"mm_u_{l}")
        ca, hs, y = _mixer_fwd(u, wa_full[l], vec(conv_a_b[l]), wr_blk[l], vec(lru_br[l]), wi_blk[l], vec(lru_bi[l]),
                               vec(lru_lambda[l]), wb_full[l], name=f"mixer_fwd_{l}")
        _, landed = _push_wait(push_out[l], [_gather_lead], y, name=f"gather_wout_{l}_wait")
        w_out_full[l] = landed[0].reshape(dmix, d)
        h_next = _matmul(y, w_out_full[l], tm=tm, tn=_tile(d, 512), tk=dmix, add=h, name=f"mm_out_{l}")
        saved.append((h, hn, u, ca, hs, y))
        h = h_next

    dh, dhb, dg_final, loss_part = _loss_head(h, loss_target[0], final_g, n_meta=n_meta, n_tok=n_tok,
                                              name="loss_head")
    loss = lax.psum(loss_part[0, 0], ("x", "y", "c"))

    small_grads = [None] * depth
    sent_out = [None] * depth
    sent_in = [None] * depth
    scatter_in = [_scatter_cols(wcol)]
    token = None
    dg_norms = []
    for l in reversed(range(depth)):
        h_in, hn, u, ca, hs, y = saved[l]
        dy = _matmul(dhb, w_out_full[l], tb=True, tm=tm, tn=_tile(dmix, 512), tk=d, dep=token, name=f"mm_dy_{l}")
        dw_out = _matmul(y, dhb, ta=True, tm=_tile(dmix, 512), tn=_tile(d, 1024), tk=tp, out_dtype=BF16,
                         name=f"mm_dwout_{l}")
        sent_out[l] = _push_start([dw_out.reshape(N_DEV, wrow, d)], [lax.empty((N_DEV - 1, wrow, d), BF16)],
                                  [_scatter_lead], name=f"scatter_wout_{l}_start")
        du, sg, dwr, dwi = _mixer_bwd(u, ca, hs, dy, wa_full[l], wr_blk[l], vec(lru_br[l]), wi_blk[l], vec(lru_bi[l]),
                                      vec(lru_lambda[l]), _dep(wb_full[l], sent_out[l][-1]), name=f"mixer_bwd_{l}")
        small_grads[l] = (sg, dwr, dwi)
        if l == 0:
            early = _pack([
                jnp.stack([small_grads[j][0][SG_BA] for j in range(depth)]),
                jnp.stack([_blockdiag_extract(small_grads[j][1], hd) for j in range(depth)]),
                jnp.stack([small_grads[j][0][SG_BR] for j in range(depth)]),
                jnp.stack([_blockdiag_extract(small_grads[j][2], hd) for j in range(depth)]),
                jnp.stack([small_grads[j][0][SG_BI] for j in range(depth)]),
                jnp.stack([small_grads[j][0][SG_LAM] for j in range(depth)]),
                jnp.stack([small_grads[j][0][SG_WA:SG_WA + 4] for j in range(depth)]),
                jnp.stack([small_grads[j][0][SG_WB:SG_WB + 3] for j in range(depth)]),
                dg_final[0], *dg_norms])
            early_land = lax.dynamic_update_slice(lax.empty((N_DEV,) + early.shape, F32), early[None], (me, 0, 0))
            sent_early = _push_start([early], [early_land], [_gather_lead], name="gather_early_grads_start")
        parts = 2 if l == 0 else 1
        token = sent_early[-1] if l == 0 else None
        sent_in[l] = []
        for p in range(parts):
            dw_in = _matmul(hn, du, ta=True, tm=_tile(d // parts, 512), tn=_tile(din, 768), tk=tp, out_dtype=BF16,
                            dep=token, m_part=(p, parts), name=f"mm_dwin_{l}_{p}")
            sent_in[l].append(_push_start([dw_in], [lax.empty((N_DEV - 1, d // parts, wcol), BF16)], scatter_in,
                                          name=f"scatter_win_{l}_{p}_start"))
            token = sent_in[l][-1][-1]
        dhn = _matmul(du, w_in_full[l], tb=True, tm=_tile(tp, 704, 2 * SUBLANES), tn=_tile(d, 512), tk=din, dep=token,
                      name=f"mm_dhn_{l}")
        if l > 0:
            dh, dhb, dg_norm = _rms_bwd(h_in, dhn, dh, norm_g[l], name=f"rms_bwd_{l}")
            dg_norms.append(dg_norm[0])
        else:
            grad_x, d_meta, dg_norm = _rms_bwd_first(h_in, dhn, dh, norm_g[l], n_meta=n_meta, seq=seq,
                                                     name=f"rms_bwd_{l}")

    late = _pack([dg_norm[0], d_meta])
    late_all = _all_gather([late], [(N_DEV,) + late.shape], [_lead], name="gather_late_grads")[0]
    late_sum = _unpack(_slot_sum(late_all, name="sum_late_grads"), [(d,), (n_meta, d)])
    _, early_all = _push_wait(sent_early, [_gather_lead], late_sum[0], name="gather_early_grads_wait")
    early_shapes = [conv_a_b.shape, lru_wr.shape, lru_br.shape, lru_wi.shape, lru_bi.shape, lru_lambda.shape,
                    (depth, 4, dl), (depth, 3, dl), final_g.shape] + [(d,)] * (depth - 1)
    e = _unpack(_slot_sum(early_all[0], name="sum_early_grads"), early_shapes)
    g_norm = jnp.stack([late_sum[0]] + e[9:][::-1])
    g_meta = lax.dynamic_slice_in_dim(late_sum[1], me * mcol, mcol, axis=1)
    g_wa = lax.dynamic_slice_in_dim(e[6], me * ccol, ccol, axis=2)
    g_wb = lax.dynamic_slice_in_dim(e[7], me * ccol, ccol, axis=2)

    small_w = [norm_g, conv_a_b, lru_wr, lru_br, lru_wi, lru_bi, lru_lambda, final_g, meta, conv_a_w, conv_b_w]
    small_m = [m_norm_g, m_conv_a_b, m_lru_wr, m_lru_br, m_lru_wi, m_lru_bi, m_lru_lambda, m_final_g, m_meta,
               m_conv_a_w, m_conv_b_w]
    small_v = [v_norm_g, v_conv_a_b, v_lru_wr, v_lru_br, v_lru_wi, v_lru_bi, v_lru_lambda, v_final_g, v_meta,
               v_conv_a_w, v_conv_b_w]
    small_g = [g_norm, e[0], e[1], e[2], e[3], e[4], e[5], e[8], g_meta, g_wa, g_wb]
    small_out = _adamw(_pack(small_w), _pack(small_g), _pack(small_m), _pack(small_v), name="adamw_small")
    small_shapes = [a.shape for a in small_w]
    s_grad, s_delta, s_m, s_v = [_unpack(o, small_shapes) for o in small_out]

    win_out = None
    wout_out = None
    after = small_out[0]
    for l in reversed(range(depth)):
        src, landed = _push_wait(sent_out[l], [_scatter_lead], after, name=f"scatter_wout_{l}_wait")
        own = lax.dynamic_index_in_dim(src[0], me, 0, keepdims=False)
        wout_out = _adamw(w_out, own, m_w_out, v_w_out, landed=landed[0], layer=l, depth=depth,
                          into=wout_out, name=f"adamw_w_out_{l}")
        after = wout_out[0]
        for p, sent in enumerate(sent_in[l]):
            src, landed = _push_wait(sent, scatter_in, after, name=f"scatter_win_{l}_{p}_wait")
            own = lax.dynamic_slice_in_dim(src[0], me * wcol, wcol, axis=1)
            win_out = _adamw(w_in, own, m_w_in, v_w_in, landed=landed[0], layer=l, depth=depth,
                             into=win_out, row_off=p * own.shape[0], name=f"adamw_w_in_{l}_{p}")
            after = win_out[0]

    names = ["norm_g", "conv_a_b", "lru_wr", "lru_br", "lru_wi", "lru_bi", "lru_lambda", "final_g", "meta",
             "conv_a_w", "conv_b_w"]
    order = ["meta", "norm_g", "w_in", "conv_a_w", "conv_a_b", "lru_wr", "lru_br", "lru_wi", "lru_bi", "lru_lambda",
             "conv_b_w", "w_out", "final_g"]

    def family(idx, small):
        table = {nm: small[i] for i, nm in enumerate(names)}
        table["w_in"] = win_out[idx]
        table["w_out"] = wout_out[idx]
        return [table[nm] for nm in order]

    return (loss, grad_x, *family(0, s_grad), *family(1, s_delta), *family(2, s_m), *family(3, s_v))
```

```python
import functools

import jax
import jax.numpy as jnp
from jax import lax
from jax.experimental import pallas as pl
from jax.experimental.pallas import tpu as pltpu

F32 = jnp.float32
BF16 = jnp.bfloat16
MESH = pl.DeviceIdType.MESH

N_DEV = 8
RMS_EPS = 1e-6
LRU_C = 8.0
ADAM_LR = 0.001
ADAM_B1 = 0.9
ADAM_B2 = 0.999
ADAM_EPS = 1e-08
ADAM_WD = 0.01
ADAM_STEP = 10

V7X_VMEM_LIMIT = 52 * 1024 * 1024
LANES = 128
SUBLANES = 8
TOKEN_TILE = 384
MIX_ROWS = 128
GATE_BLOCK = 256


def _params(sem):
    return pltpu.CompilerParams(dimension_semantics=sem, vmem_limit_bytes=V7X_VMEM_LIMIT)


def _tile(n, target, align=LANES):
    best = None
    for t in range(align, min(n, target) + 1, align):
        if n % t == 0:
            best = t
    return n if best is None else best


def _sigmoid(z):
    return 0.5 * jnp.tanh(0.5 * z) + 0.5


def _softplus(z):
    e = jnp.exp(-jnp.abs(z))
    u = 1.0 + e
    l1p = jnp.where(u == 1.0, e, jnp.log(u) * e / jnp.where(u == 1.0, 1.0, u - 1.0))
    return jnp.maximum(z, 0.0) + l1p


def _matmul(a, b, *, ta=False, tb=False, tm, tn, tk, out_dtype=F32, add=None, dep=None, m_part=None, name):
    m, k = (a.shape[1], a.shape[0]) if ta else a.shape
    m_off = 0
    if m_part is not None:
        assert add is None and m % (m_part[1] * tm) == 0
        m //= m_part[1]
        m_off = m_part[0] * (m // tm)
    n, kb = b.shape if tb else b.shape[::-1]
    assert kb == k
    assert m % tm == 0 and n % tn == 0 and k % tk == 0, (m, n, k, tm, tn, tk)
    nk = k // tk
    a_spec = pl.BlockSpec((tk, tm), lambda i, j, q: (q, i + m_off)) if ta \
        else pl.BlockSpec((tm, tk), lambda i, j, q: (i + m_off, q))
    b_spec = pl.BlockSpec((tn, tk), lambda i, j, q: (j, q)) if tb else pl.BlockSpec((tk, tn), lambda i, j, q: (q, j))
    o_spec = pl.BlockSpec((tm, tn), lambda i, j, q: (i, j))
    o_shape = (m, n)
    dims = (((0 if ta else 1,), (1 if tb else 0,)), ((), ()))
    has_add = add is not None
    has_dep = dep is not None

    def body(*refs):
        if has_dep:
            refs = refs[:-3] + refs[-2:]
        if has_add:
            a_ref, b_ref, add_ref, o_ref, acc_ref = refs
        else:
            a_ref, b_ref, o_ref, acc_ref = refs
        q = pl.program_id(2)
        part = lax.dot_general(a_ref[...], b_ref[...], dims, preferred_element_type=F32)

        def finish(acc):
            if has_add:
                acc = acc + add_ref[...]
            o_ref[...] = acc.astype(out_dtype)

        if nk == 1:
            finish(part)
        else:
            @pl.when(q == 0)
            def _():
                acc_ref[...] = part

            @pl.when(jnp.logical_and(q > 0, q < nk - 1))
            def _():
                acc_ref[...] += part

            @pl.when(q == nk - 1)
            def _():
                finish(acc_ref[...] + part)

    in_specs = [a_spec, b_spec] + ([o_spec] if has_add else [])
    args = (a, b) + ((add,) if has_add else ())
    if has_dep:
        in_specs.append(pl.BlockSpec((SUBLANES, LANES), lambda i, j, q: (0, 0)))
        args += (dep,)
    acc_shape = (tm, tn) if nk > 1 else (SUBLANES, LANES)
    return pl.pallas_call(
        body, name=name,
        grid=(m // tm, n // tn, nk),
        in_specs=in_specs, out_specs=o_spec,
        out_shape=jax.ShapeDtypeStruct(o_shape, out_dtype),
        scratch_shapes=[pltpu.VMEM(acc_shape, F32)],
        compiler_params=_params(("parallel", "parallel", "arbitrary")),
    )(*args)


def _rms_fwd(h, g, *, name):
    tp, d = h.shape
    tr = _tile(tp, 512, SUBLANES)

    def body(h_ref, g_ref, o_ref):
        hv = h_ref[...]
        rstd = lax.rsqrt(jnp.mean(hv * hv, axis=-1, keepdims=True) + RMS_EPS)
        o_ref[...] = (hv * rstd * g_ref[...]).astype(BF16)

    return pl.pallas_call(
        body, name=name, grid=(tp // tr,),
        in_specs=[pl.BlockSpec((tr, d), lambda i: (i, 0)), pl.BlockSpec((1, d), lambda i: (0, 0))],
        out_specs=pl.BlockSpec((tr, d), lambda i: (i, 0)),
        out_shape=jax.ShapeDtypeStruct((tp, d), BF16),
        compiler_params=_params(("parallel",)),
    )(h, g.reshape(1, d))


def _rms_bwd(h, dhn, dout, g, *, name):
    tp, d = h.shape
    tr = _tile(tp, 384, SUBLANES)

    def body(h_ref, dhn_ref, dout_ref, g_ref, dh_ref, dhb_ref, dg_ref):
        hv = h_ref[...]
        rstd = lax.rsqrt(jnp.mean(hv * hv, axis=-1, keepdims=True) + RMS_EPS)
        xhat = hv * rstd
        dn = dhn_ref[...]
        dxhat = dn * g_ref[...]
        dh = dout_ref[...] + rstd * (dxhat - xhat * jnp.mean(dxhat * xhat, axis=-1, keepdims=True))
        dh_ref[...] = dh
        dhb_ref[...] = dh.astype(BF16)
        part = jnp.sum(dn * xhat, axis=0, keepdims=True)

        @pl.when(pl.program_id(0) == 0)
        def _():
            dg_ref[...] = part

        @pl.when(pl.program_id(0) > 0)
        def _():
            dg_ref[...] += part

    row = pl.BlockSpec((tr, d), lambda i: (i, 0))
    vec = pl.BlockSpec((1, d), lambda i: (0, 0))
    return pl.pallas_call(
        body, name=name, grid=(tp // tr,),
        in_specs=[row, row, row, vec],
        out_specs=[row, row, vec],
        out_shape=[jax.ShapeDtypeStruct((tp, d), F32), jax.ShapeDtypeStruct((tp, d), BF16),
                   jax.ShapeDtypeStruct((1, d), F32)],
        compiler_params=_params(("arbitrary",)),
    )(h, dhn, dout, g.reshape(1, d))


def _rms_bwd_first(h, dhn, dout, g, *, n_meta, seq, name):
    tp, d = h.shape
    tr = MIX_ROWS
    assert seq % tr == 0 and tr % n_meta == 0 and tp >= seq + n_meta
    nt = seq // tr
    per = tr // n_meta

    def grads(hv, dn, do, gv):
        rstd = lax.rsqrt(jnp.mean(hv * hv, axis=-1, keepdims=True) + RMS_EPS)
        xhat = hv * rstd
        dxhat = dn * gv
        dh = do + rstd * (dxhat - xhat * jnp.mean(dxhat * xhat, axis=-1, keepdims=True))
        return dh, jnp.sum(dn * xhat, axis=0, keepdims=True)

    def body(h_ref, dhn_ref, dout_ref, hn_ref, dhnn_ref, doutn_ref, g_ref, gx_ref, dmeta_ref, dg_ref):
        i = pl.program_id(0)
        gv = g_ref[...]
        dh, part = grads(h_ref[...], dhn_ref[...], dout_ref[...], gv)
        dh_next, part_next = grads(hn_ref[...], dhnn_ref[...], doutn_ref[...], gv)
        gx_ref[...] = jnp.concatenate([dh[n_meta:], dh_next], axis=0)

        @pl.when(i == 0)
        def _():
            dmeta_ref[...] = dh[:n_meta]
            dg_ref[...] = part

        @pl.when(i > 0)
        def _():
            dg_ref[...] += part

        @pl.when(i == nt - 1)
        def _():
            dg_ref[...] += part_next

    row = pl.BlockSpec((tr, d), lambda i: (i, 0))
    nxt = pl.BlockSpec((n_meta, d), lambda i: ((i + 1) * per, 0))
    vec = pl.BlockSpec((1, d), lambda i: (0, 0))
    return pl.pallas_call(
        body, name=name, grid=(nt,),
        in_specs=[row, row, row, nxt, nxt, nxt, vec],
        out_specs=[pl.BlockSpec((None, tr, d), lambda i: (0, i, 0)), pl.BlockSpec((n_meta, d), lambda i: (0, 0)), vec],
        out_shape=[jax.ShapeDtypeStruct((1, seq, d), F32), jax.ShapeDtypeStruct((n_meta, d), F32),
                   jax.ShapeDtypeStruct((1, d), F32)],
        compiler_params=_params(("arbitrary",)),
    )(h, dhn, dout, h, dhn, dout, g.reshape(1, d))


def _loss_head(h, tgt, g, *, n_meta, n_tok, name):
    tp, d = h.shape
    seq = tgt.shape[0]
    tr = MIX_ROWS
    assert tp % tr == 0 and seq % tr == 0 and tr % n_meta == 0
    per = tr // n_meta

    def body(h_ref, t_ref, tp_ref, g_ref, dh_ref, dhb_ref, dg_ref, loss_ref):
        i = pl.program_id(0)
        hv = h_ref[...]
        rstd = lax.rsqrt(jnp.mean(hv * hv, axis=-1, keepdims=True) + RMS_EPS)
        xhat = hv * rstd
        gv = g_ref[...]
        rows = i * tr + lax.broadcasted_iota(jnp.int32, (tr, 1), 0)
        valid = jnp.logical_and(rows >= n_meta, rows < n_tok)
        target = jnp.concatenate([tp_ref[...], t_ref[:tr - n_meta, :]], axis=0)
        err = jnp.where(valid, xhat * gv - target, 0.0)
        dy = err * (1.0 / d)
        dxhat = dy * gv
        dh = rstd * (dxhat - xhat * jnp.mean(dxhat * xhat, axis=-1, keepdims=True))
        dh_ref[...] = dh
        dhb_ref[...] = dh.astype(BF16)
        dg_part = jnp.sum(dy * xhat, axis=0, keepdims=True)
        per_row = jnp.sum(err * err, axis=-1, keepdims=True) * (1.0 / d)
        loss_part = jnp.broadcast_to(0.5 * jnp.sum(per_row, axis=0, keepdims=True), (SUBLANES, LANES))

        @pl.when(i == 0)
        def _():
            dg_ref[...] = dg_part
            loss_ref[...] = loss_part

        @pl.when(i > 0)
        def _():
            dg_ref[...] += dg_part
            loss_ref[...] += loss_part

    row = pl.BlockSpec((tr, d), lambda i: (i, 0))
    vec = pl.BlockSpec((1, d), lambda i: (0, 0))
    own = pl.BlockSpec((tr, d), lambda i: (jnp.minimum(i, seq // tr - 1), 0))
    before = pl.BlockSpec((n_meta, d), lambda i: (jnp.maximum(i * per - 1, 0), 0))
    return pl.pallas_call(
        body, name=name, grid=(tp // tr,),
        in_specs=[row, own, before, vec],
        out_specs=[row, row, vec, pl.BlockSpec((SUBLANES, LANES), lambda i: (0, 0))],
        out_shape=[jax.ShapeDtypeStruct((tp, d), F32), jax.ShapeDtypeStruct((tp, d), BF16),
                   jax.ShapeDtypeStruct((1, d), F32), jax.ShapeDtypeStruct((SUBLANES, LANES), F32)],
        compiler_params=_params(("arbitrary",)),
    )(h, tgt, tgt, g.reshape(1, d))


def _shift_down(halo, tile, s):
    if s == 0:
        return tile
    ext = jnp.concatenate([halo, tile], axis=0)
    return pltpu.roll(ext, s, 0)[SUBLANES:]


def _shift_up(tile, head, s):
    if s == 0:
        return tile
    ext = jnp.concatenate([tile, head], axis=0)
    n = ext.shape[0]
    return pltpu.roll(ext, n - s, 0)[: tile.shape[0]]


def _to_lane_blocks(ref, cols, val):
    for j in range(cols.start // LANES, cols.stop // LANES):
        ref[j] = val[:, j * LANES - cols.start:(j + 1) * LANES - cols.start]


def _from_lane_blocks(ref, cols):
    return jnp.concatenate([ref[j] for j in range(cols.start // LANES, cols.stop // LANES)], axis=1)


def _scan_tile(a_ref, b_ref, out_ref, carry, j, *, reverse):
    ng = a_ref.shape[1] // SUBLANES
    order = list(range(SUBLANES))[::-1] if reverse else list(range(SUBLANES))

    def rows(r):
        return pl.ds(r, ng, stride=SUBLANES)

    prod, loc = {}, {}
    prev = None
    for r in order:
        ar = a_ref[j, rows(r), :]
        br = b_ref[j, rows(r), :]
        prod[r] = ar if prev is None else ar * prod[prev]
        loc[r] = br if prev is None else ar * loc[prev] + br
        prev = r
    pg, lg = prod[prev], loc[prev]
    ones = jnp.ones((SUBLANES,) + pg.shape[1:], F32)
    zeros = jnp.zeros_like(ones)
    s = 1
    while s < ng:
        p_sh = _shift_up(pg, ones, s) if reverse else _shift_down(ones, pg, s)
        l_sh = _shift_up(lg, zeros, s) if reverse else _shift_down(zeros, lg, s)
        lg = pg * l_sh + lg
        pg = pg * p_sh
        s *= 2
    leaving = pg * carry[0:1, :] + lg
    entering = _shift_up(leaving, carry, 1) if reverse else _shift_down(carry, leaving, 1)
    for r in order:
        out_ref[j, rows(r), :] = loc[r] + prod[r] * entering
    last = leaving[0:1, :] if reverse else leaving[ng - 1:ng, :]
    return jnp.broadcast_to(last, carry.shape)


def _gates(ca, wr, wi, br, bi, sp):
    cab = ca.astype(BF16)
    r = _sigmoid(jnp.dot(cab, wr, preferred_element_type=F32) + br)
    ig = _sigmoid(jnp.dot(cab, wi, preferred_element_type=F32) + bi)
    la = -LRU_C * r * sp
    a = jnp.exp(la)
    mult = jnp.sqrt(-jnp.tanh(la) * (a * a + 1.0))
    return r, ig, a, mult


def _mixer_fwd(u, wa, ba, wr_blk, br, wi_blk, bi, lam, wb, *, name):
    tp, din = u.shape
    dl = din // 6
    tt = MIX_ROWS
    cw = GATE_BLOCK
    nch = dl // cw
    assert tp % tt == 0 and dl % cw == 0

    def body(u_ref, wa_ref, ba_ref, wr_ref, br_ref, wi_ref, bi_ref, lam_ref, wb_ref,
             ca_ref, hs_ref, y_ref, xa_tail, v_tail, h_carry, a_s, b_s, h_s):
        @pl.when(pl.program_id(0) == 0)
        def _():
            xa_tail[...] = jnp.zeros_like(xa_tail)
            v_tail[...] = jnp.zeros_like(v_tail)
            h_carry[...] = jnp.zeros_like(h_carry)

        for ch in range(nch):
            cs = slice(ch * cw, (ch + 1) * cw)

            def seg(s):
                return slice(s * dl + ch * cw, s * dl + (ch + 1) * cw)

            xa = u_ref[:, seg(0)]
            halo = xa_tail[:, cs]
            ca = ba_ref[:, cs] + wa_ref[3:4, cs] * xa
            for kk in range(3):
                ca = ca + wa_ref[kk:kk + 1, cs] * _shift_down(halo, xa, 3 - kk)
            xa_tail[:, cs] = xa[tt - SUBLANES:]
            ca_ref[:, cs] = ca
            sp = _softplus(-lam_ref[:, cs])
            _, ig, a, mult = _gates(ca, wr_ref[ch], wi_ref[ch], br_ref[:, cs], bi_ref[:, cs], sp)
            _to_lane_blocks(a_s, cs, a)
            _to_lane_blocks(b_s, cs, mult * (ig * ca))

            bv = u_ref[:, seg(2)]
            v = u_ref[:, seg(3)] * u_ref[:, seg(4)]
            gb = u_ref[:, seg(5)]
            vh = v_tail[:, cs]
            cb = wb_ref[2:3, cs] * v
            for kk in range(2):
                cb = cb + wb_ref[kk:kk + 1, cs] * _shift_down(vh, v, 2 - kk)
            v_tail[:, cs] = v[tt - SUBLANES:]
            y_ref[:, dl + ch * cw: dl + (ch + 1) * cw] = (bv * cb * (gb * _sigmoid(gb))).astype(BF16)

        for ch in range(nch):
            cs = slice(ch * cw, (ch + 1) * cw)
            for j in range(cs.start // LANES, cs.stop // LANES):
                lanes = slice(j * LANES, (j + 1) * LANES)
                h_carry[:, lanes] = _scan_tile(a_s, b_s, h_s, h_carry[:, lanes], j, reverse=False)
            hsv = _from_lane_blocks(h_s, cs)
            hs_ref[:, cs] = hsv
            ga = u_ref[:, dl + ch * cw: dl + (ch + 1) * cw]
            y_ref[:, cs] = (hsv * (ga * _sigmoid(ga))).astype(BF16)

    row = lambda w: pl.BlockSpec((tt, w), lambda i: (i, 0))
    full = lambda shp: pl.BlockSpec(shp, lambda i: tuple(0 for _ in shp))
    return pl.pallas_call(
        body, name=name, grid=(tp // tt,),
        in_specs=[row(din), full((4, dl)), full((1, dl)), full((nch, cw, cw)), full((1, dl)),
                  full((nch, cw, cw)), full((1, dl)), full((1, dl)), full((3, dl))],
        out_specs=[row(dl), row(dl), row(2 * dl)],
        out_shape=[jax.ShapeDtypeStruct((tp, dl), F32), jax.ShapeDtypeStruct((tp, dl), F32),
                   jax.ShapeDtypeStruct((tp, 2 * dl), BF16)],
        scratch_shapes=[pltpu.VMEM((SUBLANES, dl), F32), pltpu.VMEM((SUBLANES, dl), F32),
                        pltpu.VMEM((SUBLANES, dl), F32)] + [pltpu.VMEM((dl // LANES, tt, LANES), F32)] * 3,
        compiler_params=_params(("arbitrary",)),
    )(u, wa, ba, wr_blk, br, wi_blk, bi, lam, wb)


SG_WA, SG_BA, SG_BR, SG_BI, SG_LAM, SG_WB, SG_ROWS = 0, 4, 5, 6, 7, 8, 16


def _mixer_bwd(u, ca, hs, dy, wa, wr_blk, br, wi_blk, bi, lam, wb, *, name):
    tp, din = u.shape
    dl = din // 6
    tt = MIX_ROWS
    cw = GATE_BLOCK
    nch = dl // cw
    nt = tp // tt
    hb = tt // SUBLANES
    tn_dims = (((0,), (0,)), ((), ()))
    nt_dims = (((1,), (1,)), ((), ()))

    def body(u_ref, uh_ref, ca_ref, hs_ref, hsh_ref, dy_ref, wa_ref, wr_ref, br_ref, wi_ref, bi_ref, lam_ref, wb_ref,
             du_ref, sg_ref, dwr_ref, dwi_ref,
             g_carry, a_head, dca_head, dcb_head, r_s, i_s, a_s, an_s, d_s, g_s):
        i = pl.program_id(0)
        first_tile = i == nt - 1

        @pl.when(i == 0)
        def _():
            for ref in (g_carry, a_head, dca_head, dcb_head, sg_ref, dwr_ref, dwi_ref):
                ref[...] = jnp.zeros_like(ref)

        def halo_of(x):
            return jnp.where(first_tile, 0.0, x)

        for ch in range(nch):
            cs = slice(ch * cw, (ch + 1) * cw)
            cav = ca_ref[:, cs]
            sp = _softplus(-lam_ref[:, cs])
            r, ig, a, _ = _gates(cav, wr_ref[ch], wi_ref[ch], br_ref[:, cs], bi_ref[:, cs], sp)
            r_s[:, cs] = r
            i_s[:, cs] = ig
            a_s[:, cs] = a
            _to_lane_blocks(an_s, cs, _shift_up(a, a_head[:, cs], 1))
            a_head[:, cs] = a[:SUBLANES]
            ga = u_ref[:, dl + ch * cw: dl + (ch + 1) * cw]
            _to_lane_blocks(d_s, cs, dy_ref[:, cs] * (ga * _sigmoid(ga)))

        for j in range(dl // LANES):
            lanes = slice(j * LANES, (j + 1) * LANES)
            g_carry[:, lanes] = _scan_tile(an_s, d_s, g_s, g_carry[:, lanes], j, reverse=True)

        def acc_row(r0, val):
            sg_ref[r0:r0 + 1, cs_cur[0]] += jnp.sum(val, axis=0, keepdims=True)

        cs_cur = [None]
        for ch in range(nch):
            cs = slice(ch * cw, (ch + 1) * cw)
            cs_cur[0] = cs

            def seg(s):
                return slice(s * dl + ch * cw, s * dl + (ch + 1) * cw)

            cav = ca_ref[:, cs]
            r = r_s[:, cs]
            ig = i_s[:, cs]
            a = a_s[:, cs]
            g = _from_lane_blocks(g_s, cs)
            hsv = hs_ref[:, cs]
            lamv = lam_ref[:, cs]
            sp = _softplus(-lamv)
            la = -LRU_C * r * sp
            e2 = a * a
            one_m_e2 = -jnp.tanh(la) * (e2 + 1.0)
            mult = jnp.sqrt(one_m_e2)
            hprev = _shift_down(halo_of(hsh_ref[:, cs]), hsv, 1)
            icav = ig * cav
            dla = g * (hprev * a - icav * (e2 * lax.rsqrt(one_m_e2)))
            gm = g * mult
            dzi = gm * icav * (1.0 - ig)
            dca = gm * ig
            dla_r = dla * r
            dzr = dla_r * (1.0 - r) * (-LRU_C * sp)
            sg_ref[SG_LAM:SG_LAM + 1, cs] += jnp.sum(dla_r, axis=0, keepdims=True) * (LRU_C * _sigmoid(-lamv))
            acc_row(SG_BR, dzr)
            acc_row(SG_BI, dzi)
            dzr_b = dzr.astype(BF16)
            dzi_b = dzi.astype(BF16)
            cab = cav.astype(BF16)
            dca = dca + lax.dot_general(dzr_b, wr_ref[ch], nt_dims, preferred_element_type=F32)
            dca = dca + lax.dot_general(dzi_b, wi_ref[ch], nt_dims, preferred_element_type=F32)
            dwr_ref[ch] += lax.dot_general(cab, dzr_b, tn_dims, preferred_element_type=F32)
            dwi_ref[ch] += lax.dot_general(cab, dzi_b, tn_dims, preferred_element_type=F32)
            acc_row(SG_BA, dca)
            xa = u_ref[:, seg(0)]
            head = dca_head[:, cs]
            dxa = wa_ref[3:4, cs] * dca
            acc_row(SG_WA + 3, dca * xa)
            for kk in range(3):
                later = _shift_up(dca, head, 3 - kk)
                acc_row(SG_WA + kk, later * xa)
                dxa = dxa + wa_ref[kk:kk + 1, cs] * later
            dca_head[:, cs] = dca[:SUBLANES]
            ga = u_ref[:, seg(1)]
            sga = _sigmoid(ga)
            dga = dy_ref[:, cs] * hsv * (sga + (ga * sga) * (1.0 - sga))
            du_ref[:, seg(0)] = dxa.astype(BF16)
            du_ref[:, seg(1)] = dga.astype(BF16)

            bv = u_ref[:, seg(2)]
            cv = u_ref[:, seg(3)]
            xb = u_ref[:, seg(4)]
            gb = u_ref[:, seg(5)]
            dyb = dy_ref[:, dl + ch * cw: dl + (ch + 1) * cw]
            v = cv * xb
            vh = halo_of(uh_ref[:, seg(3)] * uh_ref[:, seg(4)])
            v1 = _shift_down(vh, v, 1)
            v2 = _shift_down(vh, v, 2)
            cb = wb_ref[2:3, cs] * v + wb_ref[1:2, cs] * v1 + wb_ref[0:1, cs] * v2
            sgb = _sigmoid(gb)
            sl = gb * sgb
            dyb_b = dyb * bv
            dyb_cb = dyb * cb
            dcb = dyb_b * sl
            du_ref[:, seg(2)] = (dyb_cb * sl).astype(BF16)
            du_ref[:, seg(5)] = (dyb_cb * bv * (sgb + sl * (1.0 - sgb))).astype(BF16)
            bhead = dcb_head[:, cs]
            dv = wb_ref[2:3, cs] * dcb
            acc_row(SG_WB + 2, dcb * v)
            for kk in range(2):
                later = _shift_up(dcb, bhead, 2 - kk)
                acc_row(SG_WB + kk, later * v)
                dv = dv + wb_ref[kk:kk + 1, cs] * later
            dcb_head[:, cs] = dcb[:SUBLANES]
            du_ref[:, seg(3)] = (dv * xb).astype(BF16)
            du_ref[:, seg(4)] = (dv * cv).astype(BF16)

    rev = lambda w: pl.BlockSpec((tt, w), lambda i: (nt - 1 - i, 0))
    halo = lambda w: pl.BlockSpec((SUBLANES, w), lambda i: (jnp.maximum((nt - 1 - i) * hb - 1, 0), 0))
    full = lambda shp: pl.BlockSpec(shp, lambda i: tuple(0 for _ in shp))
    vm = lambda r: pltpu.VMEM((r, dl), F32)
    return pl.pallas_call(
        body, name=name, grid=(nt,),
        in_specs=[rev(din), halo(din), rev(dl), rev(dl), halo(dl), rev(2 * dl), full((4, dl)),
                  full((nch, cw, cw)), full((1, dl)), full((nch, cw, cw)), full((1, dl)), full((1, dl)), full((3, dl))],
        out_specs=[rev(din), full((SG_ROWS, dl)), full((nch, cw, cw)), full((nch, cw, cw))],
        out_shape=[jax.ShapeDtypeStruct((tp, din), BF16), jax.ShapeDtypeStruct((SG_ROWS, dl), F32),
                   jax.ShapeDtypeStruct((nch, cw, cw), F32), jax.ShapeDtypeStruct((nch, cw, cw), F32)],
        scratch_shapes=[vm(SUBLANES), vm(SUBLANES), vm(SUBLANES), vm(SUBLANES), vm(tt), vm(tt), vm(tt)]
        + [pltpu.VMEM((dl // LANES, tt, LANES), F32)] * 3,
        compiler_params=_params(("arbitrary",)),
    )(u, u, ca, hs, hs, dy, wa, wr_blk, br, wi_blk, bi, lam, wb)


def _adamw(w, g, m, v, *, name, landed=None, layer=None, depth=None, into=None, row_off=0):
    r, c = w.shape[-2:]
    rows = g.shape[0]
    tr = _tile(rows, 256, 2 * SUBLANES)
    assert row_off % tr == 0
    boff = row_off // tr
    bc1 = 1.0 - ADAM_B1 ** ADAM_STEP
    bc2 = 1.0 - ADAM_B2 ** ADAM_STEP
    slots = landed is not None

    def body(*refs):
        if into is not None:
            refs = refs[:-8] + refs[-4:]
        if slots:
            w_ref, g_ref, l_ref, m_ref, v_ref, grad_ref, delta_ref, nm_ref, nv_ref = refs
            gv = g_ref[...].astype(F32)
            for s in range(N_DEV - 1):
                gv = gv + l_ref[s].astype(F32)
        else:
            w_ref, g_ref, m_ref, v_ref, grad_ref, delta_ref, nm_ref, nv_ref = refs
            gv = g_ref[...]
        wv = w_ref[...]
        mn = ADAM_B1 * m_ref[...] + (1.0 - ADAM_B1) * gv
        vn = ADAM_B2 * v_ref[...] + (1.0 - ADAM_B2) * (gv * gv)
        m_hat = mn / bc1
        v_hat = vn / bc2
        grad_ref[...] = gv
        delta_ref[...] = -ADAM_LR * (m_hat / (jnp.sqrt(v_hat) + ADAM_EPS) + ADAM_WD * wv)
        nm_ref[...] = mn
        nv_ref[...] = vn

    if depth is None:
        blk = pl.BlockSpec((tr, c), lambda i: (i + boff, 0))
    else:
        blk = pl.BlockSpec((None, tr, c), lambda i: (layer, i + boff, 0))
    g_blk = pl.BlockSpec((tr, c), lambda i: (i, 0))
    l_spec = [pl.BlockSpec((N_DEV - 1, tr, c), lambda i: (0, i, 0))] if slots else []
    args = (w, g, landed, m, v) if slots else (w, g, m, v)
    in_specs = [blk, g_blk] + l_spec + [blk, blk]
    if depth is None:
        shp = jax.ShapeDtypeStruct((r, c), F32)
        out_blk = blk
    else:
        shp = jax.ShapeDtypeStruct((depth, r, c), F32)
        out_blk = pl.BlockSpec((None, tr, c), lambda i: (layer, i + boff, 0))
    aliases = {}
    if into is not None:
        aliases = {len(args) + j: j for j in range(4)}
        in_specs = in_specs + [ANY] * 4
        args = args + tuple(into)
    return pl.pallas_call(
        body, name=name, grid=(rows // tr,),
        in_specs=in_specs, out_specs=[out_blk] * 4,
        out_shape=[shp] * 4, input_output_aliases=aliases,
        compiler_params=_params(("parallel",)),
    )(*args)


def _slot_sum(g, *, name):
    _, r, c = g.shape
    tr = _tile(r, 512, SUBLANES)

    def body(g_ref, o_ref):
        gv = g_ref[0].astype(F32)
        for s in range(1, N_DEV):
            gv = gv + g_ref[s].astype(F32)
        o_ref[...] = gv

    return pl.pallas_call(
        body, name=name, grid=(r // tr,),
        in_specs=[pl.BlockSpec((N_DEV, tr, c), lambda i: (0, i, 0))],
        out_specs=pl.BlockSpec((tr, c), lambda i: (i, 0)),
        out_shape=jax.ShapeDtypeStruct((r, c), F32),
        compiler_params=_params(("parallel",)),
    )(g)


def _mesh_pos():
    x, y, c = lax.axis_index("x"), lax.axis_index("y"), lax.axis_index("c")
    return x, y, c, 4 * x + 2 * y + c


ANY = pl.BlockSpec(memory_space=pl.ANY)


def _all_gather(srcs, out_shapes, views, *, name, place=()):
    n = len(srcs)
    npl = len(place)

    def body(*refs):
        src = refs[:n]
        psrc = refs[n:n + npl]
        dst = refs[n + npl:2 * n + npl]
        pdst = refs[2 * n + npl:2 * (n + npl)]
        send_sems, recv_sems, local_sems = refs[2 * (n + npl):]
        x, y, c, me = _mesh_pos()
        sibling = (x, y, 1 - c)
        chips = [(1 - x, y), (x, 1 - y), (1 - x, 1 - y)]

        def dev(px, py, pc):
            return 4 * px + 2 * py + pc

        def copy(a, k, block, to, from_src=False):
            win = views[a](dst[a], dev(*block))
            return pltpu.make_async_remote_copy(
                src_ref=src[a] if from_src else win, dst_ref=win,
                send_sem=send_sems.at[a * 7 + k], recv_sem=recv_sems.at[a * 7 + k],
                device_id=to, device_id_type=MESH)

        mine = [pltpu.make_async_copy(src[a], views[a](dst[a], me), local_sems.at[a]) for a in range(n)]
        mine += [pltpu.make_async_copy(psrc[j], place[j][2](pdst[j], me), local_sems.at[n + j]) for j in range(npl)]
        started = []
        for a in range(n):
            mine[a].start()
            first = [copy(a, 0, (x, y, c), sibling, True)]
            first += [copy(a, 1 + j, (x, y, c), (*chip, c), True) for j, chip in enumerate(chips)]
            for cp in first:
                cp.start()
            started += first
        for cp in mine[n:]:
            cp.start()
        for a in range(n):
            for j, chip in enumerate(chips):
                copy(a, 1 + j, (*chip, c), (x, y, c)).wait_recv()
                fwd = copy(a, 4 + j, (*chip, c), sibling)
                fwd.start()
                started.append(fwd)
        for a in range(n):
            copy(a, 0, (x, y, 1 - c), (x, y, c)).wait_recv()
            for j, chip in enumerate(chips):
                copy(a, 4 + j, (*chip, 1 - c), (x, y, c)).wait_recv()
        for cp in started:
            cp.wait_send()
        for cp in mine:
            cp.wait()

    return pl.pallas_call(
        body, name=name,
        in_specs=[ANY] * (n + npl), out_specs=[ANY] * (n + npl),
        out_shape=[jax.ShapeDtypeStruct(s, x.dtype) for s, x in zip(out_shapes, srcs)]
        + [jax.ShapeDtypeStruct(shape, arr.dtype) for arr, shape, _ in place],
        scratch_shapes=[pltpu.SemaphoreType.DMA((7 * n,)), pltpu.SemaphoreType.DMA((7 * n,)),
                        pltpu.SemaphoreType.DMA((n + npl,))],
    )(*srcs, *[arr for arr, _, _ in place])


HBM = pl.BlockSpec(memory_space=pltpu.HBM)
SEM = pl.BlockSpec(memory_space=pltpu.SEMAPHORE)
EFFECT = pltpu.SideEffectType.DATAFLOW_SIDE_EFFECTING


def _peer_of(x, y, c, k):
    return (1 - x if k & 4 else x, 1 - y if k & 2 else y, 1 - c if k & 1 else c)


def _peer_copies(n, wins, src, land, send_sems, recv_sems):
    x, y, c, me = _mesh_pos()
    out = []
    for a in range(n):
        for k in range(1, N_DEV):
            px, py, pc = _peer_of(x, y, c, k)
            s_win, d_win = wins[a](src[a], land[a], me, 4 * px + 2 * py + pc, k)
            out.append(pltpu.make_async_remote_copy(
                src_ref=s_win, dst_ref=d_win,
                send_sem=send_sems.at[a * 7 + k - 1], recv_sem=recv_sems.at[a * 7 + k - 1],
                device_id=(px, py, pc), device_id_type=MESH))
    return out


def _push_start(srcs, lands, wins, *, name):
    n = len(srcs)

    def body(*refs):
        src = refs[:n]
        land = refs[n:2 * n]
        send_sems, recv_sems = refs[2 * n], refs[2 * n + 1]
        token = refs[-1]
        for cp in _peer_copies(n, wins, src, land, send_sems, recv_sems):
            cp.start()
        token[...] = jnp.zeros_like(token)

    bufs = (*srcs, *lands)
    return pl.pallas_call(
        body, name=name,
        out_shape=(pltpu.SemaphoreType.DMA((7 * n,)), pltpu.SemaphoreType.DMA((7 * n,)),
                   *[pltpu.HBM(v.shape, v.dtype) for v in bufs], jax.ShapeDtypeStruct((SUBLANES, LANES), F32)),
        in_specs=[HBM] * (2 * n),
        out_specs=(SEM, SEM, *[HBM] * (2 * n), pl.BlockSpec(memory_space=pltpu.VMEM)),
        input_output_aliases={i: 2 + i for i in range(2 * n)},
        compiler_params=pltpu.CompilerParams(has_side_effects=EFFECT),
    )(*[pltpu.with_memory_space_constraint(v, pltpu.HBM) for v in bufs])


def _push_wait(handle, wins, after, *, name):
    send_sems, recv_sems, *bufs, _ = handle
    n = len(bufs) // 2

    def body(*refs):
        src = refs[:n]
        land = refs[n:2 * n]
        for cp in _peer_copies(n, wins, src, land, refs[2 * n], refs[2 * n + 1]):
            cp.wait_send()
            cp.wait_recv()

    outs = pl.pallas_call(
        body, name=name,
        out_shape=tuple(pltpu.HBM(v.shape, v.dtype) for v in bufs),
        in_specs=[HBM] * (2 * n) + [SEM, SEM, ANY],
        out_specs=tuple([HBM] * (2 * n)),
        input_output_aliases={i: i for i in range(2 * n)},
        compiler_params=pltpu.CompilerParams(has_side_effects=EFFECT),
    )(*bufs, send_sems, recv_sems, after)
    return outs[:n], outs[n:]


def _gather_lead(src, land, me, peer, k):
    return src, land.at[me]


def _gather_cols(width):
    def win(src, land, me, peer, k):
        return src, land.at[:, pl.ds(me * width, width)]
    return win


def _scatter_lead(src, land, me, peer, k):
    return src.at[peer], land.at[k - 1]


def _scatter_cols(width):
    def win(src, land, me, peer, k):
        return src.at[:, pl.ds(peer * width, width)], land.at[k - 1]
    return win


def _place_block(own, *, cols, name):
    rows, width = own.shape
    tr = _tile(rows, 512, 2 * SUBLANES)
    _, _, _, me = _mesh_pos()

    def body(me_ref, x_ref, o_ref):
        o_ref[...] = x_ref[...]

    if cols:
        out_spec = pl.BlockSpec((tr, width), lambda i, me_ref: (i, me_ref[0]))
        shape = (rows, N_DEV * width)
    else:
        out_spec = pl.BlockSpec((None, tr, width), lambda i, me_ref: (me_ref[0], i, 0))
        shape = (N_DEV, rows, width)
    return pl.pallas_call(
        body, name=name,
        grid_spec=pltpu.PrefetchScalarGridSpec(
            num_scalar_prefetch=1, grid=(rows // tr,),
            in_specs=[pl.BlockSpec((tr, width), lambda i, me_ref: (i, 0))], out_specs=out_spec),
        out_shape=jax.ShapeDtypeStruct(shape, own.dtype),
        compiler_params=_params(("arbitrary",)),
    )(me.astype(jnp.int32).reshape(1), own)


def _dep(x, token):
    return x + token[0, 0].astype(x.dtype)


def _lead(ref, d):
    return ref.at[d]


def _col_window(width):
    def view(ref, d):
        return ref.at[:, pl.ds(d * width, width)]
    return view


def _pack(arrs):
    flat = jnp.concatenate([a.reshape(-1).astype(F32) for a in arrs])
    n = flat.shape[0]
    rows = -(-n // (SUBLANES * LANES)) * SUBLANES
    return jnp.pad(flat, (0, rows * LANES - n)).reshape(rows, LANES)


def _unpack(buf, shapes):
    flat = buf.reshape(-1)
    out, off = [], 0
    for s in shapes:
        n = 1
        for q in s:
            n *= q
        out.append(flat[off:off + n].reshape(s))
        off += n
    return out


def _blockdiag(w, cw):
    h, hd, _ = w.shape
    per = cw // hd
    wg = w.reshape(h // per, per, hd, hd)
    eye = jnp.eye(per, dtype=w.dtype)
    blk = jnp.einsum("gpij,pq->gpiqj", wg, eye)
    return blk.reshape(h // per, cw, cw).astype(BF16)


def _blockdiag_extract(g, hd):
    n, cw, _ = g.shape
    per = cw // hd
    g5 = g.reshape(n, per, hd, per, hd)
    idx = jnp.arange(per)
    return g5[:, idx, :, idx, :].transpose(1, 0, 2, 3).reshape(n * per, hd, hd)


def kernel(x, meta, norm_g, w_in, conv_a_w, conv_a_b, lru_wr, lru_br, lru_wi, lru_bi, lru_lambda, conv_b_w, w_out, final_g, loss_target, m_meta, m_norm_g, m_w_in, m_conv_a_w, m_conv_a_b, m_lru_wr, m_lru_br, m_lru_wi, m_lru_bi, m_lru_lambda, m_conv_b_w, m_w_out, m_final_g, v_meta, v_norm_g, v_w_in, v_conv_a_w, v_conv_a_b, v_lru_wr, v_lru_br, v_lru_wi, v_lru_bi, v_lru_lambda, v_conv_b_w, v_w_out, v_final_g):
    _, seq, d = x.shape
    n_meta = meta.shape[0]
    depth = w_in.shape[0]
    din = w_in.shape[2] * N_DEV
    dl = din // 6
    dmix = 2 * dl
    wcol = w_in.shape[2]
    wrow = w_out.shape[1]
    mcol = meta.shape[1]
    ccol = conv_a_w.shape[2]
    heads, hd = lru_wr.shape[1], lru_wr.shape[2]
    n_tok = n_meta + seq
    tp = -(-n_tok // TOKEN_TILE) * TOKEN_TILE
    me = 4 * lax.axis_index("x") + 2 * lax.axis_index("y") + lax.axis_index("c")

    bf = lambda a: a.astype(BF16)
    small_mine = _pack([meta, conv_a_w, conv_b_w])
    first = _all_gather([bf(w_in[0]), small_mine], [(d, din), (N_DEV,) + small_mine.shape],
                        [_col_window(wcol), _lead], name="gather_first")
    parts = [_unpack(first[1][s], [meta.shape, conv_a_w.shape, conv_b_w.shape]) for s in range(N_DEV)]
    meta_full = jnp.concatenate([p[0] for p in parts], axis=1)
    wa_full = jnp.concatenate([p[1] for p in parts], axis=2)
    wb_full = jnp.concatenate([p[2] for p in parts], axis=2)
    w_in_full = [None] * depth
    w_out_full = [None] * depth

    push_out = [None] * depth
    push_in = [None] * depth
    w_in_full[0], src = lax.optimization_barrier((first[0], bf(w_out[0])))
    push_out[0] = _push_start([src], [_place_block(src, cols=False, name="place_wout_0")], [_gather_lead],
                              name="gather_wout_0_start")
    token = push_out[0][-1]
    for l in range(1, depth):
        src = bf(_dep(w_in[l], token))
        push_in[l] = _push_start([src], [_place_block(src, cols=True, name=f"place_win_{l}")], [_gather_cols(wcol)],
                                 name=f"gather_win_{l}_start")
        src = bf(_dep(w_out[l], push_in[l][-1]))
        push_out[l] = _push_start([src], [_place_block(src, cols=False, name=f"place_wout_{l}")], [_gather_lead],
                                  name=f"gather_wout_{l}_start")
        token = push_out[l][-1]

    wr_blk = [_blockdiag(lru_wr[l], GATE_BLOCK) for l in range(depth)]
    wi_blk = [_blockdiag(lru_wi[l], GATE_BLOCK) for l in range(depth)]
    vec = lambda a: a.reshape(1, dl)

    h = jnp.concatenate([meta_full, x[0], jnp.zeros((tp - n_tok, d), F32)], axis=0)
    tm = _tile(tp, 1408)
    saved = []
    for l in range(depth):
        hn = _rms_fwd(h, _dep(norm_g[l], token) if l == 0 else norm_g[l], name=f"rms_fwd_{l}")
        if l > 0:
            _, landed = _push_wait(push_in[l], [_gather_cols(wcol)], hn, name=f"gather_win_{l}_wait")
            w_in_full[l] = landed[0]
        u = _matmul(hn, w_in_full[l], tm=tm, tn=_tile(din, 768), tk=d, name=f"mm_u_{l}")
        ca, hs, y = _mixer_fwd(u, wa_full[l], vec(conv_a_b[l]), wr_blk[l], vec(lru_br[l]), wi_blk[l], vec(lru_bi[l]),
                               vec(lru_lambda[l]), wb_full[l], name=f"mixer_fwd_{l}")
        _, landed = _push_wait(push_out[l], [_gather_lead], y, name=f"gather_wout_{l}_wait")
        w_out_full[l] = landed[0].reshape(dmix, d)
        h_next = _matmul(y, w_out_full[l], tm=tm, tn=_tile(d, 512), tk=dmix, add=h, name=f"mm_out_{l}")
        saved.append((h, hn, u, ca, hs, y))
        h = h_next

    dh, dhb, dg_final, loss_part = _loss_head(h, loss_target[0], final_g, n_meta=n_meta, n_tok=n_tok,
                                              name="loss_head")

    small_grads = [None] * depth
    sent_out = [None] * depth
    sent_in = [None] * depth
    scatter_in = [_scatter_cols(wcol)]
    token = None
    dg_norms = []
    for l in reversed(range(depth)):
        h_in, hn, u, ca, hs, y = saved[l]
        dy = _matmul(dhb, w_out_full[l], tb=True, tm=tm, tn=_tile(dmix, 512), tk=d, dep=token, name=f"mm_dy_{l}")
        dw_out = _matmul(y, dhb, ta=True, tm=_tile(dmix, 512), tn=_tile(d, 1024), tk=tp, out_dtype=BF16,
                         name=f"mm_dwout_{l}")
        sent_out[l] = _push_start([dw_out.reshape(N_DEV, wrow, d)], [lax.empty((N_DEV - 1, wrow, d), BF16)],
                                  [_scatter_lead], name=f"scatter_wout_{l}_start")
        du, sg, dwr, dwi = _mixer_bwd(u, ca, hs, dy, wa_full[l], wr_blk[l], vec(lru_br[l]), wi_blk[l], vec(lru_bi[l]),
                                      vec(lru_lambda[l]), _dep(wb_full[l], sent_out[l][-1]), name=f"mixer_bwd_{l}")
        small_grads[l] = (sg, dwr, dwi)
        if l == 0:
            early = [_pack([
                jnp.stack([small_grads[j][0][SG_BA] for j in range(depth)]),
                jnp.stack([small_grads[j][0][SG_BR] for j in range(depth)]),
                jnp.stack([small_grads[j][0][SG_BI] for j in range(depth)]),
                jnp.stack([small_grads[j][0][SG_LAM] for j in range(depth)]),
                jnp.stack([small_grads[j][0][SG_WA:SG_WA + 4] for j in range(depth)]),
                jnp.stack([small_grads[j][0][SG_WB:SG_WB + 3] for j in range(depth)]),
                dg_final[0], *dg_norms]),
                _pack([jnp.stack([_blockdiag_extract(small_grads[j][1], hd) for j in range(depth)]),
                       jnp.stack([_blockdiag_extract(small_grads[j][2], hd) for j in range(depth)])]).astype(BF16)]
            early_land = [lax.dynamic_update_slice(lax.empty((N_DEV,) + a.shape, a.dtype), a[None], (me, 0, 0))
                          for a in early]
            sent_early = _push_start(early, early_land, [_gather_lead] * 2, name="gather_early_grads_start")
        parts = 2 if l == 0 else 1
        token = sent_early[-1] if l == 0 else None
        sent_in[l] = []
        for p in range(parts):
            dw_in = _matmul(hn, du, ta=True, tm=_tile(d // parts, 512), tn=_tile(din, 768), tk=tp, out_dtype=BF16,
                            dep=token, m_part=(p, parts), name=f"mm_dwin_{l}_{p}")
            sent_in[l].append(_push_start([dw_in], [lax.empty((N_DEV - 1, d // parts, wcol), BF16)], scatter_in,
                                          name=f"scatter_win_{l}_{p}_start"))
            token = sent_in[l][-1][-1]
        dhn = _matmul(du, w_in_full[l], tb=True, tm=_tile(tp, 704, 2 * SUBLANES), tn=_tile(d, 512), tk=din, dep=token,
                      name=f"mm_dhn_{l}")
        if l > 0:
            dh, dhb, dg_norm = _rms_bwd(h_in, dhn, dh, norm_g[l], name=f"rms_bwd_{l}")
            dg_norms.append(dg_norm[0])
        else:
            grad_x, d_meta, dg_norm = _rms_bwd_first(h_in, dhn, dh, norm_g[l], n_meta=n_meta, seq=seq,
                                                     name=f"rms_bwd_{l}")

    late = _pack([dg_norm[0], d_meta, loss_part[0:1, 0:1]])
    late_all = _all_gather([late], [(N_DEV,) + late.shape], [_lead], name="gather_late_grads")[0]
    late_sum = _unpack(_slot_sum(late_all, name="sum_late_grads"), [(d,), (n_meta, d), ()])
    loss = late_sum[2]
    _, early_all = _push_wait(sent_early, [_gather_lead] * 2, late_sum[0], name="gather_early_grads_wait")
    vec_shapes = [conv_a_b.shape, lru_br.shape, lru_bi.shape, lru_lambda.shape, (depth, 4, dl), (depth, 3, dl),
                  final_g.shape] + [(d,)] * (depth - 1)
    e = _unpack(_slot_sum(early_all[0], name="sum_early_vectors"), vec_shapes)
    g_wr, g_wi = _unpack(_slot_sum(early_all[1], name="sum_early_maps"), [lru_wr.shape, lru_wi.shape])
    g_norm = jnp.stack([late_sum[0]] + e[7:][::-1])
    g_meta = lax.dynamic_slice_in_dim(late_sum[1], me * mcol, mcol, axis=1)
    g_wa = lax.dynamic_slice_in_dim(e[4], me * ccol, ccol, axis=2)
    g_wb = lax.dynamic_slice_in_dim(e[5], me * ccol, ccol, axis=2)

    small_w = [norm_g, conv_a_b, lru_wr, lru_br, lru_wi, lru_bi, lru_lambda, final_g, meta, conv_a_w, conv_b_w]
    small_m = [m_norm_g, m_conv_a_b, m_lru_wr, m_lru_br, m_lru_wi, m_lru_bi, m_lru_lambda, m_final_g, m_meta,
               m_conv_a_w, m_conv_b_w]
    small_v = [v_norm_g, v_conv_a_b, v_lru_wr, v_lru_br, v_lru_wi, v_lru_bi, v_lru_lambda, v_final_g, v_meta,
               v_conv_a_w, v_conv_b_w]
    small_g = [g_norm, e[0], g_wr, e[1], g_wi, e[2], e[3], e[6], g_meta, g_wa, g_wb]
    small_out = _adamw(_pack(small_w), _pack(small_g), _pack(small_m), _pack(small_v), name="adamw_small")
    small_shapes = [a.shape for a in small_w]
    s_grad, s_delta, s_m, s_v = [_unpack(o, small_shapes) for o in small_out]

    win_out = None
    wout_out = None
    after = small_out[0]
    for l in reversed(range(depth)):
        src, landed = _push_wait(sent_out[l], [_scatter_lead], after, name=f"scatter_wout_{l}_wait")
        own = lax.dynamic_index_in_dim(src[0], me, 0, keepdims=False)
        wout_out = _adamw(w_out, own, m_w_out, v_w_out, landed=landed[0], layer=l, depth=depth,
                          into=wout_out, name=f"adamw_w_out_{l}")
        after = wout_out[0]
        for p, sent in enumerate(sent_in[l]):
            src, landed = _push_wait(sent, scatter_in, after, name=f"scatter_win_{l}_{p}_wait")
            own = lax.dynamic_slice_in_dim(src[0], me * wcol, wcol, axis=1)
            win_out = _adamw(w_in, own, m_w_in, v_w_in, landed=landed[0], layer=l, depth=depth,
                             into=win_out, row_off=p * own.shape[0], name=f"adamw_w_in_{l}_{p}")
            after = win_out[0]

    names = ["norm_g", "conv_a_b", "lru_wr", "lru_br", "lru_wi", "lru_bi", "lru_lambda", "final_g", "meta",
             "conv_a_w", "conv_b_w"]
    order = ["meta", "norm_g", "w_in", "conv_a_w", "conv_a_b", "lru_wr", "lru_br", "lru_wi", "lru_bi", "lru_lambda",
             "conv_b_w", "w_out", "final_g"]

    def family(idx, small):
        table = {nm: small[i] for i, nm in enumerate(names)}
        table["w_in"] = win_out[idx]
        table["w_out"] = wout_out[idx]
        return [table[nm] for nm in order]

    return (loss, grad_x, *family(0, s_grad), *family(1, s_delta), *family(2, s_m), *family(3, s_v))
```

```python
import functools

import jax
import jax.numpy as jnp
from jax import lax
from jax.experimental import pallas as pl
from jax.experimental.pallas import tpu as pltpu

F32 = jnp.float32
BF16 = jnp.bfloat16
MESH = pl.DeviceIdType.MESH

N_DEV = 8
RMS_EPS = 1e-6
LRU_C = 8.0
ADAM_LR = 0.001
ADAM_B1 = 0.9
ADAM_B2 = 0.999
ADAM_EPS = 1e-08
ADAM_WD = 0.01
ADAM_STEP = 10

V7X_VMEM_LIMIT = 52 * 1024 * 1024
LANES = 128
SUBLANES = 8
TOKEN_TILE = 384
MIX_ROWS = 128
GATE_BLOCK = 256


def _params(sem):
    return pltpu.CompilerParams(dimension_semantics=sem, vmem_limit_bytes=V7X_VMEM_LIMIT)


def _tile(n, target, align=LANES):
    best = None
    for t in range(align, min(n, target) + 1, align):
        if n % t == 0:
            best = t
    return n if best is None else best


def _sigmoid(z):
    return 0.5 * jnp.tanh(0.5 * z) + 0.5


def _softplus(z):
    e = jnp.exp(-jnp.abs(z))
    u = 1.0 + e
    l1p = jnp.where(u == 1.0, e, jnp.log(u) * e / jnp.where(u == 1.0, 1.0, u - 1.0))
    return jnp.maximum(z, 0.0) + l1p


def _matmul(a, b, *, ta=False, tb=False, tm, tn, tk, out_dtype=F32, add=None, dep=None, m_part=None, name):
    m, k = (a.shape[1], a.shape[0]) if ta else a.shape
    m_off = 0
    if m_part is not None:
        assert add is None and m % (m_part[1] * tm) == 0
        m //= m_part[1]
        m_off = m_part[0] * (m // tm)
    n, kb = b.shape if tb else b.shape[::-1]
    assert kb == k
    assert m % tm == 0 and n % tn == 0 and k % tk == 0, (m, n, k, tm, tn, tk)
    nk = k // tk
    a_spec = pl.BlockSpec((tk, tm), lambda i, j, q: (q, i + m_off)) if ta \
        else pl.BlockSpec((tm, tk), lambda i, j, q: (i + m_off, q))
    b_spec = pl.BlockSpec((tn, tk), lambda i, j, q: (j, q)) if tb else pl.BlockSpec((tk, tn), lambda i, j, q: (q, j))
    o_spec = pl.BlockSpec((tm, tn), lambda i, j, q: (i, j))
    o_shape = (m, n)
    dims = (((0 if ta else 1,), (1 if tb else 0,)), ((), ()))
    has_add = add is not None
    has_dep = dep is not None

    def body(*refs):
        if has_dep:
            refs = refs[:-3] + refs[-2:]
        if has_add:
            a_ref, b_ref, add_ref, o_ref, acc_ref = refs
        else:
            a_ref, b_ref, o_ref, acc_ref = refs
        q = pl.program_id(2)
        part = lax.dot_general(a_ref[...], b_ref[...], dims, preferred_element_type=F32)

        def finish(acc):
            if has_add:
                acc = acc + add_ref[...]
            o_ref[...] = acc.astype(out_dtype)

        if nk == 1:
            finish(part)
        else:
            @pl.when(q == 0)
            def _():
                acc_ref[...] = part

            @pl.when(jnp.logical_and(q > 0, q < nk - 1))
            def _():
                acc_ref[...] += part

            @pl.when(q == nk - 1)
            def _():
                finish(acc_ref[...] + part)

    in_specs = [a_spec, b_spec] + ([o_spec] if has_add else [])
    args = (a, b) + ((add,) if has_add else ())
    if has_dep:
        in_specs.append(pl.BlockSpec((SUBLANES, LANES), lambda i, j, q: (0, 0)))
        args += (dep,)
    acc_shape = (tm, tn) if nk > 1 else (SUBLANES, LANES)
    return pl.pallas_call(
        body, name=name,
        grid=(m // tm, n // tn, nk),
        in_specs=in_specs, out_specs=o_spec,
        out_shape=jax.ShapeDtypeStruct(o_shape, out_dtype),
        scratch_shapes=[pltpu.VMEM(acc_shape, F32)],
        compiler_params=_params(("parallel", "parallel", "arbitrary")),
    )(*args)


def _rms_fwd(h, g, *, name):
    tp, d = h.shape
    tr = _tile(tp, 512, SUBLANES)

    def body(h_ref, g_ref, o_ref):
        hv = h_ref[...]
        rstd = lax.rsqrt(jnp.mean(hv * hv, axis=-1, keepdims=True) + RMS_EPS)
        o_ref[...] = (hv * rstd * g_ref[...]).astype(BF16)

    return pl.pallas_call(
        body, name=name, grid=(tp // tr,),
        in_specs=[pl.BlockSpec((tr, d), lambda i: (i, 0)), pl.BlockSpec((1, d), lambda i: (0, 0))],
        out_specs=pl.BlockSpec((tr, d), lambda i: (i, 0)),
        out_shape=jax.ShapeDtypeStruct((tp, d), BF16),
        compiler_params=_params(("parallel",)),
    )(h, g.reshape(1, d))


def _rms_fwd_first(x, meta, g, *, tp, name):
    seq, d = x.shape
    n_meta = meta.shape[0]
    n_tok = n_meta + seq
    tr = MIX_ROWS
    assert tp % tr == 0 and seq % tr == 0 and tr % n_meta == 0
    per = tr // n_meta

    def body(x_ref, xp_ref, m_ref, g_ref, h_ref, o_ref):
        i = pl.program_id(0)
        head = jnp.where(i == 0, m_ref[...], xp_ref[...])
        rows = i * tr + lax.broadcasted_iota(jnp.int32, (tr, 1), 0)
        hv = jnp.where(rows < n_tok, jnp.concatenate([head, x_ref[:tr - n_meta, :]], axis=0), 0.0)
        h_ref[...] = hv
        rstd = lax.rsqrt(jnp.mean(hv * hv, axis=-1, keepdims=True) + RMS_EPS)
        o_ref[...] = (hv * rstd * g_ref[...]).astype(BF16)

    row = pl.BlockSpec((tr, d), lambda i: (i, 0))
    own = pl.BlockSpec((tr, d), lambda i: (jnp.minimum(i, seq // tr - 1), 0))
    before = pl.BlockSpec((n_meta, d), lambda i: (jnp.maximum(i * per - 1, 0), 0))
    return pl.pallas_call(
        body, name=name, grid=(tp // tr,),
        in_specs=[own, before, pl.BlockSpec((n_meta, d), lambda i: (0, 0)), pl.BlockSpec((1, d), lambda i: (0, 0))],
        out_specs=[row, row],
        out_shape=[jax.ShapeDtypeStruct((tp, d), F32), jax.ShapeDtypeStruct((tp, d), BF16)],
        compiler_params=_params(("parallel",)),
    )(x, x, meta, g.reshape(1, d))


def _rms_bwd(h, dhn, dout, g, *, name):
    tp, d = h.shape
    tr = _tile(tp, 384, SUBLANES)

    def body(h_ref, dhn_ref, dout_ref, g_ref, dh_ref, dhb_ref, dg_ref):
        hv = h_ref[...]
        rstd = lax.rsqrt(jnp.mean(hv * hv, axis=-1, keepdims=True) + RMS_EPS)
        xhat = hv * rstd
        dn = dhn_ref[...]
        dxhat = dn * g_ref[...]
        dh = dout_ref[...] + rstd * (dxhat - xhat * jnp.mean(dxhat * xhat, axis=-1, keepdims=True))
        dh_ref[...] = dh
        dhb_ref[...] = dh.astype(BF16)
        part = jnp.sum(dn * xhat, axis=0, keepdims=True)

        @pl.when(pl.program_id(0) == 0)
        def _():
            dg_ref[...] = part

        @pl.when(pl.program_id(0) > 0)
        def _():
            dg_ref[...] += part

    row = pl.BlockSpec((tr, d), lambda i: (i, 0))
    vec = pl.BlockSpec((1, d), lambda i: (0, 0))
    return pl.pallas_call(
        body, name=name, grid=(tp // tr,),
        in_specs=[row, row, row, vec],
        out_specs=[row, row, vec],
        out_shape=[jax.ShapeDtypeStruct((tp, d), F32), jax.ShapeDtypeStruct((tp, d), BF16),
                   jax.ShapeDtypeStruct((1, d), F32)],
        compiler_params=_params(("arbitrary",)),
    )(h, dhn, dout, g.reshape(1, d))


def _rms_bwd_first(h, dhn, dout, g, *, n_meta, seq, name):
    tp, d = h.shape
    tr = MIX_ROWS
    assert seq % tr == 0 and tr % n_meta == 0 and tp >= seq + n_meta
    nt = seq // tr
    per = tr // n_meta

    def grads(hv, dn, do, gv):
        rstd = lax.rsqrt(jnp.mean(hv * hv, axis=-1, keepdims=True) + RMS_EPS)
        xhat = hv * rstd
        dxhat = dn * gv
        dh = do + rstd * (dxhat - xhat * jnp.mean(dxhat * xhat, axis=-1, keepdims=True))
        return dh, jnp.sum(dn * xhat, axis=0, keepdims=True)

    def body(h_ref, dhn_ref, dout_ref, hn_ref, dhnn_ref, doutn_ref, g_ref, gx_ref, dmeta_ref, dg_ref):
        i = pl.program_id(0)
        gv = g_ref[...]
        dh, part = grads(h_ref[...], dhn_ref[...], dout_ref[...], gv)
        dh_next, part_next = grads(hn_ref[...], dhnn_ref[...], doutn_ref[...], gv)
        gx_ref[...] = jnp.concatenate([dh[n_meta:], dh_next], axis=0)

        @pl.when(i == 0)
        def _():
            dmeta_ref[...] = dh[:n_meta]
            dg_ref[...] = part

        @pl.when(i > 0)
        def _():
            dg_ref[...] += part

        @pl.when(i == nt - 1)
        def _():
            dg_ref[...] += part_next

    row = pl.BlockSpec((tr, d), lambda i: (i, 0))
    nxt = pl.BlockSpec((n_meta, d), lambda i: ((i + 1) * per, 0))
    vec = pl.BlockSpec((1, d), lambda i: (0, 0))
    return pl.pallas_call(
        body, name=name, grid=(nt,),
        in_specs=[row, row, row, nxt, nxt, nxt, vec],
        out_specs=[pl.BlockSpec((None, tr, d), lambda i: (0, i, 0)), pl.BlockSpec((n_meta, d), lambda i: (0, 0)), vec],
        out_shape=[jax.ShapeDtypeStruct((1, seq, d), F32), jax.ShapeDtypeStruct((n_meta, d), F32),
                   jax.ShapeDtypeStruct((1, d), F32)],
        compiler_params=_params(("arbitrary",)),
    )(h, dhn, dout, h, dhn, dout, g.reshape(1, d))


def _loss_head(h, tgt, g, *, n_meta, n_tok, name):
    tp, d = h.shape
    seq = tgt.shape[0]
    tr = MIX_ROWS
    assert tp % tr == 0 and seq % tr == 0 and tr % n_meta == 0
    per = tr // n_meta

    def body(h_ref, t_ref, tp_ref, g_ref, dh_ref, dhb_ref, dg_ref, loss_ref):
        i = pl.program_id(0)
        hv = h_ref[...]
        rstd = lax.rsqrt(jnp.mean(hv * hv, axis=-1, keepdims=True) + RMS_EPS)
        xhat = hv * rstd
        gv = g_ref[...]
        rows = i * tr + lax.broadcasted_iota(jnp.int32, (tr, 1), 0)
        valid = jnp.logical_and(rows >= n_meta, rows < n_tok)
        target = jnp.concatenate([tp_ref[...], t_ref[:tr - n_meta, :]], axis=0)
        err = jnp.where(valid, xhat * gv - target, 0.0)
        dy = err * (1.0 / d)
        dxhat = dy * gv
        dh = rstd * (dxhat - xhat * jnp.mean(dxhat * xhat, axis=-1, keepdims=True))
        dh_ref[...] = dh
        dhb_ref[...] = dh.astype(BF16)
        dg_part = jnp.sum(dy * xhat, axis=0, keepdims=True)
        per_row = jnp.sum(err * err, axis=-1, keepdims=True) * (1.0 / d)
        loss_part = jnp.broadcast_to(0.5 * jnp.sum(per_row, axis=0, keepdims=True), (SUBLANES, LANES))

        @pl.when(i == 0)
        def _():
            dg_ref[...] = dg_part
            loss_ref[...] = loss_part

        @pl.when(i > 0)
        def _():
            dg_ref[...] += dg_part
            loss_ref[...] += loss_part

    row = pl.BlockSpec((tr, d), lambda i: (i, 0))
    vec = pl.BlockSpec((1, d), lambda i: (0, 0))
    own = pl.BlockSpec((tr, d), lambda i: (jnp.minimum(i, seq // tr - 1), 0))
    before = pl.BlockSpec((n_meta, d), lambda i: (jnp.maximum(i * per - 1, 0), 0))
    return pl.pallas_call(
        body, name=name, grid=(tp // tr,),
        in_specs=[row, own, before, vec],
        out_specs=[row, row, vec, pl.BlockSpec((SUBLANES, LANES), lambda i: (0, 0))],
        out_shape=[jax.ShapeDtypeStruct((tp, d), F32), jax.ShapeDtypeStruct((tp, d), BF16),
                   jax.ShapeDtypeStruct((1, d), F32), jax.ShapeDtypeStruct((SUBLANES, LANES), F32)],
        compiler_params=_params(("arbitrary",)),
    )(h, tgt, tgt, g.reshape(1, d))


def _shift_down(halo, tile, s):
    if s == 0:
        return tile
    ext = jnp.concatenate([halo, tile], axis=0)
    return pltpu.roll(ext, s, 0)[SUBLANES:]


def _shift_up(tile, head, s):
    if s == 0:
        return tile
    ext = jnp.concatenate([tile, head], axis=0)
    n = ext.shape[0]
    return pltpu.roll(ext, n - s, 0)[: tile.shape[0]]


def _to_lane_blocks(ref, cols, val):
    for j in range(cols.start // LANES, cols.stop // LANES):
        ref[j] = val[:, j * LANES - cols.start:(j + 1) * LANES - cols.start]


def _from_lane_blocks(ref, cols):
    return jnp.concatenate([ref[j] for j in range(cols.start // LANES, cols.stop // LANES)], axis=1)


def _scan_tile(a_ref, b_ref, out_ref, carry, j, *, reverse):
    ng = a_ref.shape[1] // SUBLANES
    order = list(range(SUBLANES))[::-1] if reverse else list(range(SUBLANES))

    def rows(r):
        return pl.ds(r, ng, stride=SUBLANES)

    prod, loc = {}, {}
    prev = None
    for r in order:
        ar = a_ref[j, rows(r), :]
        br = b_ref[j, rows(r), :]
        prod[r] = ar if prev is None else ar * prod[prev]
        loc[r] = br if prev is None else ar * loc[prev] + br
        prev = r
    pg, lg = prod[prev], loc[prev]
    ones = jnp.ones((SUBLANES,) + pg.shape[1:], F32)
    zeros = jnp.zeros_like(ones)
    s = 1
    while s < ng:
        p_sh = _shift_up(pg, ones, s) if reverse else _shift_down(ones, pg, s)
        l_sh = _shift_up(lg, zeros, s) if reverse else _shift_down(zeros, lg, s)
        lg = pg * l_sh + lg
        pg = pg * p_sh
        s *= 2
    leaving = pg * carry[0:1, :] + lg
    entering = _shift_up(leaving, carry, 1) if reverse else _shift_down(carry, leaving, 1)
    for r in order:
        out_ref[j, rows(r), :] = loc[r] + prod[r] * entering
    last = leaving[0:1, :] if reverse else leaving[ng - 1:ng, :]
    return jnp.broadcast_to(last, carry.shape)


def _gates(ca, wr, wi, br, bi, sp):
    cab = ca.astype(BF16)
    r = _sigmoid(jnp.dot(cab, wr, preferred_element_type=F32) + br)
    ig = _sigmoid(jnp.dot(cab, wi, preferred_element_type=F32) + bi)
    la = -LRU_C * r * sp
    a = jnp.exp(la)
    mult = jnp.sqrt(-jnp.tanh(la) * (a * a + 1.0))
    return r, ig, a, mult


def _mixer_fwd(u, wa, ba, wr_blk, br, wi_blk, bi, lam, wb, *, name):
    tp, din = u.shape
    dl = din // 6
    tt = MIX_ROWS
    cw = GATE_BLOCK
    nch = dl // cw
    assert tp % tt == 0 and dl % cw == 0

    def body(u_ref, wa_ref, ba_ref, wr_ref, br_ref, wi_ref, bi_ref, lam_ref, wb_ref,
             ca_ref, hs_ref, y_ref, xa_tail, v_tail, h_carry, a_s, b_s, h_s):
        @pl.when(pl.program_id(0) == 0)
        def _():
            xa_tail[...] = jnp.zeros_like(xa_tail)
            v_tail[...] = jnp.zeros_like(v_tail)
            h_carry[...] = jnp.zeros_like(h_carry)

        for ch in range(nch):
            cs = slice(ch * cw, (ch + 1) * cw)

            def seg(s):
                return slice(s * dl + ch * cw, s * dl + (ch + 1) * cw)

            xa = u_ref[:, seg(0)]
            halo = xa_tail[:, cs]
            ca = ba_ref[:, cs] + wa_ref[3:4, cs] * xa
            for kk in range(3):
                ca = ca + wa_ref[kk:kk + 1, cs] * _shift_down(halo, xa, 3 - kk)
            xa_tail[:, cs] = xa[tt - SUBLANES:]
            ca_ref[:, cs] = ca
            sp = _softplus(-lam_ref[:, cs])
            _, ig, a, mult = _gates(ca, wr_ref[ch], wi_ref[ch], br_ref[:, cs], bi_ref[:, cs], sp)
            _to_lane_blocks(a_s, cs, a)
            _to_lane_blocks(b_s, cs, mult * (ig * ca))

            bv = u_ref[:, seg(2)]
            v = u_ref[:, seg(3)] * u_ref[:, seg(4)]
            gb = u_ref[:, seg(5)]
            vh = v_tail[:, cs]
            cb = wb_ref[2:3, cs] * v
            for kk in range(2):
                cb = cb + wb_ref[kk:kk + 1, cs] * _shift_down(vh, v, 2 - kk)
            v_tail[:, cs] = v[tt - SUBLANES:]
            y_ref[:, dl + ch * cw: dl + (ch + 1) * cw] = (bv * cb * (gb * _sigmoid(gb))).astype(BF16)

        for ch in range(nch):
            cs = slice(ch * cw, (ch + 1) * cw)
            for j in range(cs.start // LANES, cs.stop // LANES):
                lanes = slice(j * LANES, (j + 1) * LANES)
                h_carry[:, lanes] = _scan_tile(a_s, b_s, h_s, h_carry[:, lanes], j, reverse=False)
            hsv = _from_lane_blocks(h_s, cs)
            hs_ref[:, cs] = hsv
            ga = u_ref[:, dl + ch * cw: dl + (ch + 1) * cw]
            y_ref[:, cs] = (hsv * (ga * _sigmoid(ga))).astype(BF16)

    row = lambda w: pl.BlockSpec((tt, w), lambda i: (i, 0))
    full = lambda shp: pl.BlockSpec(shp, lambda i: tuple(0 for _ in shp))
    return pl.pallas_call(
        body, name=name, grid=(tp // tt,),
        in_specs=[row(din), full((4, dl)), full((1, dl)), full((nch, cw, cw)), full((1, dl)),
                  full((nch, cw, cw)), full((1, dl)), full((1, dl)), full((3, dl))],
        out_specs=[row(dl), row(dl), row(2 * dl)],
        out_shape=[jax.ShapeDtypeStruct((tp, dl), F32), jax.ShapeDtypeStruct((tp, dl), F32),
                   jax.ShapeDtypeStruct((tp, 2 * dl), BF16)],
        scratch_shapes=[pltpu.VMEM((SUBLANES, dl), F32), pltpu.VMEM((SUBLANES, dl), F32),
                        pltpu.VMEM((SUBLANES, dl), F32)] + [pltpu.VMEM((dl // LANES, tt, LANES), F32)] * 3,
        compiler_params=_params(("arbitrary",)),
    )(u, wa, ba, wr_blk, br, wi_blk, bi, lam, wb)


SG_WA, SG_BA, SG_BR, SG_BI, SG_LAM, SG_WB, SG_ROWS = 0, 4, 5, 6, 7, 8, 16


def _mixer_bwd(u, ca, hs, dy, wa, wr_blk, br, wi_blk, bi, lam, wb, *, name):
    tp, din = u.shape
    dl = din // 6
    tt = MIX_ROWS
    cw = GATE_BLOCK
    nch = dl // cw
    nt = tp // tt
    hb = tt // SUBLANES
    tn_dims = (((0,), (0,)), ((), ()))
    nt_dims = (((1,), (1,)), ((), ()))

    def body(u_ref, uh_ref, ca_ref, hs_ref, hsh_ref, dy_ref, wa_ref, wr_ref, br_ref, wi_ref, bi_ref, lam_ref, wb_ref,
             du_ref, sg_ref, dwr_ref, dwi_ref,
             g_carry, a_head, dca_head, dcb_head, r_s, i_s, a_s, an_s, d_s, g_s):
        i = pl.program_id(0)
        first_tile = i == nt - 1

        @pl.when(i == 0)
        def _():
            for ref in (g_carry, a_head, dca_head, dcb_head, sg_ref, dwr_ref, dwi_ref):
                ref[...] = jnp.zeros_like(ref)

        def halo_of(x):
            return jnp.where(first_tile, 0.0, x)

        for ch in range(nch):
            cs = slice(ch * cw, (ch + 1) * cw)
            cav = ca_ref[:, cs]
            sp = _softplus(-lam_ref[:, cs])
            r, ig, a, _ = _gates(cav, wr_ref[ch], wi_ref[ch], br_ref[:, cs], bi_ref[:, cs], sp)
            r_s[:, cs] = r
            i_s[:, cs] = ig
            a_s[:, cs] = a
            _to_lane_blocks(an_s, cs, _shift_up(a, a_head[:, cs], 1))
            a_head[:, cs] = a[:SUBLANES]
            ga = u_ref[:, dl + ch * cw: dl + (ch + 1) * cw]
            _to_lane_blocks(d_s, cs, dy_ref[:, cs] * (ga * _sigmoid(ga)))

        for j in range(dl // LANES):
            lanes = slice(j * LANES, (j + 1) * LANES)
            g_carry[:, lanes] = _scan_tile(an_s, d_s, g_s, g_carry[:, lanes], j, reverse=True)

        def acc_row(r0, val):
            sg_ref[r0:r0 + 1, cs_cur[0]] += jnp.sum(val, axis=0, keepdims=True)

        cs_cur = [None]
        for ch in range(nch):
            cs = slice(ch * cw, (ch + 1) * cw)
            cs_cur[0] = cs

            def seg(s):
                return slice(s * dl + ch * cw, s * dl + (ch + 1) * cw)

            cav = ca_ref[:, cs]
            r = r_s[:, cs]
            ig = i_s[:, cs]
            a = a_s[:, cs]
            g = _from_lane_blocks(g_s, cs)
            hsv = hs_ref[:, cs]
            lamv = lam_ref[:, cs]
            sp = _softplus(-lamv)
            la = -LRU_C * r * sp
            e2 = a * a
            one_m_e2 = -jnp.tanh(la) * (e2 + 1.0)
            mult = jnp.sqrt(one_m_e2)
            hprev = _shift_down(halo_of(hsh_ref[:, cs]), hsv, 1)
            icav = ig * cav
            dla = g * (hprev * a - icav * (e2 * lax.rsqrt(one_m_e2)))
            gm = g * mult
            dzi = gm * icav * (1.0 - ig)
            dca = gm * ig
            dla_r = dla * r
            dzr = dla_r * (1.0 - r) * (-LRU_C * sp)
            sg_ref[SG_LAM:SG_LAM + 1, cs] += jnp.sum(dla_r, axis=0, keepdims=True) * (LRU_C * _sigmoid(-lamv))
            acc_row(SG_BR, dzr)
            acc_row(SG_BI, dzi)
            dzr_b = dzr.astype(BF16)
            dzi_b = dzi.astype(BF16)
            cab = cav.astype(BF16)
            dca = dca + lax.dot_general(dzr_b, wr_ref[ch], nt_dims, preferred_element_type=F32)
            dca = dca + lax.dot_general(dzi_b, wi_ref[ch], nt_dims, preferred_element_type=F32)
            dwr_ref[ch] += lax.dot_general(cab, dzr_b, tn_dims, preferred_element_type=F32)
            dwi_ref[ch] += lax.dot_general(cab, dzi_b, tn_dims, preferred_element_type=F32)
            acc_row(SG_BA, dca)
            xa = u_ref[:, seg(0)]
            head = dca_head[:, cs]
            dxa = wa_ref[3:4, cs] * dca
            acc_row(SG_WA + 3, dca * xa)
            for kk in range(3):
                later = _shift_up(dca, head, 3 - kk)
                acc_row(SG_WA + kk, later * xa)
                dxa = dxa + wa_ref[kk:kk + 1, cs] * later
            dca_head[:, cs] = dca[:SUBLANES]
            ga = u_ref[:, seg(1)]
            sga = _sigmoid(ga)
            dga = dy_ref[:, cs] * hsv * (sga + (ga * sga) * (1.0 - sga))
            du_ref[:, seg(0)] = dxa.astype(BF16)
            du_ref[:, seg(1)] = dga.astype(BF16)

            bv = u_ref[:, seg(2)]
            cv = u_ref[:, seg(3)]
            xb = u_ref[:, seg(4)]
            gb = u_ref[:, seg(5)]
            dyb = dy_ref[:, dl + ch * cw: dl + (ch + 1) * cw]
            v = cv * xb
            vh = halo_of(uh_ref[:, seg(3)] * uh_ref[:, seg(4)])
            v1 = _shift_down(vh, v, 1)
            v2 = _shift_down(vh, v, 2)
            cb = wb_ref[2:3, cs] * v + wb_ref[1:2, cs] * v1 + wb_ref[0:1, cs] * v2
            sgb = _sigmoid(gb)
            sl = gb * sgb
            dyb_b = dyb * bv
            dyb_cb = dyb * cb
            dcb = dyb_b * sl
            du_ref[:, seg(2)] = (dyb_cb * sl).astype(BF16)
            du_ref[:, seg(5)] = (dyb_cb * bv * (sgb + sl * (1.0 - sgb))).astype(BF16)
            bhead = dcb_head[:, cs]
            dv = wb_ref[2:3, cs] * dcb
            acc_row(SG_WB + 2, dcb * v)
            for kk in range(2):
                later = _shift_up(dcb, bhead, 2 - kk)
                acc_row(SG_WB + kk, later * v)
                dv = dv + wb_ref[kk:kk + 1, cs] * later
            dcb_head[:, cs] = dcb[:SUBLANES]
            du_ref[:, seg(3)] = (dv * xb).astype(BF16)
            du_ref[:, seg(4)] = (dv * cv).astype(BF16)

    rev = lambda w: pl.BlockSpec((tt, w), lambda i: (nt - 1 - i, 0))
    halo = lambda w: pl.BlockSpec((SUBLANES, w), lambda i: (jnp.maximum((nt - 1 - i) * hb - 1, 0), 0))
    full = lambda shp: pl.BlockSpec(shp, lambda i: tuple(0 for _ in shp))
    vm = lambda r: pltpu.VMEM((r, dl), F32)
    return pl.pallas_call(
        body, name=name, grid=(nt,),
        in_specs=[rev(din), halo(din), rev(dl), rev(dl), halo(dl), rev(2 * dl), full((4, dl)),
                  full((nch, cw, cw)), full((1, dl)), full((nch, cw, cw)), full((1, dl)), full((1, dl)), full((3, dl))],
        out_specs=[rev(din), full((SG_ROWS, dl)), full((nch, cw, cw)), full((nch, cw, cw))],
        out_shape=[jax.ShapeDtypeStruct((tp, din), BF16), jax.ShapeDtypeStruct((SG_ROWS, dl), F32),
                   jax.ShapeDtypeStruct((nch, cw, cw), F32), jax.ShapeDtypeStruct((nch, cw, cw), F32)],
        scratch_shapes=[vm(SUBLANES), vm(SUBLANES), vm(SUBLANES), vm(SUBLANES), vm(tt), vm(tt), vm(tt)]
        + [pltpu.VMEM((dl // LANES, tt, LANES), F32)] * 3,
        compiler_params=_params(("arbitrary",)),
    )(u, u, ca, hs, hs, dy, wa, wr_blk, br, wi_blk, bi, lam, wb)


def _adamw(w, g, m, v, *, name, landed=None, layer=None, depth=None, into=None, row_off=0):
    r, c = w.shape[-2:]
    rows = g.shape[0]
    tr = _tile(rows, 256, 2 * SUBLANES)
    assert row_off % tr == 0
    boff = row_off // tr
    bc1 = 1.0 - ADAM_B1 ** ADAM_STEP
    bc2 = 1.0 - ADAM_B2 ** ADAM_STEP
    slots = landed is not None

    def body(*refs):
        if into is not None:
            refs = refs[:-8] + refs[-4:]
        if slots:
            w_ref, g_ref, l_ref, m_ref, v_ref, grad_ref, delta_ref, nm_ref, nv_ref = refs
            gv = g_ref[...].astype(F32)
            for s in range(N_DEV - 1):
                gv = gv + l_ref[s].astype(F32)
        else:
            w_ref, g_ref, m_ref, v_ref, grad_ref, delta_ref, nm_ref, nv_ref = refs
            gv = g_ref[...]
        wv = w_ref[...]
        mn = ADAM_B1 * m_ref[...] + (1.0 - ADAM_B1) * gv
        vn = ADAM_B2 * v_ref[...] + (1.0 - ADAM_B2) * (gv * gv)
        m_hat = mn / bc1
        v_hat = vn / bc2
        grad_ref[...] = gv
        delta_ref[...] = -ADAM_LR * (m_hat / (jnp.sqrt(v_hat) + ADAM_EPS) + ADAM_WD * wv)
        nm_ref[...] = mn
        nv_ref[...] = vn

    if depth is None:
        blk = pl.BlockSpec((tr, c), lambda i: (i + boff, 0))
    else:
        blk = pl.BlockSpec((None, tr, c), lambda i: (layer, i + boff, 0))
    g_blk = pl.BlockSpec((tr, c), lambda i: (i, 0))
    l_spec = [pl.BlockSpec((N_DEV - 1, tr, c), lambda i: (0, i, 0))] if slots else []
    args = (w, g, landed, m, v) if slots else (w, g, m, v)
    in_specs = [blk, g_blk] + l_spec + [blk, blk]
    if depth is None:
        shp = jax.ShapeDtypeStruct((r, c), F32)
        out_blk = blk
    else:
        shp = jax.ShapeDtypeStruct((depth, r, c), F32)
        out_blk = pl.BlockSpec((None, tr, c), lambda i: (layer, i + boff, 0))
    aliases = {}
    if into is not None:
        aliases = {len(args) + j: j for j in range(4)}
        in_specs = in_specs + [ANY] * 4
        args = args + tuple(into)
    return pl.pallas_call(
        body, name=name, grid=(rows // tr,),
        in_specs=in_specs, out_specs=[out_blk] * 4,
        out_shape=[shp] * 4, input_output_aliases=aliases,
        compiler_params=_params(("parallel",)),
    )(*args)


def _slot_sum(g, *, name):
    _, r, c = g.shape
    tr = _tile(r, 512, SUBLANES)

    def body(g_ref, o_ref):
        gv = g_ref[0].astype(F32)
        for s in range(1, N_DEV):
            gv = gv + g_ref[s].astype(F32)
        o_ref[...] = gv

    return pl.pallas_call(
        body, name=name, grid=(r // tr,),
        in_specs=[pl.BlockSpec((N_DEV, tr, c), lambda i: (0, i, 0))],
        out_specs=pl.BlockSpec((tr, c), lambda i: (i, 0)),
        out_shape=jax.ShapeDtypeStruct((r, c), F32),
        compiler_params=_params(("parallel",)),
    )(g)


def _mesh_pos():
    x, y, c = lax.axis_index("x"), lax.axis_index("y"), lax.axis_index("c")
    return x, y, c, 4 * x + 2 * y + c


ANY = pl.BlockSpec(memory_space=pl.ANY)


GATHER_COPIES = 9


def _all_gather(srcs, out_shapes, views, *, name):
    n = len(srcs)
    SIB, X_OWN, Y_OWN, X_DIAG, Y_DIAG, SIB_X, SIB_Y, SIB_DIAG_TOP, SIB_DIAG_BOTTOM = range(GATHER_COPIES)

    def body(*refs):
        src = refs[:n]
        dst = refs[n:2 * n]
        send_sems, recv_sems, local_sems = refs[2 * n:]
        x, y, c, me = _mesh_pos()
        sibling, x_nbr, y_nbr = (x, y, 1 - c), (1 - x, y, c), (x, 1 - y, c)

        def block(a, px, py, pc, half=None):
            win = views[a](dst[a], 4 * px + 2 * py + pc)
            if half is None:
                return win
            rows = win.shape[0] // 2
            return win.at[pl.ds(half * rows, rows)]

        def copy(a, k, win, to, from_src=False):
            return pltpu.make_async_remote_copy(
                src_ref=src[a] if from_src else win, dst_ref=win,
                send_sem=send_sems.at[a * GATHER_COPIES + k], recv_sem=recv_sems.at[a * GATHER_COPIES + k],
                device_id=to, device_id_type=MESH)

        mine = [pltpu.make_async_copy(src[a], block(a, x, y, c), local_sems.at[a]) for a in range(n)]
        started = []

        def start(cp):
            cp.start()
            started.append(cp)

        for a in range(n):
            mine[a].start()
            own = block(a, x, y, c)
            start(copy(a, SIB, own, sibling, True))
            start(copy(a, X_OWN, own, x_nbr, True))
            start(copy(a, Y_OWN, own, y_nbr, True))
        for a in range(n):
            from_y = block(a, x, 1 - y, c)
            copy(a, Y_OWN, from_y, y_nbr).wait_recv()
            start(copy(a, X_DIAG, block(a, x, 1 - y, c, 0), x_nbr))
            start(copy(a, SIB_Y, from_y, sibling))
            from_x = block(a, 1 - x, y, c)
            copy(a, X_OWN, from_x, x_nbr).wait_recv()
            start(copy(a, Y_DIAG, block(a, 1 - x, y, c, 1), y_nbr))
            start(copy(a, SIB_X, from_x, sibling))
        for a in range(n):
            top = block(a, 1 - x, 1 - y, c, 0)
            copy(a, X_DIAG, top, x_nbr).wait_recv()
            start(copy(a, SIB_DIAG_TOP, top, sibling))
            bottom = block(a, 1 - x, 1 - y, c, 1)
            copy(a, Y_DIAG, bottom, y_nbr).wait_recv()
            start(copy(a, SIB_DIAG_BOTTOM, bottom, sibling))
        for a in range(n):
            copy(a, SIB, block(a, x, y, 1 - c), sibling).wait_recv()
            copy(a, SIB_X, block(a, 1 - x, y, 1 - c), sibling).wait_recv()
            copy(a, SIB_Y, block(a, x, 1 - y, 1 - c), sibling).wait_recv()
            copy(a, SIB_DIAG_TOP, block(a, 1 - x, 1 - y, 1 - c, 0), sibling).wait_recv()
            copy(a, SIB_DIAG_BOTTOM, block(a, 1 - x, 1 - y, 1 - c, 1), sibling).wait_recv()
        for cp in started:
            cp.wait_send()
        for cp in mine:
            cp.wait()

    return pl.pallas_call(
        body, name=name,
        in_specs=[ANY] * n, out_specs=[ANY] * n,
        out_shape=[jax.ShapeDtypeStruct(s, x.dtype) for s, x in zip(out_shapes, srcs)],
        scratch_shapes=[pltpu.SemaphoreType.DMA((GATHER_COPIES * n,)), pltpu.SemaphoreType.DMA((GATHER_COPIES * n,)),
                        pltpu.SemaphoreType.DMA((n,))],
    )(*srcs)


HBM = pl.BlockSpec(memory_space=pltpu.HBM)
SEM = pl.BlockSpec(memory_space=pltpu.SEMAPHORE)
EFFECT = pltpu.SideEffectType.DATAFLOW_SIDE_EFFECTING


def _peer_of(x, y, c, k):
    return (1 - x if k & 4 else x, 1 - y if k & 2 else y, 1 - c if k & 1 else c)


def _peer_copies(n, wins, src, land, send_sems, recv_sems):
    x, y, c, me = _mesh_pos()
    out = []
    for a in range(n):
        for k in range(1, N_DEV):
            px, py, pc = _peer_of(x, y, c, k)
            s_win, d_win = wins[a](src[a], land[a], me, 4 * px + 2 * py + pc, k)
            out.append(pltpu.make_async_remote_copy(
                src_ref=s_win, dst_ref=d_win,
                send_sem=send_sems.at[a * 7 + k - 1], recv_sem=recv_sems.at[a * 7 + k - 1],
                device_id=(px, py, pc), device_id_type=MESH))
    return out


def _push_start(srcs, lands, wins, *, name):
    n = len(srcs)

    def body(*refs):
        src = refs[:n]
        land = refs[n:2 * n]
        send_sems, recv_sems = refs[2 * n], refs[2 * n + 1]
        token = refs[-1]
        for cp in _peer_copies(n, wins, src, land, send_sems, recv_sems):
            cp.start()
        token[...] = jnp.zeros_like(token)

    bufs = (*srcs, *lands)
    return pl.pallas_call(
        body, name=name,
        out_shape=(pltpu.SemaphoreType.DMA((7 * n,)), pltpu.SemaphoreType.DMA((7 * n,)),
                   *[pltpu.HBM(v.shape, v.dtype) for v in bufs], jax.ShapeDtypeStruct((SUBLANES, LANES), F32)),
        in_specs=[HBM] * (2 * n),
        out_specs=(SEM, SEM, *[HBM] * (2 * n), pl.BlockSpec(memory_space=pltpu.VMEM)),
        input_output_aliases={i: 2 + i for i in range(2 * n)},
        compiler_params=pltpu.CompilerParams(has_side_effects=EFFECT),
    )(*[pltpu.with_memory_space_constraint(v, pltpu.HBM) for v in bufs])


def _push_wait(handle, wins, after, *, name):
    send_sems, recv_sems, *bufs, _ = handle
    n = len(bufs) // 2

    def body(*refs):
        src = refs[:n]
        land = refs[n:2 * n]
        for cp in _peer_copies(n, wins, src, land, refs[2 * n], refs[2 * n + 1]):
            cp.wait_send()
            cp.wait_recv()

    outs = pl.pallas_call(
        body, name=name,
        out_shape=tuple(pltpu.HBM(v.shape, v.dtype) for v in bufs),
        in_specs=[HBM] * (2 * n) + [SEM, SEM, ANY],
        out_specs=tuple([HBM] * (2 * n)),
        input_output_aliases={i: i for i in range(2 * n)},
        compiler_params=pltpu.CompilerParams(has_side_effects=EFFECT),
    )(*bufs, send_sems, recv_sems, after)
    return outs[:n], outs[n:]


def _gather_lead(src, land, me, peer, k):
    return src, land.at[me]


def _gather_cols(width):
    def win(src, land, me, peer, k):
        return src, land.at[:, pl.ds(me * width, width)]
    return win


def _scatter_lead(src, land, me, peer, k):
    return src.at[peer], land.at[k - 1]


def _scatter_cols(width):
    def win(src, land, me, peer, k):
        return src.at[:, pl.ds(peer * width, width)], land.at[k - 1]
    return win


def _place_block(own, *, cols, name):
    rows, width = own.shape
    tr = _tile(rows, 512, 2 * SUBLANES)
    _, _, _, me = _mesh_pos()

    def body(me_ref, x_ref, o_ref):
        o_ref[...] = x_ref[...]

    if cols:
        out_spec = pl.BlockSpec((tr, width), lambda i, me_ref: (i, me_ref[0]))
        shape = (rows, N_DEV * width)
    else:
        out_spec = pl.BlockSpec((None, tr, width), lambda i, me_ref: (me_ref[0], i, 0))
        shape = (N_DEV, rows, width)
    return pl.pallas_call(
        body, name=name,
        grid_spec=pltpu.PrefetchScalarGridSpec(
            num_scalar_prefetch=1, grid=(rows // tr,),
            in_specs=[pl.BlockSpec((tr, width), lambda i, me_ref: (i, 0))], out_specs=out_spec),
        out_shape=jax.ShapeDtypeStruct(shape, own.dtype),
        compiler_params=_params(("arbitrary",)),
    )(me.astype(jnp.int32).reshape(1), own)


def _dep(x, token):
    return x + token[0, 0].astype(x.dtype)


def _lead(ref, d):
    return ref.at[d]


def _col_window(width):
    def view(ref, d):
        return ref.at[:, pl.ds(d * width, width)]
    return view


def _pack(arrs):
    flat = jnp.concatenate([a.reshape(-1).astype(F32) for a in arrs])
    n = flat.shape[0]
    rows = -(-n // (2 * SUBLANES * LANES)) * 2 * SUBLANES
    return jnp.pad(flat, (0, rows * LANES - n)).reshape(rows, LANES)


def _unpack(buf, shapes):
    flat = buf.reshape(-1)
    out, off = [], 0
    for s in shapes:
        n = 1
        for q in s:
            n *= q
        out.append(flat[off:off + n].reshape(s))
        off += n
    return out


def _blockdiag(w, cw):
    h, hd, _ = w.shape
    per = cw // hd
    wg = w.reshape(h // per, per, hd, hd)
    eye = jnp.eye(per, dtype=w.dtype)
    blk = jnp.einsum("gpij,pq->gpiqj", wg, eye)
    return blk.reshape(h // per, cw, cw).astype(BF16)


def _blockdiag_extract(g, hd):
    n, cw, _ = g.shape
    per = cw // hd
    g5 = g.reshape(n, per, hd, per, hd)
    idx = jnp.arange(per)
    return g5[:, idx, :, idx, :].transpose(1, 0, 2, 3).reshape(n * per, hd, hd)


def kernel(x, meta, norm_g, w_in, conv_a_w, conv_a_b, lru_wr, lru_br, lru_wi, lru_bi, lru_lambda, conv_b_w, w_out, final_g, loss_target, m_meta, m_norm_g, m_w_in, m_conv_a_w, m_conv_a_b, m_lru_wr, m_lru_br, m_lru_wi, m_lru_bi, m_lru_lambda, m_conv_b_w, m_w_out, m_final_g, v_meta, v_norm_g, v_w_in, v_conv_a_w, v_conv_a_b, v_lru_wr, v_lru_br, v_lru_wi, v_lru_bi, v_lru_lambda, v_conv_b_w, v_w_out, v_final_g):
    _, seq, d = x.shape
    n_meta = meta.shape[0]
    depth = w_in.shape[0]
    din = w_in.shape[2] * N_DEV
    dl = din // 6
    dmix = 2 * dl
    wcol = w_in.shape[2]
    wrow = w_out.shape[1]
    mcol = meta.shape[1]
    ccol = conv_a_w.shape[2]
    heads, hd = lru_wr.shape[1], lru_wr.shape[2]
    n_tok = n_meta + seq
    tp = -(-n_tok // TOKEN_TILE) * TOKEN_TILE
    me = 4 * lax.axis_index("x") + 2 * lax.axis_index("y") + lax.axis_index("c")

    bf = lambda a: a.astype(BF16)
    small_mine = _pack([meta, conv_a_w, conv_b_w])
    first = _all_gather([bf(w_in[0]), small_mine], [(d, din), (N_DEV,) + small_mine.shape],
                        [_col_window(wcol), _lead], name="gather_first")
    parts = [_unpack(first[1][s], [meta.shape, conv_a_w.shape, conv_b_w.shape]) for s in range(N_DEV)]
    meta_full = jnp.concatenate([p[0] for p in parts], axis=1)
    wa_full = jnp.concatenate([p[1] for p in parts], axis=2)
    wb_full = jnp.concatenate([p[2] for p in parts], axis=2)
    w_in_full = [None] * depth
    w_out_full = [None] * depth

    push_out = [None] * depth
    push_in = [None] * depth
    w_in_full[0], src = lax.optimization_barrier((first[0], bf(w_out[0])))
    push_out[0] = _push_start([src], [_place_block(src, cols=False, name="place_wout_0")], [_gather_lead],
                              name="gather_wout_0_start")
    token = push_out[0][-1]
    for l in range(1, depth):
        src = bf(_dep(w_in[l], token))
        push_in[l] = _push_start([src], [_place_block(src, cols=True, name=f"place_win_{l}")], [_gather_cols(wcol)],
                                 name=f"gather_win_{l}_start")
        src = bf(_dep(w_out[l], push_in[l][-1]))
        push_out[l] = _push_start([src], [_place_block(src, cols=False, name=f"place_wout_{l}")], [_gather_lead],
                                  name=f"gather_wout_{l}_start")
        token = push_out[l][-1]

    wr_blk = [_blockdiag(lru_wr[l], GATE_BLOCK) for l in range(depth)]
    wi_blk = [_blockdiag(lru_wi[l], GATE_BLOCK) for l in range(depth)]
    vec = lambda a: a.reshape(1, dl)

    tm = _tile(tp, 1408)
    saved = []
    for l in range(depth):
        if l == 0:
            h, hn = _rms_fwd_first(x[0], meta_full, _dep(norm_g[l], token), tp=tp, name=f"rms_fwd_{l}")
        else:
            hn = _rms_fwd(h, norm_g[l], name=f"rms_fwd_{l}")
        if l > 0:
            _, landed = _push_wait(push_in[l], [_gather_cols(wcol)], hn, name=f"gather_win_{l}_wait")
            w_in_full[l] = landed[0]
        u = _matmul(hn, w_in_full[l], tm=tm, tn=_tile(din, 768), tk=d, name=f"mm_u_{l}")
        ca, hs, y = _mixer_fwd(u, wa_full[l], vec(conv_a_b[l]), wr_blk[l], vec(lru_br[l]), wi_blk[l], vec(lru_bi[l]),
                               vec(lru_lambda[l]), wb_full[l], name=f"mixer_fwd_{l}")
        _, landed = _push_wait(push_out[l], [_gather_lead], y, name=f"gather_wout_{l}_wait")
        w_out_full[l] = landed[0].reshape(dmix, d)
        h_next = _matmul(y, w_out_full[l], tm=tm, tn=_tile(d, 512), tk=dmix, add=h, name=f"mm_out_{l}")
        saved.append((h, hn, u, ca, hs, y))
        h = h_next

    dh, dhb, dg_final, loss_part = _loss_head(h, loss_target[0], final_g, n_meta=n_meta, n_tok=n_tok,
                                              name="loss_head")

    small_grads = [None] * depth
    sent_out = [None] * depth
    sent_in = [None] * depth
    scatter_in = [_scatter_cols(wcol)]
    token = None
    dg_norms = []
    for l in reversed(range(depth)):
        h_in, hn, u, ca, hs, y = saved[l]
        dy = _matmul(dhb, w_out_full[l], tb=True, tm=tm, tn=_tile(dmix, 512), tk=d, dep=token, name=f"mm_dy_{l}")
        dw_out = _matmul(y, dhb, ta=True, tm=_tile(dmix, 512), tn=_tile(d, 1024), tk=tp, out_dtype=BF16,
                         name=f"mm_dwout_{l}")
        sent_out[l] = _push_start([dw_out.reshape(N_DEV, wrow, d)], [lax.empty((N_DEV - 1, wrow, d), BF16)],
                                  [_scatter_lead], name=f"scatter_wout_{l}_start")
        du, sg, dwr, dwi = _mixer_bwd(u, ca, hs, dy, wa_full[l], wr_blk[l], vec(lru_br[l]), wi_blk[l], vec(lru_bi[l]),
                                      vec(lru_lambda[l]), _dep(wb_full[l], sent_out[l][-1]), name=f"mixer_bwd_{l}")
        small_grads[l] = (sg, dwr, dwi)
        if l == 0:
            early = [_pack([
                jnp.stack([small_grads[j][0][SG_BA] for j in range(depth)]),
                jnp.stack([small_grads[j][0][SG_BR] for j in range(depth)]),
                jnp.stack([small_grads[j][0][SG_BI] for j in range(depth)]),
                jnp.stack([small_grads[j][0][SG_LAM] for j in range(depth)]),
                jnp.stack([small_grads[j][0][SG_WA:SG_WA + 4] for j in range(depth)]),
                jnp.stack([small_grads[j][0][SG_WB:SG_WB + 3] for j in range(depth)]),
                dg_final[0], *dg_norms]),
                _pack([jnp.stack([_blockdiag_extract(small_grads[j][1], hd) for j in range(depth)]),
                       jnp.stack([_blockdiag_extract(small_grads[j][2], hd) for j in range(depth)])]).astype(BF16)]
            early_land = [lax.dynamic_update_slice(lax.empty((N_DEV,) + a.shape, a.dtype), a[None], (me, 0, 0))
                          for a in early]
            sent_early = _push_start(early, early_land, [_gather_lead] * 2, name="gather_early_grads_start")
        parts = 2 if l == 0 else 1
        token = sent_early[-1] if l == 0 else None
        sent_in[l] = []
        for p in range(parts):
            dw_in = _matmul(hn, du, ta=True, tm=_tile(d // parts, 512), tn=_tile(din, 768), tk=tp, out_dtype=BF16,
                            dep=token, m_part=(p, parts), name=f"mm_dwin_{l}_{p}")
            sent_in[l].append(_push_start([dw_in], [lax.empty((N_DEV - 1, d // parts, wcol), BF16)], scatter_in,
                                          name=f"scatter_win_{l}_{p}_start"))
            token = sent_in[l][-1][-1]
        dhn = _matmul(du, w_in_full[l], tb=True, tm=_tile(tp, 704, 2 * SUBLANES), tn=_tile(d, 512), tk=din, dep=token,
                      name=f"mm_dhn_{l}")
        if l > 0:
            dh, dhb, dg_norm = _rms_bwd(h_in, dhn, dh, norm_g[l], name=f"rms_bwd_{l}")
            dg_norms.append(dg_norm[0])
        else:
            grad_x, d_meta, dg_norm = _rms_bwd_first(h_in, dhn, dh, norm_g[l], n_meta=n_meta, seq=seq,
                                                     name=f"rms_bwd_{l}")

    late = _pack([dg_norm[0], d_meta, loss_part[0:1, 0:1]])
    late_all = _all_gather([late], [(N_DEV,) + late.shape], [_lead], name="gather_late_grads")[0]
    late_sum = _unpack(_slot_sum(late_all, name="sum_late_grads"), [(d,), (n_meta, d), ()])
    loss = late_sum[2]
    _, early_all = _push_wait(sent_early, [_gather_lead] * 2, late_sum[0], name="gather_early_grads_wait")
    vec_shapes = [conv_a_b.shape, lru_br.shape, lru_bi.shape, lru_lambda.shape, (depth, 4, dl), (depth, 3, dl),
                  final_g.shape] + [(d,)] * (depth - 1)
    e = _unpack(_slot_sum(early_all[0], name="sum_early_vectors"), vec_shapes)
    g_wr, g_wi = _unpack(_slot_sum(early_all[1], name="sum_early_maps"), [lru_wr.shape, lru_wi.shape])
    g_norm = jnp.stack([late_sum[0]] + e[7:][::-1])
    g_meta = lax.dynamic_slice_in_dim(late_sum[1], me * mcol, mcol, axis=1)
    g_wa = lax.dynamic_slice_in_dim(e[4], me * ccol, ccol, axis=2)
    g_wb = lax.dynamic_slice_in_dim(e[5], me * ccol, ccol, axis=2)

    small_w = [norm_g, conv_a_b, lru_wr, lru_br, lru_wi, lru_bi, lru_lambda, final_g, meta, conv_a_w, conv_b_w]
    small_m = [m_norm_g, m_conv_a_b, m_lru_wr, m_lru_br, m_lru_wi, m_lru_bi, m_lru_lambda, m_final_g, m_meta,
               m_conv_a_w, m_conv_b_w]
    small_v = [v_norm_g, v_conv_a_b, v_lru_wr, v_lru_br, v_lru_wi, v_lru_bi, v_lru_lambda, v_final_g, v_meta,
               v_conv_a_w, v_conv_b_w]
    small_g = [g_norm, e[0], g_wr, e[1], g_wi, e[2], e[3], e[6], g_meta, g_wa, g_wb]
    small_out = _adamw(_pack(small_w), _pack(small_g), _pack(small_m), _pack(small_v), name="adamw_small")
    small_shapes = [a.shape for a in small_w]
    s_grad, s_delta, s_m, s_v = [_unpack(o, small_shapes) for o in small_out]

    win_out = None
    wout_out = None
    after = small_out[0]
    for l in reversed(range(depth)):
        src, landed = _push_wait(sent_out[l], [_scatter_lead], after, name=f"scatter_wout_{l}_wait")
        own = lax.dynamic_index_in_dim(src[0], me, 0, keepdims=False)
        wout_out = _adamw(w_out, own, m_w_out, v_w_out, landed=landed[0], layer=l, depth=depth,
                          into=wout_out, name=f"adamw_w_out_{l}")
        after = wout_out[0]
        for p, sent in enumerate(sent_in[l]):
            src, landed = _push_wait(sent, scatter_in, after, name=f"scatter_win_{l}_{p}_wait")
            own = lax.dynamic_slice_in_dim(src[0], me * wcol, wcol, axis=1)
            win_out = _adamw(w_in, own, m_w_in, v_w_in, landed=landed[0], layer=l, depth=depth,
                             into=win_out, row_off=p * own.shape[0], name=f"adamw_w_in_{l}_{p}")
            after = win_out[0]

    names = ["norm_g", "conv_a_b", "lru_wr", "lru_br", "lru_wi", "lru_bi", "lru_lambda", "final_g", "meta",
             "conv_a_w", "conv_b_w"]
    order = ["meta", "norm_g", "w_in", "conv_a_w", "conv_a_b", "lru_wr", "lru_br", "lru_wi", "lru_bi", "lru_lambda",
             "conv_b_w", "w_out", "final_g"]

    def family(idx, small):
        table = {nm: small[i] for i, nm in enumerate(names)}
        table["w_in"] = win_out[idx]
        table["w_out"] = wout_out[idx]
        return [table[nm] for nm in order]

    return (loss, grad_x, *family(0, s_grad), *family(1, s_delta), *family(2, s_m), *family(3, s_v))
```

```python
import functools

import jax
import jax.numpy as jnp
from jax import lax
from jax.experimental import pallas as pl
from jax.experimental.pallas import tpu as pltpu

F32 = jnp.float32
BF16 = jnp.bfloat16
MESH = pl.DeviceIdType.MESH

N_DEV = 8
RMS_EPS = 1e-6
LRU_C = 8.0
ADAM_LR = 0.001
ADAM_B1 = 0.9
ADAM_B2 = 0.999
ADAM_EPS = 1e-08
ADAM_WD = 0.01
ADAM_STEP = 10

V7X_VMEM_LIMIT = 52 * 1024 * 1024
LANES = 128
SUBLANES = 8
TOKEN_TILE = 384
MIX_ROWS = 128
GATE_BLOCK = 128


def _params(sem):
    return pltpu.CompilerParams(dimension_semantics=sem, vmem_limit_bytes=V7X_VMEM_LIMIT)


def _tile(n, target, align=LANES):
    best = None
    for t in range(align, min(n, target) + 1, align):
        if n % t == 0:
            best = t
    return n if best is None else best


def _sigmoid(z):
    return 0.5 * jnp.tanh(0.5 * z) + 0.5


def _softplus(z):
    e = jnp.exp(-jnp.abs(z))
    u = 1.0 + e
    l1p = jnp.where(u == 1.0, e, jnp.log(u) * e / jnp.where(u == 1.0, 1.0, u - 1.0))
    return jnp.maximum(z, 0.0) + l1p


def _matmul(a, b, *, ta=False, tb=False, tm, tn, tk, out_dtype=F32, add=None, dep=None, m_part=None, name):
    m, k = (a.shape[1], a.shape[0]) if ta else a.shape
    m_off = 0
    if m_part is not None:
        assert add is None and m % (m_part[1] * tm) == 0
        m //= m_part[1]
        m_off = m_part[0] * (m // tm)
    n, kb = b.shape if tb else b.shape[::-1]
    assert kb == k
    assert m % tm == 0 and n % tn == 0 and k % tk == 0, (m, n, k, tm, tn, tk)
    nk = k // tk
    a_spec = pl.BlockSpec((tk, tm), lambda i, j, q: (q, i + m_off)) if ta \
        else pl.BlockSpec((tm, tk), lambda i, j, q: (i + m_off, q))
    b_spec = pl.BlockSpec((tn, tk), lambda i, j, q: (j, q)) if tb else pl.BlockSpec((tk, tn), lambda i, j, q: (q, j))
    o_spec = pl.BlockSpec((tm, tn), lambda i, j, q: (i, j))
    o_shape = (m, n)
    dims = (((0 if ta else 1,), (1 if tb else 0,)), ((), ()))
    has_add = add is not None
    has_dep = dep is not None

    def body(*refs):
        if has_dep:
            refs = refs[:-3] + refs[-2:]
        if has_add:
            a_ref, b_ref, add_ref, o_ref, acc_ref = refs
        else:
            a_ref, b_ref, o_ref, acc_ref = refs
        q = pl.program_id(2)
        part = lax.dot_general(a_ref[...], b_ref[...], dims, preferred_element_type=F32)

        def finish(acc):
            if has_add:
                acc = acc + add_ref[...]
            o_ref[...] = acc.astype(out_dtype)

        if nk == 1:
            finish(part)
        else:
            @pl.when(q == 0)
            def _():
                acc_ref[...] = part

            @pl.when(jnp.logical_and(q > 0, q < nk - 1))
            def _():
                acc_ref[...] += part

            @pl.when(q == nk - 1)
            def _():
                finish(acc_ref[...] + part)

    in_specs = [a_spec, b_spec] + ([o_spec] if has_add else [])
    args = (a, b) + ((add,) if has_add else ())
    if has_dep:
        in_specs.append(pl.BlockSpec((SUBLANES, LANES), lambda i, j, q: (0, 0)))
        args += (dep,)
    acc_shape = (tm, tn) if nk > 1 else (SUBLANES, LANES)
    return pl.pallas_call(
        body, name=name,
        grid=(m // tm, n // tn, nk),
        in_specs=in_specs, out_specs=o_spec,
        out_shape=jax.ShapeDtypeStruct(o_shape, out_dtype),
        scratch_shapes=[pltpu.VMEM(acc_shape, F32)],
        compiler_params=_params(("parallel", "parallel", "arbitrary")),
    )(*args)


def _rms_fwd(h, g, *, name):
    tp, d = h.shape
    tr = _tile(tp, 512, SUBLANES)

    def body(h_ref, g_ref, o_ref):
        hv = h_ref[...]
        rstd = lax.rsqrt(jnp.mean(hv * hv, axis=-1, keepdims=True) + RMS_EPS)
        o_ref[...] = (hv * rstd * g_ref[...]).astype(BF16)

    return pl.pallas_call(
        body, name=name, grid=(tp // tr,),
        in_specs=[pl.BlockSpec((tr, d), lambda i: (i, 0)), pl.BlockSpec((1, d), lambda i: (0, 0))],
        out_specs=pl.BlockSpec((tr, d), lambda i: (i, 0)),
        out_shape=jax.ShapeDtypeStruct((tp, d), BF16),
        compiler_params=_params(("parallel",)),
    )(h, g.reshape(1, d))


def _rms_fwd_first(x, meta, g, *, tp, name):
    seq, d = x.shape
    n_meta = meta.shape[0]
    n_tok = n_meta + seq
    tr = MIX_ROWS
    assert tp % tr == 0 and seq % tr == 0 and tr % n_meta == 0
    per = tr // n_meta

    def body(x_ref, xp_ref, m_ref, g_ref, h_ref, o_ref):
        i = pl.program_id(0)
        head = jnp.where(i == 0, m_ref[...], xp_ref[...])
        rows = i * tr + lax.broadcasted_iota(jnp.int32, (tr, 1), 0)
        hv = jnp.where(rows < n_tok, jnp.concatenate([head, x_ref[:tr - n_meta, :]], axis=0), 0.0)
        h_ref[...] = hv
        rstd = lax.rsqrt(jnp.mean(hv * hv, axis=-1, keepdims=True) + RMS_EPS)
        o_ref[...] = (hv * rstd * g_ref[...]).astype(BF16)

    row = pl.BlockSpec((tr, d), lambda i: (i, 0))
    own = pl.BlockSpec((tr, d), lambda i: (jnp.minimum(i, seq // tr - 1), 0))
    before = pl.BlockSpec((n_meta, d), lambda i: (jnp.maximum(i * per - 1, 0), 0))
    return pl.pallas_call(
        body, name=name, grid=(tp // tr,),
        in_specs=[own, before, pl.BlockSpec((n_meta, d), lambda i: (0, 0)), pl.BlockSpec((1, d), lambda i: (0, 0))],
        out_specs=[row, row],
        out_shape=[jax.ShapeDtypeStruct((tp, d), F32), jax.ShapeDtypeStruct((tp, d), BF16)],
        compiler_params=_params(("parallel",)),
    )(x, x, meta, g.reshape(1, d))


def _rms_bwd(h, dhn, dout, g, *, name):
    tp, d = h.shape
    tr = _tile(tp, 528, 2 * SUBLANES)

    def body(h_ref, dhn_ref, dout_ref, g_ref, dh_ref, dhb_ref, dg_ref):
        hv = h_ref[...]
        rstd = lax.rsqrt(jnp.mean(hv * hv, axis=-1, keepdims=True) + RMS_EPS)
        xhat = hv * rstd
        dn = dhn_ref[...]
        dxhat = dn * g_ref[...]
        dh = dout_ref[...] + rstd * (dxhat - xhat * jnp.mean(dxhat * xhat, axis=-1, keepdims=True))
        dh_ref[...] = dh
        dhb_ref[...] = dh.astype(BF16)
        part = jnp.sum(dn * xhat, axis=0, keepdims=True)

        @pl.when(pl.program_id(0) == 0)
        def _():
            dg_ref[...] = part

        @pl.when(pl.program_id(0) > 0)
        def _():
            dg_ref[...] += part

    row = pl.BlockSpec((tr, d), lambda i: (i, 0))
    vec = pl.BlockSpec((1, d), lambda i: (0, 0))
    return pl.pallas_call(
        body, name=name, grid=(tp // tr,),
        in_specs=[row, row, row, vec],
        out_specs=[row, row, vec],
        out_shape=[jax.ShapeDtypeStruct((tp, d), F32), jax.ShapeDtypeStruct((tp, d), BF16),
                   jax.ShapeDtypeStruct((1, d), F32)],
        compiler_params=_params(("arbitrary",)),
    )(h, dhn, dout, g.reshape(1, d))


def _rms_bwd_first(h, dhn, dout, g, *, n_meta, seq, name):
    tp, d = h.shape
    tr = MIX_ROWS
    assert seq % tr == 0 and tr % n_meta == 0 and tp >= seq + n_meta
    nt = seq // tr
    per = tr // n_meta

    def grads(hv, dn, do, gv):
        rstd = lax.rsqrt(jnp.mean(hv * hv, axis=-1, keepdims=True) + RMS_EPS)
        xhat = hv * rstd
        dxhat = dn * gv
        dh = do + rstd * (dxhat - xhat * jnp.mean(dxhat * xhat, axis=-1, keepdims=True))
        return dh, jnp.sum(dn * xhat, axis=0, keepdims=True)

    def body(h_ref, dhn_ref, dout_ref, hn_ref, dhnn_ref, doutn_ref, g_ref, gx_ref, dmeta_ref, dg_ref):
        i = pl.program_id(0)
        gv = g_ref[...]
        dh, part = grads(h_ref[...], dhn_ref[...], dout_ref[...], gv)
        dh_next, part_next = grads(hn_ref[...], dhnn_ref[...], doutn_ref[...], gv)
        gx_ref[...] = jnp.concatenate([dh[n_meta:], dh_next], axis=0)

        @pl.when(i == 0)
        def _():
            dmeta_ref[...] = dh[:n_meta]
            dg_ref[...] = part

        @pl.when(i > 0)
        def _():
            dg_ref[...] += part

        @pl.when(i == nt - 1)
        def _():
            dg_ref[...] += part_next

    row = pl.BlockSpec((tr, d), lambda i: (i, 0))
    nxt = pl.BlockSpec((n_meta, d), lambda i: ((i + 1) * per, 0))
    vec = pl.BlockSpec((1, d), lambda i: (0, 0))
    return pl.pallas_call(
        body, name=name, grid=(nt,),
        in_specs=[row, row, row, nxt, nxt, nxt, vec],
        out_specs=[pl.BlockSpec((None, tr, d), lambda i: (0, i, 0)), pl.BlockSpec((n_meta, d), lambda i: (0, 0)), vec],
        out_shape=[jax.ShapeDtypeStruct((1, seq, d), F32), jax.ShapeDtypeStruct((n_meta, d), F32),
                   jax.ShapeDtypeStruct((1, d), F32)],
        compiler_params=_params(("arbitrary",)),
    )(h, dhn, dout, h, dhn, dout, g.reshape(1, d))


def _loss_head(h, tgt, g, *, n_meta, n_tok, name):
    tp, d = h.shape
    seq = tgt.shape[0]
    tr = MIX_ROWS
    assert tp % tr == 0 and seq % tr == 0 and tr % n_meta == 0
    per = tr // n_meta

    def body(h_ref, t_ref, tp_ref, g_ref, dh_ref, dhb_ref, dg_ref, loss_ref):
        i = pl.program_id(0)
        hv = h_ref[...]
        rstd = lax.rsqrt(jnp.mean(hv * hv, axis=-1, keepdims=True) + RMS_EPS)
        xhat = hv * rstd
        gv = g_ref[...]
        rows = i * tr + lax.broadcasted_iota(jnp.int32, (tr, 1), 0)
        valid = jnp.logical_and(rows >= n_meta, rows < n_tok)
        target = jnp.concatenate([tp_ref[...], t_ref[:tr - n_meta, :]], axis=0)
        err = jnp.where(valid, xhat * gv - target, 0.0)
        dy = err * (1.0 / d)
        dxhat = dy * gv
        dh = rstd * (dxhat - xhat * jnp.mean(dxhat * xhat, axis=-1, keepdims=True))
        dh_ref[...] = dh
        dhb_ref[...] = dh.astype(BF16)
        dg_part = jnp.sum(dy * xhat, axis=0, keepdims=True)
        per_row = jnp.sum(err * err, axis=-1, keepdims=True) * (1.0 / d)
        loss_part = jnp.broadcast_to(0.5 * jnp.sum(per_row, axis=0, keepdims=True), (SUBLANES, LANES))

        @pl.when(i == 0)
        def _():
            dg_ref[...] = dg_part
            loss_ref[...] = loss_part

        @pl.when(i > 0)
        def _():
            dg_ref[...] += dg_part
            loss_ref[...] += loss_part

    row = pl.BlockSpec((tr, d), lambda i: (i, 0))
    vec = pl.BlockSpec((1, d), lambda i: (0, 0))
    own = pl.BlockSpec((tr, d), lambda i: (jnp.minimum(i, seq // tr - 1), 0))
    before = pl.BlockSpec((n_meta, d), lambda i: (jnp.maximum(i * per - 1, 0), 0))
    return pl.pallas_call(
        body, name=name, grid=(tp // tr,),
        in_specs=[row, own, before, vec],
        out_specs=[row, row, vec, pl.BlockSpec((SUBLANES, LANES), lambda i: (0, 0))],
        out_shape=[jax.ShapeDtypeStruct((tp, d), F32), jax.ShapeDtypeStruct((tp, d), BF16),
                   jax.ShapeDtypeStruct((1, d), F32), jax.ShapeDtypeStruct((SUBLANES, LANES), F32)],
        compiler_params=_params(("arbitrary",)),
    )(h, tgt, tgt, g.reshape(1, d))


def _shift_down(halo, tile, s):
    if s == 0:
        return tile
    ext = jnp.concatenate([halo, tile], axis=0)
    return pltpu.roll(ext, s, 0)[SUBLANES:]


def _shift_up(tile, head, s):
    if s == 0:
        return tile
    ext = jnp.concatenate([tile, head], axis=0)
    n = ext.shape[0]
    return pltpu.roll(ext, n - s, 0)[: tile.shape[0]]


def _to_lane_blocks(ref, cols, val):
    for j in range(cols.start // LANES, cols.stop // LANES):
        ref[j] = val[:, j * LANES - cols.start:(j + 1) * LANES - cols.start]


def _from_lane_blocks(ref, cols):
    return jnp.concatenate([ref[j] for j in range(cols.start // LANES, cols.stop // LANES)], axis=1)


def _scan_tile(a_ref, b_ref, out_ref, carry, j, *, reverse):
    ng = a_ref.shape[1] // SUBLANES
    order = list(range(SUBLANES))[::-1] if reverse else list(range(SUBLANES))

    def rows(r):
        return pl.ds(r, ng, stride=SUBLANES)

    prod, loc = {}, {}
    prev = None
    for r in order:
        ar = a_ref[j, rows(r), :]
        br = b_ref[j, rows(r), :]
        prod[r] = ar if prev is None else ar * prod[prev]
        loc[r] = br if prev is None else ar * loc[prev] + br
        prev = r
    pg, lg = prod[prev], loc[prev]
    ones = jnp.ones((SUBLANES,) + pg.shape[1:], F32)
    zeros = jnp.zeros_like(ones)
    s = 1
    while s < ng:
        p_sh = _shift_up(pg, ones, s) if reverse else _shift_down(ones, pg, s)
        l_sh = _shift_up(lg, zeros, s) if reverse else _shift_down(zeros, lg, s)
        lg = pg * l_sh + lg
        pg = pg * p_sh
        s *= 2
    leaving = pg * carry[0:1, :] + lg
    entering = _shift_up(leaving, carry, 1) if reverse else _shift_down(carry, leaving, 1)
    for r in order:
        out_ref[j, rows(r), :] = loc[r] + prod[r] * entering
    last = leaving[0:1, :] if reverse else leaving[ng - 1:ng, :]
    return jnp.broadcast_to(last, carry.shape)


def _gates(ca, wr, wi, br, bi, sp):
    cab = ca.astype(BF16)
    r = _sigmoid(jnp.dot(cab, wr, preferred_element_type=F32) + br)
    ig = _sigmoid(jnp.dot(cab, wi, preferred_element_type=F32) + bi)
    la = -LRU_C * r * sp
    a = jnp.exp(la)
    mult = jnp.sqrt(-jnp.tanh(la) * (a * a + 1.0))
    return r, ig, a, mult


def _mixer_fwd(u, wa, ba, wr_blk, br, wi_blk, bi, lam, wb, *, name):
    tp, din = u.shape
    dl = din // 6
    tt = MIX_ROWS
    cw = GATE_BLOCK
    nch = dl // cw
    assert tp % tt == 0 and dl % cw == 0

    def body(u_ref, wa_ref, ba_ref, wr_ref, br_ref, wi_ref, bi_ref, lam_ref, wb_ref,
             ca_ref, hs_ref, y_ref, xa_tail, v_tail, h_carry, a_s, b_s, h_s):
        @pl.when(pl.program_id(0) == 0)
        def _():
            xa_tail[...] = jnp.zeros_like(xa_tail)
            v_tail[...] = jnp.zeros_like(v_tail)
            h_carry[...] = jnp.zeros_like(h_carry)

        for ch in range(nch):
            cs = slice(ch * cw, (ch + 1) * cw)

            def seg(s):
                return slice(s * dl + ch * cw, s * dl + (ch + 1) * cw)

            xa = u_ref[:, seg(0)]
            halo = xa_tail[:, cs]
            ca = ba_ref[:, cs] + wa_ref[3:4, cs] * xa
            for kk in range(3):
                ca = ca + wa_ref[kk:kk + 1, cs] * _shift_down(halo, xa, 3 - kk)
            xa_tail[:, cs] = xa[tt - SUBLANES:]
            ca_ref[:, cs] = ca
            sp = _softplus(-lam_ref[:, cs])
            _, ig, a, mult = _gates(ca, wr_ref[ch], wi_ref[ch], br_ref[:, cs], bi_ref[:, cs], sp)
            _to_lane_blocks(a_s, cs, a)
            _to_lane_blocks(b_s, cs, mult * (ig * ca))

            bv = u_ref[:, seg(2)]
            v = u_ref[:, seg(3)] * u_ref[:, seg(4)]
            gb = u_ref[:, seg(5)]
            vh = v_tail[:, cs]
            cb = wb_ref[2:3, cs] * v
            for kk in range(2):
                cb = cb + wb_ref[kk:kk + 1, cs] * _shift_down(vh, v, 2 - kk)
            v_tail[:, cs] = v[tt - SUBLANES:]
            y_ref[:, dl + ch * cw: dl + (ch + 1) * cw] = (bv * cb * (gb * _sigmoid(gb))).astype(BF16)

        for ch in range(nch):
            cs = slice(ch * cw, (ch + 1) * cw)
            for j in range(cs.start // LANES, cs.stop // LANES):
                lanes = slice(j * LANES, (j + 1) * LANES)
                h_carry[:, lanes] = _scan_tile(a_s, b_s, h_s, h_carry[:, lanes], j, reverse=False)
            hsv = _from_lane_blocks(h_s, cs)
            hs_ref[:, cs] = hsv
            ga = u_ref[:, dl + ch * cw: dl + (ch + 1) * cw]
            y_ref[:, cs] = (hsv * (ga * _sigmoid(ga))).astype(BF16)

    row = lambda w: pl.BlockSpec((tt, w), lambda i: (i, 0))
    full = lambda shp: pl.BlockSpec(shp, lambda i: tuple(0 for _ in shp))
    return pl.pallas_call(
        body, name=name, grid=(tp // tt,),
        in_specs=[row(din), full((4, dl)), full((1, dl)), full((nch, cw, cw)), full((1, dl)),
                  full((nch, cw, cw)), full((1, dl)), full((1, dl)), full((3, dl))],
        out_specs=[row(dl), row(dl), row(2 * dl)],
        out_shape=[jax.ShapeDtypeStruct((tp, dl), F32), jax.ShapeDtypeStruct((tp, dl), F32),
                   jax.ShapeDtypeStruct((tp, 2 * dl), BF16)],
        scratch_shapes=[pltpu.VMEM((SUBLANES, dl), F32), pltpu.VMEM((SUBLANES, dl), F32),
                        pltpu.VMEM((SUBLANES, dl), F32)] + [pltpu.VMEM((dl // LANES, tt, LANES), F32)] * 3,
        compiler_params=_params(("arbitrary",)),
    )(u, wa, ba, wr_blk, br, wi_blk, bi, lam, wb)


SG_WA, SG_BA, SG_BR, SG_BI, SG_LAM, SG_WB, SG_ROWS = 0, 4, 5, 6, 7, 8, 16


def _mixer_bwd(u, ca, hs, dy, wa, wr_blk, br, wi_blk, bi, lam, wb, *, name):
    tp, din = u.shape
    dl = din // 6
    tt = MIX_ROWS
    cw = GATE_BLOCK
    nch = dl // cw
    nt = tp // tt
    hb = tt // SUBLANES
    tn_dims = (((0,), (0,)), ((), ()))
    nt_dims = (((1,), (1,)), ((), ()))

    def body(u_ref, uh_ref, ca_ref, hs_ref, hsh_ref, dy_ref, wa_ref, wr_ref, br_ref, wi_ref, bi_ref, lam_ref, wb_ref,
             du_ref, sg_ref, dwr_ref, dwi_ref,
             g_carry, a_head, dca_head, dcb_head, r_s, i_s, a_s, an_s, d_s, g_s):
        i = pl.program_id(0)
        first_tile = i == nt - 1

        @pl.when(i == 0)
        def _():
            for ref in (g_carry, a_head, dca_head, dcb_head, sg_ref, dwr_ref, dwi_ref):
                ref[...] = jnp.zeros_like(ref)

        def halo_of(x):
            return jnp.where(first_tile, 0.0, x)

        for ch in range(nch):
            cs = slice(ch * cw, (ch + 1) * cw)
            cav = ca_ref[:, cs]
            sp = _softplus(-lam_ref[:, cs])
            r, ig, a, _ = _gates(cav, wr_ref[ch], wi_ref[ch], br_ref[:, cs], bi_ref[:, cs], sp)
            r_s[:, cs] = r
            i_s[:, cs] = ig
            a_s[:, cs] = a
            _to_lane_blocks(an_s, cs, _shift_up(a, a_head[:, cs], 1))
            a_head[:, cs] = a[:SUBLANES]
            ga = u_ref[:, dl + ch * cw: dl + (ch + 1) * cw]
            _to_lane_blocks(d_s, cs, dy_ref[:, cs] * (ga * _sigmoid(ga)))

        for j in range(dl // LANES):
            lanes = slice(j * LANES, (j + 1) * LANES)
            g_carry[:, lanes] = _scan_tile(an_s, d_s, g_s, g_carry[:, lanes], j, reverse=True)

        def acc_row(r0, val):
            sg_ref[r0:r0 + 1, cs_cur[0]] += jnp.sum(val, axis=0, keepdims=True)

        cs_cur = [None]
        for ch in range(nch):
            cs = slice(ch * cw, (ch + 1) * cw)
            cs_cur[0] = cs

            def seg(s):
                return slice(s * dl + ch * cw, s * dl + (ch + 1) * cw)

            cav = ca_ref[:, cs]
            r = r_s[:, cs]
            ig = i_s[:, cs]
            a = a_s[:, cs]
            g = _from_lane_blocks(g_s, cs)
            hsv = hs_ref[:, cs]
            lamv = lam_ref[:, cs]
            sp = _softplus(-lamv)
            la = -LRU_C * r * sp
            e2 = a * a
            one_m_e2 = -jnp.tanh(la) * (e2 + 1.0)
            mult = jnp.sqrt(one_m_e2)
            hprev = _shift_down(halo_of(hsh_ref[:, cs]), hsv, 1)
            icav = ig * cav
            dla = g * (hprev * a - icav * (e2 * lax.rsqrt(one_m_e2)))
            gm = g * mult
            dzi = gm * icav * (1.0 - ig)
            dca = gm * ig
            dla_r = dla * r
            dzr = dla_r * (1.0 - r) * (-LRU_C * sp)
            sg_ref[SG_LAM:SG_LAM + 1, cs] += jnp.sum(dla_r, axis=0, keepdims=True) * (LRU_C * _sigmoid(-lamv))
            acc_row(SG_BR, dzr)
            acc_row(SG_BI, dzi)
            dzr_b = dzr.astype(BF16)
            dzi_b = dzi.astype(BF16)
            cab = cav.astype(BF16)
            dca = dca + lax.dot_general(dzr_b, wr_ref[ch], nt_dims, preferred_element_type=F32)
            dca = dca + lax.dot_general(dzi_b, wi_ref[ch], nt_dims, preferred_element_type=F32)
            dwr_ref[ch] += lax.dot_general(cab, dzr_b, tn_dims, preferred_element_type=F32)
            dwi_ref[ch] += lax.dot_general(cab, dzi_b, tn_dims, preferred_element_type=F32)
            acc_row(SG_BA, dca)
            xa = u_ref[:, seg(0)]
            head = dca_head[:, cs]
            dxa = wa_ref[3:4, cs] * dca
            acc_row(SG_WA + 3, dca * xa)
            for kk in range(3):
                later = _shift_up(dca, head, 3 - kk)
                acc_row(SG_WA + kk, later * xa)
                dxa = dxa + wa_ref[kk:kk + 1, cs] * later
            dca_head[:, cs] = dca[:SUBLANES]
            ga = u_ref[:, seg(1)]
            sga = _sigmoid(ga)
            dga = dy_ref[:, cs] * hsv * (sga + (ga * sga) * (1.0 - sga))
            du_ref[:, seg(0)] = dxa.astype(BF16)
            du_ref[:, seg(1)] = dga.astype(BF16)

            bv = u_ref[:, seg(2)]
            cv = u_ref[:, seg(3)]
            xb = u_ref[:, seg(4)]
            gb = u_ref[:, seg(5)]
            dyb = dy_ref[:, dl + ch * cw: dl + (ch + 1) * cw]
            v = cv * xb
            vh = halo_of(uh_ref[:, seg(3)] * uh_ref[:, seg(4)])
            v1 = _shift_down(vh, v, 1)
            v2 = _shift_down(vh, v, 2)
            cb = wb_ref[2:3, cs] * v + wb_ref[1:2, cs] * v1 + wb_ref[0:1, cs] * v2
            sgb = _sigmoid(gb)
            sl = gb * sgb
            dyb_b = dyb * bv
            dyb_cb = dyb * cb
            dcb = dyb_b * sl
            du_ref[:, seg(2)] = (dyb_cb * sl).astype(BF16)
            du_ref[:, seg(5)] = (dyb_cb * bv * (sgb + sl * (1.0 - sgb))).astype(BF16)
            bhead = dcb_head[:, cs]
            dv = wb_ref[2:3, cs] * dcb
            acc_row(SG_WB + 2, dcb * v)
            for kk in range(2):
                later = _shift_up(dcb, bhead, 2 - kk)
                acc_row(SG_WB + kk, later * v)
                dv = dv + wb_ref[kk:kk + 1, cs] * later
            dcb_head[:, cs] = dcb[:SUBLANES]
            du_ref[:, seg(3)] = (dv * xb).astype(BF16)
            du_ref[:, seg(4)] = (dv * cv).astype(BF16)

    rev = lambda w: pl.BlockSpec((tt, w), lambda i: (nt - 1 - i, 0))
    halo = lambda w: pl.BlockSpec((SUBLANES, w), lambda i: (jnp.maximum((nt - 1 - i) * hb - 1, 0), 0))
    full = lambda shp: pl.BlockSpec(shp, lambda i: tuple(0 for _ in shp))
    vm = lambda r: pltpu.VMEM((r, dl), F32)
    return pl.pallas_call(
        body, name=name, grid=(nt,),
        in_specs=[rev(din), halo(din), rev(dl), rev(dl), halo(dl), rev(2 * dl), full((4, dl)),
                  full((nch, cw, cw)), full((1, dl)), full((nch, cw, cw)), full((1, dl)), full((1, dl)), full((3, dl))],
        out_specs=[rev(din), full((SG_ROWS, dl)), full((nch, cw, cw)), full((nch, cw, cw))],
        out_shape=[jax.ShapeDtypeStruct((tp, din), BF16), jax.ShapeDtypeStruct((SG_ROWS, dl), F32),
                   jax.ShapeDtypeStruct((nch, cw, cw), F32), jax.ShapeDtypeStruct((nch, cw, cw), F32)],
        scratch_shapes=[vm(SUBLANES), vm(SUBLANES), vm(SUBLANES), vm(SUBLANES), vm(tt), vm(tt), vm(tt)]
        + [pltpu.VMEM((dl // LANES, tt, LANES), F32)] * 3,
        compiler_params=_params(("arbitrary",)),
    )(u, u, ca, hs, hs, dy, wa, wr_blk, br, wi_blk, bi, lam, wb)


def _adamw(w, g, m, v, *, name, landed=None, layer=None, depth=None, into=None, row_off=0):
    r, c = w.shape[-2:]
    rows = g.shape[0]
    tr = _tile(rows, 512, 2 * SUBLANES)
    assert row_off % tr == 0
    boff = row_off // tr
    bc1 = 1.0 - ADAM_B1 ** ADAM_STEP
    bc2 = 1.0 - ADAM_B2 ** ADAM_STEP
    slots = landed is not None

    def body(*refs):
        if into is not None:
            refs = refs[:-8] + refs[-4:]
        if slots:
            w_ref, g_ref, l_ref, m_ref, v_ref, grad_ref, delta_ref, nm_ref, nv_ref = refs
            gv = g_ref[...].astype(F32)
            for s in range(N_DEV - 1):
                gv = gv + l_ref[s].astype(F32)
        else:
            w_ref, g_ref, m_ref, v_ref, grad_ref, delta_ref, nm_ref, nv_ref = refs
            gv = g_ref[...]
        wv = w_ref[...]
        mn = ADAM_B1 * m_ref[...] + (1.0 - ADAM_B1) * gv
        vn = ADAM_B2 * v_ref[...] + (1.0 - ADAM_B2) * (gv * gv)
        m_hat = mn / bc1
        v_hat = vn / bc2
        grad_ref[...] = gv
        delta_ref[...] = -ADAM_LR * (m_hat / (jnp.sqrt(v_hat) + ADAM_EPS) + ADAM_WD * wv)
        nm_ref[...] = mn
        nv_ref[...] = vn

    if depth is None:
        blk = pl.BlockSpec((tr, c), lambda i: (i + boff, 0))
    else:
        blk = pl.BlockSpec((None, tr, c), lambda i: (layer, i + boff, 0))
    g_blk = pl.BlockSpec((tr, c), lambda i: (i, 0))
    l_spec = [pl.BlockSpec((N_DEV - 1, tr, c), lambda i: (0, i, 0))] if slots else []
    args = (w, g, landed, m, v) if slots else (w, g, m, v)
    in_specs = [blk, g_blk] + l_spec + [blk, blk]
    if depth is None:
        shp = jax.ShapeDtypeStruct((r, c), F32)
        out_blk = blk
    else:
        shp = jax.ShapeDtypeStruct((depth, r, c), F32)
        out_blk = pl.BlockSpec((None, tr, c), lambda i: (layer, i + boff, 0))
    aliases = {}
    if into is not None:
        aliases = {len(args) + j: j for j in range(4)}
        in_specs = in_specs + [ANY] * 4
        args = args + tuple(into)
    return pl.pallas_call(
        body, name=name, grid=(rows // tr,),
        in_specs=in_specs, out_specs=[out_blk] * 4,
        out_shape=[shp] * 4, input_output_aliases=aliases,
        compiler_params=_params(("parallel",)),
    )(*args)


def _slot_sum(g, *, name):
    _, r, c = g.shape
    tr = _tile(r, 512, SUBLANES)

    def body(g_ref, o_ref):
        gv = g_ref[0].astype(F32)
        for s in range(1, N_DEV):
            gv = gv + g_ref[s].astype(F32)
        o_ref[...] = gv

    return pl.pallas_call(
        body, name=name, grid=(r // tr,),
        in_specs=[pl.BlockSpec((N_DEV, tr, c), lambda i: (0, i, 0))],
        out_specs=pl.BlockSpec((tr, c), lambda i: (i, 0)),
        out_shape=jax.ShapeDtypeStruct((r, c), F32),
        compiler_params=_params(("parallel",)),
    )(g)


def _mesh_pos():
    x, y, c = lax.axis_index("x"), lax.axis_index("y"), lax.axis_index("c")
    return x, y, c, 4 * x + 2 * y + c


ANY = pl.BlockSpec(memory_space=pl.ANY)


GATHER_COPIES = 9


def _all_gather(srcs, out_shapes, views, *, name):
    n = len(srcs)
    SIB, X_OWN, Y_OWN, X_DIAG, Y_DIAG, SIB_X, SIB_Y, SIB_DIAG_TOP, SIB_DIAG_BOTTOM = range(GATHER_COPIES)

    def body(*refs):
        src = refs[:n]
        dst = refs[n:2 * n]
        send_sems, recv_sems, local_sems = refs[2 * n:]
        x, y, c, me = _mesh_pos()
        sibling, x_nbr, y_nbr = (x, y, 1 - c), (1 - x, y, c), (x, 1 - y, c)

        def block(a, px, py, pc, half=None):
            win = views[a](dst[a], 4 * px + 2 * py + pc)
            if half is None:
                return win
            rows = win.shape[0] // 2
            return win.at[pl.ds(half * rows, rows)]

        def copy(a, k, win, to, from_src=False):
            return pltpu.make_async_remote_copy(
                src_ref=src[a] if from_src else win, dst_ref=win,
                send_sem=send_sems.at[a * GATHER_COPIES + k], recv_sem=recv_sems.at[a * GATHER_COPIES + k],
                device_id=to, device_id_type=MESH)

        mine = [pltpu.make_async_copy(src[a], block(a, x, y, c), local_sems.at[a]) for a in range(n)]
        started = []

        def start(cp):
            cp.start()
            started.append(cp)

        for a in range(n):
            mine[a].start()
            own = block(a, x, y, c)
            start(copy(a, SIB, own, sibling, True))
            start(copy(a, X_OWN, own, x_nbr, True))
            start(copy(a, Y_OWN, own, y_nbr, True))
        for a in range(n):
            from_y = block(a, x, 1 - y, c)
            copy(a, Y_OWN, from_y, y_nbr).wait_recv()
            start(copy(a, X_DIAG, block(a, x, 1 - y, c, 0), x_nbr))
            start(copy(a, SIB_Y, from_y, sibling))
            from_x = block(a, 1 - x, y, c)
            copy(a, X_OWN, from_x, x_nbr).wait_recv()
            start(copy(a, Y_DIAG, block(a, 1 - x, y, c, 1), y_nbr))
            start(copy(a, SIB_X, from_x, sibling))
        for a in range(n):
            top = block(a, 1 - x, 1 - y, c, 0)
            copy(a, X_DIAG, top, x_nbr).wait_recv()
            start(copy(a, SIB_DIAG_TOP, top, sibling))
            bottom = block(a, 1 - x, 1 - y, c, 1)
            copy(a, Y_DIAG, bottom, y_nbr).wait_recv()
            start(copy(a, SIB_DIAG_BOTTOM, bottom, sibling))
        for a in range(n):
            copy(a, SIB, block(a, x, y, 1 - c), sibling).wait_recv()
            copy(a, SIB_X, block(a, 1 - x, y, 1 - c), sibling).wait_recv()
            copy(a, SIB_Y, block(a, x, 1 - y, 1 - c), sibling).wait_recv()
            copy(a, SIB_DIAG_TOP, block(a, 1 - x, 1 - y, 1 - c, 0), sibling).wait_recv()
            copy(a, SIB_DIAG_BOTTOM, block(a, 1 - x, 1 - y, 1 - c, 1), sibling).wait_recv()
        for cp in started:
            cp.wait_send()
        for cp in mine:
            cp.wait()

    return pl.pallas_call(
        body, name=name,
        in_specs=[ANY] * n, out_specs=[ANY] * n,
        out_shape=[jax.ShapeDtypeStruct(s, x.dtype) for s, x in zip(out_shapes, srcs)],
        scratch_shapes=[pltpu.SemaphoreType.DMA((GATHER_COPIES * n,)), pltpu.SemaphoreType.DMA((GATHER_COPIES * n,)),
                        pltpu.SemaphoreType.DMA((n,))],
    )(*srcs)


HBM = pl.BlockSpec(memory_space=pltpu.HBM)
SEM = pl.BlockSpec(memory_space=pltpu.SEMAPHORE)
EFFECT = pltpu.SideEffectType.DATAFLOW_SIDE_EFFECTING


def _peer_of(x, y, c, k):
    return (1 - x if k & 4 else x, 1 - y if k & 2 else y, 1 - c if k & 1 else c)


def _peer_copies(n, wins, src, land, send_sems, recv_sems):
    x, y, c, me = _mesh_pos()
    out = []
    for a in range(n):
        for k in range(1, N_DEV):
            px, py, pc = _peer_of(x, y, c, k)
            s_win, d_win = wins[a](src[a], land[a], me, 4 * px + 2 * py + pc, k)
            out.append(pltpu.make_async_remote_copy(
                src_ref=s_win, dst_ref=d_win,
                send_sem=send_sems.at[a * 7 + k - 1], recv_sem=recv_sems.at[a * 7 + k - 1],
                device_id=(px, py, pc), device_id_type=MESH))
    return out


def _push_start(srcs, lands, wins, *, name):
    n = len(srcs)

    def body(*refs):
        src = refs[:n]
        land = refs[n:2 * n]
        send_sems, recv_sems = refs[2 * n], refs[2 * n + 1]
        token = refs[-1]
        for cp in _peer_copies(n, wins, src, land, send_sems, recv_sems):
            cp.start()
        token[...] = jnp.zeros_like(token)

    bufs = (*srcs, *lands)
    return pl.pallas_call(
        body, name=name,
        out_shape=(pltpu.SemaphoreType.DMA((7 * n,)), pltpu.SemaphoreType.DMA((7 * n,)),
                   *[pltpu.HBM(v.shape, v.dtype) for v in bufs], jax.ShapeDtypeStruct((SUBLANES, LANES), F32)),
        in_specs=[HBM] * (2 * n),
        out_specs=(SEM, SEM, *[HBM] * (2 * n), pl.BlockSpec(memory_space=pltpu.VMEM)),
        input_output_aliases={i: 2 + i for i in range(2 * n)},
        compiler_params=pltpu.CompilerParams(has_side_effects=EFFECT),
    )(*[pltpu.with_memory_space_constraint(v, pltpu.HBM) for v in bufs])


def _push_wait(handle, wins, after, *, name):
    send_sems, recv_sems, *bufs, _ = handle
    n = len(bufs) // 2

    def body(*refs):
        src = refs[:n]
        land = refs[n:2 * n]
        for cp in _peer_copies(n, wins, src, land, refs[2 * n], refs[2 * n + 1]):
            cp.wait_send()
            cp.wait_recv()

    outs = pl.pallas_call(
        body, name=name,
        out_shape=tuple(pltpu.HBM(v.shape, v.dtype) for v in bufs),
        in_specs=[HBM] * (2 * n) + [SEM, SEM, ANY],
        out_specs=tuple([HBM] * (2 * n)),
        input_output_aliases={i: i for i in range(2 * n)},
        compiler_params=pltpu.CompilerParams(has_side_effects=EFFECT),
    )(*bufs, send_sems, recv_sems, after)
    return outs[:n], outs[n:]


def _gather_lead(src, land, me, peer, k):
    return src, land.at[me]


def _gather_cols(width):
    def win(src, land, me, peer, k):
        return src, land.at[:, pl.ds(me * width, width)]
    return win


def _scatter_lead(src, land, me, peer, k):
    return src.at[peer], land.at[k - 1]


def _scatter_cols(width):
    def win(src, land, me, peer, k):
        return src.at[:, pl.ds(peer * width, width)], land.at[k - 1]
    return win


def _place_block(own, *, cols, name):
    rows, width = own.shape
    tr = _tile(rows, 512, 2 * SUBLANES)
    _, _, _, me = _mesh_pos()

    def body(me_ref, x_ref, o_ref):
        o_ref[...] = x_ref[...]

    if cols:
        out_spec = pl.BlockSpec((tr, width), lambda i, me_ref: (i, me_ref[0]))
        shape = (rows, N_DEV * width)
    else:
        out_spec = pl.BlockSpec((None, tr, width), lambda i, me_ref: (me_ref[0], i, 0))
        shape = (N_DEV, rows, width)
    return pl.pallas_call(
        body, name=name,
        grid_spec=pltpu.PrefetchScalarGridSpec(
            num_scalar_prefetch=1, grid=(rows // tr,),
            in_specs=[pl.BlockSpec((tr, width), lambda i, me_ref: (i, 0))], out_specs=out_spec),
        out_shape=jax.ShapeDtypeStruct(shape, own.dtype),
        compiler_params=_params(("arbitrary",)),
    )(me.astype(jnp.int32).reshape(1), own)


def _dep(x, token):
    return x + token[0, 0].astype(x.dtype)


def _lead(ref, d):
    return ref.at[d]


def _col_window(width):
    def view(ref, d):
        return ref.at[:, pl.ds(d * width, width)]
    return view


def _pack(arrs):
    flat = jnp.concatenate([a.reshape(-1).astype(F32) for a in arrs])
    n = flat.shape[0]
    rows = -(-n // (2 * SUBLANES * LANES)) * 2 * SUBLANES
    return jnp.pad(flat, (0, rows * LANES - n)).reshape(rows, LANES)


def _unpack(buf, shapes):
    flat = buf.reshape(-1)
    out, off = [], 0
    for s in shapes:
        n = 1
        for q in s:
            n *= q
        out.append(flat[off:off + n].reshape(s))
        off += n
    return out


def _blockdiag(w, cw):
    h, hd, _ = w.shape
    per = cw // hd
    wg = w.reshape(h // per, per, hd, hd)
    eye = jnp.eye(per, dtype=w.dtype)
    blk = jnp.einsum("gpij,pq->gpiqj", wg, eye)
    return blk.reshape(h // per, cw, cw).astype(BF16)


def _blockdiag_extract(g, hd):
    n, cw, _ = g.shape
    per = cw // hd
    g5 = g.reshape(n, per, hd, per, hd)
    idx = jnp.arange(per)
    return g5[:, idx, :, idx, :].transpose(1, 0, 2, 3).reshape(n * per, hd, hd)


def kernel(x, meta, norm_g, w_in, conv_a_w, conv_a_b, lru_wr, lru_br, lru_wi, lru_bi, lru_lambda, conv_b_w, w_out, final_g, loss_target, m_meta, m_norm_g, m_w_in, m_conv_a_w, m_conv_a_b, m_lru_wr, m_lru_br, m_lru_wi, m_lru_bi, m_lru_lambda, m_conv_b_w, m_w_out, m_final_g, v_meta, v_norm_g, v_w_in, v_conv_a_w, v_conv_a_b, v_lru_wr, v_lru_br, v_lru_wi, v_lru_bi, v_lru_lambda, v_conv_b_w, v_w_out, v_final_g):
    _, seq, d = x.shape
    n_meta = meta.shape[0]
    depth = w_in.shape[0]
    din = w_in.shape[2] * N_DEV
    dl = din // 6
    dmix = 2 * dl
    wcol = w_in.shape[2]
    wrow = w_out.shape[1]
    mcol = meta.shape[1]
    ccol = conv_a_w.shape[2]
    heads, hd = lru_wr.shape[1], lru_wr.shape[2]
    n_tok = n_meta + seq
    tp = -(-n_tok // TOKEN_TILE) * TOKEN_TILE
    me = 4 * lax.axis_index("x") + 2 * lax.axis_index("y") + lax.axis_index("c")

    bf = lambda a: a.astype(BF16)
    small_mine = _pack([meta, conv_a_w, conv_b_w])
    first = _all_gather([bf(w_in[0]), small_mine], [(d, din), (N_DEV,) + small_mine.shape],
                        [_col_window(wcol), _lead], name="gather_first")
    parts = [_unpack(first[1][s], [meta.shape, conv_a_w.shape, conv_b_w.shape]) for s in range(N_DEV)]
    meta_full = jnp.concatenate([p[0] for p in parts], axis=1)
    wa_full = jnp.concatenate([p[1] for p in parts], axis=2)
    wb_full = jnp.concatenate([p[2] for p in parts], axis=2)
    w_in_full = [None] * depth
    w_out_full = [None] * depth

    push_out = [None] * depth
    push_in = [None] * depth
    w_in_full[0], src = lax.optimization_barrier((first[0], bf(w_out[0])))
    push_out[0] = _push_start([src], [_place_block(src, cols=False, name="place_wout_0")], [_gather_lead],
                              name="gather_wout_0_start")
    token = push_out[0][-1]
    for l in range(1, depth):
        src = bf(_dep(w_in[l], token))
        push_in[l] = _push_start([src], [_place_block(src, cols=True, name=f"place_win_{l}")], [_gather_cols(wcol)],
                                 name=f"gather_win_{l}_start")
        src = bf(_dep(w_out[l], push_in[l][-1]))
        push_out[l] = _push_start([src], [_place_block(src, cols=False, name=f"place_wout_{l}")], [_gather_lead],
                                  name=f"gather_wout_{l}_start")
        token = push_out[l][-1]

    wr_blk = [_blockdiag(lru_wr[l], GATE_BLOCK) for l in range(depth)]
    wi_blk = [_blockdiag(lru_wi[l], GATE_BLOCK) for l in range(depth)]
    vec = lambda a: a.reshape(1, dl)

    tm = _tile(tp, 1408)
    saved = []
    for l in range(depth):
        if l == 0:
            h, hn = _rms_fwd_first(x[0], meta_full, _dep(norm_g[l], token), tp=tp, name=f"rms_fwd_{l}")
        else:
            hn = _rms_fwd(h, norm_g[l], name=f"rms_fwd_{l}")
        if l > 0:
            _, landed = _push_wait(push_in[l], [_gather_cols(wcol)], hn, name=f"gather_win_{l}_wait")
            w_in_full[l] = landed[0]
        u = _matmul(hn, w_in_full[l], tm=tm, tn=_tile(din, 768), tk=d, name=f"mm_u_{l}")
        ca, hs, y = _mixer_fwd(u, wa_full[l], vec(conv_a_b[l]), wr_blk[l], vec(lru_br[l]), wi_blk[l], vec(lru_bi[l]),
                               vec(lru_lambda[l]), wb_full[l], name=f"mixer_fwd_{l}")
        _, landed = _push_wait(push_out[l], [_gather_lead], y, name=f"gather_wout_{l}_wait")
        w_out_full[l] = landed[0].reshape(dmix, d)
        h_next = _matmul(y, w_out_full[l], tm=tm, tn=_tile(d, 512), tk=dmix, add=h, name=f"mm_out_{l}")
        saved.append((h, hn, u, ca, hs, y))
        h = h_next

    dh, dhb, dg_final, loss_part = _loss_head(h, loss_target[0], final_g, n_meta=n_meta, n_tok=n_tok,
                                              name="loss_head")

    small_grads = [None] * depth
    sent_out = [None] * depth
    sent_in = [None] * depth
    scatter_in = [_scatter_cols(wcol)]
    token = None
    dg_norms = []
    for l in reversed(range(depth)):
        h_in, hn, u, ca, hs, y = saved[l]
        dy = _matmul(dhb, w_out_full[l], tb=True, tm=tm, tn=_tile(dmix, 512), tk=d, dep=token, name=f"mm_dy_{l}")
        dw_out = _matmul(y, dhb, ta=True, tm=_tile(dmix, 512), tn=_tile(d, 1024), tk=tp, out_dtype=BF16,
                         name=f"mm_dwout_{l}")
        sent_out[l] = _push_start([dw_out.reshape(N_DEV, wrow, d)], [lax.empty((N_DEV - 1, wrow, d), BF16)],
                                  [_scatter_lead], name=f"scatter_wout_{l}_start")
        du, sg, dwr, dwi = _mixer_bwd(u, ca, hs, dy, wa_full[l], wr_blk[l], vec(lru_br[l]), wi_blk[l], vec(lru_bi[l]),
                                      vec(lru_lambda[l]), _dep(wb_full[l], sent_out[l][-1]), name=f"mixer_bwd_{l}")
        small_grads[l] = (sg, dwr, dwi)
        if l == 0:
            early = [_pack([
                jnp.stack([small_grads[j][0][SG_BA] for j in range(depth)]),
                jnp.stack([small_grads[j][0][SG_BR] for j in range(depth)]),
                jnp.stack([small_grads[j][0][SG_BI] for j in range(depth)]),
                jnp.stack([small_grads[j][0][SG_LAM] for j in range(depth)]),
                jnp.stack([small_grads[j][0][SG_WA:SG_WA + 4] for j in range(depth)]),
                jnp.stack([small_grads[j][0][SG_WB:SG_WB + 3] for j in range(depth)]),
                dg_final[0], *dg_norms]),
                _pack([jnp.stack([_blockdiag_extract(small_grads[j][1], hd) for j in range(depth)]),
                       jnp.stack([_blockdiag_extract(small_grads[j][2], hd) for j in range(depth)])]).astype(BF16)]
            early_land = [lax.dynamic_update_slice(lax.empty((N_DEV,) + a.shape, a.dtype), a[None], (me, 0, 0))
                          for a in early]
            sent_early = _push_start(early, early_land, [_gather_lead] * 2, name="gather_early_grads_start")
        parts = 2 if l == 0 else 1
        token = sent_early[-1] if l == 0 else None
        sent_in[l] = []
        for p in range(parts):
            dw_in = _matmul(hn, du, ta=True, tm=_tile(d // parts, 512), tn=_tile(din, 768), tk=tp, out_dtype=BF16,
                            dep=token, m_part=(p, parts), name=f"mm_dwin_{l}_{p}")
            sent_in[l].append(_push_start([dw_in], [lax.empty((N_DEV - 1, d // parts, wcol), BF16)], scatter_in,
                                          name=f"scatter_win_{l}_{p}_start"))
            token = sent_in[l][-1][-1]
        dhn = _matmul(du, w_in_full[l], tb=True, tm=_tile(tp, 704, 2 * SUBLANES), tn=_tile(d, 512), tk=din, dep=token,
                      name=f"mm_dhn_{l}")
        if l > 0:
            dh, dhb, dg_norm = _rms_bwd(h_in, dhn, dh, norm_g[l], name=f"rms_bwd_{l}")
            dg_norms.append(dg_norm[0])
        else:
            grad_x, d_meta, dg_norm = _rms_bwd_first(h_in, dhn, dh, norm_g[l], n_meta=n_meta, seq=seq,
                                                     name=f"rms_bwd_{l}")

    late = _pack([dg_norm[0], d_meta, loss_part[0:1, 0:1]])
    late_all = _all_gather([late], [(N_DEV,) + late.shape], [_lead], name="gather_late_grads")[0]
    late_sum = _unpack(_slot_sum(late_all, name="sum_late_grads"), [(d,), (n_meta, d), ()])
    loss = late_sum[2]
    _, early_all = _push_wait(sent_early, [_gather_lead] * 2, late_sum[0], name="gather_early_grads_wait")
    vec_shapes = [conv_a_b.shape, lru_br.shape, lru_bi.shape, lru_lambda.shape, (depth, 4, dl), (depth, 3, dl),
                  final_g.shape] + [(d,)] * (depth - 1)
    e = _unpack(_slot_sum(early_all[0], name="sum_early_vectors"), vec_shapes)
    g_wr, g_wi = _unpack(_slot_sum(early_all[1], name="sum_early_maps"), [lru_wr.shape, lru_wi.shape])
    g_norm = jnp.stack([late_sum[0]] + e[7:][::-1])
    g_meta = lax.dynamic_slice_in_dim(late_sum[1], me * mcol, mcol, axis=1)
    g_wa = lax.dynamic_slice_in_dim(e[4], me * ccol, ccol, axis=2)
    g_wb = lax.dynamic_slice_in_dim(e[5], me * ccol, ccol, axis=2)

    small_w = [norm_g, conv_a_b, lru_wr, lru_br, lru_wi, lru_bi, lru_lambda, final_g, meta, conv_a_w, conv_b_w]
    small_m = [m_norm_g, m_conv_a_b, m_lru_wr, m_lru_br, m_lru_wi, m_lru_bi, m_lru_lambda, m_final_g, m_meta,
               m_conv_a_w, m_conv_b_w]
    small_v = [v_norm_g, v_conv_a_b, v_lru_wr, v_lru_br, v_lru_wi, v_lru_bi, v_lru_lambda, v_final_g, v_meta,
               v_conv_a_w, v_conv_b_w]
    small_g = [g_norm, e[0], g_wr, e[1], g_wi, e[2], e[3], e[6], g_meta, g_wa, g_wb]
    small_out = _adamw(_pack(small_w), _pack(small_g), _pack(small_m), _pack(small_v), name="adamw_small")
    small_shapes = [a.shape for a in small_w]
    s_grad, s_delta, s_m, s_v = [_unpack(o, small_shapes) for o in small_out]

    win_out = None
    wout_out = None
    after = small_out[0]
    for l in reversed(range(depth)):
        src, landed = _push_wait(sent_out[l], [_scatter_lead], after, name=f"scatter_wout_{l}_wait")
        own = lax.dynamic_index_in_dim(src[0], me, 0, keepdims=False)
        wout_out = _adamw(w_out, own, m_w_out, v_w_out, landed=landed[0], layer=l, depth=depth,
                          into=wout_out, name=f"adamw_w_out_{l}")
        after = wout_out[0]
        for p, sent in enumerate(sent_in[l]):
            src, landed = _push_wait(sent, scatter_in, after, name=f"scatter_win_{l}_{p}_wait")
            own = lax.dynamic_slice_in_dim(src[0], me * wcol, wcol, axis=1)
            win_out = _adamw(w_in, own, m_w_in, v_w_in, landed=landed[0], layer=l, depth=depth,
                             into=win_out, row_off=p * own.shape[0], name=f"adamw_w_in_{l}_{p}")
            after = win_out[0]

    names = ["norm_g", "conv_a_b", "lru_wr", "lru_br", "lru_wi", "lru_bi", "lru_lambda", "final_g", "meta",
             "conv_a_w", "conv_b_w"]
    order = ["meta", "norm_g", "w_in", "conv_a_w", "conv_a_b", "lru_wr", "lru_br", "lru_wi", "lru_bi", "lru_lambda",
             "conv_b_w", "w_out", "final_g"]

    def family(idx, small):
        table = {nm: small[i] for i, nm in enumerate(names)}
        table["w_in"] = win_out[idx]
        table["w_out"] = wout_out[idx]
        return [table[nm] for nm in order]

    return (loss, grad_x, *family(0, s_grad), *family(1, s_delta), *family(2, s_m), *family(3, s_v))
```

```python
import functools

import jax
import jax.numpy as jnp
from jax import lax
from jax.experimental import pallas as pl
from jax.experimental.pallas import tpu as pltpu

F32 = jnp.float32
BF16 = jnp.bfloat16
MESH = pl.DeviceIdType.MESH

N_DEV = 8
RMS_EPS = 1e-6
LRU_C = 8.0
ADAM_LR = 0.001
ADAM_B1 = 0.9
ADAM_B2 = 0.999
ADAM_EPS = 1e-08
ADAM_WD = 0.01
ADAM_STEP = 10

V7X_VMEM_LIMIT = 52 * 1024 * 1024
LANES = 128
SUBLANES = 8
TOKEN_TILE = 384
MIX_ROWS = 128
SHIFTED_ROWS = 128
GATE_BLOCK = 128


def _params(sem):
    return pltpu.CompilerParams(dimension_semantics=sem, vmem_limit_bytes=V7X_VMEM_LIMIT)


def _tile(n, target, align=LANES):
    best = None
    for t in range(align, min(n, target) + 1, align):
        if n % t == 0:
            best = t
    return n if best is None else best


def _sigmoid(z):
    return 0.5 * jnp.tanh(0.5 * z) + 0.5


def _softplus(z):
    e = jnp.exp(-jnp.abs(z))
    u = 1.0 + e
    l1p = jnp.where(u == 1.0, e, jnp.log(u) * e / jnp.where(u == 1.0, 1.0, u - 1.0))
    return jnp.maximum(z, 0.0) + l1p


def _matmul(a, b, *, ta=False, tb=False, tm, tn, tk, out_dtype=F32, add=None, dep=None, m_part=None, name):
    m, k = (a.shape[1], a.shape[0]) if ta else a.shape
    m_off = 0
    if m_part is not None:
        assert add is None and m % (m_part[1] * tm) == 0
        m //= m_part[1]
        m_off = m_part[0] * (m // tm)
    n, kb = b.shape if tb else b.shape[::-1]
    assert kb == k
    assert m % tm == 0 and n % tn == 0 and k % tk == 0, (m, n, k, tm, tn, tk)
    nk = k // tk
    a_spec = pl.BlockSpec((tk, tm), lambda i, j, q: (q, i + m_off)) if ta \
        else pl.BlockSpec((tm, tk), lambda i, j, q: (i + m_off, q))
    b_spec = pl.BlockSpec((tn, tk), lambda i, j, q: (j, q)) if tb else pl.BlockSpec((tk, tn), lambda i, j, q: (q, j))
    o_spec = pl.BlockSpec((tm, tn), lambda i, j, q: (i, j))
    o_shape = (m, n)
    dims = (((0 if ta else 1,), (1 if tb else 0,)), ((), ()))
    has_add = add is not None
    has_dep = dep is not None

    def body(*refs):
        if has_dep:
            refs = refs[:-3] + refs[-2:]
        if has_add:
            a_ref, b_ref, add_ref, o_ref, acc_ref = refs
        else:
            a_ref, b_ref, o_ref, acc_ref = refs
        q = pl.program_id(2)
        part = lax.dot_general(a_ref[...], b_ref[...], dims, preferred_element_type=F32)

        def finish(acc):
            if has_add:
                acc = acc + add_ref[...]
            o_ref[...] = acc.astype(out_dtype)

        if nk == 1:
            finish(part)
        else:
            @pl.when(q == 0)
            def _():
                acc_ref[...] = part

            @pl.when(jnp.logical_and(q > 0, q < nk - 1))
            def _():
                acc_ref[...] += part

            @pl.when(q == nk - 1)
            def _():
                finish(acc_ref[...] + part)

    in_specs = [a_spec, b_spec] + ([o_spec] if has_add else [])
    args = (a, b) + ((add,) if has_add else ())
    if has_dep:
        in_specs.append(pl.BlockSpec((SUBLANES, LANES), lambda i, j, q: (0, 0)))
        args += (dep,)
    acc_shape = (tm, tn) if nk > 1 else (SUBLANES, LANES)
    return pl.pallas_call(
        body, name=name,
        grid=(m // tm, n // tn, nk),
        in_specs=in_specs, out_specs=o_spec,
        out_shape=jax.ShapeDtypeStruct(o_shape, out_dtype),
        scratch_shapes=[pltpu.VMEM(acc_shape, F32)],
        compiler_params=_params(("parallel", "parallel", "arbitrary")),
    )(*args)


def _rms_fwd(h, g, *, name):
    tp, d = h.shape
    tr = _tile(tp, 512, SUBLANES)

    def body(h_ref, g_ref, o_ref):
        hv = h_ref[...]
        rstd = lax.rsqrt(jnp.mean(hv * hv, axis=-1, keepdims=True) + RMS_EPS)
        o_ref[...] = (hv * rstd * g_ref[...]).astype(BF16)

    return pl.pallas_call(
        body, name=name, grid=(tp // tr,),
        in_specs=[pl.BlockSpec((tr, d), lambda i: (i, 0)), pl.BlockSpec((1, d), lambda i: (0, 0))],
        out_specs=pl.BlockSpec((tr, d), lambda i: (i, 0)),
        out_shape=jax.ShapeDtypeStruct((tp, d), BF16),
        compiler_params=_params(("parallel",)),
    )(h, g.reshape(1, d))


def _rms_fwd_first(x, meta, g, *, tp, name):
    seq, d = x.shape
    n_meta = meta.shape[0]
    n_tok = n_meta + seq
    tr = SHIFTED_ROWS
    assert tp % tr == 0 and seq % tr == 0 and tr % n_meta == 0
    per = tr // n_meta

    def body(x_ref, xp_ref, m_ref, g_ref, h_ref, o_ref):
        i = pl.program_id(0)
        head = jnp.where(i == 0, m_ref[...], xp_ref[...])
        rows = i * tr + lax.broadcasted_iota(jnp.int32, (tr, 1), 0)
        hv = jnp.where(rows < n_tok, jnp.concatenate([head, x_ref[:tr - n_meta, :]], axis=0), 0.0)
        h_ref[...] = hv
        rstd = lax.rsqrt(jnp.mean(hv * hv, axis=-1, keepdims=True) + RMS_EPS)
        o_ref[...] = (hv * rstd * g_ref[...]).astype(BF16)

    row = pl.BlockSpec((tr, d), lambda i: (i, 0))
    own = pl.BlockSpec((tr, d), lambda i: (jnp.minimum(i, seq // tr - 1), 0))
    before = pl.BlockSpec((n_meta, d), lambda i: (jnp.maximum(i * per - 1, 0), 0))
    return pl.pallas_call(
        body, name=name, grid=(tp // tr,),
        in_specs=[own, before, pl.BlockSpec((n_meta, d), lambda i: (0, 0)), pl.BlockSpec((1, d), lambda i: (0, 0))],
        out_specs=[row, row],
        out_shape=[jax.ShapeDtypeStruct((tp, d), F32), jax.ShapeDtypeStruct((tp, d), BF16)],
        compiler_params=_params(("parallel",)),
    )(x, x, meta, g.reshape(1, d))


def _rms_bwd(h, dhn, dout, g, *, name):
    tp, d = h.shape
    tr = _tile(tp, 528, 2 * SUBLANES)

    def body(h_ref, dhn_ref, dout_ref, g_ref, dh_ref, dhb_ref, dg_ref):
        hv = h_ref[...]
        rstd = lax.rsqrt(jnp.mean(hv * hv, axis=-1, keepdims=True) + RMS_EPS)
        xhat = hv * rstd
        dn = dhn_ref[...]
        dxhat = dn * g_ref[...]
        dh = dout_ref[...] + rstd * (dxhat - xhat * jnp.mean(dxhat * xhat, axis=-1, keepdims=True))
        dh_ref[...] = dh
        dhb_ref[...] = dh.astype(BF16)
        part = jnp.sum(dn * xhat, axis=0, keepdims=True)

        @pl.when(pl.program_id(0) == 0)
        def _():
            dg_ref[...] = part

        @pl.when(pl.program_id(0) > 0)
        def _():
            dg_ref[...] += part

    row = pl.BlockSpec((tr, d), lambda i: (i, 0))
    vec = pl.BlockSpec((1, d), lambda i: (0, 0))
    return pl.pallas_call(
        body, name=name, grid=(tp // tr,),
        in_specs=[row, row, row, vec],
        out_specs=[row, row, vec],
        out_shape=[jax.ShapeDtypeStruct((tp, d), F32), jax.ShapeDtypeStruct((tp, d), BF16),
                   jax.ShapeDtypeStruct((1, d), F32)],
        compiler_params=_params(("arbitrary",)),
    )(h, dhn, dout, g.reshape(1, d))


def _rms_bwd_first(h, dhn, dout, g, *, n_meta, seq, name):
    tp, d = h.shape
    tr = SHIFTED_ROWS
    assert seq % tr == 0 and tr % n_meta == 0 and tp >= seq + n_meta
    nt = seq // tr
    per = tr // n_meta

    def grads(hv, dn, do, gv):
        rstd = lax.rsqrt(jnp.mean(hv * hv, axis=-1, keepdims=True) + RMS_EPS)
        xhat = hv * rstd
        dxhat = dn * gv
        dh = do + rstd * (dxhat - xhat * jnp.mean(dxhat * xhat, axis=-1, keepdims=True))
        return dh, jnp.sum(dn * xhat, axis=0, keepdims=True)

    def body(h_ref, dhn_ref, dout_ref, hn_ref, dhnn_ref, doutn_ref, g_ref, gx_ref, dmeta_ref, dg_ref):
        i = pl.program_id(0)
        gv = g_ref[...]
        dh, part = grads(h_ref[...], dhn_ref[...], dout_ref[...], gv)
        dh_next, part_next = grads(hn_ref[...], dhnn_ref[...], doutn_ref[...], gv)
        gx_ref[...] = jnp.concatenate([dh[n_meta:], dh_next], axis=0)

        @pl.when(i == 0)
        def _():
            dmeta_ref[...] = dh[:n_meta]
            dg_ref[...] = part

        @pl.when(i > 0)
        def _():
            dg_ref[...] += part

        @pl.when(i == nt - 1)
        def _():
            dg_ref[...] += part_next

    row = pl.BlockSpec((tr, d), lambda i: (i, 0))
    nxt = pl.BlockSpec((n_meta, d), lambda i: ((i + 1) * per, 0))
    vec = pl.BlockSpec((1, d), lambda i: (0, 0))
    return pl.pallas_call(
        body, name=name, grid=(nt,),
        in_specs=[row, row, row, nxt, nxt, nxt, vec],
        out_specs=[pl.BlockSpec((None, tr, d), lambda i: (0, i, 0)), pl.BlockSpec((n_meta, d), lambda i: (0, 0)), vec],
        out_shape=[jax.ShapeDtypeStruct((1, seq, d), F32), jax.ShapeDtypeStruct((n_meta, d), F32),
                   jax.ShapeDtypeStruct((1, d), F32)],
        compiler_params=_params(("arbitrary",)),
    )(h, dhn, dout, h, dhn, dout, g.reshape(1, d))


def _loss_head(h, tgt, g, *, n_meta, n_tok, name):
    tp, d = h.shape
    seq = tgt.shape[0]
    tr = SHIFTED_ROWS
    assert tp % tr == 0 and seq % tr == 0 and tr % n_meta == 0
    per = tr // n_meta

    def body(h_ref, t_ref, tp_ref, g_ref, dh_ref, dhb_ref, dg_ref, loss_ref):
        i = pl.program_id(0)
        hv = h_ref[...]
        rstd = lax.rsqrt(jnp.mean(hv * hv, axis=-1, keepdims=True) + RMS_EPS)
        xhat = hv * rstd
        gv = g_ref[...]
        rows = i * tr + lax.broadcasted_iota(jnp.int32, (tr, 1), 0)
        valid = jnp.logical_and(rows >= n_meta, rows < n_tok)
        target = jnp.concatenate([tp_ref[...], t_ref[:tr - n_meta, :]], axis=0)
        err = jnp.where(valid, xhat * gv - target, 0.0)
        dy = err * (1.0 / d)
        dxhat = dy * gv
        dh = rstd * (dxhat - xhat * jnp.mean(dxhat * xhat, axis=-1, keepdims=True))
        dh_ref[...] = dh
        dhb_ref[...] = dh.astype(BF16)
        dg_part = jnp.sum(dy * xhat, axis=0, keepdims=True)
        per_row = jnp.sum(err * err, axis=-1, keepdims=True) * (1.0 / d)
        loss_part = jnp.broadcast_to(0.5 * jnp.sum(per_row, axis=0, keepdims=True), (SUBLANES, LANES))

        @pl.when(i == 0)
        def _():
            dg_ref[...] = dg_part
            loss_ref[...] = loss_part

        @pl.when(i > 0)
        def _():
            dg_ref[...] += dg_part
            loss_ref[...] += loss_part

    row = pl.BlockSpec((tr, d), lambda i: (i, 0))
    vec = pl.BlockSpec((1, d), lambda i: (0, 0))
    own = pl.BlockSpec((tr, d), lambda i: (jnp.minimum(i, seq // tr - 1), 0))
    before = pl.BlockSpec((n_meta, d), lambda i: (jnp.maximum(i * per - 1, 0), 0))
    return pl.pallas_call(
        body, name=name, grid=(tp // tr,),
        in_specs=[row, own, before, vec],
        out_specs=[row, row, vec, pl.BlockSpec((SUBLANES, LANES), lambda i: (0, 0))],
        out_shape=[jax.ShapeDtypeStruct((tp, d), F32), jax.ShapeDtypeStruct((tp, d), BF16),
                   jax.ShapeDtypeStruct((1, d), F32), jax.ShapeDtypeStruct((SUBLANES, LANES), F32)],
        compiler_params=_params(("arbitrary",)),
    )(h, tgt, tgt, g.reshape(1, d))


def _shift_down(halo, tile, s):
    if s == 0:
        return tile
    ext = jnp.concatenate([halo, tile], axis=0)
    return pltpu.roll(ext, s, 0)[SUBLANES:]


def _shift_up(tile, head, s):
    if s == 0:
        return tile
    ext = jnp.concatenate([tile, head], axis=0)
    n = ext.shape[0]
    return pltpu.roll(ext, n - s, 0)[: tile.shape[0]]


def _to_lane_blocks(ref, cols, val):
    for j in range(cols.start // LANES, cols.stop // LANES):
        ref[j] = val[:, j * LANES - cols.start:(j + 1) * LANES - cols.start]


def _from_lane_blocks(ref, cols):
    return jnp.concatenate([ref[j] for j in range(cols.start // LANES, cols.stop // LANES)], axis=1)


def _scan_tile(a_ref, b_ref, out_ref, carry, j, *, reverse):
    ng = a_ref.shape[1] // SUBLANES
    order = list(range(SUBLANES))[::-1] if reverse else list(range(SUBLANES))

    def rows(r):
        return pl.ds(r, ng, stride=SUBLANES)

    prod, loc = {}, {}
    prev = None
    for r in order:
        ar = a_ref[j, rows(r), :]
        br = b_ref[j, rows(r), :]
        prod[r] = ar if prev is None else ar * prod[prev]
        loc[r] = br if prev is None else ar * loc[prev] + br
        prev = r
    pg, lg = prod[prev], loc[prev]
    ones = jnp.ones((SUBLANES,) + pg.shape[1:], F32)
    zeros = jnp.zeros_like(ones)
    s = 1
    while s < ng:
        p_sh = _shift_up(pg, ones, s) if reverse else _shift_down(ones, pg, s)
        l_sh = _shift_up(lg, zeros, s) if reverse else _shift_down(zeros, lg, s)
        lg = pg * l_sh + lg
        pg = pg * p_sh
        s *= 2
    leaving = pg * carry[0:1, :] + lg
    entering = _shift_up(leaving, carry, 1) if reverse else _shift_down(carry, leaving, 1)
    for r in order:
        out_ref[j, rows(r), :] = loc[r] + prod[r] * entering
    last = leaving[0:1, :] if reverse else leaving[ng - 1:ng, :]
    return jnp.broadcast_to(last, carry.shape)


def _gates(ca, wr, wi, br, bi, sp):
    cab = ca.astype(BF16)
    r = _sigmoid(jnp.dot(cab, wr, preferred_element_type=F32) + br)
    ig = _sigmoid(jnp.dot(cab, wi, preferred_element_type=F32) + bi)
    la = -LRU_C * r * sp
    a = jnp.exp(la)
    mult = jnp.sqrt(-jnp.tanh(la) * (a * a + 1.0))
    return r, ig, a, mult


def _mixer_fwd(u, wa, ba, wr_blk, br, wi_blk, bi, lam, wb, *, name):
    tp, din = u.shape
    dl = din // 6
    tt = MIX_ROWS
    cw = GATE_BLOCK
    nch = dl // cw
    assert tp % tt == 0 and dl % cw == 0

    def body(u_ref, wa_ref, ba_ref, wr_ref, br_ref, wi_ref, bi_ref, lam_ref, wb_ref,
             ca_ref, hs_ref, y_ref, r_ref, i_ref, a_ref, xa_tail, v_tail, h_carry, a_s, b_s, h_s):
        @pl.when(pl.program_id(0) == 0)
        def _():
            xa_tail[...] = jnp.zeros_like(xa_tail)
            v_tail[...] = jnp.zeros_like(v_tail)
            h_carry[...] = jnp.zeros_like(h_carry)

        for ch in range(nch):
            cs = slice(ch * cw, (ch + 1) * cw)

            def seg(s):
                return slice(s * dl + ch * cw, s * dl + (ch + 1) * cw)

            xa = u_ref[:, seg(0)]
            halo = xa_tail[:, cs]
            ca = ba_ref[:, cs] + wa_ref[3:4, cs] * xa
            for kk in range(3):
                ca = ca + wa_ref[kk:kk + 1, cs] * _shift_down(halo, xa, 3 - kk)
            xa_tail[:, cs] = xa[tt - SUBLANES:]
            ca_ref[:, cs] = ca
            sp = _softplus(-lam_ref[:, cs])
            r, ig, a, mult = _gates(ca, wr_ref[ch], wi_ref[ch], br_ref[:, cs], bi_ref[:, cs], sp)
            r_ref[:, cs] = r
            i_ref[:, cs] = ig
            a_ref[:, cs] = a
            _to_lane_blocks(a_s, cs, a)
            _to_lane_blocks(b_s, cs, mult * (ig * ca))

            bv = u_ref[:, seg(2)]
            v = u_ref[:, seg(3)] * u_ref[:, seg(4)]
            gb = u_ref[:, seg(5)]
            vh = v_tail[:, cs]
            cb = wb_ref[2:3, cs] * v
            for kk in range(2):
                cb = cb + wb_ref[kk:kk + 1, cs] * _shift_down(vh, v, 2 - kk)
            v_tail[:, cs] = v[tt - SUBLANES:]
            y_ref[:, dl + ch * cw: dl + (ch + 1) * cw] = (bv * cb * (gb * _sigmoid(gb))).astype(BF16)

        for ch in range(nch):
            cs = slice(ch * cw, (ch + 1) * cw)
            for j in range(cs.start // LANES, cs.stop // LANES):
                lanes = slice(j * LANES, (j + 1) * LANES)
                h_carry[:, lanes] = _scan_tile(a_s, b_s, h_s, h_carry[:, lanes], j, reverse=False)
            hsv = _from_lane_blocks(h_s, cs)
            hs_ref[:, cs] = hsv
            ga = u_ref[:, dl + ch * cw: dl + (ch + 1) * cw]
            y_ref[:, cs] = (hsv * (ga * _sigmoid(ga))).astype(BF16)

    row = lambda w: pl.BlockSpec((tt, w), lambda i: (i, 0))
    full = lambda shp: pl.BlockSpec(shp, lambda i: tuple(0 for _ in shp))
    return pl.pallas_call(
        body, name=name, grid=(tp // tt,),
        in_specs=[row(din), full((4, dl)), full((1, dl)), full((nch, cw, cw)), full((1, dl)),
                  full((nch, cw, cw)), full((1, dl)), full((1, dl)), full((3, dl))],
        out_specs=[row(dl), row(dl), row(2 * dl), row(dl), row(dl), row(dl)],
        out_shape=[jax.ShapeDtypeStruct((tp, dl), F32), jax.ShapeDtypeStruct((tp, dl), F32),
                   jax.ShapeDtypeStruct((tp, 2 * dl), BF16)] + [jax.ShapeDtypeStruct((tp, dl), F32)] * 3,
        scratch_shapes=[pltpu.VMEM((SUBLANES, dl), F32), pltpu.VMEM((SUBLANES, dl), F32),
                        pltpu.VMEM((SUBLANES, dl), F32)] + [pltpu.VMEM((dl // LANES, tt, LANES), F32)] * 3,
        compiler_params=_params(("arbitrary",)),
    )(u, wa, ba, wr_blk, br, wi_blk, bi, lam, wb)


SG_WA, SG_BA, SG_BR, SG_BI, SG_LAM, SG_WB, SG_ROWS = 0, 4, 5, 6, 7, 8, 16


def _mixer_bwd(u, ca, hs, gates, dy, wa, wr_blk, wi_blk, lam, wb, *, name):
    tp, din = u.shape
    dl = din // 6
    tt = MIX_ROWS
    cw = GATE_BLOCK
    nch = dl // cw
    nt = tp // tt
    hb = tt // SUBLANES
    tn_dims = (((0,), (0,)), ((), ()))
    nt_dims = (((1,), (1,)), ((), ()))

    def body(u_ref, uh_ref, ca_ref, hs_ref, hsh_ref, r_ref, i_ref, a_ref, dy_ref, wa_ref, wr_ref, wi_ref, lam_ref, wb_ref,
             du_ref, sg_ref, dwr_ref, dwi_ref,
             g_carry, a_head, dca_head, dcb_head, an_s, d_s, g_s):
        i = pl.program_id(0)
        first_tile = i == nt - 1

        @pl.when(i == 0)
        def _():
            for ref in (g_carry, a_head, dca_head, dcb_head, sg_ref, dwr_ref, dwi_ref):
                ref[...] = jnp.zeros_like(ref)

        def halo_of(x):
            return jnp.where(first_tile, 0.0, x)

        for ch in range(nch):
            cs = slice(ch * cw, (ch + 1) * cw)
            a = a_ref[:, cs]
            _to_lane_blocks(an_s, cs, _shift_up(a, a_head[:, cs], 1))
            a_head[:, cs] = a[:SUBLANES]
            ga = u_ref[:, dl + ch * cw: dl + (ch + 1) * cw]
            _to_lane_blocks(d_s, cs, dy_ref[:, cs] * (ga * _sigmoid(ga)))

        for j in range(dl // LANES):
            lanes = slice(j * LANES, (j + 1) * LANES)
            g_carry[:, lanes] = _scan_tile(an_s, d_s, g_s, g_carry[:, lanes], j, reverse=True)

        def acc_row(r0, val):
            sg_ref[r0:r0 + 1, cs_cur[0]] += jnp.sum(val, axis=0, keepdims=True)

        cs_cur = [None]
        for ch in range(nch):
            cs = slice(ch * cw, (ch + 1) * cw)
            cs_cur[0] = cs

            def seg(s):
                return slice(s * dl + ch * cw, s * dl + (ch + 1) * cw)

            cav = ca_ref[:, cs]
            r = r_ref[:, cs]
            ig = i_ref[:, cs]
            a = a_ref[:, cs]
            g = _from_lane_blocks(g_s, cs)
            hsv = hs_ref[:, cs]
            lamv = lam_ref[:, cs]
            sp = _softplus(-lamv)
            la = -LRU_C * r * sp
            e2 = a * a
            one_m_e2 = -jnp.tanh(la) * (e2 + 1.0)
            mult = jnp.sqrt(one_m_e2)
            hprev = _shift_down(halo_of(hsh_ref[:, cs]), hsv, 1)
            icav = ig * cav
            dla = g * (hprev * a - icav * (e2 * lax.rsqrt(one_m_e2)))
            gm = g * mult
            dzi = gm * icav * (1.0 - ig)
            dca = gm * ig
            dla_r = dla * r
            dzr = dla_r * (1.0 - r) * (-LRU_C * sp)
            sg_ref[SG_LAM:SG_LAM + 1, cs] += jnp.sum(dla_r, axis=0, keepdims=True) * (LRU_C * _sigmoid(-lamv))
            acc_row(SG_BR, dzr)
            acc_row(SG_BI, dzi)
            dzr_b = dzr.astype(BF16)
            dzi_b = dzi.astype(BF16)
            cab = cav.astype(BF16)
            dca = dca + lax.dot_general(dzr_b, wr_ref[ch], nt_dims, preferred_element_type=F32)
            dca = dca + lax.dot_general(dzi_b, wi_ref[ch], nt_dims, preferred_element_type=F32)
            dwr_ref[ch] += lax.dot_general(cab, dzr_b, tn_dims, preferred_element_type=F32)
            dwi_ref[ch] += lax.dot_general(cab, dzi_b, tn_dims, preferred_element_type=F32)
            acc_row(SG_BA, dca)
            xa = u_ref[:, seg(0)]
            head = dca_head[:, cs]
            dxa = wa_ref[3:4, cs] * dca
            acc_row(SG_WA + 3, dca * xa)
            for kk in range(3):
                later = _shift_up(dca, head, 3 - kk)
                acc_row(SG_WA + kk, later * xa)
                dxa = dxa + wa_ref[kk:kk + 1, cs] * later
            dca_head[:, cs] = dca[:SUBLANES]
            ga = u_ref[:, seg(1)]
            sga = _sigmoid(ga)
            dga = dy_ref[:, cs] * hsv * (sga + (ga * sga) * (1.0 - sga))
            du_ref[:, seg(0)] = dxa.astype(BF16)
            du_ref[:, seg(1)] = dga.astype(BF16)

            bv = u_ref[:, seg(2)]
            cv = u_ref[:, seg(3)]
            xb = u_ref[:, seg(4)]
            gb = u_ref[:, seg(5)]
            dyb = dy_ref[:, dl + ch * cw: dl + (ch + 1) * cw]
            v = cv * xb
            vh = halo_of(uh_ref[:, seg(3)] * uh_ref[:, seg(4)])
            v1 = _shift_down(vh, v, 1)
            v2 = _shift_down(vh, v, 2)
            cb = wb_ref[2:3, cs] * v + wb_ref[1:2, cs] * v1 + wb_ref[0:1, cs] * v2
            sgb = _sigmoid(gb)
            sl = gb * sgb
            dyb_b = dyb * bv
            dyb_cb = dyb * cb
            dcb = dyb_b * sl
            du_ref[:, seg(2)] = (dyb_cb * sl).astype(BF16)
            du_ref[:, seg(5)] = (dyb_cb * bv * (sgb + sl * (1.0 - sgb))).astype(BF16)
            bhead = dcb_head[:, cs]
            dv = wb_ref[2:3, cs] * dcb
            acc_row(SG_WB + 2, dcb * v)
            for kk in range(2):
                later = _shift_up(dcb, bhead, 2 - kk)
                acc_row(SG_WB + kk, later * v)
                dv = dv + wb_ref[kk:kk + 1, cs] * later
            dcb_head[:, cs] = dcb[:SUBLANES]
            du_ref[:, seg(3)] = (dv * xb).astype(BF16)
            du_ref[:, seg(4)] = (dv * cv).astype(BF16)

    rev = lambda w: pl.BlockSpec((tt, w), lambda i: (nt - 1 - i, 0))
    halo = lambda w: pl.BlockSpec((SUBLANES, w), lambda i: (jnp.maximum((nt - 1 - i) * hb - 1, 0), 0))
    full = lambda shp: pl.BlockSpec(shp, lambda i: tuple(0 for _ in shp))
    vm = lambda r: pltpu.VMEM((r, dl), F32)
    return pl.pallas_call(
        body, name=name, grid=(nt,),
        in_specs=[rev(din), halo(din), rev(dl), rev(dl), halo(dl), rev(dl), rev(dl), rev(dl), rev(2 * dl), full((4, dl)),
                  full((nch, cw, cw)), full((nch, cw, cw)), full((1, dl)), full((3, dl))],
        out_specs=[rev(din), full((SG_ROWS, dl)), full((nch, cw, cw)), full((nch, cw, cw))],
        out_shape=[jax.ShapeDtypeStruct((tp, din), BF16), jax.ShapeDtypeStruct((SG_ROWS, dl), F32),
                   jax.ShapeDtypeStruct((nch, cw, cw), F32), jax.ShapeDtypeStruct((nch, cw, cw), F32)],
        scratch_shapes=[vm(SUBLANES), vm(SUBLANES), vm(SUBLANES), vm(SUBLANES)]
        + [pltpu.VMEM((dl // LANES, tt, LANES), F32)] * 3,
        compiler_params=_params(("arbitrary",)),
    )(u, u, ca, hs, hs, *gates, dy, wa, wr_blk, wi_blk, lam, wb)


def _adamw(w, g, m, v, *, name, landed=None, layer=None, depth=None, into=None, row_off=0):
    r, c = w.shape[-2:]
    rows = g.shape[0]
    tr = _tile(rows, 512, 2 * SUBLANES)
    assert row_off % tr == 0
    boff = row_off // tr
    bc1 = 1.0 - ADAM_B1 ** ADAM_STEP
    bc2 = 1.0 - ADAM_B2 ** ADAM_STEP
    slots = landed is not None

    def body(*refs):
        if into is not None:
            refs = refs[:-8] + refs[-4:]
        if slots:
            w_ref, g_ref, l_ref, m_ref, v_ref, grad_ref, delta_ref, nm_ref, nv_ref = refs
            gv = g_ref[...].astype(F32)
            for s in range(N_DEV - 1):
                gv = gv + l_ref[s].astype(F32)
        else:
            w_ref, g_ref, m_ref, v_ref, grad_ref, delta_ref, nm_ref, nv_ref = refs
            gv = g_ref[...]
        wv = w_ref[...]
        mn = ADAM_B1 * m_ref[...] + (1.0 - ADAM_B1) * gv
        vn = ADAM_B2 * v_ref[...] + (1.0 - ADAM_B2) * (gv * gv)
        m_hat = mn / bc1
        v_hat = vn / bc2
        grad_ref[...] = gv
        delta_ref[...] = -ADAM_LR * (m_hat / (jnp.sqrt(v_hat) + ADAM_EPS) + ADAM_WD * wv)
        nm_ref[...] = mn
        nv_ref[...] = vn

    if depth is None:
        blk = pl.BlockSpec((tr, c), lambda i: (i + boff, 0))
    else:
        blk = pl.BlockSpec((None, tr, c), lambda i: (layer, i + boff, 0))
    g_blk = pl.BlockSpec((tr, c), lambda i: (i, 0))
    l_spec = [pl.BlockSpec((N_DEV - 1, tr, c), lambda i: (0, i, 0))] if slots else []
    args = (w, g, landed, m, v) if slots else (w, g, m, v)
    in_specs = [blk, g_blk] + l_spec + [blk, blk]
    if depth is None:
        shp = jax.ShapeDtypeStruct((r, c), F32)
        out_blk = blk
    else:
        shp = jax.ShapeDtypeStruct((depth, r, c), F32)
        out_blk = pl.BlockSpec((None, tr, c), lambda i: (layer, i + boff, 0))
    aliases = {}
    if into is not None:
        aliases = {len(args) + j: j for j in range(4)}
        in_specs = in_specs + [ANY] * 4
        args = args + tuple(into)
    return pl.pallas_call(
        body, name=name, grid=(rows // tr,),
        in_specs=in_specs, out_specs=[out_blk] * 4,
        out_shape=[shp] * 4, input_output_aliases=aliases,
        compiler_params=_params(("parallel",)),
    )(*args)


def _slot_sum(g, *, name):
    _, r, c = g.shape
    tr = _tile(r, 512, SUBLANES)

    def body(g_ref, o_ref):
        gv = g_ref[0].astype(F32)
        for s in range(1, N_DEV):
            gv = gv + g_ref[s].astype(F32)
        o_ref[...] = gv

    return pl.pallas_call(
        body, name=name, grid=(r // tr,),
        in_specs=[pl.BlockSpec((N_DEV, tr, c), lambda i: (0, i, 0))],
        out_specs=pl.BlockSpec((tr, c), lambda i: (i, 0)),
        out_shape=jax.ShapeDtypeStruct((r, c), F32),
        compiler_params=_params(("parallel",)),
    )(g)


def _mesh_pos():
    x, y, c = lax.axis_index("x"), lax.axis_index("y"), lax.axis_index("c")
    return x, y, c, 4 * x + 2 * y + c


ANY = pl.BlockSpec(memory_space=pl.ANY)


GATHER_COPIES = 9


def _all_gather(srcs, out_shapes, views, *, name):
    n = len(srcs)
    SIB, X_OWN, Y_OWN, X_DIAG, Y_DIAG, SIB_X, SIB_Y, SIB_DIAG_TOP, SIB_DIAG_BOTTOM = range(GATHER_COPIES)

    def body(*refs):
        src = refs[:n]
        dst = refs[n:2 * n]
        send_sems, recv_sems, local_sems = refs[2 * n:]
        x, y, c, me = _mesh_pos()
        sibling, x_nbr, y_nbr = (x, y, 1 - c), (1 - x, y, c), (x, 1 - y, c)

        def block(a, px, py, pc, half=None):
            win = views[a](dst[a], 4 * px + 2 * py + pc)
            if half is None:
                return win
            rows = win.shape[0] // 2
            return win.at[pl.ds(half * rows, rows)]

        def copy(a, k, win, to, from_src=False):
            return pltpu.make_async_remote_copy(
                src_ref=src[a] if from_src else win, dst_ref=win,
                send_sem=send_sems.at[a * GATHER_COPIES + k], recv_sem=recv_sems.at[a * GATHER_COPIES + k],
                device_id=to, device_id_type=MESH)

        mine = [pltpu.make_async_copy(src[a], block(a, x, y, c), local_sems.at[a]) for a in range(n)]
        started = []

        def start(cp):
            cp.start()
            started.append(cp)

        for a in range(n):
            mine[a].start()
            own = block(a, x, y, c)
            start(copy(a, SIB, own, sibling, True))
            start(copy(a, X_OWN, own, x_nbr, True))
            start(copy(a, Y_OWN, own, y_nbr, True))
        for a in range(n):
            from_y = block(a, x, 1 - y, c)
            copy(a, Y_OWN, from_y, y_nbr).wait_recv()
            start(copy(a, X_DIAG, block(a, x, 1 - y, c, 0), x_nbr))
            start(copy(a, SIB_Y, from_y, sibling))
            from_x = block(a, 1 - x, y, c)
            copy(a, X_OWN, from_x, x_nbr).wait_recv()
            start(copy(a, Y_DIAG, block(a, 1 - x, y, c, 1), y_nbr))
            start(copy(a, SIB_X, from_x, sibling))
        for a in range(n):
            top = block(a, 1 - x, 1 - y, c, 0)
            copy(a, X_DIAG, top, x_nbr).wait_recv()
            start(copy(a, SIB_DIAG_TOP, top, sibling))
            bottom = block(a, 1 - x, 1 - y, c, 1)
            copy(a, Y_DIAG, bottom, y_nbr).wait_recv()
            start(copy(a, SIB_DIAG_BOTTOM, bottom, sibling))
        for a in range(n):
            copy(a, SIB, block(a, x, y, 1 - c), sibling).wait_recv()
            copy(a, SIB_X, block(a, 1 - x, y, 1 - c), sibling).wait_recv()
            copy(a, SIB_Y, block(a, x, 1 - y, 1 - c), sibling).wait_recv()
            copy(a, SIB_DIAG_TOP, block(a, 1 - x, 1 - y, 1 - c, 0), sibling).wait_recv()
            copy(a, SIB_DIAG_BOTTOM, block(a, 1 - x, 1 - y, 1 - c, 1), sibling).wait_recv()
        for cp in started:
            cp.wait_send()
        for cp in mine:
            cp.wait()

    return pl.pallas_call(
        body, name=name,
        in_specs=[ANY] * n, out_specs=[ANY] * n,
        out_shape=[jax.ShapeDtypeStruct(s, x.dtype) for s, x in zip(out_shapes, srcs)],
        scratch_shapes=[pltpu.SemaphoreType.DMA((GATHER_COPIES * n,)), pltpu.SemaphoreType.DMA((GATHER_COPIES * n,)),
                        pltpu.SemaphoreType.DMA((n,))],
    )(*srcs)


HBM = pl.BlockSpec(memory_space=pltpu.HBM)
SEM = pl.BlockSpec(memory_space=pltpu.SEMAPHORE)
EFFECT = pltpu.SideEffectType.DATAFLOW_SIDE_EFFECTING


def _peer_of(x, y, c, k):
    return (1 - x if k & 4 else x, 1 - y if k & 2 else y, 1 - c if k & 1 else c)


def _peer_copies(n, wins, src, land, send_sems, recv_sems):
    x, y, c, me = _mesh_pos()
    out = []
    for a in range(n):
        for k in range(1, N_DEV):
            px, py, pc = _peer_of(x, y, c, k)
            s_win, d_win = wins[a](src[a], land[a], me, 4 * px + 2 * py + pc, k)
            out.append(pltpu.make_async_remote_copy(
                src_ref=s_win, dst_ref=d_win,
                send_sem=send_sems.at[a * 7 + k - 1], recv_sem=recv_sems.at[a * 7 + k - 1],
                device_id=(px, py, pc), device_id_type=MESH))
    return out


def _push_start(srcs, lands, wins, *, name):
    n = len(srcs)

    def body(*refs):
        src = refs[:n]
        land = refs[n:2 * n]
        send_sems, recv_sems = refs[2 * n], refs[2 * n + 1]
        token = refs[-1]
        for cp in _peer_copies(n, wins, src, land, send_sems, recv_sems):
            cp.start()
        token[...] = jnp.zeros_like(token)

    bufs = (*srcs, *lands)
    return pl.pallas_call(
        body, name=name,
        out_shape=(pltpu.SemaphoreType.DMA((7 * n,)), pltpu.SemaphoreType.DMA((7 * n,)),
                   *[pltpu.HBM(v.shape, v.dtype) for v in bufs], jax.ShapeDtypeStruct((SUBLANES, LANES), F32)),
        in_specs=[HBM] * (2 * n),
        out_specs=(SEM, SEM, *[HBM] * (2 * n), pl.BlockSpec(memory_space=pltpu.VMEM)),
        input_output_aliases={i: 2 + i for i in range(2 * n)},
        compiler_params=pltpu.CompilerParams(has_side_effects=EFFECT),
    )(*[pltpu.with_memory_space_constraint(v, pltpu.HBM) for v in bufs])


def _push_wait(handle, wins, after, *, name):
    send_sems, recv_sems, *bufs, _ = handle
    n = len(bufs) // 2

    def body(*refs):
        src = refs[:n]
        land = refs[n:2 * n]
        for cp in _peer_copies(n, wins, src, land, refs[2 * n], refs[2 * n + 1]):
            cp.wait_send()
            cp.wait_recv()

    outs = pl.pallas_call(
        body, name=name,
        out_shape=tuple(pltpu.HBM(v.shape, v.dtype) for v in bufs),
        in_specs=[HBM] * (2 * n) + [SEM, SEM, ANY],
        out_specs=tuple([HBM] * (2 * n)),
        input_output_aliases={i: i for i in range(2 * n)},
        compiler_params=pltpu.CompilerParams(has_side_effects=EFFECT),
    )(*bufs, send_sems, recv_sems, after)
    return outs[:n], outs[n:]


def _gather_lead(src, land, me, peer, k):
    return src, land.at[me]


def _gather_cols(width):
    def win(src, land, me, peer, k):
        return src, land.at[:, pl.ds(me * width, width)]
    return win


def _scatter_lead(src, land, me, peer, k):
    return src.at[peer], land.at[k - 1]


def _scatter_cols(width):
    def win(src, land, me, peer, k):
        return src.at[:, pl.ds(peer * width, width)], land.at[k - 1]
    return win


def _place_block(own, *, cols, name):
    rows, width = own.shape
    tr = _tile(rows, 512, 2 * SUBLANES)
    _, _, _, me = _mesh_pos()

    def body(me_ref, x_ref, o_ref):
        o_ref[...] = x_ref[...]

    if cols:
        out_spec = pl.BlockSpec((tr, width), lambda i, me_ref: (i, me_ref[0]))
        shape = (rows, N_DEV * width)
    else:
        out_spec = pl.BlockSpec((None, tr, width), lambda i, me_ref: (me_ref[0], i, 0))
        shape = (N_DEV, rows, width)
    return pl.pallas_call(
        body, name=name,
        grid_spec=pltpu.PrefetchScalarGridSpec(
            num_scalar_prefetch=1, grid=(rows // tr,),
            in_specs=[pl.BlockSpec((tr, width), lambda i, me_ref: (i, 0))], out_specs=out_spec),
        out_shape=jax.ShapeDtypeStruct(shape, own.dtype),
        compiler_params=_params(("arbitrary",)),
    )(me.astype(jnp.int32).reshape(1), own)


def _dep(x, token):
    return x + token[0, 0].astype(x.dtype)


def _lead(ref, d):
    return ref.at[d]


def _col_window(width):
    def view(ref, d):
        return ref.at[:, pl.ds(d * width, width)]
    return view


def _pack(arrs):
    flat = jnp.concatenate([a.reshape(-1).astype(F32) for a in arrs])
    n = flat.shape[0]
    rows = -(-n // (2 * SUBLANES * LANES)) * 2 * SUBLANES
    return jnp.pad(flat, (0, rows * LANES - n)).reshape(rows, LANES)


def _unpack(buf, shapes):
    flat = buf.reshape(-1)
    out, off = [], 0
    for s in shapes:
        n = 1
        for q in s:
            n *= q
        out.append(flat[off:off + n].reshape(s))
        off += n
    return out


def _blockdiag(w, cw):
    h, hd, _ = w.shape
    per = cw // hd
    wg = w.reshape(h // per, per, hd, hd)
    eye = jnp.eye(per, dtype=w.dtype)
    blk = jnp.einsum("gpij,pq->gpiqj", wg, eye)
    return blk.reshape(h // per, cw, cw).astype(BF16)


def _blockdiag_extract(g, hd):
    n, cw, _ = g.shape
    per = cw // hd
    g5 = g.reshape(n, per, hd, per, hd)
    idx = jnp.arange(per)
    return g5[:, idx, :, idx, :].transpose(1, 0, 2, 3).reshape(n * per, hd, hd)


def kernel(x, meta, norm_g, w_in, conv_a_w, conv_a_b, lru_wr, lru_br, lru_wi, lru_bi, lru_lambda, conv_b_w, w_out, final_g, loss_target, m_meta, m_norm_g, m_w_in, m_conv_a_w, m_conv_a_b, m_lru_wr, m_lru_br, m_lru_wi, m_lru_bi, m_lru_lambda, m_conv_b_w, m_w_out, m_final_g, v_meta, v_norm_g, v_w_in, v_conv_a_w, v_conv_a_b, v_lru_wr, v_lru_br, v_lru_wi, v_lru_bi, v_lru_lambda, v_conv_b_w, v_w_out, v_final_g):
    _, seq, d = x.shape
    n_meta = meta.shape[0]
    depth = w_in.shape[0]
    din = w_in.shape[2] * N_DEV
    dl = din // 6
    dmix = 2 * dl
    wcol = w_in.shape[2]
    wrow = w_out.shape[1]
    mcol = meta.shape[1]
    ccol = conv_a_w.shape[2]
    heads, hd = lru_wr.shape[1], lru_wr.shape[2]
    n_tok = n_meta + seq
    tp = -(-n_tok // TOKEN_TILE) * TOKEN_TILE
    me = 4 * lax.axis_index("x") + 2 * lax.axis_index("y") + lax.axis_index("c")

    bf = lambda a: a.astype(BF16)
    small_mine = _pack([meta, conv_a_w, conv_b_w])
    first = _all_gather([bf(w_in[0]), small_mine], [(d, din), (N_DEV,) + small_mine.shape],
                        [_col_window(wcol), _lead], name="gather_first")
    parts = [_unpack(first[1][s], [meta.shape, conv_a_w.shape, conv_b_w.shape]) for s in range(N_DEV)]
    meta_full = jnp.concatenate([p[0] for p in parts], axis=1)
    wa_full = jnp.concatenate([p[1] for p in parts], axis=2)
    wb_full = jnp.concatenate([p[2] for p in parts], axis=2)
    w_in_full = [None] * depth
    w_out_full = [None] * depth

    push_out = [None] * depth
    push_in = [None] * depth
    w_in_full[0], src = lax.optimization_barrier((first[0], bf(w_out[0])))
    push_out[0] = _push_start([src], [_place_block(src, cols=False, name="place_wout_0")], [_gather_lead],
                              name="gather_wout_0_start")
    token = push_out[0][-1]
    for l in range(1, depth):
        src = bf(_dep(w_in[l], token))
        push_in[l] = _push_start([src], [_place_block(src, cols=True, name=f"place_win_{l}")], [_gather_cols(wcol)],
                                 name=f"gather_win_{l}_start")
        src = bf(_dep(w_out[l], push_in[l][-1]))
        push_out[l] = _push_start([src], [_place_block(src, cols=False, name=f"place_wout_{l}")], [_gather_lead],
                                  name=f"gather_wout_{l}_start")
        token = push_out[l][-1]

    wr_blk = [_blockdiag(lru_wr[l], GATE_BLOCK) for l in range(depth)]
    wi_blk = [_blockdiag(lru_wi[l], GATE_BLOCK) for l in range(depth)]
    vec = lambda a: a.reshape(1, dl)

    tm = _tile(tp, 1408)
    saved = []
    for l in range(depth):
        if l == 0:
            h, hn = _rms_fwd_first(x[0], meta_full, _dep(norm_g[l], token), tp=tp, name=f"rms_fwd_{l}")
        else:
            hn = _rms_fwd(h, norm_g[l], name=f"rms_fwd_{l}")
        if l > 0:
            _, landed = _push_wait(push_in[l], [_gather_cols(wcol)], hn, name=f"gather_win_{l}_wait")
            w_in_full[l] = landed[0]
        u = _matmul(hn, w_in_full[l], tm=tm, tn=_tile(din, 768), tk=d, name=f"mm_u_{l}")
        ca, hs, y, *gates = _mixer_fwd(u, wa_full[l], vec(conv_a_b[l]), wr_blk[l], vec(lru_br[l]), wi_blk[l],
                                       vec(lru_bi[l]), vec(lru_lambda[l]), wb_full[l], name=f"mixer_fwd_{l}")
        _, landed = _push_wait(push_out[l], [_gather_lead], y, name=f"gather_wout_{l}_wait")
        w_out_full[l] = landed[0].reshape(dmix, d)
        h_next = _matmul(y, w_out_full[l], tm=tm, tn=_tile(d, 512), tk=dmix, add=h, name=f"mm_out_{l}")
        saved.append((h, hn, u, ca, hs, y, gates))
        h = h_next

    dh, dhb, dg_final, loss_part = _loss_head(h, loss_target[0], final_g, n_meta=n_meta, n_tok=n_tok,
                                              name="loss_head")

    small_grads = [None] * depth
    sent_out = [None] * depth
    sent_in = [None] * depth
    scatter_in = [_scatter_cols(wcol)]
    token = None
    dg_norms = []
    for l in reversed(range(depth)):
        h_in, hn, u, ca, hs, y, gates = saved[l]
        dy = _matmul(dhb, w_out_full[l], tb=True, tm=tm, tn=_tile(dmix, 512), tk=d, dep=token, name=f"mm_dy_{l}")
        dw_out = _matmul(y, dhb, ta=True, tm=_tile(dmix, 512), tn=_tile(d, 1024), tk=tp, out_dtype=BF16,
                         name=f"mm_dwout_{l}")
        sent_out[l] = _push_start([dw_out.reshape(N_DEV, wrow, d)], [lax.empty((N_DEV - 1, wrow, d), BF16)],
                                  [_scatter_lead], name=f"scatter_wout_{l}_start")
        du, sg, dwr, dwi = _mixer_bwd(u, ca, hs, gates, dy, wa_full[l], wr_blk[l], wi_blk[l], vec(lru_lambda[l]),
                                      _dep(wb_full[l], sent_out[l][-1]), name=f"mixer_bwd_{l}")
        small_grads[l] = (sg, dwr, dwi)
        if l == 0:
            early = [_pack([
                jnp.stack([small_grads[j][0][SG_BA] for j in range(depth)]),
                jnp.stack([small_grads[j][0][SG_BR] for j in range(depth)]),
                jnp.stack([small_grads[j][0][SG_BI] for j in range(depth)]),
                jnp.stack([small_grads[j][0][SG_LAM] for j in range(depth)]),
                jnp.stack([small_grads[j][0][SG_WA:SG_WA + 4] for j in range(depth)]),
                jnp.stack([small_grads[j][0][SG_WB:SG_WB + 3] for j in range(depth)]),
                dg_final[0], *dg_norms]),
                _pack([jnp.stack([_blockdiag_extract(small_grads[j][1], hd) for j in range(depth)]),
                       jnp.stack([_blockdiag_extract(small_grads[j][2], hd) for j in range(depth)])]).astype(BF16)]
            early_land = [lax.dynamic_update_slice(lax.empty((N_DEV,) + a.shape, a.dtype), a[None], (me, 0, 0))
                          for a in early]
            sent_early = _push_start(early, early_land, [_gather_lead] * 2, name="gather_early_grads_start")
        parts = 2 if l == 0 else 1
        token = sent_early[-1] if l == 0 else None
        sent_in[l] = []
        for p in range(parts):
            dw_in = _matmul(hn, du, ta=True, tm=_tile(d // parts, 512), tn=_tile(din, 768), tk=tp, out_dtype=BF16,
                            dep=token, m_part=(p, parts), name=f"mm_dwin_{l}_{p}")
            sent_in[l].append(_push_start([dw_in], [lax.empty((N_DEV - 1, d // parts, wcol), BF16)], scatter_in,
                                          name=f"scatter_win_{l}_{p}_start"))
            token = sent_in[l][-1][-1]
        dhn = _matmul(du, w_in_full[l], tb=True, tm=_tile(tp, 704, 2 * SUBLANES), tn=_tile(d, 512), tk=din, dep=token,
                      name=f"mm_dhn_{l}")
        if l > 0:
            dh, dhb, dg_norm = _rms_bwd(h_in, dhn, dh, norm_g[l], name=f"rms_bwd_{l}")
            dg_norms.append(dg_norm[0])
        else:
            grad_x, d_meta, dg_norm = _rms_bwd_first(h_in, dhn, dh, norm_g[l], n_meta=n_meta, seq=seq,
                                                     name=f"rms_bwd_{l}")

    late = _pack([dg_norm[0], d_meta, loss_part[0:1, 0:1]])
    late_all = _all_gather([late], [(N_DEV,) + late.shape], [_lead], name="gather_late_grads")[0]
    late_sum = _unpack(_slot_sum(late_all, name="sum_late_grads"), [(d,), (n_meta, d), ()])
    loss = late_sum[2]
    _, early_all = _push_wait(sent_early, [_gather_lead] * 2, late_sum[0], name="gather_early_grads_wait")
    vec_shapes = [conv_a_b.shape, lru_br.shape, lru_bi.shape, lru_lambda.shape, (depth, 4, dl), (depth, 3, dl),
                  final_g.shape] + [(d,)] * (depth - 1)
    e = _unpack(_slot_sum(early_all[0], name="sum_early_vectors"), vec_shapes)
    g_wr, g_wi = _unpack(_slot_sum(early_all[1], name="sum_early_maps"), [lru_wr.shape, lru_wi.shape])
    g_norm = jnp.stack([late_sum[0]] + e[7:][::-1])
    g_meta = lax.dynamic_slice_in_dim(late_sum[1], me * mcol, mcol, axis=1)
    g_wa = lax.dynamic_slice_in_dim(e[4], me * ccol, ccol, axis=2)
    g_wb = lax.dynamic_slice_in_dim(e[5], me * ccol, ccol, axis=2)

    small_w = [norm_g, conv_a_b, lru_wr, lru_br, lru_wi, lru_bi, lru_lambda, final_g, meta, conv_a_w, conv_b_w]
    small_m = [m_norm_g, m_conv_a_b, m_lru_wr, m_lru_br, m_lru_wi, m_lru_bi, m_lru_lambda, m_final_g, m_meta,
               m_conv_a_w, m_conv_b_w]
    small_v = [v_norm_g, v_conv_a_b, v_lru_wr, v_lru_br, v_lru_wi, v_lru_bi, v_lru_lambda, v_final_g, v_meta,
               v_conv_a_w, v_conv_b_w]
    small_g = [g_norm, e[0], g_wr, e[1], g_wi, e[2], e[3], e[6], g_meta, g_wa, g_wb]
    small_out = _adamw(_pack(small_w), _pack(small_g), _pack(small_m), _pack(small_v), name="adamw_small")
    small_shapes = [a.shape for a in small_w]
    s_grad, s_delta, s_m, s_v = [_unpack(o, small_shapes) for o in small_out]

    win_out = None
    wout_out = None
    after = small_out[0]
    for l in reversed(range(depth)):
        src, landed = _push_wait(sent_out[l], [_scatter_lead], after, name=f"scatter_wout_{l}_wait")
        own = lax.dynamic_index_in_dim(src[0], me, 0, keepdims=False)
        wout_out = _adamw(w_out, own, m_w_out, v_w_out, landed=landed[0], layer=l, depth=depth,
                          into=wout_out, name=f"adamw_w_out_{l}")
        after = wout_out[0]
        for p, sent in enumerate(sent_in[l]):
            src, landed = _push_wait(sent, scatter_in, after, name=f"scatter_win_{l}_{p}_wait")
            own = lax.dynamic_slice_in_dim(src[0], me * wcol, wcol, axis=1)
            win_out = _adamw(w_in, own, m_w_in, v_w_in, landed=landed[0], layer=l, depth=depth,
                             into=win_out, row_off=p * own.shape[0], name=f"adamw_w_in_{l}_{p}")
            after = win_out[0]

    names = ["norm_g", "conv_a_b", "lru_wr", "lru_br", "lru_wi", "lru_bi", "lru_lambda", "final_g", "meta",
             "conv_a_w", "conv_b_w"]
    order = ["meta", "norm_g", "w_in", "conv_a_w", "conv_a_b", "lru_wr", "lru_br", "lru_wi", "lru_bi", "lru_lambda",
             "conv_b_w", "w_out", "final_g"]

    def family(idx, small):
        table = {nm: small[i] for i, nm in enumerate(names)}
        table["w_in"] = win_out[idx]
        table["w_out"] = wout_out[idx]
        return [table[nm] for nm in order]

    return (loss, grad_x, *family(0, s_grad), *family(1, s_delta), *family(2, s_m), *family(3, s_v))
```

```python
import functools

import jax
import jax.numpy as jnp
from jax import lax
from jax.experimental import pallas as pl
from jax.experimental.pallas import tpu as pltpu

F32 = jnp.float32
BF16 = jnp.bfloat16
MESH = pl.DeviceIdType.MESH

N_DEV = 8
RMS_EPS = 1e-6
LRU_C = 8.0
ADAM_LR = 0.001
ADAM_B1 = 0.9
ADAM_B2 = 0.999
ADAM_EPS = 1e-08
ADAM_WD = 0.01
ADAM_STEP = 10

V7X_VMEM_LIMIT = 52 * 1024 * 1024
LANES = 128
SUBLANES = 8
TOKEN_TILE = 384
MIX_ROWS = 128
SHIFTED_ROWS = 128
GATE_BLOCK = 128


def _params(sem):
    return pltpu.CompilerParams(dimension_semantics=sem, vmem_limit_bytes=V7X_VMEM_LIMIT)


def _tile(n, target, align=LANES):
    best = None
    for t in range(align, min(n, target) + 1, align):
        if n % t == 0:
            best = t
    return n if best is None else best


def _sigmoid(z):
    return 0.5 * jnp.tanh(0.5 * z) + 0.5


def _softplus(z):
    e = jnp.exp(-jnp.abs(z))
    u = 1.0 + e
    l1p = jnp.where(u == 1.0, e, jnp.log(u) * e / jnp.where(u == 1.0, 1.0, u - 1.0))
    return jnp.maximum(z, 0.0) + l1p


def _matmul(a, b, *, ta=False, tb=False, tm, tn, tk, out_dtype=F32, add=None, dep=None, m_part=None, name):
    m, k = (a.shape[1], a.shape[0]) if ta else a.shape
    m_off = 0
    if m_part is not None:
        assert add is None and m % (m_part[1] * tm) == 0
        m //= m_part[1]
        m_off = m_part[0] * (m // tm)
    n, kb = b.shape if tb else b.shape[::-1]
    assert kb == k
    assert m % tm == 0 and n % tn == 0 and k % tk == 0, (m, n, k, tm, tn, tk)
    nk = k // tk
    a_spec = pl.BlockSpec((tk, tm), lambda i, j, q: (q, i + m_off)) if ta \
        else pl.BlockSpec((tm, tk), lambda i, j, q: (i + m_off, q))
    b_spec = pl.BlockSpec((tn, tk), lambda i, j, q: (j, q)) if tb else pl.BlockSpec((tk, tn), lambda i, j, q: (q, j))
    o_spec = pl.BlockSpec((tm, tn), lambda i, j, q: (i, j))
    o_shape = (m, n)
    dims = (((0 if ta else 1,), (1 if tb else 0,)), ((), ()))
    has_add = add is not None
    has_dep = dep is not None

    def body(*refs):
        if has_dep:
            refs = refs[:-3] + refs[-2:]
        if has_add:
            a_ref, b_ref, add_ref, o_ref, acc_ref = refs
        else:
            a_ref, b_ref, o_ref, acc_ref = refs
        q = pl.program_id(2)
        part = lax.dot_general(a_ref[...], b_ref[...], dims, preferred_element_type=F32)

        def finish(acc):
            if has_add:
                acc = acc + add_ref[...]
            o_ref[...] = acc.astype(out_dtype)

        if nk == 1:
            finish(part)
        else:
            @pl.when(q == 0)
            def _():
                acc_ref[...] = part

            @pl.when(jnp.logical_and(q > 0, q < nk - 1))
            def _():
                acc_ref[...] += part

            @pl.when(q == nk - 1)
            def _():
                finish(acc_ref[...] + part)

    in_specs = [a_spec, b_spec] + ([o_spec] if has_add else [])
    args = (a, b) + ((add,) if has_add else ())
    if has_dep:
        in_specs.append(pl.BlockSpec((SUBLANES, LANES), lambda i, j, q: (0, 0)))
        args += (dep,)
    acc_shape = (tm, tn) if nk > 1 else (SUBLANES, LANES)
    return pl.pallas_call(
        body, name=name,
        grid=(m // tm, n // tn, nk),
        in_specs=in_specs, out_specs=o_spec,
        out_shape=jax.ShapeDtypeStruct(o_shape, out_dtype),
        scratch_shapes=[pltpu.VMEM(acc_shape, F32)],
        compiler_params=_params(("parallel", "parallel", "arbitrary")),
    )(*args)


def _rms_fwd(h, g, *, name):
    tp, d = h.shape
    tr = _tile(tp, 512, SUBLANES)

    def body(h_ref, g_ref, o_ref):
        hv = h_ref[...]
        rstd = lax.rsqrt(jnp.mean(hv * hv, axis=-1, keepdims=True) + RMS_EPS)
        o_ref[...] = (hv * rstd * g_ref[...]).astype(BF16)

    return pl.pallas_call(
        body, name=name, grid=(tp // tr,),
        in_specs=[pl.BlockSpec((tr, d), lambda i: (i, 0)), pl.BlockSpec((1, d), lambda i: (0, 0))],
        out_specs=pl.BlockSpec((tr, d), lambda i: (i, 0)),
        out_shape=jax.ShapeDtypeStruct((tp, d), BF16),
        compiler_params=_params(("parallel",)),
    )(h, g.reshape(1, d))


def _rms_fwd_first(x, meta, g, *, tp, name):
    seq, d = x.shape
    n_meta = meta.shape[0]
    n_tok = n_meta + seq
    tr = SHIFTED_ROWS
    assert tp % tr == 0 and seq % tr == 0 and tr % n_meta == 0
    per = tr // n_meta

    def body(x_ref, xp_ref, m_ref, g_ref, h_ref, o_ref):
        i = pl.program_id(0)
        head = jnp.where(i == 0, m_ref[...], xp_ref[...])
        rows = i * tr + lax.broadcasted_iota(jnp.int32, (tr, 1), 0)
        hv = jnp.where(rows < n_tok, jnp.concatenate([head, x_ref[:tr - n_meta, :]], axis=0), 0.0)
        h_ref[...] = hv
        rstd = lax.rsqrt(jnp.mean(hv * hv, axis=-1, keepdims=True) + RMS_EPS)
        o_ref[...] = (hv * rstd * g_ref[...]).astype(BF16)

    row = pl.BlockSpec((tr, d), lambda i: (i, 0))
    own = pl.BlockSpec((tr, d), lambda i: (jnp.minimum(i, seq // tr - 1), 0))
    before = pl.BlockSpec((n_meta, d), lambda i: (jnp.maximum(i * per - 1, 0), 0))
    return pl.pallas_call(
        body, name=name, grid=(tp // tr,),
        in_specs=[own, before, pl.BlockSpec((n_meta, d), lambda i: (0, 0)), pl.BlockSpec((1, d), lambda i: (0, 0))],
        out_specs=[row, row],
        out_shape=[jax.ShapeDtypeStruct((tp, d), F32), jax.ShapeDtypeStruct((tp, d), BF16)],
        compiler_params=_params(("parallel",)),
    )(x, x, meta, g.reshape(1, d))


def _rms_bwd(h, dhn, dout, g, *, name):
    tp, d = h.shape
    tr = _tile(tp, 528, 2 * SUBLANES)

    def body(h_ref, dhn_ref, dout_ref, g_ref, dh_ref, dhb_ref, dg_ref):
        hv = h_ref[...]
        rstd = lax.rsqrt(jnp.mean(hv * hv, axis=-1, keepdims=True) + RMS_EPS)
        xhat = hv * rstd
        dn = dhn_ref[...]
        dxhat = dn * g_ref[...]
        dh = dout_ref[...] + rstd * (dxhat - xhat * jnp.mean(dxhat * xhat, axis=-1, keepdims=True))
        dh_ref[...] = dh
        dhb_ref[...] = dh.astype(BF16)
        part = jnp.sum(dn * xhat, axis=0, keepdims=True)

        @pl.when(pl.program_id(0) == 0)
        def _():
            dg_ref[...] = part

        @pl.when(pl.program_id(0) > 0)
        def _():
            dg_ref[...] += part

    row = pl.BlockSpec((tr, d), lambda i: (i, 0))
    vec = pl.BlockSpec((1, d), lambda i: (0, 0))
    return pl.pallas_call(
        body, name=name, grid=(tp // tr,),
        in_specs=[row, row, row, vec],
        out_specs=[row, row, vec],
        out_shape=[jax.ShapeDtypeStruct((tp, d), F32), jax.ShapeDtypeStruct((tp, d), BF16),
                   jax.ShapeDtypeStruct((1, d), F32)],
        compiler_params=_params(("arbitrary",)),
    )(h, dhn, dout, g.reshape(1, d))


def _rms_bwd_first(h, dhn, dout, g, *, n_meta, seq, name):
    tp, d = h.shape
    tr = SHIFTED_ROWS
    assert seq % tr == 0 and tr % n_meta == 0 and tp >= seq + n_meta
    nt = seq // tr
    per = tr // n_meta

    def grads(hv, dn, do, gv):
        rstd = lax.rsqrt(jnp.mean(hv * hv, axis=-1, keepdims=True) + RMS_EPS)
        xhat = hv * rstd
        dxhat = dn * gv
        dh = do + rstd * (dxhat - xhat * jnp.mean(dxhat * xhat, axis=-1, keepdims=True))
        return dh, jnp.sum(dn * xhat, axis=0, keepdims=True)

    def body(h_ref, dhn_ref, dout_ref, hn_ref, dhnn_ref, doutn_ref, g_ref, gx_ref, dmeta_ref, dg_ref):
        i = pl.program_id(0)
        gv = g_ref[...]
        dh, part = grads(h_ref[...], dhn_ref[...], dout_ref[...], gv)
        dh_next, part_next = grads(hn_ref[...], dhnn_ref[...], doutn_ref[...], gv)
        gx_ref[...] = jnp.concatenate([dh[n_meta:], dh_next], axis=0)

        @pl.when(i == 0)
        def _():
            dmeta_ref[...] = dh[:n_meta]
            dg_ref[...] = part

        @pl.when(i > 0)
        def _():
            dg_ref[...] += part

        @pl.when(i == nt - 1)
        def _():
            dg_ref[...] += part_next

    row = pl.BlockSpec((tr, d), lambda i: (i, 0))
    nxt = pl.BlockSpec((n_meta, d), lambda i: ((i + 1) * per, 0))
    vec = pl.BlockSpec((1, d), lambda i: (0, 0))
    return pl.pallas_call(
        body, name=name, grid=(nt,),
        in_specs=[row, row, row, nxt, nxt, nxt, vec],
        out_specs=[pl.BlockSpec((None, tr, d), lambda i: (0, i, 0)), pl.BlockSpec((n_meta, d), lambda i: (0, 0)), vec],
        out_shape=[jax.ShapeDtypeStruct((1, seq, d), F32), jax.ShapeDtypeStruct((n_meta, d), F32),
                   jax.ShapeDtypeStruct((1, d), F32)],
        compiler_params=_params(("arbitrary",)),
    )(h, dhn, dout, h, dhn, dout, g.reshape(1, d))


def _loss_head(h, tgt, g, *, n_meta, n_tok, name):
    tp, d = h.shape
    seq = tgt.shape[0]
    tr = SHIFTED_ROWS
    assert tp % tr == 0 and seq % tr == 0 and tr % n_meta == 0
    per = tr // n_meta

    def body(h_ref, t_ref, tp_ref, g_ref, dh_ref, dhb_ref, dg_ref, loss_ref):
        i = pl.program_id(0)
        hv = h_ref[...]
        rstd = lax.rsqrt(jnp.mean(hv * hv, axis=-1, keepdims=True) + RMS_EPS)
        xhat = hv * rstd
        gv = g_ref[...]
        rows = i * tr + lax.broadcasted_iota(jnp.int32, (tr, 1), 0)
        valid = jnp.logical_and(rows >= n_meta, rows < n_tok)
        target = jnp.concatenate([tp_ref[...], t_ref[:tr - n_meta, :]], axis=0)
        err = jnp.where(valid, xhat * gv - target, 0.0)
        dy = err * (1.0 / d)
        dxhat = dy * gv
        dh = rstd * (dxhat - xhat * jnp.mean(dxhat * xhat, axis=-1, keepdims=True))
        dh_ref[...] = dh
        dhb_ref[...] = dh.astype(BF16)
        dg_part = jnp.sum(dy * xhat, axis=0, keepdims=True)
        per_row = jnp.sum(err * err, axis=-1, keepdims=True) * (1.0 / d)
        loss_part = jnp.broadcast_to(0.5 * jnp.sum(per_row, axis=0, keepdims=True), (SUBLANES, LANES))

        @pl.when(i == 0)
        def _():
            dg_ref[...] = dg_part
            loss_ref[...] = loss_part

        @pl.when(i > 0)
        def _():
            dg_ref[...] += dg_part
            loss_ref[...] += loss_part

    row = pl.BlockSpec((tr, d), lambda i: (i, 0))
    vec = pl.BlockSpec((1, d), lambda i: (0, 0))
    own = pl.BlockSpec((tr, d), lambda i: (jnp.minimum(i, seq // tr - 1), 0))
    before = pl.BlockSpec((n_meta, d), lambda i: (jnp.maximum(i * per - 1, 0), 0))
    return pl.pallas_call(
        body, name=name, grid=(tp // tr,),
        in_specs=[row, own, before, vec],
        out_specs=[row, row, vec, pl.BlockSpec((SUBLANES, LANES), lambda i: (0, 0))],
        out_shape=[jax.ShapeDtypeStruct((tp, d), F32), jax.ShapeDtypeStruct((tp, d), BF16),
                   jax.ShapeDtypeStruct((1, d), F32), jax.ShapeDtypeStruct((SUBLANES, LANES), F32)],
        compiler_params=_params(("arbitrary",)),
    )(h, tgt, tgt, g.reshape(1, d))


def _shift_down(halo, tile, s):
    if s == 0:
        return tile
    ext = jnp.concatenate([halo, tile], axis=0)
    return pltpu.roll(ext, s, 0)[SUBLANES:]


def _shift_up(tile, head, s):
    if s == 0:
        return tile
    ext = jnp.concatenate([tile, head], axis=0)
    n = ext.shape[0]
    return pltpu.roll(ext, n - s, 0)[: tile.shape[0]]


def _to_lane_blocks(ref, cols, val):
    for j in range(cols.start // LANES, cols.stop // LANES):
        ref[j] = val[:, j * LANES - cols.start:(j + 1) * LANES - cols.start]


def _from_lane_blocks(ref, cols):
    return jnp.concatenate([ref[j] for j in range(cols.start // LANES, cols.stop // LANES)], axis=1)


def _scan_tile(a_ref, b_ref, out_ref, carry, j, *, reverse):
    ng = a_ref.shape[1] // SUBLANES
    order = list(range(SUBLANES))[::-1] if reverse else list(range(SUBLANES))

    def rows(r):
        return pl.ds(r, ng, stride=SUBLANES)

    prod, loc = {}, {}
    prev = None
    for r in order:
        ar = a_ref[j, rows(r), :]
        br = b_ref[j, rows(r), :]
        prod[r] = ar if prev is None else ar * prod[prev]
        loc[r] = br if prev is None else ar * loc[prev] + br
        prev = r
    pg, lg = prod[prev], loc[prev]
    ones = jnp.ones((SUBLANES,) + pg.shape[1:], F32)
    zeros = jnp.zeros_like(ones)
    s = 1
    while s < ng:
        p_sh = _shift_up(pg, ones, s) if reverse else _shift_down(ones, pg, s)
        l_sh = _shift_up(lg, zeros, s) if reverse else _shift_down(zeros, lg, s)
        lg = pg * l_sh + lg
        pg = pg * p_sh
        s *= 2
    leaving = pg * carry[0:1, :] + lg
    entering = _shift_up(leaving, carry, 1) if reverse else _shift_down(carry, leaving, 1)
    for r in order:
        out_ref[j, rows(r), :] = loc[r] + prod[r] * entering
    last = leaving[0:1, :] if reverse else leaving[ng - 1:ng, :]
    return jnp.broadcast_to(last, carry.shape)


def _gates(ca, wr, wi, br, bi, sp):
    cab = ca.astype(BF16)
    r = _sigmoid(jnp.dot(cab, wr, preferred_element_type=F32) + br)
    ig = _sigmoid(jnp.dot(cab, wi, preferred_element_type=F32) + bi)
    la = -LRU_C * r * sp
    a = jnp.exp(la)
    mult = jnp.sqrt(-jnp.tanh(la) * (a * a + 1.0))
    return r, ig, a, mult


def _mixer_fwd(u, wa, ba, wr_blk, br, wi_blk, bi, lam, wb, *, name):
    tp, din = u.shape
    dl = din // 6
    tt = MIX_ROWS
    cw = GATE_BLOCK
    nch = dl // cw
    assert tp % tt == 0 and dl % cw == 0

    def body(u_ref, wa_ref, ba_ref, wr_ref, br_ref, wi_ref, bi_ref, lam_ref, wb_ref,
             s_ref, y_ref, xa_tail, v_tail, h_carry, a_s, b_s, h_s):
        @pl.when(pl.program_id(0) == 0)
        def _():
            xa_tail[...] = jnp.zeros_like(xa_tail)
            v_tail[...] = jnp.zeros_like(v_tail)
            h_carry[...] = jnp.zeros_like(h_carry)

        for ch in range(nch):
            cs = slice(ch * cw, (ch + 1) * cw)

            def seg(s):
                return slice(s * dl + ch * cw, s * dl + (ch + 1) * cw)

            xa = u_ref[:, seg(0)]
            halo = xa_tail[:, cs]
            ca = ba_ref[:, cs] + wa_ref[3:4, cs] * xa
            for kk in range(3):
                ca = ca + wa_ref[kk:kk + 1, cs] * _shift_down(halo, xa, 3 - kk)
            xa_tail[:, cs] = xa[tt - SUBLANES:]
            s_ref[:, cs] = ca
            sp = _softplus(-lam_ref[:, cs])
            _, ig, a, mult = _gates(ca, wr_ref[ch], wi_ref[ch], br_ref[:, cs], bi_ref[:, cs], sp)
            _to_lane_blocks(a_s, cs, a)
            _to_lane_blocks(b_s, cs, mult * (ig * ca))

            bv = u_ref[:, seg(2)]
            v = u_ref[:, seg(3)] * u_ref[:, seg(4)]
            gb = u_ref[:, seg(5)]
            vh = v_tail[:, cs]
            cb = wb_ref[2:3, cs] * v
            for kk in range(2):
                cb = cb + wb_ref[kk:kk + 1, cs] * _shift_down(vh, v, 2 - kk)
            v_tail[:, cs] = v[tt - SUBLANES:]
            y_ref[:, dl + ch * cw: dl + (ch + 1) * cw] = (bv * cb * (gb * _sigmoid(gb))).astype(BF16)

        for ch in range(nch):
            cs = slice(ch * cw, (ch + 1) * cw)
            for j in range(cs.start // LANES, cs.stop // LANES):
                lanes = slice(j * LANES, (j + 1) * LANES)
                h_carry[:, lanes] = _scan_tile(a_s, b_s, h_s, h_carry[:, lanes], j, reverse=False)
            hsv = _from_lane_blocks(h_s, cs)
            s_ref[:, dl + ch * cw: dl + (ch + 1) * cw] = hsv
            ga = u_ref[:, dl + ch * cw: dl + (ch + 1) * cw]
            y_ref[:, cs] = (hsv * (ga * _sigmoid(ga))).astype(BF16)

    row = lambda w: pl.BlockSpec((tt, w), lambda i: (i, 0))
    full = lambda shp: pl.BlockSpec(shp, lambda i: tuple(0 for _ in shp))
    return pl.pallas_call(
        body, name=name, grid=(tp // tt,),
        in_specs=[row(din), full((4, dl)), full((1, dl)), full((nch, cw, cw)), full((1, dl)),
                  full((nch, cw, cw)), full((1, dl)), full((1, dl)), full((3, dl))],
        out_specs=[row(2 * dl), row(2 * dl)],
        out_shape=[jax.ShapeDtypeStruct((tp, 2 * dl), F32), jax.ShapeDtypeStruct((tp, 2 * dl), BF16)],
        scratch_shapes=[pltpu.VMEM((SUBLANES, dl), F32), pltpu.VMEM((SUBLANES, dl), F32),
                        pltpu.VMEM((SUBLANES, dl), F32)] + [pltpu.VMEM((dl // LANES, tt, LANES), F32)] * 3,
        compiler_params=_params(("arbitrary",)),
    )(u, wa, ba, wr_blk, br, wi_blk, bi, lam, wb)


SG_WA, SG_BA, SG_BR, SG_BI, SG_LAM, SG_WB, SG_ROWS = 0, 4, 5, 6, 7, 8, 16


def _mixer_bwd(u, saved, dy, wa, wr_blk, br, wi_blk, bi, lam, wb, *, name):
    tp, din = u.shape
    dl = din // 6
    tt = MIX_ROWS
    cw = GATE_BLOCK
    nch = dl // cw
    nt = tp // tt
    hb = tt // SUBLANES
    tn_dims = (((0,), (0,)), ((), ()))
    nt_dims = (((1,), (1,)), ((), ()))

    def body(u_ref, uh_ref, s_ref, sh_ref, dy_ref, wa_ref, wr_ref, br_ref, wi_ref, bi_ref, lam_ref, wb_ref,
             du_ref, sg_ref, dwr_ref, dwi_ref,
             g_carry, a_head, dca_head, dcb_head, r_s, i_s, a_s, an_s, d_s, g_s):
        i = pl.program_id(0)
        first_tile = i == nt - 1

        @pl.when(i == 0)
        def _():
            for ref in (g_carry, a_head, dca_head, dcb_head, sg_ref, dwr_ref, dwi_ref):
                ref[...] = jnp.zeros_like(ref)

        def halo_of(x):
            return jnp.where(first_tile, 0.0, x)

        for ch in range(nch):
            cs = slice(ch * cw, (ch + 1) * cw)
            cav = s_ref[:, cs]
            sp = _softplus(-lam_ref[:, cs])
            r, ig, a, _ = _gates(cav, wr_ref[ch], wi_ref[ch], br_ref[:, cs], bi_ref[:, cs], sp)
            r_s[:, cs] = r
            i_s[:, cs] = ig
            a_s[:, cs] = a
            _to_lane_blocks(an_s, cs, _shift_up(a, a_head[:, cs], 1))
            a_head[:, cs] = a[:SUBLANES]
            ga = u_ref[:, dl + ch * cw: dl + (ch + 1) * cw]
            _to_lane_blocks(d_s, cs, dy_ref[:, cs] * (ga * _sigmoid(ga)))

        for j in range(dl // LANES):
            lanes = slice(j * LANES, (j + 1) * LANES)
            g_carry[:, lanes] = _scan_tile(an_s, d_s, g_s, g_carry[:, lanes], j, reverse=True)

        def acc_row(r0, val):
            sg_ref[r0:r0 + 1, cs_cur[0]] += jnp.sum(val, axis=0, keepdims=True)

        cs_cur = [None]
        for ch in range(nch):
            cs = slice(ch * cw, (ch + 1) * cw)
            cs_cur[0] = cs

            def seg(s):
                return slice(s * dl + ch * cw, s * dl + (ch + 1) * cw)

            cav = s_ref[:, cs]
            r = r_s[:, cs]
            ig = i_s[:, cs]
            a = a_s[:, cs]
            g = _from_lane_blocks(g_s, cs)
            hsv = s_ref[:, dl + ch * cw: dl + (ch + 1) * cw]
            lamv = lam_ref[:, cs]
            sp = _softplus(-lamv)
            la = -LRU_C * r * sp
            e2 = a * a
            one_m_e2 = -jnp.tanh(la) * (e2 + 1.0)
            mult = jnp.sqrt(one_m_e2)
            hprev = _shift_down(halo_of(sh_ref[:, dl + ch * cw: dl + (ch + 1) * cw]), hsv, 1)
            icav = ig * cav
            dla = g * (hprev * a - icav * (e2 * lax.rsqrt(one_m_e2)))
            gm = g * mult
            dzi = gm * icav * (1.0 - ig)
            dca = gm * ig
            dla_r = dla * r
            dzr = dla_r * (1.0 - r) * (-LRU_C * sp)
            sg_ref[SG_LAM:SG_LAM + 1, cs] += jnp.sum(dla_r, axis=0, keepdims=True) * (LRU_C * _sigmoid(-lamv))
            acc_row(SG_BR, dzr)
            acc_row(SG_BI, dzi)
            dzr_b = dzr.astype(BF16)
            dzi_b = dzi.astype(BF16)
            cab = cav.astype(BF16)
            dca = dca + lax.dot_general(dzr_b, wr_ref[ch], nt_dims, preferred_element_type=F32)
            dca = dca + lax.dot_general(dzi_b, wi_ref[ch], nt_dims, preferred_element_type=F32)
            dwr_ref[ch] += lax.dot_general(cab, dzr_b, tn_dims, preferred_element_type=F32)
            dwi_ref[ch] += lax.dot_general(cab, dzi_b, tn_dims, preferred_element_type=F32)
            acc_row(SG_BA, dca)
            xa = u_ref[:, seg(0)]
            head = dca_head[:, cs]
            dxa = wa_ref[3:4, cs] * dca
            acc_row(SG_WA + 3, dca * xa)
            for kk in range(3):
                later = _shift_up(dca, head, 3 - kk)
                acc_row(SG_WA + kk, later * xa)
                dxa = dxa + wa_ref[kk:kk + 1, cs] * later
            dca_head[:, cs] = dca[:SUBLANES]
            ga = u_ref[:, seg(1)]
            sga = _sigmoid(ga)
            dga = dy_ref[:, cs] * hsv * (sga + (ga * sga) * (1.0 - sga))
            du_ref[:, seg(0)] = dxa.astype(BF16)
            du_ref[:, seg(1)] = dga.astype(BF16)

            bv = u_ref[:, seg(2)]
            cv = u_ref[:, seg(3)]
            xb = u_ref[:, seg(4)]
            gb = u_ref[:, seg(5)]
            dyb = dy_ref[:, dl + ch * cw: dl + (ch + 1) * cw]
            v = cv * xb
            vh = halo_of(uh_ref[:, seg(3)] * uh_ref[:, seg(4)])
            v1 = _shift_down(vh, v, 1)
            v2 = _shift_down(vh, v, 2)
            cb = wb_ref[2:3, cs] * v + wb_ref[1:2, cs] * v1 + wb_ref[0:1, cs] * v2
            sgb = _sigmoid(gb)
            sl = gb * sgb
            dyb_b = dyb * bv
            dyb_cb = dyb * cb
            dcb = dyb_b * sl
            du_ref[:, seg(2)] = (dyb_cb * sl).astype(BF16)
            du_ref[:, seg(5)] = (dyb_cb * bv * (sgb + sl * (1.0 - sgb))).astype(BF16)
            bhead = dcb_head[:, cs]
            dv = wb_ref[2:3, cs] * dcb
            acc_row(SG_WB + 2, dcb * v)
            for kk in range(2):
                later = _shift_up(dcb, bhead, 2 - kk)
                acc_row(SG_WB + kk, later * v)
                dv = dv + wb_ref[kk:kk + 1, cs] * later
            dcb_head[:, cs] = dcb[:SUBLANES]
            du_ref[:, seg(3)] = (dv * xb).astype(BF16)
            du_ref[:, seg(4)] = (dv * cv).astype(BF16)

    rev = lambda w: pl.BlockSpec((tt, w), lambda i: (nt - 1 - i, 0))
    halo = lambda w: pl.BlockSpec((SUBLANES, w), lambda i: (jnp.maximum((nt - 1 - i) * hb - 1, 0), 0))
    full = lambda shp: pl.BlockSpec(shp, lambda i: tuple(0 for _ in shp))
    vm = lambda r: pltpu.VMEM((r, dl), F32)
    return pl.pallas_call(
        body, name=name, grid=(nt,),
        in_specs=[rev(din), halo(din), rev(2 * dl), halo(2 * dl), rev(2 * dl), full((4, dl)),
                  full((nch, cw, cw)), full((1, dl)), full((nch, cw, cw)), full((1, dl)), full((1, dl)), full((3, dl))],
        out_specs=[rev(din), full((SG_ROWS, dl)), full((nch, cw, cw)), full((nch, cw, cw))],
        out_shape=[jax.ShapeDtypeStruct((tp, din), BF16), jax.ShapeDtypeStruct((SG_ROWS, dl), F32),
                   jax.ShapeDtypeStruct((nch, cw, cw), F32), jax.ShapeDtypeStruct((nch, cw, cw), F32)],
        scratch_shapes=[vm(SUBLANES), vm(SUBLANES), vm(SUBLANES), vm(SUBLANES), vm(tt), vm(tt), vm(tt)]
        + [pltpu.VMEM((dl // LANES, tt, LANES), F32)] * 3,
        compiler_params=_params(("arbitrary",)),
    )(u, u, saved, saved, dy, wa, wr_blk, br, wi_blk, bi, lam, wb)


def _adamw(w, g, m, v, *, name, landed=None, layer=None, depth=None, into=None, row_off=0):
    r, c = w.shape[-2:]
    rows = g.shape[0]
    tr = _tile(rows, 512, 2 * SUBLANES)
    assert row_off % tr == 0
    boff = row_off // tr
    bc1 = 1.0 - ADAM_B1 ** ADAM_STEP
    bc2 = 1.0 - ADAM_B2 ** ADAM_STEP
    slots = landed is not None

    def body(*refs):
        if into is not None:
            refs = refs[:-8] + refs[-4:]
        if slots:
            w_ref, g_ref, l_ref, m_ref, v_ref, grad_ref, delta_ref, nm_ref, nv_ref = refs
            gv = g_ref[...].astype(F32)
            for s in range(N_DEV - 1):
                gv = gv + l_ref[s].astype(F32)
        else:
            w_ref, g_ref, m_ref, v_ref, grad_ref, delta_ref, nm_ref, nv_ref = refs
            gv = g_ref[...]
        wv = w_ref[...]
        mn = ADAM_B1 * m_ref[...] + (1.0 - ADAM_B1) * gv
        vn = ADAM_B2 * v_ref[...] + (1.0 - ADAM_B2) * (gv * gv)
        m_hat = mn / bc1
        v_hat = vn / bc2
        grad_ref[...] = gv
        delta_ref[...] = -ADAM_LR * (m_hat / (jnp.sqrt(v_hat) + ADAM_EPS) + ADAM_WD * wv)
        nm_ref[...] = mn
        nv_ref[...] = vn

    if depth is None:
        blk = pl.BlockSpec((tr, c), lambda i: (i + boff, 0))
    else:
        blk = pl.BlockSpec((None, tr, c), lambda i: (layer, i + boff, 0))
    g_blk = pl.BlockSpec((tr, c), lambda i: (i, 0))
    l_spec = [pl.BlockSpec((N_DEV - 1, tr, c), lambda i: (0, i, 0))] if slots else []
    args = (w, g, landed, m, v) if slots else (w, g, m, v)
    in_specs = [blk, g_blk] + l_spec + [blk, blk]
    if depth is None:
        shp = jax.ShapeDtypeStruct((r, c), F32)
        out_blk = blk
    else:
        shp = jax.ShapeDtypeStruct((depth, r, c), F32)
        out_blk = pl.BlockSpec((None, tr, c), lambda i: (layer, i + boff, 0))
    aliases = {}
    if into is not None:
        aliases = {len(args) + j: j for j in range(4)}
        in_specs = in_specs + [ANY] * 4
        args = args + tuple(into)
    return pl.pallas_call(
        body, name=name, grid=(rows // tr,),
        in_specs=in_specs, out_specs=[out_blk] * 4,
        out_shape=[shp] * 4, input_output_aliases=aliases,
        compiler_params=_params(("parallel",)),
    )(*args)


def _slot_sum(g, *, name):
    _, r, c = g.shape
    tr = _tile(r, 512, SUBLANES)

    def body(g_ref, o_ref):
        gv = g_ref[0].astype(F32)
        for s in range(1, N_DEV):
            gv = gv + g_ref[s].astype(F32)
        o_ref[...] = gv

    return pl.pallas_call(
        body, name=name, grid=(r // tr,),
        in_specs=[pl.BlockSpec((N_DEV, tr, c), lambda i: (0, i, 0))],
        out_specs=pl.BlockSpec((tr, c), lambda i: (i, 0)),
        out_shape=jax.ShapeDtypeStruct((r, c), F32),
        compiler_params=_params(("parallel",)),
    )(g)


def _mesh_pos():
    x, y, c = lax.axis_index("x"), lax.axis_index("y"), lax.axis_index("c")
    return x, y, c, 4 * x + 2 * y + c


ANY = pl.BlockSpec(memory_space=pl.ANY)


GATHER_COPIES = 9


def _all_gather(srcs, out_shapes, views, *, name):
    n = len(srcs)
    SIB, X_OWN, Y_OWN, X_DIAG, Y_DIAG, SIB_X, SIB_Y, SIB_DIAG_TOP, SIB_DIAG_BOTTOM = range(GATHER_COPIES)

    def body(*refs):
        src = refs[:n]
        dst = refs[n:2 * n]
        send_sems, recv_sems, local_sems = refs[2 * n:]
        x, y, c, me = _mesh_pos()
        sibling, x_nbr, y_nbr = (x, y, 1 - c), (1 - x, y, c), (x, 1 - y, c)

        def block(a, px, py, pc, half=None):
            win = views[a](dst[a], 4 * px + 2 * py + pc)
            if half is None:
                return win
            rows = win.shape[0] // 2
            return win.at[pl.ds(half * rows, rows)]

        def copy(a, k, win, to, from_src=False):
            return pltpu.make_async_remote_copy(
                src_ref=src[a] if from_src else win, dst_ref=win,
                send_sem=send_sems.at[a * GATHER_COPIES + k], recv_sem=recv_sems.at[a * GATHER_COPIES + k],
                device_id=to, device_id_type=MESH)

        mine = [pltpu.make_async_copy(src[a], block(a, x, y, c), local_sems.at[a]) for a in range(n)]
        started = []

        def start(cp):
            cp.start()
            started.append(cp)

        for a in range(n):
            mine[a].start()
            own = block(a, x, y, c)
            start(copy(a, SIB, own, sibling, True))
            start(copy(a, X_OWN, own, x_nbr, True))
            start(copy(a, Y_OWN, own, y_nbr, True))
        for a in range(n):
            from_y = block(a, x, 1 - y, c)
            copy(a, Y_OWN, from_y, y_nbr).wait_recv()
            start(copy(a, X_DIAG, block(a, x, 1 - y, c, 0), x_nbr))
            start(copy(a, SIB_Y, from_y, sibling))
            from_x = block(a, 1 - x, y, c)
            copy(a, X_OWN, from_x, x_nbr).wait_recv()
            start(copy(a, Y_DIAG, block(a, 1 - x, y, c, 1), y_nbr))
            start(copy(a, SIB_X, from_x, sibling))
        for a in range(n):
            top = block(a, 1 - x, 1 - y, c, 0)
            copy(a, X_DIAG, top, x_nbr).wait_recv()
            start(copy(a, SIB_DIAG_TOP, top, sibling))
            bottom = block(a, 1 - x, 1 - y, c, 1)
            copy(a, Y_DIAG, bottom, y_nbr).wait_recv()
            start(copy(a, SIB_DIAG_BOTTOM, bottom, sibling))
        for a in range(n):
            copy(a, SIB, block(a, x, y, 1 - c), sibling).wait_recv()
            copy(a, SIB_X, block(a, 1 - x, y, 1 - c), sibling).wait_recv()
            copy(a, SIB_Y, block(a, x, 1 - y, 1 - c), sibling).wait_recv()
            copy(a, SIB_DIAG_TOP, block(a, 1 - x, 1 - y, 1 - c, 0), sibling).wait_recv()
            copy(a, SIB_DIAG_BOTTOM, block(a, 1 - x, 1 - y, 1 - c, 1), sibling).wait_recv()
        for cp in started:
            cp.wait_send()
        for cp in mine:
            cp.wait()

    return pl.pallas_call(
        body, name=name,
        in_specs=[ANY] * n, out_specs=[ANY] * n,
        out_shape=[jax.ShapeDtypeStruct(s, x.dtype) for s, x in zip(out_shapes, srcs)],
        scratch_shapes=[pltpu.SemaphoreType.DMA((GATHER_COPIES * n,)), pltpu.SemaphoreType.DMA((GATHER_COPIES * n,)),
                        pltpu.SemaphoreType.DMA((n,))],
    )(*srcs)


HBM = pl.BlockSpec(memory_space=pltpu.HBM)
SEM = pl.BlockSpec(memory_space=pltpu.SEMAPHORE)
EFFECT = pltpu.SideEffectType.DATAFLOW_SIDE_EFFECTING


def _peer_of(x, y, c, k):
    return (1 - x if k & 4 else x, 1 - y if k & 2 else y, 1 - c if k & 1 else c)


def _peer_copies(n, wins, src, land, send_sems, recv_sems):
    x, y, c, me = _mesh_pos()
    out = []
    for a in range(n):
        for k in range(1, N_DEV):
            px, py, pc = _peer_of(x, y, c, k)
            s_win, d_win = wins[a](src[a], land[a], me, 4 * px + 2 * py + pc, k)
            out.append(pltpu.make_async_remote_copy(
                src_ref=s_win, dst_ref=d_win,
                send_sem=send_sems.at[a * 7 + k - 1], recv_sem=recv_sems.at[a * 7 + k - 1],
                device_id=(px, py, pc), device_id_type=MESH))
    return out


def _push_start(srcs, lands, wins, *, name):
    n = len(srcs)

    def body(*refs):
        src = refs[:n]
        land = refs[n:2 * n]
        send_sems, recv_sems = refs[2 * n], refs[2 * n + 1]
        token = refs[-1]
        for cp in _peer_copies(n, wins, src, land, send_sems, recv_sems):
            cp.start()
        token[...] = jnp.zeros_like(token)

    bufs = (*srcs, *lands)
    return pl.pallas_call(
        body, name=name,
        out_shape=(pltpu.SemaphoreType.DMA((7 * n,)), pltpu.SemaphoreType.DMA((7 * n,)),
                   *[pltpu.HBM(v.shape, v.dtype) for v in bufs], jax.ShapeDtypeStruct((SUBLANES, LANES), F32)),
        in_specs=[HBM] * (2 * n),
        out_specs=(SEM, SEM, *[HBM] * (2 * n), pl.BlockSpec(memory_space=pltpu.VMEM)),
        input_output_aliases={i: 2 + i for i in range(2 * n)},
        compiler_params=pltpu.CompilerParams(has_side_effects=EFFECT),
    )(*[pltpu.with_memory_space_constraint(v, pltpu.HBM) for v in bufs])


def _push_wait(handle, wins, after, *, name):
    send_sems, recv_sems, *bufs, _ = handle
    n = len(bufs) // 2

    def body(*refs):
        src = refs[:n]
        land = refs[n:2 * n]
        for cp in _peer_copies(n, wins, src, land, refs[2 * n], refs[2 * n + 1]):
            cp.wait_send()
            cp.wait_recv()

    outs = pl.pallas_call(
        body, name=name,
        out_shape=tuple(pltpu.HBM(v.shape, v.dtype) for v in bufs),
        in_specs=[HBM] * (2 * n) + [SEM, SEM, ANY],
        out_specs=tuple([HBM] * (2 * n)),
        input_output_aliases={i: i for i in range(2 * n)},
        compiler_params=pltpu.CompilerParams(has_side_effects=EFFECT),
    )(*bufs, send_sems, recv_sems, after)
    return outs[:n], outs[n:]


def _gather_lead(src, land, me, peer, k):
    return src, land.at[me]


def _gather_cols(width):
    def win(src, land, me, peer, k):
        return src, land.at[:, pl.ds(me * width, width)]
    return win


def _scatter_lead(src, land, me, peer, k):
    return src.at[peer], land.at[k - 1]


def _scatter_cols(width):
    def win(src, land, me, peer, k):
        return src.at[:, pl.ds(peer * width, width)], land.at[k - 1]
    return win


def _place_block(own, *, cols, name):
    rows, width = own.shape
    tr = _tile(rows, 512, 2 * SUBLANES)
    _, _, _, me = _mesh_pos()

    def body(me_ref, x_ref, o_ref):
        o_ref[...] = x_ref[...]

    if cols:
        out_spec = pl.BlockSpec((tr, width), lambda i, me_ref: (i, me_ref[0]))
        shape = (rows, N_DEV * width)
    else:
        out_spec = pl.BlockSpec((None, tr, width), lambda i, me_ref: (me_ref[0], i, 0))
        shape = (N_DEV, rows, width)
    return pl.pallas_call(
        body, name=name,
        grid_spec=pltpu.PrefetchScalarGridSpec(
            num_scalar_prefetch=1, grid=(rows // tr,),
            in_specs=[pl.BlockSpec((tr, width), lambda i, me_ref: (i, 0))], out_specs=out_spec),
        out_shape=jax.ShapeDtypeStruct(shape, own.dtype),
        compiler_params=_params(("arbitrary",)),
    )(me.astype(jnp.int32).reshape(1), own)


def _dep(x, token):
    return x + token[0, 0].astype(x.dtype)


def _lead(ref, d):
    return ref.at[d]


def _col_window(width):
    def view(ref, d):
        return ref.at[:, pl.ds(d * width, width)]
    return view


def _pack(arrs):
    flat = jnp.concatenate([a.reshape(-1).astype(F32) for a in arrs])
    n = flat.shape[0]
    rows = -(-n // (2 * SUBLANES * LANES)) * 2 * SUBLANES
    return jnp.pad(flat, (0, rows * LANES - n)).reshape(rows, LANES)


def _unpack(buf, shapes):
    flat = buf.reshape(-1)
    out, off = [], 0
    for s in shapes:
        n = 1
        for q in s:
            n *= q
        out.append(flat[off:off + n].reshape(s))
        off += n
    return out


def _blockdiag(w, cw):
    h, hd, _ = w.shape
    per = cw // hd
    wg = w.reshape(h // per, per, hd, hd)
    eye = jnp.eye(per, dtype=w.dtype)
    blk = jnp.einsum("gpij,pq->gpiqj", wg, eye)
    return blk.reshape(h // per, cw, cw).astype(BF16)


def _blockdiag_extract(g, hd):
    n, cw, _ = g.shape
    per = cw // hd
    g5 = g.reshape(n, per, hd, per, hd)
    idx = jnp.arange(per)
    return g5[:, idx, :, idx, :].transpose(1, 0, 2, 3).reshape(n * per, hd, hd)


def kernel(x, meta, norm_g, w_in, conv_a_w, conv_a_b, lru_wr, lru_br, lru_wi, lru_bi, lru_lambda, conv_b_w, w_out, final_g, loss_target, m_meta, m_norm_g, m_w_in, m_conv_a_w, m_conv_a_b, m_lru_wr, m_lru_br, m_lru_wi, m_lru_bi, m_lru_lambda, m_conv_b_w, m_w_out, m_final_g, v_meta, v_norm_g, v_w_in, v_conv_a_w, v_conv_a_b, v_lru_wr, v_lru_br, v_lru_wi, v_lru_bi, v_lru_lambda, v_conv_b_w, v_w_out, v_final_g):
    _, seq, d = x.shape
    n_meta = meta.shape[0]
    depth = w_in.shape[0]
    din = w_in.shape[2] * N_DEV
    dl = din // 6
    dmix = 2 * dl
    wcol = w_in.shape[2]
    wrow = w_out.shape[1]
    mcol = meta.shape[1]
    ccol = conv_a_w.shape[2]
    heads, hd = lru_wr.shape[1], lru_wr.shape[2]
    n_tok = n_meta + seq
    tp = -(-n_tok // TOKEN_TILE) * TOKEN_TILE
    me = 4 * lax.axis_index("x") + 2 * lax.axis_index("y") + lax.axis_index("c")

    bf = lambda a: a.astype(BF16)
    small_mine = _pack([meta, conv_a_w, conv_b_w])
    first = _all_gather([bf(w_in[0]), small_mine], [(d, din), (N_DEV,) + small_mine.shape],
                        [_col_window(wcol), _lead], name="gather_first")
    parts = [_unpack(first[1][s], [meta.shape, conv_a_w.shape, conv_b_w.shape]) for s in range(N_DEV)]
    meta_full = jnp.concatenate([p[0] for p in parts], axis=1)
    wa_full = jnp.concatenate([p[1] for p in parts], axis=2)
    wb_full = jnp.concatenate([p[2] for p in parts], axis=2)
    w_in_full = [None] * depth
    w_out_full = [None] * depth

    push_out = [None] * depth
    push_in = [None] * depth
    w_in_full[0], src = lax.optimization_barrier((first[0], bf(w_out[0])))
    push_out[0] = _push_start([src], [_place_block(src, cols=False, name="place_wout_0")], [_gather_lead],
                              name="gather_wout_0_start")
    token = push_out[0][-1]
    for l in range(1, depth):
        src = bf(_dep(w_in[l], token))
        push_in[l] = _push_start([src], [_place_block(src, cols=True, name=f"place_win_{l}")], [_gather_cols(wcol)],
                                 name=f"gather_win_{l}_start")
        src = bf(_dep(w_out[l], push_in[l][-1]))
        push_out[l] = _push_start([src], [_place_block(src, cols=False, name=f"place_wout_{l}")], [_gather_lead],
                                  name=f"gather_wout_{l}_start")
        token = push_out[l][-1]

    wr_blk = [_blockdiag(lru_wr[l], GATE_BLOCK) for l in range(depth)]
    wi_blk = [_blockdiag(lru_wi[l], GATE_BLOCK) for l in range(depth)]
    vec = lambda a: a.reshape(1, dl)

    tm = _tile(tp, 1408)
    saved = []
    for l in range(depth):
        if l == 0:
            h, hn = _rms_fwd_first(x[0], meta_full, _dep(norm_g[l], token), tp=tp, name=f"rms_fwd_{l}")
        else:
            hn = _rms_fwd(h, norm_g[l], name=f"rms_fwd_{l}")
        if l > 0:
            _, landed = _push_wait(push_in[l], [_gather_cols(wcol)], hn, name=f"gather_win_{l}_wait")
            w_in_full[l] = landed[0]
        u = _matmul(hn, w_in_full[l], tm=tm, tn=_tile(din, 768), tk=d, name=f"mm_u_{l}")
        mixed, y = _mixer_fwd(u, wa_full[l], vec(conv_a_b[l]), wr_blk[l], vec(lru_br[l]), wi_blk[l], vec(lru_bi[l]),
                              vec(lru_lambda[l]), wb_full[l], name=f"mixer_fwd_{l}")
        _, landed = _push_wait(push_out[l], [_gather_lead], y, name=f"gather_wout_{l}_wait")
        w_out_full[l] = landed[0].reshape(dmix, d)
        h_next = _matmul(y, w_out_full[l], tm=tm, tn=_tile(d, 512), tk=dmix, add=h, name=f"mm_out_{l}")
        saved.append((h, hn, u, mixed, y))
        h = h_next

    dh, dhb, dg_final, loss_part = _loss_head(h, loss_target[0], final_g, n_meta=n_meta, n_tok=n_tok,
                                              name="loss_head")

    small_grads = [None] * depth
    sent_out = [None] * depth
    sent_in = [None] * depth
    scatter_in = [_scatter_cols(wcol)]
    token = None
    dg_norms = []
    for l in reversed(range(depth)):
        h_in, hn, u, mixed, y = saved[l]
        dy = _matmul(dhb, w_out_full[l], tb=True, tm=tm, tn=_tile(dmix, 512), tk=d, dep=token, name=f"mm_dy_{l}")
        dw_out = _matmul(y, dhb, ta=True, tm=_tile(dmix, 512), tn=_tile(d, 1024), tk=tp, out_dtype=BF16,
                         name=f"mm_dwout_{l}")
        sent_out[l] = _push_start([dw_out.reshape(N_DEV, wrow, d)], [lax.empty((N_DEV - 1, wrow, d), BF16)],
                                  [_scatter_lead], name=f"scatter_wout_{l}_start")
        du, sg, dwr, dwi = _mixer_bwd(u, mixed, dy, wa_full[l], wr_blk[l], vec(lru_br[l]), wi_blk[l], vec(lru_bi[l]),
                                      vec(lru_lambda[l]), _dep(wb_full[l], sent_out[l][-1]), name=f"mixer_bwd_{l}")
        small_grads[l] = (sg, dwr, dwi)
        if l == 0:
            early = [_pack([
                jnp.stack([small_grads[j][0][SG_BA] for j in range(depth)]),
                jnp.stack([small_grads[j][0][SG_BR] for j in range(depth)]),
                jnp.stack([small_grads[j][0][SG_BI] for j in range(depth)]),
                jnp.stack([small_grads[j][0][SG_LAM] for j in range(depth)]),
                jnp.stack([small_grads[j][0][SG_WA:SG_WA + 4] for j in range(depth)]),
                jnp.stack([small_grads[j][0][SG_WB:SG_WB + 3] for j in range(depth)]),
                dg_final[0], *dg_norms]),
                _pack([jnp.stack([_blockdiag_extract(small_grads[j][1], hd) for j in range(depth)]),
                       jnp.stack([_blockdiag_extract(small_grads[j][2], hd) for j in range(depth)])]).astype(BF16)]
            early_land = [lax.dynamic_update_slice(lax.empty((N_DEV,) + a.shape, a.dtype), a[None], (me, 0, 0))
                          for a in early]
            sent_early = _push_start(early, early_land, [_gather_lead] * 2, name="gather_early_grads_start")
        parts = 2 if l == 0 else 1
        token = sent_early[-1] if l == 0 else None
        sent_in[l] = []
        for p in range(parts):
            dw_in = _matmul(hn, du, ta=True, tm=_tile(d // parts, 512), tn=_tile(din, 768), tk=tp, out_dtype=BF16,
                            dep=token, m_part=(p, parts), name=f"mm_dwin_{l}_{p}")
            sent_in[l].append(_push_start([dw_in], [lax.empty((N_DEV - 1, d // parts, wcol), BF16)], scatter_in,
                                          name=f"scatter_win_{l}_{p}_start"))
            token = sent_in[l][-1][-1]
        dhn = _matmul(du, w_in_full[l], tb=True, tm=_tile(tp, 704, 2 * SUBLANES), tn=_tile(d, 512), tk=din, dep=token,
                      name=f"mm_dhn_{l}")
        if l > 0:
            dh, dhb, dg_norm = _rms_bwd(h_in, dhn, dh, norm_g[l], name=f"rms_bwd_{l}")
            dg_norms.append(dg_norm[0])
        else:
            grad_x, d_meta, dg_norm = _rms_bwd_first(h_in, dhn, dh, norm_g[l], n_meta=n_meta, seq=seq,
                                                     name=f"rms_bwd_{l}")

    late = _pack([dg_norm[0], d_meta, loss_part[0:1, 0:1]])
    late_all = _all_gather([late], [(N_DEV,) + late.shape], [_lead], name="gather_late_grads")[0]
    late_sum = _unpack(_slot_sum(late_all, name="sum_late_grads"), [(d,), (n_meta, d), ()])
    loss = late_sum[2]
    _, early_all = _push_wait(sent_early, [_gather_lead] * 2, late_sum[0], name="gather_early_grads_wait")
    vec_shapes = [conv_a_b.shape, lru_br.shape, lru_bi.shape, lru_lambda.shape, (depth, 4, dl), (depth, 3, dl),
                  final_g.shape] + [(d,)] * (depth - 1)
    e = _unpack(_slot_sum(early_all[0], name="sum_early_vectors"), vec_shapes)
    g_wr, g_wi = _unpack(_slot_sum(early_all[1], name="sum_early_maps"), [lru_wr.shape, lru_wi.shape])
    g_norm = jnp.stack([late_sum[0]] + e[7:][::-1])
    g_meta = lax.dynamic_slice_in_dim(late_sum[1], me * mcol, mcol, axis=1)
    g_wa = lax.dynamic_slice_in_dim(e[4], me * ccol, ccol, axis=2)
    g_wb = lax.dynamic_slice_in_dim(e[5], me * ccol, ccol, axis=2)

    small_w = [norm_g, conv_a_b, lru_wr, lru_br, lru_wi, lru_bi, lru_lambda, final_g, meta, conv_a_w, conv_b_w]
    small_m = [m_norm_g, m_conv_a_b, m_lru_wr, m_lru_br, m_lru_wi, m_lru_bi, m_lru_lambda, m_final_g, m_meta,
               m_conv_a_w, m_conv_b_w]
    small_v = [v_norm_g, v_conv_a_b, v_lru_wr, v_lru_br, v_lru_wi, v_lru_bi, v_lru_lambda, v_final_g, v_meta,
               v_conv_a_w, v_conv_b_w]
    small_g = [g_norm, e[0], g_wr, e[1], g_wi, e[2], e[3], e[6], g_meta, g_wa, g_wb]
    small_out = _adamw(_pack(small_w), _pack(small_g), _pack(small_m), _pack(small_v), name="adamw_small")
    small_shapes = [a.shape for a in small_w]
    s_grad, s_delta, s_m, s_v = [_unpack(o, small_shapes) for o in small_out]

    win_out = None
    wout_out = None
    after = small_out[0]
    for l in reversed(range(depth)):
        src, landed = _push_wait(sent_out[l], [_scatter_lead], after, name=f"scatter_wout_{l}_wait")
        own = lax.dynamic_index_in_dim(src[0], me, 0, keepdims=False)
        wout_out = _adamw(w_out, own, m_w_out, v_w_out, landed=landed[0], layer=l, depth=depth,
                          into=wout_out, name=f"adamw_w_out_{l}")
        after = wout_out[0]
        for p, sent in enumerate(sent_in[l]):
            src, landed = _push_wait(sent, scatter_in, after, name=f"scatter_win_{l}_{p}_wait")
            own = lax.dynamic_slice_in_dim(src[0], me * wcol, wcol, axis=1)
            win_out = _adamw(w_in, own, m_w_in, v_w_in, landed=landed[0], layer=l, depth=depth,
                             into=win_out, row_off=p * own.shape[0], name=f"adamw_w_in_{l}_{p}")
            after = win_out[0]

    names = ["norm_g", "conv_a_b", "lru_wr", "lru_br", "lru_wi", "lru_bi", "lru_lambda", "final_g", "meta",
             "conv_a_w", "conv_b_w"]
    order = ["meta", "norm_g", "w_in", "conv_a_w", "conv_a_b", "lru_wr", "lru_br", "lru_wi", "lru_bi", "lru_lambda",
             "conv_b_w", "w_out", "final_g"]

    def family(idx, small):
        table = {nm: small[i] for i, nm in enumerate(names)}
        table["w_in"] = win_out[idx]
        table["w_out"] = wout_out[idx]
        return [table[nm] for nm in order]

    return (loss, grad_x, *family(0, s_grad), *family(1, s_delta), *family(2, s_m), *family(3, s_v))
```

```python
import functools

import jax
import jax.numpy as jnp
from jax import lax
from jax.experimental import pallas as pl
from jax.experimental.pallas import tpu as pltpu

F32 = jnp.float32
BF16 = jnp.bfloat16
MESH = pl.DeviceIdType.MESH

N_DEV = 8
RMS_EPS = 1e-6
LRU_C = 8.0
ADAM_LR = 0.001
ADAM_B1 = 0.9
ADAM_B2 = 0.999
ADAM_EPS = 1e-08
ADAM_WD = 0.01
ADAM_STEP = 10

V7X_VMEM_LIMIT = 52 * 1024 * 1024
LANES = 128
SUBLANES = 8
TOKEN_TILE = 384
MIX_ROWS = 128
SHIFTED_ROWS = 128
GATE_BLOCK = 128


def _params(sem):
    return pltpu.CompilerParams(dimension_semantics=sem, vmem_limit_bytes=V7X_VMEM_LIMIT)


def _tile(n, target, align=LANES):
    best = None
    for t in range(align, min(n, target) + 1, align):
        if n % t == 0:
            best = t
    return n if best is None else best


def _sigmoid(z):
    return 0.5 * jnp.tanh(0.5 * z) + 0.5


def _softplus(z):
    e = jnp.exp(-jnp.abs(z))
    u = 1.0 + e
    l1p = jnp.where(u == 1.0, e, jnp.log(u) * e / jnp.where(u == 1.0, 1.0, u - 1.0))
    return jnp.maximum(z, 0.0) + l1p


def _matmul(a, b, *, ta=False, tb=False, tm, tn, tk, out_dtype=F32, add=None, dep=None, m_part=None, name):
    m, k = (a.shape[1], a.shape[0]) if ta else a.shape
    m_off = 0
    if m_part is not None:
        assert add is None and m % (m_part[1] * tm) == 0
        m //= m_part[1]
        m_off = m_part[0] * (m // tm)
    n, kb = b.shape if tb else b.shape[::-1]
    assert kb == k
    assert m % tm == 0 and n % tn == 0 and k % tk == 0, (m, n, k, tm, tn, tk)
    nk = k // tk
    a_spec = pl.BlockSpec((tk, tm), lambda i, j, q: (q, i + m_off)) if ta \
        else pl.BlockSpec((tm, tk), lambda i, j, q: (i + m_off, q))
    b_spec = pl.BlockSpec((tn, tk), lambda i, j, q: (j, q)) if tb else pl.BlockSpec((tk, tn), lambda i, j, q: (q, j))
    o_spec = pl.BlockSpec((tm, tn), lambda i, j, q: (i, j))
    o_shape = (m, n)
    dims = (((0 if ta else 1,), (1 if tb else 0,)), ((), ()))
    has_add = add is not None
    has_dep = dep is not None

    def body(*refs):
        if has_dep:
            refs = refs[:-3] + refs[-2:]
        if has_add:
            a_ref, b_ref, add_ref, o_ref, acc_ref = refs
        else:
            a_ref, b_ref, o_ref, acc_ref = refs
        q = pl.program_id(2)
        part = lax.dot_general(a_ref[...], b_ref[...], dims, preferred_element_type=F32)

        def finish(acc):
            if has_add:
                acc = acc + add_ref[...]
            o_ref[...] = acc.astype(out_dtype)

        if nk == 1:
            finish(part)
        else:
            @pl.when(q == 0)
            def _():
                acc_ref[...] = part

            @pl.when(jnp.logical_and(q > 0, q < nk - 1))
            def _():
                acc_ref[...] += part

            @pl.when(q == nk - 1)
            def _():
                finish(acc_ref[...] + part)

    in_specs = [a_spec, b_spec] + ([o_spec] if has_add else [])
    args = (a, b) + ((add,) if has_add else ())
    if has_dep:
        in_specs.append(pl.BlockSpec((SUBLANES, LANES), lambda i, j, q: (0, 0)))
        args += (dep,)
    acc_shape = (tm, tn) if nk > 1 else (SUBLANES, LANES)
    return pl.pallas_call(
        body, name=name,
        grid=(m // tm, n // tn, nk),
        in_specs=in_specs, out_specs=o_spec,
        out_shape=jax.ShapeDtypeStruct(o_shape, out_dtype),
        scratch_shapes=[pltpu.VMEM(acc_shape, F32)],
        compiler_params=_params(("parallel", "parallel", "arbitrary")),
    )(*args)


def _rms_fwd(h, g, *, name):
    tp, d = h.shape
    tr = _tile(tp, 512, SUBLANES)

    def body(h_ref, g_ref, o_ref):
        hv = h_ref[...]
        rstd = lax.rsqrt(jnp.mean(hv * hv, axis=-1, keepdims=True) + RMS_EPS)
        o_ref[...] = (hv * rstd * g_ref[...]).astype(BF16)

    return pl.pallas_call(
        body, name=name, grid=(tp // tr,),
        in_specs=[pl.BlockSpec((tr, d), lambda i: (i, 0)), pl.BlockSpec((1, d), lambda i: (0, 0))],
        out_specs=pl.BlockSpec((tr, d), lambda i: (i, 0)),
        out_shape=jax.ShapeDtypeStruct((tp, d), BF16),
        compiler_params=_params(("parallel",)),
    )(h, g.reshape(1, d))


def _rms_fwd_first(x, meta, g, *, tp, name):
    seq, d = x.shape
    n_meta = meta.shape[0]
    n_tok = n_meta + seq
    tr = SHIFTED_ROWS
    assert tp % tr == 0 and seq % tr == 0 and tr % n_meta == 0
    per = tr // n_meta

    def body(x_ref, xp_ref, m_ref, g_ref, h_ref, o_ref):
        i = pl.program_id(0)
        head = jnp.where(i == 0, m_ref[...], xp_ref[...])
        rows = i * tr + lax.broadcasted_iota(jnp.int32, (tr, 1), 0)
        hv = jnp.where(rows < n_tok, jnp.concatenate([head, x_ref[:tr - n_meta, :]], axis=0), 0.0)
        h_ref[...] = hv
        rstd = lax.rsqrt(jnp.mean(hv * hv, axis=-1, keepdims=True) + RMS_EPS)
        o_ref[...] = (hv * rstd * g_ref[...]).astype(BF16)

    row = pl.BlockSpec((tr, d), lambda i: (i, 0))
    own = pl.BlockSpec((tr, d), lambda i: (jnp.minimum(i, seq // tr - 1), 0))
    before = pl.BlockSpec((n_meta, d), lambda i: (jnp.maximum(i * per - 1, 0), 0))
    return pl.pallas_call(
        body, name=name, grid=(tp // tr,),
        in_specs=[own, before, pl.BlockSpec((n_meta, d), lambda i: (0, 0)), pl.BlockSpec((1, d), lambda i: (0, 0))],
        out_specs=[row, row],
        out_shape=[jax.ShapeDtypeStruct((tp, d), F32), jax.ShapeDtypeStruct((tp, d), BF16)],
        compiler_params=_params(("parallel",)),
    )(x, x, meta, g.reshape(1, d))


def _rms_bwd(h, dhn, dout, g, *, name):
    tp, d = h.shape
    tr = _tile(tp, 528, 2 * SUBLANES)

    def body(h_ref, dhn_ref, dout_ref, g_ref, dh_ref, dhb_ref, dg_ref):
        hv = h_ref[...]
        rstd = lax.rsqrt(jnp.mean(hv * hv, axis=-1, keepdims=True) + RMS_EPS)
        xhat = hv * rstd
        dn = dhn_ref[...]
        dxhat = dn * g_ref[...]
        dh = dout_ref[...] + rstd * (dxhat - xhat * jnp.mean(dxhat * xhat, axis=-1, keepdims=True))
        dh_ref[...] = dh
        dhb_ref[...] = dh.astype(BF16)
        part = jnp.sum(dn * xhat, axis=0, keepdims=True)

        @pl.when(pl.program_id(0) == 0)
        def _():
            dg_ref[...] = part

        @pl.when(pl.program_id(0) > 0)
        def _():
            dg_ref[...] += part

    row = pl.BlockSpec((tr, d), lambda i: (i, 0))
    vec = pl.BlockSpec((1, d), lambda i: (0, 0))
    return pl.pallas_call(
        body, name=name, grid=(tp // tr,),
        in_specs=[row, row, row, vec],
        out_specs=[row, row, vec],
        out_shape=[jax.ShapeDtypeStruct((tp, d), F32), jax.ShapeDtypeStruct((tp, d), BF16),
                   jax.ShapeDtypeStruct((1, d), F32)],
        compiler_params=_params(("arbitrary",)),
    )(h, dhn, dout, g.reshape(1, d))


def _rms_bwd_first(h, dhn, dout, g, *, n_meta, seq, name):
    tp, d = h.shape
    tr = SHIFTED_ROWS
    assert seq % tr == 0 and tr % n_meta == 0 and tp >= seq + n_meta
    nt = seq // tr
    per = tr // n_meta

    def grads(hv, dn, do, gv):
        rstd = lax.rsqrt(jnp.mean(hv * hv, axis=-1, keepdims=True) + RMS_EPS)
        xhat = hv * rstd
        dxhat = dn * gv
        dh = do + rstd * (dxhat - xhat * jnp.mean(dxhat * xhat, axis=-1, keepdims=True))
        return dh, jnp.sum(dn * xhat, axis=0, keepdims=True)

    def body(h_ref, dhn_ref, dout_ref, hn_ref, dhnn_ref, doutn_ref, g_ref, gx_ref, dmeta_ref, dg_ref):
        i = pl.program_id(0)
        gv = g_ref[...]
        dh, part = grads(h_ref[...], dhn_ref[...], dout_ref[...], gv)
        dh_next, part_next = grads(hn_ref[...], dhnn_ref[...], doutn_ref[...], gv)
        gx_ref[...] = jnp.concatenate([dh[n_meta:], dh_next], axis=0)

        @pl.when(i == 0)
        def _():
            dmeta_ref[...] = dh[:n_meta]
            dg_ref[...] = part

        @pl.when(i > 0)
        def _():
            dg_ref[...] += part

        @pl.when(i == nt - 1)
        def _():
            dg_ref[...] += part_next

    row = pl.BlockSpec((tr, d), lambda i: (i, 0))
    nxt = pl.BlockSpec((n_meta, d), lambda i: ((i + 1) * per, 0))
    vec = pl.BlockSpec((1, d), lambda i: (0, 0))
    return pl.pallas_call(
        body, name=name, grid=(nt,),
        in_specs=[row, row, row, nxt, nxt, nxt, vec],
        out_specs=[pl.BlockSpec((None, tr, d), lambda i: (0, i, 0)), pl.BlockSpec((n_meta, d), lambda i: (0, 0)), vec],
        out_shape=[jax.ShapeDtypeStruct((1, seq, d), F32), jax.ShapeDtypeStruct((n_meta, d), F32),
                   jax.ShapeDtypeStruct((1, d), F32)],
        compiler_params=_params(("arbitrary",)),
    )(h, dhn, dout, h, dhn, dout, g.reshape(1, d))


def _loss_head(h, tgt, g, *, n_meta, n_tok, name):
    tp, d = h.shape
    seq = tgt.shape[0]
    tr = SHIFTED_ROWS
    assert tp % tr == 0 and seq % tr == 0 and tr % n_meta == 0
    per = tr // n_meta

    def body(h_ref, t_ref, tp_ref, g_ref, dh_ref, dhb_ref, dg_ref, loss_ref):
        i = pl.program_id(0)
        hv = h_ref[...]
        rstd = lax.rsqrt(jnp.mean(hv * hv, axis=-1, keepdims=True) + RMS_EPS)
        xhat = hv * rstd
        gv = g_ref[...]
        rows = i * tr + lax.broadcasted_iota(jnp.int32, (tr, 1), 0)
        valid = jnp.logical_and(rows >= n_meta, rows < n_tok)
        target = jnp.concatenate([tp_ref[...], t_ref[:tr - n_meta, :]], axis=0)
        err = jnp.where(valid, xhat * gv - target, 0.0)
        dy = err * (1.0 / d)
        dxhat = dy * gv
        dh = rstd * (dxhat - xhat * jnp.mean(dxhat * xhat, axis=-1, keepdims=True))
        dh_ref[...] = dh
        dhb_ref[...] = dh.astype(BF16)
        dg_part = jnp.sum(dy * xhat, axis=0, keepdims=True)
        per_row = jnp.sum(err * err, axis=-1, keepdims=True) * (1.0 / d)
        loss_part = jnp.broadcast_to(0.5 * jnp.sum(per_row, axis=0, keepdims=True), (SUBLANES, LANES))

        @pl.when(i == 0)
        def _():
            dg_ref[...] = dg_part
            loss_ref[...] = loss_part

        @pl.when(i > 0)
        def _():
            dg_ref[...] += dg_part
            loss_ref[...] += loss_part

    row = pl.BlockSpec((tr, d), lambda i: (i, 0))
    vec = pl.BlockSpec((1, d), lambda i: (0, 0))
    own = pl.BlockSpec((tr, d), lambda i: (jnp.minimum(i, seq // tr - 1), 0))
    before = pl.BlockSpec((n_meta, d), lambda i: (jnp.maximum(i * per - 1, 0), 0))
    return pl.pallas_call(
        body, name=name, grid=(tp // tr,),
        in_specs=[row, own, before, vec],
        out_specs=[row, row, vec, pl.BlockSpec((SUBLANES, LANES), lambda i: (0, 0))],
        out_shape=[jax.ShapeDtypeStruct((tp, d), F32), jax.ShapeDtypeStruct((tp, d), BF16),
                   jax.ShapeDtypeStruct((1, d), F32), jax.ShapeDtypeStruct((SUBLANES, LANES), F32)],
        compiler_params=_params(("arbitrary",)),
    )(h, tgt, tgt, g.reshape(1, d))


def _shift_down(halo, tile, s):
    if s == 0:
        return tile
    ext = jnp.concatenate([halo, tile], axis=0)
    return pltpu.roll(ext, s, 0)[SUBLANES:]


def _shift_up(tile, head, s):
    if s == 0:
        return tile
    ext = jnp.concatenate([tile, head], axis=0)
    n = ext.shape[0]
    return pltpu.roll(ext, n - s, 0)[: tile.shape[0]]


def _to_lane_blocks(ref, cols, val):
    for j in range(cols.start // LANES, cols.stop // LANES):
        ref[j] = val[:, j * LANES - cols.start:(j + 1) * LANES - cols.start]


def _from_lane_blocks(ref, cols):
    return jnp.concatenate([ref[j] for j in range(cols.start // LANES, cols.stop // LANES)], axis=1)


def _scan_tile(a_ref, b_ref, out_ref, carry, j, *, reverse):
    ng = a_ref.shape[1] // SUBLANES
    order = list(range(SUBLANES))[::-1] if reverse else list(range(SUBLANES))

    def rows(r):
        return pl.ds(r, ng, stride=SUBLANES)

    prod, loc = {}, {}
    prev = None
    for r in order:
        ar = a_ref[j, rows(r), :]
        br = b_ref[j, rows(r), :]
        prod[r] = ar if prev is None else ar * prod[prev]
        loc[r] = br if prev is None else ar * loc[prev] + br
        prev = r
    pg, lg = prod[prev], loc[prev]
    ones = jnp.ones((SUBLANES,) + pg.shape[1:], F32)
    zeros = jnp.zeros_like(ones)
    s = 1
    while s < ng:
        p_sh = _shift_up(pg, ones, s) if reverse else _shift_down(ones, pg, s)
        l_sh = _shift_up(lg, zeros, s) if reverse else _shift_down(zeros, lg, s)
        lg = pg * l_sh + lg
        pg = pg * p_sh
        s *= 2
    leaving = pg * carry[0:1, :] + lg
    entering = _shift_up(leaving, carry, 1) if reverse else _shift_down(carry, leaving, 1)
    for r in order:
        out_ref[j, rows(r), :] = loc[r] + prod[r] * entering
    last = leaving[0:1, :] if reverse else leaving[ng - 1:ng, :]
    return jnp.broadcast_to(last, carry.shape)


def _gates(ca, wr, wi, br, bi, sp):
    cab = ca.astype(BF16)
    r = _sigmoid(jnp.dot(cab, wr, preferred_element_type=F32) + br)
    ig = _sigmoid(jnp.dot(cab, wi, preferred_element_type=F32) + bi)
    la = -LRU_C * r * sp
    a = jnp.exp(la)
    mult = jnp.sqrt(-jnp.tanh(la) * (a * a + 1.0))
    return r, ig, a, mult


def _mixer_fwd(u, wa, ba, wr_blk, br, wi_blk, bi, lam, wb, *, name):
    tp, din = u.shape
    dl = din // 6
    tt = MIX_ROWS
    cw = GATE_BLOCK
    nch = dl // cw
    assert tp % tt == 0 and dl % cw == 0

    def body(u_ref, wa_ref, ba_ref, wr_ref, br_ref, wi_ref, bi_ref, lam_ref, wb_ref,
             s_ref, y_ref, xa_tail, v_tail, h_carry, a_s, b_s, h_s):
        @pl.when(pl.program_id(0) == 0)
        def _():
            xa_tail[...] = jnp.zeros_like(xa_tail)
            v_tail[...] = jnp.zeros_like(v_tail)
            h_carry[...] = jnp.zeros_like(h_carry)

        for ch in range(nch):
            cs = slice(ch * cw, (ch + 1) * cw)

            def seg(s):
                return slice(s * dl + ch * cw, s * dl + (ch + 1) * cw)

            xa = u_ref[:, seg(0)]
            halo = xa_tail[:, cs]
            ca = ba_ref[:, cs] + wa_ref[3:4, cs] * xa
            for kk in range(3):
                ca = ca + wa_ref[kk:kk + 1, cs] * _shift_down(halo, xa, 3 - kk)
            xa_tail[:, cs] = xa[tt - SUBLANES:]
            s_ref[:, cs] = ca
            sp = _softplus(-lam_ref[:, cs])
            _, ig, a, mult = _gates(ca, wr_ref[ch], wi_ref[ch], br_ref[:, cs], bi_ref[:, cs], sp)
            _to_lane_blocks(a_s, cs, a)
            _to_lane_blocks(b_s, cs, mult * (ig * ca))

            bv = u_ref[:, seg(2)]
            v = u_ref[:, seg(3)] * u_ref[:, seg(4)]
            gb = u_ref[:, seg(5)]
            vh = v_tail[:, cs]
            cb = wb_ref[2:3, cs] * v
            for kk in range(2):
                cb = cb + wb_ref[kk:kk + 1, cs] * _shift_down(vh, v, 2 - kk)
            v_tail[:, cs] = v[tt - SUBLANES:]
            y_ref[:, dl + ch * cw: dl + (ch + 1) * cw] = (bv * cb * (gb * _sigmoid(gb))).astype(BF16)

        for ch in range(nch):
            cs = slice(ch * cw, (ch + 1) * cw)
            for j in range(cs.start // LANES, cs.stop // LANES):
                lanes = slice(j * LANES, (j + 1) * LANES)
                h_carry[:, lanes] = _scan_tile(a_s, b_s, h_s, h_carry[:, lanes], j, reverse=False)
            hsv = _from_lane_blocks(h_s, cs)
            s_ref[:, dl + ch * cw: dl + (ch + 1) * cw] = hsv
            ga = u_ref[:, dl + ch * cw: dl + (ch + 1) * cw]
            y_ref[:, cs] = (hsv * (ga * _sigmoid(ga))).astype(BF16)

    row = lambda w: pl.BlockSpec((tt, w), lambda i: (i, 0))
    full = lambda shp: pl.BlockSpec(shp, lambda i: tuple(0 for _ in shp))
    return pl.pallas_call(
        body, name=name, grid=(tp // tt,),
        in_specs=[row(din), full((4, dl)), full((1, dl)), full((nch, cw, cw)), full((1, dl)),
                  full((nch, cw, cw)), full((1, dl)), full((1, dl)), full((3, dl))],
        out_specs=[row(2 * dl), row(2 * dl)],
        out_shape=[jax.ShapeDtypeStruct((tp, 2 * dl), F32), jax.ShapeDtypeStruct((tp, 2 * dl), BF16)],
        scratch_shapes=[pltpu.VMEM((SUBLANES, dl), F32), pltpu.VMEM((SUBLANES, dl), F32),
                        pltpu.VMEM((SUBLANES, dl), F32)] + [pltpu.VMEM((dl // LANES, tt, LANES), F32)] * 3,
        compiler_params=_params(("arbitrary",)),
    )(u, wa, ba, wr_blk, br, wi_blk, bi, lam, wb)


SG_WA, SG_BA, SG_BR, SG_BI, SG_LAM, SG_WB, SG_ROWS = 0, 4, 5, 6, 7, 8, 16


def _mixer_bwd(u, saved, dy, wa, wr_blk, br, wi_blk, bi, lam, wb, *, name):
    tp, din = u.shape
    dl = din // 6
    tt = MIX_ROWS
    cw = GATE_BLOCK
    nch = dl // cw
    nt = tp // tt
    hb = tt // SUBLANES
    tn_dims = (((0,), (0,)), ((), ()))
    nt_dims = (((1,), (1,)), ((), ()))

    def body(u_ref, uh_ref, s_ref, sh_ref, dy_ref, wa_ref, wr_ref, br_ref, wi_ref, bi_ref, lam_ref, wb_ref,
             du_ref, sg_ref, dwr_ref, dwi_ref,
             g_carry, a_head, dca_head, dcb_head, r_s, i_s, a_s, an_s, d_s, g_s):
        i = pl.program_id(0)
        first_tile = i == nt - 1

        @pl.when(i == 0)
        def _():
            for ref in (g_carry, a_head, dca_head, dcb_head, sg_ref, dwr_ref, dwi_ref):
                ref[...] = jnp.zeros_like(ref)

        def halo_of(x):
            return jnp.where(first_tile, 0.0, x)

        for ch in range(nch):
            cs = slice(ch * cw, (ch + 1) * cw)
            cav = s_ref[:, cs]
            sp = _softplus(-lam_ref[:, cs])
            r, ig, a, _ = _gates(cav, wr_ref[ch], wi_ref[ch], br_ref[:, cs], bi_ref[:, cs], sp)
            r_s[:, cs] = r
            i_s[:, cs] = ig
            a_s[:, cs] = a
            _to_lane_blocks(an_s, cs, _shift_up(a, a_head[:, cs], 1))
            a_head[:, cs] = a[:SUBLANES]
            ga = u_ref[:, dl + ch * cw: dl + (ch + 1) * cw]
            _to_lane_blocks(d_s, cs, dy_ref[:, cs] * (ga * _sigmoid(ga)))

        for j in range(dl // LANES):
            lanes = slice(j * LANES, (j + 1) * LANES)
            g_carry[:, lanes] = _scan_tile(an_s, d_s, g_s, g_carry[:, lanes], j, reverse=True)

        def acc_row(r0, val):
            sg_ref[r0:r0 + 1, cs_cur[0]] += jnp.sum(val, axis=0, keepdims=True)

        cs_cur = [None]
        for ch in range(nch):
            cs = slice(ch * cw, (ch + 1) * cw)
            cs_cur[0] = cs

            def seg(s):
                return slice(s * dl + ch * cw, s * dl + (ch + 1) * cw)

            cav = s_ref[:, cs]
            r = r_s[:, cs]
            ig = i_s[:, cs]
            a = a_s[:, cs]
            g = _from_lane_blocks(g_s, cs)
            hsv = s_ref[:, dl + ch * cw: dl + (ch + 1) * cw]
            lamv = lam_ref[:, cs]
            sp = _softplus(-lamv)
            la = -LRU_C * r * sp
            e2 = a * a
            one_m_e2 = -jnp.tanh(la) * (e2 + 1.0)
            mult = jnp.sqrt(one_m_e2)
            hprev = _shift_down(halo_of(sh_ref[:, dl + ch * cw: dl + (ch + 1) * cw]), hsv, 1)
            icav = ig * cav
            dla = g * (hprev * a - icav * (e2 * lax.rsqrt(one_m_e2)))
            gm = g * mult
            dzi = gm * icav * (1.0 - ig)
            dca = gm * ig
            dla_r = dla * r
            dzr = dla_r * (1.0 - r) * (-LRU_C * sp)
            sg_ref[SG_LAM:SG_LAM + 1, cs] += jnp.sum(dla_r, axis=0, keepdims=True) * (LRU_C * _sigmoid(-lamv))
            acc_row(SG_BR, dzr)
            acc_row(SG_BI, dzi)
            dzr_b = dzr.astype(BF16)
            dzi_b = dzi.astype(BF16)
            cab = cav.astype(BF16)
            dca = dca + lax.dot_general(dzr_b, wr_ref[ch], nt_dims, preferred_element_type=F32)
            dca = dca + lax.dot_general(dzi_b, wi_ref[ch], nt_dims, preferred_element_type=F32)
            dwr_ref[ch] += lax.dot_general(cab, dzr_b, tn_dims, preferred_element_type=F32)
            dwi_ref[ch] += lax.dot_general(cab, dzi_b, tn_dims, preferred_element_type=F32)
            acc_row(SG_BA, dca)
            xa = u_ref[:, seg(0)]
            head = dca_head[:, cs]
            dxa = wa_ref[3:4, cs] * dca
            acc_row(SG_WA + 3, dca * xa)
            for kk in range(3):
                later = _shift_up(dca, head, 3 - kk)
                acc_row(SG_WA + kk, later * xa)
                dxa = dxa + wa_ref[kk:kk + 1, cs] * later
            dca_head[:, cs] = dca[:SUBLANES]
            ga = u_ref[:, seg(1)]
            sga = _sigmoid(ga)
            dga = dy_ref[:, cs] * hsv * (sga + (ga * sga) * (1.0 - sga))
            du_ref[:, seg(0)] = dxa.astype(BF16)
            du_ref[:, seg(1)] = dga.astype(BF16)

            bv = u_ref[:, seg(2)]
            cv = u_ref[:, seg(3)]
            xb = u_ref[:, seg(4)]
            gb = u_ref[:, seg(5)]
            dyb = dy_ref[:, dl + ch * cw: dl + (ch + 1) * cw]
            v = cv * xb
            vh = halo_of(uh_ref[:, seg(3)] * uh_ref[:, seg(4)])
            v1 = _shift_down(vh, v, 1)
            v2 = _shift_down(vh, v, 2)
            cb = wb_ref[2:3, cs] * v + wb_ref[1:2, cs] * v1 + wb_ref[0:1, cs] * v2
            sgb = _sigmoid(gb)
            sl = gb * sgb
            dyb_b = dyb * bv
            dyb_cb = dyb * cb
            dcb = dyb_b * sl
            du_ref[:, seg(2)] = (dyb_cb * sl).astype(BF16)
            du_ref[:, seg(5)] = (dyb_cb * bv * (sgb + sl * (1.0 - sgb))).astype(BF16)
            bhead = dcb_head[:, cs]
            dv = wb_ref[2:3, cs] * dcb
            acc_row(SG_WB + 2, dcb * v)
            for kk in range(2):
                later = _shift_up(dcb, bhead, 2 - kk)
                acc_row(SG_WB + kk, later * v)
                dv = dv + wb_ref[kk:kk + 1, cs] * later
            dcb_head[:, cs] = dcb[:SUBLANES]
            du_ref[:, seg(3)] = (dv * xb).astype(BF16)
            du_ref[:, seg(4)] = (dv * cv).astype(BF16)

    rev = lambda w: pl.BlockSpec((tt, w), lambda i: (nt - 1 - i, 0))
    halo = lambda w: pl.BlockSpec((SUBLANES, w), lambda i: (jnp.maximum((nt - 1 - i) * hb - 1, 0), 0))
    full = lambda shp: pl.BlockSpec(shp, lambda i: tuple(0 for _ in shp))
    vm = lambda r: pltpu.VMEM((r, dl), F32)
    return pl.pallas_call(
        body, name=name, grid=(nt,),
        in_specs=[rev(din), halo(din), rev(2 * dl), halo(2 * dl), rev(2 * dl), full((4, dl)),
                  full((nch, cw, cw)), full((1, dl)), full((nch, cw, cw)), full((1, dl)), full((1, dl)), full((3, dl))],
        out_specs=[rev(din), full((SG_ROWS, dl)), full((nch, cw, cw)), full((nch, cw, cw))],
        out_shape=[jax.ShapeDtypeStruct((tp, din), BF16), jax.ShapeDtypeStruct((SG_ROWS, dl), F32),
                   jax.ShapeDtypeStruct((nch, cw, cw), F32), jax.ShapeDtypeStruct((nch, cw, cw), F32)],
        scratch_shapes=[vm(SUBLANES), vm(SUBLANES), vm(SUBLANES), vm(SUBLANES), vm(tt), vm(tt), vm(tt)]
        + [pltpu.VMEM((dl // LANES, tt, LANES), F32)] * 3,
        compiler_params=_params(("arbitrary",)),
    )(u, u, saved, saved, dy, wa, wr_blk, br, wi_blk, bi, lam, wb)


def _adamw(w, g, m, v, *, name, landed=None, layer=None, depth=None, into=None, row_off=0):
    r, c = w.shape[-2:]
    rows = g.shape[0]
    tr = _tile(rows, 512, 2 * SUBLANES)
    assert row_off % tr == 0
    boff = row_off // tr
    bc1 = 1.0 - ADAM_B1 ** ADAM_STEP
    bc2 = 1.0 - ADAM_B2 ** ADAM_STEP
    slots = landed is not None

    def body(*refs):
        if into is not None:
            refs = refs[:-8] + refs[-4:]
        if slots:
            w_ref, g_ref, l_ref, m_ref, v_ref, grad_ref, delta_ref, nm_ref, nv_ref = refs
            gv = g_ref[...].astype(F32)
            for s in range(N_DEV - 1):
                gv = gv + l_ref[s].astype(F32)
        else:
            w_ref, g_ref, m_ref, v_ref, grad_ref, delta_ref, nm_ref, nv_ref = refs
            gv = g_ref[...]
        wv = w_ref[...]
        mn = ADAM_B1 * m_ref[...] + (1.0 - ADAM_B1) * gv
        vn = ADAM_B2 * v_ref[...] + (1.0 - ADAM_B2) * (gv * gv)
        m_hat = mn / bc1
        v_hat = vn / bc2
        grad_ref[...] = gv
        delta_ref[...] = -ADAM_LR * (m_hat / (jnp.sqrt(v_hat) + ADAM_EPS) + ADAM_WD * wv)
        nm_ref[...] = mn
        nv_ref[...] = vn

    if depth is None:
        blk = pl.BlockSpec((tr, c), lambda i: (i + boff, 0))
    else:
        blk = pl.BlockSpec((None, tr, c), lambda i: (layer, i + boff, 0))
    g_blk = pl.BlockSpec((tr, c), lambda i: (i, 0))
    l_spec = [pl.BlockSpec((N_DEV - 1, tr, c), lambda i: (0, i, 0))] if slots else []
    args = (w, g, landed, m, v) if slots else (w, g, m, v)
    in_specs = [blk, g_blk] + l_spec + [blk, blk]
    if depth is None:
        shp = jax.ShapeDtypeStruct((r, c), F32)
        out_blk = blk
    else:
        shp = jax.ShapeDtypeStruct((depth, r, c), F32)
        out_blk = pl.BlockSpec((None, tr, c), lambda i: (layer, i + boff, 0))
    aliases = {}
    if into is not None:
        aliases = {len(args) + j: j for j in range(4)}
        in_specs = in_specs + [ANY] * 4
        args = args + tuple(into)
    return pl.pallas_call(
        body, name=name, grid=(rows // tr,),
        in_specs=in_specs, out_specs=[out_blk] * 4,
        out_shape=[shp] * 4, input_output_aliases=aliases,
        compiler_params=_params(("parallel",)),
    )(*args)


def _slot_sum(g, *, name):
    _, r, c = g.shape
    tr = _tile(r, 512, SUBLANES)

    def body(g_ref, o_ref):
        gv = g_ref[0].astype(F32)
        for s in range(1, N_DEV):
            gv = gv + g_ref[s].astype(F32)
        o_ref[...] = gv

    return pl.pallas_call(
        body, name=name, grid=(r // tr,),
        in_specs=[pl.BlockSpec((N_DEV, tr, c), lambda i: (0, i, 0))],
        out_specs=pl.BlockSpec((tr, c), lambda i: (i, 0)),
        out_shape=jax.ShapeDtypeStruct((r, c), F32),
        compiler_params=_params(("parallel",)),
    )(g)


def _mesh_pos():
    x, y, c = lax.axis_index("x"), lax.axis_index("y"), lax.axis_index("c")
    return x, y, c, 4 * x + 2 * y + c


ANY = pl.BlockSpec(memory_space=pl.ANY)


GATHER_COPIES = 9


def _all_gather(srcs, out_shapes, views, *, name):
    n = len(srcs)
    SIB, X_OWN, Y_OWN, X_DIAG, Y_DIAG, SIB_X, SIB_Y, SIB_DIAG_TOP, SIB_DIAG_BOTTOM = range(GATHER_COPIES)

    def body(*refs):
        src = refs[:n]
        dst = refs[n:2 * n]
        send_sems, recv_sems, local_sems = refs[2 * n:]
        x, y, c, me = _mesh_pos()
        sibling, x_nbr, y_nbr = (x, y, 1 - c), (1 - x, y, c), (x, 1 - y, c)

        def block(a, px, py, pc, half=None):
            win = views[a](dst[a], 4 * px + 2 * py + pc)
            if half is None:
                return win
            rows = win.shape[0] // 2
            return win.at[pl.ds(half * rows, rows)]

        def copy(a, k, win, to, from_src=False):
            return pltpu.make_async_remote_copy(
                src_ref=src[a] if from_src else win, dst_ref=win,
                send_sem=send_sems.at[a * GATHER_COPIES + k], recv_sem=recv_sems.at[a * GATHER_COPIES + k],
                device_id=to, device_id_type=MESH)

        mine = [pltpu.make_async_copy(src[a], block(a, x, y, c), local_sems.at[a]) for a in range(n)]
        started = []

        def start(cp):
            cp.start()
            started.append(cp)

        for a in range(n):
            mine[a].start()
            own = block(a, x, y, c)
            start(copy(a, SIB, own, sibling, True))
            start(copy(a, X_OWN, own, x_nbr, True))
            start(copy(a, Y_OWN, own, y_nbr, True))
        for a in range(n):
            from_y = block(a, x, 1 - y, c)
            copy(a, Y_OWN, from_y, y_nbr).wait_recv()
            start(copy(a, X_DIAG, block(a, x, 1 - y, c, 0), x_nbr))
            start(copy(a, SIB_Y, from_y, sibling))
            from_x = block(a, 1 - x, y, c)
            copy(a, X_OWN, from_x, x_nbr).wait_recv()
            start(copy(a, Y_DIAG, block(a, 1 - x, y, c, 1), y_nbr))
            start(copy(a, SIB_X, from_x, sibling))
        for a in range(n):
            top = block(a, 1 - x, 1 - y, c, 0)
            copy(a, X_DIAG, top, x_nbr).wait_recv()
            start(copy(a, SIB_DIAG_TOP, top, sibling))
            bottom = block(a, 1 - x, 1 - y, c, 1)
            copy(a, Y_DIAG, bottom, y_nbr).wait_recv()
            start(copy(a, SIB_DIAG_BOTTOM, bottom, sibling))
        for a in range(n):
            copy(a, SIB, block(a, x, y, 1 - c), sibling).wait_recv()
            copy(a, SIB_X, block(a, 1 - x, y, 1 - c), sibling).wait_recv()
            copy(a, SIB_Y, block(a, x, 1 - y, 1 - c), sibling).wait_recv()
            copy(a, SIB_DIAG_TOP, block(a, 1 - x, 1 - y, 1 - c, 0), sibling).wait_recv()
            copy(a, SIB_DIAG_BOTTOM, block(a, 1 - x, 1 - y, 1 - c, 1), sibling).wait_recv()
        for cp in started:
            cp.wait_send()
        for cp in mine:
            cp.wait()

    return pl.pallas_call(
        body, name=name,
        in_specs=[ANY] * n, out_specs=[ANY] * n,
        out_shape=[jax.ShapeDtypeStruct(s, x.dtype) for s, x in zip(out_shapes, srcs)],
        scratch_shapes=[pltpu.SemaphoreType.DMA((GATHER_COPIES * n,)), pltpu.SemaphoreType.DMA((GATHER_COPIES * n,)),
                        pltpu.SemaphoreType.DMA((n,))],
    )(*srcs)


HBM = pl.BlockSpec(memory_space=pltpu.HBM)
SEM = pl.BlockSpec(memory_space=pltpu.SEMAPHORE)
EFFECT = pltpu.SideEffectType.DATAFLOW_SIDE_EFFECTING


def _peer_of(x, y, c, k):
    return (1 - x if k & 4 else x, 1 - y if k & 2 else y, 1 - c if k & 1 else c)


def _peer_copies(n, wins, src, land, send_sems, recv_sems):
    x, y, c, me = _mesh_pos()
    out = []
    for a in range(n):
        for k in range(1, N_DEV):
            px, py, pc = _peer_of(x, y, c, k)
            s_win, d_win = wins[a](src[a], land[a], me, 4 * px + 2 * py + pc, k)
            out.append(pltpu.make_async_remote_copy(
                src_ref=s_win, dst_ref=d_win,
                send_sem=send_sems.at[a * 7 + k - 1], recv_sem=recv_sems.at[a * 7 + k - 1],
                device_id=(px, py, pc), device_id_type=MESH))
    return out


def _push_start(srcs, lands, wins, *, name):
    n = len(srcs)

    def body(*refs):
        src = refs[:n]
        land = refs[n:2 * n]
        send_sems, recv_sems = refs[2 * n], refs[2 * n + 1]
        token = refs[-1]
        for cp in _peer_copies(n, wins, src, land, send_sems, recv_sems):
            cp.start()
        token[...] = jnp.zeros_like(token)

    bufs = (*srcs, *lands)
    return pl.pallas_call(
        body, name=name,
        out_shape=(pltpu.SemaphoreType.DMA((7 * n,)), pltpu.SemaphoreType.DMA((7 * n,)),
                   *[pltpu.HBM(v.shape, v.dtype) for v in bufs], jax.ShapeDtypeStruct((SUBLANES, LANES), F32)),
        in_specs=[HBM] * (2 * n),
        out_specs=(SEM, SEM, *[HBM] * (2 * n), pl.BlockSpec(memory_space=pltpu.VMEM)),
        input_output_aliases={i: 2 + i for i in range(2 * n)},
        compiler_params=pltpu.CompilerParams(has_side_effects=EFFECT),
    )(*[pltpu.with_memory_space_constraint(v, pltpu.HBM) for v in bufs])


def _push_wait(handle, wins, after, *, name):
    send_sems, recv_sems, *bufs, _ = handle
    n = len(bufs) // 2

    def body(*refs):
        src = refs[:n]
        land = refs[n:2 * n]
        for cp in _peer_copies(n, wins, src, land, refs[2 * n], refs[2 * n + 1]):
            cp.wait_send()
            cp.wait_recv()

    outs = pl.pallas_call(
        body, name=name,
        out_shape=tuple(pltpu.HBM(v.shape, v.dtype) for v in bufs),
        in_specs=[HBM] * (2 * n) + [SEM, SEM, ANY],
        out_specs=tuple([HBM] * (2 * n)),
        input_output_aliases={i: i for i in range(2 * n)},
        compiler_params=pltpu.CompilerParams(has_side_effects=EFFECT),
    )(*bufs, send_sems, recv_sems, after)
    return outs[:n], outs[n:]


def _gather_lead(src, land, me, peer, k):
    return src, land.at[me]


def _gather_cols(width):
    def win(src, land, me, peer, k):
        return src, land.at[:, pl.ds(me * width, width)]
    return win


def _scatter_lead(src, land, me, peer, k):
    return src.at[peer], land.at[k - 1]


def _scatter_cols(width):
    def win(src, land, me, peer, k):
        return src.at[:, pl.ds(peer * width, width)], land.at[k - 1]
    return win


def _place_block(own, *, cols, name):
    rows, width = own.shape
    tr = _tile(rows, 512, 2 * SUBLANES)
    _, _, _, me = _mesh_pos()

    def body(me_ref, x_ref, o_ref):
        o_ref[...] = x_ref[...]

    if cols:
        out_spec = pl.BlockSpec((tr, width), lambda i, me_ref: (i, me_ref[0]))
        shape = (rows, N_DEV * width)
    else:
        out_spec = pl.BlockSpec((None, tr, width), lambda i, me_ref: (me_ref[0], i, 0))
        shape = (N_DEV, rows, width)
    return pl.pallas_call(
        body, name=name,
        grid_spec=pltpu.PrefetchScalarGridSpec(
            num_scalar_prefetch=1, grid=(rows // tr,),
            in_specs=[pl.BlockSpec((tr, width), lambda i, me_ref: (i, 0))], out_specs=out_spec),
        out_shape=jax.ShapeDtypeStruct(shape, own.dtype),
        compiler_params=_params(("arbitrary",)),
    )(me.astype(jnp.int32).reshape(1), own)


def _dep(x, token):
    return x + token[0, 0].astype(x.dtype)


def _lead(ref, d):
    return ref.at[d]


def _col_window(width):
    def view(ref, d):
        return ref.at[:, pl.ds(d * width, width)]
    return view


def _pack(arrs):
    flat = jnp.concatenate([a.reshape(-1).astype(F32) for a in arrs])
    n = flat.shape[0]
    rows = -(-n // (2 * SUBLANES * LANES)) * 2 * SUBLANES
    return jnp.pad(flat, (0, rows * LANES - n)).reshape(rows, LANES)


def _unpack(buf, shapes):
    flat = buf.reshape(-1)
    out, off = [], 0
    for s in shapes:
        n = 1
        for q in s:
            n *= q
        out.append(flat[off:off + n].reshape(s))
        off += n
    return out


def _blockdiag(w, cw):
    h, hd, _ = w.shape
    per = cw // hd
    wg = w.reshape(h // per, per, hd, hd)
    eye = jnp.eye(per, dtype=w.dtype)
    blk = jnp.einsum("gpij,pq->gpiqj", wg, eye)
    return blk.reshape(h // per, cw, cw).astype(BF16)


def _blockdiag_extract(g, hd):
    n, cw, _ = g.shape
    per = cw // hd
    g5 = g.reshape(n, per, hd, per, hd)
    idx = jnp.arange(per)
    return g5[:, idx, :, idx, :].transpose(1, 0, 2, 3).reshape(n * per, hd, hd)


def kernel(x, meta, norm_g, w_in, conv_a_w, conv_a_b, lru_wr, lru_br, lru_wi, lru_bi, lru_lambda, conv_b_w, w_out, final_g, loss_target, m_meta, m_norm_g, m_w_in, m_conv_a_w, m_conv_a_b, m_lru_wr, m_lru_br, m_lru_wi, m_lru_bi, m_lru_lambda, m_conv_b_w, m_w_out, m_final_g, v_meta, v_norm_g, v_w_in, v_conv_a_w, v_conv_a_b, v_lru_wr, v_lru_br, v_lru_wi, v_lru_bi, v_lru_lambda, v_conv_b_w, v_w_out, v_final_g):
    _, seq, d = x.shape
    n_meta = meta.shape[0]
    depth = w_in.shape[0]
    din = w_in.shape[2] * N_DEV
    dl = din // 6
    dmix = 2 * dl
    wcol = w_in.shape[2]
    wrow = w_out.shape[1]
    mcol = meta.shape[1]
    ccol = conv_a_w.shape[2]
    heads, hd = lru_wr.shape[1], lru_wr.shape[2]
    n_tok = n_meta + seq
    tp = -(-n_tok // TOKEN_TILE) * TOKEN_TILE
    me = 4 * lax.axis_index("x") + 2 * lax.axis_index("y") + lax.axis_index("c")

    bf = lambda a: a.astype(BF16)
    small_mine = _pack([meta, conv_a_w, conv_b_w])
    first = _all_gather([bf(w_in[0]), small_mine], [(d, din), (N_DEV,) + small_mine.shape],
                        [_col_window(wcol), _lead], name="gather_first")
    flat = first[1].reshape(N_DEV, -1)
    sizes = [meta.size, conv_a_w.size, conv_b_w.size]
    meta_full = jnp.moveaxis(flat[:, :sizes[0]].reshape(N_DEV, n_meta, mcol), 0, 1).reshape(n_meta, d)
    wa_full = jnp.moveaxis(flat[:, sizes[0]:sizes[0] + sizes[1]].reshape(N_DEV, depth, 4, ccol), 0, 2) \
        .reshape(depth, 4, dl)
    wb_full = jnp.moveaxis(flat[:, sizes[0] + sizes[1]:sum(sizes)].reshape(N_DEV, depth, 3, ccol), 0, 2) \
        .reshape(depth, 3, dl)
    w_in_full = [None] * depth
    w_out_full = [None] * depth

    push_out = [None] * depth
    push_in = [None] * depth
    w_in_full[0], src = lax.optimization_barrier((first[0], bf(w_out[0])))
    push_out[0] = _push_start([src], [_place_block(src, cols=False, name="place_wout_0")], [_gather_lead],
                              name="gather_wout_0_start")
    token = push_out[0][-1]
    for l in range(1, depth):
        src = bf(_dep(w_in[l], token))
        push_in[l] = _push_start([src], [_place_block(src, cols=True, name=f"place_win_{l}")], [_gather_cols(wcol)],
                                 name=f"gather_win_{l}_start")
        src = bf(_dep(w_out[l], push_in[l][-1]))
        push_out[l] = _push_start([src], [_place_block(src, cols=False, name=f"place_wout_{l}")], [_gather_lead],
                                  name=f"gather_wout_{l}_start")
        token = push_out[l][-1]

    wr_blk = [_blockdiag(lru_wr[l], GATE_BLOCK) for l in range(depth)]
    wi_blk = [_blockdiag(lru_wi[l], GATE_BLOCK) for l in range(depth)]
    vec = lambda a: a.reshape(1, dl)

    tm = _tile(tp, 1408)
    saved = []
    for l in range(depth):
        if l == 0:
            h, hn = _rms_fwd_first(x[0], meta_full, _dep(norm_g[l], token), tp=tp, name=f"rms_fwd_{l}")
        else:
            hn = _rms_fwd(h, norm_g[l], name=f"rms_fwd_{l}")
        if l > 0:
            _, landed = _push_wait(push_in[l], [_gather_cols(wcol)], hn, name=f"gather_win_{l}_wait")
            w_in_full[l] = landed[0]
        u = _matmul(hn, w_in_full[l], tm=tm, tn=_tile(din, 768), tk=d, name=f"mm_u_{l}")
        mixed, y = _mixer_fwd(u, wa_full[l], vec(conv_a_b[l]), wr_blk[l], vec(lru_br[l]), wi_blk[l], vec(lru_bi[l]),
                              vec(lru_lambda[l]), wb_full[l], name=f"mixer_fwd_{l}")
        _, landed = _push_wait(push_out[l], [_gather_lead], y, name=f"gather_wout_{l}_wait")
        w_out_full[l] = landed[0].reshape(dmix, d)
        h_next = _matmul(y, w_out_full[l], tm=tm, tn=_tile(d, 512), tk=dmix, add=h, name=f"mm_out_{l}")
        saved.append((h, hn, u, mixed, y))
        h = h_next

    dh, dhb, dg_final, loss_part = _loss_head(h, loss_target[0], final_g, n_meta=n_meta, n_tok=n_tok,
                                              name="loss_head")

    small_grads = [None] * depth
    sent_out = [None] * depth
    sent_in = [None] * depth
    scatter_in = [_scatter_cols(wcol)]
    token = None
    dg_norms = []
    for l in reversed(range(depth)):
        h_in, hn, u, mixed, y = saved[l]
        dy = _matmul(dhb, w_out_full[l], tb=True, tm=tm, tn=_tile(dmix, 512), tk=d, dep=token, name=f"mm_dy_{l}")
        dw_out = _matmul(y, dhb, ta=True, tm=_tile(dmix, 512), tn=_tile(d, 1024), tk=tp, out_dtype=BF16,
                         name=f"mm_dwout_{l}")
        sent_out[l] = _push_start([dw_out.reshape(N_DEV, wrow, d)], [lax.empty((N_DEV - 1, wrow, d), BF16)],
                                  [_scatter_lead], name=f"scatter_wout_{l}_start")
        du, sg, dwr, dwi = _mixer_bwd(u, mixed, dy, wa_full[l], wr_blk[l], vec(lru_br[l]), wi_blk[l], vec(lru_bi[l]),
                                      vec(lru_lambda[l]), _dep(wb_full[l], sent_out[l][-1]), name=f"mixer_bwd_{l}")
        small_grads[l] = (sg, dwr, dwi)
        if l == 0:
            rows = jnp.stack([small_grads[j][0] for j in range(depth)])
            early = [_pack([
                rows[:, SG_BA], rows[:, SG_BR], rows[:, SG_BI], rows[:, SG_LAM], rows[:, SG_WA:SG_WA + 4],
                rows[:, SG_WB:SG_WB + 3], dg_final[0], *dg_norms]),
                _pack([jnp.stack([_blockdiag_extract(small_grads[j][1], hd) for j in range(depth)]),
                       jnp.stack([_blockdiag_extract(small_grads[j][2], hd) for j in range(depth)])]).astype(BF16)]
            early_land = [lax.dynamic_update_slice(lax.empty((N_DEV,) + a.shape, a.dtype), a[None], (me, 0, 0))
                          for a in early]
            sent_early = _push_start(early, early_land, [_gather_lead] * 2, name="gather_early_grads_start")
        parts = 2 if l == 0 else 1
        token = sent_early[-1] if l == 0 else None
        sent_in[l] = []
        for p in range(parts):
            dw_in = _matmul(hn, du, ta=True, tm=_tile(d // parts, 512), tn=_tile(din, 768), tk=tp, out_dtype=BF16,
                            dep=token, m_part=(p, parts), name=f"mm_dwin_{l}_{p}")
            sent_in[l].append(_push_start([dw_in], [lax.empty((N_DEV - 1, d // parts, wcol), BF16)], scatter_in,
                                          name=f"scatter_win_{l}_{p}_start"))
            token = sent_in[l][-1][-1]
        dhn = _matmul(du, w_in_full[l], tb=True, tm=_tile(tp, 704, 2 * SUBLANES), tn=_tile(d, 512), tk=din, dep=token,
                      name=f"mm_dhn_{l}")
        if l > 0:
            dh, dhb, dg_norm = _rms_bwd(h_in, dhn, dh, norm_g[l], name=f"rms_bwd_{l}")
            dg_norms.append(dg_norm[0])
        else:
            grad_x, d_meta, dg_norm = _rms_bwd_first(h_in, dhn, dh, norm_g[l], n_meta=n_meta, seq=seq,
                                                     name=f"rms_bwd_{l}")

    late = _pack([dg_norm[0], d_meta, loss_part[0:1, 0:1]])
    late_all = _all_gather([late], [(N_DEV,) + late.shape], [_lead], name="gather_late_grads")[0]
    late_sum = _unpack(_slot_sum(late_all, name="sum_late_grads"), [(d,), (n_meta, d), ()])
    loss = late_sum[2]
    _, early_all = _push_wait(sent_early, [_gather_lead] * 2, late_sum[0], name="gather_early_grads_wait")
    vec_shapes = [conv_a_b.shape, lru_br.shape, lru_bi.shape, lru_lambda.shape, (depth, 4, dl), (depth, 3, dl),
                  final_g.shape] + [(d,)] * (depth - 1)
    e = _unpack(_slot_sum(early_all[0], name="sum_early_vectors"), vec_shapes)
    g_wr, g_wi = _unpack(_slot_sum(early_all[1], name="sum_early_maps"), [lru_wr.shape, lru_wi.shape])
    g_norm = jnp.stack([late_sum[0]] + e[7:][::-1])
    g_meta = lax.dynamic_slice_in_dim(late_sum[1], me * mcol, mcol, axis=1)
    g_wa = lax.dynamic_slice_in_dim(e[4], me * ccol, ccol, axis=2)
    g_wb = lax.dynamic_slice_in_dim(e[5], me * ccol, ccol, axis=2)

    small_w = [norm_g, conv_a_b, lru_wr, lru_br, lru_wi, lru_bi, lru_lambda, final_g, meta, conv_a_w, conv_b_w]
    small_m = [m_norm_g, m_conv_a_b, m_lru_wr, m_lru_br, m_lru_wi, m_lru_bi, m_lru_lambda, m_final_g, m_meta,
               m_conv_a_w, m_conv_b_w]
    small_v = [v_norm_g, v_conv_a_b, v_lru_wr, v_lru_br, v_lru_wi, v_lru_bi, v_lru_lambda, v_final_g, v_meta,
               v_conv_a_w, v_conv_b_w]
    small_g = [g_norm, e[0], g_wr, e[1], g_wi, e[2], e[3], e[6], g_meta, g_wa, g_wb]
    small_out = _adamw(_pack(small_w), _pack(small_g), _pack(small_m), _pack(small_v), name="adamw_small")
    small_shapes = [a.shape for a in small_w]
    s_grad, s_delta, s_m, s_v = [_unpack(o, small_shapes) for o in small_out]

    win_out = None
    wout_out = None
    after = small_out[0]
    for l in reversed(range(depth)):
        src, landed = _push_wait(sent_out[l], [_scatter_lead], after, name=f"scatter_wout_{l}_wait")
        own = lax.dynamic_index_in_dim(src[0], me, 0, keepdims=False)
        wout_out = _adamw(w_out, own, m_w_out, v_w_out, landed=landed[0], layer=l, depth=depth,
                          into=wout_out, name=f"adamw_w_out_{l}")
        after = wout_out[0]
        for p, sent in enumerate(sent_in[l]):
            src, landed = _push_wait(sent, scatter_in, after, name=f"scatter_win_{l}_{p}_wait")
            own = lax.dynamic_slice_in_dim(src[0], me * wcol, wcol, axis=1)
            win_out = _adamw(w_in, own, m_w_in, v_w_in, landed=landed[0], layer=l, depth=depth,
                             into=win_out, row_off=p * own.shape[0], name=f"adamw_w_in_{l}_{p}")
            after = win_out[0]

    names = ["norm_g", "conv_a_b", "lru_wr", "lru_br", "lru_wi", "lru_bi", "lru_lambda", "final_g", "meta",
             "conv_a_w", "conv_b_w"]
    order = ["meta", "norm_g", "w_in", "conv_a_w", "conv_a_b", "lru_wr", "lru_br", "lru_wi", "lru_bi", "lru_lambda",
             "conv_b_w", "w_out", "final_g"]

    def family(idx, small):
        table = {nm: small[i] for i, nm in enumerate(names)}
        table["w_in"] = win_out[idx]
        table["w_out"] = wout_out[idx]
        return [table[nm] for nm in order]

    return (loss, grad_x, *family(0, s_grad), *family(1, s_delta), *family(2, s_m), *family(3, s_v))
```

```python
import functools

import jax
import jax.numpy as jnp
from jax import lax
from jax.experimental import pallas as pl
from jax.experimental.pallas import tpu as pltpu

F32 = jnp.float32
BF16 = jnp.bfloat16
MESH = pl.DeviceIdType.MESH

N_DEV = 8
RMS_EPS = 1e-6
LRU_C = 8.0
ADAM_LR = 0.001
ADAM_B1 = 0.9
ADAM_B2 = 0.999
ADAM_EPS = 1e-08
ADAM_WD = 0.01
ADAM_STEP = 10

V7X_VMEM_LIMIT = 52 * 1024 * 1024
LANES = 128
SUBLANES = 8
TOKEN_TILE = 384
MIX_ROWS = 128
SHIFTED_ROWS = 128
GATE_BLOCK = 128


def _params(sem):
    return pltpu.CompilerParams(dimension_semantics=sem, vmem_limit_bytes=V7X_VMEM_LIMIT)


def _tile(n, target, align=LANES):
    best = None
    for t in range(align, min(n, target) + 1, align):
        if n % t == 0:
            best = t
    return n if best is None else best


def _sigmoid(z):
    return 0.5 * jnp.tanh(0.5 * z) + 0.5


def _softplus(z):
    e = jnp.exp(-jnp.abs(z))
    u = 1.0 + e
    l1p = jnp.where(u == 1.0, e, jnp.log(u) * e / jnp.where(u == 1.0, 1.0, u - 1.0))
    return jnp.maximum(z, 0.0) + l1p


def _matmul(a, b, *, ta=False, tb=False, tm, tn, tk, out_dtype=F32, add=None, dep=None, m_part=None, name):
    m, k = (a.shape[1], a.shape[0]) if ta else a.shape
    m_off = 0
    if m_part is not None:
        assert add is None and m % (m_part[1] * tm) == 0
        m //= m_part[1]
        m_off = m_part[0] * (m // tm)
    n, kb = b.shape if tb else b.shape[::-1]
    assert kb == k
    assert m % tm == 0 and n % tn == 0 and k % tk == 0, (m, n, k, tm, tn, tk)
    nk = k // tk
    a_spec = pl.BlockSpec((tk, tm), lambda i, j, q: (q, i + m_off)) if ta \
        else pl.BlockSpec((tm, tk), lambda i, j, q: (i + m_off, q))
    b_spec = pl.BlockSpec((tn, tk), lambda i, j, q: (j, q)) if tb else pl.BlockSpec((tk, tn), lambda i, j, q: (q, j))
    o_spec = pl.BlockSpec((tm, tn), lambda i, j, q: (i, j))
    o_shape = (m, n)
    dims = (((0 if ta else 1,), (1 if tb else 0,)), ((), ()))
    has_add = add is not None
    has_dep = dep is not None

    def body(*refs):
        if has_dep:
            refs = refs[:-3] + refs[-2:]
        if has_add:
            a_ref, b_ref, add_ref, o_ref, acc_ref = refs
        else:
            a_ref, b_ref, o_ref, acc_ref = refs
        q = pl.program_id(2)
        part = lax.dot_general(a_ref[...], b_ref[...], dims, preferred_element_type=F32)

        def finish(acc):
            if has_add:
                acc = acc + add_ref[...]
            o_ref[...] = acc.astype(out_dtype)

        if nk == 1:
            finish(part)
        else:
            @pl.when(q == 0)
            def _():
                acc_ref[...] = part

            @pl.when(jnp.logical_and(q > 0, q < nk - 1))
            def _():
                acc_ref[...] += part

            @pl.when(q == nk - 1)
            def _():
                finish(acc_ref[...] + part)

    in_specs = [a_spec, b_spec] + ([o_spec] if has_add else [])
    args = (a, b) + ((add,) if has_add else ())
    if has_dep:
        in_specs.append(pl.BlockSpec((SUBLANES, LANES), lambda i, j, q: (0, 0)))
        args += (dep,)
    acc_shape = (tm, tn) if nk > 1 else (SUBLANES, LANES)
    return pl.pallas_call(
        body, name=name,
        grid=(m // tm, n // tn, nk),
        in_specs=in_specs, out_specs=o_spec,
        out_shape=jax.ShapeDtypeStruct(o_shape, out_dtype),
        scratch_shapes=[pltpu.VMEM(acc_shape, F32)],
        compiler_params=_params(("parallel", "parallel", "arbitrary")),
    )(*args)


def _rms_fwd(h, g, *, name):
    tp, d = h.shape
    tr = _tile(tp, 512, SUBLANES)

    def body(h_ref, g_ref, o_ref):
        hv = h_ref[...]
        rstd = lax.rsqrt(jnp.mean(hv * hv, axis=-1, keepdims=True) + RMS_EPS)
        o_ref[...] = (hv * rstd * g_ref[...]).astype(BF16)

    return pl.pallas_call(
        body, name=name, grid=(tp // tr,),
        in_specs=[pl.BlockSpec((tr, d), lambda i: (i, 0)), pl.BlockSpec((1, d), lambda i: (0, 0))],
        out_specs=pl.BlockSpec((tr, d), lambda i: (i, 0)),
        out_shape=jax.ShapeDtypeStruct((tp, d), BF16),
        compiler_params=_params(("parallel",)),
    )(h, g.reshape(1, d))


def _rms_fwd_first(x, meta, g, *, tp, name):
    seq, d = x.shape
    n_meta = meta.shape[0]
    n_tok = n_meta + seq
    tr = SHIFTED_ROWS
    assert tp % tr == 0 and seq % tr == 0 and tr % n_meta == 0
    per = tr // n_meta

    def body(x_ref, xp_ref, m_ref, g_ref, h_ref, o_ref):
        i = pl.program_id(0)
        head = jnp.where(i == 0, m_ref[...], xp_ref[...])
        rows = i * tr + lax.broadcasted_iota(jnp.int32, (tr, 1), 0)
        hv = jnp.where(rows < n_tok, jnp.concatenate([head, x_ref[:tr - n_meta, :]], axis=0), 0.0)
        h_ref[...] = hv
        rstd = lax.rsqrt(jnp.mean(hv * hv, axis=-1, keepdims=True) + RMS_EPS)
        o_ref[...] = (hv * rstd * g_ref[...]).astype(BF16)

    row = pl.BlockSpec((tr, d), lambda i: (i, 0))
    own = pl.BlockSpec((tr, d), lambda i: (jnp.minimum(i, seq // tr - 1), 0))
    before = pl.BlockSpec((n_meta, d), lambda i: (jnp.maximum(i * per - 1, 0), 0))
    return pl.pallas_call(
        body, name=name, grid=(tp // tr,),
        in_specs=[own, before, pl.BlockSpec((n_meta, d), lambda i: (0, 0)), pl.BlockSpec((1, d), lambda i: (0, 0))],
        out_specs=[row, row],
        out_shape=[jax.ShapeDtypeStruct((tp, d), F32), jax.ShapeDtypeStruct((tp, d), BF16)],
        compiler_params=_params(("parallel",)),
    )(x, x, meta, g.reshape(1, d))


def _rms_bwd(h, dhn, dout, g, *, name):
    tp, d = h.shape
    tr = _tile(tp, 528, 2 * SUBLANES)

    def body(h_ref, dhn_ref, dout_ref, g_ref, dh_ref, dhb_ref, dg_ref):
        hv = h_ref[...]
        rstd = lax.rsqrt(jnp.mean(hv * hv, axis=-1, keepdims=True) + RMS_EPS)
        xhat = hv * rstd
        dn = dhn_ref[...]
        dxhat = dn * g_ref[...]
        dh = dout_ref[...] + rstd * (dxhat - xhat * jnp.mean(dxhat * xhat, axis=-1, keepdims=True))
        dh_ref[...] = dh
        dhb_ref[...] = dh.astype(BF16)
        part = jnp.sum(dn * xhat, axis=0, keepdims=True)

        @pl.when(pl.program_id(0) == 0)
        def _():
            dg_ref[...] = part

        @pl.when(pl.program_id(0) > 0)
        def _():
            dg_ref[...] += part

    row = pl.BlockSpec((tr, d), lambda i: (i, 0))
    vec = pl.BlockSpec((1, d), lambda i: (0, 0))
    return pl.pallas_call(
        body, name=name, grid=(tp // tr,),
        in_specs=[row, row, row, vec],
        out_specs=[row, row, vec],
        out_shape=[jax.ShapeDtypeStruct((tp, d), F32), jax.ShapeDtypeStruct((tp, d), BF16),
                   jax.ShapeDtypeStruct((1, d), F32)],
        compiler_params=_params(("arbitrary",)),
    )(h, dhn, dout, g.reshape(1, d))


def _rms_bwd_first(h, dhn, dout, g, *, n_meta, seq, name):
    tp, d = h.shape
    tr = SHIFTED_ROWS
    assert seq % tr == 0 and tr % n_meta == 0 and tp >= seq + n_meta
    nt = seq // tr
    per = tr // n_meta

    def grads(hv, dn, do, gv):
        rstd = lax.rsqrt(jnp.mean(hv * hv, axis=-1, keepdims=True) + RMS_EPS)
        xhat = hv * rstd
        dxhat = dn * gv
        dh = do + rstd * (dxhat - xhat * jnp.mean(dxhat * xhat, axis=-1, keepdims=True))
        return dh, jnp.sum(dn * xhat, axis=0, keepdims=True)

    def body(h_ref, dhn_ref, dout_ref, hn_ref, dhnn_ref, doutn_ref, g_ref, gx_ref, dmeta_ref, dg_ref):
        i = pl.program_id(0)
        gv = g_ref[...]
        dh, part = grads(h_ref[...], dhn_ref[...], dout_ref[...], gv)
        dh_next, part_next = grads(hn_ref[...], dhnn_ref[...], doutn_ref[...], gv)
        gx_ref[...] = jnp.concatenate([dh[n_meta:], dh_next], axis=0)

        @pl.when(i == 0)
        def _():
            dmeta_ref[...] = dh[:n_meta]
            dg_ref[...] = part

        @pl.when(i > 0)
        def _():
            dg_ref[...] += part

        @pl.when(i == nt - 1)
        def _():
            dg_ref[...] += part_next

    row = pl.BlockSpec((tr, d), lambda i: (i, 0))
    nxt = pl.BlockSpec((n_meta, d), lambda i: ((i + 1) * per, 0))
    vec = pl.BlockSpec((1, d), lambda i: (0, 0))
    return pl.pallas_call(
        body, name=name, grid=(nt,),
        in_specs=[row, row, row, nxt, nxt, nxt, vec],
        out_specs=[pl.BlockSpec((None, tr, d), lambda i: (0, i, 0)), pl.BlockSpec((n_meta, d), lambda i: (0, 0)), vec],
        out_shape=[jax.ShapeDtypeStruct((1, seq, d), F32), jax.ShapeDtypeStruct((n_meta, d), F32),
                   jax.ShapeDtypeStruct((1, d), F32)],
        compiler_params=_params(("arbitrary",)),
    )(h, dhn, dout, h, dhn, dout, g.reshape(1, d))


def _loss_head(h, tgt, g, *, n_meta, n_tok, name):
    tp, d = h.shape
    seq = tgt.shape[0]
    tr = SHIFTED_ROWS
    assert tp % tr == 0 and seq % tr == 0 and tr % n_meta == 0
    per = tr // n_meta

    def body(h_ref, t_ref, tp_ref, g_ref, dh_ref, dhb_ref, dg_ref, loss_ref):
        i = pl.program_id(0)
        hv = h_ref[...]
        rstd = lax.rsqrt(jnp.mean(hv * hv, axis=-1, keepdims=True) + RMS_EPS)
        xhat = hv * rstd
        gv = g_ref[...]
        rows = i * tr + lax.broadcasted_iota(jnp.int32, (tr, 1), 0)
        valid = jnp.logical_and(rows >= n_meta, rows < n_tok)
        target = jnp.concatenate([tp_ref[...], t_ref[:tr - n_meta, :]], axis=0)
        err = jnp.where(valid, xhat * gv - target, 0.0)
        dy = err * (1.0 / d)
        dxhat = dy * gv
        dh = rstd * (dxhat - xhat * jnp.mean(dxhat * xhat, axis=-1, keepdims=True))
        dh_ref[...] = dh
        dhb_ref[...] = dh.astype(BF16)
        dg_part = jnp.sum(dy * xhat, axis=0, keepdims=True)
        per_row = jnp.sum(err * err, axis=-1, keepdims=True) * (1.0 / d)
        loss_part = jnp.broadcast_to(0.5 * jnp.sum(per_row, axis=0, keepdims=True), (SUBLANES, LANES))

        @pl.when(i == 0)
        def _():
            dg_ref[...] = dg_part
            loss_ref[...] = loss_part

        @pl.when(i > 0)
        def _():
            dg_ref[...] += dg_part
            loss_ref[...] += loss_part

    row = pl.BlockSpec((tr, d), lambda i: (i, 0))
    vec = pl.BlockSpec((1, d), lambda i: (0, 0))
    own = pl.BlockSpec((tr, d), lambda i: (jnp.minimum(i, seq // tr - 1), 0))
    before = pl.BlockSpec((n_meta, d), lambda i: (jnp.maximum(i * per - 1, 0), 0))
    return pl.pallas_call(
        body, name=name, grid=(tp // tr,),
        in_specs=[row, own, before, vec],
        out_specs=[row, row, vec, pl.BlockSpec((SUBLANES, LANES), lambda i: (0, 0))],
        out_shape=[jax.ShapeDtypeStruct((tp, d), F32), jax.ShapeDtypeStruct((tp, d), BF16),
                   jax.ShapeDtypeStruct((1, d), F32), jax.ShapeDtypeStruct((SUBLANES, LANES), F32)],
        compiler_params=_params(("arbitrary",)),
    )(h, tgt, tgt, g.reshape(1, d))


def _shift_down(halo, tile, s):
    if s == 0:
        return tile
    ext = jnp.concatenate([halo, tile], axis=0)
    return pltpu.roll(ext, s, 0)[SUBLANES:]


def _shift_up(tile, head, s):
    if s == 0:
        return tile
    ext = jnp.concatenate([tile, head], axis=0)
    n = ext.shape[0]
    return pltpu.roll(ext, n - s, 0)[: tile.shape[0]]


def _to_lane_blocks(ref, cols, val):
    for j in range(cols.start // LANES, cols.stop // LANES):
        ref[j] = val[:, j * LANES - cols.start:(j + 1) * LANES - cols.start]


def _from_lane_blocks(ref, cols):
    return jnp.concatenate([ref[j] for j in range(cols.start // LANES, cols.stop // LANES)], axis=1)


def _scan_tile(a_ref, b_ref, out_ref, carry, j, *, reverse):
    ng = a_ref.shape[1] // SUBLANES
    order = list(range(SUBLANES))[::-1] if reverse else list(range(SUBLANES))

    def rows(r):
        return pl.ds(r, ng, stride=SUBLANES)

    prod, loc = {}, {}
    prev = None
    for r in order:
        ar = a_ref[j, rows(r), :]
        br = b_ref[j, rows(r), :]
        prod[r] = ar if prev is None else ar * prod[prev]
        loc[r] = br if prev is None else ar * loc[prev] + br
        prev = r
    pg, lg = prod[prev], loc[prev]
    ones = jnp.ones((SUBLANES,) + pg.shape[1:], F32)
    zeros = jnp.zeros_like(ones)
    s = 1
    while s < ng:
        p_sh = _shift_up(pg, ones, s) if reverse else _shift_down(ones, pg, s)
        l_sh = _shift_up(lg, zeros, s) if reverse else _shift_down(zeros, lg, s)
        lg = pg * l_sh + lg
        pg = pg * p_sh
        s *= 2
    leaving = pg * carry[0:1, :] + lg
    entering = _shift_up(leaving, carry, 1) if reverse else _shift_down(carry, leaving, 1)
    for r in order:
        out_ref[j, rows(r), :] = loc[r] + prod[r] * entering
    last = leaving[0:1, :] if reverse else leaving[ng - 1:ng, :]
    return jnp.broadcast_to(last, carry.shape)


def _gates(ca, wr, wi, br, bi, sp):
    cab = ca.astype(BF16)
    r = _sigmoid(jnp.dot(cab, wr, preferred_element_type=F32) + br)
    ig = _sigmoid(jnp.dot(cab, wi, preferred_element_type=F32) + bi)
    la = -LRU_C * r * sp
    a = jnp.exp(la)
    mult = jnp.sqrt(-jnp.tanh(la) * (a * a + 1.0))
    return r, ig, a, mult


U_STREAMS = 3


class _ColumnParts:
    def __init__(self, refs):
        self.refs = refs
        self.width = refs[0].shape[-1]

    def __getitem__(self, idx):
        rows, cols = idx
        k = cols.start // self.width
        assert cols.stop <= (k + 1) * self.width
        return self.refs[k][rows, cols.start - k * self.width:cols.stop - k * self.width]


def _mixer_fwd(u, wa, ba, wr_blk, br, wi_blk, bi, lam, wb, *, name):
    tp, din = u.shape
    dl = din // 6
    tt = MIX_ROWS
    cw = GATE_BLOCK
    nch = dl // cw
    assert tp % tt == 0 and dl % cw == 0

    def body(*refs):
        u_ref = _ColumnParts(refs[:U_STREAMS])
        (wa_ref, ba_ref, wr_ref, br_ref, wi_ref, bi_ref, lam_ref, wb_ref,
         s_ref, y_ref, xa_tail, v_tail, h_carry, a_s, b_s, h_s) = refs[U_STREAMS:]

        @pl.when(pl.program_id(0) == 0)
        def _():
            xa_tail[...] = jnp.zeros_like(xa_tail)
            v_tail[...] = jnp.zeros_like(v_tail)
            h_carry[...] = jnp.zeros_like(h_carry)

        for ch in range(nch):
            cs = slice(ch * cw, (ch + 1) * cw)

            def seg(s):
                return slice(s * dl + ch * cw, s * dl + (ch + 1) * cw)

            xa = u_ref[:, seg(0)]
            halo = xa_tail[:, cs]
            ca = ba_ref[:, cs] + wa_ref[3:4, cs] * xa
            for kk in range(3):
                ca = ca + wa_ref[kk:kk + 1, cs] * _shift_down(halo, xa, 3 - kk)
            xa_tail[:, cs] = xa[tt - SUBLANES:]
            s_ref[:, cs] = ca
            sp = _softplus(-lam_ref[:, cs])
            _, ig, a, mult = _gates(ca, wr_ref[ch], wi_ref[ch], br_ref[:, cs], bi_ref[:, cs], sp)
            _to_lane_blocks(a_s, cs, a)
            _to_lane_blocks(b_s, cs, mult * (ig * ca))

            bv = u_ref[:, seg(2)]
            v = u_ref[:, seg(3)] * u_ref[:, seg(4)]
            gb = u_ref[:, seg(5)]
            vh = v_tail[:, cs]
            cb = wb_ref[2:3, cs] * v
            for kk in range(2):
                cb = cb + wb_ref[kk:kk + 1, cs] * _shift_down(vh, v, 2 - kk)
            v_tail[:, cs] = v[tt - SUBLANES:]
            y_ref[:, dl + ch * cw: dl + (ch + 1) * cw] = (bv * cb * (gb * _sigmoid(gb))).astype(BF16)

        for ch in range(nch):
            cs = slice(ch * cw, (ch + 1) * cw)
            for j in range(cs.start // LANES, cs.stop // LANES):
                lanes = slice(j * LANES, (j + 1) * LANES)
                h_carry[:, lanes] = _scan_tile(a_s, b_s, h_s, h_carry[:, lanes], j, reverse=False)
            hsv = _from_lane_blocks(h_s, cs)
            s_ref[:, dl + ch * cw: dl + (ch + 1) * cw] = hsv
            ga = u_ref[:, dl + ch * cw: dl + (ch + 1) * cw]
            y_ref[:, cs] = (hsv * (ga * _sigmoid(ga))).astype(BF16)

    row = lambda w: pl.BlockSpec((tt, w), lambda i: (i, 0))
    full = lambda shp: pl.BlockSpec(shp, lambda i: tuple(0 for _ in shp))
    return pl.pallas_call(
        body, name=name, grid=(tp // tt,),
        in_specs=[pl.BlockSpec((tt, din // U_STREAMS), functools.partial(lambda k, i: (i, k), k))
                  for k in range(U_STREAMS)]
        + [full((4, dl)), full((1, dl)), full((nch, cw, cw)), full((1, dl)),
           full((nch, cw, cw)), full((1, dl)), full((1, dl)), full((3, dl))],
        out_specs=[row(2 * dl), row(2 * dl)],
        out_shape=[jax.ShapeDtypeStruct((tp, 2 * dl), F32), jax.ShapeDtypeStruct((tp, 2 * dl), BF16)],
        scratch_shapes=[pltpu.VMEM((SUBLANES, dl), F32), pltpu.VMEM((SUBLANES, dl), F32),
                        pltpu.VMEM((SUBLANES, dl), F32)] + [pltpu.VMEM((dl // LANES, tt, LANES), F32)] * 3,
        compiler_params=_params(("arbitrary",)),
    )(*[u] * U_STREAMS, wa, ba, wr_blk, br, wi_blk, bi, lam, wb)


SG_WA, SG_BA, SG_BR, SG_BI, SG_LAM, SG_WB, SG_ROWS = 0, 4, 5, 6, 7, 8, 16


def _mixer_bwd(u, saved, dy, wa, wr_blk, br, wi_blk, bi, lam, wb, *, name):
    tp, din = u.shape
    dl = din // 6
    tt = MIX_ROWS
    cw = GATE_BLOCK
    nch = dl // cw
    nt = tp // tt
    hb = tt // SUBLANES
    tn_dims = (((0,), (0,)), ((), ()))
    nt_dims = (((1,), (1,)), ((), ()))

    def body(*refs):
        u_ref = _ColumnParts(refs[:U_STREAMS])
        (uh_ref, s_ref, sh_ref, dy_ref, wa_ref, wr_ref, br_ref, wi_ref, bi_ref, lam_ref, wb_ref,
         du_ref, sg_ref, dwr_ref, dwi_ref,
         g_carry, a_head, dca_head, dcb_head, r_s, i_s, a_s, an_s, d_s, g_s) = refs[U_STREAMS:]
        i = pl.program_id(0)
        first_tile = i == nt - 1

        @pl.when(i == 0)
        def _():
            for ref in (g_carry, a_head, dca_head, dcb_head, sg_ref, dwr_ref, dwi_ref):
                ref[...] = jnp.zeros_like(ref)

        def halo_of(x):
            return jnp.where(first_tile, 0.0, x)

        for ch in range(nch):
            cs = slice(ch * cw, (ch + 1) * cw)
            cav = s_ref[:, cs]
            sp = _softplus(-lam_ref[:, cs])
            r, ig, a, _ = _gates(cav, wr_ref[ch], wi_ref[ch], br_ref[:, cs], bi_ref[:, cs], sp)
            r_s[:, cs] = r
            i_s[:, cs] = ig
            a_s[:, cs] = a
            _to_lane_blocks(an_s, cs, _shift_up(a, a_head[:, cs], 1))
            a_head[:, cs] = a[:SUBLANES]
            ga = u_ref[:, dl + ch * cw: dl + (ch + 1) * cw]
            _to_lane_blocks(d_s, cs, dy_ref[:, cs] * (ga * _sigmoid(ga)))

        for j in range(dl // LANES):
            lanes = slice(j * LANES, (j + 1) * LANES)
            g_carry[:, lanes] = _scan_tile(an_s, d_s, g_s, g_carry[:, lanes], j, reverse=True)

        def acc_row(r0, val):
            sg_ref[r0:r0 + 1, cs_cur[0]] += jnp.sum(val, axis=0, keepdims=True)

        cs_cur = [None]
        for ch in range(nch):
            cs = slice(ch * cw, (ch + 1) * cw)
            cs_cur[0] = cs

            def seg(s):
                return slice(s * dl + ch * cw, s * dl + (ch + 1) * cw)

            cav = s_ref[:, cs]
            r = r_s[:, cs]
            ig = i_s[:, cs]
            a = a_s[:, cs]
            g = _from_lane_blocks(g_s, cs)
            hsv = s_ref[:, dl + ch * cw: dl + (ch + 1) * cw]
            lamv = lam_ref[:, cs]
            sp = _softplus(-lamv)
            la = -LRU_C * r * sp
            e2 = a * a
            one_m_e2 = -jnp.tanh(la) * (e2 + 1.0)
            mult = jnp.sqrt(one_m_e2)
            hprev = _shift_down(halo_of(sh_ref[:, dl + ch * cw: dl + (ch + 1) * cw]), hsv, 1)
            icav = ig * cav
            dla = g * (hprev * a - icav * (e2 * lax.rsqrt(one_m_e2)))
            gm = g * mult
            dzi = gm * icav * (1.0 - ig)
            dca = gm * ig
            dla_r = dla * r
            dzr = dla_r * (1.0 - r) * (-LRU_C * sp)
            sg_ref[SG_LAM:SG_LAM + 1, cs] += jnp.sum(dla_r, axis=0, keepdims=True) * (LRU_C * _sigmoid(-lamv))
            acc_row(SG_BR, dzr)
            acc_row(SG_BI, dzi)
            dzr_b = dzr.astype(BF16)
            dzi_b = dzi.astype(BF16)
            cab = cav.astype(BF16)
            dca = dca + lax.dot_general(dzr_b, wr_ref[ch], nt_dims, preferred_element_type=F32)
            dca = dca + lax.dot_general(dzi_b, wi_ref[ch], nt_dims, preferred_element_type=F32)
            dwr_ref[ch] += lax.dot_general(cab, dzr_b, tn_dims, preferred_element_type=F32)
            dwi_ref[ch] += lax.dot_general(cab, dzi_b, tn_dims, preferred_element_type=F32)
            acc_row(SG_BA, dca)
            xa = u_ref[:, seg(0)]
            head = dca_head[:, cs]
            dxa = wa_ref[3:4, cs] * dca
            acc_row(SG_WA + 3, dca * xa)
            for kk in range(3):
                later = _shift_up(dca, head, 3 - kk)
                acc_row(SG_WA + kk, later * xa)
                dxa = dxa + wa_ref[kk:kk + 1, cs] * later
            dca_head[:, cs] = dca[:SUBLANES]
            ga = u_ref[:, seg(1)]
            sga = _sigmoid(ga)
            dga = dy_ref[:, cs] * hsv * (sga + (ga * sga) * (1.0 - sga))
            du_ref[:, seg(0)] = dxa.astype(BF16)
            du_ref[:, seg(1)] = dga.astype(BF16)

            bv = u_ref[:, seg(2)]
            cv = u_ref[:, seg(3)]
            xb = u_ref[:, seg(4)]
            gb = u_ref[:, seg(5)]
            dyb = dy_ref[:, dl + ch * cw: dl + (ch + 1) * cw]
            v = cv * xb
            vh = halo_of(uh_ref[:, seg(3)] * uh_ref[:, seg(4)])
            v1 = _shift_down(vh, v, 1)
            v2 = _shift_down(vh, v, 2)
            cb = wb_ref[2:3, cs] * v + wb_ref[1:2, cs] * v1 + wb_ref[0:1, cs] * v2
            sgb = _sigmoid(gb)
            sl = gb * sgb
            dyb_b = dyb * bv
            dyb_cb = dyb * cb
            dcb = dyb_b * sl
            du_ref[:, seg(2)] = (dyb_cb * sl).astype(BF16)
            du_ref[:, seg(5)] = (dyb_cb * bv * (sgb + sl * (1.0 - sgb))).astype(BF16)
            bhead = dcb_head[:, cs]
            dv = wb_ref[2:3, cs] * dcb
            acc_row(SG_WB + 2, dcb * v)
            for kk in range(2):
                later = _shift_up(dcb, bhead, 2 - kk)
                acc_row(SG_WB + kk, later * v)
                dv = dv + wb_ref[kk:kk + 1, cs] * later
            dcb_head[:, cs] = dcb[:SUBLANES]
            du_ref[:, seg(3)] = (dv * xb).astype(BF16)
            du_ref[:, seg(4)] = (dv * cv).astype(BF16)

    rev = lambda w: pl.BlockSpec((tt, w), lambda i: (nt - 1 - i, 0))
    halo = lambda w: pl.BlockSpec((SUBLANES, w), lambda i: (jnp.maximum((nt - 1 - i) * hb - 1, 0), 0))
    full = lambda shp: pl.BlockSpec(shp, lambda i: tuple(0 for _ in shp))
    vm = lambda r: pltpu.VMEM((r, dl), F32)
    return pl.pallas_call(
        body, name=name, grid=(nt,),
        in_specs=[pl.BlockSpec((tt, din // U_STREAMS), functools.partial(lambda k, i: (nt - 1 - i, k), k))
                  for k in range(U_STREAMS)]
        + [halo(din), rev(2 * dl), halo(2 * dl), rev(2 * dl), full((4, dl)),
           full((nch, cw, cw)), full((1, dl)), full((nch, cw, cw)), full((1, dl)), full((1, dl)), full((3, dl))],
        out_specs=[rev(din), full((SG_ROWS, dl)), full((nch, cw, cw)), full((nch, cw, cw))],
        out_shape=[jax.ShapeDtypeStruct((tp, din), BF16), jax.ShapeDtypeStruct((SG_ROWS, dl), F32),
                   jax.ShapeDtypeStruct((nch, cw, cw), F32), jax.ShapeDtypeStruct((nch, cw, cw), F32)],
        scratch_shapes=[vm(SUBLANES), vm(SUBLANES), vm(SUBLANES), vm(SUBLANES), vm(tt), vm(tt), vm(tt)]
        + [pltpu.VMEM((dl // LANES, tt, LANES), F32)] * 3,
        compiler_params=_params(("arbitrary",)),
    )(*[u] * (U_STREAMS + 1), saved, saved, dy, wa, wr_blk, br, wi_blk, bi, lam, wb)


def _adamw(w, g, m, v, *, name, landed=None, layer=None, depth=None, into=None, row_off=0):
    r, c = w.shape[-2:]
    rows = g.shape[0]
    tr = _tile(rows, 512, 2 * SUBLANES)
    assert row_off % tr == 0
    boff = row_off // tr
    bc1 = 1.0 - ADAM_B1 ** ADAM_STEP
    bc2 = 1.0 - ADAM_B2 ** ADAM_STEP
    slots = landed is not None

    def body(*refs):
        if into is not None:
            refs = refs[:-8] + refs[-4:]
        if slots:
            w_ref, g_ref, l_ref, m_ref, v_ref, grad_ref, delta_ref, nm_ref, nv_ref = refs
            gv = g_ref[...].astype(F32)
            for s in range(N_DEV - 1):
                gv = gv + l_ref[s].astype(F32)
        else:
            w_ref, g_ref, m_ref, v_ref, grad_ref, delta_ref, nm_ref, nv_ref = refs
            gv = g_ref[...]
        wv = w_ref[...]
        mn = ADAM_B1 * m_ref[...] + (1.0 - ADAM_B1) * gv
        vn = ADAM_B2 * v_ref[...] + (1.0 - ADAM_B2) * (gv * gv)
        m_hat = mn / bc1
        v_hat = vn / bc2
        grad_ref[...] = gv
        delta_ref[...] = -ADAM_LR * (m_hat / (jnp.sqrt(v_hat) + ADAM_EPS) + ADAM_WD * wv)
        nm_ref[...] = mn
        nv_ref[...] = vn

    if depth is None:
        blk = pl.BlockSpec((tr, c), lambda i: (i + boff, 0))
    else:
        blk = pl.BlockSpec((None, tr, c), lambda i: (layer, i + boff, 0))
    g_blk = pl.BlockSpec((tr, c), lambda i: (i, 0))
    l_spec = [pl.BlockSpec((N_DEV - 1, tr, c), lambda i: (0, i, 0))] if slots else []
    args = (w, g, landed, m, v) if slots else (w, g, m, v)
    in_specs = [blk, g_blk] + l_spec + [blk, blk]
    if depth is None:
        shp = jax.ShapeDtypeStruct((r, c), F32)
        out_blk = blk
    else:
        shp = jax.ShapeDtypeStruct((depth, r, c), F32)
        out_blk = pl.BlockSpec((None, tr, c), lambda i: (layer, i + boff, 0))
    aliases = {}
    if into is not None:
        aliases = {len(args) + j: j for j in range(4)}
        in_specs = in_specs + [ANY] * 4
        args = args + tuple(into)
    return pl.pallas_call(
        body, name=name, grid=(rows // tr,),
        in_specs=in_specs, out_specs=[out_blk] * 4,
        out_shape=[shp] * 4, input_output_aliases=aliases,
        compiler_params=_params(("parallel",)),
    )(*args)


def _slot_sum(g, *, name):
    _, r, c = g.shape
    tr = _tile(r, 512, SUBLANES)

    def body(g_ref, o_ref):
        gv = g_ref[0].astype(F32)
        for s in range(1, N_DEV):
            gv = gv + g_ref[s].astype(F32)
        o_ref[...] = gv

    return pl.pallas_call(
        body, name=name, grid=(r // tr,),
        in_specs=[pl.BlockSpec((N_DEV, tr, c), lambda i: (0, i, 0))],
        out_specs=pl.BlockSpec((tr, c), lambda i: (i, 0)),
        out_shape=jax.ShapeDtypeStruct((r, c), F32),
        compiler_params=_params(("parallel",)),
    )(g)


def _mesh_pos():
    x, y, c = lax.axis_index("x"), lax.axis_index("y"), lax.axis_index("c")
    return x, y, c, 4 * x + 2 * y + c


ANY = pl.BlockSpec(memory_space=pl.ANY)


GATHER_COPIES = 9


def _all_gather(srcs, out_shapes, views, *, name):
    n = len(srcs)
    SIB, X_OWN, Y_OWN, X_DIAG, Y_DIAG, SIB_X, SIB_Y, SIB_DIAG_TOP, SIB_DIAG_BOTTOM = range(GATHER_COPIES)

    def body(*refs):
        src = refs[:n]
        dst = refs[n:2 * n]
        send_sems, recv_sems, local_sems = refs[2 * n:]
        x, y, c, me = _mesh_pos()
        sibling, x_nbr, y_nbr = (x, y, 1 - c), (1 - x, y, c), (x, 1 - y, c)

        def block(a, px, py, pc, half=None):
            win = views[a](dst[a], 4 * px + 2 * py + pc)
            if half is None:
                return win
            rows = win.shape[0] // 2
            return win.at[pl.ds(half * rows, rows)]

        def copy(a, k, win, to, from_src=False):
            return pltpu.make_async_remote_copy(
                src_ref=src[a] if from_src else win, dst_ref=win,
                send_sem=send_sems.at[a * GATHER_COPIES + k], recv_sem=recv_sems.at[a * GATHER_COPIES + k],
                device_id=to, device_id_type=MESH)

        mine = [pltpu.make_async_copy(src[a], block(a, x, y, c), local_sems.at[a]) for a in range(n)]
        started = []

        def start(cp):
            cp.start()
            started.append(cp)

        for a in range(n):
            mine[a].start()
            own = block(a, x, y, c)
            start(copy(a, SIB, own, sibling, True))
            start(copy(a, X_OWN, own, x_nbr, True))
            start(copy(a, Y_OWN, own, y_nbr, True))
        for a in range(n):
            from_y = block(a, x, 1 - y, c)
            copy(a, Y_OWN, from_y, y_nbr).wait_recv()
            start(copy(a, X_DIAG, block(a, x, 1 - y, c, 0), x_nbr))
            start(copy(a, SIB_Y, from_y, sibling))
            from_x = block(a, 1 - x, y, c)
            copy(a, X_OWN, from_x, x_nbr).wait_recv()
            start(copy(a, Y_DIAG, block(a, 1 - x, y, c, 1), y_nbr))
            start(copy(a, SIB_X, from_x, sibling))
        for a in range(n):
            top = block(a, 1 - x, 1 - y, c, 0)
            copy(a, X_DIAG, top, x_nbr).wait_recv()
            start(copy(a, SIB_DIAG_TOP, top, sibling))
            bottom = block(a, 1 - x, 1 - y, c, 1)
            copy(a, Y_DIAG, bottom, y_nbr).wait_recv()
            start(copy(a, SIB_DIAG_BOTTOM, bottom, sibling))
        for a in range(n):
            copy(a, SIB, block(a, x, y, 1 - c), sibling).wait_recv()
            copy(a, SIB_X, block(a, 1 - x, y, 1 - c), sibling).wait_recv()
            copy(a, SIB_Y, block(a, x, 1 - y, 1 - c), sibling).wait_recv()
            copy(a, SIB_DIAG_TOP, block(a, 1 - x, 1 - y, 1 - c, 0), sibling).wait_recv()
            copy(a, SIB_DIAG_BOTTOM, block(a, 1 - x, 1 - y, 1 - c, 1), sibling).wait_recv()
        for cp in started:
            cp.wait_send()
        for cp in mine:
            cp.wait()

    return pl.pallas_call(
        body, name=name,
        in_specs=[ANY] * n, out_specs=[ANY] * n,
        out_shape=[jax.ShapeDtypeStruct(s, x.dtype) for s, x in zip(out_shapes, srcs)],
        scratch_shapes=[pltpu.SemaphoreType.DMA((GATHER_COPIES * n,)), pltpu.SemaphoreType.DMA((GATHER_COPIES * n,)),
                        pltpu.SemaphoreType.DMA((n,))],
    )(*srcs)


HBM = pl.BlockSpec(memory_space=pltpu.HBM)
SEM = pl.BlockSpec(memory_space=pltpu.SEMAPHORE)
EFFECT = pltpu.SideEffectType.DATAFLOW_SIDE_EFFECTING


def _peer_of(x, y, c, k):
    return (1 - x if k & 4 else x, 1 - y if k & 2 else y, 1 - c if k & 1 else c)


def _peer_copies(n, wins, src, land, send_sems, recv_sems):
    x, y, c, me = _mesh_pos()
    out = []
    for a in range(n):
        for k in range(1, N_DEV):
            px, py, pc = _peer_of(x, y, c, k)
            s_win, d_win = wins[a](src[a], land[a], me, 4 * px + 2 * py + pc, k)
            out.append(pltpu.make_async_remote_copy(
                src_ref=s_win, dst_ref=d_win,
                send_sem=send_sems.at[a * 7 + k - 1], recv_sem=recv_sems.at[a * 7 + k - 1],
                device_id=(px, py, pc), device_id_type=MESH))
    return out


def _push_start(srcs, lands, wins, *, name):
    n = len(srcs)

    def body(*refs):
        src = refs[:n]
        land = refs[n:2 * n]
        send_sems, recv_sems = refs[2 * n], refs[2 * n + 1]
        token = refs[-1]
        for cp in _peer_copies(n, wins, src, land, send_sems, recv_sems):
            cp.start()
        token[...] = jnp.zeros_like(token)

    bufs = (*srcs, *lands)
    return pl.pallas_call(
        body, name=name,
        out_shape=(pltpu.SemaphoreType.DMA((7 * n,)), pltpu.SemaphoreType.DMA((7 * n,)),
                   *[pltpu.HBM(v.shape, v.dtype) for v in bufs], jax.ShapeDtypeStruct((SUBLANES, LANES), F32)),
        in_specs=[HBM] * (2 * n),
        out_specs=(SEM, SEM, *[HBM] * (2 * n), pl.BlockSpec(memory_space=pltpu.VMEM)),
        input_output_aliases={i: 2 + i for i in range(2 * n)},
        compiler_params=pltpu.CompilerParams(has_side_effects=EFFECT),
    )(*[pltpu.with_memory_space_constraint(v, pltpu.HBM) for v in bufs])


def _push_wait(handle, wins, after, *, name):
    send_sems, recv_sems, *bufs, _ = handle
    n = len(bufs) // 2

    def body(*refs):
        src = refs[:n]
        land = refs[n:2 * n]
        for cp in _peer_copies(n, wins, src, land, refs[2 * n], refs[2 * n + 1]):
            cp.wait_send()
            cp.wait_recv()

    outs = pl.pallas_call(
        body, name=name,
        out_shape=tuple(pltpu.HBM(v.shape, v.dtype) for v in bufs),
        in_specs=[HBM] * (2 * n) + [SEM, SEM, ANY],
        out_specs=tuple([HBM] * (2 * n)),
        input_output_aliases={i: i for i in range(2 * n)},
        compiler_params=pltpu.CompilerParams(has_side_effects=EFFECT),
    )(*bufs, send_sems, recv_sems, after)
    return outs[:n], outs[n:]


def _gather_lead(src, land, me, peer, k):
    return src, land.at[me]


def _gather_cols(width):
    def win(src, land, me, peer, k):
        return src, land.at[:, pl.ds(me * width, width)]
    return win


def _scatter_lead(src, land, me, peer, k):
    return src.at[peer], land.at[k - 1]


def _scatter_cols(width):
    def win(src, land, me, peer, k):
        return src.at[:, pl.ds(peer * width, width)], land.at[k - 1]
    return win


def _place_block(own, *, cols, name):
    rows, width = own.shape
    tr = _tile(rows, 512, 2 * SUBLANES)
    _, _, _, me = _mesh_pos()

    def body(me_ref, x_ref, o_ref):
        o_ref[...] = x_ref[...]

    if cols:
        out_spec = pl.BlockSpec((tr, width), lambda i, me_ref: (i, me_ref[0]))
        shape = (rows, N_DEV * width)
    else:
        out_spec = pl.BlockSpec((None, tr, width), lambda i, me_ref: (me_ref[0], i, 0))
        shape = (N_DEV, rows, width)
    return pl.pallas_call(
        body, name=name,
        grid_spec=pltpu.PrefetchScalarGridSpec(
            num_scalar_prefetch=1, grid=(rows // tr,),
            in_specs=[pl.BlockSpec((tr, width), lambda i, me_ref: (i, 0))], out_specs=out_spec),
        out_shape=jax.ShapeDtypeStruct(shape, own.dtype),
        compiler_params=_params(("arbitrary",)),
    )(me.astype(jnp.int32).reshape(1), own)


def _dep(x, token):
    return x + token[0, 0].astype(x.dtype)


def _lead(ref, d):
    return ref.at[d]


def _col_window(width):
    def view(ref, d):
        return ref.at[:, pl.ds(d * width, width)]
    return view


def _pack(arrs):
    flat = jnp.concatenate([a.reshape(-1).astype(F32) for a in arrs])
    n = flat.shape[0]
    rows = -(-n // (2 * SUBLANES * LANES)) * 2 * SUBLANES
    return jnp.pad(flat, (0, rows * LANES - n)).reshape(rows, LANES)


def _unpack(buf, shapes):
    flat = buf.reshape(-1)
    out, off = [], 0
    for s in shapes:
        n = 1
        for q in s:
            n *= q
        out.append(flat[off:off + n].reshape(s))
        off += n
    return out


def _blockdiag(w, cw):
    h, hd, _ = w.shape
    per = cw // hd
    wg = w.reshape(h // per, per, hd, hd)
    eye = jnp.eye(per, dtype=w.dtype)
    blk = jnp.einsum("gpij,pq->gpiqj", wg, eye)
    return blk.reshape(h // per, cw, cw).astype(BF16)


def _blockdiag_extract(g, hd):
    n, cw, _ = g.shape
    per = cw // hd
    g5 = g.reshape(n, per, hd, per, hd)
    idx = jnp.arange(per)
    return g5[:, idx, :, idx, :].transpose(1, 0, 2, 3).reshape(n * per, hd, hd)


def kernel(x, meta, norm_g, w_in, conv_a_w, conv_a_b, lru_wr, lru_br, lru_wi, lru_bi, lru_lambda, conv_b_w, w_out, final_g, loss_target, m_meta, m_norm_g, m_w_in, m_conv_a_w, m_conv_a_b, m_lru_wr, m_lru_br, m_lru_wi, m_lru_bi, m_lru_lambda, m_conv_b_w, m_w_out, m_final_g, v_meta, v_norm_g, v_w_in, v_conv_a_w, v_conv_a_b, v_lru_wr, v_lru_br, v_lru_wi, v_lru_bi, v_lru_lambda, v_conv_b_w, v_w_out, v_final_g):
    _, seq, d = x.shape
    n_meta = meta.shape[0]
    depth = w_in.shape[0]
    din = w_in.shape[2] * N_DEV
    dl = din // 6
    dmix = 2 * dl
    wcol = w_in.shape[2]
    wrow = w_out.shape[1]
    mcol = meta.shape[1]
    ccol = conv_a_w.shape[2]
    heads, hd = lru_wr.shape[1], lru_wr.shape[2]
    n_tok = n_meta + seq
    tp = -(-n_tok // TOKEN_TILE) * TOKEN_TILE
    me = 4 * lax.axis_index("x") + 2 * lax.axis_index("y") + lax.axis_index("c")

    bf = lambda a: a.astype(BF16)
    small_mine = _pack([meta, conv_a_w, conv_b_w])
    first = _all_gather([bf(w_in[0]), small_mine], [(d, din), (N_DEV,) + small_mine.shape],
                        [_col_window(wcol), _lead], name="gather_first")
    flat = first[1].reshape(N_DEV, -1)
    sizes = [meta.size, conv_a_w.size, conv_b_w.size]
    meta_full = jnp.moveaxis(flat[:, :sizes[0]].reshape(N_DEV, n_meta, mcol), 0, 1).reshape(n_meta, d)
    wa_full = jnp.moveaxis(flat[:, sizes[0]:sizes[0] + sizes[1]].reshape(N_DEV, depth, 4, ccol), 0, 2) \
        .reshape(depth, 4, dl)
    wb_full = jnp.moveaxis(flat[:, sizes[0] + sizes[1]:sum(sizes)].reshape(N_DEV, depth, 3, ccol), 0, 2) \
        .reshape(depth, 3, dl)
    w_in_full = [None] * depth
    w_out_full = [None] * depth

    push_out = [None] * depth
    push_in = [None] * depth
    w_in_full[0], src = lax.optimization_barrier((first[0], bf(w_out[0])))
    push_out[0] = _push_start([src], [_place_block(src, cols=False, name="place_wout_0")], [_gather_lead],
                              name="gather_wout_0_start")
    token = push_out[0][-1]
    for l in range(1, depth):
        src = bf(_dep(w_in[l], token))
        push_in[l] = _push_start([src], [_place_block(src, cols=True, name=f"place_win_{l}")], [_gather_cols(wcol)],
                                 name=f"gather_win_{l}_start")
        src = bf(_dep(w_out[l], push_in[l][-1]))
        push_out[l] = _push_start([src], [_place_block(src, cols=False, name=f"place_wout_{l}")], [_gather_lead],
                                  name=f"gather_wout_{l}_start")
        token = push_out[l][-1]

    wr_blk = [_blockdiag(lru_wr[l], GATE_BLOCK) for l in range(depth)]
    wi_blk = [_blockdiag(lru_wi[l], GATE_BLOCK) for l in range(depth)]
    vec = lambda a: a.reshape(1, dl)

    tm = _tile(tp, 1408)
    saved = []
    for l in range(depth):
        if l == 0:
            h, hn = _rms_fwd_first(x[0], meta_full, _dep(norm_g[l], token), tp=tp, name=f"rms_fwd_{l}")
        else:
            hn = _rms_fwd(h, norm_g[l], name=f"rms_fwd_{l}")
        if l > 0:
            _, landed = _push_wait(push_in[l], [_gather_cols(wcol)], hn, name=f"gather_win_{l}_wait")
            w_in_full[l] = landed[0]
        u = _matmul(hn, w_in_full[l], tm=tm, tn=_tile(din, 768), tk=d, name=f"mm_u_{l}")
        mixed, y = _mixer_fwd(u, wa_full[l], vec(conv_a_b[l]), wr_blk[l], vec(lru_br[l]), wi_blk[l], vec(lru_bi[l]),
                              vec(lru_lambda[l]), wb_full[l], name=f"mixer_fwd_{l}")
        _, landed = _push_wait(push_out[l], [_gather_lead], y, name=f"gather_wout_{l}_wait")
        w_out_full[l] = landed[0].reshape(dmix, d)
        h_next = _matmul(y, w_out_full[l], tm=tm, tn=_tile(d, 512), tk=dmix, add=h, name=f"mm_out_{l}")
        saved.append((h, hn, u, mixed, y))
        h = h_next

    dh, dhb, dg_final, loss_part = _loss_head(h, loss_target[0], final_g, n_meta=n_meta, n_tok=n_tok,
                                              name="loss_head")

    small_grads = [None] * depth
    sent_out = [None] * depth
    sent_in = [None] * depth
    scatter_in = [_scatter_cols(wcol)]
    token = None
    dg_norms = []
    for l in reversed(range(depth)):
        h_in, hn, u, mixed, y = saved[l]
        dy = _matmul(dhb, w_out_full[l], tb=True, tm=tm, tn=_tile(dmix, 512), tk=d, dep=token, name=f"mm_dy_{l}")
        dw_out = _matmul(y, dhb, ta=True, tm=_tile(dmix, 512), tn=_tile(d, 1024), tk=tp, out_dtype=BF16,
                         name=f"mm_dwout_{l}")
        sent_out[l] = _push_start([dw_out.reshape(N_DEV, wrow, d)], [lax.empty((N_DEV - 1, wrow, d), BF16)],
                                  [_scatter_lead], name=f"scatter_wout_{l}_start")
        du, sg, dwr, dwi = _mixer_bwd(u, mixed, dy, wa_full[l], wr_blk[l], vec(lru_br[l]), wi_blk[l], vec(lru_bi[l]),
                                      vec(lru_lambda[l]), _dep(wb_full[l], sent_out[l][-1]), name=f"mixer_bwd_{l}")
        small_grads[l] = (sg, dwr, dwi)
        if l == 0:
            rows = jnp.stack([small_grads[j][0] for j in range(depth)])
            early = [_pack([
                rows[:, SG_BA], rows[:, SG_BR], rows[:, SG_BI], rows[:, SG_LAM], rows[:, SG_WA:SG_WA + 4],
                rows[:, SG_WB:SG_WB + 3], dg_final[0], *dg_norms]),
                _pack([jnp.stack([_blockdiag_extract(small_grads[j][1], hd) for j in range(depth)]),
                       jnp.stack([_blockdiag_extract(small_grads[j][2], hd) for j in range(depth)])]).astype(BF16)]
            early_land = [lax.dynamic_update_slice(lax.empty((N_DEV,) + a.shape, a.dtype), a[None], (me, 0, 0))
                          for a in early]
            sent_early = _push_start(early, early_land, [_gather_lead] * 2, name="gather_early_grads_start")
        parts = 2 if l == 0 else 1
        token = sent_early[-1] if l == 0 else None
        sent_in[l] = []
        for p in range(parts):
            dw_in = _matmul(hn, du, ta=True, tm=_tile(d // parts, 512), tn=_tile(din, 768), tk=tp, out_dtype=BF16,
                            dep=token, m_part=(p, parts), name=f"mm_dwin_{l}_{p}")
            sent_in[l].append(_push_start([dw_in], [lax.empty((N_DEV - 1, d // parts, wcol), BF16)], scatter_in,
                                          name=f"scatter_win_{l}_{p}_start"))
            token = sent_in[l][-1][-1]
        dhn = _matmul(du, w_in_full[l], tb=True, tm=_tile(tp, 704, 2 * SUBLANES), tn=_tile(d, 512), tk=din, dep=token,
                      name=f"mm_dhn_{l}")
        if l > 0:
            dh, dhb, dg_norm = _rms_bwd(h_in, dhn, dh, norm_g[l], name=f"rms_bwd_{l}")
            dg_norms.append(dg_norm[0])
        else:
            grad_x, d_meta, dg_norm = _rms_bwd_first(h_in, dhn, dh, norm_g[l], n_meta=n_meta, seq=seq,
                                                     name=f"rms_bwd_{l}")

    late = _pack([dg_norm[0], d_meta, loss_part[0:1, 0:1]])
    late_all = _all_gather([late], [(N_DEV,) + late.shape], [_lead], name="gather_late_grads")[0]
    late_sum = _unpack(_slot_sum(late_all, name="sum_late_grads"), [(d,), (n_meta, d), ()])
    loss = late_sum[2]
    _, early_all = _push_wait(sent_early, [_gather_lead] * 2, late_sum[0], name="gather_early_grads_wait")
    vec_shapes = [conv_a_b.shape, lru_br.shape, lru_bi.shape, lru_lambda.shape, (depth, 4, dl), (depth, 3, dl),
                  final_g.shape] + [(d,)] * (depth - 1)
    e = _unpack(_slot_sum(early_all[0], name="sum_early_vectors"), vec_shapes)
    g_wr, g_wi = _unpack(_slot_sum(early_all[1], name="sum_early_maps"), [lru_wr.shape, lru_wi.shape])
    g_norm = jnp.stack([late_sum[0]] + e[7:][::-1])
    g_meta = lax.dynamic_slice_in_dim(late_sum[1], me * mcol, mcol, axis=1)
    g_wa = lax.dynamic_slice_in_dim(e[4], me * ccol, ccol, axis=2)
    g_wb = lax.dynamic_slice_in_dim(e[5], me * ccol, ccol, axis=2)

    small_w = [norm_g, conv_a_b, lru_wr, lru_br, lru_wi, lru_bi, lru_lambda, final_g, meta, conv_a_w, conv_b_w]
    small_m = [m_norm_g, m_conv_a_b, m_lru_wr, m_lru_br, m_lru_wi, m_lru_bi, m_lru_lambda, m_final_g, m_meta,
               m_conv_a_w, m_conv_b_w]
    small_v = [v_norm_g, v_conv_a_b, v_lru_wr, v_lru_br, v_lru_wi, v_lru_bi, v_lru_lambda, v_final_g, v_meta,
               v_conv_a_w, v_conv_b_w]
    small_g = [g_norm, e[0], g_wr, e[1], g_wi, e[2], e[3], e[6], g_meta, g_wa, g_wb]
    small_out = _adamw(_pack(small_w), _pack(small_g), _pack(small_m), _pack(small_v), name="adamw_small")
    small_shapes = [a.shape for a in small_w]
    s_grad, s_delta, s_m, s_v = [_unpack(o, small_shapes) for o in small_out]

    win_out = None
    wout_out = None
    after = small_out[0]
    for l in reversed(range(depth)):
        src, landed = _push_wait(sent_out[l], [_scatter_lead], after, name=f"scatter_wout_{l}_wait")
        own = lax.dynamic_index_in_dim(src[0], me, 0, keepdims=False)
        wout_out = _adamw(w_out, own, m_w_out, v_w_out, landed=landed[0], layer=l, depth=depth,
                          into=wout_out, name=f"adamw_w_out_{l}")
        after = wout_out[0]
        for p, sent in enumerate(sent_in[l]):
            src, landed = _push_wait(sent, scatter_in, after, name=f"scatter_win_{l}_{p}_wait")
            own = lax.dynamic_slice_in_dim(src[0], me * wcol, wcol, axis=1)
            win_out = _adamw(w_in, own, m_w_in, v_w_in, landed=landed[0], layer=l, depth=depth,
                             into=win_out, row_off=p * own.shape[0], name=f"adamw_w_in_{l}_{p}")
            after = win_out[0]

    names = ["norm_g", "conv_a_b", "lru_wr", "lru_br", "lru_wi", "lru_bi", "lru_lambda", "final_g", "meta",
             "conv_a_w", "conv_b_w"]
    order = ["meta", "norm_g", "w_in", "conv_a_w", "conv_a_b", "lru_wr", "lru_br", "lru_wi", "lru_bi", "lru_lambda",
             "conv_b_w", "w_out", "final_g"]

    def family(idx, small):
        table = {nm: small[i] for i, nm in enumerate(names)}
        table["w_in"] = win_out[idx]
        table["w_out"] = wout_out[idx]
        return [table[nm] for nm in order]

    return (loss, grad_x, *family(0, s_grad), *family(1, s_delta), *family(2, s_m), *family(3, s_v))
```

```python
import functools

import jax
import jax.numpy as jnp
from jax import lax
from jax.experimental import pallas as pl
from jax.experimental.pallas import tpu as pltpu

F32 = jnp.float32
BF16 = jnp.bfloat16
MESH = pl.DeviceIdType.MESH

N_DEV = 8
RMS_EPS = 1e-6
LRU_C = 8.0
ADAM_LR = 0.001
ADAM_B1 = 0.9
ADAM_B2 = 0.999
ADAM_EPS = 1e-08
ADAM_WD = 0.01
ADAM_STEP = 10

V7X_VMEM_LIMIT = 52 * 1024 * 1024
LANES = 128
SUBLANES = 8
TOKEN_TILE = 384
MIX_ROWS = 128
SHIFTED_ROWS = 128
GATE_BLOCK = 128


def _params(sem):
    return pltpu.CompilerParams(dimension_semantics=sem, vmem_limit_bytes=V7X_VMEM_LIMIT)


def _tile(n, target, align=LANES):
    best = None
    for t in range(align, min(n, target) + 1, align):
        if n % t == 0:
            best = t
    return n if best is None else best


def _sigmoid(z):
    return 0.5 * jnp.tanh(0.5 * z) + 0.5


def _softplus(z):
    e = jnp.exp(-jnp.abs(z))
    u = 1.0 + e
    l1p = jnp.where(u == 1.0, e, jnp.log(u) * e / jnp.where(u == 1.0, 1.0, u - 1.0))
    return jnp.maximum(z, 0.0) + l1p


def _matmul(a, b, *, ta=False, tb=False, tm, tn, tk, out_dtype=F32, add=None, dep=None, m_part=None, name):
    m, k = (a.shape[1], a.shape[0]) if ta else a.shape
    m_off = 0
    if m_part is not None:
        assert add is None and m % (m_part[1] * tm) == 0
        m //= m_part[1]
        m_off = m_part[0] * (m // tm)
    n, kb = b.shape if tb else b.shape[::-1]
    assert kb == k
    assert m % tm == 0 and n % tn == 0 and k % tk == 0, (m, n, k, tm, tn, tk)
    nk = k // tk
    a_spec = pl.BlockSpec((tk, tm), lambda i, j, q: (q, i + m_off)) if ta \
        else pl.BlockSpec((tm, tk), lambda i, j, q: (i + m_off, q))
    b_spec = pl.BlockSpec((tn, tk), lambda i, j, q: (j, q)) if tb else pl.BlockSpec((tk, tn), lambda i, j, q: (q, j))
    o_spec = pl.BlockSpec((tm, tn), lambda i, j, q: (i, j))
    o_shape = (m, n)
    dims = (((0 if ta else 1,), (1 if tb else 0,)), ((), ()))
    has_add = add is not None
    has_dep = dep is not None

    def body(*refs):
        if has_dep:
            refs = refs[:-3] + refs[-2:]
        if has_add:
            a_ref, b_ref, add_ref, o_ref, acc_ref = refs
        else:
            a_ref, b_ref, o_ref, acc_ref = refs
        q = pl.program_id(2)
        part = lax.dot_general(a_ref[...], b_ref[...], dims, preferred_element_type=F32)

        def finish(acc):
            if has_add:
                acc = acc + add_ref[...]
            o_ref[...] = acc.astype(out_dtype)

        if nk == 1:
            finish(part)
        else:
            @pl.when(q == 0)
            def _():
                acc_ref[...] = part

            @pl.when(jnp.logical_and(q > 0, q < nk - 1))
            def _():
                acc_ref[...] += part

            @pl.when(q == nk - 1)
            def _():
                finish(acc_ref[...] + part)

    in_specs = [a_spec, b_spec] + ([o_spec] if has_add else [])
    args = (a, b) + ((add,) if has_add else ())
    if has_dep:
        in_specs.append(pl.BlockSpec((SUBLANES, LANES), lambda i, j, q: (0, 0)))
        args += (dep,)
    acc_shape = (tm, tn) if nk > 1 else (SUBLANES, LANES)
    return pl.pallas_call(
        body, name=name,
        grid=(m // tm, n // tn, nk),
        in_specs=in_specs, out_specs=o_spec,
        out_shape=jax.ShapeDtypeStruct(o_shape, out_dtype),
        scratch_shapes=[pltpu.VMEM(acc_shape, F32)],
        compiler_params=_params(("parallel", "parallel", "arbitrary")),
    )(*args)


def _rms_fwd(h, g, *, name):
    tp, d = h.shape
    tr = _tile(tp, 512, SUBLANES)

    def body(h_ref, g_ref, o_ref):
        hv = h_ref[...]
        rstd = lax.rsqrt(jnp.mean(hv * hv, axis=-1, keepdims=True) + RMS_EPS)
        o_ref[...] = (hv * rstd * g_ref[...]).astype(BF16)

    return pl.pallas_call(
        body, name=name, grid=(tp // tr,),
        in_specs=[pl.BlockSpec((tr, d), lambda i: (i, 0)), pl.BlockSpec((1, d), lambda i: (0, 0))],
        out_specs=pl.BlockSpec((tr, d), lambda i: (i, 0)),
        out_shape=jax.ShapeDtypeStruct((tp, d), BF16),
        compiler_params=_params(("parallel",)),
    )(h, g.reshape(1, d))


def _rms_fwd_first(x, meta, g, *, tp, name):
    seq, d = x.shape
    n_meta = meta.shape[0]
    n_tok = n_meta + seq
    tr = SHIFTED_ROWS
    assert tp % tr == 0 and seq % tr == 0 and tr % n_meta == 0
    per = tr // n_meta

    def body(x_ref, xp_ref, m_ref, g_ref, h_ref, o_ref):
        i = pl.program_id(0)
        head = jnp.where(i == 0, m_ref[...], xp_ref[...])
        rows = i * tr + lax.broadcasted_iota(jnp.int32, (tr, 1), 0)
        hv = jnp.where(rows < n_tok, jnp.concatenate([head, x_ref[:tr - n_meta, :]], axis=0), 0.0)
        h_ref[...] = hv
        rstd = lax.rsqrt(jnp.mean(hv * hv, axis=-1, keepdims=True) + RMS_EPS)
        o_ref[...] = (hv * rstd * g_ref[...]).astype(BF16)

    row = pl.BlockSpec((tr, d), lambda i: (i, 0))
    own = pl.BlockSpec((tr, d), lambda i: (jnp.minimum(i, seq // tr - 1), 0))
    before = pl.BlockSpec((n_meta, d), lambda i: (jnp.maximum(i * per - 1, 0), 0))
    return pl.pallas_call(
        body, name=name, grid=(tp // tr,),
        in_specs=[own, before, pl.BlockSpec((n_meta, d), lambda i: (0, 0)), pl.BlockSpec((1, d), lambda i: (0, 0))],
        out_specs=[row, row],
        out_shape=[jax.ShapeDtypeStruct((tp, d), F32), jax.ShapeDtypeStruct((tp, d), BF16)],
        compiler_params=_params(("parallel",)),
    )(x, x, meta, g.reshape(1, d))


def _rms_bwd(h, dhn, dout, g, *, name):
    tp, d = h.shape
    tr = _tile(tp, 528, 2 * SUBLANES)

    def body(h_ref, dhn_ref, dout_ref, g_ref, dh_ref, dhb_ref, dg_ref):
        hv = h_ref[...]
        rstd = lax.rsqrt(jnp.mean(hv * hv, axis=-1, keepdims=True) + RMS_EPS)
        xhat = hv * rstd
        dn = dhn_ref[...]
        dxhat = dn * g_ref[...]
        dh = dout_ref[...] + rstd * (dxhat - xhat * jnp.mean(dxhat * xhat, axis=-1, keepdims=True))
        dh_ref[...] = dh
        dhb_ref[...] = dh.astype(BF16)
        part = jnp.sum(dn * xhat, axis=0, keepdims=True)

        @pl.when(pl.program_id(0) == 0)
        def _():
            dg_ref[...] = part

        @pl.when(pl.program_id(0) > 0)
        def _():
            dg_ref[...] += part

    row = pl.BlockSpec((tr, d), lambda i: (i, 0))
    vec = pl.BlockSpec((1, d), lambda i: (0, 0))
    return pl.pallas_call(
        body, name=name, grid=(tp // tr,),
        in_specs=[row, row, row, vec],
        out_specs=[row, row, vec],
        out_shape=[jax.ShapeDtypeStruct((tp, d), F32), jax.ShapeDtypeStruct((tp, d), BF16),
                   jax.ShapeDtypeStruct((1, d), F32)],
        compiler_params=_params(("arbitrary",)),
    )(h, dhn, dout, g.reshape(1, d))


def _rms_bwd_first(h, dhn, dout, g, *, n_meta, seq, name):
    tp, d = h.shape
    tr = SHIFTED_ROWS
    assert seq % tr == 0 and tr % n_meta == 0 and tp >= seq + n_meta
    nt = seq // tr
    per = tr // n_meta

    def grads(hv, dn, do, gv):
        rstd = lax.rsqrt(jnp.mean(hv * hv, axis=-1, keepdims=True) + RMS_EPS)
        xhat = hv * rstd
        dxhat = dn * gv
        dh = do + rstd * (dxhat - xhat * jnp.mean(dxhat * xhat, axis=-1, keepdims=True))
        return dh, jnp.sum(dn * xhat, axis=0, keepdims=True)

    def body(h_ref, dhn_ref, dout_ref, hn_ref, dhnn_ref, doutn_ref, g_ref, gx_ref, dmeta_ref, dg_ref):
        i = pl.program_id(0)
        gv = g_ref[...]
        dh, part = grads(h_ref[...], dhn_ref[...], dout_ref[...], gv)
        dh_next, part_next = grads(hn_ref[...], dhnn_ref[...], doutn_ref[...], gv)
        gx_ref[...] = jnp.concatenate([dh[n_meta:], dh_next], axis=0)

        @pl.when(i == 0)
        def _():
            dmeta_ref[...] = dh[:n_meta]
            dg_ref[...] = part

        @pl.when(i > 0)
        def _():
            dg_ref[...] += part

        @pl.when(i == nt - 1)
        def _():
            dg_ref[...] += part_next

    row = pl.BlockSpec((tr, d), lambda i: (i, 0))
    nxt = pl.BlockSpec((n_meta, d), lambda i: ((i + 1) * per, 0))
    vec = pl.BlockSpec((1, d), lambda i: (0, 0))
    return pl.pallas_call(
        body, name=name, grid=(nt,),
        in_specs=[row, row, row, nxt, nxt, nxt, vec],
        out_specs=[pl.BlockSpec((None, tr, d), lambda i: (0, i, 0)), pl.BlockSpec((n_meta, d), lambda i: (0, 0)), vec],
        out_shape=[jax.ShapeDtypeStruct((1, seq, d), F32), jax.ShapeDtypeStruct((n_meta, d), F32),
                   jax.ShapeDtypeStruct((1, d), F32)],
        compiler_params=_params(("arbitrary",)),
    )(h, dhn, dout, h, dhn, dout, g.reshape(1, d))


def _loss_head(h, tgt, g, *, n_meta, n_tok, name):
    tp, d = h.shape
    seq = tgt.shape[0]
    tr = SHIFTED_ROWS
    assert tp % tr == 0 and seq % tr == 0 and tr % n_meta == 0
    per = tr // n_meta

    def body(h_ref, t_ref, tp_ref, g_ref, dh_ref, dhb_ref, dg_ref, loss_ref):
        i = pl.program_id(0)
        hv = h_ref[...]
        rstd = lax.rsqrt(jnp.mean(hv * hv, axis=-1, keepdims=True) + RMS_EPS)
        xhat = hv * rstd
        gv = g_ref[...]
        rows = i * tr + lax.broadcasted_iota(jnp.int32, (tr, 1), 0)
        valid = jnp.logical_and(rows >= n_meta, rows < n_tok)
        target = jnp.concatenate([tp_ref[...], t_ref[:tr - n_meta, :]], axis=0)
        err = jnp.where(valid, xhat * gv - target, 0.0)
        dy = err * (1.0 / d)
        dxhat = dy * gv
        dh = rstd * (dxhat - xhat * jnp.mean(dxhat * xhat, axis=-1, keepdims=True))
        dh_ref[...] = dh
        dhb_ref[...] = dh.astype(BF16)
        dg_part = jnp.sum(dy * xhat, axis=0, keepdims=True)
        per_row = jnp.sum(err * err, axis=-1, keepdims=True) * (1.0 / d)
        loss_part = jnp.broadcast_to(0.5 * jnp.sum(per_row, axis=0, keepdims=True), (SUBLANES, LANES))

        @pl.when(i == 0)
        def _():
            dg_ref[...] = dg_part
            loss_ref[...] = loss_part

        @pl.when(i > 0)
        def _():
            dg_ref[...] += dg_part
            loss_ref[...] += loss_part

    row = pl.BlockSpec((tr, d), lambda i: (i, 0))
    vec = pl.BlockSpec((1, d), lambda i: (0, 0))
    own = pl.BlockSpec((tr, d), lambda i: (jnp.minimum(i, seq // tr - 1), 0))
    before = pl.BlockSpec((n_meta, d), lambda i: (jnp.maximum(i * per - 1, 0), 0))
    return pl.pallas_call(
        body, name=name, grid=(tp // tr,),
        in_specs=[row, own, before, vec],
        out_specs=[row, row, vec, pl.BlockSpec((SUBLANES, LANES), lambda i: (0, 0))],
        out_shape=[jax.ShapeDtypeStruct((tp, d), F32), jax.ShapeDtypeStruct((tp, d), BF16),
                   jax.ShapeDtypeStruct((1, d), F32), jax.ShapeDtypeStruct((SUBLANES, LANES), F32)],
        compiler_params=_params(("arbitrary",)),
    )(h, tgt, tgt, g.reshape(1, d))


def _shift_down(halo, tile, s):
    if s == 0:
        return tile
    ext = jnp.concatenate([halo, tile], axis=0)
    return pltpu.roll(ext, s, 0)[SUBLANES:]


def _shift_up(tile, head, s):
    if s == 0:
        return tile
    ext = jnp.concatenate([tile, head], axis=0)
    n = ext.shape[0]
    return pltpu.roll(ext, n - s, 0)[: tile.shape[0]]


def _to_lane_blocks(ref, cols, val):
    for j in range(cols.start // LANES, cols.stop // LANES):
        ref[j] = val[:, j * LANES - cols.start:(j + 1) * LANES - cols.start]


def _from_lane_blocks(ref, cols):
    return jnp.concatenate([ref[j] for j in range(cols.start // LANES, cols.stop // LANES)], axis=1)


def _scan_tile(a_ref, b_ref, out_ref, carry, j, *, reverse):
    ng = a_ref.shape[1] // SUBLANES
    order = list(range(SUBLANES))[::-1] if reverse else list(range(SUBLANES))

    def rows(r):
        return pl.ds(r, ng, stride=SUBLANES)

    prod, loc = {}, {}
    prev = None
    for r in order:
        ar = a_ref[j, rows(r), :]
        br = b_ref[j, rows(r), :]
        prod[r] = ar if prev is None else ar * prod[prev]
        loc[r] = br if prev is None else ar * loc[prev] + br
        prev = r
    pg, lg = prod[prev], loc[prev]
    ones = jnp.ones((SUBLANES,) + pg.shape[1:], F32)
    zeros = jnp.zeros_like(ones)
    s = 1
    while s < ng:
        p_sh = _shift_up(pg, ones, s) if reverse else _shift_down(ones, pg, s)
        l_sh = _shift_up(lg, zeros, s) if reverse else _shift_down(zeros, lg, s)
        lg = pg * l_sh + lg
        pg = pg * p_sh
        s *= 2
    leaving = pg * carry[0:1, :] + lg
    entering = _shift_up(leaving, carry, 1) if reverse else _shift_down(carry, leaving, 1)
    for r in order:
        out_ref[j, rows(r), :] = loc[r] + prod[r] * entering
    last = leaving[0:1, :] if reverse else leaving[ng - 1:ng, :]
    return jnp.broadcast_to(last, carry.shape)


def _gates(ca, wr, wi, br, bi, sp):
    cab = ca.astype(BF16)
    r = _sigmoid(jnp.dot(cab, wr, preferred_element_type=F32) + br)
    ig = _sigmoid(jnp.dot(cab, wi, preferred_element_type=F32) + bi)
    la = -LRU_C * r * sp
    a = jnp.exp(la)
    mult = jnp.sqrt(-jnp.tanh(la) * (a * a + 1.0))
    return r, ig, a, mult


U_STREAMS = 3


class _ColumnParts:
    def __init__(self, refs):
        self.refs = refs
        self.width = refs[0].shape[-1]

    def __getitem__(self, idx):
        rows, cols = idx
        k = cols.start // self.width
        assert cols.stop <= (k + 1) * self.width
        return self.refs[k][rows, cols.start - k * self.width:cols.stop - k * self.width]


def _mixer_fwd(u, wa, ba, wr_blk, br, wi_blk, bi, lam, wb, *, name):
    tp, din = u.shape
    dl = din // 6
    tt = MIX_ROWS
    cw = GATE_BLOCK
    nch = dl // cw
    assert tp % tt == 0 and dl % cw == 0

    def body(*refs):
        u_ref = _ColumnParts(refs[:U_STREAMS])
        (wa_ref, ba_ref, wr_ref, br_ref, wi_ref, bi_ref, lam_ref, wb_ref,
         s_ref, y_ref, xa_tail, v_tail, h_carry, a_s, b_s, h_s) = refs[U_STREAMS:]

        @pl.when(pl.program_id(0) == 0)
        def _():
            xa_tail[...] = jnp.zeros_like(xa_tail)
            v_tail[...] = jnp.zeros_like(v_tail)
            h_carry[...] = jnp.zeros_like(h_carry)

        for ch in range(nch):
            cs = slice(ch * cw, (ch + 1) * cw)

            def seg(s):
                return slice(s * dl + ch * cw, s * dl + (ch + 1) * cw)

            xa = u_ref[:, seg(0)]
            halo = xa_tail[:, cs]
            ca = ba_ref[:, cs] + wa_ref[3:4, cs] * xa
            for kk in range(3):
                ca = ca + wa_ref[kk:kk + 1, cs] * _shift_down(halo, xa, 3 - kk)
            xa_tail[:, cs] = xa[tt - SUBLANES:]
            s_ref[:, cs] = ca
            sp = _softplus(-lam_ref[:, cs])
            _, ig, a, mult = _gates(ca, wr_ref[ch], wi_ref[ch], br_ref[:, cs], bi_ref[:, cs], sp)
            _to_lane_blocks(a_s, cs, a)
            _to_lane_blocks(b_s, cs, mult * (ig * ca))

            bv = u_ref[:, seg(2)]
            v = u_ref[:, seg(3)] * u_ref[:, seg(4)]
            gb = u_ref[:, seg(5)]
            vh = v_tail[:, cs]
            cb = wb_ref[2:3, cs] * v
            for kk in range(2):
                cb = cb + wb_ref[kk:kk + 1, cs] * _shift_down(vh, v, 2 - kk)
            v_tail[:, cs] = v[tt - SUBLANES:]
            y_ref[:, dl + ch * cw: dl + (ch + 1) * cw] = (bv * cb * (gb * _sigmoid(gb))).astype(BF16)

        for ch in range(nch):
            cs = slice(ch * cw, (ch + 1) * cw)
            for j in range(cs.start // LANES, cs.stop // LANES):
                lanes = slice(j * LANES, (j + 1) * LANES)
                h_carry[:, lanes] = _scan_tile(a_s, b_s, h_s, h_carry[:, lanes], j, reverse=False)
            hsv = _from_lane_blocks(h_s, cs)
            s_ref[:, dl + ch * cw: dl + (ch + 1) * cw] = hsv
            ga = u_ref[:, dl + ch * cw: dl + (ch + 1) * cw]
            y_ref[:, cs] = (hsv * (ga * _sigmoid(ga))).astype(BF16)

    row = lambda w: pl.BlockSpec((tt, w), lambda i: (i, 0))
    full = lambda shp: pl.BlockSpec(shp, lambda i: tuple(0 for _ in shp))
    return pl.pallas_call(
        body, name=name, grid=(tp // tt,),
        in_specs=[pl.BlockSpec((tt, din // U_STREAMS), functools.partial(lambda k, i: (i, k), k))
                  for k in range(U_STREAMS)]
        + [full((4, dl)), full((1, dl)), full((nch, cw, cw)), full((1, dl)),
           full((nch, cw, cw)), full((1, dl)), full((1, dl)), full((3, dl))],
        out_specs=[row(2 * dl), row(2 * dl)],
        out_shape=[jax.ShapeDtypeStruct((tp, 2 * dl), F32), jax.ShapeDtypeStruct((tp, 2 * dl), BF16)],
        scratch_shapes=[pltpu.VMEM((SUBLANES, dl), F32), pltpu.VMEM((SUBLANES, dl), F32),
                        pltpu.VMEM((SUBLANES, dl), F32)] + [pltpu.VMEM((dl // LANES, tt, LANES), F32)] * 3,
        compiler_params=_params(("arbitrary",)),
    )(*[u] * U_STREAMS, wa, ba, wr_blk, br, wi_blk, bi, lam, wb)


SG_WA, SG_BA, SG_BR, SG_BI, SG_LAM, SG_WB, SG_ROWS = 0, 4, 5, 6, 7, 8, 16


def _mixer_bwd(u, saved, dy, wa, wr_blk, br, wi_blk, bi, lam, wb, *, name):
    tp, din = u.shape
    dl = din // 6
    tt = MIX_ROWS
    cw = GATE_BLOCK
    nch = dl // cw
    nt = tp // tt
    hb = tt // SUBLANES
    tn_dims = (((0,), (0,)), ((), ()))
    nt_dims = (((1,), (1,)), ((), ()))

    def body(*refs):
        u_ref = _ColumnParts(refs[:U_STREAMS])
        (uh_ref, s_ref, sh_ref, dy_ref, wa_ref, wr_ref, br_ref, wi_ref, bi_ref, lam_ref, wb_ref,
         du_ref, sg_ref, dwr_ref, dwi_ref,
         g_carry, a_head, dca_head, dcb_head, r_s, i_s, a_s, an_s, d_s, g_s) = refs[U_STREAMS:]
        i = pl.program_id(0)
        first_tile = i == nt - 1

        @pl.when(i == 0)
        def _():
            for ref in (g_carry, a_head, dca_head, dcb_head, sg_ref, dwr_ref, dwi_ref):
                ref[...] = jnp.zeros_like(ref)

        def halo_of(x):
            return jnp.where(first_tile, 0.0, x)

        for ch in range(nch):
            cs = slice(ch * cw, (ch + 1) * cw)
            cav = s_ref[:, cs]
            sp = _softplus(-lam_ref[:, cs])
            r, ig, a, _ = _gates(cav, wr_ref[ch], wi_ref[ch], br_ref[:, cs], bi_ref[:, cs], sp)
            r_s[:, cs] = r
            i_s[:, cs] = ig
            a_s[:, cs] = a
            _to_lane_blocks(an_s, cs, _shift_up(a, a_head[:, cs], 1))
            a_head[:, cs] = a[:SUBLANES]
            ga = u_ref[:, dl + ch * cw: dl + (ch + 1) * cw]
            _to_lane_blocks(d_s, cs, dy_ref[:, cs] * (ga * _sigmoid(ga)))

        for j in range(dl // LANES):
            lanes = slice(j * LANES, (j + 1) * LANES)
            g_carry[:, lanes] = _scan_tile(an_s, d_s, g_s, g_carry[:, lanes], j, reverse=True)

        def acc_row(r0, val):
            sg_ref[r0:r0 + 1, cs_cur[0]] += jnp.sum(val, axis=0, keepdims=True)

        cs_cur = [None]
        for ch in range(nch):
            cs = slice(ch * cw, (ch + 1) * cw)
            cs_cur[0] = cs

            def seg(s):
                return slice(s * dl + ch * cw, s * dl + (ch + 1) * cw)

            cav = s_ref[:, cs]
            r = r_s[:, cs]
            ig = i_s[:, cs]
            a = a_s[:, cs]
            g = _from_lane_blocks(g_s, cs)
            hsv = s_ref[:, dl + ch * cw: dl + (ch + 1) * cw]
            lamv = lam_ref[:, cs]
            sp = _softplus(-lamv)
            la = -LRU_C * r * sp
            e2 = a * a
            one_m_e2 = -jnp.tanh(la) * (e2 + 1.0)
            mult = jnp.sqrt(one_m_e2)
            hprev = _shift_down(halo_of(sh_ref[:, dl + ch * cw: dl + (ch + 1) * cw]), hsv, 1)
            icav = ig * cav
            dla = g * (hprev * a - icav * (e2 * lax.rsqrt(one_m_e2)))
            gm = g * mult
            dzi = gm * icav * (1.0 - ig)
            dca = gm * ig
            dla_r = dla * r
            dzr = dla_r * (1.0 - r) * (-LRU_C * sp)
            sg_ref[SG_LAM:SG_LAM + 1, cs] += jnp.sum(dla_r, axis=0, keepdims=True) * (LRU_C * _sigmoid(-lamv))
            acc_row(SG_BR, dzr)
            acc_row(SG_BI, dzi)
            dzr_b = dzr.astype(BF16)
            dzi_b = dzi.astype(BF16)
            cab = cav.astype(BF16)
            dca = dca + lax.dot_general(dzr_b, wr_ref[ch], nt_dims, preferred_element_type=F32)
            dca = dca + lax.dot_general(dzi_b, wi_ref[ch], nt_dims, preferred_element_type=F32)
            dwr_ref[ch] += lax.dot_general(cab, dzr_b, tn_dims, preferred_element_type=F32)
            dwi_ref[ch] += lax.dot_general(cab, dzi_b, tn_dims, preferred_element_type=F32)
            acc_row(SG_BA, dca)
            xa = u_ref[:, seg(0)]
            head = dca_head[:, cs]
            dxa = wa_ref[3:4, cs] * dca
            acc_row(SG_WA + 3, dca * xa)
            for kk in range(3):
                later = _shift_up(dca, head, 3 - kk)
                acc_row(SG_WA + kk, later * xa)
                dxa = dxa + wa_ref[kk:kk + 1, cs] * later
            dca_head[:, cs] = dca[:SUBLANES]
            ga = u_ref[:, seg(1)]
            sga = _sigmoid(ga)
            dga = dy_ref[:, cs] * hsv * (sga + (ga * sga) * (1.0 - sga))
            du_ref[:, seg(0)] = dxa.astype(BF16)
            du_ref[:, seg(1)] = dga.astype(BF16)

            bv = u_ref[:, seg(2)]
            cv = u_ref[:, seg(3)]
            xb = u_ref[:, seg(4)]
            gb = u_ref[:, seg(5)]
            dyb = dy_ref[:, dl + ch * cw: dl + (ch + 1) * cw]
            v = cv * xb
            vh = halo_of(uh_ref[:, seg(3)] * uh_ref[:, seg(4)])
            v1 = _shift_down(vh, v, 1)
            v2 = _shift_down(vh, v, 2)
            cb = wb_ref[2:3, cs] * v + wb_ref[1:2, cs] * v1 + wb_ref[0:1, cs] * v2
            sgb = _sigmoid(gb)
            sl = gb * sgb
            dyb_b = dyb * bv
            dyb_cb = dyb * cb
            dcb = dyb_b * sl
            du_ref[:, seg(2)] = (dyb_cb * sl).astype(BF16)
            du_ref[:, seg(5)] = (dyb_cb * bv * (sgb + sl * (1.0 - sgb))).astype(BF16)
            bhead = dcb_head[:, cs]
            dv = wb_ref[2:3, cs] * dcb
            acc_row(SG_WB + 2, dcb * v)
            for kk in range(2):
                later = _shift_up(dcb, bhead, 2 - kk)
                acc_row(SG_WB + kk, later * v)
                dv = dv + wb_ref[kk:kk + 1, cs] * later
            dcb_head[:, cs] = dcb[:SUBLANES]
            du_ref[:, seg(3)] = (dv * xb).astype(BF16)
            du_ref[:, seg(4)] = (dv * cv).astype(BF16)

    rev = lambda w: pl.BlockSpec((tt, w), lambda i: (nt - 1 - i, 0))
    halo = lambda w: pl.BlockSpec((SUBLANES, w), lambda i: (jnp.maximum((nt - 1 - i) * hb - 1, 0), 0))
    full = lambda shp: pl.BlockSpec(shp, lambda i: tuple(0 for _ in shp))
    vm = lambda r: pltpu.VMEM((r, dl), F32)
    return pl.pallas_call(
        body, name=name, grid=(nt,),
        in_specs=[pl.BlockSpec((tt, din // U_STREAMS), functools.partial(lambda k, i: (nt - 1 - i, k), k))
                  for k in range(U_STREAMS)]
        + [halo(din), rev(2 * dl), halo(2 * dl), rev(2 * dl), full((4, dl)),
           full((nch, cw, cw)), full((1, dl)), full((nch, cw, cw)), full((1, dl)), full((1, dl)), full((3, dl))],
        out_specs=[rev(din), full((SG_ROWS, dl)), full((nch, cw, cw)), full((nch, cw, cw))],
        out_shape=[jax.ShapeDtypeStruct((tp, din), BF16), jax.ShapeDtypeStruct((SG_ROWS, dl), F32),
                   jax.ShapeDtypeStruct((nch, cw, cw), F32), jax.ShapeDtypeStruct((nch, cw, cw), F32)],
        scratch_shapes=[vm(SUBLANES), vm(SUBLANES), vm(SUBLANES), vm(SUBLANES), vm(tt), vm(tt), vm(tt)]
        + [pltpu.VMEM((dl // LANES, tt, LANES), F32)] * 3,
        compiler_params=_params(("arbitrary",)),
    )(*[u] * (U_STREAMS + 1), saved, saved, dy, wa, wr_blk, br, wi_blk, bi, lam, wb)


def _adamw(w, g, m, v, *, name, landed=None, layer=None, depth=None, into=None, row_off=0):
    r, c = w.shape[-2:]
    rows = g.shape[0]
    tr = _tile(rows, 512, 2 * SUBLANES)
    assert row_off % tr == 0
    boff = row_off // tr
    bc1 = 1.0 - ADAM_B1 ** ADAM_STEP
    bc2 = 1.0 - ADAM_B2 ** ADAM_STEP
    slots = landed is not None

    def body(*refs):
        if into is not None:
            refs = refs[:-8] + refs[-4:]
        if slots:
            w_ref, g_ref, l_ref, m_ref, v_ref, grad_ref, delta_ref, nm_ref, nv_ref = refs
            gv = g_ref[...].astype(F32)
            for s in range(N_DEV - 1):
                gv = gv + l_ref[s].astype(F32)
        else:
            w_ref, g_ref, m_ref, v_ref, grad_ref, delta_ref, nm_ref, nv_ref = refs
            gv = g_ref[...]
        wv = w_ref[...]
        mn = ADAM_B1 * m_ref[...] + (1.0 - ADAM_B1) * gv
        vn = ADAM_B2 * v_ref[...] + (1.0 - ADAM_B2) * (gv * gv)
        m_hat = mn / bc1
        v_hat = vn / bc2
        grad_ref[...] = gv
        delta_ref[...] = -ADAM_LR * (m_hat / (jnp.sqrt(v_hat) + ADAM_EPS) + ADAM_WD * wv)
        nm_ref[...] = mn
        nv_ref[...] = vn

    if depth is None:
        blk = pl.BlockSpec((tr, c), lambda i: (i + boff, 0))
    else:
        blk = pl.BlockSpec((None, tr, c), lambda i: (layer, i + boff, 0))
    g_blk = pl.BlockSpec((tr, c), lambda i: (i, 0))
    l_spec = [pl.BlockSpec((N_DEV - 1, tr, c), lambda i: (0, i, 0))] if slots else []
    args = (w, g, landed, m, v) if slots else (w, g, m, v)
    in_specs = [blk, g_blk] + l_spec + [blk, blk]
    if depth is None:
        shp = jax.ShapeDtypeStruct((r, c), F32)
        out_blk = blk
    else:
        shp = jax.ShapeDtypeStruct((depth, r, c), F32)
        out_blk = pl.BlockSpec((None, tr, c), lambda i: (layer, i + boff, 0))
    aliases = {}
    if into is not None:
        aliases = {len(args) + j: j for j in range(4)}
        in_specs = in_specs + [ANY] * 4
        args = args + tuple(into)
    return pl.pallas_call(
        body, name=name, grid=(rows // tr,),
        in_specs=in_specs, out_specs=[out_blk] * 4,
        out_shape=[shp] * 4, input_output_aliases=aliases,
        compiler_params=_params(("parallel",)),
    )(*args)


def _slot_sum(g, *, name):
    _, r, c = g.shape
    tr = _tile(r, 512, SUBLANES)

    def body(g_ref, o_ref):
        gv = g_ref[0].astype(F32)
        for s in range(1, N_DEV):
            gv = gv + g_ref[s].astype(F32)
        o_ref[...] = gv

    return pl.pallas_call(
        body, name=name, grid=(r // tr,),
        in_specs=[pl.BlockSpec((N_DEV, tr, c), lambda i: (0, i, 0))],
        out_specs=pl.BlockSpec((tr, c), lambda i: (i, 0)),
        out_shape=jax.ShapeDtypeStruct((r, c), F32),
        compiler_params=_params(("parallel",)),
    )(g)


def _mesh_pos():
    x, y, c = lax.axis_index("x"), lax.axis_index("y"), lax.axis_index("c")
    return x, y, c, 4 * x + 2 * y + c


ANY = pl.BlockSpec(memory_space=pl.ANY)


GATHER_COPIES = 9


def _all_gather(srcs, out_shapes, views, *, name):
    n = len(srcs)
    SIB, X_OWN, Y_OWN, X_DIAG, Y_DIAG, SIB_X, SIB_Y, SIB_DIAG_TOP, SIB_DIAG_BOTTOM = range(GATHER_COPIES)

    def body(*refs):
        src = refs[:n]
        dst = refs[n:2 * n]
        send_sems, recv_sems, local_sems = refs[2 * n:]
        x, y, c, me = _mesh_pos()
        sibling, x_nbr, y_nbr = (x, y, 1 - c), (1 - x, y, c), (x, 1 - y, c)

        def block(a, px, py, pc, half=None):
            win = views[a](dst[a], 4 * px + 2 * py + pc)
            if half is None:
                return win
            rows = win.shape[0] // 2
            return win.at[pl.ds(half * rows, rows)]

        def copy(a, k, win, to, from_src=False):
            return pltpu.make_async_remote_copy(
                src_ref=src[a] if from_src else win, dst_ref=win,
                send_sem=send_sems.at[a * GATHER_COPIES + k], recv_sem=recv_sems.at[a * GATHER_COPIES + k],
                device_id=to, device_id_type=MESH)

        mine = [pltpu.make_async_copy(src[a], block(a, x, y, c), local_sems.at[a]) for a in range(n)]
        started = []

        def start(cp):
            cp.start()
            started.append(cp)

        for a in range(n):
            mine[a].start()
            own = block(a, x, y, c)
            start(copy(a, SIB, own, sibling, True))
            start(copy(a, X_OWN, own, x_nbr, True))
            start(copy(a, Y_OWN, own, y_nbr, True))
        for a in range(n):
            from_y = block(a, x, 1 - y, c)
            copy(a, Y_OWN, from_y, y_nbr).wait_recv()
            start(copy(a, X_DIAG, block(a, x, 1 - y, c, 0), x_nbr))
            start(copy(a, SIB_Y, from_y, sibling))
            from_x = block(a, 1 - x, y, c)
            copy(a, X_OWN, from_x, x_nbr).wait_recv()
            start(copy(a, Y_DIAG, block(a, 1 - x, y, c, 1), y_nbr))
            start(copy(a, SIB_X, from_x, sibling))
        for a in range(n):
            top = block(a, 1 - x, 1 - y, c, 0)
            copy(a, X_DIAG, top, x_nbr).wait_recv()
            start(copy(a, SIB_DIAG_TOP, top, sibling))
            bottom = block(a, 1 - x, 1 - y, c, 1)
            copy(a, Y_DIAG, bottom, y_nbr).wait_recv()
            start(copy(a, SIB_DIAG_BOTTOM, bottom, sibling))
        for a in range(n):
            copy(a, SIB, block(a, x, y, 1 - c), sibling).wait_recv()
            copy(a, SIB_X, block(a, 1 - x, y, 1 - c), sibling).wait_recv()
            copy(a, SIB_Y, block(a, x, 1 - y, 1 - c), sibling).wait_recv()
            copy(a, SIB_DIAG_TOP, block(a, 1 - x, 1 - y, 1 - c, 0), sibling).wait_recv()
            copy(a, SIB_DIAG_BOTTOM, block(a, 1 - x, 1 - y, 1 - c, 1), sibling).wait_recv()
        for cp in started:
            cp.wait_send()
        for cp in mine:
            cp.wait()

    return pl.pallas_call(
        body, name=name,
        in_specs=[ANY] * n, out_specs=[ANY] * n,
        out_shape=[jax.ShapeDtypeStruct(s, x.dtype) for s, x in zip(out_shapes, srcs)],
        scratch_shapes=[pltpu.SemaphoreType.DMA((GATHER_COPIES * n,)), pltpu.SemaphoreType.DMA((GATHER_COPIES * n,)),
                        pltpu.SemaphoreType.DMA((n,))],
    )(*srcs)


HBM = pl.BlockSpec(memory_space=pltpu.HBM)
SEM = pl.BlockSpec(memory_space=pltpu.SEMAPHORE)
EFFECT = pltpu.SideEffectType.DATAFLOW_SIDE_EFFECTING


def _peer_of(x, y, c, k):
    return (1 - x if k & 4 else x, 1 - y if k & 2 else y, 1 - c if k & 1 else c)


def _peer_copies(n, wins, src, land, send_sems, recv_sems):
    x, y, c, me = _mesh_pos()
    out = []
    for a in range(n):
        for k in range(1, N_DEV):
            px, py, pc = _peer_of(x, y, c, k)
            s_win, d_win = wins[a](src[a], land[a], me, 4 * px + 2 * py + pc, k)
            out.append(pltpu.make_async_remote_copy(
                src_ref=s_win, dst_ref=d_win,
                send_sem=send_sems.at[a * 7 + k - 1], recv_sem=recv_sems.at[a * 7 + k - 1],
                device_id=(px, py, pc), device_id_type=MESH))
    return out


def _push_start(srcs, lands, wins, *, name):
    n = len(srcs)

    def body(*refs):
        src = refs[:n]
        land = refs[n:2 * n]
        send_sems, recv_sems = refs[2 * n], refs[2 * n + 1]
        token = refs[-1]
        for cp in _peer_copies(n, wins, src, land, send_sems, recv_sems):
            cp.start()
        token[...] = jnp.zeros_like(token)

    bufs = (*srcs, *lands)
    return pl.pallas_call(
        body, name=name,
        out_shape=(pltpu.SemaphoreType.DMA((7 * n,)), pltpu.SemaphoreType.DMA((7 * n,)),
                   *[pltpu.HBM(v.shape, v.dtype) for v in bufs], jax.ShapeDtypeStruct((SUBLANES, LANES), F32)),
        in_specs=[HBM] * (2 * n),
        out_specs=(SEM, SEM, *[HBM] * (2 * n), pl.BlockSpec(memory_space=pltpu.VMEM)),
        input_output_aliases={i: 2 + i for i in range(2 * n)},
        compiler_params=pltpu.CompilerParams(has_side_effects=EFFECT),
    )(*[pltpu.with_memory_space_constraint(v, pltpu.HBM) for v in bufs])


def _push_wait(handle, wins, after, *, name):
    send_sems, recv_sems, *bufs, _ = handle
    n = len(bufs) // 2

    def body(*refs):
        src = refs[:n]
        land = refs[n:2 * n]
        for cp in _peer_copies(n, wins, src, land, refs[2 * n], refs[2 * n + 1]):
            cp.wait_send()
            cp.wait_recv()

    outs = pl.pallas_call(
        body, name=name,
        out_shape=tuple(pltpu.HBM(v.shape, v.dtype) for v in bufs),
        in_specs=[HBM] * (2 * n) + [SEM, SEM, ANY],
        out_specs=tuple([HBM] * (2 * n)),
        input_output_aliases={i: i for i in range(2 * n)},
        compiler_params=pltpu.CompilerParams(has_side_effects=EFFECT),
    )(*bufs, send_sems, recv_sems, after)
    return outs[:n], outs[n:]


def _gather_lead(src, land, me, peer, k):
    return src, land.at[me]


def _gather_cols(width):
    def win(src, land, me, peer, k):
        return src, land.at[:, pl.ds(me * width, width)]
    return win


def _scatter_lead(src, land, me, peer, k):
    return src.at[peer], land.at[k - 1]


def _scatter_cols(width):
    def win(src, land, me, peer, k):
        return src.at[:, pl.ds(peer * width, width)], land.at[k - 1]
    return win


def _place_block(own, *, cols, name):
    rows, width = own.shape
    tr = _tile(rows, 512, 2 * SUBLANES)
    _, _, _, me = _mesh_pos()

    def body(me_ref, x_ref, o_ref):
        o_ref[...] = x_ref[...]

    if cols:
        out_spec = pl.BlockSpec((tr, width), lambda i, me_ref: (i, me_ref[0]))
        shape = (rows, N_DEV * width)
    else:
        out_spec = pl.BlockSpec((None, tr, width), lambda i, me_ref: (me_ref[0], i, 0))
        shape = (N_DEV, rows, width)
    return pl.pallas_call(
        body, name=name,
        grid_spec=pltpu.PrefetchScalarGridSpec(
            num_scalar_prefetch=1, grid=(rows // tr,),
            in_specs=[pl.BlockSpec((tr, width), lambda i, me_ref: (i, 0))], out_specs=out_spec),
        out_shape=jax.ShapeDtypeStruct(shape, own.dtype),
        compiler_params=_params(("arbitrary",)),
    )(me.astype(jnp.int32).reshape(1), own)


def _dep(x, token):
    return x + token[0, 0].astype(x.dtype)


def _lead(ref, d):
    return ref.at[d]


def _col_window(width):
    def view(ref, d):
        return ref.at[:, pl.ds(d * width, width)]
    return view


def _pack(arrs):
    flat = jnp.concatenate([a.reshape(-1).astype(F32) for a in arrs])
    n = flat.shape[0]
    rows = -(-n // (2 * SUBLANES * LANES)) * 2 * SUBLANES
    return jnp.pad(flat, (0, rows * LANES - n)).reshape(rows, LANES)


def _unpack(buf, shapes):
    flat = buf.reshape(-1)
    out, off = [], 0
    for s in shapes:
        n = 1
        for q in s:
            n *= q
        out.append(flat[off:off + n].reshape(s))
        off += n
    return out


def _blockdiag(w, cw):
    h, hd, _ = w.shape
    per = cw // hd
    wg = w.reshape(h // per, per, hd, hd)
    eye = jnp.eye(per, dtype=w.dtype)
    blk = jnp.einsum("gpij,pq->gpiqj", wg, eye)
    return blk.reshape(h // per, cw, cw).astype(BF16)


def _blockdiag_extract(g, hd):
    n, cw, _ = g.shape
    per = cw // hd
    g5 = g.reshape(n, per, hd, per, hd)
    idx = jnp.arange(per)
    return g5[:, idx, :, idx, :].transpose(1, 0, 2, 3).reshape(n * per, hd, hd)


def kernel(x, meta, norm_g, w_in, conv_a_w, conv_a_b, lru_wr, lru_br, lru_wi, lru_bi, lru_lambda, conv_b_w, w_out, final_g, loss_target, m_meta, m_norm_g, m_w_in, m_conv_a_w, m_conv_a_b, m_lru_wr, m_lru_br, m_lru_wi, m_lru_bi, m_lru_lambda, m_conv_b_w, m_w_out, m_final_g, v_meta, v_norm_g, v_w_in, v_conv_a_w, v_conv_a_b, v_lru_wr, v_lru_br, v_lru_wi, v_lru_bi, v_lru_lambda, v_conv_b_w, v_w_out, v_final_g):
    _, seq, d = x.shape
    n_meta = meta.shape[0]
    depth = w_in.shape[0]
    din = w_in.shape[2] * N_DEV
    dl = din // 6
    dmix = 2 * dl
    wcol = w_in.shape[2]
    wrow = w_out.shape[1]
    mcol = meta.shape[1]
    ccol = conv_a_w.shape[2]
    heads, hd = lru_wr.shape[1], lru_wr.shape[2]
    n_tok = n_meta + seq
    tp = -(-n_tok // TOKEN_TILE) * TOKEN_TILE
    me = 4 * lax.axis_index("x") + 2 * lax.axis_index("y") + lax.axis_index("c")

    bf = lambda a: a.astype(BF16)
    small_mine = _pack([meta, conv_a_w, conv_b_w])
    first = _all_gather([bf(w_in[0]), small_mine], [(d, din), (N_DEV,) + small_mine.shape],
                        [_col_window(wcol), _lead], name="gather_first")
    flat = first[1].reshape(N_DEV, -1)
    sizes = [meta.size, conv_a_w.size, conv_b_w.size]
    meta_full = jnp.moveaxis(flat[:, :sizes[0]].reshape(N_DEV, n_meta, mcol), 0, 1).reshape(n_meta, d)
    wa_full = jnp.moveaxis(flat[:, sizes[0]:sizes[0] + sizes[1]].reshape(N_DEV, depth, 4, ccol), 0, 2) \
        .reshape(depth, 4, dl)
    wb_full = jnp.moveaxis(flat[:, sizes[0] + sizes[1]:sum(sizes)].reshape(N_DEV, depth, 3, ccol), 0, 2) \
        .reshape(depth, 3, dl)
    w_in_full = [None] * depth
    w_out_full = [None] * depth

    push_out = [None] * depth
    push_in = [None] * depth
    w_in_full[0], src = lax.optimization_barrier((first[0], bf(w_out[0])))
    push_out[0] = _push_start([src], [_place_block(src, cols=False, name="place_wout_0")], [_gather_lead],
                              name="gather_wout_0_start")
    token = push_out[0][-1]
    for l in range(1, depth):
        src = bf(_dep(w_in[l], token))
        push_in[l] = _push_start([src], [_place_block(src, cols=True, name=f"place_win_{l}")], [_gather_cols(wcol)],
                                 name=f"gather_win_{l}_start")
        src = bf(_dep(w_out[l], push_in[l][-1]))
        push_out[l] = _push_start([src], [_place_block(src, cols=False, name=f"place_wout_{l}")], [_gather_lead],
                                  name=f"gather_wout_{l}_start")
        token = push_out[l][-1]

    wr_blk = [_blockdiag(lru_wr[l], GATE_BLOCK) for l in range(depth)]
    wi_blk = [_blockdiag(lru_wi[l], GATE_BLOCK) for l in range(depth)]
    vec = lambda a: a.reshape(1, dl)

    tm = _tile(tp, 1408)
    saved = []
    for l in range(depth):
        if l == 0:
            h, hn = _rms_fwd_first(x[0], meta_full, _dep(norm_g[l], token), tp=tp, name=f"rms_fwd_{l}")
        else:
            hn = _rms_fwd(h, norm_g[l], name=f"rms_fwd_{l}")
        if l > 0:
            _, landed = _push_wait(push_in[l], [_gather_cols(wcol)], hn, name=f"gather_win_{l}_wait")
            w_in_full[l] = landed[0]
        u = _matmul(hn, w_in_full[l], tm=tm, tn=_tile(din, 1536), tk=d, name=f"mm_u_{l}")
        mixed, y = _mixer_fwd(u, wa_full[l], vec(conv_a_b[l]), wr_blk[l], vec(lru_br[l]), wi_blk[l], vec(lru_bi[l]),
                              vec(lru_lambda[l]), wb_full[l], name=f"mixer_fwd_{l}")
        _, landed = _push_wait(push_out[l], [_gather_lead], y, name=f"gather_wout_{l}_wait")
        w_out_full[l] = landed[0].reshape(dmix, d)
        h_next = _matmul(y, w_out_full[l], tm=tm, tn=_tile(d, 1024), tk=dmix, add=h, name=f"mm_out_{l}")
        saved.append((h, hn, u, mixed, y))
        h = h_next

    dh, dhb, dg_final, loss_part = _loss_head(h, loss_target[0], final_g, n_meta=n_meta, n_tok=n_tok,
                                              name="loss_head")

    small_grads = [None] * depth
    sent_out = [None] * depth
    sent_in = [None] * depth
    scatter_in = [_scatter_cols(wcol)]
    token = None
    dg_norms = []
    for l in reversed(range(depth)):
        h_in, hn, u, mixed, y = saved[l]
        dy = _matmul(dhb, w_out_full[l], tb=True, tm=tm, tn=_tile(dmix, 1024), tk=d, dep=token, name=f"mm_dy_{l}")
        dw_out = _matmul(y, dhb, ta=True, tm=_tile(dmix, 1024), tn=_tile(d, 1024), tk=tp, out_dtype=BF16,
                         name=f"mm_dwout_{l}")
        sent_out[l] = _push_start([dw_out.reshape(N_DEV, wrow, d)], [lax.empty((N_DEV - 1, wrow, d), BF16)],
                                  [_scatter_lead], name=f"scatter_wout_{l}_start")
        du, sg, dwr, dwi = _mixer_bwd(u, mixed, dy, wa_full[l], wr_blk[l], vec(lru_br[l]), wi_blk[l], vec(lru_bi[l]),
                                      vec(lru_lambda[l]), _dep(wb_full[l], sent_out[l][-1]), name=f"mixer_bwd_{l}")
        small_grads[l] = (sg, dwr, dwi)
        if l == 0:
            rows = jnp.stack([small_grads[j][0] for j in range(depth)])
            early = [_pack([
                rows[:, SG_BA], rows[:, SG_BR], rows[:, SG_BI], rows[:, SG_LAM], rows[:, SG_WA:SG_WA + 4],
                rows[:, SG_WB:SG_WB + 3], dg_final[0], *dg_norms]),
                _pack([jnp.stack([_blockdiag_extract(small_grads[j][1], hd) for j in range(depth)]),
                       jnp.stack([_blockdiag_extract(small_grads[j][2], hd) for j in range(depth)])]).astype(BF16)]
            early_land = [lax.dynamic_update_slice(lax.empty((N_DEV,) + a.shape, a.dtype), a[None], (me, 0, 0))
                          for a in early]
            sent_early = _push_start(early, early_land, [_gather_lead] * 2, name="gather_early_grads_start")
        parts = 2 if l == 0 else 1
        token = sent_early[-1] if l == 0 else None
        sent_in[l] = []
        for p in range(parts):
            dw_in = _matmul(hn, du, ta=True, tm=_tile(d // parts, 512), tn=_tile(din, 1536), tk=tp, out_dtype=BF16,
                            dep=token, m_part=(p, parts), name=f"mm_dwin_{l}_{p}")
            sent_in[l].append(_push_start([dw_in], [lax.empty((N_DEV - 1, d // parts, wcol), BF16)], scatter_in,
                                          name=f"scatter_win_{l}_{p}_start"))
            token = sent_in[l][-1][-1]
        dhn = _matmul(du, w_in_full[l], tb=True, tm=_tile(tp, 704, 2 * SUBLANES), tn=_tile(d, 512), tk=din, dep=token,
                      name=f"mm_dhn_{l}")
        if l > 0:
            dh, dhb, dg_norm = _rms_bwd(h_in, dhn, dh, norm_g[l], name=f"rms_bwd_{l}")
            dg_norms.append(dg_norm[0])
        else:
            grad_x, d_meta, dg_norm = _rms_bwd_first(h_in, dhn, dh, norm_g[l], n_meta=n_meta, seq=seq,
                                                     name=f"rms_bwd_{l}")

    late = _pack([dg_norm[0], d_meta, loss_part[0:1, 0:1]])
    late_all = _all_gather([late], [(N_DEV,) + late.shape], [_lead], name="gather_late_grads")[0]
    late_sum = _unpack(_slot_sum(late_all, name="sum_late_grads"), [(d,), (n_meta, d), ()])
    loss = late_sum[2]
    _, early_all = _push_wait(sent_early, [_gather_lead] * 2, late_sum[0], name="gather_early_grads_wait")
    vec_shapes = [conv_a_b.shape, lru_br.shape, lru_bi.shape, lru_lambda.shape, (depth, 4, dl), (depth, 3, dl),
                  final_g.shape] + [(d,)] * (depth - 1)
    e = _unpack(_slot_sum(early_all[0], name="sum_early_vectors"), vec_shapes)
    g_wr, g_wi = _unpack(_slot_sum(early_all[1], name="sum_early_maps"), [lru_wr.shape, lru_wi.shape])
    g_norm = jnp.stack([late_sum[0]] + e[7:][::-1])
    g_meta = lax.dynamic_slice_in_dim(late_sum[1], me * mcol, mcol, axis=1)
    g_wa = lax.dynamic_slice_in_dim(e[4], me * ccol, ccol, axis=2)
    g_wb = lax.dynamic_slice_in_dim(e[5], me * ccol, ccol, axis=2)

    small_w = [norm_g, conv_a_b, lru_wr, lru_br, lru_wi, lru_bi, lru_lambda, final_g, meta, conv_a_w, conv_b_w]
    small_m = [m_norm_g, m_conv_a_b, m_lru_wr, m_lru_br, m_lru_wi, m_lru_bi, m_lru_lambda, m_final_g, m_meta,
               m_conv_a_w, m_conv_b_w]
    small_v = [v_norm_g, v_conv_a_b, v_lru_wr, v_lru_br, v_lru_wi, v_lru_bi, v_lru_lambda, v_final_g, v_meta,
               v_conv_a_w, v_conv_b_w]
    small_g = [g_norm, e[0], g_wr, e[1], g_wi, e[2], e[3], e[6], g_meta, g_wa, g_wb]
    small_out = _adamw(_pack(small_w), _pack(small_g), _pack(small_m), _pack(small_v), name="adamw_small")
    small_shapes = [a.shape for a in small_w]
    s_grad, s_delta, s_m, s_v = [_unpack(o, small_shapes) for o in small_out]

    win_out = None
    wout_out = None
    after = small_out[0]
    for l in reversed(range(depth)):
        src, landed = _push_wait(sent_out[l], [_scatter_lead], after, name=f"scatter_wout_{l}_wait")
        own = lax.dynamic_index_in_dim(src[0], me, 0, keepdims=False)
        wout_out = _adamw(w_out, own, m_w_out, v_w_out, landed=landed[0], layer=l, depth=depth,
                          into=wout_out, name=f"adamw_w_out_{l}")
        after = wout_out[0]
        for p, sent in enumerate(sent_in[l]):
            src, landed = _push_wait(sent, scatter_in, after, name=f"scatter_win_{l}_{p}_wait")
            own = lax.dynamic_slice_in_dim(src[0], me * wcol, wcol, axis=1)
            win_out = _adamw(w_in, own, m_w_in, v_w_in, landed=landed[0], layer=l, depth=depth,
                             into=win_out, row_off=p * own.shape[0], name=f"adamw_w_in_{l}_{p}")
            after = win_out[0]

    names = ["norm_g", "conv_a_b", "lru_wr", "lru_br", "lru_wi", "lru_bi", "lru_lambda", "final_g", "meta",
             "conv_a_w", "conv_b_w"]
    order = ["meta", "norm_g", "w_in", "conv_a_w", "conv_a_b", "lru_wr", "lru_br", "lru_wi", "lru_bi", "lru_lambda",
             "conv_b_w", "w_out", "final_g"]

    def family(idx, small):
        table = {nm: small[i] for i, nm in enumerate(names)}
        table["w_in"] = win_out[idx]
        table["w_out"] = wout_out[idx]
        return [table[nm] for nm in order]

    return (loss, grad_x, *family(0, s_grad), *family(1, s_delta), *family(2, s_m), *family(3, s_v))
```

```python
import functools

import jax
import jax.numpy as jnp
from jax import lax
from jax.experimental import pallas as pl
from jax.experimental.pallas import tpu as pltpu

F32 = jnp.float32
BF16 = jnp.bfloat16
MESH = pl.DeviceIdType.MESH

N_DEV = 8
RMS_EPS = 1e-6
LRU_C = 8.0
ADAM_LR = 0.001
ADAM_B1 = 0.9
ADAM_B2 = 0.999
ADAM_EPS = 1e-08
ADAM_WD = 0.01
ADAM_STEP = 10

V7X_VMEM_LIMIT = 52 * 1024 * 1024
LANES = 128
SUBLANES = 8
TOKEN_TILE = 384
MIX_ROWS = 128
SHIFTED_ROWS = 128
GATE_BLOCK = 128


def _params(sem):
    return pltpu.CompilerParams(dimension_semantics=sem, vmem_limit_bytes=V7X_VMEM_LIMIT)


def _tile(n, target, align=LANES):
    best = None
    for t in range(align, min(n, target) + 1, align):
        if n % t == 0:
            best = t
    return n if best is None else best


def _sigmoid(z):
    return 0.5 * jnp.tanh(0.5 * z) + 0.5


def _softplus(z):
    e = jnp.exp(-jnp.abs(z))
    u = 1.0 + e
    l1p = jnp.where(u == 1.0, e, jnp.log(u) * e / jnp.where(u == 1.0, 1.0, u - 1.0))
    return jnp.maximum(z, 0.0) + l1p


def _matmul(a, b, *, ta=False, tb=False, tm, tn, tk, out_dtype=F32, add=None, dep=None, m_part=None, name):
    m, k = (a.shape[1], a.shape[0]) if ta else a.shape
    m_off = 0
    if m_part is not None:
        assert add is None and m % (m_part[1] * tm) == 0
        m //= m_part[1]
        m_off = m_part[0] * (m // tm)
    n, kb = b.shape if tb else b.shape[::-1]
    assert kb == k
    assert m % tm == 0 and n % tn == 0 and k % tk == 0, (m, n, k, tm, tn, tk)
    nk = k // tk
    a_spec = pl.BlockSpec((tk, tm), lambda i, j, q: (q, i + m_off)) if ta \
        else pl.BlockSpec((tm, tk), lambda i, j, q: (i + m_off, q))
    b_spec = pl.BlockSpec((tn, tk), lambda i, j, q: (j, q)) if tb else pl.BlockSpec((tk, tn), lambda i, j, q: (q, j))
    o_spec = pl.BlockSpec((tm, tn), lambda i, j, q: (i, j))
    o_shape = (m, n)
    dims = (((0 if ta else 1,), (1 if tb else 0,)), ((), ()))
    has_add = add is not None
    has_dep = dep is not None

    def body(*refs):
        if has_dep:
            refs = refs[:-3] + refs[-2:]
        if has_add:
            a_ref, b_ref, add_ref, o_ref, acc_ref = refs
        else:
            a_ref, b_ref, o_ref, acc_ref = refs
        q = pl.program_id(2)
        part = lax.dot_general(a_ref[...], b_ref[...], dims, preferred_element_type=F32)

        def finish(acc):
            if has_add:
                acc = acc + add_ref[...]
            o_ref[...] = acc.astype(out_dtype)

        if nk == 1:
            finish(part)
        else:
            @pl.when(q == 0)
            def _():
                acc_ref[...] = part

            @pl.when(jnp.logical_and(q > 0, q < nk - 1))
            def _():
                acc_ref[...] += part

            @pl.when(q == nk - 1)
            def _():
                finish(acc_ref[...] + part)

    in_specs = [a_spec, b_spec] + ([o_spec] if has_add else [])
    args = (a, b) + ((add,) if has_add else ())
    if has_dep:
        in_specs.append(pl.BlockSpec((SUBLANES, LANES), lambda i, j, q: (0, 0)))
        args += (dep,)
    acc_shape = (tm, tn) if nk > 1 else (SUBLANES, LANES)
    return pl.pallas_call(
        body, name=name,
        grid=(m // tm, n // tn, nk),
        in_specs=in_specs, out_specs=o_spec,
        out_shape=jax.ShapeDtypeStruct(o_shape, out_dtype),
        scratch_shapes=[pltpu.VMEM(acc_shape, F32)],
        compiler_params=_params(("parallel", "parallel", "arbitrary")),
    )(*args)


def _rms_fwd(h, g, *, name):
    tp, d = h.shape
    tr = _tile(tp, 512, SUBLANES)

    def body(h_ref, g_ref, o_ref):
        hv = h_ref[...]
        rstd = lax.rsqrt(jnp.mean(hv * hv, axis=-1, keepdims=True) + RMS_EPS)
        o_ref[...] = (hv * rstd * g_ref[...]).astype(BF16)

    return pl.pallas_call(
        body, name=name, grid=(tp // tr,),
        in_specs=[pl.BlockSpec((tr, d), lambda i: (i, 0)), pl.BlockSpec((1, d), lambda i: (0, 0))],
        out_specs=pl.BlockSpec((tr, d), lambda i: (i, 0)),
        out_shape=jax.ShapeDtypeStruct((tp, d), BF16),
        compiler_params=_params(("parallel",)),
    )(h, g.reshape(1, d))


def _rms_fwd_first(x, meta, g, *, tp, name):
    seq, d = x.shape
    n_meta = meta.shape[0]
    n_tok = n_meta + seq
    tr = SHIFTED_ROWS
    assert tp % tr == 0 and seq % tr == 0 and tr % n_meta == 0
    per = tr // n_meta

    def body(x_ref, xp_ref, m_ref, g_ref, h_ref, o_ref):
        i = pl.program_id(0)
        head = jnp.where(i == 0, m_ref[...], xp_ref[...])
        rows = i * tr + lax.broadcasted_iota(jnp.int32, (tr, 1), 0)
        hv = jnp.where(rows < n_tok, jnp.concatenate([head, x_ref[:tr - n_meta, :]], axis=0), 0.0)
        h_ref[...] = hv
        rstd = lax.rsqrt(jnp.mean(hv * hv, axis=-1, keepdims=True) + RMS_EPS)
        o_ref[...] = (hv * rstd * g_ref[...]).astype(BF16)

    row = pl.BlockSpec((tr, d), lambda i: (i, 0))
    own = pl.BlockSpec((tr, d), lambda i: (jnp.minimum(i, seq // tr - 1), 0))
    before = pl.BlockSpec((n_meta, d), lambda i: (jnp.maximum(i * per - 1, 0), 0))
    return pl.pallas_call(
        body, name=name, grid=(tp // tr,),
        in_specs=[own, before, pl.BlockSpec((n_meta, d), lambda i: (0, 0)), pl.BlockSpec((1, d), lambda i: (0, 0))],
        out_specs=[row, row],
        out_shape=[jax.ShapeDtypeStruct((tp, d), F32), jax.ShapeDtypeStruct((tp, d), BF16)],
        compiler_params=_params(("parallel",)),
    )(x, x, meta, g.reshape(1, d))


def _rms_bwd(h, dhn, dout, g, *, name):
    tp, d = h.shape
    tr = _tile(tp, 528, 2 * SUBLANES)

    def body(h_ref, dhn_ref, dout_ref, g_ref, dh_ref, dhb_ref, dg_ref):
        hv = h_ref[...]
        rstd = lax.rsqrt(jnp.mean(hv * hv, axis=-1, keepdims=True) + RMS_EPS)
        xhat = hv * rstd
        dn = dhn_ref[...]
        dxhat = dn * g_ref[...]
        dh = dout_ref[...] + rstd * (dxhat - xhat * jnp.mean(dxhat * xhat, axis=-1, keepdims=True))
        dh_ref[...] = dh
        dhb_ref[...] = dh.astype(BF16)
        part = jnp.sum(dn * xhat, axis=0, keepdims=True)

        @pl.when(pl.program_id(0) == 0)
        def _():
            dg_ref[...] = part

        @pl.when(pl.program_id(0) > 0)
        def _():
            dg_ref[...] += part

    row = pl.BlockSpec((tr, d), lambda i: (i, 0))
    vec = pl.BlockSpec((1, d), lambda i: (0, 0))
    return pl.pallas_call(
        body, name=name, grid=(tp // tr,),
        in_specs=[row, row, row, vec],
        out_specs=[row, row, vec],
        out_shape=[jax.ShapeDtypeStruct((tp, d), F32), jax.ShapeDtypeStruct((tp, d), BF16),
                   jax.ShapeDtypeStruct((1, d), F32)],
        compiler_params=_params(("arbitrary",)),
    )(h, dhn, dout, g.reshape(1, d))


def _rms_bwd_first(h, dhn, dout, g, *, n_meta, seq, name):
    tp, d = h.shape
    tr = SHIFTED_ROWS
    assert seq % tr == 0 and tr % n_meta == 0 and tp >= seq + n_meta
    nt = seq // tr
    per = tr // n_meta

    def grads(hv, dn, do, gv):
        rstd = lax.rsqrt(jnp.mean(hv * hv, axis=-1, keepdims=True) + RMS_EPS)
        xhat = hv * rstd
        dxhat = dn * gv
        dh = do + rstd * (dxhat - xhat * jnp.mean(dxhat * xhat, axis=-1, keepdims=True))
        return dh, jnp.sum(dn * xhat, axis=0, keepdims=True)

    def body(h_ref, dhn_ref, dout_ref, hn_ref, dhnn_ref, doutn_ref, g_ref, gx_ref, dmeta_ref, dg_ref):
        i = pl.program_id(0)
        gv = g_ref[...]
        dh, part = grads(h_ref[...], dhn_ref[...], dout_ref[...], gv)
        dh_next, part_next = grads(hn_ref[...], dhnn_ref[...], doutn_ref[...], gv)
        gx_ref[...] = jnp.concatenate([dh[n_meta:], dh_next], axis=0)

        @pl.when(i == 0)
        def _():
            dmeta_ref[...] = dh[:n_meta]
            dg_ref[...] = part

        @pl.when(i > 0)
        def _():
            dg_ref[...] += part

        @pl.when(i == nt - 1)
        def _():
            dg_ref[...] += part_next

    row = pl.BlockSpec((tr, d), lambda i: (i, 0))
    nxt = pl.BlockSpec((n_meta, d), lambda i: ((i + 1) * per, 0))
    vec = pl.BlockSpec((1, d), lambda i: (0, 0))
    return pl.pallas_call(
        body, name=name, grid=(nt,),
        in_specs=[row, row, row, nxt, nxt, nxt, vec],
        out_specs=[pl.BlockSpec((None, tr, d), lambda i: (0, i, 0)), pl.BlockSpec((n_meta, d), lambda i: (0, 0)), vec],
        out_shape=[jax.ShapeDtypeStruct((1, seq, d), F32), jax.ShapeDtypeStruct((n_meta, d), F32),
                   jax.ShapeDtypeStruct((1, d), F32)],
        compiler_params=_params(("arbitrary",)),
    )(h, dhn, dout, h, dhn, dout, g.reshape(1, d))


def _loss_head(h, tgt, g, *, n_meta, n_tok, name):
    tp, d = h.shape
    seq = tgt.shape[0]
    tr = SHIFTED_ROWS
    assert tp % tr == 0 and seq % tr == 0 and tr % n_meta == 0
    per = tr // n_meta

    def body(h_ref, t_ref, tp_ref, g_ref, dh_ref, dhb_ref, dg_ref, loss_ref):
        i = pl.program_id(0)
        hv = h_ref[...]
        rstd = lax.rsqrt(jnp.mean(hv * hv, axis=-1, keepdims=True) + RMS_EPS)
        xhat = hv * rstd
        gv = g_ref[...]
        rows = i * tr + lax.broadcasted_iota(jnp.int32, (tr, 1), 0)
        valid = jnp.logical_and(rows >= n_meta, rows < n_tok)
        target = jnp.concatenate([tp_ref[...], t_ref[:tr - n_meta, :]], axis=0)
        err = jnp.where(valid, xhat * gv - target, 0.0)
        dy = err * (1.0 / d)
        dxhat = dy * gv
        dh = rstd * (dxhat - xhat * jnp.mean(dxhat * xhat, axis=-1, keepdims=True))
        dh_ref[...] = dh
        dhb_ref[...] = dh.astype(BF16)
        dg_part = jnp.sum(dy * xhat, axis=0, keepdims=True)
        per_row = jnp.sum(err * err, axis=-1, keepdims=True) * (1.0 / d)
        loss_part = jnp.broadcast_to(0.5 * jnp.sum(per_row, axis=0, keepdims=True), (SUBLANES, LANES))

        @pl.when(i == 0)
        def _():
            dg_ref[...] = dg_part
            loss_ref[...] = loss_part

        @pl.when(i > 0)
        def _():
            dg_ref[...] += dg_part
            loss_ref[...] += loss_part

    row = pl.BlockSpec((tr, d), lambda i: (i, 0))
    vec = pl.BlockSpec((1, d), lambda i: (0, 0))
    own = pl.BlockSpec((tr, d), lambda i: (jnp.minimum(i, seq // tr - 1), 0))
    before = pl.BlockSpec((n_meta, d), lambda i: (jnp.maximum(i * per - 1, 0), 0))
    return pl.pallas_call(
        body, name=name, grid=(tp // tr,),
        in_specs=[row, own, before, vec],
        out_specs=[row, row, vec, pl.BlockSpec((SUBLANES, LANES), lambda i: (0, 0))],
        out_shape=[jax.ShapeDtypeStruct((tp, d), F32), jax.ShapeDtypeStruct((tp, d), BF16),
                   jax.ShapeDtypeStruct((1, d), F32), jax.ShapeDtypeStruct((SUBLANES, LANES), F32)],
        compiler_params=_params(("arbitrary",)),
    )(h, tgt, tgt, g.reshape(1, d))


def _shift_down(halo, tile, s):
    if s == 0:
        return tile
    ext = jnp.concatenate([halo, tile], axis=0)
    return pltpu.roll(ext, s, 0)[SUBLANES:]


def _shift_up(tile, head, s):
    if s == 0:
        return tile
    ext = jnp.concatenate([tile, head], axis=0)
    n = ext.shape[0]
    return pltpu.roll(ext, n - s, 0)[: tile.shape[0]]


def _to_lane_blocks(ref, cols, val):
    for j in range(cols.start // LANES, cols.stop // LANES):
        ref[j] = val[:, j * LANES - cols.start:(j + 1) * LANES - cols.start]


def _from_lane_blocks(ref, cols):
    return jnp.concatenate([ref[j] for j in range(cols.start // LANES, cols.stop // LANES)], axis=1)


def _scan_tile(a_ref, b_ref, out_ref, carry, j, *, reverse):
    ng = a_ref.shape[1] // SUBLANES
    order = list(range(SUBLANES))[::-1] if reverse else list(range(SUBLANES))

    def rows(r):
        return pl.ds(r, ng, stride=SUBLANES)

    prod, loc = {}, {}
    prev = None
    for r in order:
        ar = a_ref[j, rows(r), :]
        br = b_ref[j, rows(r), :]
        prod[r] = ar if prev is None else ar * prod[prev]
        loc[r] = br if prev is None else ar * loc[prev] + br
        prev = r
    pg, lg = prod[prev], loc[prev]
    ones = jnp.ones((SUBLANES,) + pg.shape[1:], F32)
    zeros = jnp.zeros_like(ones)
    s = 1
    while s < ng:
        p_sh = _shift_up(pg, ones, s) if reverse else _shift_down(ones, pg, s)
        l_sh = _shift_up(lg, zeros, s) if reverse else _shift_down(zeros, lg, s)
        lg = pg * l_sh + lg
        pg = pg * p_sh
        s *= 2
    leaving = pg * carry[0:1, :] + lg
    entering = _shift_up(leaving, carry, 1) if reverse else _shift_down(carry, leaving, 1)
    for r in order:
        out_ref[j, rows(r), :] = loc[r] + prod[r] * entering
    last = leaving[0:1, :] if reverse else leaving[ng - 1:ng, :]
    return jnp.broadcast_to(last, carry.shape)


def _gates(ca, wr, wi, br, bi, sp):
    cab = ca.astype(BF16)
    r = _sigmoid(jnp.dot(cab, wr, preferred_element_type=F32) + br)
    ig = _sigmoid(jnp.dot(cab, wi, preferred_element_type=F32) + bi)
    la = -LRU_C * r * sp
    a = jnp.exp(la)
    mult = jnp.sqrt(-jnp.tanh(la) * (a * a + 1.0))
    return r, ig, a, mult


U_STREAMS = 3


class _ColumnParts:
    def __init__(self, refs):
        self.refs = refs
        self.width = refs[0].shape[-1]

    def __getitem__(self, idx):
        rows, cols = idx
        k = cols.start // self.width
        assert cols.stop <= (k + 1) * self.width
        return self.refs[k][rows, cols.start - k * self.width:cols.stop - k * self.width]


def _mixer_fwd(u, wa, ba, wr_blk, br, wi_blk, bi, lam, wb, *, name):
    tp, din = u.shape
    dl = din // 6
    tt = MIX_ROWS
    cw = GATE_BLOCK
    nch = dl // cw
    assert tp % tt == 0 and dl % cw == 0

    def body(*refs):
        u_ref = _ColumnParts(refs[:U_STREAMS])
        (wa_ref, ba_ref, wr_ref, br_ref, wi_ref, bi_ref, lam_ref, wb_ref,
         s_ref, y_ref, xa_tail, v_tail, h_carry, a_s, b_s, h_s) = refs[U_STREAMS:]

        @pl.when(pl.program_id(0) == 0)
        def _():
            xa_tail[...] = jnp.zeros_like(xa_tail)
            v_tail[...] = jnp.zeros_like(v_tail)
            h_carry[...] = jnp.zeros_like(h_carry)

        for ch in range(nch):
            cs = slice(ch * cw, (ch + 1) * cw)

            def seg(s):
                return slice(s * dl + ch * cw, s * dl + (ch + 1) * cw)

            xa = u_ref[:, seg(0)]
            halo = xa_tail[:, cs]
            ca = ba_ref[:, cs] + wa_ref[3:4, cs] * xa
            for kk in range(3):
                ca = ca + wa_ref[kk:kk + 1, cs] * _shift_down(halo, xa, 3 - kk)
            xa_tail[:, cs] = xa[tt - SUBLANES:]
            s_ref[:, cs] = ca
            sp = _softplus(-lam_ref[:, cs])
            _, ig, a, mult = _gates(ca, wr_ref[ch], wi_ref[ch], br_ref[:, cs], bi_ref[:, cs], sp)
            _to_lane_blocks(a_s, cs, a)
            _to_lane_blocks(b_s, cs, mult * (ig * ca))

            bv = u_ref[:, seg(2)]
            v = u_ref[:, seg(3)] * u_ref[:, seg(4)]
            gb = u_ref[:, seg(5)]
            vh = v_tail[:, cs]
            cb = wb_ref[2:3, cs] * v
            for kk in range(2):
                cb = cb + wb_ref[kk:kk + 1, cs] * _shift_down(vh, v, 2 - kk)
            v_tail[:, cs] = v[tt - SUBLANES:]
            y_ref[:, dl + ch * cw: dl + (ch + 1) * cw] = (bv * cb * (gb * _sigmoid(gb))).astype(BF16)

        for ch in range(nch):
            cs = slice(ch * cw, (ch + 1) * cw)
            for j in range(cs.start // LANES, cs.stop // LANES):
                lanes = slice(j * LANES, (j + 1) * LANES)
                h_carry[:, lanes] = _scan_tile(a_s, b_s, h_s, h_carry[:, lanes], j, reverse=False)
            hsv = _from_lane_blocks(h_s, cs)
            s_ref[:, dl + ch * cw: dl + (ch + 1) * cw] = hsv
            ga = u_ref[:, dl + ch * cw: dl + (ch + 1) * cw]
            y_ref[:, cs] = (hsv * (ga * _sigmoid(ga))).astype(BF16)

    row = lambda w: pl.BlockSpec((tt, w), lambda i: (i, 0))
    full = lambda shp: pl.BlockSpec(shp, lambda i: tuple(0 for _ in shp))
    return pl.pallas_call(
        body, name=name, grid=(tp // tt,),
        in_specs=[pl.BlockSpec((tt, din // U_STREAMS), functools.partial(lambda k, i: (i, k), k))
                  for k in range(U_STREAMS)]
        + [full((4, dl)), full((1, dl)), full((nch, cw, cw)), full((1, dl)),
           full((nch, cw, cw)), full((1, dl)), full((1, dl)), full((3, dl))],
        out_specs=[row(2 * dl), row(2 * dl)],
        out_shape=[jax.ShapeDtypeStruct((tp, 2 * dl), F32), jax.ShapeDtypeStruct((tp, 2 * dl), BF16)],
        scratch_shapes=[pltpu.VMEM((SUBLANES, dl), F32), pltpu.VMEM((SUBLANES, dl), F32),
                        pltpu.VMEM((SUBLANES, dl), F32)] + [pltpu.VMEM((dl // LANES, tt, LANES), F32)] * 3,
        compiler_params=_params(("arbitrary",)),
    )(*[u] * U_STREAMS, wa, ba, wr_blk, br, wi_blk, bi, lam, wb)


SG_WA, SG_BA, SG_BR, SG_BI, SG_LAM, SG_WB, SG_ROWS = 0, 4, 5, 6, 7, 8, 16


def _mixer_bwd(u, saved, dy, wa, wr_blk, br, wi_blk, bi, lam, wb, *, name):
    tp, din = u.shape
    dl = din // 6
    tt = MIX_ROWS
    cw = GATE_BLOCK
    nch = dl // cw
    nt = tp // tt
    hb = tt // SUBLANES
    tn_dims = (((0,), (0,)), ((), ()))
    nt_dims = (((1,), (1,)), ((), ()))

    def body(*refs):
        u_ref = _ColumnParts(refs[:U_STREAMS])
        (uh_ref, s_ref, sh_ref, dy_ref, wa_ref, wr_ref, br_ref, wi_ref, bi_ref, lam_ref, wb_ref,
         du_ref, sg_ref, dwr_ref, dwi_ref,
         g_carry, a_head, dca_head, dcb_head, r_s, i_s, a_s, an_s, d_s, g_s) = refs[U_STREAMS:]
        i = pl.program_id(0)
        first_tile = i == nt - 1

        @pl.when(i == 0)
        def _():
            for ref in (g_carry, a_head, dca_head, dcb_head, sg_ref, dwr_ref, dwi_ref):
                ref[...] = jnp.zeros_like(ref)

        def halo_of(x):
            return jnp.where(first_tile, 0.0, x)

        for ch in range(nch):
            cs = slice(ch * cw, (ch + 1) * cw)
            cav = s_ref[:, cs]
            sp = _softplus(-lam_ref[:, cs])
            r, ig, a, _ = _gates(cav, wr_ref[ch], wi_ref[ch], br_ref[:, cs], bi_ref[:, cs], sp)
            r_s[:, cs] = r
            i_s[:, cs] = ig
            a_s[:, cs] = a
            _to_lane_blocks(an_s, cs, _shift_up(a, a_head[:, cs], 1))
            a_head[:, cs] = a[:SUBLANES]
            ga = u_ref[:, dl + ch * cw: dl + (ch + 1) * cw]
            _to_lane_blocks(d_s, cs, dy_ref[:, cs] * (ga * _sigmoid(ga)))

        for j in range(dl // LANES):
            lanes = slice(j * LANES, (j + 1) * LANES)
            g_carry[:, lanes] = _scan_tile(an_s, d_s, g_s, g_carry[:, lanes], j, reverse=True)

        def acc_row(r0, val):
            sg_ref[r0:r0 + 1, cs_cur[0]] += jnp.sum(val, axis=0, keepdims=True)

        cs_cur = [None]
        for ch in range(nch):
            cs = slice(ch * cw, (ch + 1) * cw)
            cs_cur[0] = cs

            def seg(s):
                return slice(s * dl + ch * cw, s * dl + (ch + 1) * cw)

            cav = s_ref[:, cs]
            r = r_s[:, cs]
            ig = i_s[:, cs]
            a = a_s[:, cs]
            g = _from_lane_blocks(g_s, cs)
            hsv = s_ref[:, dl + ch * cw: dl + (ch + 1) * cw]
            lamv = lam_ref[:, cs]
            sp = _softplus(-lamv)
            la = -LRU_C * r * sp
            e2 = a * a
            one_m_e2 = -jnp.tanh(la) * (e2 + 1.0)
            mult = jnp.sqrt(one_m_e2)
            hprev = _shift_down(halo_of(sh_ref[:, dl + ch * cw: dl + (ch + 1) * cw]), hsv, 1)
            icav = ig * cav
            dla = g * (hprev * a - icav * (e2 * lax.rsqrt(one_m_e2)))
            gm = g * mult
            dzi = gm * icav * (1.0 - ig)
            dca = gm * ig
            dla_r = dla * r
            dzr = dla_r * (1.0 - r) * (-LRU_C * sp)
            sg_ref[SG_LAM:SG_LAM + 1, cs] += jnp.sum(dla_r, axis=0, keepdims=True) * (LRU_C * _sigmoid(-lamv))
            acc_row(SG_BR, dzr)
            acc_row(SG_BI, dzi)
            dzr_b = dzr.astype(BF16)
            dzi_b = dzi.astype(BF16)
            cab = cav.astype(BF16)
            dca = dca + lax.dot_general(dzr_b, wr_ref[ch], nt_dims, preferred_element_type=F32)
            dca = dca + lax.dot_general(dzi_b, wi_ref[ch], nt_dims, preferred_element_type=F32)
            dwr_ref[ch] += lax.dot_general(cab, dzr_b, tn_dims, preferred_element_type=F32)
            dwi_ref[ch] += lax.dot_general(cab, dzi_b, tn_dims, preferred_element_type=F32)
            acc_row(SG_BA, dca)
            xa = u_ref[:, seg(0)]
            head = dca_head[:, cs]
            dxa = wa_ref[3:4, cs] * dca
            acc_row(SG_WA + 3, dca * xa)
            for kk in range(3):
                later = _shift_up(dca, head, 3 - kk)
                acc_row(SG_WA + kk, later * xa)
                dxa = dxa + wa_ref[kk:kk + 1, cs] * later
            dca_head[:, cs] = dca[:SUBLANES]
            ga = u_ref[:, seg(1)]
            sga = _sigmoid(ga)
            dga = dy_ref[:, cs] * hsv * (sga + (ga * sga) * (1.0 - sga))
            du_ref[:, seg(0)] = dxa.astype(BF16)
            du_ref[:, seg(1)] = dga.astype(BF16)

            bv = u_ref[:, seg(2)]
            cv = u_ref[:, seg(3)]
            xb = u_ref[:, seg(4)]
            gb = u_ref[:, seg(5)]
            dyb = dy_ref[:, dl + ch * cw: dl + (ch + 1) * cw]
            v = cv * xb
            vh = halo_of(uh_ref[:, seg(3)] * uh_ref[:, seg(4)])
            v1 = _shift_down(vh, v, 1)
            v2 = _shift_down(vh, v, 2)
            cb = wb_ref[2:3, cs] * v + wb_ref[1:2, cs] * v1 + wb_ref[0:1, cs] * v2
            sgb = _sigmoid(gb)
            sl = gb * sgb
            dyb_b = dyb * bv
            dyb_cb = dyb * cb
            dcb = dyb_b * sl
            du_ref[:, seg(2)] = (dyb_cb * sl).astype(BF16)
            du_ref[:, seg(5)] = (dyb_cb * bv * (sgb + sl * (1.0 - sgb))).astype(BF16)
            bhead = dcb_head[:, cs]
            dv = wb_ref[2:3, cs] * dcb
            acc_row(SG_WB + 2, dcb * v)
            for kk in range(2):
                later = _shift_up(dcb, bhead, 2 - kk)
                acc_row(SG_WB + kk, later * v)
                dv = dv + wb_ref[kk:kk + 1, cs] * later
            dcb_head[:, cs] = dcb[:SUBLANES]
            du_ref[:, seg(3)] = (dv * xb).astype(BF16)
            du_ref[:, seg(4)] = (dv * cv).astype(BF16)

    rev = lambda w: pl.BlockSpec((tt, w), lambda i: (nt - 1 - i, 0))
    halo = lambda w: pl.BlockSpec((SUBLANES, w), lambda i: (jnp.maximum((nt - 1 - i) * hb - 1, 0), 0))
    full = lambda shp: pl.BlockSpec(shp, lambda i: tuple(0 for _ in shp))
    vm = lambda r: pltpu.VMEM((r, dl), F32)
    return pl.pallas_call(
        body, name=name, grid=(nt,),
        in_specs=[pl.BlockSpec((tt, din // U_STREAMS), functools.partial(lambda k, i: (nt - 1 - i, k), k))
                  for k in range(U_STREAMS)]
        + [halo(din), rev(2 * dl), halo(2 * dl), rev(2 * dl), full((4, dl)),
           full((nch, cw, cw)), full((1, dl)), full((nch, cw, cw)), full((1, dl)), full((1, dl)), full((3, dl))],
        out_specs=[rev(din), full((SG_ROWS, dl)), full((nch, cw, cw)), full((nch, cw, cw))],
        out_shape=[jax.ShapeDtypeStruct((tp, din), BF16), jax.ShapeDtypeStruct((SG_ROWS, dl), F32),
                   jax.ShapeDtypeStruct((nch, cw, cw), F32), jax.ShapeDtypeStruct((nch, cw, cw), F32)],
        scratch_shapes=[vm(SUBLANES), vm(SUBLANES), vm(SUBLANES), vm(SUBLANES), vm(tt), vm(tt), vm(tt)]
        + [pltpu.VMEM((dl // LANES, tt, LANES), F32)] * 3,
        compiler_params=_params(("arbitrary",)),
    )(*[u] * (U_STREAMS + 1), saved, saved, dy, wa, wr_blk, br, wi_blk, bi, lam, wb)


def _adamw(w, g, m, v, *, name, landed=None, layer=None, depth=None, into=None, row_off=0):
    r, c = w.shape[-2:]
    rows = g.shape[0]
    tr = _tile(rows, 512, 2 * SUBLANES)
    assert row_off % tr == 0
    boff = row_off // tr
    bc1 = 1.0 - ADAM_B1 ** ADAM_STEP
    bc2 = 1.0 - ADAM_B2 ** ADAM_STEP
    slots = landed is not None

    def body(*refs):
        if into is not None:
            refs = refs[:-8] + refs[-4:]
        if slots:
            w_ref, g_ref, l_ref, m_ref, v_ref, grad_ref, delta_ref, nm_ref, nv_ref = refs
            gv = g_ref[...].astype(F32)
            for s in range(N_DEV - 1):
                gv = gv + l_ref[s].astype(F32)
        else:
            w_ref, g_ref, m_ref, v_ref, grad_ref, delta_ref, nm_ref, nv_ref = refs
            gv = g_ref[...]
        wv = w_ref[...]
        mn = ADAM_B1 * m_ref[...] + (1.0 - ADAM_B1) * gv
        vn = ADAM_B2 * v_ref[...] + (1.0 - ADAM_B2) * (gv * gv)
        m_hat = mn / bc1
        v_hat = vn / bc2
        grad_ref[...] = gv
        delta_ref[...] = -ADAM_LR * (m_hat / (jnp.sqrt(v_hat) + ADAM_EPS) + ADAM_WD * wv)
        nm_ref[...] = mn
        nv_ref[...] = vn

    if depth is None:
        blk = pl.BlockSpec((tr, c), lambda i: (i + boff, 0))
    else:
        blk = pl.BlockSpec((None, tr, c), lambda i: (layer, i + boff, 0))
    g_blk = pl.BlockSpec((tr, c), lambda i: (i, 0))
    l_spec = [pl.BlockSpec((N_DEV - 1, tr, c), lambda i: (0, i, 0))] if slots else []
    args = (w, g, landed, m, v) if slots else (w, g, m, v)
    in_specs = [blk, g_blk] + l_spec + [blk, blk]
    if depth is None:
        shp = jax.ShapeDtypeStruct((r, c), F32)
        out_blk = blk
    else:
        shp = jax.ShapeDtypeStruct((depth, r, c), F32)
        out_blk = pl.BlockSpec((None, tr, c), lambda i: (layer, i + boff, 0))
    aliases = {}
    if into is not None:
        aliases = {len(args) + j: j for j in range(4)}
        in_specs = in_specs + [ANY] * 4
        args = args + tuple(into)
    return pl.pallas_call(
        body, name=name, grid=(rows // tr,),
        in_specs=in_specs, out_specs=[out_blk] * 4,
        out_shape=[shp] * 4, input_output_aliases=aliases,
        compiler_params=_params(("parallel",)),
    )(*args)


def _slot_sum(g, *, name):
    _, r, c = g.shape
    tr = _tile(r, 512, SUBLANES)

    def body(g_ref, o_ref):
        gv = g_ref[0].astype(F32)
        for s in range(1, N_DEV):
            gv = gv + g_ref[s].astype(F32)
        o_ref[...] = gv

    return pl.pallas_call(
        body, name=name, grid=(r // tr,),
        in_specs=[pl.BlockSpec((N_DEV, tr, c), lambda i: (0, i, 0))],
        out_specs=pl.BlockSpec((tr, c), lambda i: (i, 0)),
        out_shape=jax.ShapeDtypeStruct((r, c), F32),
        compiler_params=_params(("parallel",)),
    )(g)


def _mesh_pos():
    x, y, c = lax.axis_index("x"), lax.axis_index("y"), lax.axis_index("c")
    return x, y, c, 4 * x + 2 * y + c


ANY = pl.BlockSpec(memory_space=pl.ANY)


GATHER_COPIES = 9


def _all_gather(srcs, out_shapes, views, *, name):
    n = len(srcs)
    SIB, X_OWN, Y_OWN, X_DIAG, Y_DIAG, SIB_X, SIB_Y, SIB_DIAG_TOP, SIB_DIAG_BOTTOM = range(GATHER_COPIES)

    def body(*refs):
        src = refs[:n]
        dst = refs[n:2 * n]
        send_sems, recv_sems, local_sems = refs[2 * n:]
        x, y, c, me = _mesh_pos()
        sibling, x_nbr, y_nbr = (x, y, 1 - c), (1 - x, y, c), (x, 1 - y, c)

        def block(a, px, py, pc, half=None):
            win = views[a](dst[a], 4 * px + 2 * py + pc)
            if half is None:
                return win
            rows = win.shape[0] // 2
            return win.at[pl.ds(half * rows, rows)]

        def copy(a, k, win, to, from_src=False):
            return pltpu.make_async_remote_copy(
                src_ref=src[a] if from_src else win, dst_ref=win,
                send_sem=send_sems.at[a * GATHER_COPIES + k], recv_sem=recv_sems.at[a * GATHER_COPIES + k],
                device_id=to, device_id_type=MESH)

        mine = [pltpu.make_async_copy(src[a], block(a, x, y, c), local_sems.at[a]) for a in range(n)]
        started = []

        def start(cp):
            cp.start()
            started.append(cp)

        for a in range(n):
            mine[a].start()
            own = block(a, x, y, c)
            start(copy(a, SIB, own, sibling, True))
            start(copy(a, X_OWN, own, x_nbr, True))
            start(copy(a, Y_OWN, own, y_nbr, True))
        for a in range(n):
            from_y = block(a, x, 1 - y, c)
            copy(a, Y_OWN, from_y, y_nbr).wait_recv()
            start(copy(a, X_DIAG, block(a, x, 1 - y, c, 0), x_nbr))
            start(copy(a, SIB_Y, from_y, sibling))
            from_x = block(a, 1 - x, y, c)
            copy(a, X_OWN, from_x, x_nbr).wait_recv()
            start(copy(a, Y_DIAG, block(a, 1 - x, y, c, 1), y_nbr))
            start(copy(a, SIB_X, from_x, sibling))
        for a in range(n):
            top = block(a, 1 - x, 1 - y, c, 0)
            copy(a, X_DIAG, top, x_nbr).wait_recv()
            start(copy(a, SIB_DIAG_TOP, top, sibling))
            bottom = block(a, 1 - x, 1 - y, c, 1)
            copy(a, Y_DIAG, bottom, y_nbr).wait_recv()
            start(copy(a, SIB_DIAG_BOTTOM, bottom, sibling))
        for a in range(n):
            copy(a, SIB, block(a, x, y, 1 - c), sibling).wait_recv()
            copy(a, SIB_X, block(a, 1 - x, y, 1 - c), sibling).wait_recv()
            copy(a, SIB_Y, block(a, x, 1 - y, 1 - c), sibling).wait_recv()
            copy(a, SIB_DIAG_TOP, block(a, 1 - x, 1 - y, 1 - c, 0), sibling).wait_recv()
            copy(a, SIB_DIAG_BOTTOM, block(a, 1 - x, 1 - y, 1 - c, 1), sibling).wait_recv()
        for cp in started:
            cp.wait_send()
        for cp in mine:
            cp.wait()

    return pl.pallas_call(
        body, name=name,
        in_specs=[ANY] * n, out_specs=[ANY] * n,
        out_shape=[jax.ShapeDtypeStruct(s, x.dtype) for s, x in zip(out_shapes, srcs)],
        scratch_shapes=[pltpu.SemaphoreType.DMA((GATHER_COPIES * n,)), pltpu.SemaphoreType.DMA((GATHER_COPIES * n,)),
                        pltpu.SemaphoreType.DMA((n,))],
    )(*srcs)


HBM = pl.BlockSpec(memory_space=pltpu.HBM)
SEM = pl.BlockSpec(memory_space=pltpu.SEMAPHORE)
EFFECT = pltpu.SideEffectType.DATAFLOW_SIDE_EFFECTING


def _peer_of(x, y, c, k):
    return (1 - x if k & 4 else x, 1 - y if k & 2 else y, 1 - c if k & 1 else c)


def _peer_copies(n, wins, src, land, send_sems, recv_sems):
    x, y, c, me = _mesh_pos()
    out = []
    for a in range(n):
        for k in range(1, N_DEV):
            px, py, pc = _peer_of(x, y, c, k)
            s_win, d_win = wins[a](src[a], land[a], me, 4 * px + 2 * py + pc, k)
            out.append(pltpu.make_async_remote_copy(
                src_ref=s_win, dst_ref=d_win,
                send_sem=send_sems.at[a * 7 + k - 1], recv_sem=recv_sems.at[a * 7 + k - 1],
                device_id=(px, py, pc), device_id_type=MESH))
    return out


def _push_start(srcs, lands, wins, *, name):
    n = len(srcs)

    def body(*refs):
        src = refs[:n]
        land = refs[n:2 * n]
        send_sems, recv_sems = refs[2 * n], refs[2 * n + 1]
        token = refs[-1]
        for cp in _peer_copies(n, wins, src, land, send_sems, recv_sems):
            cp.start()
        token[...] = jnp.zeros_like(token)

    bufs = (*srcs, *lands)
    return pl.pallas_call(
        body, name=name,
        out_shape=(pltpu.SemaphoreType.DMA((7 * n,)), pltpu.SemaphoreType.DMA((7 * n,)),
                   *[pltpu.HBM(v.shape, v.dtype) for v in bufs], jax.ShapeDtypeStruct((SUBLANES, LANES), F32)),
        in_specs=[HBM] * (2 * n),
        out_specs=(SEM, SEM, *[HBM] * (2 * n), pl.BlockSpec(memory_space=pltpu.VMEM)),
        input_output_aliases={i: 2 + i for i in range(2 * n)},
        compiler_params=pltpu.CompilerParams(has_side_effects=EFFECT),
    )(*[pltpu.with_memory_space_constraint(v, pltpu.HBM) for v in bufs])


def _push_wait(handle, wins, after, *, name):
    send_sems, recv_sems, *bufs, _ = handle
    n = len(bufs) // 2

    def body(*refs):
        src = refs[:n]
        land = refs[n:2 * n]
        for cp in _peer_copies(n, wins, src, land, refs[2 * n], refs[2 * n + 1]):
            cp.wait_send()
            cp.wait_recv()

    outs = pl.pallas_call(
        body, name=name,
        out_shape=tuple(pltpu.HBM(v.shape, v.dtype) for v in bufs),
        in_specs=[HBM] * (2 * n) + [SEM, SEM, ANY],
        out_specs=tuple([HBM] * (2 * n)),
        input_output_aliases={i: i for i in range(2 * n)},
        compiler_params=pltpu.CompilerParams(has_side_effects=EFFECT),
    )(*bufs, send_sems, recv_sems, after)
    return outs[:n], outs[n:]


def _gather_lead(src, land, me, peer, k):
    return src, land.at[me]


def _gather_cols(width):
    def win(src, land, me, peer, k):
        return src, land.at[:, pl.ds(me * width, width)]
    return win


def _scatter_lead(src, land, me, peer, k):
    return src.at[peer], land.at[k - 1]


def _scatter_cols(width):
    def win(src, land, me, peer, k):
        return src.at[:, pl.ds(peer * width, width)], land.at[k - 1]
    return win


def _place_block(own, *, cols, name):
    rows, width = own.shape
    tr = _tile(rows, 512, 2 * SUBLANES)
    _, _, _, me = _mesh_pos()

    def body(me_ref, x_ref, o_ref):
        o_ref[...] = x_ref[...]

    if cols:
        out_spec = pl.BlockSpec((tr, width), lambda i, me_ref: (i, me_ref[0]))
        shape = (rows, N_DEV * width)
    else:
        out_spec = pl.BlockSpec((None, tr, width), lambda i, me_ref: (me_ref[0], i, 0))
        shape = (N_DEV, rows, width)
    return pl.pallas_call(
        body, name=name,
        grid_spec=pltpu.PrefetchScalarGridSpec(
            num_scalar_prefetch=1, grid=(rows // tr,),
            in_specs=[pl.BlockSpec((tr, width), lambda i, me_ref: (i, 0))], out_specs=out_spec),
        out_shape=jax.ShapeDtypeStruct(shape, own.dtype),
        compiler_params=_params(("arbitrary",)),
    )(me.astype(jnp.int32).reshape(1), own)


def _dep(x, token):
    return x + token[0, 0].astype(x.dtype)


def _lead(ref, d):
    return ref.at[d]


def _col_window(width):
    def view(ref, d):
        return ref.at[:, pl.ds(d * width, width)]
    return view


def _pack(arrs):
    flat = jnp.concatenate([a.reshape(-1).astype(F32) for a in arrs])
    n = flat.shape[0]
    rows = -(-n // (2 * SUBLANES * LANES)) * 2 * SUBLANES
    return jnp.pad(flat, (0, rows * LANES - n)).reshape(rows, LANES)


def _unpack(buf, shapes):
    flat = buf.reshape(-1)
    out, off = [], 0
    for s in shapes:
        n = 1
        for q in s:
            n *= q
        out.append(flat[off:off + n].reshape(s))
        off += n
    return out


def _blockdiag(w, cw):
    h, hd, _ = w.shape
    per = cw // hd
    wg = w.reshape(h // per, per, hd, hd)
    eye = jnp.eye(per, dtype=w.dtype)
    blk = jnp.einsum("gpij,pq->gpiqj", wg, eye)
    return blk.reshape(h // per, cw, cw).astype(BF16)


def _blockdiag_extract(g, hd):
    n, cw, _ = g.shape
    per = cw // hd
    g5 = g.reshape(n, per, hd, per, hd)
    idx = jnp.arange(per)
    return g5[:, idx, :, idx, :].transpose(1, 0, 2, 3).reshape(n * per, hd, hd)


def kernel(x, meta, norm_g, w_in, conv_a_w, conv_a_b, lru_wr, lru_br, lru_wi, lru_bi, lru_lambda, conv_b_w, w_out, final_g, loss_target, m_meta, m_norm_g, m_w_in, m_conv_a_w, m_conv_a_b, m_lru_wr, m_lru_br, m_lru_wi, m_lru_bi, m_lru_lambda, m_conv_b_w, m_w_out, m_final_g, v_meta, v_norm_g, v_w_in, v_conv_a_w, v_conv_a_b, v_lru_wr, v_lru_br, v_lru_wi, v_lru_bi, v_lru_lambda, v_conv_b_w, v_w_out, v_final_g):
    _, seq, d = x.shape
    n_meta = meta.shape[0]
    depth = w_in.shape[0]
    din = w_in.shape[2] * N_DEV
    dl = din // 6
    dmix = 2 * dl
    wcol = w_in.shape[2]
    wrow = w_out.shape[1]
    mcol = meta.shape[1]
    ccol = conv_a_w.shape[2]
    heads, hd = lru_wr.shape[1], lru_wr.shape[2]
    n_tok = n_meta + seq
    tp = -(-n_tok // TOKEN_TILE) * TOKEN_TILE
    me = 4 * lax.axis_index("x") + 2 * lax.axis_index("y") + lax.axis_index("c")

    bf = lambda a: a.astype(BF16)
    small_mine = _pack([meta, conv_a_w, conv_b_w])
    first = _all_gather([bf(w_in[0]), small_mine], [(d, din), (N_DEV,) + small_mine.shape],
                        [_col_window(wcol), _lead], name="gather_first")
    flat = first[1].reshape(N_DEV, -1)
    sizes = [meta.size, conv_a_w.size, conv_b_w.size]
    meta_full = jnp.moveaxis(flat[:, :sizes[0]].reshape(N_DEV, n_meta, mcol), 0, 1).reshape(n_meta, d)
    wa_full = jnp.moveaxis(flat[:, sizes[0]:sizes[0] + sizes[1]].reshape(N_DEV, depth, 4, ccol), 0, 2) \
        .reshape(depth, 4, dl)
    wb_full = jnp.moveaxis(flat[:, sizes[0] + sizes[1]:sum(sizes)].reshape(N_DEV, depth, 3, ccol), 0, 2) \
        .reshape(depth, 3, dl)
    w_in_full = [None] * depth
    w_out_full = [None] * depth

    push_out = [None] * depth
    push_in = [None] * depth
    w_in_full[0], src = lax.optimization_barrier((first[0], bf(w_out[0])))
    push_out[0] = _push_start([src], [_place_block(src, cols=False, name="place_wout_0")], [_gather_lead],
                              name="gather_wout_0_start")
    token = push_out[0][-1]
    for l in range(1, depth):
        src = bf(_dep(w_in[l], token))
        push_in[l] = _push_start([src], [_place_block(src, cols=True, name=f"place_win_{l}")], [_gather_cols(wcol)],
                                 name=f"gather_win_{l}_start")
        src = bf(_dep(w_out[l], push_in[l][-1]))
        push_out[l] = _push_start([src], [_place_block(src, cols=False, name=f"place_wout_{l}")], [_gather_lead],
                                  name=f"gather_wout_{l}_start")
        token = push_out[l][-1]

    wr_blk = [_blockdiag(lru_wr[l], GATE_BLOCK) for l in range(depth)]
    wi_blk = [_blockdiag(lru_wi[l], GATE_BLOCK) for l in range(depth)]
    vec = lambda a: a.reshape(1, dl)

    tm = _tile(tp, 1408)
    saved = []
    for l in range(depth):
        if l == 0:
            h, hn = _rms_fwd_first(x[0], meta_full, _dep(norm_g[l], token), tp=tp, name=f"rms_fwd_{l}")
        else:
            hn = _rms_fwd(h, norm_g[l], name=f"rms_fwd_{l}")
        if l > 0:
            _, landed = _push_wait(push_in[l], [_gather_cols(wcol)], hn, name=f"gather_win_{l}_wait")
            w_in_full[l] = landed[0]
        u = _matmul(hn, w_in_full[l], tm=tm, tn=_tile(din, 1536), tk=d, name=f"mm_u_{l}")
        mixed, y = _mixer_fwd(u, wa_full[l], vec(conv_a_b[l]), wr_blk[l], vec(lru_br[l]), wi_blk[l], vec(lru_bi[l]),
                              vec(lru_lambda[l]), wb_full[l], name=f"mixer_fwd_{l}")
        _, landed = _push_wait(push_out[l], [_gather_lead], y, name=f"gather_wout_{l}_wait")
        w_out_full[l] = landed[0].reshape(dmix, d)
        h_next = _matmul(y, w_out_full[l], tm=tm, tn=_tile(d, 512), tk=dmix, add=h, name=f"mm_out_{l}")
        saved.append((h, hn, u, mixed, y))
        h = h_next

    dh, dhb, dg_final, loss_part = _loss_head(h, loss_target[0], final_g, n_meta=n_meta, n_tok=n_tok,
                                              name="loss_head")

    small_grads = [None] * depth
    sent_out = [None] * depth
    sent_in = [None] * depth
    scatter_in = [_scatter_cols(wcol)]
    token = None
    dg_norms = []
    for l in reversed(range(depth)):
        h_in, hn, u, mixed, y = saved[l]
        dy = _matmul(dhb, w_out_full[l], tb=True, tm=tm, tn=_tile(dmix, 1024), tk=d, dep=token, name=f"mm_dy_{l}")
        dw_out = _matmul(y, dhb, ta=True, tm=_tile(dmix, 1024), tn=_tile(d, 1024), tk=tp, out_dtype=BF16,
                         name=f"mm_dwout_{l}")
        sent_out[l] = _push_start([dw_out.reshape(N_DEV, wrow, d)], [lax.empty((N_DEV - 1, wrow, d), BF16)],
                                  [_scatter_lead], name=f"scatter_wout_{l}_start")
        du, sg, dwr, dwi = _mixer_bwd(u, mixed, dy, wa_full[l], wr_blk[l], vec(lru_br[l]), wi_blk[l], vec(lru_bi[l]),
                                      vec(lru_lambda[l]), _dep(wb_full[l], sent_out[l][-1]), name=f"mixer_bwd_{l}")
        small_grads[l] = (sg, dwr, dwi)
        if l == 0:
            rows = jnp.stack([small_grads[j][0] for j in range(depth)])
            early = [_pack([
                rows[:, SG_BA], rows[:, SG_BR], rows[:, SG_BI], rows[:, SG_LAM], rows[:, SG_WA:SG_WA + 4],
                rows[:, SG_WB:SG_WB + 3], dg_final[0], *dg_norms]),
                _pack([jnp.stack([_blockdiag_extract(small_grads[j][1], hd) for j in range(depth)]),
                       jnp.stack([_blockdiag_extract(small_grads[j][2], hd) for j in range(depth)])]).astype(BF16)]
            early_land = [lax.dynamic_update_slice(lax.empty((N_DEV,) + a.shape, a.dtype), a[None], (me, 0, 0))
                          for a in early]
            sent_early = _push_start(early, early_land, [_gather_lead] * 2, name="gather_early_grads_start")
        parts = 2 if l == 0 else 1
        token = sent_early[-1] if l == 0 else None
        sent_in[l] = []
        for p in range(parts):
            dw_in = _matmul(hn, du, ta=True, tm=_tile(d // parts, 512), tn=_tile(din, 1536), tk=tp, out_dtype=BF16,
                            dep=token, m_part=(p, parts), name=f"mm_dwin_{l}_{p}")
            sent_in[l].append(_push_start([dw_in], [lax.empty((N_DEV - 1, d // parts, wcol), BF16)], scatter_in,
                                          name=f"scatter_win_{l}_{p}_start"))
            token = sent_in[l][-1][-1]
        dhn = _matmul(du, w_in_full[l], tb=True, tm=_tile(tp, 528, 2 * SUBLANES), tn=_tile(d, 1024), tk=din, dep=token,
                      name=f"mm_dhn_{l}")
        if l > 0:
            dh, dhb, dg_norm = _rms_bwd(h_in, dhn, dh, norm_g[l], name=f"rms_bwd_{l}")
            dg_norms.append(dg_norm[0])
        else:
            grad_x, d_meta, dg_norm = _rms_bwd_first(h_in, dhn, dh, norm_g[l], n_meta=n_meta, seq=seq,
                                                     name=f"rms_bwd_{l}")

    late = _pack([dg_norm[0], d_meta, loss_part[0:1, 0:1]])
    late_all = _all_gather([late], [(N_DEV,) + late.shape], [_lead], name="gather_late_grads")[0]
    late_sum = _unpack(_slot_sum(late_all, name="sum_late_grads"), [(d,), (n_meta, d), ()])
    loss = late_sum[2]
    _, early_all = _push_wait(sent_early, [_gather_lead] * 2, late_sum[0], name="gather_early_grads_wait")
    vec_shapes = [conv_a_b.shape, lru_br.shape, lru_bi.shape, lru_lambda.shape, (depth, 4, dl), (depth, 3, dl),
                  final_g.shape] + [(d,)] * (depth - 1)
    e = _unpack(_slot_sum(early_all[0], name="sum_early_vectors"), vec_shapes)
    g_wr, g_wi = _unpack(_slot_sum(early_all[1], name="sum_early_maps"), [lru_wr.shape, lru_wi.shape])
    g_norm = jnp.stack([late_sum[0]] + e[7:][::-1])
    g_meta = lax.dynamic_slice_in_dim(late_sum[1], me * mcol, mcol, axis=1)
    g_wa = lax.dynamic_slice_in_dim(e[4], me * ccol, ccol, axis=2)
    g_wb = lax.dynamic_slice_in_dim(e[5], me * ccol, ccol, axis=2)

    small_w = [norm_g, conv_a_b, lru_wr, lru_br, lru_wi, lru_bi, lru_lambda, final_g, meta, conv_a_w, conv_b_w]
    small_m = [m_norm_g, m_conv_a_b, m_lru_wr, m_lru_br, m_lru_wi, m_lru_bi, m_lru_lambda, m_final_g, m_meta,
               m_conv_a_w, m_conv_b_w]
    small_v = [v_norm_g, v_conv_a_b, v_lru_wr, v_lru_br, v_lru_wi, v_lru_bi, v_lru_lambda, v_final_g, v_meta,
               v_conv_a_w, v_conv_b_w]
    small_g = [g_norm, e[0], g_wr, e[1], g_wi, e[2], e[3], e[6], g_meta, g_wa, g_wb]
    small_out = _adamw(_pack(small_w), _pack(small_g), _pack(small_m), _pack(small_v), name="adamw_small")
    small_shapes = [a.shape for a in small_w]
    s_grad, s_delta, s_m, s_v = [_unpack(o, small_shapes) for o in small_out]

    win_out = None
    wout_out = None
    after = small_out[0]
    for l in reversed(range(depth)):
        src, landed = _push_wait(sent_out[l], [_scatter_lead], after, name=f"scatter_wout_{l}_wait")
        own = lax.dynamic_index_in_dim(src[0], me, 0, keepdims=False)
        wout_out = _adamw(w_out, own, m_w_out, v_w_out, landed=landed[0], layer=l, depth=depth,
                          into=wout_out, name=f"adamw_w_out_{l}")
        after = wout_out[0]
        for p, sent in enumerate(sent_in[l]):
            src, landed = _push_wait(sent, scatter_in, after, name=f"scatter_win_{l}_{p}_wait")
            own = lax.dynamic_slice_in_dim(src[0], me * wcol, wcol, axis=1)
            win_out = _adamw(w_in, own, m_w_in, v_w_in, landed=landed[0], layer=l, depth=depth,
                             into=win_out, row_off=p * own.shape[0], name=f"adamw_w_in_{l}_{p}")
            after = win_out[0]

    names = ["norm_g", "conv_a_b", "lru_wr", "lru_br", "lru_wi", "lru_bi", "lru_lambda", "final_g", "meta",
             "conv_a_w", "conv_b_w"]
    order = ["meta", "norm_g", "w_in", "conv_a_w", "conv_a_b", "lru_wr", "lru_br", "lru_wi", "lru_bi", "lru_lambda",
             "conv_b_w", "w_out", "final_g"]

    def family(idx, small):
        table = {nm: small[i] for i, nm in enumerate(names)}
        table["w_in"] = win_out[idx]
        table["w_out"] = wout_out[idx]
        return [table[nm] for nm in order]

    return (loss, grad_x, *family(0, s_grad), *family(1, s_delta), *family(2, s_m), *family(3, s_v))
```

```python
import functools

import jax
import jax.numpy as jnp
from jax import lax
from jax.experimental import pallas as pl
from jax.experimental.pallas import tpu as pltpu

F32 = jnp.float32
BF16 = jnp.bfloat16
MESH = pl.DeviceIdType.MESH

N_DEV = 8
RMS_EPS = 1e-6
LRU_C = 8.0
ADAM_LR = 0.001
ADAM_B1 = 0.9
ADAM_B2 = 0.999
ADAM_EPS = 1e-08
ADAM_WD = 0.01
ADAM_STEP = 10

V7X_VMEM_LIMIT = 52 * 1024 * 1024
LANES = 128
SUBLANES = 8
TOKEN_TILE = 384
MIX_ROWS = 128
SHIFTED_ROWS = 128
GATE_BLOCK = 128


def _params(sem):
    return pltpu.CompilerParams(dimension_semantics=sem, vmem_limit_bytes=V7X_VMEM_LIMIT)


def _tile(n, target, align=LANES):
    best = None
    for t in range(align, min(n, target) + 1, align):
        if n % t == 0:
            best = t
    return n if best is None else best


def _sigmoid(z):
    return 0.5 * jnp.tanh(0.5 * z) + 0.5


def _softplus(z):
    e = jnp.exp(-jnp.abs(z))
    u = 1.0 + e
    l1p = jnp.where(u == 1.0, e, jnp.log(u) * e / jnp.where(u == 1.0, 1.0, u - 1.0))
    return jnp.maximum(z, 0.0) + l1p


def _matmul(a, b, *, ta=False, tb=False, tm, tn, tk, out_dtype=F32, add=None, dep=None, m_part=None, name):
    m, k = (a.shape[1], a.shape[0]) if ta else a.shape
    m_off = 0
    if m_part is not None:
        assert add is None and m % (m_part[1] * tm) == 0
        m //= m_part[1]
        m_off = m_part[0] * (m // tm)
    n, kb = b.shape if tb else b.shape[::-1]
    assert kb == k
    assert m % tm == 0 and n % tn == 0 and k % tk == 0, (m, n, k, tm, tn, tk)
    nk = k // tk
    a_spec = pl.BlockSpec((tk, tm), lambda i, j, q: (q, i + m_off)) if ta \
        else pl.BlockSpec((tm, tk), lambda i, j, q: (i + m_off, q))
    b_spec = pl.BlockSpec((tn, tk), lambda i, j, q: (j, q)) if tb else pl.BlockSpec((tk, tn), lambda i, j, q: (q, j))
    o_spec = pl.BlockSpec((tm, tn), lambda i, j, q: (i, j))
    o_shape = (m, n)
    dims = (((0 if ta else 1,), (1 if tb else 0,)), ((), ()))
    has_add = add is not None
    has_dep = dep is not None

    def body(*refs):
        if has_dep:
            refs = refs[:-3] + refs[-2:]
        if has_add:
            a_ref, b_ref, add_ref, o_ref, acc_ref = refs
        else:
            a_ref, b_ref, o_ref, acc_ref = refs
        q = pl.program_id(2)
        part = lax.dot_general(a_ref[...], b_ref[...], dims, preferred_element_type=F32)

        def finish(acc):
            if has_add:
                acc = acc + add_ref[...]
            o_ref[...] = acc.astype(out_dtype)

        if nk == 1:
            finish(part)
        else:
            @pl.when(q == 0)
            def _():
                acc_ref[...] = part

            @pl.when(jnp.logical_and(q > 0, q < nk - 1))
            def _():
                acc_ref[...] += part

            @pl.when(q == nk - 1)
            def _():
                finish(acc_ref[...] + part)

    in_specs = [a_spec, b_spec] + ([o_spec] if has_add else [])
    args = (a, b) + ((add,) if has_add else ())
    if has_dep:
        in_specs.append(pl.BlockSpec((SUBLANES, LANES), lambda i, j, q: (0, 0)))
        args += (dep,)
    acc_shape = (tm, tn) if nk > 1 else (SUBLANES, LANES)
    return pl.pallas_call(
        body, name=name,
        grid=(m // tm, n // tn, nk),
        in_specs=in_specs, out_specs=o_spec,
        out_shape=jax.ShapeDtypeStruct(o_shape, out_dtype),
        scratch_shapes=[pltpu.VMEM(acc_shape, F32)],
        compiler_params=_params(("parallel", "parallel", "arbitrary")),
    )(*args)


def _rms_fwd(h, g, *, name):
    tp, d = h.shape
    tr = _tile(tp, 512, SUBLANES)

    def body(h_ref, g_ref, o_ref):
        hv = h_ref[...]
        rstd = lax.rsqrt(jnp.mean(hv * hv, axis=-1, keepdims=True) + RMS_EPS)
        o_ref[...] = (hv * rstd * g_ref[...]).astype(BF16)

    return pl.pallas_call(
        body, name=name, grid=(tp // tr,),
        in_specs=[pl.BlockSpec((tr, d), lambda i: (i, 0)), pl.BlockSpec((1, d), lambda i: (0, 0))],
        out_specs=pl.BlockSpec((tr, d), lambda i: (i, 0)),
        out_shape=jax.ShapeDtypeStruct((tp, d), BF16),
        compiler_params=_params(("parallel",)),
    )(h, g.reshape(1, d))


def _rms_fwd_first(x, meta, g, *, tp, name):
    seq, d = x.shape
    n_meta = meta.shape[0]
    n_tok = n_meta + seq
    tr = SHIFTED_ROWS
    assert tp % tr == 0 and seq % tr == 0 and tr % n_meta == 0
    per = tr // n_meta

    def body(x_ref, xp_ref, m_ref, g_ref, h_ref, o_ref):
        i = pl.program_id(0)
        head = jnp.where(i == 0, m_ref[...], xp_ref[...])
        rows = i * tr + lax.broadcasted_iota(jnp.int32, (tr, 1), 0)
        hv = jnp.where(rows < n_tok, jnp.concatenate([head, x_ref[:tr - n_meta, :]], axis=0), 0.0)
        h_ref[...] = hv
        rstd = lax.rsqrt(jnp.mean(hv * hv, axis=-1, keepdims=True) + RMS_EPS)
        o_ref[...] = (hv * rstd * g_ref[...]).astype(BF16)

    row = pl.BlockSpec((tr, d), lambda i: (i, 0))
    own = pl.BlockSpec((tr, d), lambda i: (jnp.minimum(i, seq // tr - 1), 0))
    before = pl.BlockSpec((n_meta, d), lambda i: (jnp.maximum(i * per - 1, 0), 0))
    return pl.pallas_call(
        body, name=name, grid=(tp // tr,),
        in_specs=[own, before, pl.BlockSpec((n_meta, d), lambda i: (0, 0)), pl.BlockSpec((1, d), lambda i: (0, 0))],
        out_specs=[row, row],
        out_shape=[jax.ShapeDtypeStruct((tp, d), F32), jax.ShapeDtypeStruct((tp, d), BF16)],
        compiler_params=_params(("parallel",)),
    )(x, x, meta, g.reshape(1, d))


def _rms_bwd(h, dhn, dout, g, *, name):
    tp, d = h.shape
    tr = _tile(tp, 528, 2 * SUBLANES)

    def body(h_ref, dhn_ref, dout_ref, g_ref, dh_ref, dhb_ref, dg_ref):
        hv = h_ref[...]
        rstd = lax.rsqrt(jnp.mean(hv * hv, axis=-1, keepdims=True) + RMS_EPS)
        xhat = hv * rstd
        dn = dhn_ref[...]
        dxhat = dn * g_ref[...]
        dh = dout_ref[...] + rstd * (dxhat - xhat * jnp.mean(dxhat * xhat, axis=-1, keepdims=True))
        dh_ref[...] = dh
        dhb_ref[...] = dh.astype(BF16)
        part = jnp.sum(dn * xhat, axis=0, keepdims=True)

        @pl.when(pl.program_id(0) == 0)
        def _():
            dg_ref[...] = part

        @pl.when(pl.program_id(0) > 0)
        def _():
            dg_ref[...] += part

    row = pl.BlockSpec((tr, d), lambda i: (i, 0))
    vec = pl.BlockSpec((1, d), lambda i: (0, 0))
    return pl.pallas_call(
        body, name=name, grid=(tp // tr,),
        in_specs=[row, row, row, vec],
        out_specs=[row, row, vec],
        out_shape=[jax.ShapeDtypeStruct((tp, d), F32), jax.ShapeDtypeStruct((tp, d), BF16),
                   jax.ShapeDtypeStruct((1, d), F32)],
        compiler_params=_params(("arbitrary",)),
    )(h, dhn, dout, g.reshape(1, d))


def _rms_bwd_first(h, dhn, dout, g, *, n_meta, seq, name):
    tp, d = h.shape
    tr = SHIFTED_ROWS
    assert seq % tr == 0 and tr % n_meta == 0 and tp >= seq + n_meta
    nt = seq // tr
    per = tr // n_meta

    def grads(hv, dn, do, gv):
        rstd = lax.rsqrt(jnp.mean(hv * hv, axis=-1, keepdims=True) + RMS_EPS)
        xhat = hv * rstd
        dxhat = dn * gv
        dh = do + rstd * (dxhat - xhat * jnp.mean(dxhat * xhat, axis=-1, keepdims=True))
        return dh, jnp.sum(dn * xhat, axis=0, keepdims=True)

    def body(h_ref, dhn_ref, dout_ref, hn_ref, dhnn_ref, doutn_ref, g_ref, gx_ref, dmeta_ref, dg_ref):
        i = pl.program_id(0)
        gv = g_ref[...]
        dh, part = grads(h_ref[...], dhn_ref[...], dout_ref[...], gv)
        dh_next, part_next = grads(hn_ref[...], dhnn_ref[...], doutn_ref[...], gv)
        gx_ref[...] = jnp.concatenate([dh[n_meta:], dh_next], axis=0)

        @pl.when(i == 0)
        def _():
            dmeta_ref[...] = dh[:n_meta]
            dg_ref[...] = part

        @pl.when(i > 0)
        def _():
            dg_ref[...] += part

        @pl.when(i == nt - 1)
        def _():
            dg_ref[...] += part_next

    row = pl.BlockSpec((tr, d), lambda i: (i, 0))
    nxt = pl.BlockSpec((n_meta, d), lambda i: ((i + 1) * per, 0))
    vec = pl.BlockSpec((1, d), lambda i: (0, 0))
    return pl.pallas_call(
        body, name=name, grid=(nt,),
        in_specs=[row, row, row, nxt, nxt, nxt, vec],
        out_specs=[pl.BlockSpec((None, tr, d), lambda i: (0, i, 0)), pl.BlockSpec((n_meta, d), lambda i: (0, 0)), vec],
        out_shape=[jax.ShapeDtypeStruct((1, seq, d), F32), jax.ShapeDtypeStruct((n_meta, d), F32),
                   jax.ShapeDtypeStruct((1, d), F32)],
        compiler_params=_params(("arbitrary",)),
    )(h, dhn, dout, h, dhn, dout, g.reshape(1, d))


def _loss_head(h, tgt, g, *, n_meta, n_tok, name):
    tp, d = h.shape
    seq = tgt.shape[0]
    tr = SHIFTED_ROWS
    assert tp % tr == 0 and seq % tr == 0 and tr % n_meta == 0
    per = tr // n_meta

    def body(h_ref, t_ref, tp_ref, g_ref, dh_ref, dhb_ref, dg_ref, loss_ref):
        i = pl.program_id(0)
        hv = h_ref[...]
        rstd = lax.rsqrt(jnp.mean(hv * hv, axis=-1, keepdims=True) + RMS_EPS)
        xhat = hv * rstd
        gv = g_ref[...]
        rows = i * tr + lax.broadcasted_iota(jnp.int32, (tr, 1), 0)
        valid = jnp.logical_and(rows >= n_meta, rows < n_tok)
        target = jnp.concatenate([tp_ref[...], t_ref[:tr - n_meta, :]], axis=0)
        err = jnp.where(valid, xhat * gv - target, 0.0)
        dy = err * (1.0 / d)
        dxhat = dy * gv
        dh = rstd * (dxhat - xhat * jnp.mean(dxhat * xhat, axis=-1, keepdims=True))
        dh_ref[...] = dh
        dhb_ref[...] = dh.astype(BF16)
        dg_part = jnp.sum(dy * xhat, axis=0, keepdims=True)
        per_row = jnp.sum(err * err, axis=-1, keepdims=True) * (1.0 / d)
        loss_part = jnp.broadcast_to(0.5 * jnp.sum(per_row, axis=0, keepdims=True), (SUBLANES, LANES))

        @pl.when(i == 0)
        def _():
            dg_ref[...] = dg_part
            loss_ref[...] = loss_part

        @pl.when(i > 0)
        def _():
            dg_ref[...] += dg_part
            loss_ref[...] += loss_part

    row = pl.BlockSpec((tr, d), lambda i: (i, 0))
    vec = pl.BlockSpec((1, d), lambda i: (0, 0))
    own = pl.BlockSpec((tr, d), lambda i: (jnp.minimum(i, seq // tr - 1), 0))
    before = pl.BlockSpec((n_meta, d), lambda i: (jnp.maximum(i * per - 1, 0), 0))
    return pl.pallas_call(
        body, name=name, grid=(tp // tr,),
        in_specs=[row, own, before, vec],
        out_specs=[row, row, vec, pl.BlockSpec((SUBLANES, LANES), lambda i: (0, 0))],
        out_shape=[jax.ShapeDtypeStruct((tp, d), F32), jax.ShapeDtypeStruct((tp, d), BF16),
                   jax.ShapeDtypeStruct((1, d), F32), jax.ShapeDtypeStruct((SUBLANES, LANES), F32)],
        compiler_params=_params(("arbitrary",)),
    )(h, tgt, tgt, g.reshape(1, d))


def _shift_down(halo, tile, s):
    if s == 0:
        return tile
    ext = jnp.concatenate([halo, tile], axis=0)
    return pltpu.roll(ext, s, 0)[SUBLANES:]


def _shift_up(tile, head, s):
    if s == 0:
        return tile
    ext = jnp.concatenate([tile, head], axis=0)
    n = ext.shape[0]
    return pltpu.roll(ext, n - s, 0)[: tile.shape[0]]


def _to_lane_blocks(ref, cols, val):
    for j in range(cols.start // LANES, cols.stop // LANES):
        ref[j] = val[:, j * LANES - cols.start:(j + 1) * LANES - cols.start]


def _from_lane_blocks(ref, cols):
    return jnp.concatenate([ref[j] for j in range(cols.start // LANES, cols.stop // LANES)], axis=1)


def _scan_tile(a_ref, b_ref, out_ref, carry, j, *, reverse):
    ng = a_ref.shape[1] // SUBLANES
    order = list(range(SUBLANES))[::-1] if reverse else list(range(SUBLANES))

    def rows(r):
        return pl.ds(r, ng, stride=SUBLANES)

    prod, loc = {}, {}
    prev = None
    for r in order:
        ar = a_ref[j, rows(r), :]
        br = b_ref[j, rows(r), :]
        prod[r] = ar if prev is None else ar * prod[prev]
        loc[r] = br if prev is None else ar * loc[prev] + br
        prev = r
    pg, lg = prod[prev], loc[prev]
    ones = jnp.ones((SUBLANES,) + pg.shape[1:], F32)
    zeros = jnp.zeros_like(ones)
    s = 1
    while s < ng:
        p_sh = _shift_up(pg, ones, s) if reverse else _shift_down(ones, pg, s)
        l_sh = _shift_up(lg, zeros, s) if reverse else _shift_down(zeros, lg, s)
        lg = pg * l_sh + lg
        pg = pg * p_sh
        s *= 2
    leaving = pg * carry[0:1, :] + lg
    entering = _shift_up(leaving, carry, 1) if reverse else _shift_down(carry, leaving, 1)
    for r in order:
        out_ref[j, rows(r), :] = loc[r] + prod[r] * entering
    last = leaving[0:1, :] if reverse else leaving[ng - 1:ng, :]
    return jnp.broadcast_to(last, carry.shape)


def _gates(ca, wr, wi, br, bi, sp):
    cab = ca.astype(BF16)
    r = _sigmoid(jnp.dot(cab, wr, preferred_element_type=F32) + br)
    ig = _sigmoid(jnp.dot(cab, wi, preferred_element_type=F32) + bi)
    la = -LRU_C * r * sp
    a = jnp.exp(la)
    mult = jnp.sqrt(-jnp.tanh(la) * (a * a + 1.0))
    return r, ig, a, mult


U_STREAMS = 3


class _ColumnParts:
    def __init__(self, refs):
        self.refs = refs
        self.width = refs[0].shape[-1]

    def __getitem__(self, idx):
        rows, cols = idx
        k = cols.start // self.width
        assert cols.stop <= (k + 1) * self.width
        return self.refs[k][rows, cols.start - k * self.width:cols.stop - k * self.width]


def _mixer_fwd(u, wa, ba, wr_blk, br, wi_blk, bi, lam, wb, *, name):
    tp, din = u.shape
    dl = din // 6
    tt = MIX_ROWS
    cw = GATE_BLOCK
    nch = dl // cw
    assert tp % tt == 0 and dl % cw == 0

    def body(*refs):
        u_ref = _ColumnParts(refs[:U_STREAMS])
        (wa_ref, ba_ref, wr_ref, br_ref, wi_ref, bi_ref, lam_ref, wb_ref,
         s_ref, y_ref, xa_tail, v_tail, h_carry, a_s, b_s, h_s) = refs[U_STREAMS:]

        @pl.when(pl.program_id(0) == 0)
        def _():
            xa_tail[...] = jnp.zeros_like(xa_tail)
            v_tail[...] = jnp.zeros_like(v_tail)
            h_carry[...] = jnp.zeros_like(h_carry)

        for ch in range(nch):
            cs = slice(ch * cw, (ch + 1) * cw)

            def seg(s):
                return slice(s * dl + ch * cw, s * dl + (ch + 1) * cw)

            xa = u_ref[:, seg(0)]
            halo = xa_tail[:, cs]
            ca = ba_ref[:, cs] + wa_ref[3:4, cs] * xa
            for kk in range(3):
                ca = ca + wa_ref[kk:kk + 1, cs] * _shift_down(halo, xa, 3 - kk)
            xa_tail[:, cs] = xa[tt - SUBLANES:]
            s_ref[:, cs] = ca
            sp = _softplus(-lam_ref[:, cs])
            _, ig, a, mult = _gates(ca, wr_ref[ch], wi_ref[ch], br_ref[:, cs], bi_ref[:, cs], sp)
            _to_lane_blocks(a_s, cs, a)
            _to_lane_blocks(b_s, cs, mult * (ig * ca))

            bv = u_ref[:, seg(2)]
            v = u_ref[:, seg(3)] * u_ref[:, seg(4)]
            gb = u_ref[:, seg(5)]
            vh = v_tail[:, cs]
            cb = wb_ref[2:3, cs] * v
            for kk in range(2):
                cb = cb + wb_ref[kk:kk + 1, cs] * _shift_down(vh, v, 2 - kk)
            v_tail[:, cs] = v[tt - SUBLANES:]
            y_ref[:, dl + ch * cw: dl + (ch + 1) * cw] = (bv * cb * (gb * _sigmoid(gb))).astype(BF16)

        for ch in range(nch):
            cs = slice(ch * cw, (ch + 1) * cw)
            for j in range(cs.start // LANES, cs.stop // LANES):
                lanes = slice(j * LANES, (j + 1) * LANES)
                h_carry[:, lanes] = _scan_tile(a_s, b_s, h_s, h_carry[:, lanes], j, reverse=False)
            hsv = _from_lane_blocks(h_s, cs)
            s_ref[:, dl + ch * cw: dl + (ch + 1) * cw] = hsv
            ga = u_ref[:, dl + ch * cw: dl + (ch + 1) * cw]
            y_ref[:, cs] = (hsv * (ga * _sigmoid(ga))).astype(BF16)

    row = lambda w: pl.BlockSpec((tt, w), lambda i: (i, 0))
    full = lambda shp: pl.BlockSpec(shp, lambda i: tuple(0 for _ in shp))
    return pl.pallas_call(
        body, name=name, grid=(tp // tt,),
        in_specs=[pl.BlockSpec((tt, din // U_STREAMS), functools.partial(lambda k, i: (i, k), k))
                  for k in range(U_STREAMS)]
        + [full((4, dl)), full((1, dl)), full((nch, cw, cw)), full((1, dl)),
           full((nch, cw, cw)), full((1, dl)), full((1, dl)), full((3, dl))],
        out_specs=[row(2 * dl), row(2 * dl)],
        out_shape=[jax.ShapeDtypeStruct((tp, 2 * dl), F32), jax.ShapeDtypeStruct((tp, 2 * dl), BF16)],
        scratch_shapes=[pltpu.VMEM((SUBLANES, dl), F32), pltpu.VMEM((SUBLANES, dl), F32),
                        pltpu.VMEM((SUBLANES, dl), F32)] + [pltpu.VMEM((dl // LANES, tt, LANES), F32)] * 3,
        compiler_params=_params(("arbitrary",)),
    )(*[u] * U_STREAMS, wa, ba, wr_blk, br, wi_blk, bi, lam, wb)


SG_WA, SG_BA, SG_BR, SG_BI, SG_LAM, SG_WB, SG_ROWS = 0, 4, 5, 6, 7, 8, 16


def _mixer_bwd(u, saved, dy, wa, wr_blk, br, wi_blk, bi, lam, wb, *, name):
    tp, din = u.shape
    dl = din // 6
    tt = MIX_ROWS
    cw = GATE_BLOCK
    nch = dl // cw
    nt = tp // tt
    hb = tt // SUBLANES
    tn_dims = (((0,), (0,)), ((), ()))
    nt_dims = (((1,), (1,)), ((), ()))

    def body(*refs):
        u_ref = _ColumnParts(refs[:U_STREAMS])
        (uh_ref, s_ref, sh_ref, dy_ref, wa_ref, wr_ref, br_ref, wi_ref, bi_ref, lam_ref, wb_ref,
         du_ref, sg_ref, dwr_ref, dwi_ref,
         g_carry, a_head, dca_head, dcb_head, r_s, i_s, a_s, an_s, d_s, g_s) = refs[U_STREAMS:]
        i = pl.program_id(0)
        first_tile = i == nt - 1

        @pl.when(i == 0)
        def _():
            for ref in (g_carry, a_head, dca_head, dcb_head, sg_ref, dwr_ref, dwi_ref):
                ref[...] = jnp.zeros_like(ref)

        def halo_of(x):
            return jnp.where(first_tile, 0.0, x)

        for ch in range(nch):
            cs = slice(ch * cw, (ch + 1) * cw)
            cav = s_ref[:, cs]
            sp = _softplus(-lam_ref[:, cs])
            r, ig, a, _ = _gates(cav, wr_ref[ch], wi_ref[ch], br_ref[:, cs], bi_ref[:, cs], sp)
            r_s[:, cs] = r
            i_s[:, cs] = ig
            a_s[:, cs] = a
            _to_lane_blocks(an_s, cs, _shift_up(a, a_head[:, cs], 1))
            a_head[:, cs] = a[:SUBLANES]
            ga = u_ref[:, dl + ch * cw: dl + (ch + 1) * cw]
            _to_lane_blocks(d_s, cs, dy_ref[:, cs] * (ga * _sigmoid(ga)))

        for j in range(dl // LANES):
            lanes = slice(j * LANES, (j + 1) * LANES)
            g_carry[:, lanes] = _scan_tile(an_s, d_s, g_s, g_carry[:, lanes], j, reverse=True)

        def acc_row(r0, val):
            sg_ref[r0:r0 + 1, cs_cur[0]] += jnp.sum(val, axis=0, keepdims=True)

        cs_cur = [None]
        for ch in range(nch):
            cs = slice(ch * cw, (ch + 1) * cw)
            cs_cur[0] = cs

            def seg(s):
                return slice(s * dl + ch * cw, s * dl + (ch + 1) * cw)

            cav = s_ref[:, cs]
            r = r_s[:, cs]
            ig = i_s[:, cs]
            a = a_s[:, cs]
            g = _from_lane_blocks(g_s, cs)
            hsv = s_ref[:, dl + ch * cw: dl + (ch + 1) * cw]
            lamv = lam_ref[:, cs]
            sp = _softplus(-lamv)
            la = -LRU_C * r * sp
            e2 = a * a
            one_m_e2 = -jnp.tanh(la) * (e2 + 1.0)
            mult = jnp.sqrt(one_m_e2)
            hprev = _shift_down(halo_of(sh_ref[:, dl + ch * cw: dl + (ch + 1) * cw]), hsv, 1)
            icav = ig * cav
            dla = g * (hprev * a - icav * (e2 * lax.rsqrt(one_m_e2)))
            gm = g * mult
            dzi = gm * icav * (1.0 - ig)
            dca = gm * ig
            dla_r = dla * r
            dzr = dla_r * (1.0 - r) * (-LRU_C * sp)
            sg_ref[SG_LAM:SG_LAM + 1, cs] += jnp.sum(dla_r, axis=0, keepdims=True) * (LRU_C * _sigmoid(-lamv))
            acc_row(SG_BR, dzr)
            acc_row(SG_BI, dzi)
            dzr_b = dzr.astype(BF16)
            dzi_b = dzi.astype(BF16)
            cab = cav.astype(BF16)
            dca = dca + lax.dot_general(dzr_b, wr_ref[ch], nt_dims, preferred_element_type=F32)
            dca = dca + lax.dot_general(dzi_b, wi_ref[ch], nt_dims, preferred_element_type=F32)
            dwr_ref[ch] += lax.dot_general(cab, dzr_b, tn_dims, preferred_element_type=F32)
            dwi_ref[ch] += lax.dot_general(cab, dzi_b, tn_dims, preferred_element_type=F32)
            acc_row(SG_BA, dca)
            xa = u_ref[:, seg(0)]
            head = dca_head[:, cs]
            dxa = wa_ref[3:4, cs] * dca
            acc_row(SG_WA + 3, dca * xa)
            for kk in range(3):
                later = _shift_up(dca, head, 3 - kk)
                acc_row(SG_WA + kk, later * xa)
                dxa = dxa + wa_ref[kk:kk + 1, cs] * later
            dca_head[:, cs] = dca[:SUBLANES]
            ga = u_ref[:, seg(1)]
            sga = _sigmoid(ga)
            dga = dy_ref[:, cs] * hsv * (sga + (ga * sga) * (1.0 - sga))
            du_ref[:, seg(0)] = dxa.astype(BF16)
            du_ref[:, seg(1)] = dga.astype(BF16)

            bv = u_ref[:, seg(2)]
            cv = u_ref[:, seg(3)]
            xb = u_ref[:, seg(4)]
            gb = u_ref[:, seg(5)]
            dyb = dy_ref[:, dl + ch * cw: dl + (ch + 1) * cw]
            v = cv * xb
            vh = halo_of(uh_ref[:, seg(3)] * uh_ref[:, seg(4)])
            v1 = _shift_down(vh, v, 1)
            v2 = _shift_down(vh, v, 2)
            cb = wb_ref[2:3, cs] * v + wb_ref[1:2, cs] * v1 + wb_ref[0:1, cs] * v2
            sgb = _sigmoid(gb)
            sl = gb * sgb
            dyb_b = dyb * bv
            dyb_cb = dyb * cb
            dcb = dyb_b * sl
            du_ref[:, seg(2)] = (dyb_cb * sl).astype(BF16)
            du_ref[:, seg(5)] = (dyb_cb * bv * (sgb + sl * (1.0 - sgb))).astype(BF16)
            bhead = dcb_head[:, cs]
            dv = wb_ref[2:3, cs] * dcb
            acc_row(SG_WB + 2, dcb * v)
            for kk in range(2):
                later = _shift_up(dcb, bhead, 2 - kk)
                acc_row(SG_WB + kk, later * v)
                dv = dv + wb_ref[kk:kk + 1, cs] * later
            dcb_head[:, cs] = dcb[:SUBLANES]
            du_ref[:, seg(3)] = (dv * xb).astype(BF16)
            du_ref[:, seg(4)] = (dv * cv).astype(BF16)

    rev = lambda w: pl.BlockSpec((tt, w), lambda i: (nt - 1 - i, 0))
    halo = lambda w: pl.BlockSpec((SUBLANES, w), lambda i: (jnp.maximum((nt - 1 - i) * hb - 1, 0), 0))
    full = lambda shp: pl.BlockSpec(shp, lambda i: tuple(0 for _ in shp))
    vm = lambda r: pltpu.VMEM((r, dl), F32)
    return pl.pallas_call(
        body, name=name, grid=(nt,),
        in_specs=[pl.BlockSpec((tt, din // U_STREAMS), functools.partial(lambda k, i: (nt - 1 - i, k), k))
                  for k in range(U_STREAMS)]
        + [halo(din), rev(2 * dl), halo(2 * dl), rev(2 * dl), full((4, dl)),
           full((nch, cw, cw)), full((1, dl)), full((nch, cw, cw)), full((1, dl)), full((1, dl)), full((3, dl))],
        out_specs=[rev(din), full((SG_ROWS, dl)), full((nch, cw, cw)), full((nch, cw, cw))],
        out_shape=[jax.ShapeDtypeStruct((tp, din), BF16), jax.ShapeDtypeStruct((SG_ROWS, dl), F32),
                   jax.ShapeDtypeStruct((nch, cw, cw), F32), jax.ShapeDtypeStruct((nch, cw, cw), F32)],
        scratch_shapes=[vm(SUBLANES), vm(SUBLANES), vm(SUBLANES), vm(SUBLANES), vm(tt), vm(tt), vm(tt)]
        + [pltpu.VMEM((dl // LANES, tt, LANES), F32)] * 3,
        compiler_params=_params(("arbitrary",)),
    )(*[u] * (U_STREAMS + 1), saved, saved, dy, wa, wr_blk, br, wi_blk, bi, lam, wb)


def _adamw(w, g, m, v, *, name, landed=None, layer=None, depth=None, into=None, row_off=0):
    r, c = w.shape[-2:]
    rows = g.shape[0]
    tr = _tile(rows, 512, 2 * SUBLANES)
    assert row_off % tr == 0
    boff = row_off // tr
    bc1 = 1.0 - ADAM_B1 ** ADAM_STEP
    bc2 = 1.0 - ADAM_B2 ** ADAM_STEP
    slots = landed is not None

    def body(*refs):
        if into is not None:
            refs = refs[:-8] + refs[-4:]
        if slots:
            w_ref, g_ref, l_ref, m_ref, v_ref, grad_ref, delta_ref, nm_ref, nv_ref = refs
            gv = g_ref[...].astype(F32)
            for s in range(N_DEV - 1):
                gv = gv + l_ref[s].astype(F32)
        else:
            w_ref, g_ref, m_ref, v_ref, grad_ref, delta_ref, nm_ref, nv_ref = refs
            gv = g_ref[...]
        wv = w_ref[...]
        mn = ADAM_B1 * m_ref[...] + (1.0 - ADAM_B1) * gv
        vn = ADAM_B2 * v_ref[...] + (1.0 - ADAM_B2) * (gv * gv)
        m_hat = mn / bc1
        v_hat = vn / bc2
        grad_ref[...] = gv
        delta_ref[...] = -ADAM_LR * (m_hat / (jnp.sqrt(v_hat) + ADAM_EPS) + ADAM_WD * wv)
        nm_ref[...] = mn
        nv_ref[...] = vn

    if depth is None:
        blk = pl.BlockSpec((tr, c), lambda i: (i + boff, 0))
    else:
        blk = pl.BlockSpec((None, tr, c), lambda i: (layer, i + boff, 0))
    g_blk = pl.BlockSpec((tr, c), lambda i: (i, 0))
    l_spec = [pl.BlockSpec((N_DEV - 1, tr, c), lambda i: (0, i, 0))] if slots else []
    args = (w, g, landed, m, v) if slots else (w, g, m, v)
    in_specs = [blk, g_blk] + l_spec + [blk, blk]
    if depth is None:
        shp = jax.ShapeDtypeStruct((r, c), F32)
        out_blk = blk
    else:
        shp = jax.ShapeDtypeStruct((depth, r, c), F32)
        out_blk = pl.BlockSpec((None, tr, c), lambda i: (layer, i + boff, 0))
    aliases = {}
    if into is not None:
        aliases = {len(args) + j: j for j in range(4)}
        in_specs = in_specs + [ANY] * 4
        args = args + tuple(into)
    return pl.pallas_call(
        body, name=name, grid=(rows // tr,),
        in_specs=in_specs, out_specs=[out_blk] * 4,
        out_shape=[shp] * 4, input_output_aliases=aliases,
        compiler_params=_params(("parallel",)),
    )(*args)


def _slot_sum(g, *, name):
    _, r, c = g.shape
    tr = _tile(r, 512, SUBLANES)

    def body(g_ref, o_ref):
        gv = g_ref[0].astype(F32)
        for s in range(1, N_DEV):
            gv = gv + g_ref[s].astype(F32)
        o_ref[...] = gv

    return pl.pallas_call(
        body, name=name, grid=(r // tr,),
        in_specs=[pl.BlockSpec((N_DEV, tr, c), lambda i: (0, i, 0))],
        out_specs=pl.BlockSpec((tr, c), lambda i: (i, 0)),
        out_shape=jax.ShapeDtypeStruct((r, c), F32),
        compiler_params=_params(("parallel",)),
    )(g)


def _mesh_pos():
    x, y, c = lax.axis_index("x"), lax.axis_index("y"), lax.axis_index("c")
    return x, y, c, 4 * x + 2 * y + c


ANY = pl.BlockSpec(memory_space=pl.ANY)


GATHER_COPIES = 9


def _all_gather(srcs, out_shapes, views, *, name):
    n = len(srcs)
    SIB, X_OWN, Y_OWN, X_DIAG, Y_DIAG, SIB_X, SIB_Y, SIB_DIAG_TOP, SIB_DIAG_BOTTOM = range(GATHER_COPIES)

    def body(*refs):
        src = refs[:n]
        dst = refs[n:2 * n]
        send_sems, recv_sems, local_sems = refs[2 * n:]
        x, y, c, me = _mesh_pos()
        sibling, x_nbr, y_nbr = (x, y, 1 - c), (1 - x, y, c), (x, 1 - y, c)

        def block(a, px, py, pc, half=None):
            win = views[a](dst[a], 4 * px + 2 * py + pc)
            if half is None:
                return win
            rows = win.shape[0] // 2
            return win.at[pl.ds(half * rows, rows)]

        def copy(a, k, win, to, from_src=False):
            return pltpu.make_async_remote_copy(
                src_ref=src[a] if from_src else win, dst_ref=win,
                send_sem=send_sems.at[a * GATHER_COPIES + k], recv_sem=recv_sems.at[a * GATHER_COPIES + k],
                device_id=to, device_id_type=MESH)

        mine = [pltpu.make_async_copy(src[a], block(a, x, y, c), local_sems.at[a]) for a in range(n)]
        started = []

        def start(cp):
            cp.start()
            started.append(cp)

        for a in range(n):
            mine[a].start()
            own = block(a, x, y, c)
            start(copy(a, SIB, own, sibling, True))
            start(copy(a, X_OWN, own, x_nbr, True))
            start(copy(a, Y_OWN, own, y_nbr, True))
        for a in range(n):
            from_y = block(a, x, 1 - y, c)
            copy(a, Y_OWN, from_y, y_nbr).wait_recv()
            start(copy(a, X_DIAG, block(a, x, 1 - y, c, 0), x_nbr))
            start(copy(a, SIB_Y, from_y, sibling))
            from_x = block(a, 1 - x, y, c)
            copy(a, X_OWN, from_x, x_nbr).wait_recv()
            start(copy(a, Y_DIAG, block(a, 1 - x, y, c, 1), y_nbr))
            start(copy(a, SIB_X, from_x, sibling))
        for a in range(n):
            top = block(a, 1 - x, 1 - y, c, 0)
            copy(a, X_DIAG, top, x_nbr).wait_recv()
            start(copy(a, SIB_DIAG_TOP, top, sibling))
            bottom = block(a, 1 - x, 1 - y, c, 1)
            copy(a, Y_DIAG, bottom, y_nbr).wait_recv()
            start(copy(a, SIB_DIAG_BOTTOM, bottom, sibling))
        for a in range(n):
            copy(a, SIB, block(a, x, y, 1 - c), sibling).wait_recv()
            copy(a, SIB_X, block(a, 1 - x, y, 1 - c), sibling).wait_recv()
            copy(a, SIB_Y, block(a, x, 1 - y, 1 - c), sibling).wait_recv()
            copy(a, SIB_DIAG_TOP, block(a, 1 - x, 1 - y, 1 - c, 0), sibling).wait_recv()
            copy(a, SIB_DIAG_BOTTOM, block(a, 1 - x, 1 - y, 1 - c, 1), sibling).wait_recv()
        for cp in started:
            cp.wait_send()
        for cp in mine:
            cp.wait()

    return pl.pallas_call(
        body, name=name,
        in_specs=[ANY] * n, out_specs=[ANY] * n,
        out_shape=[jax.ShapeDtypeStruct(s, x.dtype) for s, x in zip(out_shapes, srcs)],
        scratch_shapes=[pltpu.SemaphoreType.DMA((GATHER_COPIES * n,)), pltpu.SemaphoreType.DMA((GATHER_COPIES * n,)),
                        pltpu.SemaphoreType.DMA((n,))],
    )(*srcs)


HBM = pl.BlockSpec(memory_space=pltpu.HBM)
SEM = pl.BlockSpec(memory_space=pltpu.SEMAPHORE)
EFFECT = pltpu.SideEffectType.DATAFLOW_SIDE_EFFECTING


def _peer_of(x, y, c, k):
    return (1 - x if k & 4 else x, 1 - y if k & 2 else y, 1 - c if k & 1 else c)


def _peer_copies(n, wins, src, land, send_sems, recv_sems):
    x, y, c, me = _mesh_pos()
    out = []
    for a in range(n):
        for k in range(1, N_DEV):
            px, py, pc = _peer_of(x, y, c, k)
            s_win, d_win = wins[a](src[a], land[a], me, 4 * px + 2 * py + pc, k)
            out.append(pltpu.make_async_remote_copy(
                src_ref=s_win, dst_ref=d_win,
                send_sem=send_sems.at[a * 7 + k - 1], recv_sem=recv_sems.at[a * 7 + k - 1],
                device_id=(px, py, pc), device_id_type=MESH))
    return out


def _push_start(srcs, lands, wins, *, name):
    n = len(srcs)

    def body(*refs):
        src = refs[:n]
        land = refs[n:2 * n]
        send_sems, recv_sems = refs[2 * n], refs[2 * n + 1]
        token = refs[-1]
        for cp in _peer_copies(n, wins, src, land, send_sems, recv_sems):
            cp.start()
        token[...] = jnp.zeros_like(token)

    bufs = (*srcs, *lands)
    return pl.pallas_call(
        body, name=name,
        out_shape=(pltpu.SemaphoreType.DMA((7 * n,)), pltpu.SemaphoreType.DMA((7 * n,)),
                   *[pltpu.HBM(v.shape, v.dtype) for v in bufs], jax.ShapeDtypeStruct((SUBLANES, LANES), F32)),
        in_specs=[HBM] * (2 * n),
        out_specs=(SEM, SEM, *[HBM] * (2 * n), pl.BlockSpec(memory_space=pltpu.VMEM)),
        input_output_aliases={i: 2 + i for i in range(2 * n)},
        compiler_params=pltpu.CompilerParams(has_side_effects=EFFECT),
    )(*[pltpu.with_memory_space_constraint(v, pltpu.HBM) for v in bufs])


def _push_wait(handle, wins, after, *, name):
    send_sems, recv_sems, *bufs, _ = handle
    n = len(bufs) // 2

    def body(*refs):
        src = refs[:n]
        land = refs[n:2 * n]
        for cp in _peer_copies(n, wins, src, land, refs[2 * n], refs[2 * n + 1]):
            cp.wait_send()
            cp.wait_recv()

    outs = pl.pallas_call(
        body, name=name,
        out_shape=tuple(pltpu.HBM(v.shape, v.dtype) for v in bufs),
        in_specs=[HBM] * (2 * n) + [SEM, SEM, ANY],
        out_specs=tuple([HBM] * (2 * n)),
        input_output_aliases={i: i for i in range(2 * n)},
        compiler_params=pltpu.CompilerParams(has_side_effects=EFFECT),
    )(*bufs, send_sems, recv_sems, after)
    return outs[:n], outs[n:]


def _gather_lead(src, land, me, peer, k):
    return src, land.at[me]


def _gather_cols(width):
    def win(src, land, me, peer, k):
        return src, land.at[:, pl.ds(me * width, width)]
    return win


def _scatter_lead(src, land, me, peer, k):
    return src.at[peer], land.at[k - 1]


def _scatter_cols(width):
    def win(src, land, me, peer, k):
        return src.at[:, pl.ds(peer * width, width)], land.at[k - 1]
    return win


def _place_block(own, *, cols, name):
    rows, width = own.shape
    tr = _tile(rows, 512, 2 * SUBLANES)
    _, _, _, me = _mesh_pos()

    def body(me_ref, x_ref, o_ref):
        o_ref[...] = x_ref[...]

    if cols:
        out_spec = pl.BlockSpec((tr, width), lambda i, me_ref: (i, me_ref[0]))
        shape = (rows, N_DEV * width)
    else:
        out_spec = pl.BlockSpec((None, tr, width), lambda i, me_ref: (me_ref[0], i, 0))
        shape = (N_DEV, rows, width)
    return pl.pallas_call(
        body, name=name,
        grid_spec=pltpu.PrefetchScalarGridSpec(
            num_scalar_prefetch=1, grid=(rows // tr,),
            in_specs=[pl.BlockSpec((tr, width), lambda i, me_ref: (i, 0))], out_specs=out_spec),
        out_shape=jax.ShapeDtypeStruct(shape, own.dtype),
        compiler_params=_params(("arbitrary",)),
    )(me.astype(jnp.int32).reshape(1), own)


def _dep(x, token):
    return x + token[0, 0].astype(x.dtype)


def _lead(ref, d):
    return ref.at[d]


def _col_window(width):
    def view(ref, d):
        return ref.at[:, pl.ds(d * width, width)]
    return view


def _pack(arrs):
    flat = jnp.concatenate([a.reshape(-1).astype(F32) for a in arrs])
    n = flat.shape[0]
    rows = -(-n // (2 * SUBLANES * LANES)) * 2 * SUBLANES
    return jnp.pad(flat, (0, rows * LANES - n)).reshape(rows, LANES)


def _unpack(buf, shapes):
    flat = buf.reshape(-1)
    out, off = [], 0
    for s in shapes:
        n = 1
        for q in s:
            n *= q
        out.append(flat[off:off + n].reshape(s))
        off += n
    return out


def _blockdiag(w, cw):
    h, hd, _ = w.shape
    per = cw // hd
    wg = w.reshape(h // per, per, hd, hd)
    eye = jnp.eye(per, dtype=w.dtype)
    blk = jnp.einsum("gpij,pq->gpiqj", wg, eye)
    return blk.reshape(h // per, cw, cw).astype(BF16)


def _blockdiag_extract(g, hd):
    n, cw, _ = g.shape
    per = cw // hd
    g5 = g.reshape(n, per, hd, per, hd)
    idx = jnp.arange(per)
    return g5[:, idx, :, idx, :].transpose(1, 0, 2, 3).reshape(n * per, hd, hd)


def kernel(x, meta, norm_g, w_in, conv_a_w, conv_a_b, lru_wr, lru_br, lru_wi, lru_bi, lru_lambda, conv_b_w, w_out, final_g, loss_target, m_meta, m_norm_g, m_w_in, m_conv_a_w, m_conv_a_b, m_lru_wr, m_lru_br, m_lru_wi, m_lru_bi, m_lru_lambda, m_conv_b_w, m_w_out, m_final_g, v_meta, v_norm_g, v_w_in, v_conv_a_w, v_conv_a_b, v_lru_wr, v_lru_br, v_lru_wi, v_lru_bi, v_lru_lambda, v_conv_b_w, v_w_out, v_final_g):
    _, seq, d = x.shape
    n_meta = meta.shape[0]
    depth = w_in.shape[0]
    din = w_in.shape[2] * N_DEV
    dl = din // 6
    dmix = 2 * dl
    wcol = w_in.shape[2]
    wrow = w_out.shape[1]
    mcol = meta.shape[1]
    ccol = conv_a_w.shape[2]
    heads, hd = lru_wr.shape[1], lru_wr.shape[2]
    n_tok = n_meta + seq
    tp = -(-n_tok // TOKEN_TILE) * TOKEN_TILE
    me = 4 * lax.axis_index("x") + 2 * lax.axis_index("y") + lax.axis_index("c")

    bf = lambda a: a.astype(BF16)
    small_mine = _pack([meta, conv_a_w, conv_b_w])
    first = _all_gather([bf(w_in[0]), small_mine], [(d, din), (N_DEV,) + small_mine.shape],
                        [_col_window(wcol), _lead], name="gather_first")
    flat = first[1].reshape(N_DEV, -1)
    sizes = [meta.size, conv_a_w.size, conv_b_w.size]
    meta_full = jnp.moveaxis(flat[:, :sizes[0]].reshape(N_DEV, n_meta, mcol), 0, 1).reshape(n_meta, d)
    wa_full = jnp.moveaxis(flat[:, sizes[0]:sizes[0] + sizes[1]].reshape(N_DEV, depth, 4, ccol), 0, 2) \
        .reshape(depth, 4, dl)
    wb_full = jnp.moveaxis(flat[:, sizes[0] + sizes[1]:sum(sizes)].reshape(N_DEV, depth, 3, ccol), 0, 2) \
        .reshape(depth, 3, dl)
    w_in_full = [None] * depth
    w_out_full = [None] * depth

    push_out = [None] * depth
    push_in = [None] * depth
    w_in_full[0], src = lax.optimization_barrier((first[0], bf(w_out[0])))
    push_out[0] = _push_start([src], [_place_block(src, cols=False, name="place_wout_0")], [_gather_lead],
                              name="gather_wout_0_start")
    token = push_out[0][-1]
    for l in range(1, depth):
        src = bf(_dep(w_in[l], token))
        push_in[l] = _push_start([src], [_place_block(src, cols=True, name=f"place_win_{l}")], [_gather_cols(wcol)],
                                 name=f"gather_win_{l}_start")
        src = bf(_dep(w_out[l], push_in[l][-1]))
        push_out[l] = _push_start([src], [_place_block(src, cols=False, name=f"place_wout_{l}")], [_gather_lead],
                                  name=f"gather_wout_{l}_start")
        token = push_out[l][-1]

    wr_blk = [_blockdiag(lru_wr[l], GATE_BLOCK) for l in range(depth)]
    wi_blk = [_blockdiag(lru_wi[l], GATE_BLOCK) for l in range(depth)]
    vec = lambda a: a.reshape(1, dl)

    tm = _tile(tp, 1408)
    saved = []
    for l in range(depth):
        if l == 0:
            h, hn = _rms_fwd_first(x[0], meta_full, _dep(norm_g[l], token), tp=tp, name=f"rms_fwd_{l}")
        else:
            hn = _rms_fwd(h, norm_g[l], name=f"rms_fwd_{l}")
        if l > 0:
            _, landed = _push_wait(push_in[l], [_gather_cols(wcol)], hn, name=f"gather_win_{l}_wait")
            w_in_full[l] = landed[0]
        u = _matmul(hn, w_in_full[l], tm=tm, tn=_tile(din, 1536), tk=d, name=f"mm_u_{l}")
        mixed, y = _mixer_fwd(u, wa_full[l], vec(conv_a_b[l]), wr_blk[l], vec(lru_br[l]), wi_blk[l], vec(lru_bi[l]),
                              vec(lru_lambda[l]), wb_full[l], name=f"mixer_fwd_{l}")
        _, landed = _push_wait(push_out[l], [_gather_lead], y, name=f"gather_wout_{l}_wait")
        w_out_full[l] = landed[0].reshape(dmix, d)
        h_next = _matmul(y, w_out_full[l], tm=tm, tn=_tile(d, 512), tk=dmix, add=h, name=f"mm_out_{l}")
        saved.append((h, hn, u, mixed, y))
        h = h_next

    dh, dhb, dg_final, loss_part = _loss_head(h, loss_target[0], final_g, n_meta=n_meta, n_tok=n_tok,
                                              name="loss_head")

    small_grads = [None] * depth
    sent_out = [None] * depth
    sent_in = [None] * depth
    scatter_in = [_scatter_cols(wcol)]
    token = None
    dg_norms = []
    for l in reversed(range(depth)):
        h_in, hn, u, mixed, y = saved[l]
        dy = _matmul(dhb, w_out_full[l], tb=True, tm=tm, tn=_tile(dmix, 1024), tk=d, dep=token, name=f"mm_dy_{l}")
        dw_out = _matmul(y, dhb, ta=True, tm=_tile(dmix, 1024), tn=_tile(d, 1024), tk=tp, out_dtype=BF16,
                         name=f"mm_dwout_{l}")
        sent_out[l] = _push_start([dw_out.reshape(N_DEV, wrow, d)], [lax.empty((N_DEV - 1, wrow, d), BF16)],
                                  [_scatter_lead], name=f"scatter_wout_{l}_start")
        du, sg, dwr, dwi = _mixer_bwd(u, mixed, dy, wa_full[l], wr_blk[l], vec(lru_br[l]), wi_blk[l], vec(lru_bi[l]),
                                      vec(lru_lambda[l]), _dep(wb_full[l], sent_out[l][-1]), name=f"mixer_bwd_{l}")
        small_grads[l] = (sg, dwr, dwi)
        if l == 0:
            rows = jnp.stack([small_grads[j][0] for j in range(depth)])
            early = [_pack([
                rows[:, SG_BA], rows[:, SG_BR], rows[:, SG_BI], rows[:, SG_LAM], rows[:, SG_WA:SG_WA + 4],
                rows[:, SG_WB:SG_WB + 3], dg_final[0], *dg_norms]),
                _pack([jnp.stack([_blockdiag_extract(small_grads[j][1], hd) for j in range(depth)]),
                       jnp.stack([_blockdiag_extract(small_grads[j][2], hd) for j in range(depth)])]).astype(BF16)]
            early_land = [lax.dynamic_update_slice(lax.empty((N_DEV,) + a.shape, a.dtype), a[None], (me, 0, 0))
                          for a in early]
            sent_early = _push_start(early, early_land, [_gather_lead] * 2, name="gather_early_grads_start")
        parts = 2 if l == 0 else 1
        token = sent_early[-1] if l == 0 else None
        sent_in[l] = []
        for p in range(parts):
            dw_in = _matmul(hn, du, ta=True, tm=_tile(d // parts, 512), tn=_tile(din, 1536), tk=tp, out_dtype=BF16,
                            dep=token, m_part=(p, parts), name=f"mm_dwin_{l}_{p}")
            sent_in[l].append(_push_start([dw_in], [lax.empty((N_DEV - 1, d // parts, wcol), BF16)], scatter_in,
                                          name=f"scatter_win_{l}_{p}_start"))
            token = sent_in[l][-1][-1]
        dhn = _matmul(du, w_in_full[l], tb=True, tm=_tile(tp, 528, 2 * SUBLANES), tn=_tile(d, 1024), tk=din, dep=token,
                      name=f"mm_dhn_{l}")
        if l > 0:
            dh, dhb, dg_norm = _rms_bwd(h_in, dhn, dh, norm_g[l], name=f"rms_bwd_{l}")
            dg_norms.append(dg_norm[0])
        else:
            grad_x, d_meta, dg_norm = _rms_bwd_first(h_in, dhn, dh, norm_g[l], n_meta=n_meta, seq=seq,
                                                     name=f"rms_bwd_{l}")

    big = {"win": None, "wout": None}

    def big_adamw(l, after):
        src, landed = _push_wait(sent_out[l], [_scatter_lead], after, name=f"scatter_wout_{l}_wait")
        own = lax.dynamic_index_in_dim(src[0], me, 0, keepdims=False)
        big["wout"] = _adamw(w_out, own, m_w_out, v_w_out, landed=landed[0], layer=l, depth=depth,
                             into=big["wout"], name=f"adamw_w_out_{l}")
        after = big["wout"][0]
        for p, sent in enumerate(sent_in[l]):
            src, landed = _push_wait(sent, scatter_in, after, name=f"scatter_win_{l}_{p}_wait")
            own = lax.dynamic_slice_in_dim(src[0], me * wcol, wcol, axis=1)
            big["win"] = _adamw(w_in, own, m_w_in, v_w_in, landed=landed[0], layer=l, depth=depth,
                                into=big["win"], row_off=p * own.shape[0], name=f"adamw_w_in_{l}_{p}")
            after = big["win"][0]
        return after

    after = dg_norm
    for l in reversed(range(1, depth)):
        after = big_adamw(l, after)

    late = _pack([dg_norm[0], d_meta, loss_part[0:1, 0:1]])
    if depth > 1:
        late, after = lax.optimization_barrier((late, after))
    late_all = _all_gather([late], [(N_DEV,) + late.shape], [_lead], name="gather_late_grads")[0]
    late_sum = _unpack(_slot_sum(late_all, name="sum_late_grads"), [(d,), (n_meta, d), ()])
    loss = late_sum[2]
    _, early_all = _push_wait(sent_early, [_gather_lead] * 2, late_sum[0], name="gather_early_grads_wait")
    vec_shapes = [conv_a_b.shape, lru_br.shape, lru_bi.shape, lru_lambda.shape, (depth, 4, dl), (depth, 3, dl),
                  final_g.shape] + [(d,)] * (depth - 1)
    e = _unpack(_slot_sum(early_all[0], name="sum_early_vectors"), vec_shapes)
    g_wr, g_wi = _unpack(_slot_sum(early_all[1], name="sum_early_maps"), [lru_wr.shape, lru_wi.shape])
    g_norm = jnp.stack([late_sum[0]] + e[7:][::-1])
    g_meta = lax.dynamic_slice_in_dim(late_sum[1], me * mcol, mcol, axis=1)
    g_wa = lax.dynamic_slice_in_dim(e[4], me * ccol, ccol, axis=2)
    g_wb = lax.dynamic_slice_in_dim(e[5], me * ccol, ccol, axis=2)

    small_w = [norm_g, conv_a_b, lru_wr, lru_br, lru_wi, lru_bi, lru_lambda, final_g, meta, conv_a_w, conv_b_w]
    small_m = [m_norm_g, m_conv_a_b, m_lru_wr, m_lru_br, m_lru_wi, m_lru_bi, m_lru_lambda, m_final_g, m_meta,
               m_conv_a_w, m_conv_b_w]
    small_v = [v_norm_g, v_conv_a_b, v_lru_wr, v_lru_br, v_lru_wi, v_lru_bi, v_lru_lambda, v_final_g, v_meta,
               v_conv_a_w, v_conv_b_w]
    small_g = [g_norm, e[0], g_wr, e[1], g_wi, e[2], e[3], e[6], g_meta, g_wa, g_wb]
    small_out = _adamw(_pack(small_w), _pack(small_g), _pack(small_m), _pack(small_v), name="adamw_small")
    small_shapes = [a.shape for a in small_w]
    s_grad, s_delta, s_m, s_v = [_unpack(o, small_shapes) for o in small_out]

    big_adamw(0, small_out[0])
    win_out, wout_out = big["win"], big["wout"]

    names = ["norm_g", "conv_a_b", "lru_wr", "lru_br", "lru_wi", "lru_bi", "lru_lambda", "final_g", "meta",
             "conv_a_w", "conv_b_w"]
    order = ["meta", "norm_g", "w_in", "conv_a_w", "conv_a_b", "lru_wr", "lru_br", "lru_wi", "lru_bi", "lru_lambda",
             "conv_b_w", "w_out", "final_g"]

    def family(idx, small):
        table = {nm: small[i] for i, nm in enumerate(names)}
        table["w_in"] = win_out[idx]
        table["w_out"] = wout_out[idx]
        return [table[nm] for nm in order]

    return (loss, grad_x, *family(0, s_grad), *family(1, s_delta), *family(2, s_m), *family(3, s_v))
```

```python
import functools

import jax
import jax.numpy as jnp
from jax import lax
from jax.experimental import pallas as pl
from jax.experimental.pallas import tpu as pltpu

F32 = jnp.float32
BF16 = jnp.bfloat16
MESH = pl.DeviceIdType.MESH

N_DEV = 8
RMS_EPS = 1e-6
LRU_C = 8.0
ADAM_LR = 0.001
ADAM_B1 = 0.9
ADAM_B2 = 0.999
ADAM_EPS = 1e-08
ADAM_WD = 0.01
ADAM_STEP = 10

V7X_VMEM_LIMIT = 52 * 1024 * 1024
LANES = 128
SUBLANES = 8
TOKEN_TILE = 384
MIX_ROWS = 128
MIX_SUBTILES = 3
SHIFTED_ROWS = 128
GATE_BLOCK = 128


def _params(sem):
    return pltpu.CompilerParams(dimension_semantics=sem, vmem_limit_bytes=V7X_VMEM_LIMIT)


def _tile(n, target, align=LANES):
    best = None
    for t in range(align, min(n, target) + 1, align):
        if n % t == 0:
            best = t
    return n if best is None else best


def _sigmoid(z):
    return 0.5 * jnp.tanh(0.5 * z) + 0.5


def _softplus(z):
    e = jnp.exp(-jnp.abs(z))
    u = 1.0 + e
    l1p = jnp.where(u == 1.0, e, jnp.log(u) * e / jnp.where(u == 1.0, 1.0, u - 1.0))
    return jnp.maximum(z, 0.0) + l1p


def _matmul(a, b, *, ta=False, tb=False, tm, tn, tk, out_dtype=F32, add=None, dep=None, m_part=None, name):
    m, k = (a.shape[1], a.shape[0]) if ta else a.shape
    m_off = 0
    if m_part is not None:
        assert add is None and m % (m_part[1] * tm) == 0
        m //= m_part[1]
        m_off = m_part[0] * (m // tm)
    n, kb = b.shape if tb else b.shape[::-1]
    assert kb == k
    assert m % tm == 0 and n % tn == 0 and k % tk == 0, (m, n, k, tm, tn, tk)
    nk = k // tk
    a_spec = pl.BlockSpec((tk, tm), lambda i, j, q: (q, i + m_off)) if ta \
        else pl.BlockSpec((tm, tk), lambda i, j, q: (i + m_off, q))
    b_spec = pl.BlockSpec((tn, tk), lambda i, j, q: (j, q)) if tb else pl.BlockSpec((tk, tn), lambda i, j, q: (q, j))
    o_spec = pl.BlockSpec((tm, tn), lambda i, j, q: (i, j))
    o_shape = (m, n)
    dims = (((0 if ta else 1,), (1 if tb else 0,)), ((), ()))
    has_add = add is not None
    has_dep = dep is not None

    def body(*refs):
        if has_dep:
            refs = refs[:-3] + refs[-2:]
        if has_add:
            a_ref, b_ref, add_ref, o_ref, acc_ref = refs
        else:
            a_ref, b_ref, o_ref, acc_ref = refs
        q = pl.program_id(2)
        part = lax.dot_general(a_ref[...], b_ref[...], dims, preferred_element_type=F32)

        def finish(acc):
            if has_add:
                acc = acc + add_ref[...]
            o_ref[...] = acc.astype(out_dtype)

        if nk == 1:
            finish(part)
        else:
            @pl.when(q == 0)
            def _():
                acc_ref[...] = part

            @pl.when(jnp.logical_and(q > 0, q < nk - 1))
            def _():
                acc_ref[...] += part

            @pl.when(q == nk - 1)
            def _():
                finish(acc_ref[...] + part)

    in_specs = [a_spec, b_spec] + ([o_spec] if has_add else [])
    args = (a, b) + ((add,) if has_add else ())
    if has_dep:
        in_specs.append(pl.BlockSpec((SUBLANES, LANES), lambda i, j, q: (0, 0)))
        args += (dep,)
    acc_shape = (tm, tn) if nk > 1 else (SUBLANES, LANES)
    return pl.pallas_call(
        body, name=name,
        grid=(m // tm, n // tn, nk),
        in_specs=in_specs, out_specs=o_spec,
        out_shape=jax.ShapeDtypeStruct(o_shape, out_dtype),
        scratch_shapes=[pltpu.VMEM(acc_shape, F32)],
        compiler_params=_params(("parallel", "parallel", "arbitrary")),
    )(*args)


def _rms_fwd(h, g, *, name):
    tp, d = h.shape
    tr = _tile(tp, 512, SUBLANES)

    def body(h_ref, g_ref, o_ref):
        hv = h_ref[...]
        rstd = lax.rsqrt(jnp.mean(hv * hv, axis=-1, keepdims=True) + RMS_EPS)
        o_ref[...] = (hv * rstd * g_ref[...]).astype(BF16)

    return pl.pallas_call(
        body, name=name, grid=(tp // tr,),
        in_specs=[pl.BlockSpec((tr, d), lambda i: (i, 0)), pl.BlockSpec((1, d), lambda i: (0, 0))],
        out_specs=pl.BlockSpec((tr, d), lambda i: (i, 0)),
        out_shape=jax.ShapeDtypeStruct((tp, d), BF16),
        compiler_params=_params(("parallel",)),
    )(h, g.reshape(1, d))


def _rms_fwd_first(x, meta, g, *, tp, name):
    seq, d = x.shape
    n_meta = meta.shape[0]
    n_tok = n_meta + seq
    tr = SHIFTED_ROWS
    assert tp % tr == 0 and seq % tr == 0 and tr % n_meta == 0
    per = tr // n_meta

    def body(x_ref, xp_ref, m_ref, g_ref, h_ref, o_ref):
        i = pl.program_id(0)
        head = jnp.where(i == 0, m_ref[...], xp_ref[...])
        rows = i * tr + lax.broadcasted_iota(jnp.int32, (tr, 1), 0)
        hv = jnp.where(rows < n_tok, jnp.concatenate([head, x_ref[:tr - n_meta, :]], axis=0), 0.0)
        h_ref[...] = hv
        rstd = lax.rsqrt(jnp.mean(hv * hv, axis=-1, keepdims=True) + RMS_EPS)
        o_ref[...] = (hv * rstd * g_ref[...]).astype(BF16)

    row = pl.BlockSpec((tr, d), lambda i: (i, 0))
    own = pl.BlockSpec((tr, d), lambda i: (jnp.minimum(i, seq // tr - 1), 0))
    before = pl.BlockSpec((n_meta, d), lambda i: (jnp.maximum(i * per - 1, 0), 0))
    return pl.pallas_call(
        body, name=name, grid=(tp // tr,),
        in_specs=[own, before, pl.BlockSpec((n_meta, d), lambda i: (0, 0)), pl.BlockSpec((1, d), lambda i: (0, 0))],
        out_specs=[row, row],
        out_shape=[jax.ShapeDtypeStruct((tp, d), F32), jax.ShapeDtypeStruct((tp, d), BF16)],
        compiler_params=_params(("parallel",)),
    )(x, x, meta, g.reshape(1, d))


def _rms_bwd(h, dhn, dout, g, *, name):
    tp, d = h.shape
    tr = _tile(tp, 528, 2 * SUBLANES)

    def body(h_ref, dhn_ref, dout_ref, g_ref, dh_ref, dhb_ref, dg_ref):
        hv = h_ref[...]
        rstd = lax.rsqrt(jnp.mean(hv * hv, axis=-1, keepdims=True) + RMS_EPS)
        xhat = hv * rstd
        dn = dhn_ref[...]
        dxhat = dn * g_ref[...]
        dh = dout_ref[...] + rstd * (dxhat - xhat * jnp.mean(dxhat * xhat, axis=-1, keepdims=True))
        dh_ref[...] = dh
        dhb_ref[...] = dh.astype(BF16)
        part = jnp.sum(dn * xhat, axis=0, keepdims=True)

        @pl.when(pl.program_id(0) == 0)
        def _():
            dg_ref[...] = part

        @pl.when(pl.program_id(0) > 0)
        def _():
            dg_ref[...] += part

    row = pl.BlockSpec((tr, d), lambda i: (i, 0))
    vec = pl.BlockSpec((1, d), lambda i: (0, 0))
    return pl.pallas_call(
        body, name=name, grid=(tp // tr,),
        in_specs=[row, row, row, vec],
        out_specs=[row, row, vec],
        out_shape=[jax.ShapeDtypeStruct((tp, d), F32), jax.ShapeDtypeStruct((tp, d), BF16),
                   jax.ShapeDtypeStruct((1, d), F32)],
        compiler_params=_params(("arbitrary",)),
    )(h, dhn, dout, g.reshape(1, d))


def _rms_bwd_first(h, dhn, dout, g, *, n_meta, seq, name):
    tp, d = h.shape
    tr = SHIFTED_ROWS
    assert seq % tr == 0 and tr % n_meta == 0 and tp >= seq + n_meta
    nt = seq // tr
    per = tr // n_meta

    def grads(hv, dn, do, gv):
        rstd = lax.rsqrt(jnp.mean(hv * hv, axis=-1, keepdims=True) + RMS_EPS)
        xhat = hv * rstd
        dxhat = dn * gv
        dh = do + rstd * (dxhat - xhat * jnp.mean(dxhat * xhat, axis=-1, keepdims=True))
        return dh, jnp.sum(dn * xhat, axis=0, keepdims=True)

    def body(h_ref, dhn_ref, dout_ref, hn_ref, dhnn_ref, doutn_ref, g_ref, gx_ref, dmeta_ref, dg_ref):
        i = pl.program_id(0)
        gv = g_ref[...]
        dh, part = grads(h_ref[...], dhn_ref[...], dout_ref[...], gv)
        dh_next, part_next = grads(hn_ref[...], dhnn_ref[...], doutn_ref[...], gv)
        gx_ref[...] = jnp.concatenate([dh[n_meta:], dh_next], axis=0)

        @pl.when(i == 0)
        def _():
            dmeta_ref[...] = dh[:n_meta]
            dg_ref[...] = part

        @pl.when(i > 0)
        def _():
            dg_ref[...] += part

        @pl.when(i == nt - 1)
        def _():
            dg_ref[...] += part_next

    row = pl.BlockSpec((tr, d), lambda i: (i, 0))
    nxt = pl.BlockSpec((n_meta, d), lambda i: ((i + 1) * per, 0))
    vec = pl.BlockSpec((1, d), lambda i: (0, 0))
    return pl.pallas_call(
        body, name=name, grid=(nt,),
        in_specs=[row, row, row, nxt, nxt, nxt, vec],
        out_specs=[pl.BlockSpec((None, tr, d), lambda i: (0, i, 0)), pl.BlockSpec((n_meta, d), lambda i: (0, 0)), vec],
        out_shape=[jax.ShapeDtypeStruct((1, seq, d), F32), jax.ShapeDtypeStruct((n_meta, d), F32),
                   jax.ShapeDtypeStruct((1, d), F32)],
        compiler_params=_params(("arbitrary",)),
    )(h, dhn, dout, h, dhn, dout, g.reshape(1, d))


def _loss_head(h, tgt, g, *, n_meta, n_tok, name):
    tp, d = h.shape
    seq = tgt.shape[0]
    tr = SHIFTED_ROWS
    assert tp % tr == 0 and seq % tr == 0 and tr % n_meta == 0
    per = tr // n_meta

    def body(h_ref, t_ref, tp_ref, g_ref, dh_ref, dhb_ref, dg_ref, loss_ref):
        i = pl.program_id(0)
        hv = h_ref[...]
        rstd = lax.rsqrt(jnp.mean(hv * hv, axis=-1, keepdims=True) + RMS_EPS)
        xhat = hv * rstd
        gv = g_ref[...]
        rows = i * tr + lax.broadcasted_iota(jnp.int32, (tr, 1), 0)
        valid = jnp.logical_and(rows >= n_meta, rows < n_tok)
        target = jnp.concatenate([tp_ref[...], t_ref[:tr - n_meta, :]], axis=0)
        err = jnp.where(valid, xhat * gv - target, 0.0)
        dy = err * (1.0 / d)
        dxhat = dy * gv
        dh = rstd * (dxhat - xhat * jnp.mean(dxhat * xhat, axis=-1, keepdims=True))
        dh_ref[...] = dh
        dhb_ref[...] = dh.astype(BF16)
        dg_part = jnp.sum(dy * xhat, axis=0, keepdims=True)
        per_row = jnp.sum(err * err, axis=-1, keepdims=True) * (1.0 / d)
        loss_part = jnp.broadcast_to(0.5 * jnp.sum(per_row, axis=0, keepdims=True), (SUBLANES, LANES))

        @pl.when(i == 0)
        def _():
            dg_ref[...] = dg_part
            loss_ref[...] = loss_part

        @pl.when(i > 0)
        def _():
            dg_ref[...] += dg_part
            loss_ref[...] += loss_part

    row = pl.BlockSpec((tr, d), lambda i: (i, 0))
    vec = pl.BlockSpec((1, d), lambda i: (0, 0))
    own = pl.BlockSpec((tr, d), lambda i: (jnp.minimum(i, seq // tr - 1), 0))
    before = pl.BlockSpec((n_meta, d), lambda i: (jnp.maximum(i * per - 1, 0), 0))
    return pl.pallas_call(
        body, name=name, grid=(tp // tr,),
        in_specs=[row, own, before, vec],
        out_specs=[row, row, vec, pl.BlockSpec((SUBLANES, LANES), lambda i: (0, 0))],
        out_shape=[jax.ShapeDtypeStruct((tp, d), F32), jax.ShapeDtypeStruct((tp, d), BF16),
                   jax.ShapeDtypeStruct((1, d), F32), jax.ShapeDtypeStruct((SUBLANES, LANES), F32)],
        compiler_params=_params(("arbitrary",)),
    )(h, tgt, tgt, g.reshape(1, d))


def _shift_down(halo, tile, s):
    if s == 0:
        return tile
    ext = jnp.concatenate([halo, tile], axis=0)
    return pltpu.roll(ext, s, 0)[SUBLANES:]


def _shift_up(tile, head, s):
    if s == 0:
        return tile
    ext = jnp.concatenate([tile, head], axis=0)
    n = ext.shape[0]
    return pltpu.roll(ext, n - s, 0)[: tile.shape[0]]


def _to_lane_blocks(ref, cols, val):
    for j in range(cols.start // LANES, cols.stop // LANES):
        ref[j] = val[:, j * LANES - cols.start:(j + 1) * LANES - cols.start]


def _from_lane_blocks(ref, cols):
    return jnp.concatenate([ref[j] for j in range(cols.start // LANES, cols.stop // LANES)], axis=1)


def _scan_tile(a_ref, b_ref, out_ref, carry, j, *, reverse):
    ng = a_ref.shape[1] // SUBLANES
    order = list(range(SUBLANES))[::-1] if reverse else list(range(SUBLANES))

    def rows(r):
        return pl.ds(r, ng, stride=SUBLANES)

    prod, loc = {}, {}
    prev = None
    for r in order:
        ar = a_ref[j, rows(r), :]
        br = b_ref[j, rows(r), :]
        prod[r] = ar if prev is None else ar * prod[prev]
        loc[r] = br if prev is None else ar * loc[prev] + br
        prev = r
    pg, lg = prod[prev], loc[prev]
    ones = jnp.ones((SUBLANES,) + pg.shape[1:], F32)
    zeros = jnp.zeros_like(ones)
    s = 1
    while s < ng:
        p_sh = _shift_up(pg, ones, s) if reverse else _shift_down(ones, pg, s)
        l_sh = _shift_up(lg, zeros, s) if reverse else _shift_down(zeros, lg, s)
        lg = pg * l_sh + lg
        pg = pg * p_sh
        s *= 2
    leaving = pg * carry[0:1, :] + lg
    entering = _shift_up(leaving, carry, 1) if reverse else _shift_down(carry, leaving, 1)
    for r in order:
        out_ref[j, rows(r), :] = loc[r] + prod[r] * entering
    last = leaving[0:1, :] if reverse else leaving[ng - 1:ng, :]
    return jnp.broadcast_to(last, carry.shape)


def _gates(ca, wr, wi, br, bi, sp):
    cab = ca.astype(BF16)
    r = _sigmoid(jnp.dot(cab, wr, preferred_element_type=F32) + br)
    ig = _sigmoid(jnp.dot(cab, wi, preferred_element_type=F32) + bi)
    la = -LRU_C * r * sp
    a = jnp.exp(la)
    mult = jnp.sqrt(-jnp.tanh(la) * (a * a + 1.0))
    return r, ig, a, mult


U_STREAMS = 3


class _ColumnParts:
    def __init__(self, refs):
        self.refs = refs
        self.width = refs[0].shape[-1]

    def __getitem__(self, idx):
        rows, cols = idx
        k = cols.start // self.width
        assert cols.stop <= (k + 1) * self.width
        return self.refs[k][rows, cols.start - k * self.width:cols.stop - k * self.width]


def _mixer_fwd(u, wa, ba, wr_blk, br, wi_blk, bi, lam, wb, *, name):
    tp, din = u.shape
    dl = din // 6
    tt = MIX_ROWS
    cw = GATE_BLOCK
    nch = dl // cw
    assert tp % tt == 0 and dl % cw == 0

    def body(*refs):
        u_ref = _ColumnParts(refs[:U_STREAMS])
        (wa_ref, ba_ref, wr_ref, br_ref, wi_ref, bi_ref, lam_ref, wb_ref,
         s_ref, y_ref, xa_tail, v_tail, h_carry, a_s, b_s, h_s) = refs[U_STREAMS:]

        @pl.when(pl.program_id(0) == 0)
        def _():
            xa_tail[...] = jnp.zeros_like(xa_tail)
            v_tail[...] = jnp.zeros_like(v_tail)
            h_carry[...] = jnp.zeros_like(h_carry)

        for sub in range(MIX_SUBTILES):
            rows = slice(sub * tt, (sub + 1) * tt)
            for ch in range(nch):
                cs = slice(ch * cw, (ch + 1) * cw)

                def seg(s):
                    return slice(s * dl + ch * cw, s * dl + (ch + 1) * cw)

                xa = u_ref[rows, seg(0)]
                halo = xa_tail[:, cs]
                ca = ba_ref[:, cs] + wa_ref[3:4, cs] * xa
                for kk in range(3):
                    ca = ca + wa_ref[kk:kk + 1, cs] * _shift_down(halo, xa, 3 - kk)
                xa_tail[:, cs] = xa[tt - SUBLANES:]
                s_ref[rows, cs] = ca
                sp = _softplus(-lam_ref[:, cs])
                _, ig, a, mult = _gates(ca, wr_ref[ch], wi_ref[ch], br_ref[:, cs], bi_ref[:, cs], sp)
                _to_lane_blocks(a_s, cs, a)
                _to_lane_blocks(b_s, cs, mult * (ig * ca))

                bv = u_ref[rows, seg(2)]
                v = u_ref[rows, seg(3)] * u_ref[rows, seg(4)]
                gb = u_ref[rows, seg(5)]
                vh = v_tail[:, cs]
                cb = wb_ref[2:3, cs] * v
                for kk in range(2):
                    cb = cb + wb_ref[kk:kk + 1, cs] * _shift_down(vh, v, 2 - kk)
                v_tail[:, cs] = v[tt - SUBLANES:]
                y_ref[rows, dl + ch * cw: dl + (ch + 1) * cw] = (bv * cb * (gb * _sigmoid(gb))).astype(BF16)

            for ch in range(nch):
                cs = slice(ch * cw, (ch + 1) * cw)
                for j in range(cs.start // LANES, cs.stop // LANES):
                    lanes = slice(j * LANES, (j + 1) * LANES)
                    h_carry[:, lanes] = _scan_tile(a_s, b_s, h_s, h_carry[:, lanes], j, reverse=False)
                hsv = _from_lane_blocks(h_s, cs)
                s_ref[rows, dl + ch * cw: dl + (ch + 1) * cw] = hsv
                ga = u_ref[rows, dl + ch * cw: dl + (ch + 1) * cw]
                y_ref[rows, cs] = (hsv * (ga * _sigmoid(ga))).astype(BF16)

    tb = tt * MIX_SUBTILES
    assert tp % tb == 0
    row = lambda w: pl.BlockSpec((tb, w), lambda i: (i, 0))
    full = lambda shp: pl.BlockSpec(shp, lambda i: tuple(0 for _ in shp))
    return pl.pallas_call(
        body, name=name, grid=(tp // tb,),
        in_specs=[pl.BlockSpec((tb, din // U_STREAMS), functools.partial(lambda k, i: (i, k), k))
                  for k in range(U_STREAMS)]
        + [full((4, dl)), full((1, dl)), full((nch, cw, cw)), full((1, dl)),
           full((nch, cw, cw)), full((1, dl)), full((1, dl)), full((3, dl))],
        out_specs=[row(2 * dl), row(2 * dl)],
        out_shape=[jax.ShapeDtypeStruct((tp, 2 * dl), F32), jax.ShapeDtypeStruct((tp, 2 * dl), BF16)],
        scratch_shapes=[pltpu.VMEM((SUBLANES, dl), F32), pltpu.VMEM((SUBLANES, dl), F32),
                        pltpu.VMEM((SUBLANES, dl), F32)] + [pltpu.VMEM((dl // LANES, tt, LANES), F32)] * 3,
        compiler_params=_params(("arbitrary",)),
    )(*[u] * U_STREAMS, wa, ba, wr_blk, br, wi_blk, bi, lam, wb)


SG_WA, SG_BA, SG_BR, SG_BI, SG_LAM, SG_WB, SG_ROWS = 0, 4, 5, 6, 7, 8, 16


def _mixer_bwd(u, saved, dy, wa, wr_blk, br, wi_blk, bi, lam, wb, *, name):
    tp, din = u.shape
    dl = din // 6
    tt = MIX_ROWS
    cw = GATE_BLOCK
    nch = dl // cw
    tb = tt * MIX_SUBTILES
    assert tp % tb == 0
    nt = tp // tb
    hb = tb // SUBLANES
    tn_dims = (((0,), (0,)), ((), ()))
    nt_dims = (((1,), (1,)), ((), ()))

    def body(*refs):
        u_ref = _ColumnParts(refs[:U_STREAMS])
        (uh_ref, s_ref, sh_ref, dy_ref, wa_ref, wr_ref, br_ref, wi_ref, bi_ref, lam_ref, wb_ref,
         du_ref, sg_ref, dwr_ref, dwi_ref,
         g_carry, a_head, dca_head, dcb_head, r_s, i_s, a_s, an_s, d_s, g_s) = refs[U_STREAMS:]
        i = pl.program_id(0)
        first_tile = i == nt - 1

        @pl.when(i == 0)
        def _():
            for ref in (g_carry, a_head, dca_head, dcb_head, sg_ref, dwr_ref, dwi_ref):
                ref[...] = jnp.zeros_like(ref)

        def halo_of(x):
            return jnp.where(first_tile, 0.0, x)

        for sub in reversed(range(MIX_SUBTILES)):
            rows = slice(sub * tt, (sub + 1) * tt)

            def before(ref, halo_ref, cols):
                if sub == 0:
                    return halo_of(halo_ref[:, cols])
                return ref[sub * tt - SUBLANES:sub * tt, cols]

            for ch in range(nch):
                cs = slice(ch * cw, (ch + 1) * cw)
                cav = s_ref[rows, cs]
                sp = _softplus(-lam_ref[:, cs])
                r, ig, a, _ = _gates(cav, wr_ref[ch], wi_ref[ch], br_ref[:, cs], bi_ref[:, cs], sp)
                r_s[:, cs] = r
                i_s[:, cs] = ig
                a_s[:, cs] = a
                _to_lane_blocks(an_s, cs, _shift_up(a, a_head[:, cs], 1))
                a_head[:, cs] = a[:SUBLANES]
                ga = u_ref[rows, dl + ch * cw: dl + (ch + 1) * cw]
                _to_lane_blocks(d_s, cs, dy_ref[rows, cs] * (ga * _sigmoid(ga)))

            for j in range(dl // LANES):
                lanes = slice(j * LANES, (j + 1) * LANES)
                g_carry[:, lanes] = _scan_tile(an_s, d_s, g_s, g_carry[:, lanes], j, reverse=True)

            def acc_row(r0, val):
                sg_ref[r0:r0 + 1, cs_cur[0]] += jnp.sum(val, axis=0, keepdims=True)

            cs_cur = [None]
            for ch in range(nch):
                cs = slice(ch * cw, (ch + 1) * cw)
                cs_cur[0] = cs

                def seg(s):
                    return slice(s * dl + ch * cw, s * dl + (ch + 1) * cw)

                cav = s_ref[rows, cs]
                r = r_s[:, cs]
                ig = i_s[:, cs]
                a = a_s[:, cs]
                g = _from_lane_blocks(g_s, cs)
                hsv = s_ref[rows, dl + ch * cw: dl + (ch + 1) * cw]
                lamv = lam_ref[:, cs]
                sp = _softplus(-lamv)
                la = -LRU_C * r * sp
                e2 = a * a
                one_m_e2 = -jnp.tanh(la) * (e2 + 1.0)
                mult = jnp.sqrt(one_m_e2)
                hprev = _shift_down(before(s_ref, sh_ref, slice(dl + ch * cw, dl + (ch + 1) * cw)), hsv, 1)
                icav = ig * cav
                dla = g * (hprev * a - icav * (e2 * lax.rsqrt(one_m_e2)))
                gm = g * mult
                dzi = gm * icav * (1.0 - ig)
                dca = gm * ig
                dla_r = dla * r
                dzr = dla_r * (1.0 - r) * (-LRU_C * sp)
                sg_ref[SG_LAM:SG_LAM + 1, cs] += jnp.sum(dla_r, axis=0, keepdims=True) * (LRU_C * _sigmoid(-lamv))
                acc_row(SG_BR, dzr)
                acc_row(SG_BI, dzi)
                dzr_b = dzr.astype(BF16)
                dzi_b = dzi.astype(BF16)
                cab = cav.astype(BF16)
                dca = dca + lax.dot_general(dzr_b, wr_ref[ch], nt_dims, preferred_element_type=F32)
                dca = dca + lax.dot_general(dzi_b, wi_ref[ch], nt_dims, preferred_element_type=F32)
                dwr_ref[ch] += lax.dot_general(cab, dzr_b, tn_dims, preferred_element_type=F32)
                dwi_ref[ch] += lax.dot_general(cab, dzi_b, tn_dims, preferred_element_type=F32)
                acc_row(SG_BA, dca)
                xa = u_ref[rows, seg(0)]
                head = dca_head[:, cs]
                dxa = wa_ref[3:4, cs] * dca
                acc_row(SG_WA + 3, dca * xa)
                for kk in range(3):
                    later = _shift_up(dca, head, 3 - kk)
                    acc_row(SG_WA + kk, later * xa)
                    dxa = dxa + wa_ref[kk:kk + 1, cs] * later
                dca_head[:, cs] = dca[:SUBLANES]
                ga = u_ref[rows, seg(1)]
                sga = _sigmoid(ga)
                dga = dy_ref[rows, cs] * hsv * (sga + (ga * sga) * (1.0 - sga))
                du_ref[rows, seg(0)] = dxa.astype(BF16)
                du_ref[rows, seg(1)] = dga.astype(BF16)

                bv = u_ref[rows, seg(2)]
                cv = u_ref[rows, seg(3)]
                xb = u_ref[rows, seg(4)]
                gb = u_ref[rows, seg(5)]
                dyb = dy_ref[rows, dl + ch * cw: dl + (ch + 1) * cw]
                v = cv * xb
                vh = before(u_ref, uh_ref, seg(3)) * before(u_ref, uh_ref, seg(4))
                v1 = _shift_down(vh, v, 1)
                v2 = _shift_down(vh, v, 2)
                cb = wb_ref[2:3, cs] * v + wb_ref[1:2, cs] * v1 + wb_ref[0:1, cs] * v2
                sgb = _sigmoid(gb)
                sl = gb * sgb
                dyb_b = dyb * bv
                dyb_cb = dyb * cb
                dcb = dyb_b * sl
                du_ref[rows, seg(2)] = (dyb_cb * sl).astype(BF16)
                du_ref[rows, seg(5)] = (dyb_cb * bv * (sgb + sl * (1.0 - sgb))).astype(BF16)
                bhead = dcb_head[:, cs]
                dv = wb_ref[2:3, cs] * dcb
                acc_row(SG_WB + 2, dcb * v)
                for kk in range(2):
                    later = _shift_up(dcb, bhead, 2 - kk)
                    acc_row(SG_WB + kk, later * v)
                    dv = dv + wb_ref[kk:kk + 1, cs] * later
                dcb_head[:, cs] = dcb[:SUBLANES]
                du_ref[rows, seg(3)] = (dv * xb).astype(BF16)
                du_ref[rows, seg(4)] = (dv * cv).astype(BF16)

    rev = lambda w: pl.BlockSpec((tb, w), lambda i: (nt - 1 - i, 0))
    halo = lambda w: pl.BlockSpec((SUBLANES, w), lambda i: (jnp.maximum((nt - 1 - i) * hb - 1, 0), 0))
    full = lambda shp: pl.BlockSpec(shp, lambda i: tuple(0 for _ in shp))
    vm = lambda r: pltpu.VMEM((r, dl), F32)
    return pl.pallas_call(
        body, name=name, grid=(nt,),
        in_specs=[pl.BlockSpec((tb, din // U_STREAMS), functools.partial(lambda k, i: (nt - 1 - i, k), k))
                  for k in range(U_STREAMS)]
        + [halo(din), rev(2 * dl), halo(2 * dl), rev(2 * dl), full((4, dl)),
           full((nch, cw, cw)), full((1, dl)), full((nch, cw, cw)), full((1, dl)), full((1, dl)), full((3, dl))],
        out_specs=[rev(din), full((SG_ROWS, dl)), full((nch, cw, cw)), full((nch, cw, cw))],
        out_shape=[jax.ShapeDtypeStruct((tp, din), BF16), jax.ShapeDtypeStruct((SG_ROWS, dl), F32),
                   jax.ShapeDtypeStruct((nch, cw, cw), F32), jax.ShapeDtypeStruct((nch, cw, cw), F32)],
        scratch_shapes=[vm(SUBLANES), vm(SUBLANES), vm(SUBLANES), vm(SUBLANES), vm(tt), vm(tt), vm(tt)]
        + [pltpu.VMEM((dl // LANES, tt, LANES), F32)] * 3,
        compiler_params=_params(("arbitrary",)),
    )(*[u] * (U_STREAMS + 1), saved, saved, dy, wa, wr_blk, br, wi_blk, bi, lam, wb)


def _adamw(w, g, m, v, *, name, landed=None, layer=None, depth=None, into=None, row_off=0):
    r, c = w.shape[-2:]
    rows = g.shape[0]
    tr = _tile(rows, 512, 2 * SUBLANES)
    assert row_off % tr == 0
    boff = row_off // tr
    bc1 = 1.0 - ADAM_B1 ** ADAM_STEP
    bc2 = 1.0 - ADAM_B2 ** ADAM_STEP
    slots = landed is not None

    def body(*refs):
        if into is not None:
            refs = refs[:-8] + refs[-4:]
        if slots:
            w_ref, g_ref, l_ref, m_ref, v_ref, grad_ref, delta_ref, nm_ref, nv_ref = refs
            gv = g_ref[...].astype(F32)
            for s in range(N_DEV - 1):
                gv = gv + l_ref[s].astype(F32)
        else:
            w_ref, g_ref, m_ref, v_ref, grad_ref, delta_ref, nm_ref, nv_ref = refs
            gv = g_ref[...]
        wv = w_ref[...]
        mn = ADAM_B1 * m_ref[...] + (1.0 - ADAM_B1) * gv
        vn = ADAM_B2 * v_ref[...] + (1.0 - ADAM_B2) * (gv * gv)
        m_hat = mn / bc1
        v_hat = vn / bc2
        grad_ref[...] = gv
        delta_ref[...] = -ADAM_LR * (m_hat / (jnp.sqrt(v_hat) + ADAM_EPS) + ADAM_WD * wv)
        nm_ref[...] = mn
        nv_ref[...] = vn

    if depth is None:
        blk = pl.BlockSpec((tr, c), lambda i: (i + boff, 0))
    else:
        blk = pl.BlockSpec((None, tr, c), lambda i: (layer, i + boff, 0))
    g_blk = pl.BlockSpec((tr, c), lambda i: (i, 0))
    l_spec = [pl.BlockSpec((N_DEV - 1, tr, c), lambda i: (0, i, 0))] if slots else []
    args = (w, g, landed, m, v) if slots else (w, g, m, v)
    in_specs = [blk, g_blk] + l_spec + [blk, blk]
    if depth is None:
        shp = jax.ShapeDtypeStruct((r, c), F32)
        out_blk = blk
    else:
        shp = jax.ShapeDtypeStruct((depth, r, c), F32)
        out_blk = pl.BlockSpec((None, tr, c), lambda i: (layer, i + boff, 0))
    aliases = {}
    if into is not None:
        aliases = {len(args) + j: j for j in range(4)}
        in_specs = in_specs + [ANY] * 4
        args = args + tuple(into)
    return pl.pallas_call(
        body, name=name, grid=(rows // tr,),
        in_specs=in_specs, out_specs=[out_blk] * 4,
        out_shape=[shp] * 4, input_output_aliases=aliases,
        compiler_params=_params(("parallel",)),
    )(*args)


def _slot_sum(g, *, name):
    _, r, c = g.shape
    tr = _tile(r, 512, SUBLANES)

    def body(g_ref, o_ref):
        gv = g_ref[0].astype(F32)
        for s in range(1, N_DEV):
            gv = gv + g_ref[s].astype(F32)
        o_ref[...] = gv

    return pl.pallas_call(
        body, name=name, grid=(r // tr,),
        in_specs=[pl.BlockSpec((N_DEV, tr, c), lambda i: (0, i, 0))],
        out_specs=pl.BlockSpec((tr, c), lambda i: (i, 0)),
        out_shape=jax.ShapeDtypeStruct((r, c), F32),
        compiler_params=_params(("parallel",)),
    )(g)


def _mesh_pos():
    x, y, c = lax.axis_index("x"), lax.axis_index("y"), lax.axis_index("c")
    return x, y, c, 4 * x + 2 * y + c


ANY = pl.BlockSpec(memory_space=pl.ANY)


GATHER_COPIES = 9


def _all_gather(srcs, out_shapes, views, *, name):
    n = len(srcs)
    SIB, X_OWN, Y_OWN, X_DIAG, Y_DIAG, SIB_X, SIB_Y, SIB_DIAG_TOP, SIB_DIAG_BOTTOM = range(GATHER_COPIES)

    def body(*refs):
        src = refs[:n]
        dst = refs[n:2 * n]
        send_sems, recv_sems, local_sems = refs[2 * n:]
        x, y, c, me = _mesh_pos()
        sibling, x_nbr, y_nbr = (x, y, 1 - c), (1 - x, y, c), (x, 1 - y, c)

        def block(a, px, py, pc, half=None):
            win = views[a](dst[a], 4 * px + 2 * py + pc)
            if half is None:
                return win
            rows = win.shape[0] // 2
            return win.at[pl.ds(half * rows, rows)]

        def copy(a, k, win, to, from_src=False):
            return pltpu.make_async_remote_copy(
                src_ref=src[a] if from_src else win, dst_ref=win,
                send_sem=send_sems.at[a * GATHER_COPIES + k], recv_sem=recv_sems.at[a * GATHER_COPIES + k],
                device_id=to, device_id_type=MESH)

        mine = [pltpu.make_async_copy(src[a], block(a, x, y, c), local_sems.at[a]) for a in range(n)]
        started = []

        def start(cp):
            cp.start()
            started.append(cp)

        for a in range(n):
            mine[a].start()
            own = block(a, x, y, c)
            start(copy(a, SIB, own, sibling, True))
            start(copy(a, X_OWN, own, x_nbr, True))
            start(copy(a, Y_OWN, own, y_nbr, True))
        for a in range(n):
            from_y = block(a, x, 1 - y, c)
            copy(a, Y_OWN, from_y, y_nbr).wait_recv()
            start(copy(a, X_DIAG, block(a, x, 1 - y, c, 0), x_nbr))
            start(copy(a, SIB_Y, from_y, sibling))
            from_x = block(a, 1 - x, y, c)
            copy(a, X_OWN, from_x, x_nbr).wait_recv()
            start(copy(a, Y_DIAG, block(a, 1 - x, y, c, 1), y_nbr))
            start(copy(a, SIB_X, from_x, sibling))
        for a in range(n):
            top = block(a, 1 - x, 1 - y, c, 0)
            copy(a, X_DIAG, top, x_nbr).wait_recv()
            start(copy(a, SIB_DIAG_TOP, top, sibling))
            bottom = block(a, 1 - x, 1 - y, c, 1)
            copy(a, Y_DIAG, bottom, y_nbr).wait_recv()
            start(copy(a, SIB_DIAG_BOTTOM, bottom, sibling))
        for a in range(n):
            copy(a, SIB, block(a, x, y, 1 - c), sibling).wait_recv()
            copy(a, SIB_X, block(a, 1 - x, y, 1 - c), sibling).wait_recv()
            copy(a, SIB_Y, block(a, x, 1 - y, 1 - c), sibling).wait_recv()
            copy(a, SIB_DIAG_TOP, block(a, 1 - x, 1 - y, 1 - c, 0), sibling).wait_recv()
            copy(a, SIB_DIAG_BOTTOM, block(a, 1 - x, 1 - y, 1 - c, 1), sibling).wait_recv()
        for cp in started:
            cp.wait_send()
        for cp in mine:
            cp.wait()

    return pl.pallas_call(
        body, name=name,
        in_specs=[ANY] * n, out_specs=[ANY] * n,
        out_shape=[jax.ShapeDtypeStruct(s, x.dtype) for s, x in zip(out_shapes, srcs)],
        scratch_shapes=[pltpu.SemaphoreType.DMA((GATHER_COPIES * n,)), pltpu.SemaphoreType.DMA((GATHER_COPIES * n,)),
                        pltpu.SemaphoreType.DMA((n,))],
    )(*srcs)


HBM = pl.BlockSpec(memory_space=pltpu.HBM)
SEM = pl.BlockSpec(memory_space=pltpu.SEMAPHORE)
EFFECT = pltpu.SideEffectType.DATAFLOW_SIDE_EFFECTING


def _peer_of(x, y, c, k):
    return (1 - x if k & 4 else x, 1 - y if k & 2 else y, 1 - c if k & 1 else c)


def _peer_copies(n, wins, src, land, send_sems, recv_sems):
    x, y, c, me = _mesh_pos()
    out = []
    for a in range(n):
        for k in range(1, N_DEV):
            px, py, pc = _peer_of(x, y, c, k)
            s_win, d_win = wins[a](src[a], land[a], me, 4 * px + 2 * py + pc, k)
            out.append(pltpu.make_async_remote_copy(
                src_ref=s_win, dst_ref=d_win,
                send_sem=send_sems.at[a * 7 + k - 1], recv_sem=recv_sems.at[a * 7 + k - 1],
                device_id=(px, py, pc), device_id_type=MESH))
    return out


def _push_start(srcs, lands, wins, *, name):
    n = len(srcs)

    def body(*refs):
        src = refs[:n]
        land = refs[n:2 * n]
        send_sems, recv_sems = refs[2 * n], refs[2 * n + 1]
        token = refs[-1]
        for cp in _peer_copies(n, wins, src, land, send_sems, recv_sems):
            cp.start()
        token[...] = jnp.zeros_like(token)

    bufs = (*srcs, *lands)
    return pl.pallas_call(
        body, name=name,
        out_shape=(pltpu.SemaphoreType.DMA((7 * n,)), pltpu.SemaphoreType.DMA((7 * n,)),
                   *[pltpu.HBM(v.shape, v.dtype) for v in bufs], jax.ShapeDtypeStruct((SUBLANES, LANES), F32)),
        in_specs=[HBM] * (2 * n),
        out_specs=(SEM, SEM, *[HBM] * (2 * n), pl.BlockSpec(memory_space=pltpu.VMEM)),
        input_output_aliases={i: 2 + i for i in range(2 * n)},
        compiler_params=pltpu.CompilerParams(has_side_effects=EFFECT),
    )(*[pltpu.with_memory_space_constraint(v, pltpu.HBM) for v in bufs])


def _push_wait(handle, wins, after, *, name):
    send_sems, recv_sems, *bufs, _ = handle
    n = len(bufs) // 2

    def body(*refs):
        src = refs[:n]
        land = refs[n:2 * n]
        for cp in _peer_copies(n, wins, src, land, refs[2 * n], refs[2 * n + 1]):
            cp.wait_send()
            cp.wait_recv()

    outs = pl.pallas_call(
        body, name=name,
        out_shape=tuple(pltpu.HBM(v.shape, v.dtype) for v in bufs),
        in_specs=[HBM] * (2 * n) + [SEM, SEM, ANY],
        out_specs=tuple([HBM] * (2 * n)),
        input_output_aliases={i: i for i in range(2 * n)},
        compiler_params=pltpu.CompilerParams(has_side_effects=EFFECT),
    )(*bufs, send_sems, recv_sems, after)
    return outs[:n], outs[n:]


def _gather_lead(src, land, me, peer, k):
    return src, land.at[me]


def _gather_cols(width):
    def win(src, land, me, peer, k):
        return src, land.at[:, pl.ds(me * width, width)]
    return win


def _scatter_lead(src, land, me, peer, k):
    return src.at[peer], land.at[k - 1]


def _scatter_cols(width):
    def win(src, land, me, peer, k):
        return src.at[:, pl.ds(peer * width, width)], land.at[k - 1]
    return win


def _place_block(own, *, cols, name):
    rows, width = own.shape
    tr = _tile(rows, 512, 2 * SUBLANES)
    _, _, _, me = _mesh_pos()

    def body(me_ref, x_ref, o_ref):
        o_ref[...] = x_ref[...]

    if cols:
        out_spec = pl.BlockSpec((tr, width), lambda i, me_ref: (i, me_ref[0]))
        shape = (rows, N_DEV * width)
    else:
        out_spec = pl.BlockSpec((None, tr, width), lambda i, me_ref: (me_ref[0], i, 0))
        shape = (N_DEV, rows, width)
    return pl.pallas_call(
        body, name=name,
        grid_spec=pltpu.PrefetchScalarGridSpec(
            num_scalar_prefetch=1, grid=(rows // tr,),
            in_specs=[pl.BlockSpec((tr, width), lambda i, me_ref: (i, 0))], out_specs=out_spec),
        out_shape=jax.ShapeDtypeStruct(shape, own.dtype),
        compiler_params=_params(("arbitrary",)),
    )(me.astype(jnp.int32).reshape(1), own)


def _dep(x, token):
    return x + token[0, 0].astype(x.dtype)


def _lead(ref, d):
    return ref.at[d]


def _col_window(width):
    def view(ref, d):
        return ref.at[:, pl.ds(d * width, width)]
    return view


def _pack(arrs):
    flat = jnp.concatenate([a.reshape(-1).astype(F32) for a in arrs])
    n = flat.shape[0]
    rows = -(-n // (2 * SUBLANES * LANES)) * 2 * SUBLANES
    return jnp.pad(flat, (0, rows * LANES - n)).reshape(rows, LANES)


def _unpack(buf, shapes):
    flat = buf.reshape(-1)
    out, off = [], 0
    for s in shapes:
        n = 1
        for q in s:
            n *= q
        out.append(flat[off:off + n].reshape(s))
        off += n
    return out


def _blockdiag(w, cw):
    h, hd, _ = w.shape
    per = cw // hd
    wg = w.reshape(h // per, per, hd, hd)
    eye = jnp.eye(per, dtype=w.dtype)
    blk = jnp.einsum("gpij,pq->gpiqj", wg, eye)
    return blk.reshape(h // per, cw, cw).astype(BF16)


def _blockdiag_extract(g, hd):
    n, cw, _ = g.shape
    per = cw // hd
    g5 = g.reshape(n, per, hd, per, hd)
    idx = jnp.arange(per)
    return g5[:, idx, :, idx, :].transpose(1, 0, 2, 3).reshape(n * per, hd, hd)


def kernel(x, meta, norm_g, w_in, conv_a_w, conv_a_b, lru_wr, lru_br, lru_wi, lru_bi, lru_lambda, conv_b_w, w_out, final_g, loss_target, m_meta, m_norm_g, m_w_in, m_conv_a_w, m_conv_a_b, m_lru_wr, m_lru_br, m_lru_wi, m_lru_bi, m_lru_lambda, m_conv_b_w, m_w_out, m_final_g, v_meta, v_norm_g, v_w_in, v_conv_a_w, v_conv_a_b, v_lru_wr, v_lru_br, v_lru_wi, v_lru_bi, v_lru_lambda, v_conv_b_w, v_w_out, v_final_g):
    _, seq, d = x.shape
    n_meta = meta.shape[0]
    depth = w_in.shape[0]
    din = w_in.shape[2] * N_DEV
    dl = din // 6
    dmix = 2 * dl
    wcol = w_in.shape[2]
    wrow = w_out.shape[1]
    mcol = meta.shape[1]
    ccol = conv_a_w.shape[2]
    heads, hd = lru_wr.shape[1], lru_wr.shape[2]
    n_tok = n_meta + seq
    tp = -(-n_tok // TOKEN_TILE) * TOKEN_TILE
    me = 4 * lax.axis_index("x") + 2 * lax.axis_index("y") + lax.axis_index("c")

    bf = lambda a: a.astype(BF16)
    small_mine = _pack([meta, conv_a_w, conv_b_w])
    first = _all_gather([bf(w_in[0]), small_mine], [(d, din), (N_DEV,) + small_mine.shape],
                        [_col_window(wcol), _lead], name="gather_first")
    flat = first[1].reshape(N_DEV, -1)
    sizes = [meta.size, conv_a_w.size, conv_b_w.size]
    meta_full = jnp.moveaxis(flat[:, :sizes[0]].reshape(N_DEV, n_meta, mcol), 0, 1).reshape(n_meta, d)
    wa_full = jnp.moveaxis(flat[:, sizes[0]:sizes[0] + sizes[1]].reshape(N_DEV, depth, 4, ccol), 0, 2) \
        .reshape(depth, 4, dl)
    wb_full = jnp.moveaxis(flat[:, sizes[0] + sizes[1]:sum(sizes)].reshape(N_DEV, depth, 3, ccol), 0, 2) \
        .reshape(depth, 3, dl)
    w_in_full = [None] * depth
    w_out_full = [None] * depth

    push_out = [None] * depth
    push_in = [None] * depth
    w_in_full[0], src = lax.optimization_barrier((first[0], bf(w_out[0])))
    push_out[0] = _push_start([src], [_place_block(src, cols=False, name="place_wout_0")], [_gather_lead],
                              name="gather_wout_0_start")
    token = push_out[0][-1]
    for l in range(1, depth):
        src = bf(_dep(w_in[l], token))
        push_in[l] = _push_start([src], [_place_block(src, cols=True, name=f"place_win_{l}")], [_gather_cols(wcol)],
                                 name=f"gather_win_{l}_start")
        src = bf(_dep(w_out[l], push_in[l][-1]))
        push_out[l] = _push_start([src], [_place_block(src, cols=False, name=f"place_wout_{l}")], [_gather_lead],
                                  name=f"gather_wout_{l}_start")
        token = push_out[l][-1]

    wr_blk = [_blockdiag(lru_wr[l], GATE_BLOCK) for l in range(depth)]
    wi_blk = [_blockdiag(lru_wi[l], GATE_BLOCK) for l in range(depth)]
    vec = lambda a: a.reshape(1, dl)

    tm = _tile(tp, 1408)
    saved = []
    for l in range(depth):
        if l == 0:
            h, hn = _rms_fwd_first(x[0], meta_full, _dep(norm_g[l], token), tp=tp, name=f"rms_fwd_{l}")
        else:
            hn = _rms_fwd(h, norm_g[l], name=f"rms_fwd_{l}")
        if l > 0:
            _, landed = _push_wait(push_in[l], [_gather_cols(wcol)], hn, name=f"gather_win_{l}_wait")
            w_in_full[l] = landed[0]
        u = _matmul(hn, w_in_full[l], tm=tm, tn=_tile(din, 1536), tk=d, name=f"mm_u_{l}")
        mixed, y = _mixer_fwd(u, wa_full[l], vec(conv_a_b[l]), wr_blk[l], vec(lru_br[l]), wi_blk[l], vec(lru_bi[l]),
                              vec(lru_lambda[l]), wb_full[l], name=f"mixer_fwd_{l}")
        _, landed = _push_wait(push_out[l], [_gather_lead], y, name=f"gather_wout_{l}_wait")
        w_out_full[l] = landed[0].reshape(dmix, d)
        h_next = _matmul(y, w_out_full[l], tm=tm, tn=_tile(d, 512), tk=dmix, add=h, name=f"mm_out_{l}")
        saved.append((h, hn, u, mixed, y))
        h = h_next

    dh, dhb, dg_final, loss_part = _loss_head(h, loss_target[0], final_g, n_meta=n_meta, n_tok=n_tok,
                                              name="loss_head")

    small_grads = [None] * depth
    sent_out = [None] * depth
    sent_in = [None] * depth
    scatter_in = [_scatter_cols(wcol)]
    token = None
    dg_norms = []
    for l in reversed(range(depth)):
        h_in, hn, u, mixed, y = saved[l]
        dy = _matmul(dhb, w_out_full[l], tb=True, tm=tm, tn=_tile(dmix, 1024), tk=d, dep=token, name=f"mm_dy_{l}")
        dw_out = _matmul(y, dhb, ta=True, tm=_tile(dmix, 1024), tn=_tile(d, 1024), tk=tp, out_dtype=BF16,
                         name=f"mm_dwout_{l}")
        sent_out[l] = _push_start([dw_out.reshape(N_DEV, wrow, d)], [lax.empty((N_DEV - 1, wrow, d), BF16)],
                                  [_scatter_lead], name=f"scatter_wout_{l}_start")
        du, sg, dwr, dwi = _mixer_bwd(u, mixed, dy, wa_full[l], wr_blk[l], vec(lru_br[l]), wi_blk[l], vec(lru_bi[l]),
                                      vec(lru_lambda[l]), _dep(wb_full[l], sent_out[l][-1]), name=f"mixer_bwd_{l}")
        small_grads[l] = (sg, dwr, dwi)
        if l == 0:
            rows = jnp.stack([small_grads[j][0] for j in range(depth)])
            early = [_pack([
                rows[:, SG_BA], rows[:, SG_BR], rows[:, SG_BI], rows[:, SG_LAM], rows[:, SG_WA:SG_WA + 4],
                rows[:, SG_WB:SG_WB + 3], dg_final[0], *dg_norms]),
                _pack([jnp.stack([_blockdiag_extract(small_grads[j][1], hd) for j in range(depth)]),
                       jnp.stack([_blockdiag_extract(small_grads[j][2], hd) for j in range(depth)])]).astype(BF16)]
            early_land = [lax.dynamic_update_slice(lax.empty((N_DEV,) + a.shape, a.dtype), a[None], (me, 0, 0))
                          for a in early]
            sent_early = _push_start(early, early_land, [_gather_lead] * 2, name="gather_early_grads_start")
        parts = 2 if l == 0 else 1
        token = sent_early[-1] if l == 0 else None
        sent_in[l] = []
        for p in range(parts):
            dw_in = _matmul(hn, du, ta=True, tm=_tile(d // parts, 512), tn=_tile(din, 1536), tk=tp, out_dtype=BF16,
                            dep=token, m_part=(p, parts), name=f"mm_dwin_{l}_{p}")
            sent_in[l].append(_push_start([dw_in], [lax.empty((N_DEV - 1, d // parts, wcol), BF16)], scatter_in,
                                          name=f"scatter_win_{l}_{p}_start"))
            token = sent_in[l][-1][-1]
        dhn = _matmul(du, w_in_full[l], tb=True, tm=_tile(tp, 528, 2 * SUBLANES), tn=_tile(d, 1024), tk=din, dep=token,
                      name=f"mm_dhn_{l}")
        if l > 0:
            dh, dhb, dg_norm = _rms_bwd(h_in, dhn, dh, norm_g[l], name=f"rms_bwd_{l}")
            dg_norms.append(dg_norm[0])
        else:
            grad_x, d_meta, dg_norm = _rms_bwd_first(h_in, dhn, dh, norm_g[l], n_meta=n_meta, seq=seq,
                                                     name=f"rms_bwd_{l}")

    big = {"win": None, "wout": None}

    def big_adamw(l, after):
        src, landed = _push_wait(sent_out[l], [_scatter_lead], after, name=f"scatter_wout_{l}_wait")
        own = lax.dynamic_index_in_dim(src[0], me, 0, keepdims=False)
        big["wout"] = _adamw(w_out, own, m_w_out, v_w_out, landed=landed[0], layer=l, depth=depth,
                             into=big["wout"], name=f"adamw_w_out_{l}")
        after = big["wout"][0]
        for p, sent in enumerate(sent_in[l]):
            src, landed = _push_wait(sent, scatter_in, after, name=f"scatter_win_{l}_{p}_wait")
            own = lax.dynamic_slice_in_dim(src[0], me * wcol, wcol, axis=1)
            big["win"] = _adamw(w_in, own, m_w_in, v_w_in, landed=landed[0], layer=l, depth=depth,
                                into=big["win"], row_off=p * own.shape[0], name=f"adamw_w_in_{l}_{p}")
            after = big["win"][0]
        return after

    after = dg_norm
    for l in reversed(range(1, depth)):
        after = big_adamw(l, after)

    late = _pack([dg_norm[0], d_meta, loss_part[0:1, 0:1]])
    if depth > 1:
        late, after = lax.optimization_barrier((late, after))
    late_all = _all_gather([late], [(N_DEV,) + late.shape], [_lead], name="gather_late_grads")[0]
    late_sum = _unpack(_slot_sum(late_all, name="sum_late_grads"), [(d,), (n_meta, d), ()])
    loss = late_sum[2]
    _, early_all = _push_wait(sent_early, [_gather_lead] * 2, late_sum[0], name="gather_early_grads_wait")
    vec_shapes = [conv_a_b.shape, lru_br.shape, lru_bi.shape, lru_lambda.shape, (depth, 4, dl), (depth, 3, dl),
                  final_g.shape] + [(d,)] * (depth - 1)
    e = _unpack(_slot_sum(early_all[0], name="sum_early_vectors"), vec_shapes)
    g_wr, g_wi = _unpack(_slot_sum(early_all[1], name="sum_early_maps"), [lru_wr.shape, lru_wi.shape])
    g_norm = jnp.stack([late_sum[0]] + e[7:][::-1])
    g_meta = lax.dynamic_slice_in_dim(late_sum[1], me * mcol, mcol, axis=1)
    g_wa = lax.dynamic_slice_in_dim(e[4], me * ccol, ccol, axis=2)
    g_wb = lax.dynamic_slice_in_dim(e[5], me * ccol, ccol, axis=2)

    small_w = [norm_g, conv_a_b, lru_wr, lru_br, lru_wi, lru_bi, lru_lambda, final_g, meta, conv_a_w, conv_b_w]
    small_m = [m_norm_g, m_conv_a_b, m_lru_wr, m_lru_br, m_lru_wi, m_lru_bi, m_lru_lambda, m_final_g, m_meta,
               m_conv_a_w, m_conv_b_w]
    small_v = [v_norm_g, v_conv_a_b, v_lru_wr, v_lru_br, v_lru_wi, v_lru_bi, v_lru_lambda, v_final_g, v_meta,
               v_conv_a_w, v_conv_b_w]
    small_g = [g_norm, e[0], g_wr, e[1], g_wi, e[2], e[3], e[6], g_meta, g_wa, g_wb]
    small_out = _adamw(_pack(small_w), _pack(small_g), _pack(small_m), _pack(small_v), name="adamw_small")
    small_shapes = [a.shape for a in small_w]
    s_grad, s_delta, s_m, s_v = [_unpack(o, small_shapes) for o in small_out]

    big_adamw(0, small_out[0])
    win_out, wout_out = big["win"], big["wout"]

    names = ["norm_g", "conv_a_b", "lru_wr", "lru_br", "lru_wi", "lru_bi", "lru_lambda", "final_g", "meta",
             "conv_a_w", "conv_b_w"]
    order = ["meta", "norm_g", "w_in", "conv_a_w", "conv_a_b", "lru_wr", "lru_br", "lru_wi", "lru_bi", "lru_lambda",
             "conv_b_w", "w_out", "final_g"]

    def family(idx, small):
        table = {nm: small[i] for i, nm in enumerate(names)}
        table["w_in"] = win_out[idx]
        table["w_out"] = wout_out[idx]
        return [table[nm] for nm in order]

    return (loss, grad_x, *family(0, s_grad), *family(1, s_delta), *family(2, s_m), *family(3, s_v))
```

```python
import jax
import jax.numpy as jnp
from jax import lax
from jax.experimental import pallas as pl
from jax.experimental.pallas import tpu as pltpu

F32 = jnp.float32
BF16 = jnp.bfloat16
MESH = pl.DeviceIdType.MESH

N_DEV = 8
RMS_EPS = 1e-6
LRU_C = 8.0
ADAM_LR = 0.001
ADAM_B1 = 0.9
ADAM_B2 = 0.999
ADAM_EPS = 1e-08
ADAM_WD = 0.01
ADAM_STEP = 10

V7X_VMEM_LIMIT = 52 * 1024 * 1024
LANES = 128
SUBLANES = 8
TOKEN_TILE = 384
MIX_ROWS = 128
MIX_SUBTILES = 3
SHIFTED_ROWS = 128
GATE_BLOCK = 128


def _params(sem):
    return pltpu.CompilerParams(dimension_semantics=sem, vmem_limit_bytes=V7X_VMEM_LIMIT)


def _tile(n, target, align=LANES):
    best = None
    for t in range(align, min(n, target) + 1, align):
        if n % t == 0:
            best = t
    return n if best is None else best


def _sigmoid(z):
    return 0.5 * jnp.tanh(0.5 * z) + 0.5


def _softplus(z):
    e = jnp.exp(-jnp.abs(z))
    u = 1.0 + e
    l1p = jnp.where(u == 1.0, e, jnp.log(u) * e / jnp.where(u == 1.0, 1.0, u - 1.0))
    return jnp.maximum(z, 0.0) + l1p


def _matmul(a, b, *, ta=False, tb=False, tm, tn, tk, out_dtype=F32, add=None, dep=None, m_part=None, name):
    m, k = (a.shape[1], a.shape[0]) if ta else a.shape
    m_off = 0
    if m_part is not None:
        assert add is None and m % (m_part[1] * tm) == 0
        m //= m_part[1]
        m_off = m_part[0] * (m // tm)
    n, kb = b.shape if tb else b.shape[::-1]
    assert kb == k
    assert m % tm == 0 and n % tn == 0 and k % tk == 0, (m, n, k, tm, tn, tk)
    nk = k // tk
    a_spec = pl.BlockSpec((tk, tm), lambda i, j, q: (q, i + m_off)) if ta \
        else pl.BlockSpec((tm, tk), lambda i, j, q: (i + m_off, q))
    b_spec = pl.BlockSpec((tn, tk), lambda i, j, q: (j, q)) if tb else pl.BlockSpec((tk, tn), lambda i, j, q: (q, j))
    o_spec = pl.BlockSpec((tm, tn), lambda i, j, q: (i, j))
    o_shape = (m, n)
    dims = (((0 if ta else 1,), (1 if tb else 0,)), ((), ()))
    has_add = add is not None
    has_dep = dep is not None

    def body(*refs):
        if has_dep:
            refs = refs[:-3] + refs[-2:]
        if has_add:
            a_ref, b_ref, add_ref, o_ref, acc_ref = refs
        else:
            a_ref, b_ref, o_ref, acc_ref = refs
        q = pl.program_id(2)
        part = lax.dot_general(a_ref[...], b_ref[...], dims, preferred_element_type=F32)

        def finish(acc):
            if has_add:
                acc = acc + add_ref[...]
            o_ref[...] = acc.astype(out_dtype)

        if nk == 1:
            finish(part)
        else:
            @pl.when(q == 0)
            def _():
                acc_ref[...] = part

            @pl.when(jnp.logical_and(q > 0, q < nk - 1))
            def _():
                acc_ref[...] += part

            @pl.when(q == nk - 1)
            def _():
                finish(acc_ref[...] + part)

    in_specs = [a_spec, b_spec] + ([o_spec] if has_add else [])
    args = (a, b) + ((add,) if has_add else ())
    if has_dep:
        in_specs.append(pl.BlockSpec((SUBLANES, LANES), lambda i, j, q: (0, 0)))
        args += (dep,)
    acc_shape = (tm, tn) if nk > 1 else (SUBLANES, LANES)
    return pl.pallas_call(
        body, name=name,
        grid=(m // tm, n // tn, nk),
        in_specs=in_specs, out_specs=o_spec,
        out_shape=jax.ShapeDtypeStruct(o_shape, out_dtype),
        scratch_shapes=[pltpu.VMEM(acc_shape, F32)],
        compiler_params=_params(("parallel", "parallel", "arbitrary")),
    )(*args)


def _rms_fwd(h, g, *, name):
    tp, d = h.shape
    tr = _tile(tp, 512, SUBLANES)

    def body(h_ref, g_ref, o_ref):
        hv = h_ref[...]
        rstd = lax.rsqrt(jnp.mean(hv * hv, axis=-1, keepdims=True) + RMS_EPS)
        o_ref[...] = (hv * rstd * g_ref[...]).astype(BF16)

    return pl.pallas_call(
        body, name=name, grid=(tp // tr,),
        in_specs=[pl.BlockSpec((tr, d), lambda i: (i, 0)), pl.BlockSpec((1, d), lambda i: (0, 0))],
        out_specs=pl.BlockSpec((tr, d), lambda i: (i, 0)),
        out_shape=jax.ShapeDtypeStruct((tp, d), BF16),
        compiler_params=_params(("parallel",)),
    )(h, g.reshape(1, d))


def _rms_fwd_first(x, meta, g, *, tp, name):
    seq, d = x.shape
    n_meta = meta.shape[0]
    n_tok = n_meta + seq
    tr = SHIFTED_ROWS
    assert tp % tr == 0 and seq % tr == 0 and tr % n_meta == 0
    per = tr // n_meta

    def body(x_ref, xp_ref, m_ref, g_ref, h_ref, o_ref):
        i = pl.program_id(0)
        head = jnp.where(i == 0, m_ref[...], xp_ref[...])
        rows = i * tr + lax.broadcasted_iota(jnp.int32, (tr, 1), 0)
        hv = jnp.where(rows < n_tok, jnp.concatenate([head, x_ref[:tr - n_meta, :]], axis=0), 0.0)
        h_ref[...] = hv
        rstd = lax.rsqrt(jnp.mean(hv * hv, axis=-1, keepdims=True) + RMS_EPS)
        o_ref[...] = (hv * rstd * g_ref[...]).astype(BF16)

    row = pl.BlockSpec((tr, d), lambda i: (i, 0))
    own = pl.BlockSpec((tr, d), lambda i: (jnp.minimum(i, seq // tr - 1), 0))
    before = pl.BlockSpec((n_meta, d), lambda i: (jnp.maximum(i * per - 1, 0), 0))
    return pl.pallas_call(
        body, name=name, grid=(tp // tr,),
        in_specs=[own, before, pl.BlockSpec((n_meta, d), lambda i: (0, 0)), pl.BlockSpec((1, d), lambda i: (0, 0))],
        out_specs=[row, row],
        out_shape=[jax.ShapeDtypeStruct((tp, d), F32), jax.ShapeDtypeStruct((tp, d), BF16)],
        compiler_params=_params(("parallel",)),
    )(x, x, meta, g.reshape(1, d))


def _rms_bwd(h, dhn, dout, g, *, name):
    tp, d = h.shape
    tr = _tile(tp, 528, 2 * SUBLANES)

    def body(h_ref, dhn_ref, dout_ref, g_ref, dh_ref, dhb_ref, dg_ref):
        hv = h_ref[...]
        rstd = lax.rsqrt(jnp.mean(hv * hv, axis=-1, keepdims=True) + RMS_EPS)
        xhat = hv * rstd
        dn = dhn_ref[...]
        dxhat = dn * g_ref[...]
        dh = dout_ref[...] + rstd * (dxhat - xhat * jnp.mean(dxhat * xhat, axis=-1, keepdims=True))
        dh_ref[...] = dh
        dhb_ref[...] = dh.astype(BF16)
        part = jnp.sum(dn * xhat, axis=0, keepdims=True)

        @pl.when(pl.program_id(0) == 0)
        def _():
            dg_ref[...] = part

        @pl.when(pl.program_id(0) > 0)
        def _():
            dg_ref[...] += part

    row = pl.BlockSpec((tr, d), lambda i: (i, 0))
    vec = pl.BlockSpec((1, d), lambda i: (0, 0))
    return pl.pallas_call(
        body, name=name, grid=(tp // tr,),
        in_specs=[row, row, row, vec],
        out_specs=[row, row, vec],
        out_shape=[jax.ShapeDtypeStruct((tp, d), F32), jax.ShapeDtypeStruct((tp, d), BF16),
                   jax.ShapeDtypeStruct((1, d), F32)],
        compiler_params=_params(("arbitrary",)),
    )(h, dhn, dout, g.reshape(1, d))


def _rms_bwd_first(h, dhn, dout, g, *, n_meta, seq, name):
    tp, d = h.shape
    tr = SHIFTED_ROWS
    assert seq % tr == 0 and tr % n_meta == 0 and tp >= seq + n_meta
    nt = seq // tr
    per = tr // n_meta

    def grads(hv, dn, do, gv):
        rstd = lax.rsqrt(jnp.mean(hv * hv, axis=-1, keepdims=True) + RMS_EPS)
        xhat = hv * rstd
        dxhat = dn * gv
        dh = do + rstd * (dxhat - xhat * jnp.mean(dxhat * xhat, axis=-1, keepdims=True))
        return dh, jnp.sum(dn * xhat, axis=0, keepdims=True)

    def body(h_ref, dhn_ref, dout_ref, hn_ref, dhnn_ref, doutn_ref, g_ref, gx_ref, dmeta_ref, dg_ref):
        i = pl.program_id(0)
        gv = g_ref[...]
        dh, part = grads(h_ref[...], dhn_ref[...], dout_ref[...], gv)
        dh_next, part_next = grads(hn_ref[...], dhnn_ref[...], doutn_ref[...], gv)
        gx_ref[...] = jnp.concatenate([dh[n_meta:], dh_next], axis=0)

        @pl.when(i == 0)
        def _():
            dmeta_ref[...] = dh[:n_meta]
            dg_ref[...] = part

        @pl.when(i > 0)
        def _():
            dg_ref[...] += part

        @pl.when(i == nt - 1)
        def _():
            dg_ref[...] += part_next

    row = pl.BlockSpec((tr, d), lambda i: (i, 0))
    nxt = pl.BlockSpec((n_meta, d), lambda i: ((i + 1) * per, 0))
    vec = pl.BlockSpec((1, d), lambda i: (0, 0))
    return pl.pallas_call(
        body, name=name, grid=(nt,),
        in_specs=[row, row, row, nxt, nxt, nxt, vec],
        out_specs=[pl.BlockSpec((None, tr, d), lambda i: (0, i, 0)), pl.BlockSpec((n_meta, d), lambda i: (0, 0)), vec],
        out_shape=[jax.ShapeDtypeStruct((1, seq, d), F32), jax.ShapeDtypeStruct((n_meta, d), F32),
                   jax.ShapeDtypeStruct((1, d), F32)],
        compiler_params=_params(("arbitrary",)),
    )(h, dhn, dout, h, dhn, dout, g.reshape(1, d))


def _loss_head(h, tgt, g, *, n_meta, n_tok, name):
    tp, d = h.shape
    seq = tgt.shape[0]
    tr = SHIFTED_ROWS
    assert tp % tr == 0 and seq % tr == 0 and tr % n_meta == 0
    per = tr // n_meta

    def body(h_ref, t_ref, tp_ref, g_ref, dh_ref, dhb_ref, dg_ref, loss_ref):
        i = pl.program_id(0)
        hv = h_ref[...]
        rstd = lax.rsqrt(jnp.mean(hv * hv, axis=-1, keepdims=True) + RMS_EPS)
        xhat = hv * rstd
        gv = g_ref[...]
        rows = i * tr + lax.broadcasted_iota(jnp.int32, (tr, 1), 0)
        valid = jnp.logical_and(rows >= n_meta, rows < n_tok)
        target = jnp.concatenate([tp_ref[...], t_ref[:tr - n_meta, :]], axis=0)
        err = jnp.where(valid, xhat * gv - target, 0.0)
        dy = err * (1.0 / d)
        dxhat = dy * gv
        dh = rstd * (dxhat - xhat * jnp.mean(dxhat * xhat, axis=-1, keepdims=True))
        dh_ref[...] = dh
        dhb_ref[...] = dh.astype(BF16)
        dg_part = jnp.sum(dy * xhat, axis=0, keepdims=True)
        per_row = jnp.sum(err * err, axis=-1, keepdims=True) * (1.0 / d)
        loss_part = jnp.broadcast_to(0.5 * jnp.sum(per_row, axis=0, keepdims=True), (SUBLANES, LANES))

        @pl.when(i == 0)
        def _():
            dg_ref[...] = dg_part
            loss_ref[...] = loss_part

        @pl.when(i > 0)
        def _():
            dg_ref[...] += dg_part
            loss_ref[...] += loss_part

    row = pl.BlockSpec((tr, d), lambda i: (i, 0))
    vec = pl.BlockSpec((1, d), lambda i: (0, 0))
    own = pl.BlockSpec((tr, d), lambda i: (jnp.minimum(i, seq // tr - 1), 0))
    before = pl.BlockSpec((n_meta, d), lambda i: (jnp.maximum(i * per - 1, 0), 0))
    return pl.pallas_call(
        body, name=name, grid=(tp // tr,),
        in_specs=[row, own, before, vec],
        out_specs=[row, row, vec, pl.BlockSpec((SUBLANES, LANES), lambda i: (0, 0))],
        out_shape=[jax.ShapeDtypeStruct((tp, d), F32), jax.ShapeDtypeStruct((tp, d), BF16),
                   jax.ShapeDtypeStruct((1, d), F32), jax.ShapeDtypeStruct((SUBLANES, LANES), F32)],
        compiler_params=_params(("arbitrary",)),
    )(h, tgt, tgt, g.reshape(1, d))


def _shift_down(halo, tile, s):
    if s == 0:
        return tile
    ext = jnp.concatenate([halo, tile], axis=0)
    return pltpu.roll(ext, s, 0)[SUBLANES:]


def _shift_up(tile, head, s):
    if s == 0:
        return tile
    ext = jnp.concatenate([tile, head], axis=0)
    n = ext.shape[0]
    return pltpu.roll(ext, n - s, 0)[: tile.shape[0]]


def _to_lane_blocks(ref, cols, val):
    for j in range(cols.start // LANES, cols.stop // LANES):
        ref[j] = val[:, j * LANES - cols.start:(j + 1) * LANES - cols.start]


def _from_lane_blocks(ref, cols):
    return jnp.concatenate([ref[j] for j in range(cols.start // LANES, cols.stop // LANES)], axis=1)


def _scan_tile(a_ref, b_ref, out_ref, carry, j, *, reverse):
    ng = a_ref.shape[1] // SUBLANES
    order = list(range(SUBLANES))[::-1] if reverse else list(range(SUBLANES))

    def rows(r):
        return pl.ds(r, ng, stride=SUBLANES)

    prod, loc = {}, {}
    prev = None
    for r in order:
        ar = a_ref[j, rows(r), :]
        br = b_ref[j, rows(r), :]
        prod[r] = ar if prev is None else ar * prod[prev]
        loc[r] = br if prev is None else ar * loc[prev] + br
        prev = r
    pg, lg = prod[prev], loc[prev]
    ones = jnp.ones((SUBLANES,) + pg.shape[1:], F32)
    zeros = jnp.zeros_like(ones)
    s = 1
    while s < ng:
        p_sh = _shift_up(pg, ones, s) if reverse else _shift_down(ones, pg, s)
        l_sh = _shift_up(lg, zeros, s) if reverse else _shift_down(zeros, lg, s)
        lg = pg * l_sh + lg
        pg = pg * p_sh
        s *= 2
    leaving = pg * carry[0:1, :] + lg
    entering = _shift_up(leaving, carry, 1) if reverse else _shift_down(carry, leaving, 1)
    for r in order:
        out_ref[j, rows(r), :] = loc[r] + prod[r] * entering
    last = leaving[0:1, :] if reverse else leaving[ng - 1:ng, :]
    return jnp.broadcast_to(last, carry.shape)


def _gates(ca, wr, wi, br, bi, sp):
    cab = ca.astype(BF16)
    r = _sigmoid(jnp.dot(cab, wr, preferred_element_type=F32) + br)
    ig = _sigmoid(jnp.dot(cab, wi, preferred_element_type=F32) + bi)
    la = -LRU_C * r * sp
    a = jnp.exp(la)
    mult = jnp.sqrt(-jnp.tanh(la) * (a * a + 1.0))
    return r, ig, a, mult


def _mixer_fwd(u, wa, ba, wr_blk, br, wi_blk, bi, lam, wb, *, name):
    tp, din = u.shape
    dl = din // 6
    tt = MIX_ROWS
    cw = GATE_BLOCK
    nch = dl // cw
    assert tp % tt == 0 and dl % cw == 0

    def body(u_ref, wa_ref, ba_ref, wr_ref, br_ref, wi_ref, bi_ref, lam_ref, wb_ref,
             s_ref, y_ref, xa_tail, v_tail, h_carry, a_s, b_s, h_s):
        @pl.when(pl.program_id(0) == 0)
        def _():
            xa_tail[...] = jnp.zeros_like(xa_tail)
            v_tail[...] = jnp.zeros_like(v_tail)
            h_carry[...] = jnp.zeros_like(h_carry)

        for sub in range(MIX_SUBTILES):
            rows = slice(sub * tt, (sub + 1) * tt)
            for ch in range(nch):
                cs = slice(ch * cw, (ch + 1) * cw)

                def seg(s):
                    return slice(s * dl + ch * cw, s * dl + (ch + 1) * cw)

                xa = u_ref[rows, seg(0)]
                halo = xa_tail[:, cs]
                ca = ba_ref[:, cs] + wa_ref[3:4, cs] * xa
                for kk in range(3):
                    ca = ca + wa_ref[kk:kk + 1, cs] * _shift_down(halo, xa, 3 - kk)
                xa_tail[:, cs] = xa[tt - SUBLANES:]
                s_ref[rows, cs] = ca
                sp = _softplus(-lam_ref[:, cs])
                _, ig, a, mult = _gates(ca, wr_ref[ch], wi_ref[ch], br_ref[:, cs], bi_ref[:, cs], sp)
                _to_lane_blocks(a_s, cs, a)
                _to_lane_blocks(b_s, cs, mult * (ig * ca))

                bv = u_ref[rows, seg(2)]
                v = u_ref[rows, seg(3)] * u_ref[rows, seg(4)]
                gb = u_ref[rows, seg(5)]
                vh = v_tail[:, cs]
                cb = wb_ref[2:3, cs] * v
                for kk in range(2):
                    cb = cb + wb_ref[kk:kk + 1, cs] * _shift_down(vh, v, 2 - kk)
                v_tail[:, cs] = v[tt - SUBLANES:]
                y_ref[rows, dl + ch * cw: dl + (ch + 1) * cw] = (bv * cb * (gb * _sigmoid(gb))).astype(BF16)

            for ch in range(nch):
                cs = slice(ch * cw, (ch + 1) * cw)
                for j in range(cs.start // LANES, cs.stop // LANES):
                    lanes = slice(j * LANES, (j + 1) * LANES)
                    h_carry[:, lanes] = _scan_tile(a_s, b_s, h_s, h_carry[:, lanes], j, reverse=False)
                hsv = _from_lane_blocks(h_s, cs)
                s_ref[rows, dl + ch * cw: dl + (ch + 1) * cw] = hsv
                ga = u_ref[rows, dl + ch * cw: dl + (ch + 1) * cw]
                y_ref[rows, cs] = (hsv * (ga * _sigmoid(ga))).astype(BF16)

    tb = tt * MIX_SUBTILES
    assert tp % tb == 0
    row = lambda w: pl.BlockSpec((tb, w), lambda i: (i, 0))
    full = lambda shp: pl.BlockSpec(shp, lambda i: tuple(0 for _ in shp))
    return pl.pallas_call(
        body, name=name, grid=(tp // tb,),
        in_specs=[row(din), full((4, dl)), full((1, dl)), full((nch, cw, cw)), full((1, dl)),
                  full((nch, cw, cw)), full((1, dl)), full((1, dl)), full((3, dl))],
        out_specs=[row(2 * dl), row(2 * dl)],
        out_shape=[jax.ShapeDtypeStruct((tp, 2 * dl), F32), jax.ShapeDtypeStruct((tp, 2 * dl), BF16)],
        scratch_shapes=[pltpu.VMEM((SUBLANES, dl), F32), pltpu.VMEM((SUBLANES, dl), F32),
                        pltpu.VMEM((SUBLANES, dl), F32)] + [pltpu.VMEM((dl // LANES, tt, LANES), F32)] * 3,
        compiler_params=_params(("arbitrary",)),
    )(u, wa, ba, wr_blk, br, wi_blk, bi, lam, wb)


SG_WA, SG_BA, SG_BR, SG_BI, SG_LAM, SG_WB, SG_ROWS = 0, 4, 5, 6, 7, 8, 16


def _mixer_bwd(u, saved, dy, wa, wr_blk, br, wi_blk, bi, lam, wb, *, name):
    tp, din = u.shape
    dl = din // 6
    tt = MIX_ROWS
    cw = GATE_BLOCK
    nch = dl // cw
    tb = tt * MIX_SUBTILES
    assert tp % tb == 0
    nt = tp // tb
    hb = tb // SUBLANES
    tn_dims = (((0,), (0,)), ((), ()))
    nt_dims = (((1,), (1,)), ((), ()))

    def body(u_ref, uh_ref, s_ref, sh_ref, dy_ref, wa_ref, wr_ref, br_ref, wi_ref, bi_ref, lam_ref, wb_ref,
             du_ref, sg_ref, dwr_ref, dwi_ref,
             g_carry, a_head, dca_head, dcb_head, r_s, i_s, a_s, an_s, d_s, g_s):
        i = pl.program_id(0)
        first_tile = i == nt - 1

        @pl.when(i == 0)
        def _():
            for ref in (g_carry, a_head, dca_head, dcb_head, sg_ref, dwr_ref, dwi_ref):
                ref[...] = jnp.zeros_like(ref)

        def halo_of(x):
            return jnp.where(first_tile, 0.0, x)

        for sub in reversed(range(MIX_SUBTILES)):
            rows = slice(sub * tt, (sub + 1) * tt)

            def before(ref, halo_ref, cols):
                if sub == 0:
                    return halo_of(halo_ref[:, cols])
                return ref[sub * tt - SUBLANES:sub * tt, cols]

            for ch in range(nch):
                cs = slice(ch * cw, (ch + 1) * cw)
                cav = s_ref[rows, cs]
                sp = _softplus(-lam_ref[:, cs])
                r, ig, a, _ = _gates(cav, wr_ref[ch], wi_ref[ch], br_ref[:, cs], bi_ref[:, cs], sp)
                r_s[:, cs] = r
                i_s[:, cs] = ig
                a_s[:, cs] = a
                _to_lane_blocks(an_s, cs, _shift_up(a, a_head[:, cs], 1))
                a_head[:, cs] = a[:SUBLANES]
                ga = u_ref[rows, dl + ch * cw: dl + (ch + 1) * cw]
                _to_lane_blocks(d_s, cs, dy_ref[rows, cs] * (ga * _sigmoid(ga)))

            for j in range(dl // LANES):
                lanes = slice(j * LANES, (j + 1) * LANES)
                g_carry[:, lanes] = _scan_tile(an_s, d_s, g_s, g_carry[:, lanes], j, reverse=True)

            for ch in range(nch):
                cs = slice(ch * cw, (ch + 1) * cw)

                def acc_row(r0, val):
                    sg_ref[r0:r0 + 1, cs] += jnp.sum(val, axis=0, keepdims=True)

                def seg(s):
                    return slice(s * dl + ch * cw, s * dl + (ch + 1) * cw)

                cav = s_ref[rows, cs]
                r = r_s[:, cs]
                ig = i_s[:, cs]
                a = a_s[:, cs]
                g = _from_lane_blocks(g_s, cs)
                hsv = s_ref[rows, dl + ch * cw: dl + (ch + 1) * cw]
                lamv = lam_ref[:, cs]
                sp = _softplus(-lamv)
                la = -LRU_C * r * sp
                e2 = a * a
                one_m_e2 = -jnp.tanh(la) * (e2 + 1.0)
                mult = jnp.sqrt(one_m_e2)
                hprev = _shift_down(before(s_ref, sh_ref, slice(dl + ch * cw, dl + (ch + 1) * cw)), hsv, 1)
                icav = ig * cav
                dla = g * (hprev * a - icav * (e2 * lax.rsqrt(one_m_e2)))
                gm = g * mult
                dzi = gm * icav * (1.0 - ig)
                dca = gm * ig
                dla_r = dla * r
                dzr = dla_r * (1.0 - r) * (-LRU_C * sp)
                sg_ref[SG_LAM:SG_LAM + 1, cs] += jnp.sum(dla_r, axis=0, keepdims=True) * (LRU_C * _sigmoid(-lamv))
                acc_row(SG_BR, dzr)
                acc_row(SG_BI, dzi)
                dzr_b = dzr.astype(BF16)
                dzi_b = dzi.astype(BF16)
                cab = cav.astype(BF16)
                dca = dca + lax.dot_general(dzr_b, wr_ref[ch], nt_dims, preferred_element_type=F32)
                dca = dca + lax.dot_general(dzi_b, wi_ref[ch], nt_dims, preferred_element_type=F32)
                dwr_ref[ch] += lax.dot_general(cab, dzr_b, tn_dims, preferred_element_type=F32)
                dwi_ref[ch] += lax.dot_general(cab, dzi_b, tn_dims, preferred_element_type=F32)
                acc_row(SG_BA, dca)
                xa = u_ref[rows, seg(0)]
                head = dca_head[:, cs]
                dxa = wa_ref[3:4, cs] * dca
                acc_row(SG_WA + 3, dca * xa)
                for kk in range(3):
                    later = _shift_up(dca, head, 3 - kk)
                    acc_row(SG_WA + kk, later * xa)
                    dxa = dxa + wa_ref[kk:kk + 1, cs] * later
                dca_head[:, cs] = dca[:SUBLANES]
                ga = u_ref[rows, seg(1)]
                sga = _sigmoid(ga)
                dga = dy_ref[rows, cs] * hsv * (sga + (ga * sga) * (1.0 - sga))
                du_ref[rows, seg(0)] = dxa.astype(BF16)
                du_ref[rows, seg(1)] = dga.astype(BF16)

                bv = u_ref[rows, seg(2)]
                cv = u_ref[rows, seg(3)]
                xb = u_ref[rows, seg(4)]
                gb = u_ref[rows, seg(5)]
                dyb = dy_ref[rows, dl + ch * cw: dl + (ch + 1) * cw]
                v = cv * xb
                vh = before(u_ref, uh_ref, seg(3)) * before(u_ref, uh_ref, seg(4))
                v1 = _shift_down(vh, v, 1)
                v2 = _shift_down(vh, v, 2)
                cb = wb_ref[2:3, cs] * v + wb_ref[1:2, cs] * v1 + wb_ref[0:1, cs] * v2
                sgb = _sigmoid(gb)
                sl = gb * sgb
                dyb_b = dyb * bv
                dyb_cb = dyb * cb
                dcb = dyb_b * sl
                du_ref[rows, seg(2)] = (dyb_cb * sl).astype(BF16)
                du_ref[rows, seg(5)] = (dyb_cb * bv * (sgb + sl * (1.0 - sgb))).astype(BF16)
                bhead = dcb_head[:, cs]
                dv = wb_ref[2:3, cs] * dcb
                acc_row(SG_WB + 2, dcb * v)
                for kk in range(2):
                    later = _shift_up(dcb, bhead, 2 - kk)
                    acc_row(SG_WB + kk, later * v)
                    dv = dv + wb_ref[kk:kk + 1, cs] * later
                dcb_head[:, cs] = dcb[:SUBLANES]
                du_ref[rows, seg(3)] = (dv * xb).astype(BF16)
                du_ref[rows, seg(4)] = (dv * cv).astype(BF16)

    rev = lambda w: pl.BlockSpec((tb, w), lambda i: (nt - 1 - i, 0))
    halo = lambda w: pl.BlockSpec((SUBLANES, w), lambda i: (jnp.maximum((nt - 1 - i) * hb - 1, 0), 0))
    full = lambda shp: pl.BlockSpec(shp, lambda i: tuple(0 for _ in shp))
    vm = lambda r: pltpu.VMEM((r, dl), F32)
    return pl.pallas_call(
        body, name=name, grid=(nt,),
        in_specs=[rev(din), halo(din), rev(2 * dl), halo(2 * dl), rev(2 * dl), full((4, dl)),
                  full((nch, cw, cw)), full((1, dl)), full((nch, cw, cw)), full((1, dl)), full((1, dl)), full((3, dl))],
        out_specs=[rev(din), full((SG_ROWS, dl)), full((nch, cw, cw)), full((nch, cw, cw))],
        out_shape=[jax.ShapeDtypeStruct((tp, din), BF16), jax.ShapeDtypeStruct((SG_ROWS, dl), F32),
                   jax.ShapeDtypeStruct((nch, cw, cw), F32), jax.ShapeDtypeStruct((nch, cw, cw), F32)],
        scratch_shapes=[vm(SUBLANES), vm(SUBLANES), vm(SUBLANES), vm(SUBLANES), vm(tt), vm(tt), vm(tt)]
        + [pltpu.VMEM((dl // LANES, tt, LANES), F32)] * 3,
        compiler_params=_params(("arbitrary",)),
    )(u, u, saved, saved, dy, wa, wr_blk, br, wi_blk, bi, lam, wb)


def _adamw(w, g, m, v, *, name, landed=None, layer=None, depth=None, into=None, row_off=0):
    r, c = w.shape[-2:]
    rows = g.shape[0]
    tr = _tile(rows, 512, 2 * SUBLANES)
    assert row_off % tr == 0
    boff = row_off // tr
    bc1 = 1.0 - ADAM_B1 ** ADAM_STEP
    bc2 = 1.0 - ADAM_B2 ** ADAM_STEP
    slots = landed is not None

    def body(*refs):
        if into is not None:
            refs = refs[:-8] + refs[-4:]
        if slots:
            w_ref, g_ref, l_ref, m_ref, v_ref, grad_ref, delta_ref, nm_ref, nv_ref = refs
            gv = g_ref[...].astype(F32)
            for s in range(N_DEV - 1):
                gv = gv + l_ref[s].astype(F32)
        else:
            w_ref, g_ref, m_ref, v_ref, grad_ref, delta_ref, nm_ref, nv_ref = refs
            gv = g_ref[...]
        wv = w_ref[...]
        mn = ADAM_B1 * m_ref[...] + (1.0 - ADAM_B1) * gv
        vn = ADAM_B2 * v_ref[...] + (1.0 - ADAM_B2) * (gv * gv)
        m_hat = mn / bc1
        v_hat = vn / bc2
        grad_ref[...] = gv
        delta_ref[...] = -ADAM_LR * (m_hat / (jnp.sqrt(v_hat) + ADAM_EPS) + ADAM_WD * wv)
        nm_ref[...] = mn
        nv_ref[...] = vn

    if depth is None:
        blk = pl.BlockSpec((tr, c), lambda i: (i + boff, 0))
    else:
        blk = pl.BlockSpec((None, tr, c), lambda i: (layer, i + boff, 0))
    g_blk = pl.BlockSpec((tr, c), lambda i: (i, 0))
    l_spec = [pl.BlockSpec((N_DEV - 1, tr, c), lambda i: (0, i, 0))] if slots else []
    args = (w, g, landed, m, v) if slots else (w, g, m, v)
    in_specs = [blk, g_blk] + l_spec + [blk, blk]
    if depth is None:
        shp = jax.ShapeDtypeStruct((r, c), F32)
        out_blk = blk
    else:
        shp = jax.ShapeDtypeStruct((depth, r, c), F32)
        out_blk = pl.BlockSpec((None, tr, c), lambda i: (layer, i + boff, 0))
    aliases = {}
    if into is not None:
        aliases = {len(args) + j: j for j in range(4)}
        in_specs = in_specs + [ANY] * 4
        args = args + tuple(into)
    return pl.pallas_call(
        body, name=name, grid=(rows // tr,),
        in_specs=in_specs, out_specs=[out_blk] * 4,
        out_shape=[shp] * 4, input_output_aliases=aliases,
        compiler_params=_params(("parallel",)),
    )(*args)


def _slot_sum(g, *, name):
    _, r, c = g.shape
    tr = _tile(r, 512, SUBLANES)

    def body(g_ref, o_ref):
        gv = g_ref[0].astype(F32)
        for s in range(1, N_DEV):
            gv = gv + g_ref[s].astype(F32)
        o_ref[...] = gv

    return pl.pallas_call(
        body, name=name, grid=(r // tr,),
        in_specs=[pl.BlockSpec((N_DEV, tr, c), lambda i: (0, i, 0))],
        out_specs=pl.BlockSpec((tr, c), lambda i: (i, 0)),
        out_shape=jax.ShapeDtypeStruct((r, c), F32),
        compiler_params=_params(("parallel",)),
    )(g)


def _mesh_pos():
    x, y, c = lax.axis_index("x"), lax.axis_index("y"), lax.axis_index("c")
    return x, y, c, 4 * x + 2 * y + c


ANY = pl.BlockSpec(memory_space=pl.ANY)


GATHER_COPIES = 9


def _all_gather(srcs, out_shapes, views, *, name):
    n = len(srcs)
    SIB, X_OWN, Y_OWN, X_DIAG, Y_DIAG, SIB_X, SIB_Y, SIB_DIAG_TOP, SIB_DIAG_BOTTOM = range(GATHER_COPIES)

    def body(*refs):
        src = refs[:n]
        dst = refs[n:2 * n]
        send_sems, recv_sems, local_sems = refs[2 * n:]
        x, y, c, me = _mesh_pos()
        sibling, x_nbr, y_nbr = (x, y, 1 - c), (1 - x, y, c), (x, 1 - y, c)

        def block(a, px, py, pc, half=None):
            win = views[a](dst[a], 4 * px + 2 * py + pc)
            if half is None:
                return win
            rows = win.shape[0] // 2
            return win.at[pl.ds(half * rows, rows)]

        def copy(a, k, win, to, from_src=False):
            return pltpu.make_async_remote_copy(
                src_ref=src[a] if from_src else win, dst_ref=win,
                send_sem=send_sems.at[a * GATHER_COPIES + k], recv_sem=recv_sems.at[a * GATHER_COPIES + k],
                device_id=to, device_id_type=MESH)

        mine = [pltpu.make_async_copy(src[a], block(a, x, y, c), local_sems.at[a]) for a in range(n)]
        started = []

        def start(cp):
            cp.start()
            started.append(cp)

        for a in range(n):
            mine[a].start()
            own = block(a, x, y, c)
            start(copy(a, SIB, own, sibling, True))
            start(copy(a, X_OWN, own, x_nbr, True))
            start(copy(a, Y_OWN, own, y_nbr, True))
        for a in range(n):
            from_y = block(a, x, 1 - y, c)
            copy(a, Y_OWN, from_y, y_nbr).wait_recv()
            start(copy(a, X_DIAG, block(a, x, 1 - y, c, 0), x_nbr))
            start(copy(a, SIB_Y, from_y, sibling))
            from_x = block(a, 1 - x, y, c)
            copy(a, X_OWN, from_x, x_nbr).wait_recv()
            start(copy(a, Y_DIAG, block(a, 1 - x, y, c, 1), y_nbr))
            start(copy(a, SIB_X, from_x, sibling))
        for a in range(n):
            top = block(a, 1 - x, 1 - y, c, 0)
            copy(a, X_DIAG, top, x_nbr).wait_recv()
            start(copy(a, SIB_DIAG_TOP, top, sibling))
            bottom = block(a, 1 - x, 1 - y, c, 1)
            copy(a, Y_DIAG, bottom, y_nbr).wait_recv()
            start(copy(a, SIB_DIAG_BOTTOM, bottom, sibling))
        for a in range(n):
            copy(a, SIB, block(a, x, y, 1 - c), sibling).wait_recv()
            copy(a, SIB_X, block(a, 1 - x, y, 1 - c), sibling).wait_recv()
            copy(a, SIB_Y, block(a, x, 1 - y, 1 - c), sibling).wait_recv()
            copy(a, SIB_DIAG_TOP, block(a, 1 - x, 1 - y, 1 - c, 0), sibling).wait_recv()
            copy(a, SIB_DIAG_BOTTOM, block(a, 1 - x, 1 - y, 1 - c, 1), sibling).wait_recv()
        for cp in started:
            cp.wait_send()
        for cp in mine:
            cp.wait()

    return pl.pallas_call(
        body, name=name,
        in_specs=[ANY] * n, out_specs=[ANY] * n,
        out_shape=[jax.ShapeDtypeStruct(s, x.dtype) for s, x in zip(out_shapes, srcs)],
        scratch_shapes=[pltpu.SemaphoreType.DMA((GATHER_COPIES * n,)), pltpu.SemaphoreType.DMA((GATHER_COPIES * n,)),
                        pltpu.SemaphoreType.DMA((n,))],
    )(*srcs)


HBM = pl.BlockSpec(memory_space=pltpu.HBM)
SEM = pl.BlockSpec(memory_space=pltpu.SEMAPHORE)
EFFECT = pltpu.SideEffectType.DATAFLOW_SIDE_EFFECTING


def _peer_of(x, y, c, k):
    return (1 - x if k & 4 else x, 1 - y if k & 2 else y, 1 - c if k & 1 else c)


def _peer_copies(n, wins, src, land, send_sems, recv_sems):
    x, y, c, me = _mesh_pos()
    out = []
    for a in range(n):
        for k in range(1, N_DEV):
            px, py, pc = _peer_of(x, y, c, k)
            s_win, d_win = wins[a](src[a], land[a], me, 4 * px + 2 * py + pc, k)
            out.append(pltpu.make_async_remote_copy(
                src_ref=s_win, dst_ref=d_win,
                send_sem=send_sems.at[a * 7 + k - 1], recv_sem=recv_sems.at[a * 7 + k - 1],
                device_id=(px, py, pc), device_id_type=MESH))
    return out


def _push_start(srcs, lands, wins, *, name):
    n = len(srcs)

    def body(*refs):
        src = refs[:n]
        land = refs[n:2 * n]
        send_sems, recv_sems = refs[2 * n], refs[2 * n + 1]
        token = refs[-1]
        for cp in _peer_copies(n, wins, src, land, send_sems, recv_sems):
            cp.start()
        token[...] = jnp.zeros_like(token)

    bufs = (*srcs, *lands)
    return pl.pallas_call(
        body, name=name,
        out_shape=(pltpu.SemaphoreType.DMA((7 * n,)), pltpu.SemaphoreType.DMA((7 * n,)),
                   *[pltpu.HBM(v.shape, v.dtype) for v in bufs], jax.ShapeDtypeStruct((SUBLANES, LANES), F32)),
        in_specs=[HBM] * (2 * n),
        out_specs=(SEM, SEM, *[HBM] * (2 * n), pl.BlockSpec(memory_space=pltpu.VMEM)),
        input_output_aliases={i: 2 + i for i in range(2 * n)},
        compiler_params=pltpu.CompilerParams(has_side_effects=EFFECT),
    )(*[pltpu.with_memory_space_constraint(v, pltpu.HBM) for v in bufs])


def _push_wait(handle, wins, after, *, name):
    send_sems, recv_sems, *bufs, _ = handle
    n = len(bufs) // 2

    def body(*refs):
        src = refs[:n]
        land = refs[n:2 * n]
        for cp in _peer_copies(n, wins, src, land, refs[2 * n], refs[2 * n + 1]):
            cp.wait_send()
            cp.wait_recv()

    outs = pl.pallas_call(
        body, name=name,
        out_shape=tuple(pltpu.HBM(v.shape, v.dtype) for v in bufs),
        in_specs=[HBM] * (2 * n) + [SEM, SEM, ANY],
        out_specs=tuple([HBM] * (2 * n)),
        input_output_aliases={i: i for i in range(2 * n)},
        compiler_params=pltpu.CompilerParams(has_side_effects=EFFECT),
    )(*bufs, send_sems, recv_sems, after)
    return outs[:n], outs[n:]


def _gather_lead(src, land, me, peer, k):
    return src, land.at[me]


def _gather_cols(width):
    def win(src, land, me, peer, k):
        return src, land.at[:, pl.ds(me * width, width)]
    return win


def _scatter_lead(src, land, me, peer, k):
    return src.at[peer], land.at[k - 1]


def _scatter_cols(width):
    def win(src, land, me, peer, k):
        return src.at[:, pl.ds(peer * width, width)], land.at[k - 1]
    return win


def _place_block(own, *, cols, name):
    rows, width = own.shape
    tr = _tile(rows, 512, 2 * SUBLANES)
    _, _, _, me = _mesh_pos()

    def body(me_ref, x_ref, o_ref):
        o_ref[...] = x_ref[...]

    if cols:
        out_spec = pl.BlockSpec((tr, width), lambda i, me_ref: (i, me_ref[0]))
        shape = (rows, N_DEV * width)
    else:
        out_spec = pl.BlockSpec((None, tr, width), lambda i, me_ref: (me_ref[0], i, 0))
        shape = (N_DEV, rows, width)
    return pl.pallas_call(
        body, name=name,
        grid_spec=pltpu.PrefetchScalarGridSpec(
            num_scalar_prefetch=1, grid=(rows // tr,),
            in_specs=[pl.BlockSpec((tr, width), lambda i, me_ref: (i, 0))], out_specs=out_spec),
        out_shape=jax.ShapeDtypeStruct(shape, own.dtype),
        compiler_params=_params(("arbitrary",)),
    )(me.astype(jnp.int32).reshape(1), own)


def _dep(x, token):
    return x + token[0, 0].astype(x.dtype)


def _lead(ref, d):
    return ref.at[d]


def _col_window(width):
    def view(ref, d):
        return ref.at[:, pl.ds(d * width, width)]
    return view


def _pack(arrs):
    flat = jnp.concatenate([a.reshape(-1).astype(F32) for a in arrs])
    n = flat.shape[0]
    rows = -(-n // (2 * SUBLANES * LANES)) * 2 * SUBLANES
    return jnp.pad(flat, (0, rows * LANES - n)).reshape(rows, LANES)


def _unpack(buf, shapes):
    flat = buf.reshape(-1)
    out, off = [], 0
    for s in shapes:
        n = 1
        for q in s:
            n *= q
        out.append(flat[off:off + n].reshape(s))
        off += n
    return out


def _blockdiag(w, cw):
    h, hd, _ = w.shape
    per = cw // hd
    wg = w.reshape(h // per, per, hd, hd)
    eye = jnp.eye(per, dtype=w.dtype)
    blk = jnp.einsum("gpij,pq->gpiqj", wg, eye)
    return blk.reshape(h // per, cw, cw).astype(BF16)


def _blockdiag_extract(g, hd):
    n, cw, _ = g.shape
    per = cw // hd
    g5 = g.reshape(n, per, hd, per, hd)
    idx = jnp.arange(per)
    return g5[:, idx, :, idx, :].transpose(1, 0, 2, 3).reshape(n * per, hd, hd)


def kernel(x, meta, norm_g, w_in, conv_a_w, conv_a_b, lru_wr, lru_br, lru_wi, lru_bi, lru_lambda, conv_b_w, w_out, final_g, loss_target, m_meta, m_norm_g, m_w_in, m_conv_a_w, m_conv_a_b, m_lru_wr, m_lru_br, m_lru_wi, m_lru_bi, m_lru_lambda, m_conv_b_w, m_w_out, m_final_g, v_meta, v_norm_g, v_w_in, v_conv_a_w, v_conv_a_b, v_lru_wr, v_lru_br, v_lru_wi, v_lru_bi, v_lru_lambda, v_conv_b_w, v_w_out, v_final_g):
    _, seq, d = x.shape
    n_meta = meta.shape[0]
    depth = w_in.shape[0]
    din = w_in.shape[2] * N_DEV
    dl = din // 6
    dmix = 2 * dl
    wcol = w_in.shape[2]
    wrow = w_out.shape[1]
    mcol = meta.shape[1]
    ccol = conv_a_w.shape[2]
    hd = lru_wr.shape[2]
    n_tok = n_meta + seq
    tp = -(-n_tok // TOKEN_TILE) * TOKEN_TILE
    me = 4 * lax.axis_index("x") + 2 * lax.axis_index("y") + lax.axis_index("c")

    bf = lambda a: a.astype(BF16)
    small_mine = _pack([meta, conv_a_w, conv_b_w])
    first = _all_gather([bf(w_in[0]), small_mine], [(d, din), (N_DEV,) + small_mine.shape],
                        [_col_window(wcol), _lead], name="gather_first")
    flat = first[1].reshape(N_DEV, -1)
    sizes = [meta.size, conv_a_w.size, conv_b_w.size]
    meta_full = jnp.moveaxis(flat[:, :sizes[0]].reshape(N_DEV, n_meta, mcol), 0, 1).reshape(n_meta, d)
    wa_full = jnp.moveaxis(flat[:, sizes[0]:sizes[0] + sizes[1]].reshape(N_DEV, depth, 4, ccol), 0, 2) \
        .reshape(depth, 4, dl)
    wb_full = jnp.moveaxis(flat[:, sizes[0] + sizes[1]:sum(sizes)].reshape(N_DEV, depth, 3, ccol), 0, 2) \
        .reshape(depth, 3, dl)
    w_in_full = [None] * depth
    w_out_full = [None] * depth

    push_out = [None] * depth
    push_in = [None] * depth
    w_in_full[0], src = lax.optimization_barrier((first[0], bf(w_out[0])))
    push_out[0] = _push_start([src], [_place_block(src, cols=False, name="place_wout_0")], [_gather_lead],
                              name="gather_wout_0_start")
    token = push_out[0][-1]
    for l in range(1, depth):
        src = bf(_dep(w_in[l], token))
        push_in[l] = _push_start([src], [_place_block(src, cols=True, name=f"place_win_{l}")], [_gather_cols(wcol)],
                                 name=f"gather_win_{l}_start")
        src = bf(_dep(w_out[l], push_in[l][-1]))
        push_out[l] = _push_start([src], [_place_block(src, cols=False, name=f"place_wout_{l}")], [_gather_lead],
                                  name=f"gather_wout_{l}_start")
        token = push_out[l][-1]

    wr_blk = [_blockdiag(lru_wr[l], GATE_BLOCK) for l in range(depth)]
    wi_blk = [_blockdiag(lru_wi[l], GATE_BLOCK) for l in range(depth)]
    vec = lambda a: a.reshape(1, dl)

    tm = _tile(tp, 1408)
    saved = []
    for l in range(depth):
        if l == 0:
            h, hn = _rms_fwd_first(x[0], meta_full, _dep(norm_g[l], token), tp=tp, name=f"rms_fwd_{l}")
        else:
            hn = _rms_fwd(h, norm_g[l], name=f"rms_fwd_{l}")
        if l > 0:
            _, landed = _push_wait(push_in[l], [_gather_cols(wcol)], hn, name=f"gather_win_{l}_wait")
            w_in_full[l] = landed[0]
        u = _matmul(hn, w_in_full[l], tm=tm, tn=_tile(din, 1536), tk=d, name=f"mm_u_{l}")
        mixed, y = _mixer_fwd(u, wa_full[l], vec(conv_a_b[l]), wr_blk[l], vec(lru_br[l]), wi_blk[l], vec(lru_bi[l]),
                              vec(lru_lambda[l]), wb_full[l], name=f"mixer_fwd_{l}")
        _, landed = _push_wait(push_out[l], [_gather_lead], y, name=f"gather_wout_{l}_wait")
        w_out_full[l] = landed[0].reshape(dmix, d)
        h_next = _matmul(y, w_out_full[l], tm=tm, tn=_tile(d, 512), tk=dmix, add=h, name=f"mm_out_{l}")
        saved.append((h, hn, u, mixed, y))
        h = h_next

    dh, dhb, dg_final, loss_part = _loss_head(h, loss_target[0], final_g, n_meta=n_meta, n_tok=n_tok,
                                              name="loss_head")

    small_grads = [None] * depth
    sent_out = [None] * depth
    sent_in = [None] * depth
    scatter_in = [_scatter_cols(wcol)]
    token = None
    dg_norms = []
    for l in reversed(range(depth)):
        h_in, hn, u, mixed, y = saved[l]
        dy = _matmul(dhb, w_out_full[l], tb=True, tm=tm, tn=_tile(dmix, 1024), tk=d, dep=token, name=f"mm_dy_{l}")
        dw_out = _matmul(y, dhb, ta=True, tm=_tile(dmix, 1024), tn=_tile(d, 1024), tk=tp, out_dtype=BF16,
                         name=f"mm_dwout_{l}")
        sent_out[l] = _push_start([dw_out.reshape(N_DEV, wrow, d)], [lax.empty((N_DEV - 1, wrow, d), BF16)],
                                  [_scatter_lead], name=f"scatter_wout_{l}_start")
        du, sg, dwr, dwi = _mixer_bwd(u, mixed, dy, wa_full[l], wr_blk[l], vec(lru_br[l]), wi_blk[l], vec(lru_bi[l]),
                                      vec(lru_lambda[l]), _dep(wb_full[l], sent_out[l][-1]), name=f"mixer_bwd_{l}")
        small_grads[l] = (sg, dwr, dwi)
        if l == 0:
            rows = jnp.stack([small_grads[j][0] for j in range(depth)])
            early = [_pack([
                rows[:, SG_BA], rows[:, SG_BR], rows[:, SG_BI], rows[:, SG_LAM], rows[:, SG_WA:SG_WA + 4],
                rows[:, SG_WB:SG_WB + 3], dg_final[0], *dg_norms]),
                _pack([jnp.stack([_blockdiag_extract(small_grads[j][1], hd) for j in range(depth)]),
                       jnp.stack([_blockdiag_extract(small_grads[j][2], hd) for j in range(depth)])]).astype(BF16)]
            early_land = [lax.dynamic_update_slice(lax.empty((N_DEV,) + a.shape, a.dtype), a[None], (me, 0, 0))
                          for a in early]
            sent_early = _push_start(early, early_land, [_gather_lead] * 2, name="gather_early_grads_start")
        parts = 2 if l == 0 else 1
        token = sent_early[-1] if l == 0 else None
        sent_in[l] = []
        for p in range(parts):
            dw_in = _matmul(hn, du, ta=True, tm=_tile(d // parts, 512), tn=_tile(din, 1536), tk=tp, out_dtype=BF16,
                            dep=token, m_part=(p, parts), name=f"mm_dwin_{l}_{p}")
            sent_in[l].append(_push_start([dw_in], [lax.empty((N_DEV - 1, d // parts, wcol), BF16)], scatter_in,
                                          name=f"scatter_win_{l}_{p}_start"))
            token = sent_in[l][-1][-1]
        dhn = _matmul(du, w_in_full[l], tb=True, tm=_tile(tp, 528, 2 * SUBLANES), tn=_tile(d, 1024), tk=din, dep=token,
                      name=f"mm_dhn_{l}")
        if l > 0:
            dh, dhb, dg_norm = _rms_bwd(h_in, dhn, dh, norm_g[l], name=f"rms_bwd_{l}")
            dg_norms.append(dg_norm[0])
        else:
            grad_x, d_meta, dg_norm = _rms_bwd_first(h_in, dhn, dh, norm_g[l], n_meta=n_meta, seq=seq,
                                                     name=f"rms_bwd_{l}")

    big = {"win": None, "wout": None}

    def big_adamw(l, after):
        src, landed = _push_wait(sent_out[l], [_scatter_lead], after, name=f"scatter_wout_{l}_wait")
        own = lax.dynamic_index_in_dim(src[0], me, 0, keepdims=False)
        big["wout"] = _adamw(w_out, own, m_w_out, v_w_out, landed=landed[0], layer=l, depth=depth,
                             into=big["wout"], name=f"adamw_w_out_{l}")
        after = big["wout"][0]
        for p, sent in enumerate(sent_in[l]):
            src, landed = _push_wait(sent, scatter_in, after, name=f"scatter_win_{l}_{p}_wait")
            own = lax.dynamic_slice_in_dim(src[0], me * wcol, wcol, axis=1)
            big["win"] = _adamw(w_in, own, m_w_in, v_w_in, landed=landed[0], layer=l, depth=depth,
                                into=big["win"], row_off=p * own.shape[0], name=f"adamw_w_in_{l}_{p}")
            after = big["win"][0]
        return after

    after = dg_norm
    for l in reversed(range(1, depth)):
        after = big_adamw(l, after)

    late = _pack([dg_norm[0], d_meta, loss_part[0:1, 0:1]])
    if depth > 1:
        late, after = lax.optimization_barrier((late, after))
    late_all = _all_gather([late], [(N_DEV,) + late.shape], [_lead], name="gather_late_grads")[0]
    late_sum = _unpack(_slot_sum(late_all, name="sum_late_grads"), [(d,), (n_meta, d), ()])
    loss = late_sum[2]
    _, early_all = _push_wait(sent_early, [_gather_lead] * 2, late_sum[0], name="gather_early_grads_wait")
    vec_shapes = [conv_a_b.shape, lru_br.shape, lru_bi.shape, lru_lambda.shape, (depth, 4, dl), (depth, 3, dl),
                  final_g.shape] + [(d,)] * (depth - 1)
    e = _unpack(_slot_sum(early_all[0], name="sum_early_vectors"), vec_shapes)
    g_wr, g_wi = _unpack(_slot_sum(early_all[1], name="sum_early_maps"), [lru_wr.shape, lru_wi.shape])
    g_norm = jnp.stack([late_sum[0]] + e[7:][::-1])
    g_meta = lax.dynamic_slice_in_dim(late_sum[1], me * mcol, mcol, axis=1)
    g_wa = lax.dynamic_slice_in_dim(e[4], me * ccol, ccol, axis=2)
    g_wb = lax.dynamic_slice_in_dim(e[5], me * ccol, ccol, axis=2)

    small_w = [norm_g, conv_a_b, lru_wr, lru_br, lru_wi, lru_bi, lru_lambda, final_g, meta, conv_a_w, conv_b_w]
    small_m = [m_norm_g, m_conv_a_b, m_lru_wr, m_lru_br, m_lru_wi, m_lru_bi, m_lru_lambda, m_final_g, m_meta,
               m_conv_a_w, m_conv_b_w]
    small_v = [v_norm_g, v_conv_a_b, v_lru_wr, v_lru_br, v_lru_wi, v_lru_bi, v_lru_lambda, v_final_g, v_meta,
               v_conv_a_w, v_conv_b_w]
    small_g = [g_norm, e[0], g_wr, e[1], g_wi, e[2], e[3], e[6], g_meta, g_wa, g_wb]
    small_out = _adamw(_pack(small_w), _pack(small_g), _pack(small_m), _pack(small_v), name="adamw_small")
    small_shapes = [a.shape for a in small_w]
    s_grad, s_delta, s_m, s_v = [_unpack(o, small_shapes) for o in small_out]

    big_adamw(0, small_out[0])
    win_out, wout_out = big["win"], big["wout"]

    names = ["norm_g", "conv_a_b", "lru_wr", "lru_br", "lru_wi", "lru_bi", "lru_lambda", "final_g", "meta",
             "conv_a_w", "conv_b_w"]
    order = ["meta", "norm_g", "w_in", "conv_a_w", "conv_a_b", "lru_wr", "lru_br", "lru_wi", "lru_bi", "lru_lambda",
             "conv_b_w", "w_out", "final_g"]

    def family(idx, small):
        table = {nm: small[i] for i, nm in enumerate(names)}
        table["w_in"] = win_out[idx]
        table["w_out"] = wout_out[idx]
        return [table[nm] for nm in order]

    return (loss, grad_x, *family(0, s_grad), *family(1, s_delta), *family(2, s_m), *family(3, s_v))
```

```python
import jax
import jax.numpy as jnp
from jax import lax
from jax.experimental import pallas as pl
from jax.experimental.pallas import tpu as pltpu

F32 = jnp.float32
BF16 = jnp.bfloat16
MESH = pl.DeviceIdType.MESH

N_DEV = 8
RMS_EPS = 1e-6
LRU_C = 8.0
ADAM_LR = 0.001
ADAM_B1 = 0.9
ADAM_B2 = 0.999
ADAM_EPS = 1e-08
ADAM_WD = 0.01
ADAM_STEP = 10

V7X_VMEM_LIMIT = 52 * 1024 * 1024
LANES = 128
SUBLANES = 8
TOKEN_TILE = 384
MIX_ROWS = 128
MIX_SUBTILES = 3
GATE_BLOCK = 128


def _params(sem):
    return pltpu.CompilerParams(dimension_semantics=sem, vmem_limit_bytes=V7X_VMEM_LIMIT)


def _tile(n, target, align=LANES):
    best = None
    for t in range(align, min(n, target) + 1, align):
        if n % t == 0:
            best = t
    return n if best is None else best


def _sigmoid(z):
    return 0.5 * jnp.tanh(0.5 * z) + 0.5


def _softplus(z):
    e = jnp.exp(-jnp.abs(z))
    u = 1.0 + e
    l1p = jnp.where(u == 1.0, e, jnp.log(u) * e / jnp.where(u == 1.0, 1.0, u - 1.0))
    return jnp.maximum(z, 0.0) + l1p


def _matmul(a, b, *, ta=False, tb=False, tm, tn, tk, out_dtype=F32, add=None, dep=None, m_part=None, name):
    m, k = (a.shape[1], a.shape[0]) if ta else a.shape
    m_off = 0
    if m_part is not None:
        assert add is None and m % (m_part[1] * tm) == 0
        m //= m_part[1]
        m_off = m_part[0] * (m // tm)
    n, kb = b.shape if tb else b.shape[::-1]
    assert kb == k
    assert m % tm == 0 and n % tn == 0 and k % tk == 0, (m, n, k, tm, tn, tk)
    nk = k // tk
    a_spec = pl.BlockSpec((tk, tm), lambda i, j, q: (q, i + m_off)) if ta \
        else pl.BlockSpec((tm, tk), lambda i, j, q: (i + m_off, q))
    b_spec = pl.BlockSpec((tn, tk), lambda i, j, q: (j, q)) if tb else pl.BlockSpec((tk, tn), lambda i, j, q: (q, j))
    o_spec = pl.BlockSpec((tm, tn), lambda i, j, q: (i, j))
    o_shape = (m, n)
    dims = (((0 if ta else 1,), (1 if tb else 0,)), ((), ()))
    has_add = add is not None
    has_dep = dep is not None

    def body(*refs):
        if has_dep:
            refs = refs[:-3] + refs[-2:]
        if has_add:
            a_ref, b_ref, add_ref, o_ref, acc_ref = refs
        else:
            a_ref, b_ref, o_ref, acc_ref = refs
        q = pl.program_id(2)
        part = lax.dot_general(a_ref[...], b_ref[...], dims, preferred_element_type=F32)

        def finish(acc):
            if has_add:
                acc = acc + add_ref[...]
            o_ref[...] = acc.astype(out_dtype)

        if nk == 1:
            finish(part)
        else:
            @pl.when(q == 0)
            def _():
                acc_ref[...] = part

            @pl.when(jnp.logical_and(q > 0, q < nk - 1))
            def _():
                acc_ref[...] += part

            @pl.when(q == nk - 1)
            def _():
                finish(acc_ref[...] + part)

    in_specs = [a_spec, b_spec] + ([o_spec] if has_add else [])
    args = (a, b) + ((add,) if has_add else ())
    if has_dep:
        in_specs.append(pl.BlockSpec((SUBLANES, LANES), lambda i, j, q: (0, 0)))
        args += (dep,)
    acc_shape = (tm, tn) if nk > 1 else (SUBLANES, LANES)
    return pl.pallas_call(
        body, name=name,
        grid=(m // tm, n // tn, nk),
        in_specs=in_specs, out_specs=o_spec,
        out_shape=jax.ShapeDtypeStruct(o_shape, out_dtype),
        scratch_shapes=[pltpu.VMEM(acc_shape, F32)],
        compiler_params=_params(("parallel", "parallel", "arbitrary")),
    )(*args)


def _rms_fwd(h, g, *, name):
    tp, d = h.shape
    tr = _tile(tp, 512, SUBLANES)

    def body(h_ref, g_ref, o_ref):
        hv = h_ref[...]
        rstd = lax.rsqrt(jnp.mean(hv * hv, axis=-1, keepdims=True) + RMS_EPS)
        o_ref[...] = (hv * rstd * g_ref[...]).astype(BF16)

    return pl.pallas_call(
        body, name=name, grid=(tp // tr,),
        in_specs=[pl.BlockSpec((tr, d), lambda i: (i, 0)), pl.BlockSpec((1, d), lambda i: (0, 0))],
        out_specs=pl.BlockSpec((tr, d), lambda i: (i, 0)),
        out_shape=jax.ShapeDtypeStruct((tp, d), BF16),
        compiler_params=_params(("parallel",)),
    )(h, g.reshape(1, d))


def _rms_fwd_first(x, meta, g, *, tp, name):
    seq, d = x.shape
    n_meta = meta.shape[0]
    n_tok = n_meta + seq
    tr = TOKEN_TILE
    assert tp % tr == 0 and tr % n_meta == 0
    per = tr // n_meta

    def body(x_ref, xp_ref, m_ref, g_ref, h_ref, o_ref):
        i = pl.program_id(0)
        head = jnp.where(i == 0, m_ref[...], xp_ref[...])
        rows = i * tr + lax.broadcasted_iota(jnp.int32, (tr, 1), 0)
        hv = jnp.where(rows < n_tok, jnp.concatenate([head, x_ref[:tr - n_meta, :]], axis=0), 0.0)
        h_ref[...] = hv
        rstd = lax.rsqrt(jnp.mean(hv * hv, axis=-1, keepdims=True) + RMS_EPS)
        o_ref[...] = (hv * rstd * g_ref[...]).astype(BF16)

    row = pl.BlockSpec((tr, d), lambda i: (i, 0))
    own = pl.BlockSpec((tr, d), lambda i: (jnp.minimum(i, -(-seq // tr) - 1), 0))
    before = pl.BlockSpec((n_meta, d), lambda i: (jnp.maximum(i * per - 1, 0), 0))
    return pl.pallas_call(
        body, name=name, grid=(tp // tr,),
        in_specs=[own, before, pl.BlockSpec((n_meta, d), lambda i: (0, 0)), pl.BlockSpec((1, d), lambda i: (0, 0))],
        out_specs=[row, row],
        out_shape=[jax.ShapeDtypeStruct((tp, d), F32), jax.ShapeDtypeStruct((tp, d), BF16)],
        compiler_params=_params(("parallel",)),
    )(x, x, meta, g.reshape(1, d))


def _rms_bwd(h, dhn, dout, g, *, name):
    tp, d = h.shape
    tr = _tile(tp, 528, 2 * SUBLANES)

    def body(h_ref, dhn_ref, dout_ref, g_ref, dh_ref, dhb_ref, dg_ref):
        hv = h_ref[...]
        rstd = lax.rsqrt(jnp.mean(hv * hv, axis=-1, keepdims=True) + RMS_EPS)
        xhat = hv * rstd
        dn = dhn_ref[...]
        dxhat = dn * g_ref[...]
        dh = dout_ref[...] + rstd * (dxhat - xhat * jnp.mean(dxhat * xhat, axis=-1, keepdims=True))
        dh_ref[...] = dh
        dhb_ref[...] = dh.astype(BF16)
        part = jnp.sum(dn * xhat, axis=0, keepdims=True)

        @pl.when(pl.program_id(0) == 0)
        def _():
            dg_ref[...] = part

        @pl.when(pl.program_id(0) > 0)
        def _():
            dg_ref[...] += part

    row = pl.BlockSpec((tr, d), lambda i: (i, 0))
    vec = pl.BlockSpec((1, d), lambda i: (0, 0))
    return pl.pallas_call(
        body, name=name, grid=(tp // tr,),
        in_specs=[row, row, row, vec],
        out_specs=[row, row, vec],
        out_shape=[jax.ShapeDtypeStruct((tp, d), F32), jax.ShapeDtypeStruct((tp, d), BF16),
                   jax.ShapeDtypeStruct((1, d), F32)],
        compiler_params=_params(("arbitrary",)),
    )(h, dhn, dout, g.reshape(1, d))


def _rms_bwd_first(h, dhn, dout, g, *, n_meta, seq, name):
    tp, d = h.shape
    tr = _tile(seq, 512)
    assert seq % tr == 0 and tr % n_meta == 0 and tp >= seq + n_meta
    nt = seq // tr
    per = tr // n_meta

    def grads(hv, dn, do, gv):
        rstd = lax.rsqrt(jnp.mean(hv * hv, axis=-1, keepdims=True) + RMS_EPS)
        xhat = hv * rstd
        dxhat = dn * gv
        dh = do + rstd * (dxhat - xhat * jnp.mean(dxhat * xhat, axis=-1, keepdims=True))
        return dh, jnp.sum(dn * xhat, axis=0, keepdims=True)

    def body(h_ref, dhn_ref, dout_ref, hn_ref, dhnn_ref, doutn_ref, g_ref, gx_ref, dmeta_ref, dg_ref):
        i = pl.program_id(0)
        gv = g_ref[...]
        dh, part = grads(h_ref[...], dhn_ref[...], dout_ref[...], gv)
        dh_next, part_next = grads(hn_ref[...], dhnn_ref[...], doutn_ref[...], gv)
        gx_ref[...] = jnp.concatenate([dh[n_meta:], dh_next], axis=0)

        @pl.when(i == 0)
        def _():
            dmeta_ref[...] = dh[:n_meta]
            dg_ref[...] = part

        @pl.when(i > 0)
        def _():
            dg_ref[...] += part

        @pl.when(i == nt - 1)
        def _():
            dg_ref[...] += part_next

    row = pl.BlockSpec((tr, d), lambda i: (i, 0))
    nxt = pl.BlockSpec((n_meta, d), lambda i: ((i + 1) * per, 0))
    vec = pl.BlockSpec((1, d), lambda i: (0, 0))
    return pl.pallas_call(
        body, name=name, grid=(nt,),
        in_specs=[row, row, row, nxt, nxt, nxt, vec],
        out_specs=[pl.BlockSpec((None, tr, d), lambda i: (0, i, 0)), pl.BlockSpec((n_meta, d), lambda i: (0, 0)), vec],
        out_shape=[jax.ShapeDtypeStruct((1, seq, d), F32), jax.ShapeDtypeStruct((n_meta, d), F32),
                   jax.ShapeDtypeStruct((1, d), F32)],
        compiler_params=_params(("arbitrary",)),
    )(h, dhn, dout, h, dhn, dout, g.reshape(1, d))


def _loss_head(h, tgt, g, *, n_meta, n_tok, name):
    tp, d = h.shape
    seq = tgt.shape[0]
    tr = TOKEN_TILE
    assert tp % tr == 0 and tr % n_meta == 0
    per = tr // n_meta

    def body(h_ref, t_ref, tp_ref, g_ref, dh_ref, dhb_ref, dg_ref, loss_ref):
        i = pl.program_id(0)
        hv = h_ref[...]
        rstd = lax.rsqrt(jnp.mean(hv * hv, axis=-1, keepdims=True) + RMS_EPS)
        xhat = hv * rstd
        gv = g_ref[...]
        rows = i * tr + lax.broadcasted_iota(jnp.int32, (tr, 1), 0)
        valid = jnp.logical_and(rows >= n_meta, rows < n_tok)
        target = jnp.concatenate([tp_ref[...], t_ref[:tr - n_meta, :]], axis=0)
        err = jnp.where(valid, xhat * gv - target, 0.0)
        dy = err * (1.0 / d)
        dxhat = dy * gv
        dh = rstd * (dxhat - xhat * jnp.mean(dxhat * xhat, axis=-1, keepdims=True))
        dh_ref[...] = dh
        dhb_ref[...] = dh.astype(BF16)
        dg_part = jnp.sum(dy * xhat, axis=0, keepdims=True)
        per_row = jnp.sum(err * err, axis=-1, keepdims=True) * (1.0 / d)
        loss_part = jnp.broadcast_to(0.5 * jnp.sum(per_row, axis=0, keepdims=True), (SUBLANES, LANES))

        @pl.when(i == 0)
        def _():
            dg_ref[...] = dg_part
            loss_ref[...] = loss_part

        @pl.when(i > 0)
        def _():
            dg_ref[...] += dg_part
            loss_ref[...] += loss_part

    row = pl.BlockSpec((tr, d), lambda i: (i, 0))
    vec = pl.BlockSpec((1, d), lambda i: (0, 0))
    own = pl.BlockSpec((tr, d), lambda i: (jnp.minimum(i, -(-seq // tr) - 1), 0))
    before = pl.BlockSpec((n_meta, d), lambda i: (jnp.maximum(i * per - 1, 0), 0))
    return pl.pallas_call(
        body, name=name, grid=(tp // tr,),
        in_specs=[row, own, before, vec],
        out_specs=[row, row, vec, pl.BlockSpec((SUBLANES, LANES), lambda i: (0, 0))],
        out_shape=[jax.ShapeDtypeStruct((tp, d), F32), jax.ShapeDtypeStruct((tp, d), BF16),
                   jax.ShapeDtypeStruct((1, d), F32), jax.ShapeDtypeStruct((SUBLANES, LANES), F32)],
        compiler_params=_params(("arbitrary",)),
    )(h, tgt, tgt, g.reshape(1, d))


def _shift_down(halo, tile, s):
    if s == 0:
        return tile
    ext = jnp.concatenate([halo, tile], axis=0)
    return pltpu.roll(ext, s, 0)[SUBLANES:]


def _shift_up(tile, head, s):
    if s == 0:
        return tile
    ext = jnp.concatenate([tile, head], axis=0)
    n = ext.shape[0]
    return pltpu.roll(ext, n - s, 0)[: tile.shape[0]]


def _to_lane_blocks(ref, cols, val):
    for j in range(cols.start // LANES, cols.stop // LANES):
        ref[j] = val[:, j * LANES - cols.start:(j + 1) * LANES - cols.start]


def _from_lane_blocks(ref, cols):
    return jnp.concatenate([ref[j] for j in range(cols.start // LANES, cols.stop // LANES)], axis=1)


def _scan_tile(a_ref, b_ref, out_ref, carry, j, *, reverse):
    ng = a_ref.shape[1] // SUBLANES
    order = list(range(SUBLANES))[::-1] if reverse else list(range(SUBLANES))

    def rows(r):
        return pl.ds(r, ng, stride=SUBLANES)

    prod, loc = {}, {}
    prev = None
    for r in order:
        ar = a_ref[j, rows(r), :]
        br = b_ref[j, rows(r), :]
        prod[r] = ar if prev is None else ar * prod[prev]
        loc[r] = br if prev is None else ar * loc[prev] + br
        prev = r
    pg, lg = prod[prev], loc[prev]
    ones = jnp.ones((SUBLANES,) + pg.shape[1:], F32)
    zeros = jnp.zeros_like(ones)
    s = 1
    while s < ng:
        p_sh = _shift_up(pg, ones, s) if reverse else _shift_down(ones, pg, s)
        l_sh = _shift_up(lg, zeros, s) if reverse else _shift_down(zeros, lg, s)
        lg = pg * l_sh + lg
        pg = pg * p_sh
        s *= 2
    leaving = pg * carry[0:1, :] + lg
    entering = _shift_up(leaving, carry, 1) if reverse else _shift_down(carry, leaving, 1)
    for r in order:
        out_ref[j, rows(r), :] = loc[r] + prod[r] * entering
    last = leaving[0:1, :] if reverse else leaving[ng - 1:ng, :]
    return jnp.broadcast_to(last, carry.shape)


def _gates(ca, wr, wi, br, bi, sp):
    cab = ca.astype(BF16)
    r = _sigmoid(jnp.dot(cab, wr, preferred_element_type=F32) + br)
    ig = _sigmoid(jnp.dot(cab, wi, preferred_element_type=F32) + bi)
    la = -LRU_C * r * sp
    a = jnp.exp(la)
    mult = jnp.sqrt(-jnp.tanh(la) * (a * a + 1.0))
    return r, ig, a, mult


def _mixer_fwd(u, wa, ba, wr_blk, br, wi_blk, bi, lam, wb, *, name):
    tp, din = u.shape
    dl = din // 6
    tt = MIX_ROWS
    cw = GATE_BLOCK
    nch = dl // cw
    assert tp % tt == 0 and dl % cw == 0

    def body(u_ref, wa_ref, ba_ref, wr_ref, br_ref, wi_ref, bi_ref, lam_ref, wb_ref,
             s_ref, y_ref, xa_tail, v_tail, h_carry, a_s, b_s, h_s):
        @pl.when(pl.program_id(0) == 0)
        def _():
            xa_tail[...] = jnp.zeros_like(xa_tail)
            v_tail[...] = jnp.zeros_like(v_tail)
            h_carry[...] = jnp.zeros_like(h_carry)

        for sub in range(MIX_SUBTILES):
            rows = slice(sub * tt, (sub + 1) * tt)
            for ch in range(nch):
                cs = slice(ch * cw, (ch + 1) * cw)

                def seg(s):
                    return slice(s * dl + ch * cw, s * dl + (ch + 1) * cw)

                xa = u_ref[rows, seg(0)]
                halo = xa_tail[:, cs]
                ca = ba_ref[:, cs] + wa_ref[3:4, cs] * xa
                for kk in range(3):
                    ca = ca + wa_ref[kk:kk + 1, cs] * _shift_down(halo, xa, 3 - kk)
                xa_tail[:, cs] = xa[tt - SUBLANES:]
                s_ref[rows, cs] = ca
                sp = _softplus(-lam_ref[:, cs])
                _, ig, a, mult = _gates(ca, wr_ref[ch], wi_ref[ch], br_ref[:, cs], bi_ref[:, cs], sp)
                _to_lane_blocks(a_s, cs, a)
                _to_lane_blocks(b_s, cs, mult * (ig * ca))

                bv = u_ref[rows, seg(2)]
                v = u_ref[rows, seg(3)] * u_ref[rows, seg(4)]
                gb = u_ref[rows, seg(5)]
                vh = v_tail[:, cs]
                cb = wb_ref[2:3, cs] * v
                for kk in range(2):
                    cb = cb + wb_ref[kk:kk + 1, cs] * _shift_down(vh, v, 2 - kk)
                v_tail[:, cs] = v[tt - SUBLANES:]
                y_ref[rows, dl + ch * cw: dl + (ch + 1) * cw] = (bv * cb * (gb * _sigmoid(gb))).astype(BF16)

            for ch in range(nch):
                cs = slice(ch * cw, (ch + 1) * cw)
                for j in range(cs.start // LANES, cs.stop // LANES):
                    lanes = slice(j * LANES, (j + 1) * LANES)
                    h_carry[:, lanes] = _scan_tile(a_s, b_s, h_s, h_carry[:, lanes], j, reverse=False)
                hsv = _from_lane_blocks(h_s, cs)
                s_ref[rows, dl + ch * cw: dl + (ch + 1) * cw] = hsv
                ga = u_ref[rows, dl + ch * cw: dl + (ch + 1) * cw]
                y_ref[rows, cs] = (hsv * (ga * _sigmoid(ga))).astype(BF16)

    tb = tt * MIX_SUBTILES
    assert tp % tb == 0
    row = lambda w: pl.BlockSpec((tb, w), lambda i: (i, 0))
    full = lambda shp: pl.BlockSpec(shp, lambda i: tuple(0 for _ in shp))
    return pl.pallas_call(
        body, name=name, grid=(tp // tb,),
        in_specs=[row(din), full((4, dl)), full((1, dl)), full((nch, cw, cw)), full((1, dl)),
                  full((nch, cw, cw)), full((1, dl)), full((1, dl)), full((3, dl))],
        out_specs=[row(2 * dl), row(2 * dl)],
        out_shape=[jax.ShapeDtypeStruct((tp, 2 * dl), F32), jax.ShapeDtypeStruct((tp, 2 * dl), BF16)],
        scratch_shapes=[pltpu.VMEM((SUBLANES, dl), F32), pltpu.VMEM((SUBLANES, dl), F32),
                        pltpu.VMEM((SUBLANES, dl), F32)] + [pltpu.VMEM((dl // LANES, tt, LANES), F32)] * 3,
        compiler_params=_params(("arbitrary",)),
    )(u, wa, ba, wr_blk, br, wi_blk, bi, lam, wb)


SG_WA, SG_BA, SG_BR, SG_BI, SG_LAM, SG_WB, SG_ROWS = 0, 4, 5, 6, 7, 8, 16


def _mixer_bwd(u, saved, dy, wa, wr_blk, br, wi_blk, bi, lam, wb, *, name):
    tp, din = u.shape
    dl = din // 6
    tt = MIX_ROWS
    cw = GATE_BLOCK
    nch = dl // cw
    tb = tt * MIX_SUBTILES
    assert tp % tb == 0
    nt = tp // tb
    hb = tb // SUBLANES
    tn_dims = (((0,), (0,)), ((), ()))
    nt_dims = (((1,), (1,)), ((), ()))

    def body(u_ref, uh_ref, s_ref, sh_ref, dy_ref, wa_ref, wr_ref, br_ref, wi_ref, bi_ref, lam_ref, wb_ref,
             du_ref, sg_ref, dwr_ref, dwi_ref,
             g_carry, a_head, dca_head, dcb_head, r_s, i_s, a_s, an_s, d_s, g_s):
        i = pl.program_id(0)
        first_tile = i == nt - 1

        @pl.when(i == 0)
        def _():
            for ref in (g_carry, a_head, dca_head, dcb_head, sg_ref, dwr_ref, dwi_ref):
                ref[...] = jnp.zeros_like(ref)

        def halo_of(x):
            return jnp.where(first_tile, 0.0, x)

        for sub in reversed(range(MIX_SUBTILES)):
            rows = slice(sub * tt, (sub + 1) * tt)

            def before(ref, halo_ref, cols):
                if sub == 0:
                    return halo_of(halo_ref[:, cols])
                return ref[sub * tt - SUBLANES:sub * tt, cols]

            for ch in range(nch):
                cs = slice(ch * cw, (ch + 1) * cw)
                cav = s_ref[rows, cs]
                sp = _softplus(-lam_ref[:, cs])
                r, ig, a, _ = _gates(cav, wr_ref[ch], wi_ref[ch], br_ref[:, cs], bi_ref[:, cs], sp)
                r_s[:, cs] = r
                i_s[:, cs] = ig
                a_s[:, cs] = a
                _to_lane_blocks(an_s, cs, _shift_up(a, a_head[:, cs], 1))
                a_head[:, cs] = a[:SUBLANES]
                ga = u_ref[rows, dl + ch * cw: dl + (ch + 1) * cw]
                _to_lane_blocks(d_s, cs, dy_ref[rows, cs] * (ga * _sigmoid(ga)))

            for j in range(dl // LANES):
                lanes = slice(j * LANES, (j + 1) * LANES)
                g_carry[:, lanes] = _scan_tile(an_s, d_s, g_s, g_carry[:, lanes], j, reverse=True)

            for ch in range(nch):
                cs = slice(ch * cw, (ch + 1) * cw)

                def acc_row(r0, val):
                    sg_ref[r0:r0 + 1, cs] += jnp.sum(val, axis=0, keepdims=True)

                def seg(s):
                    return slice(s * dl + ch * cw, s * dl + (ch + 1) * cw)

                cav = s_ref[rows, cs]
                r = r_s[:, cs]
                ig = i_s[:, cs]
                a = a_s[:, cs]
                g = _from_lane_blocks(g_s, cs)
                hsv = s_ref[rows, dl + ch * cw: dl + (ch + 1) * cw]
                lamv = lam_ref[:, cs]
                sp = _softplus(-lamv)
                la = -LRU_C * r * sp
                e2 = a * a
                one_m_e2 = -jnp.tanh(la) * (e2 + 1.0)
                mult = jnp.sqrt(one_m_e2)
                hprev = _shift_down(before(s_ref, sh_ref, slice(dl + ch * cw, dl + (ch + 1) * cw)), hsv, 1)
                icav = ig * cav
                dla = g * (hprev * a - icav * (e2 * lax.rsqrt(one_m_e2)))
                gm = g * mult
                dzi = gm * icav * (1.0 - ig)
                dca = gm * ig
                dla_r = dla * r
                dzr = dla_r * (1.0 - r) * (-LRU_C * sp)
                sg_ref[SG_LAM:SG_LAM + 1, cs] += jnp.sum(dla_r, axis=0, keepdims=True) * (LRU_C * _sigmoid(-lamv))
                acc_row(SG_BR, dzr)
                acc_row(SG_BI, dzi)
                dzr_b = dzr.astype(BF16)
                dzi_b = dzi.astype(BF16)
                cab = cav.astype(BF16)
                dca = dca + lax.dot_general(dzr_b, wr_ref[ch], nt_dims, preferred_element_type=F32)
                dca = dca + lax.dot_general(dzi_b, wi_ref[ch], nt_dims, preferred_element_type=F32)
                dwr_ref[ch] += lax.dot_general(cab, dzr_b, tn_dims, preferred_element_type=F32)
                dwi_ref[ch] += lax.dot_general(cab, dzi_b, tn_dims, preferred_element_type=F32)
                acc_row(SG_BA, dca)
                xa = u_ref[rows, seg(0)]
                head = dca_head[:, cs]
                dxa = wa_ref[3:4, cs] * dca
                acc_row(SG_WA + 3, dca * xa)
                for kk in range(3):
                    later = _shift_up(dca, head, 3 - kk)
                    acc_row(SG_WA + kk, later * xa)
                    dxa = dxa + wa_ref[kk:kk + 1, cs] * later
                dca_head[:, cs] = dca[:SUBLANES]
                ga = u_ref[rows, seg(1)]
                sga = _sigmoid(ga)
                dga = dy_ref[rows, cs] * hsv * (sga + (ga * sga) * (1.0 - sga))
                du_ref[rows, seg(0)] = dxa.astype(BF16)
                du_ref[rows, seg(1)] = dga.astype(BF16)

                bv = u_ref[rows, seg(2)]
                cv = u_ref[rows, seg(3)]
                xb = u_ref[rows, seg(4)]
                gb = u_ref[rows, seg(5)]
                dyb = dy_ref[rows, dl + ch * cw: dl + (ch + 1) * cw]
                v = cv * xb
                vh = before(u_ref, uh_ref, seg(3)) * before(u_ref, uh_ref, seg(4))
                v1 = _shift_down(vh, v, 1)
                v2 = _shift_down(vh, v, 2)
                cb = wb_ref[2:3, cs] * v + wb_ref[1:2, cs] * v1 + wb_ref[0:1, cs] * v2
                sgb = _sigmoid(gb)
                sl = gb * sgb
                dyb_b = dyb * bv
                dyb_cb = dyb * cb
                dcb = dyb_b * sl
                du_ref[rows, seg(2)] = (dyb_cb * sl).astype(BF16)
                du_ref[rows, seg(5)] = (dyb_cb * bv * (sgb + sl * (1.0 - sgb))).astype(BF16)
                bhead = dcb_head[:, cs]
                dv = wb_ref[2:3, cs] * dcb
                acc_row(SG_WB + 2, dcb * v)
                for kk in range(2):
                    later = _shift_up(dcb, bhead, 2 - kk)
                    acc_row(SG_WB + kk, later * v)
                    dv = dv + wb_ref[kk:kk + 1, cs] * later
                dcb_head[:, cs] = dcb[:SUBLANES]
                du_ref[rows, seg(3)] = (dv * xb).astype(BF16)
                du_ref[rows, seg(4)] = (dv * cv).astype(BF16)

    rev = lambda w: pl.BlockSpec((tb, w), lambda i: (nt - 1 - i, 0))
    halo = lambda w: pl.BlockSpec((SUBLANES, w), lambda i: (jnp.maximum((nt - 1 - i) * hb - 1, 0), 0))
    full = lambda shp: pl.BlockSpec(shp, lambda i: tuple(0 for _ in shp))
    vm = lambda r: pltpu.VMEM((r, dl), F32)
    return pl.pallas_call(
        body, name=name, grid=(nt,),
        in_specs=[rev(din), halo(din), rev(2 * dl), halo(2 * dl), rev(2 * dl), full((4, dl)),
                  full((nch, cw, cw)), full((1, dl)), full((nch, cw, cw)), full((1, dl)), full((1, dl)), full((3, dl))],
        out_specs=[rev(din), full((SG_ROWS, dl)), full((nch, cw, cw)), full((nch, cw, cw))],
        out_shape=[jax.ShapeDtypeStruct((tp, din), BF16), jax.ShapeDtypeStruct((SG_ROWS, dl), F32),
                   jax.ShapeDtypeStruct((nch, cw, cw), F32), jax.ShapeDtypeStruct((nch, cw, cw), F32)],
        scratch_shapes=[vm(SUBLANES), vm(SUBLANES), vm(SUBLANES), vm(SUBLANES), vm(tt), vm(tt), vm(tt)]
        + [pltpu.VMEM((dl // LANES, tt, LANES), F32)] * 3,
        compiler_params=_params(("arbitrary",)),
    )(u, u, saved, saved, dy, wa, wr_blk, br, wi_blk, bi, lam, wb)


def _adamw(w, g, m, v, *, name, landed=None, layer=None, depth=None, into=None, row_off=0):
    r, c = w.shape[-2:]
    rows = g.shape[0]
    tr = _tile(rows, 512, 2 * SUBLANES)
    assert row_off % tr == 0
    boff = row_off // tr
    bc1 = 1.0 - ADAM_B1 ** ADAM_STEP
    bc2 = 1.0 - ADAM_B2 ** ADAM_STEP
    slots = landed is not None

    def body(*refs):
        if into is not None:
            refs = refs[:-8] + refs[-4:]
        if slots:
            w_ref, g_ref, l_ref, m_ref, v_ref, grad_ref, delta_ref, nm_ref, nv_ref = refs
            gv = g_ref[...].astype(F32)
            for s in range(N_DEV - 1):
                gv = gv + l_ref[s].astype(F32)
        else:
            w_ref, g_ref, m_ref, v_ref, grad_ref, delta_ref, nm_ref, nv_ref = refs
            gv = g_ref[...]
        wv = w_ref[...]
        mn = ADAM_B1 * m_ref[...] + (1.0 - ADAM_B1) * gv
        vn = ADAM_B2 * v_ref[...] + (1.0 - ADAM_B2) * (gv * gv)
        m_hat = mn / bc1
        v_hat = vn / bc2
        grad_ref[...] = gv
        delta_ref[...] = -ADAM_LR * (m_hat / (jnp.sqrt(v_hat) + ADAM_EPS) + ADAM_WD * wv)
        nm_ref[...] = mn
        nv_ref[...] = vn

    if depth is None:
        blk = pl.BlockSpec((tr, c), lambda i: (i + boff, 0))
    else:
        blk = pl.BlockSpec((None, tr, c), lambda i: (layer, i + boff, 0))
    g_blk = pl.BlockSpec((tr, c), lambda i: (i, 0))
    l_spec = [pl.BlockSpec((N_DEV - 1, tr, c), lambda i: (0, i, 0))] if slots else []
    args = (w, g, landed, m, v) if slots else (w, g, m, v)
    in_specs = [blk, g_blk] + l_spec + [blk, blk]
    if depth is None:
        shp = jax.ShapeDtypeStruct((r, c), F32)
        out_blk = blk
    else:
        shp = jax.ShapeDtypeStruct((depth, r, c), F32)
        out_blk = pl.BlockSpec((None, tr, c), lambda i: (layer, i + boff, 0))
    aliases = {}
    if into is not None:
        aliases = {len(args) + j: j for j in range(4)}
        in_specs = in_specs + [ANY] * 4
        args = args + tuple(into)
    return pl.pallas_call(
        body, name=name, grid=(rows // tr,),
        in_specs=in_specs, out_specs=[out_blk] * 4,
        out_shape=[shp] * 4, input_output_aliases=aliases,
        compiler_params=_params(("parallel",)),
    )(*args)


def _slot_sum(g, *, name):
    _, r, c = g.shape
    tr = _tile(r, 512, SUBLANES)

    def body(g_ref, o_ref):
        gv = g_ref[0].astype(F32)
        for s in range(1, N_DEV):
            gv = gv + g_ref[s].astype(F32)
        o_ref[...] = gv

    return pl.pallas_call(
        body, name=name, grid=(r // tr,),
        in_specs=[pl.BlockSpec((N_DEV, tr, c), lambda i: (0, i, 0))],
        out_specs=pl.BlockSpec((tr, c), lambda i: (i, 0)),
        out_shape=jax.ShapeDtypeStruct((r, c), F32),
        compiler_params=_params(("parallel",)),
    )(g)


def _mesh_pos():
    x, y, c = lax.axis_index("x"), lax.axis_index("y"), lax.axis_index("c")
    return x, y, c, 4 * x + 2 * y + c


ANY = pl.BlockSpec(memory_space=pl.ANY)


GATHER_COPIES = 9


def _all_gather(srcs, out_shapes, views, *, name):
    n = len(srcs)
    SIB, X_OWN, Y_OWN, X_DIAG, Y_DIAG, SIB_X, SIB_Y, SIB_DIAG_TOP, SIB_DIAG_BOTTOM = range(GATHER_COPIES)

    def body(*refs):
        src = refs[:n]
        dst = refs[n:2 * n]
        send_sems, recv_sems, local_sems = refs[2 * n:]
        x, y, c, me = _mesh_pos()
        sibling, x_nbr, y_nbr = (x, y, 1 - c), (1 - x, y, c), (x, 1 - y, c)

        def block(a, px, py, pc, half=None):
            win = views[a](dst[a], 4 * px + 2 * py + pc)
            if half is None:
                return win
            rows = win.shape[0] // 2
            return win.at[pl.ds(half * rows, rows)]

        def copy(a, k, win, to, from_src=False):
            return pltpu.make_async_remote_copy(
                src_ref=src[a] if from_src else win, dst_ref=win,
                send_sem=send_sems.at[a * GATHER_COPIES + k], recv_sem=recv_sems.at[a * GATHER_COPIES + k],
                device_id=to, device_id_type=MESH)

        mine = [pltpu.make_async_copy(src[a], block(a, x, y, c), local_sems.at[a]) for a in range(n)]
        started = []

        def start(cp):
            cp.start()
            started.append(cp)

        for a in range(n):
            mine[a].start()
            own = block(a, x, y, c)
            start(copy(a, SIB, own, sibling, True))
            start(copy(a, X_OWN, own, x_nbr, True))
            start(copy(a, Y_OWN, own, y_nbr, True))
        for a in range(n):
            from_y = block(a, x, 1 - y, c)
            copy(a, Y_OWN, from_y, y_nbr).wait_recv()
            start(copy(a, X_DIAG, block(a, x, 1 - y, c, 0), x_nbr))
            start(copy(a, SIB_Y, from_y, sibling))
            from_x = block(a, 1 - x, y, c)
            copy(a, X_OWN, from_x, x_nbr).wait_recv()
            start(copy(a, Y_DIAG, block(a, 1 - x, y, c, 1), y_nbr))
            start(copy(a, SIB_X, from_x, sibling))
        for a in range(n):
            top = block(a, 1 - x, 1 - y, c, 0)
            copy(a, X_DIAG, top, x_nbr).wait_recv()
            start(copy(a, SIB_DIAG_TOP, top, sibling))
            bottom = block(a, 1 - x, 1 - y, c, 1)
            copy(a, Y_DIAG, bottom, y_nbr).wait_recv()
            start(copy(a, SIB_DIAG_BOTTOM, bottom, sibling))
        for a in range(n):
            copy(a, SIB, block(a, x, y, 1 - c), sibling).wait_recv()
            copy(a, SIB_X, block(a, 1 - x, y, 1 - c), sibling).wait_recv()
            copy(a, SIB_Y, block(a, x, 1 - y, 1 - c), sibling).wait_recv()
            copy(a, SIB_DIAG_TOP, block(a, 1 - x, 1 - y, 1 - c, 0), sibling).wait_recv()
            copy(a, SIB_DIAG_BOTTOM, block(a, 1 - x, 1 - y, 1 - c, 1), sibling).wait_recv()
        for cp in started:
            cp.wait_send()
        for cp in mine:
            cp.wait()

    return pl.pallas_call(
        body, name=name,
        in_specs=[ANY] * n, out_specs=[ANY] * n,
        out_shape=[jax.ShapeDtypeStruct(s, x.dtype) for s, x in zip(out_shapes, srcs)],
        scratch_shapes=[pltpu.SemaphoreType.DMA((GATHER_COPIES * n,)), pltpu.SemaphoreType.DMA((GATHER_COPIES * n,)),
                        pltpu.SemaphoreType.DMA((n,))],
    )(*srcs)


HBM = pl.BlockSpec(memory_space=pltpu.HBM)
SEM = pl.BlockSpec(memory_space=pltpu.SEMAPHORE)
EFFECT = pltpu.SideEffectType.DATAFLOW_SIDE_EFFECTING


def _peer_of(x, y, c, k):
    return (1 - x if k & 4 else x, 1 - y if k & 2 else y, 1 - c if k & 1 else c)


def _peer_copies(n, wins, src, land, send_sems, recv_sems):
    x, y, c, me = _mesh_pos()
    out = []
    for a in range(n):
        for k in range(1, N_DEV):
            px, py, pc = _peer_of(x, y, c, k)
            s_win, d_win = wins[a](src[a], land[a], me, 4 * px + 2 * py + pc, k)
            out.append(pltpu.make_async_remote_copy(
                src_ref=s_win, dst_ref=d_win,
                send_sem=send_sems.at[a * 7 + k - 1], recv_sem=recv_sems.at[a * 7 + k - 1],
                device_id=(px, py, pc), device_id_type=MESH))
    return out


def _push_start(srcs, lands, wins, *, name):
    n = len(srcs)

    def body(*refs):
        src = refs[:n]
        land = refs[n:2 * n]
        send_sems, recv_sems = refs[2 * n], refs[2 * n + 1]
        token = refs[-1]
        for cp in _peer_copies(n, wins, src, land, send_sems, recv_sems):
            cp.start()
        token[...] = jnp.zeros_like(token)

    bufs = (*srcs, *lands)
    return pl.pallas_call(
        body, name=name,
        out_shape=(pltpu.SemaphoreType.DMA((7 * n,)), pltpu.SemaphoreType.DMA((7 * n,)),
                   *[pltpu.HBM(v.shape, v.dtype) for v in bufs], jax.ShapeDtypeStruct((SUBLANES, LANES), F32)),
        in_specs=[HBM] * (2 * n),
        out_specs=(SEM, SEM, *[HBM] * (2 * n), pl.BlockSpec(memory_space=pltpu.VMEM)),
        input_output_aliases={i: 2 + i for i in range(2 * n)},
        compiler_params=pltpu.CompilerParams(has_side_effects=EFFECT),
    )(*[pltpu.with_memory_space_constraint(v, pltpu.HBM) for v in bufs])


def _push_wait(handle, wins, after, *, name):
    send_sems, recv_sems, *bufs, _ = handle
    n = len(bufs) // 2

    def body(*refs):
        src = refs[:n]
        land = refs[n:2 * n]
        for cp in _peer_copies(n, wins, src, land, refs[2 * n], refs[2 * n + 1]):
            cp.wait_send()
            cp.wait_recv()

    outs = pl.pallas_call(
        body, name=name,
        out_shape=tuple(pltpu.HBM(v.shape, v.dtype) for v in bufs),
        in_specs=[HBM] * (2 * n) + [SEM, SEM, ANY],
        out_specs=tuple([HBM] * (2 * n)),
        input_output_aliases={i: i for i in range(2 * n)},
        compiler_params=pltpu.CompilerParams(has_side_effects=EFFECT),
    )(*bufs, send_sems, recv_sems, after)
    return outs[:n], outs[n:]


def _gather_lead(src, land, me, peer, k):
    return src, land.at[me]


def _gather_cols(width):
    def win(src, land, me, peer, k):
        return src, land.at[:, pl.ds(me * width, width)]
    return win


def _scatter_lead(src, land, me, peer, k):
    return src.at[peer], land.at[k - 1]


def _scatter_cols(width):
    def win(src, land, me, peer, k):
        return src.at[:, pl.ds(peer * width, width)], land.at[k - 1]
    return win


def _place_block(own, *, cols, name):
    rows, width = own.shape
    tr = _tile(rows, 512, 2 * SUBLANES)
    _, _, _, me = _mesh_pos()

    def body(me_ref, x_ref, o_ref):
        o_ref[...] = x_ref[...]

    if cols:
        out_spec = pl.BlockSpec((tr, width), lambda i, me_ref: (i, me_ref[0]))
        shape = (rows, N_DEV * width)
    else:
        out_spec = pl.BlockSpec((None, tr, width), lambda i, me_ref: (me_ref[0], i, 0))
        shape = (N_DEV, rows, width)
    return pl.pallas_call(
        body, name=name,
        grid_spec=pltpu.PrefetchScalarGridSpec(
            num_scalar_prefetch=1, grid=(rows // tr,),
            in_specs=[pl.BlockSpec((tr, width), lambda i, me_ref: (i, 0))], out_specs=out_spec),
        out_shape=jax.ShapeDtypeStruct(shape, own.dtype),
        compiler_params=_params(("arbitrary",)),
    )(me.astype(jnp.int32).reshape(1), own)


def _dep(x, token):
    return x + token[0, 0].astype(x.dtype)


def _lead(ref, d):
    return ref.at[d]


def _col_window(width):
    def view(ref, d):
        return ref.at[:, pl.ds(d * width, width)]
    return view


def _pack(arrs):
    flat = jnp.concatenate([a.reshape(-1).astype(F32) for a in arrs])
    n = flat.shape[0]
    rows = -(-n // (2 * SUBLANES * LANES)) * 2 * SUBLANES
    return jnp.pad(flat, (0, rows * LANES - n)).reshape(rows, LANES)


def _unpack(buf, shapes):
    flat = buf.reshape(-1)
    out, off = [], 0
    for s in shapes:
        n = 1
        for q in s:
            n *= q
        out.append(flat[off:off + n].reshape(s))
        off += n
    return out


def _blockdiag(w, cw):
    h, hd, _ = w.shape
    per = cw // hd
    wg = w.reshape(h // per, per, hd, hd)
    eye = jnp.eye(per, dtype=w.dtype)
    blk = jnp.einsum("gpij,pq->gpiqj", wg, eye)
    return blk.reshape(h // per, cw, cw).astype(BF16)


def _blockdiag_extract(g, hd):
    n, cw, _ = g.shape
    per = cw // hd
    g5 = g.reshape(n, per, hd, per, hd)
    idx = jnp.arange(per)
    return g5[:, idx, :, idx, :].transpose(1, 0, 2, 3).reshape(n * per, hd, hd)


def kernel(x, meta, norm_g, w_in, conv_a_w, conv_a_b, lru_wr, lru_br, lru_wi, lru_bi, lru_lambda, conv_b_w, w_out, final_g, loss_target, m_meta, m_norm_g, m_w_in, m_conv_a_w, m_conv_a_b, m_lru_wr, m_lru_br, m_lru_wi, m_lru_bi, m_lru_lambda, m_conv_b_w, m_w_out, m_final_g, v_meta, v_norm_g, v_w_in, v_conv_a_w, v_conv_a_b, v_lru_wr, v_lru_br, v_lru_wi, v_lru_bi, v_lru_lambda, v_conv_b_w, v_w_out, v_final_g):
    _, seq, d = x.shape
    n_meta = meta.shape[0]
    depth = w_in.shape[0]
    din = w_in.shape[2] * N_DEV
    dl = din // 6
    dmix = 2 * dl
    wcol = w_in.shape[2]
    wrow = w_out.shape[1]
    mcol = meta.shape[1]
    ccol = conv_a_w.shape[2]
    hd = lru_wr.shape[2]
    n_tok = n_meta + seq
    tp = -(-n_tok // TOKEN_TILE) * TOKEN_TILE
    me = 4 * lax.axis_index("x") + 2 * lax.axis_index("y") + lax.axis_index("c")

    bf = lambda a: a.astype(BF16)
    small_mine = _pack([meta, conv_a_w, conv_b_w])
    first = _all_gather([bf(w_in[0]), small_mine], [(d, din), (N_DEV,) + small_mine.shape],
                        [_col_window(wcol), _lead], name="gather_first")
    flat = first[1].reshape(N_DEV, -1)
    sizes = [meta.size, conv_a_w.size, conv_b_w.size]
    meta_full = jnp.moveaxis(flat[:, :sizes[0]].reshape(N_DEV, n_meta, mcol), 0, 1).reshape(n_meta, d)
    wa_full = jnp.moveaxis(flat[:, sizes[0]:sizes[0] + sizes[1]].reshape(N_DEV, depth, 4, ccol), 0, 2) \
        .reshape(depth, 4, dl)
    wb_full = jnp.moveaxis(flat[:, sizes[0] + sizes[1]:sum(sizes)].reshape(N_DEV, depth, 3, ccol), 0, 2) \
        .reshape(depth, 3, dl)
    w_in_full = [None] * depth
    w_out_full = [None] * depth

    push_out = [None] * depth
    push_in = [None] * depth
    w_in_full[0], src = lax.optimization_barrier((first[0], bf(w_out[0])))
    push_out[0] = _push_start([src], [_place_block(src, cols=False, name="place_wout_0")], [_gather_lead],
                              name="gather_wout_0_start")
    token = push_out[0][-1]
    for l in range(1, depth):
        src = bf(_dep(w_in[l], token))
        push_in[l] = _push_start([src], [_place_block(src, cols=True, name=f"place_win_{l}")], [_gather_cols(wcol)],
                                 name=f"gather_win_{l}_start")
        src = bf(_dep(w_out[l], push_in[l][-1]))
        push_out[l] = _push_start([src], [_place_block(src, cols=False, name=f"place_wout_{l}")], [_gather_lead],
                                  name=f"gather_wout_{l}_start")
        token = push_out[l][-1]

    wr_blk = [_blockdiag(lru_wr[l], GATE_BLOCK) for l in range(depth)]
    wi_blk = [_blockdiag(lru_wi[l], GATE_BLOCK) for l in range(depth)]
    vec = lambda a: a.reshape(1, dl)

    tm = _tile(tp, 1408)
    saved = []
    for l in range(depth):
        if l == 0:
            h, hn = _rms_fwd_first(x[0], meta_full, _dep(norm_g[l], token), tp=tp, name=f"rms_fwd_{l}")
        else:
            hn = _rms_fwd(h, norm_g[l], name=f"rms_fwd_{l}")
        if l > 0:
            _, landed = _push_wait(push_in[l], [_gather_cols(wcol)], hn, name=f"gather_win_{l}_wait")
            w_in_full[l] = landed[0]
        u = _matmul(hn, w_in_full[l], tm=tm, tn=_tile(din, 1536), tk=d, name=f"mm_u_{l}")
        mixed, y = _mixer_fwd(u, wa_full[l], vec(conv_a_b[l]), wr_blk[l], vec(lru_br[l]), wi_blk[l], vec(lru_bi[l]),
                              vec(lru_lambda[l]), wb_full[l], name=f"mixer_fwd_{l}")
        _, landed = _push_wait(push_out[l], [_gather_lead], y, name=f"gather_wout_{l}_wait")
        w_out_full[l] = landed[0].reshape(dmix, d)
        h_next = _matmul(y, w_out_full[l], tm=tm, tn=_tile(d, 512), tk=dmix, add=h, name=f"mm_out_{l}")
        saved.append((h, hn, u, mixed, y))
        h = h_next

    dh, dhb, dg_final, loss_part = _loss_head(h, loss_target[0], final_g, n_meta=n_meta, n_tok=n_tok,
                                              name="loss_head")

    small_grads = [None] * depth
    sent_out = [None] * depth
    sent_in = [None] * depth
    scatter_in = [_scatter_cols(wcol)]
    token = None
    dg_norms = []
    for l in reversed(range(depth)):
        h_in, hn, u, mixed, y = saved[l]
        dy = _matmul(dhb, w_out_full[l], tb=True, tm=tm, tn=_tile(dmix, 1024), tk=d, dep=token, name=f"mm_dy_{l}")
        dw_out = _matmul(y, dhb, ta=True, tm=_tile(dmix, 1024), tn=_tile(d, 1024), tk=tp, out_dtype=BF16,
                         name=f"mm_dwout_{l}")
        sent_out[l] = _push_start([dw_out.reshape(N_DEV, wrow, d)], [lax.empty((N_DEV - 1, wrow, d), BF16)],
                                  [_scatter_lead], name=f"scatter_wout_{l}_start")
        du, sg, dwr, dwi = _mixer_bwd(u, mixed, dy, wa_full[l], wr_blk[l], vec(lru_br[l]), wi_blk[l], vec(lru_bi[l]),
                                      vec(lru_lambda[l]), _dep(wb_full[l], sent_out[l][-1]), name=f"mixer_bwd_{l}")
        small_grads[l] = (sg, dwr, dwi)
        if l == 0:
            rows = jnp.stack([small_grads[j][0] for j in range(depth)])
            early = [_pack([
                rows[:, SG_BA], rows[:, SG_BR], rows[:, SG_BI], rows[:, SG_LAM], rows[:, SG_WA:SG_WA + 4],
                rows[:, SG_WB:SG_WB + 3], dg_final[0], *dg_norms]),
                _pack([jnp.stack([_blockdiag_extract(small_grads[j][1], hd) for j in range(depth)]),
                       jnp.stack([_blockdiag_extract(small_grads[j][2], hd) for j in range(depth)])]).astype(BF16)]
            early_land = [lax.dynamic_update_slice(lax.empty((N_DEV,) + a.shape, a.dtype), a[None], (me, 0, 0))
                          for a in early]
            sent_early = _push_start(early, early_land, [_gather_lead] * 2, name="gather_early_grads_start")
        parts = 2 if l == 0 else 1
        token = sent_early[-1] if l == 0 else None
        sent_in[l] = []
        for p in range(parts):
            dw_in = _matmul(hn, du, ta=True, tm=_tile(d // parts, 512), tn=_tile(din, 1536), tk=tp, out_dtype=BF16,
                            dep=token, m_part=(p, parts), name=f"mm_dwin_{l}_{p}")
            sent_in[l].append(_push_start([dw_in], [lax.empty((N_DEV - 1, d // parts, wcol), BF16)], scatter_in,
                                          name=f"scatter_win_{l}_{p}_start"))
            token = sent_in[l][-1][-1]
        dhn = _matmul(du, w_in_full[l], tb=True, tm=_tile(tp, 528, 2 * SUBLANES), tn=_tile(d, 1024), tk=din, dep=token,
                      name=f"mm_dhn_{l}")
        if l > 0:
            dh, dhb, dg_norm = _rms_bwd(h_in, dhn, dh, norm_g[l], name=f"rms_bwd_{l}")
            dg_norms.append(dg_norm[0])
        else:
            grad_x, d_meta, dg_norm = _rms_bwd_first(h_in, dhn, dh, norm_g[l], n_meta=n_meta, seq=seq,
                                                     name=f"rms_bwd_{l}")

    big = {"win": None, "wout": None}

    def big_adamw(l, after):
        src, landed = _push_wait(sent_out[l], [_scatter_lead], after, name=f"scatter_wout_{l}_wait")
        own = lax.dynamic_index_in_dim(src[0], me, 0, keepdims=False)
        big["wout"] = _adamw(w_out, own, m_w_out, v_w_out, landed=landed[0], layer=l, depth=depth,
                             into=big["wout"], name=f"adamw_w_out_{l}")
        after = big["wout"][0]
        for p, sent in enumerate(sent_in[l]):
            src, landed = _push_wait(sent, scatter_in, after, name=f"scatter_win_{l}_{p}_wait")
            own = lax.dynamic_slice_in_dim(src[0], me * wcol, wcol, axis=1)
            big["win"] = _adamw(w_in, own, m_w_in, v_w_in, landed=landed[0], layer=l, depth=depth,
                                into=big["win"], row_off=p * own.shape[0], name=f"adamw_w_in_{l}_{p}")
            after = big["win"][0]
        return after

    after = dg_norm
    for l in reversed(range(1, depth)):
        after = big_adamw(l, after)

    late = _pack([dg_norm[0], d_meta, loss_part[0:1, 0:1]])
    if depth > 1:
        late, after = lax.optimization_barrier((late, after))
    late_all = _all_gather([late], [(N_DEV,) + late.shape], [_lead], name="gather_late_grads")[0]
    late_sum = _unpack(_slot_sum(late_all, name="sum_late_grads"), [(d,), (n_meta, d), ()])
    loss = late_sum[2]
    _, early_all = _push_wait(sent_early, [_gather_lead] * 2, late_sum[0], name="gather_early_grads_wait")
    vec_shapes = [conv_a_b.shape, lru_br.shape, lru_bi.shape, lru_lambda.shape, (depth, 4, dl), (depth, 3, dl),
                  final_g.shape] + [(d,)] * (depth - 1)
    e = _unpack(_slot_sum(early_all[0], name="sum_early_vectors"), vec_shapes)
    g_wr, g_wi = _unpack(_slot_sum(early_all[1], name="sum_early_maps"), [lru_wr.shape, lru_wi.shape])
    g_norm = jnp.stack([late_sum[0]] + e[7:][::-1])
    g_meta = lax.dynamic_slice_in_dim(late_sum[1], me * mcol, mcol, axis=1)
    g_wa = lax.dynamic_slice_in_dim(e[4], me * ccol, ccol, axis=2)
    g_wb = lax.dynamic_slice_in_dim(e[5], me * ccol, ccol, axis=2)

    small_w = [norm_g, conv_a_b, lru_wr, lru_br, lru_wi, lru_bi, lru_lambda, final_g, meta, conv_a_w, conv_b_w]
    small_m = [m_norm_g, m_conv_a_b, m_lru_wr, m_lru_br, m_lru_wi, m_lru_bi, m_lru_lambda, m_final_g, m_meta,
               m_conv_a_w, m_conv_b_w]
    small_v = [v_norm_g, v_conv_a_b, v_lru_wr, v_lru_br, v_lru_wi, v_lru_bi, v_lru_lambda, v_final_g, v_meta,
               v_conv_a_w, v_conv_b_w]
    small_g = [g_norm, e[0], g_wr, e[1], g_wi, e[2], e[3], e[6], g_meta, g_wa, g_wb]
    small_out = _adamw(_pack(small_w), _pack(small_g), _pack(small_m), _pack(small_v), name="adamw_small")
    small_shapes = [a.shape for a in small_w]
    s_grad, s_delta, s_m, s_v = [_unpack(o, small_shapes) for o in small_out]

    big_adamw(0, small_out[0])
    win_out, wout_out = big["win"], big["wout"]

    names = ["norm_g", "conv_a_b", "lru_wr", "lru_br", "lru_wi", "lru_bi", "lru_lambda", "final_g", "meta",
             "conv_a_w", "conv_b_w"]
    order = ["meta", "norm_g", "w_in", "conv_a_w", "conv_a_b", "lru_wr", "lru_br", "lru_wi", "lru_bi", "lru_lambda",
             "conv_b_w", "w_out", "final_g"]

    def family(idx, small):
        table = {nm: small[i] for i, nm in enumerate(names)}
        table["w_in"] = win_out[idx]
        table["w_out"] = wout_out[idx]
        return [table[nm] for nm in order]

    return (loss, grad_x, *family(0, s_grad), *family(1, s_delta), *family(2, s_m), *family(3, s_v))
```

```python
import jax
import jax.numpy as jnp
from jax import lax
from jax.experimental import pallas as pl
from jax.experimental.pallas import tpu as pltpu

F32 = jnp.float32
BF16 = jnp.bfloat16
MESH = pl.DeviceIdType.MESH

N_DEV = 8
RMS_EPS = 1e-6
LRU_C = 8.0
ADAM_LR = 0.001
ADAM_B1 = 0.9
ADAM_B2 = 0.999
ADAM_EPS = 1e-08
ADAM_WD = 0.01
ADAM_STEP = 10

V7X_VMEM_LIMIT = 52 * 1024 * 1024
LANES = 128
SUBLANES = 8
TOKEN_TILE = 384
MIX_ROWS = 128
MIX_SUBTILES = 3
GATE_BLOCK = 128


def _params(sem):
    return pltpu.CompilerParams(dimension_semantics=sem, vmem_limit_bytes=V7X_VMEM_LIMIT)


def _tile(n, target, align=LANES):
    best = None
    for t in range(align, min(n, target) + 1, align):
        if n % t == 0:
            best = t
    return n if best is None else best


def _sigmoid(z):
    return 0.5 * jnp.tanh(0.5 * z) + 0.5


def _softplus(z):
    e = jnp.exp(-jnp.abs(z))
    u = 1.0 + e
    l1p = jnp.where(u == 1.0, e, jnp.log(u) * e / jnp.where(u == 1.0, 1.0, u - 1.0))
    return jnp.maximum(z, 0.0) + l1p


def _matmul(a, b, *, ta=False, tb=False, tm, tn, tk, out_dtype=F32, add=None, dep=None, m_part=None, name):
    m, k = (a.shape[1], a.shape[0]) if ta else a.shape
    m_off = 0
    if m_part is not None:
        assert add is None and m % (m_part[1] * tm) == 0
        m //= m_part[1]
        m_off = m_part[0] * (m // tm)
    n, kb = b.shape if tb else b.shape[::-1]
    assert kb == k
    assert m % tm == 0 and n % tn == 0 and k % tk == 0, (m, n, k, tm, tn, tk)
    nk = k // tk
    a_spec = pl.BlockSpec((tk, tm), lambda i, j, q: (q, i + m_off)) if ta \
        else pl.BlockSpec((tm, tk), lambda i, j, q: (i + m_off, q))
    b_spec = pl.BlockSpec((tn, tk), lambda i, j, q: (j, q)) if tb else pl.BlockSpec((tk, tn), lambda i, j, q: (q, j))
    o_spec = pl.BlockSpec((tm, tn), lambda i, j, q: (i, j))
    o_shape = (m, n)
    dims = (((0 if ta else 1,), (1 if tb else 0,)), ((), ()))
    has_add = add is not None
    has_dep = dep is not None

    def body(*refs):
        if has_dep:
            refs = refs[:-3] + refs[-2:]
        if has_add:
            a_ref, b_ref, add_ref, o_ref, acc_ref = refs
        else:
            a_ref, b_ref, o_ref, acc_ref = refs
        q = pl.program_id(2)
        part = lax.dot_general(a_ref[...], b_ref[...], dims, preferred_element_type=F32)

        def finish(acc):
            if has_add:
                acc = acc + add_ref[...]
            o_ref[...] = acc.astype(out_dtype)

        if nk == 1:
            finish(part)
        else:
            @pl.when(q == 0)
            def _():
                acc_ref[...] = part

            @pl.when(jnp.logical_and(q > 0, q < nk - 1))
            def _():
                acc_ref[...] += part

            @pl.when(q == nk - 1)
            def _():
                finish(acc_ref[...] + part)

    in_specs = [a_spec, b_spec] + ([o_spec] if has_add else [])
    args = (a, b) + ((add,) if has_add else ())
    if has_dep:
        in_specs.append(pl.BlockSpec((SUBLANES, LANES), lambda i, j, q: (0, 0)))
        args += (dep,)
    acc_shape = (tm, tn) if nk > 1 else (SUBLANES, LANES)
    return pl.pallas_call(
        body, name=name,
        grid=(m // tm, n // tn, nk),
        in_specs=in_specs, out_specs=o_spec,
        out_shape=jax.ShapeDtypeStruct(o_shape, out_dtype),
        scratch_shapes=[pltpu.VMEM(acc_shape, F32)],
        compiler_params=_params(("parallel", "parallel", "arbitrary")),
    )(*args)


def _rms_fwd(h, g, *, name):
    tp, d = h.shape
    tr = _tile(tp, 512, SUBLANES)

    def body(h_ref, g_ref, o_ref):
        hv = h_ref[...]
        rstd = lax.rsqrt(jnp.mean(hv * hv, axis=-1, keepdims=True) + RMS_EPS)
        o_ref[...] = (hv * rstd * g_ref[...]).astype(BF16)

    return pl.pallas_call(
        body, name=name, grid=(tp // tr,),
        in_specs=[pl.BlockSpec((tr, d), lambda i: (i, 0)), pl.BlockSpec((1, d), lambda i: (0, 0))],
        out_specs=pl.BlockSpec((tr, d), lambda i: (i, 0)),
        out_shape=jax.ShapeDtypeStruct((tp, d), BF16),
        compiler_params=_params(("parallel",)),
    )(h, g.reshape(1, d))


def _rms_fwd_first(x, meta, g, *, tp, name):
    seq, d = x.shape
    n_meta = meta.shape[0]
    n_tok = n_meta + seq
    tr = TOKEN_TILE
    assert tp % tr == 0 and tr % n_meta == 0
    per = tr // n_meta

    def body(x_ref, xp_ref, m_ref, g_ref, h_ref, o_ref):
        i = pl.program_id(0)
        head = jnp.where(i == 0, m_ref[...], xp_ref[...])
        rows = i * tr + lax.broadcasted_iota(jnp.int32, (tr, 1), 0)
        hv = jnp.where(rows < n_tok, jnp.concatenate([head, x_ref[:tr - n_meta, :]], axis=0), 0.0)
        h_ref[...] = hv
        rstd = lax.rsqrt(jnp.mean(hv * hv, axis=-1, keepdims=True) + RMS_EPS)
        o_ref[...] = (hv * rstd * g_ref[...]).astype(BF16)

    row = pl.BlockSpec((tr, d), lambda i: (i, 0))
    own = pl.BlockSpec((tr, d), lambda i: (jnp.minimum(i, -(-seq // tr) - 1), 0))
    before = pl.BlockSpec((n_meta, d), lambda i: (jnp.maximum(i * per - 1, 0), 0))
    return pl.pallas_call(
        body, name=name, grid=(tp // tr,),
        in_specs=[own, before, pl.BlockSpec((n_meta, d), lambda i: (0, 0)), pl.BlockSpec((1, d), lambda i: (0, 0))],
        out_specs=[row, row],
        out_shape=[jax.ShapeDtypeStruct((tp, d), F32), jax.ShapeDtypeStruct((tp, d), BF16)],
        compiler_params=_params(("parallel",)),
    )(x, x, meta, g.reshape(1, d))


def _rms_bwd(h, dhn, dout, g, *, name):
    tp, d = h.shape
    tr = _tile(tp, 528, 2 * SUBLANES)

    def body(h_ref, dhn_ref, dout_ref, g_ref, dh_ref, dhb_ref, dg_ref):
        hv = h_ref[...]
        rstd = lax.rsqrt(jnp.mean(hv * hv, axis=-1, keepdims=True) + RMS_EPS)
        xhat = hv * rstd
        dn = dhn_ref[...]
        dxhat = dn * g_ref[...]
        dh = dout_ref[...] + rstd * (dxhat - xhat * jnp.mean(dxhat * xhat, axis=-1, keepdims=True))
        dh_ref[...] = dh
        dhb_ref[...] = dh.astype(BF16)
        part = jnp.sum(dn * xhat, axis=0, keepdims=True)

        @pl.when(pl.program_id(0) == 0)
        def _():
            dg_ref[...] = part

        @pl.when(pl.program_id(0) > 0)
        def _():
            dg_ref[...] += part

    row = pl.BlockSpec((tr, d), lambda i: (i, 0))
    vec = pl.BlockSpec((1, d), lambda i: (0, 0))
    return pl.pallas_call(
        body, name=name, grid=(tp // tr,),
        in_specs=[row, row, row, vec],
        out_specs=[row, row, vec],
        out_shape=[jax.ShapeDtypeStruct((tp, d), F32), jax.ShapeDtypeStruct((tp, d), BF16),
                   jax.ShapeDtypeStruct((1, d), F32)],
        compiler_params=_params(("arbitrary",)),
    )(h, dhn, dout, g.reshape(1, d))


def _rms_bwd_first(h, dhn, dout, g, *, n_meta, seq, name):
    tp, d = h.shape
    tr = _tile(seq, 512)
    assert seq % tr == 0 and tr % n_meta == 0 and tp >= seq + n_meta
    nt = seq // tr
    per = tr // n_meta

    def grads(hv, dn, do, gv):
        rstd = lax.rsqrt(jnp.mean(hv * hv, axis=-1, keepdims=True) + RMS_EPS)
        xhat = hv * rstd
        dxhat = dn * gv
        dh = do + rstd * (dxhat - xhat * jnp.mean(dxhat * xhat, axis=-1, keepdims=True))
        return dh, jnp.sum(dn * xhat, axis=0, keepdims=True)

    def body(h_ref, dhn_ref, dout_ref, hn_ref, dhnn_ref, doutn_ref, g_ref, gx_ref, dmeta_ref, dg_ref):
        i = pl.program_id(0)
        gv = g_ref[...]
        dh, part = grads(h_ref[...], dhn_ref[...], dout_ref[...], gv)
        dh_next, part_next = grads(hn_ref[...], dhnn_ref[...], doutn_ref[...], gv)
        gx_ref[...] = jnp.concatenate([dh[n_meta:], dh_next], axis=0)

        @pl.when(i == 0)
        def _():
            dmeta_ref[...] = dh[:n_meta]
            dg_ref[...] = part

        @pl.when(i > 0)
        def _():
            dg_ref[...] += part

        @pl.when(i == nt - 1)
        def _():
            dg_ref[...] += part_next

    row = pl.BlockSpec((tr, d), lambda i: (i, 0))
    nxt = pl.BlockSpec((n_meta, d), lambda i: ((i + 1) * per, 0))
    vec = pl.BlockSpec((1, d), lambda i: (0, 0))
    return pl.pallas_call(
        body, name=name, grid=(nt,),
        in_specs=[row, row, row, nxt, nxt, nxt, vec],
        out_specs=[pl.BlockSpec((None, tr, d), lambda i: (0, i, 0)), pl.BlockSpec((n_meta, d), lambda i: (0, 0)), vec],
        out_shape=[jax.ShapeDtypeStruct((1, seq, d), F32), jax.ShapeDtypeStruct((n_meta, d), F32),
                   jax.ShapeDtypeStruct((1, d), F32)],
        compiler_params=_params(("arbitrary",)),
    )(h, dhn, dout, h, dhn, dout, g.reshape(1, d))


def _loss_head(h, tgt, g, *, n_meta, n_tok, name):
    tp, d = h.shape
    seq = tgt.shape[0]
    tr = TOKEN_TILE
    assert tp % tr == 0 and tr % n_meta == 0
    per = tr // n_meta

    def body(h_ref, t_ref, tp_ref, g_ref, dh_ref, dhb_ref, dg_ref, loss_ref):
        i = pl.program_id(0)
        hv = h_ref[...]
        rstd = lax.rsqrt(jnp.mean(hv * hv, axis=-1, keepdims=True) + RMS_EPS)
        xhat = hv * rstd
        gv = g_ref[...]
        rows = i * tr + lax.broadcasted_iota(jnp.int32, (tr, 1), 0)
        valid = jnp.logical_and(rows >= n_meta, rows < n_tok)
        target = jnp.concatenate([tp_ref[...], t_ref[:tr - n_meta, :]], axis=0)
        err = jnp.where(valid, xhat * gv - target, 0.0)
        dy = err * (1.0 / d)
        dxhat = dy * gv
        dh = rstd * (dxhat - xhat * jnp.mean(dxhat * xhat, axis=-1, keepdims=True))
        dh_ref[...] = dh
        dhb_ref[...] = dh.astype(BF16)
        dg_part = jnp.sum(dy * xhat, axis=0, keepdims=True)
        per_row = jnp.sum(err * err, axis=-1, keepdims=True) * (1.0 / d)
        loss_part = jnp.broadcast_to(0.5 * jnp.sum(per_row, axis=0, keepdims=True), (SUBLANES, LANES))

        @pl.when(i == 0)
        def _():
            dg_ref[...] = dg_part
            loss_ref[...] = loss_part

        @pl.when(i > 0)
        def _():
            dg_ref[...] += dg_part
            loss_ref[...] += loss_part

    row = pl.BlockSpec((tr, d), lambda i: (i, 0))
    vec = pl.BlockSpec((1, d), lambda i: (0, 0))
    own = pl.BlockSpec((tr, d), lambda i: (jnp.minimum(i, -(-seq // tr) - 1), 0))
    before = pl.BlockSpec((n_meta, d), lambda i: (jnp.maximum(i * per - 1, 0), 0))
    return pl.pallas_call(
        body, name=name, grid=(tp // tr,),
        in_specs=[row, own, before, vec],
        out_specs=[row, row, vec, pl.BlockSpec((SUBLANES, LANES), lambda i: (0, 0))],
        out_shape=[jax.ShapeDtypeStruct((tp, d), F32), jax.ShapeDtypeStruct((tp, d), BF16),
                   jax.ShapeDtypeStruct((1, d), F32), jax.ShapeDtypeStruct((SUBLANES, LANES), F32)],
        compiler_params=_params(("arbitrary",)),
    )(h, tgt, tgt, g.reshape(1, d))


def _shift_down(halo, tile, s):
    if s == 0:
        return tile
    ext = jnp.concatenate([halo, tile], axis=0)
    return pltpu.roll(ext, s, 0)[SUBLANES:]


def _shift_up(tile, head, s):
    if s == 0:
        return tile
    ext = jnp.concatenate([tile, head], axis=0)
    n = ext.shape[0]
    return pltpu.roll(ext, n - s, 0)[: tile.shape[0]]


def _to_lane_blocks(ref, cols, val):
    for j in range(cols.start // LANES, cols.stop // LANES):
        ref[j] = val[:, j * LANES - cols.start:(j + 1) * LANES - cols.start]


def _from_lane_blocks(ref, cols):
    return jnp.concatenate([ref[j] for j in range(cols.start // LANES, cols.stop // LANES)], axis=1)


def _scan_tile(a_ref, b_ref, out_ref, carry, j, *, reverse):
    ng = a_ref.shape[1] // SUBLANES
    order = list(range(SUBLANES))[::-1] if reverse else list(range(SUBLANES))

    def rows(r):
        return pl.ds(r, ng, stride=SUBLANES)

    prod, loc = {}, {}
    prev = None
    for r in order:
        ar = a_ref[j, rows(r), :]
        br = b_ref[j, rows(r), :]
        prod[r] = ar if prev is None else ar * prod[prev]
        loc[r] = br if prev is None else ar * loc[prev] + br
        prev = r
    pg, lg = prod[prev], loc[prev]
    ones = jnp.ones((SUBLANES,) + pg.shape[1:], F32)
    zeros = jnp.zeros_like(ones)
    s = 1
    while s < ng:
        p_sh = _shift_up(pg, ones, s) if reverse else _shift_down(ones, pg, s)
        l_sh = _shift_up(lg, zeros, s) if reverse else _shift_down(zeros, lg, s)
        lg = pg * l_sh + lg
        pg = pg * p_sh
        s *= 2
    leaving = pg * carry[0:1, :] + lg
    entering = _shift_up(leaving, carry, 1) if reverse else _shift_down(carry, leaving, 1)
    for r in order:
        out_ref[j, rows(r), :] = loc[r] + prod[r] * entering
    last = leaving[0:1, :] if reverse else leaving[ng - 1:ng, :]
    return jnp.broadcast_to(last, carry.shape)


def _gates(ca, wr, wi, br, bi, sp):
    cab = ca.astype(BF16)
    r = _sigmoid(jnp.dot(cab, wr, preferred_element_type=F32) + br)
    ig = _sigmoid(jnp.dot(cab, wi, preferred_element_type=F32) + bi)
    la = -LRU_C * r * sp
    a = jnp.exp(la)
    mult = jnp.sqrt(-jnp.tanh(la) * (a * a + 1.0))
    return r, ig, a, mult


def _mixer_fwd(u, wa, ba, wr_blk, br, wi_blk, bi, lam, wb, *, name):
    tp, din = u.shape
    dl = din // 6
    tt = MIX_ROWS
    cw = GATE_BLOCK
    nch = dl // cw
    assert tp % tt == 0 and dl % cw == 0

    def body(u_ref, wa_ref, ba_ref, wr_ref, br_ref, wi_ref, bi_ref, lam_ref, wb_ref,
             s_ref, y_ref, xa_tail, v_tail, h_carry, a_s, b_s, h_s):
        @pl.when(pl.program_id(0) == 0)
        def _():
            xa_tail[...] = jnp.zeros_like(xa_tail)
            v_tail[...] = jnp.zeros_like(v_tail)
            h_carry[...] = jnp.zeros_like(h_carry)

        for sub in range(MIX_SUBTILES):
            rows = slice(sub * tt, (sub + 1) * tt)
            for ch in range(nch):
                cs = slice(ch * cw, (ch + 1) * cw)

                def seg(s):
                    return slice(s * dl + ch * cw, s * dl + (ch + 1) * cw)

                xa = u_ref[rows, seg(0)]
                halo = xa_tail[:, cs]
                ca = ba_ref[:, cs] + wa_ref[3:4, cs] * xa
                for kk in range(3):
                    ca = ca + wa_ref[kk:kk + 1, cs] * _shift_down(halo, xa, 3 - kk)
                xa_tail[:, cs] = xa[tt - SUBLANES:]
                s_ref[rows, cs] = ca
                sp = _softplus(-lam_ref[:, cs])
                _, ig, a, mult = _gates(ca, wr_ref[ch], wi_ref[ch], br_ref[:, cs], bi_ref[:, cs], sp)
                _to_lane_blocks(a_s, cs, a)
                _to_lane_blocks(b_s, cs, mult * (ig * ca))

                bv = u_ref[rows, seg(2)]
                v = u_ref[rows, seg(3)] * u_ref[rows, seg(4)]
                gb = u_ref[rows, seg(5)]
                vh = v_tail[:, cs]
                cb = wb_ref[2:3, cs] * v
                for kk in range(2):
                    cb = cb + wb_ref[kk:kk + 1, cs] * _shift_down(vh, v, 2 - kk)
                v_tail[:, cs] = v[tt - SUBLANES:]
                y_ref[rows, dl + ch * cw: dl + (ch + 1) * cw] = (bv * cb * (gb * _sigmoid(gb))).astype(BF16)

            for ch in range(nch):
                cs = slice(ch * cw, (ch + 1) * cw)
                for j in range(cs.start // LANES, cs.stop // LANES):
                    lanes = slice(j * LANES, (j + 1) * LANES)
                    h_carry[:, lanes] = _scan_tile(a_s, b_s, h_s, h_carry[:, lanes], j, reverse=False)
                hsv = _from_lane_blocks(h_s, cs)
                s_ref[rows, dl + ch * cw: dl + (ch + 1) * cw] = hsv
                ga = u_ref[rows, dl + ch * cw: dl + (ch + 1) * cw]
                y_ref[rows, cs] = (hsv * (ga * _sigmoid(ga))).astype(BF16)

    tb = tt * MIX_SUBTILES
    assert tp % tb == 0
    row = lambda w: pl.BlockSpec((tb, w), lambda i: (i, 0))
    full = lambda shp: pl.BlockSpec(shp, lambda i: tuple(0 for _ in shp))
    return pl.pallas_call(
        body, name=name, grid=(tp // tb,),
        in_specs=[row(din), full((4, dl)), full((1, dl)), full((nch, cw, cw)), full((1, dl)),
                  full((nch, cw, cw)), full((1, dl)), full((1, dl)), full((3, dl))],
        out_specs=[row(2 * dl), row(2 * dl)],
        out_shape=[jax.ShapeDtypeStruct((tp, 2 * dl), F32), jax.ShapeDtypeStruct((tp, 2 * dl), BF16)],
        scratch_shapes=[pltpu.VMEM((SUBLANES, dl), F32), pltpu.VMEM((SUBLANES, dl), F32),
                        pltpu.VMEM((SUBLANES, dl), F32)] + [pltpu.VMEM((dl // LANES, tt, LANES), F32)] * 3,
        compiler_params=_params(("arbitrary",)),
    )(u, wa, ba, wr_blk, br, wi_blk, bi, lam, wb)


SG_WA, SG_BA, SG_BR, SG_BI, SG_LAM, SG_WB, SG_ROWS = 0, 4, 5, 6, 7, 8, 16


def _mixer_bwd(u, saved, dy, wa, wr_blk, br, wi_blk, bi, lam, wb, *, name):
    tp, din = u.shape
    dl = din // 6
    tt = MIX_ROWS
    cw = GATE_BLOCK
    nch = dl // cw
    tb = tt * MIX_SUBTILES
    assert tp % tb == 0
    nt = tp // tb
    hb = tb // SUBLANES
    tn_dims = (((0,), (0,)), ((), ()))
    nt_dims = (((1,), (1,)), ((), ()))

    def body(u_ref, uh_ref, s_ref, sh_ref, dy_ref, wa_ref, wr_ref, br_ref, wi_ref, bi_ref, lam_ref, wb_ref,
             du_ref, sg_ref, dwr_ref, dwi_ref,
             g_carry, a_head, dca_head, dcb_head, r_s, i_s, a_s, an_s, d_s, g_s):
        i = pl.program_id(0)
        first_tile = i == nt - 1

        @pl.when(i == 0)
        def _():
            for ref in (g_carry, a_head, dca_head, dcb_head, sg_ref, dwr_ref, dwi_ref):
                ref[...] = jnp.zeros_like(ref)

        def halo_of(x):
            return jnp.where(first_tile, 0.0, x)

        for sub in reversed(range(MIX_SUBTILES)):
            rows = slice(sub * tt, (sub + 1) * tt)

            def before(ref, halo_ref, cols):
                if sub == 0:
                    return halo_of(halo_ref[:, cols])
                return ref[sub * tt - SUBLANES:sub * tt, cols]

            for ch in range(nch):
                cs = slice(ch * cw, (ch + 1) * cw)
                cav = s_ref[rows, cs]
                sp = _softplus(-lam_ref[:, cs])
                r, ig, a, _ = _gates(cav, wr_ref[ch], wi_ref[ch], br_ref[:, cs], bi_ref[:, cs], sp)
                r_s[:, cs] = r
                i_s[:, cs] = ig
                a_s[:, cs] = a
                _to_lane_blocks(an_s, cs, _shift_up(a, a_head[:, cs], 1))
                a_head[:, cs] = a[:SUBLANES]
                ga = u_ref[rows, dl + ch * cw: dl + (ch + 1) * cw]
                _to_lane_blocks(d_s, cs, dy_ref[rows, cs] * (ga * _sigmoid(ga)))

            for j in range(dl // LANES):
                lanes = slice(j * LANES, (j + 1) * LANES)
                g_carry[:, lanes] = _scan_tile(an_s, d_s, g_s, g_carry[:, lanes], j, reverse=True)

            for ch in range(nch):
                cs = slice(ch * cw, (ch + 1) * cw)

                def acc_row(r0, val):
                    sg_ref[r0:r0 + 1, cs] += jnp.sum(val, axis=0, keepdims=True)

                def seg(s):
                    return slice(s * dl + ch * cw, s * dl + (ch + 1) * cw)

                cav = s_ref[rows, cs]
                r = r_s[:, cs]
                ig = i_s[:, cs]
                a = a_s[:, cs]
                g = _from_lane_blocks(g_s, cs)
                hsv = s_ref[rows, dl + ch * cw: dl + (ch + 1) * cw]
                lamv = lam_ref[:, cs]
                sp = _softplus(-lamv)
                la = -LRU_C * r * sp
                e2 = a * a
                one_m_e2 = -jnp.tanh(la) * (e2 + 1.0)
                mult = jnp.sqrt(one_m_e2)
                hprev = _shift_down(before(s_ref, sh_ref, slice(dl + ch * cw, dl + (ch + 1) * cw)), hsv, 1)
                icav = ig * cav
                dla = g * (hprev * a - icav * (e2 * lax.rsqrt(one_m_e2)))
                gm = g * mult
                dzi = gm * icav * (1.0 - ig)
                dca = gm * ig
                dla_r = dla * r
                dzr = dla_r * (1.0 - r) * (-LRU_C * sp)
                sg_ref[SG_LAM:SG_LAM + 1, cs] += jnp.sum(dla_r, axis=0, keepdims=True) * (LRU_C * _sigmoid(-lamv))
                acc_row(SG_BR, dzr)
                acc_row(SG_BI, dzi)
                dzr_b = dzr.astype(BF16)
                dzi_b = dzi.astype(BF16)
                cab = cav.astype(BF16)
                dca = dca + lax.dot_general(dzr_b, wr_ref[ch], nt_dims, preferred_element_type=F32)
                dca = dca + lax.dot_general(dzi_b, wi_ref[ch], nt_dims, preferred_element_type=F32)
                dwr_ref[ch] += lax.dot_general(cab, dzr_b, tn_dims, preferred_element_type=F32)
                dwi_ref[ch] += lax.dot_general(cab, dzi_b, tn_dims, preferred_element_type=F32)
                acc_row(SG_BA, dca)
                xa = u_ref[rows, seg(0)]
                head = dca_head[:, cs]
                dxa = wa_ref[3:4, cs] * dca
                acc_row(SG_WA + 3, dca * xa)
                for kk in range(3):
                    later = _shift_up(dca, head, 3 - kk)
                    acc_row(SG_WA + kk, later * xa)
                    dxa = dxa + wa_ref[kk:kk + 1, cs] * later
                dca_head[:, cs] = dca[:SUBLANES]
                ga = u_ref[rows, seg(1)]
                sga = _sigmoid(ga)
                dga = dy_ref[rows, cs] * hsv * (sga + (ga * sga) * (1.0 - sga))
                du_ref[rows, seg(0)] = dxa.astype(BF16)
                du_ref[rows, seg(1)] = dga.astype(BF16)

                bv = u_ref[rows, seg(2)]
                cv = u_ref[rows, seg(3)]
                xb = u_ref[rows, seg(4)]
                gb = u_ref[rows, seg(5)]
                dyb = dy_ref[rows, dl + ch * cw: dl + (ch + 1) * cw]
                v = cv * xb
                vh = before(u_ref, uh_ref, seg(3)) * before(u_ref, uh_ref, seg(4))
                v1 = _shift_down(vh, v, 1)
                v2 = _shift_down(vh, v, 2)
                cb = wb_ref[2:3, cs] * v + wb_ref[1:2, cs] * v1 + wb_ref[0:1, cs] * v2
                sgb = _sigmoid(gb)
                sl = gb * sgb
                dyb_b = dyb * bv
                dyb_cb = dyb * cb
                dcb = dyb_b * sl
                du_ref[rows, seg(2)] = (dyb_cb * sl).astype(BF16)
                du_ref[rows, seg(5)] = (dyb_cb * bv * (sgb + sl * (1.0 - sgb))).astype(BF16)
                bhead = dcb_head[:, cs]
                dv = wb_ref[2:3, cs] * dcb
                acc_row(SG_WB + 2, dcb * v)
                for kk in range(2):
                    later = _shift_up(dcb, bhead, 2 - kk)
                    acc_row(SG_WB + kk, later * v)
                    dv = dv + wb_ref[kk:kk + 1, cs] * later
                dcb_head[:, cs] = dcb[:SUBLANES]
                du_ref[rows, seg(3)] = (dv * xb).astype(BF16)
                du_ref[rows, seg(4)] = (dv * cv).astype(BF16)

    rev = lambda w: pl.BlockSpec((tb, w), lambda i: (nt - 1 - i, 0))
    halo = lambda w: pl.BlockSpec((SUBLANES, w), lambda i: (jnp.maximum((nt - 1 - i) * hb - 1, 0), 0))
    full = lambda shp: pl.BlockSpec(shp, lambda i: tuple(0 for _ in shp))
    vm = lambda r: pltpu.VMEM((r, dl), F32)
    return pl.pallas_call(
        body, name=name, grid=(nt,),
        in_specs=[rev(din), halo(din), rev(2 * dl), halo(2 * dl), rev(2 * dl), full((4, dl)),
                  full((nch, cw, cw)), full((1, dl)), full((nch, cw, cw)), full((1, dl)), full((1, dl)), full((3, dl))],
        out_specs=[rev(din), full((SG_ROWS, dl)), full((nch, cw, cw)), full((nch, cw, cw))],
        out_shape=[jax.ShapeDtypeStruct((tp, din), BF16), jax.ShapeDtypeStruct((SG_ROWS, dl), F32),
                   jax.ShapeDtypeStruct((nch, cw, cw), F32), jax.ShapeDtypeStruct((nch, cw, cw), F32)],
        scratch_shapes=[vm(SUBLANES), vm(SUBLANES), vm(SUBLANES), vm(SUBLANES), vm(tt), vm(tt), vm(tt)]
        + [pltpu.VMEM((dl // LANES, tt, LANES), F32)] * 3,
        compiler_params=_params(("arbitrary",)),
    )(u, u, saved, saved, dy, wa, wr_blk, br, wi_blk, bi, lam, wb)


def _adamw(w, g, m, v, *, name, landed=None, layer=None, depth=None, into=None, row_off=0):
    r, c = w.shape[-2:]
    rows = g.shape[0]
    tr = _tile(rows, 512, 2 * SUBLANES)
    assert row_off % tr == 0
    boff = row_off // tr
    bc1 = 1.0 - ADAM_B1 ** ADAM_STEP
    bc2 = 1.0 - ADAM_B2 ** ADAM_STEP
    slots = landed is not None

    def body(*refs):
        if into is not None:
            refs = refs[:-8] + refs[-4:]
        if slots:
            w_ref, g_ref, l_ref, m_ref, v_ref, grad_ref, delta_ref, nm_ref, nv_ref = refs
            gv = g_ref[...].astype(F32)
            for s in range(N_DEV - 1):
                gv = gv + l_ref[s].astype(F32)
        else:
            w_ref, g_ref, m_ref, v_ref, grad_ref, delta_ref, nm_ref, nv_ref = refs
            gv = g_ref[...]
        wv = w_ref[...]
        mn = ADAM_B1 * m_ref[...] + (1.0 - ADAM_B1) * gv
        vn = ADAM_B2 * v_ref[...] + (1.0 - ADAM_B2) * (gv * gv)
        m_hat = mn / bc1
        v_hat = vn / bc2
        grad_ref[...] = gv
        delta_ref[...] = -ADAM_LR * (m_hat / (jnp.sqrt(v_hat) + ADAM_EPS) + ADAM_WD * wv)
        nm_ref[...] = mn
        nv_ref[...] = vn

    if depth is None:
        blk = pl.BlockSpec((tr, c), lambda i: (i + boff, 0))
    else:
        blk = pl.BlockSpec((None, tr, c), lambda i: (layer, i + boff, 0))
    g_blk = pl.BlockSpec((tr, c), lambda i: (i, 0))
    l_spec = [pl.BlockSpec((N_DEV - 1, tr, c), lambda i: (0, i, 0))] if slots else []
    args = (w, g, landed, m, v) if slots else (w, g, m, v)
    in_specs = [blk, g_blk] + l_spec + [blk, blk]
    if depth is None:
        shp = jax.ShapeDtypeStruct((r, c), F32)
        out_blk = blk
    else:
        shp = jax.ShapeDtypeStruct((depth, r, c), F32)
        out_blk = pl.BlockSpec((None, tr, c), lambda i: (layer, i + boff, 0))
    aliases = {}
    if into is not None:
        aliases = {len(args) + j: j for j in range(4)}
        in_specs = in_specs + [ANY] * 4
        args = args + tuple(into)
    return pl.pallas_call(
        body, name=name, grid=(rows // tr,),
        in_specs=in_specs, out_specs=[out_blk] * 4,
        out_shape=[shp] * 4, input_output_aliases=aliases,
        compiler_params=_params(("parallel",)),
    )(*args)


def _slot_sum(g, *, name):
    _, r, c = g.shape
    tr = _tile(r, 512, SUBLANES)

    def body(g_ref, o_ref):
        gv = g_ref[0].astype(F32)
        for s in range(1, N_DEV):
            gv = gv + g_ref[s].astype(F32)
        o_ref[...] = gv

    return pl.pallas_call(
        body, name=name, grid=(r // tr,),
        in_specs=[pl.BlockSpec((N_DEV, tr, c), lambda i: (0, i, 0))],
        out_specs=pl.BlockSpec((tr, c), lambda i: (i, 0)),
        out_shape=jax.ShapeDtypeStruct((r, c), F32),
        compiler_params=_params(("parallel",)),
    )(g)


def _mesh_pos():
    x, y, c = lax.axis_index("x"), lax.axis_index("y"), lax.axis_index("c")
    return x, y, c, 4 * x + 2 * y + c


ANY = pl.BlockSpec(memory_space=pl.ANY)


GATHER_COPIES = 9


def _all_gather(srcs, out_shapes, views, *, name):
    n = len(srcs)
    SIB, X_OWN, Y_OWN, X_DIAG, Y_DIAG, SIB_X, SIB_Y, SIB_DIAG_TOP, SIB_DIAG_BOTTOM = range(GATHER_COPIES)

    def body(*refs):
        src = refs[:n]
        dst = refs[n:2 * n]
        send_sems, recv_sems, local_sems = refs[2 * n:]
        x, y, c, me = _mesh_pos()
        sibling, x_nbr, y_nbr = (x, y, 1 - c), (1 - x, y, c), (x, 1 - y, c)

        def block(a, px, py, pc, half=None):
            win = views[a](dst[a], 4 * px + 2 * py + pc)
            if half is None:
                return win
            rows = win.shape[0] // 2
            return win.at[pl.ds(half * rows, rows)]

        def copy(a, k, win, to, from_src=False):
            return pltpu.make_async_remote_copy(
                src_ref=src[a] if from_src else win, dst_ref=win,
                send_sem=send_sems.at[a * GATHER_COPIES + k], recv_sem=recv_sems.at[a * GATHER_COPIES + k],
                device_id=to, device_id_type=MESH)

        mine = [pltpu.make_async_copy(src[a], block(a, x, y, c), local_sems.at[a]) for a in range(n)]
        started = []

        def start(cp):
            cp.start()
            started.append(cp)

        for a in range(n):
            mine[a].start()
            own = block(a, x, y, c)
            start(copy(a, SIB, own, sibling, True))
            start(copy(a, X_OWN, own, x_nbr, True))
            start(copy(a, Y_OWN, own, y_nbr, True))
        for a in range(n):
            from_y = block(a, x, 1 - y, c)
            copy(a, Y_OWN, from_y, y_nbr).wait_recv()
            start(copy(a, X_DIAG, block(a, x, 1 - y, c, 0), x_nbr))
            start(copy(a, SIB_Y, from_y, sibling))
            from_x = block(a, 1 - x, y, c)
            copy(a, X_OWN, from_x, x_nbr).wait_recv()
            start(copy(a, Y_DIAG, block(a, 1 - x, y, c, 1), y_nbr))
            start(copy(a, SIB_X, from_x, sibling))
        for a in range(n):
            top = block(a, 1 - x, 1 - y, c, 0)
            copy(a, X_DIAG, top, x_nbr).wait_recv()
            start(copy(a, SIB_DIAG_TOP, top, sibling))
            bottom = block(a, 1 - x, 1 - y, c, 1)
            copy(a, Y_DIAG, bottom, y_nbr).wait_recv()
            start(copy(a, SIB_DIAG_BOTTOM, bottom, sibling))
        for a in range(n):
            copy(a, SIB, block(a, x, y, 1 - c), sibling).wait_recv()
            copy(a, SIB_X, block(a, 1 - x, y, 1 - c), sibling).wait_recv()
            copy(a, SIB_Y, block(a, x, 1 - y, 1 - c), sibling).wait_recv()
            copy(a, SIB_DIAG_TOP, block(a, 1 - x, 1 - y, 1 - c, 0), sibling).wait_recv()
            copy(a, SIB_DIAG_BOTTOM, block(a, 1 - x, 1 - y, 1 - c, 1), sibling).wait_recv()
        for cp in started:
            cp.wait_send()
        for cp in mine:
            cp.wait()

    return pl.pallas_call(
        body, name=name,
        in_specs=[ANY] * n, out_specs=[ANY] * n,
        out_shape=[jax.ShapeDtypeStruct(s, x.dtype) for s, x in zip(out_shapes, srcs)],
        scratch_shapes=[pltpu.SemaphoreType.DMA((GATHER_COPIES * n,)), pltpu.SemaphoreType.DMA((GATHER_COPIES * n,)),
                        pltpu.SemaphoreType.DMA((n,))],
    )(*srcs)


HBM = pl.BlockSpec(memory_space=pltpu.HBM)
SEM = pl.BlockSpec(memory_space=pltpu.SEMAPHORE)
EFFECT = pltpu.SideEffectType.DATAFLOW_SIDE_EFFECTING


def _peer_of(x, y, c, k):
    return (1 - x if k & 4 else x, 1 - y if k & 2 else y, 1 - c if k & 1 else c)


def _peer_copies(n, wins, src, land, send_sems, recv_sems):
    x, y, c, me = _mesh_pos()
    out = []
    for a in range(n):
        for k in range(1, N_DEV):
            px, py, pc = _peer_of(x, y, c, k)
            plan = wins[a](src[a], land[a], me, 4 * px + 2 * py + pc, k)
            if plan is None:
                continue
            target = _peer_of(x, y, c, plan[2]) if len(plan) == 3 else (px, py, pc)
            out.append(pltpu.make_async_remote_copy(
                src_ref=plan[0], dst_ref=plan[1],
                send_sem=send_sems.at[a * 7 + k - 1], recv_sem=recv_sems.at[a * 7 + k - 1],
                device_id=target, device_id_type=MESH))
    return out


def _push_start(srcs, lands, wins, *, name):
    n = len(srcs)

    def body(*refs):
        src = refs[:n]
        land = refs[n:2 * n]
        send_sems, recv_sems = refs[2 * n], refs[2 * n + 1]
        token = refs[-1]
        for cp in _peer_copies(n, wins, src, land, send_sems, recv_sems):
            cp.start()
        token[...] = jnp.zeros_like(token)

    bufs = (*srcs, *lands)
    return pl.pallas_call(
        body, name=name,
        out_shape=(pltpu.SemaphoreType.DMA((7 * n,)), pltpu.SemaphoreType.DMA((7 * n,)),
                   *[pltpu.HBM(v.shape, v.dtype) for v in bufs], jax.ShapeDtypeStruct((SUBLANES, LANES), F32)),
        in_specs=[HBM] * (2 * n),
        out_specs=(SEM, SEM, *[HBM] * (2 * n), pl.BlockSpec(memory_space=pltpu.VMEM)),
        input_output_aliases={i: 2 + i for i in range(2 * n)},
        compiler_params=pltpu.CompilerParams(has_side_effects=EFFECT),
    )(*[pltpu.with_memory_space_constraint(v, pltpu.HBM) for v in bufs])


def _push_wait(handle, wins, after, *, name):
    send_sems, recv_sems, *bufs, _ = handle
    n = len(bufs) // 2

    def body(*refs):
        src = refs[:n]
        land = refs[n:2 * n]
        for cp in _peer_copies(n, wins, src, land, refs[2 * n], refs[2 * n + 1]):
            cp.wait_send()
            cp.wait_recv()

    outs = pl.pallas_call(
        body, name=name,
        out_shape=tuple(pltpu.HBM(v.shape, v.dtype) for v in bufs),
        in_specs=[HBM] * (2 * n) + [SEM, SEM, ANY],
        out_specs=tuple([HBM] * (2 * n)),
        input_output_aliases={i: i for i in range(2 * n)},
        compiler_params=pltpu.CompilerParams(has_side_effects=EFFECT),
    )(*bufs, send_sems, recv_sems, after)
    return outs[:n], outs[n:]


def _gather_lead(src, land, me, peer, k):
    return src, land.at[me]


SAME_CORE_PEERS = (2, 4, 6)
SIBLING = 1


def _gather_cols(width, ks=range(1, N_DEV)):
    def win(src, land, me, peer, k):
        return (src, land.at[:, pl.ds(me * width, width)]) if k in ks else None
    return win


def _forward_cols(width):
    def win(src, land, me, peer, k):
        block = land.at[:, pl.ds(peer * width, width)]
        return (block, block, SIBLING) if k in SAME_CORE_PEERS else None
    return win


def _scatter_lead(src, land, me, peer, k):
    return src.at[peer], land.at[k - 1]


def _scatter_cols(width):
    def win(src, land, me, peer, k):
        return src.at[:, pl.ds(peer * width, width)], land.at[k - 1]
    return win


def _place_block(own, *, cols, name):
    rows, width = own.shape
    tr = _tile(rows, 512, 2 * SUBLANES)
    _, _, _, me = _mesh_pos()

    def body(me_ref, x_ref, o_ref):
        o_ref[...] = x_ref[...]

    if cols:
        out_spec = pl.BlockSpec((tr, width), lambda i, me_ref: (i, me_ref[0]))
        shape = (rows, N_DEV * width)
    else:
        out_spec = pl.BlockSpec((None, tr, width), lambda i, me_ref: (me_ref[0], i, 0))
        shape = (N_DEV, rows, width)
    return pl.pallas_call(
        body, name=name,
        grid_spec=pltpu.PrefetchScalarGridSpec(
            num_scalar_prefetch=1, grid=(rows // tr,),
            in_specs=[pl.BlockSpec((tr, width), lambda i, me_ref: (i, 0))], out_specs=out_spec),
        out_shape=jax.ShapeDtypeStruct(shape, own.dtype),
        compiler_params=_params(("arbitrary",)),
    )(me.astype(jnp.int32).reshape(1), own)


def _dep(x, token):
    return x + token[0, 0].astype(x.dtype)


def _lead(ref, d):
    return ref.at[d]


def _col_window(width):
    def view(ref, d):
        return ref.at[:, pl.ds(d * width, width)]
    return view


def _pack(arrs):
    flat = jnp.concatenate([a.reshape(-1).astype(F32) for a in arrs])
    n = flat.shape[0]
    rows = -(-n // (2 * SUBLANES * LANES)) * 2 * SUBLANES
    return jnp.pad(flat, (0, rows * LANES - n)).reshape(rows, LANES)


def _unpack(buf, shapes):
    flat = buf.reshape(-1)
    out, off = [], 0
    for s in shapes:
        n = 1
        for q in s:
            n *= q
        out.append(flat[off:off + n].reshape(s))
        off += n
    return out


def _blockdiag(w, cw):
    h, hd, _ = w.shape
    per = cw // hd
    wg = w.reshape(h // per, per, hd, hd)
    eye = jnp.eye(per, dtype=w.dtype)
    blk = jnp.einsum("gpij,pq->gpiqj", wg, eye)
    return blk.reshape(h // per, cw, cw).astype(BF16)


def _blockdiag_extract(g, hd):
    n, cw, _ = g.shape
    per = cw // hd
    g5 = g.reshape(n, per, hd, per, hd)
    idx = jnp.arange(per)
    return g5[:, idx, :, idx, :].transpose(1, 0, 2, 3).reshape(n * per, hd, hd)


def kernel(x, meta, norm_g, w_in, conv_a_w, conv_a_b, lru_wr, lru_br, lru_wi, lru_bi, lru_lambda, conv_b_w, w_out, final_g, loss_target, m_meta, m_norm_g, m_w_in, m_conv_a_w, m_conv_a_b, m_lru_wr, m_lru_br, m_lru_wi, m_lru_bi, m_lru_lambda, m_conv_b_w, m_w_out, m_final_g, v_meta, v_norm_g, v_w_in, v_conv_a_w, v_conv_a_b, v_lru_wr, v_lru_br, v_lru_wi, v_lru_bi, v_lru_lambda, v_conv_b_w, v_w_out, v_final_g):
    _, seq, d = x.shape
    n_meta = meta.shape[0]
    depth = w_in.shape[0]
    din = w_in.shape[2] * N_DEV
    dl = din // 6
    dmix = 2 * dl
    wcol = w_in.shape[2]
    wrow = w_out.shape[1]
    mcol = meta.shape[1]
    ccol = conv_a_w.shape[2]
    hd = lru_wr.shape[2]
    n_tok = n_meta + seq
    tp = -(-n_tok // TOKEN_TILE) * TOKEN_TILE
    me = 4 * lax.axis_index("x") + 2 * lax.axis_index("y") + lax.axis_index("c")

    bf = lambda a: a.astype(BF16)
    small_mine = _pack([meta, conv_a_w, conv_b_w])
    first = _all_gather([bf(w_in[0]), small_mine], [(d, din), (N_DEV,) + small_mine.shape],
                        [_col_window(wcol), _lead], name="gather_first")
    flat = first[1].reshape(N_DEV, -1)
    sizes = [meta.size, conv_a_w.size, conv_b_w.size]
    meta_full = jnp.moveaxis(flat[:, :sizes[0]].reshape(N_DEV, n_meta, mcol), 0, 1).reshape(n_meta, d)
    wa_full = jnp.moveaxis(flat[:, sizes[0]:sizes[0] + sizes[1]].reshape(N_DEV, depth, 4, ccol), 0, 2) \
        .reshape(depth, 4, dl)
    wb_full = jnp.moveaxis(flat[:, sizes[0] + sizes[1]:sum(sizes)].reshape(N_DEV, depth, 3, ccol), 0, 2) \
        .reshape(depth, 3, dl)
    w_in_full = [None] * depth
    w_out_full = [None] * depth

    push_out = [None] * depth
    push_in = [None] * depth
    w_in_full[0], src = lax.optimization_barrier((first[0], bf(w_out[0])))
    push_out[0] = _push_start([src], [_place_block(src, cols=False, name="place_wout_0")], [_gather_lead],
                              name="gather_wout_0_start")
    token = push_out[0][-1]
    first_hop = [_gather_cols(wcol, ks=(SIBLING,) + SAME_CORE_PEERS)]
    second_hop = [_forward_cols(wcol)]
    for l in range(1, depth):
        src = bf(_dep(w_in[l], token))
        push_in[l] = _push_start([src], [_place_block(src, cols=True, name=f"place_win_{l}")], first_hop,
                                 name=f"gather_win_{l}_start")
        src = bf(_dep(w_out[l], push_in[l][-1]))
        push_out[l] = _push_start([src], [_place_block(src, cols=False, name=f"place_wout_{l}")], [_gather_lead],
                                  name=f"gather_wout_{l}_start")
        token = push_out[l][-1]

    wr_blk = [_blockdiag(lru_wr[l], GATE_BLOCK) for l in range(depth)]
    wi_blk = [_blockdiag(lru_wi[l], GATE_BLOCK) for l in range(depth)]
    vec = lambda a: a.reshape(1, dl)

    tm = _tile(tp, 1408)
    saved = []
    for l in range(depth):
        if l == 0:
            h, hn = _rms_fwd_first(x[0], meta_full, _dep(norm_g[l], token), tp=tp, name=f"rms_fwd_{l}")
        else:
            hn = _rms_fwd(h, _dep(norm_g[l], push_in[l][-1]), name=f"rms_fwd_{l}")
        if l > 0:
            _, landed = _push_wait(push_in[l], second_hop, hn, name=f"forward_win_{l}_wait")
            w_in_full[l] = landed[0]
        u = _matmul(hn, w_in_full[l], tm=tm, tn=_tile(din, 1536), tk=d, name=f"mm_u_{l}")
        mixed, y = _mixer_fwd(u, wa_full[l], vec(conv_a_b[l]), wr_blk[l], vec(lru_br[l]), wi_blk[l], vec(lru_bi[l]),
                              vec(lru_lambda[l]), wb_full[l], name=f"mixer_fwd_{l}")
        _, landed = _push_wait(push_out[l], [_gather_lead], y, name=f"gather_wout_{l}_wait")
        w_out_full[l] = landed[0].reshape(dmix, d)
        h_next = _matmul(y, w_out_full[l], tm=tm, tn=_tile(d, 512), tk=dmix, add=h, name=f"mm_out_{l}")
        if l + 1 < depth:
            src, landed = _push_wait(push_in[l + 1], first_hop, h_next, name=f"gather_win_{l + 1}_wait")
            push_in[l + 1] = _push_start(src, landed, second_hop, name=f"forward_win_{l + 1}_start")
        saved.append((h, hn, u, mixed, y))
        h = h_next

    dh, dhb, dg_final, loss_part = _loss_head(h, loss_target[0], final_g, n_meta=n_meta, n_tok=n_tok,
                                              name="loss_head")

    small_grads = [None] * depth
    sent_out = [None] * depth
    sent_in = [None] * depth
    scatter_in = [_scatter_cols(wcol)]
    token = None
    dg_norms = []
    for l in reversed(range(depth)):
        h_in, hn, u, mixed, y = saved[l]
        dy = _matmul(dhb, w_out_full[l], tb=True, tm=tm, tn=_tile(dmix, 1024), tk=d, dep=token, name=f"mm_dy_{l}")
        dw_out = _matmul(y, dhb, ta=True, tm=_tile(dmix, 1024), tn=_tile(d, 1024), tk=tp, out_dtype=BF16,
                         name=f"mm_dwout_{l}")
        sent_out[l] = _push_start([dw_out.reshape(N_DEV, wrow, d)], [lax.empty((N_DEV - 1, wrow, d), BF16)],
                                  [_scatter_lead], name=f"scatter_wout_{l}_start")
        du, sg, dwr, dwi = _mixer_bwd(u, mixed, dy, wa_full[l], wr_blk[l], vec(lru_br[l]), wi_blk[l], vec(lru_bi[l]),
                                      vec(lru_lambda[l]), _dep(wb_full[l], sent_out[l][-1]), name=f"mixer_bwd_{l}")
        small_grads[l] = (sg, dwr, dwi)
        if l == 0:
            rows = jnp.stack([small_grads[j][0] for j in range(depth)])
            early = [_pack([
                rows[:, SG_BA], rows[:, SG_BR], rows[:, SG_BI], rows[:, SG_LAM], rows[:, SG_WA:SG_WA + 4],
                rows[:, SG_WB:SG_WB + 3], dg_final[0], *dg_norms]),
                _pack([jnp.stack([_blockdiag_extract(small_grads[j][1], hd) for j in range(depth)]),
                       jnp.stack([_blockdiag_extract(small_grads[j][2], hd) for j in range(depth)])]).astype(BF16)]
            early_land = [lax.dynamic_update_slice(lax.empty((N_DEV,) + a.shape, a.dtype), a[None], (me, 0, 0))
                          for a in early]
            sent_early = _push_start(early, early_land, [_gather_lead] * 2, name="gather_early_grads_start")
        parts = 2 if l == 0 else 1
        token = sent_early[-1] if l == 0 else None
        sent_in[l] = []
        for p in range(parts):
            dw_in = _matmul(hn, du, ta=True, tm=_tile(d // parts, 512), tn=_tile(din, 1536), tk=tp, out_dtype=BF16,
                            dep=token, m_part=(p, parts), name=f"mm_dwin_{l}_{p}")
            sent_in[l].append(_push_start([dw_in], [lax.empty((N_DEV - 1, d // parts, wcol), BF16)], scatter_in,
                                          name=f"scatter_win_{l}_{p}_start"))
            token = sent_in[l][-1][-1]
        dhn = _matmul(du, w_in_full[l], tb=True, tm=_tile(tp, 528, 2 * SUBLANES), tn=_tile(d, 1024), tk=din, dep=token,
                      name=f"mm_dhn_{l}")
        if l > 0:
            dh, dhb, dg_norm = _rms_bwd(h_in, dhn, dh, norm_g[l], name=f"rms_bwd_{l}")
            dg_norms.append(dg_norm[0])
        else:
            grad_x, d_meta, dg_norm = _rms_bwd_first(h_in, dhn, dh, norm_g[l], n_meta=n_meta, seq=seq,
                                                     name=f"rms_bwd_{l}")

    big = {"win": None, "wout": None}

    def big_adamw(l, after):
        src, landed = _push_wait(sent_out[l], [_scatter_lead], after, name=f"scatter_wout_{l}_wait")
        own = lax.dynamic_index_in_dim(src[0], me, 0, keepdims=False)
        big["wout"] = _adamw(w_out, own, m_w_out, v_w_out, landed=landed[0], layer=l, depth=depth,
                             into=big["wout"], name=f"adamw_w_out_{l}")
        after = big["wout"][0]
        for p, sent in enumerate(sent_in[l]):
            src, landed = _push_wait(sent, scatter_in, after, name=f"scatter_win_{l}_{p}_wait")
            own = lax.dynamic_slice_in_dim(src[0], me * wcol, wcol, axis=1)
            big["win"] = _adamw(w_in, own, m_w_in, v_w_in, landed=landed[0], layer=l, depth=depth,
                                into=big["win"], row_off=p * own.shape[0], name=f"adamw_w_in_{l}_{p}")
            after = big["win"][0]
        return after

    after = dg_norm
    for l in reversed(range(1, depth)):
        after = big_adamw(l, after)

    late = _pack([dg_norm[0], d_meta, loss_part[0:1, 0:1]])
    if depth > 1:
        late, after = lax.optimization_barrier((late, after))
    late_all = _all_gather([late], [(N_DEV,) + late.shape], [_lead], name="gather_late_grads")[0]
    late_sum = _unpack(_slot_sum(late_all, name="sum_late_grads"), [(d,), (n_meta, d), ()])
    loss = late_sum[2]
    _, early_all = _push_wait(sent_early, [_gather_lead] * 2, late_sum[0], name="gather_early_grads_wait")
    vec_shapes = [conv_a_b.shape, lru_br.shape, lru_bi.shape, lru_lambda.shape, (depth, 4, dl), (depth, 3, dl),
                  final_g.shape] + [(d,)] * (depth - 1)
    e = _unpack(_slot_sum(early_all[0], name="sum_early_vectors"), vec_shapes)
    g_wr, g_wi = _unpack(_slot_sum(early_all[1], name="sum_early_maps"), [lru_wr.shape, lru_wi.shape])
    g_norm = jnp.stack([late_sum[0]] + e[7:][::-1])
    g_meta = lax.dynamic_slice_in_dim(late_sum[1], me * mcol, mcol, axis=1)
    g_wa = lax.dynamic_slice_in_dim(e[4], me * ccol, ccol, axis=2)
    g_wb = lax.dynamic_slice_in_dim(e[5], me * ccol, ccol, axis=2)

    small_w = [norm_g, conv_a_b, lru_wr, lru_br, lru_wi, lru_bi, lru_lambda, final_g, meta, conv_a_w, conv_b_w]
    small_m = [m_norm_g, m_conv_a_b, m_lru_wr, m_lru_br, m_lru_wi, m_lru_bi, m_lru_lambda, m_final_g, m_meta,
               m_conv_a_w, m_conv_b_w]
    small_v = [v_norm_g, v_conv_a_b, v_lru_wr, v_lru_br, v_lru_wi, v_lru_bi, v_lru_lambda, v_final_g, v_meta,
               v_conv_a_w, v_conv_b_w]
    small_g = [g_norm, e[0], g_wr, e[1], g_wi, e[2], e[3], e[6], g_meta, g_wa, g_wb]
    small_out = _adamw(_pack(small_w), _pack(small_g), _pack(small_m), _pack(small_v), name="adamw_small")
    small_shapes = [a.shape for a in small_w]
    s_grad, s_delta, s_m, s_v = [_unpack(o, small_shapes) for o in small_out]

    big_adamw(0, small_out[0])
    win_out, wout_out = big["win"], big["wout"]

    names = ["norm_g", "conv_a_b", "lru_wr", "lru_br", "lru_wi", "lru_bi", "lru_lambda", "final_g", "meta",
             "conv_a_w", "conv_b_w"]
    order = ["meta", "norm_g", "w_in", "conv_a_w", "conv_a_b", "lru_wr", "lru_br", "lru_wi", "lru_bi", "lru_lambda",
             "conv_b_w", "w_out", "final_g"]

    def family(idx, small):
        table = {nm: small[i] for i, nm in enumerate(names)}
        table["w_in"] = win_out[idx]
        table["w_out"] = wout_out[idx]
        return [table[nm] for nm in order]

    return (loss, grad_x, *family(0, s_grad), *family(1, s_delta), *family(2, s_m), *family(3, s_v))
```

```python
import jax
import jax.numpy as jnp
from jax import lax
from jax.experimental import pallas as pl
from jax.experimental.pallas import tpu as pltpu

F32 = jnp.float32
BF16 = jnp.bfloat16
MESH = pl.DeviceIdType.MESH

N_DEV = 8
RMS_EPS = 1e-6
LRU_C = 8.0
ADAM_LR = 0.001
ADAM_B1 = 0.9
ADAM_B2 = 0.999
ADAM_EPS = 1e-08
ADAM_WD = 0.01
ADAM_STEP = 10

V7X_VMEM_LIMIT = 52 * 1024 * 1024
LANES = 128
SUBLANES = 8
TOKEN_TILE = 384
MIX_ROWS = 128
MIX_SUBTILES = 3
GATE_BLOCK = 128


def _params(sem):
    return pltpu.CompilerParams(dimension_semantics=sem, vmem_limit_bytes=V7X_VMEM_LIMIT)


def _tile(n, target, align=LANES):
    best = None
    for t in range(align, min(n, target) + 1, align):
        if n % t == 0:
            best = t
    return n if best is None else best


def _sigmoid(z):
    return 0.5 * jnp.tanh(0.5 * z) + 0.5


def _softplus(z):
    e = jnp.exp(-jnp.abs(z))
    u = 1.0 + e
    l1p = jnp.where(u == 1.0, e, jnp.log(u) * e / jnp.where(u == 1.0, 1.0, u - 1.0))
    return jnp.maximum(z, 0.0) + l1p


def _matmul(a, b, *, ta=False, tb=False, tm, tn, tk, out_dtype=F32, add=None, dep=None, m_part=None, name):
    m, k = (a.shape[1], a.shape[0]) if ta else a.shape
    m_off = 0
    if m_part is not None:
        assert add is None and m % (m_part[1] * tm) == 0
        m //= m_part[1]
        m_off = m_part[0] * (m // tm)
    n, kb = b.shape if tb else b.shape[::-1]
    assert kb == k
    assert m % tm == 0 and n % tn == 0 and k % tk == 0, (m, n, k, tm, tn, tk)
    nk = k // tk
    a_spec = pl.BlockSpec((tk, tm), lambda i, j, q: (q, i + m_off)) if ta \
        else pl.BlockSpec((tm, tk), lambda i, j, q: (i + m_off, q))
    b_spec = pl.BlockSpec((tn, tk), lambda i, j, q: (j, q)) if tb else pl.BlockSpec((tk, tn), lambda i, j, q: (q, j))
    o_spec = pl.BlockSpec((tm, tn), lambda i, j, q: (i, j))
    o_shape = (m, n)
    dims = (((0 if ta else 1,), (1 if tb else 0,)), ((), ()))
    has_add = add is not None
    has_dep = dep is not None

    def body(*refs):
        if has_dep:
            refs = refs[:-3] + refs[-2:]
        if has_add:
            a_ref, b_ref, add_ref, o_ref, acc_ref = refs
        else:
            a_ref, b_ref, o_ref, acc_ref = refs
        q = pl.program_id(2)
        part = lax.dot_general(a_ref[...], b_ref[...], dims, preferred_element_type=F32)

        def finish(acc):
            if has_add:
                acc = acc + add_ref[...]
            o_ref[...] = acc.astype(out_dtype)

        if nk == 1:
            finish(part)
        else:
            @pl.when(q == 0)
            def _():
                acc_ref[...] = part

            @pl.when(jnp.logical_and(q > 0, q < nk - 1))
            def _():
                acc_ref[...] += part

            @pl.when(q == nk - 1)
            def _():
                finish(acc_ref[...] + part)

    in_specs = [a_spec, b_spec] + ([o_spec] if has_add else [])
    args = (a, b) + ((add,) if has_add else ())
    if has_dep:
        in_specs.append(pl.BlockSpec((SUBLANES, LANES), lambda i, j, q: (0, 0)))
        args += (dep,)
    acc_shape = (tm, tn) if nk > 1 else (SUBLANES, LANES)
    return pl.pallas_call(
        body, name=name,
        grid=(m // tm, n // tn, nk),
        in_specs=in_specs, out_specs=o_spec,
        out_shape=jax.ShapeDtypeStruct(o_shape, out_dtype),
        scratch_shapes=[pltpu.VMEM(acc_shape, F32)],
        compiler_params=_params(("parallel", "parallel", "arbitrary")),
    )(*args)


def _rms_fwd(h, g, *, name):
    tp, d = h.shape
    tr = _tile(tp, 1408, 2 * SUBLANES)

    def body(h_ref, g_ref, o_ref):
        hv = h_ref[...]
        rstd = lax.rsqrt(jnp.mean(hv * hv, axis=-1, keepdims=True) + RMS_EPS)
        o_ref[...] = (hv * rstd * g_ref[...]).astype(BF16)

    return pl.pallas_call(
        body, name=name, grid=(tp // tr,),
        in_specs=[pl.BlockSpec((tr, d), lambda i: (i, 0)), pl.BlockSpec((1, d), lambda i: (0, 0))],
        out_specs=pl.BlockSpec((tr, d), lambda i: (i, 0)),
        out_shape=jax.ShapeDtypeStruct((tp, d), BF16),
        compiler_params=_params(("parallel",)),
    )(h, g.reshape(1, d))


def _rms_fwd_first(x, meta, g, *, tp, name):
    seq, d = x.shape
    n_meta = meta.shape[0]
    n_tok = n_meta + seq
    tr = TOKEN_TILE
    assert tp % tr == 0 and tr % n_meta == 0
    per = tr // n_meta

    def body(x_ref, xp_ref, m_ref, g_ref, h_ref, o_ref):
        i = pl.program_id(0)
        head = jnp.where(i == 0, m_ref[...], xp_ref[...])
        rows = i * tr + lax.broadcasted_iota(jnp.int32, (tr, 1), 0)
        hv = jnp.where(rows < n_tok, jnp.concatenate([head, x_ref[:tr - n_meta, :]], axis=0), 0.0)
        h_ref[...] = hv
        rstd = lax.rsqrt(jnp.mean(hv * hv, axis=-1, keepdims=True) + RMS_EPS)
        o_ref[...] = (hv * rstd * g_ref[...]).astype(BF16)

    row = pl.BlockSpec((tr, d), lambda i: (i, 0))
    own = pl.BlockSpec((tr, d), lambda i: (jnp.minimum(i, -(-seq // tr) - 1), 0))
    before = pl.BlockSpec((n_meta, d), lambda i: (jnp.maximum(i * per - 1, 0), 0))
    return pl.pallas_call(
        body, name=name, grid=(tp // tr,),
        in_specs=[own, before, pl.BlockSpec((n_meta, d), lambda i: (0, 0)), pl.BlockSpec((1, d), lambda i: (0, 0))],
        out_specs=[row, row],
        out_shape=[jax.ShapeDtypeStruct((tp, d), F32), jax.ShapeDtypeStruct((tp, d), BF16)],
        compiler_params=_params(("parallel",)),
    )(x, x, meta, g.reshape(1, d))


def _rms_bwd(h, dhn, dout, g, *, name):
    tp, d = h.shape
    tr = _tile(tp, 528, 2 * SUBLANES)

    def body(h_ref, dhn_ref, dout_ref, g_ref, dh_ref, dhb_ref, dg_ref):
        hv = h_ref[...]
        rstd = lax.rsqrt(jnp.mean(hv * hv, axis=-1, keepdims=True) + RMS_EPS)
        xhat = hv * rstd
        dn = dhn_ref[...]
        dxhat = dn * g_ref[...]
        dh = dout_ref[...] + rstd * (dxhat - xhat * jnp.mean(dxhat * xhat, axis=-1, keepdims=True))
        dh_ref[...] = dh
        dhb_ref[...] = dh.astype(BF16)
        part = jnp.sum(dn * xhat, axis=0, keepdims=True)

        @pl.when(pl.program_id(0) == 0)
        def _():
            dg_ref[...] = part

        @pl.when(pl.program_id(0) > 0)
        def _():
            dg_ref[...] += part

    row = pl.BlockSpec((tr, d), lambda i: (i, 0))
    vec = pl.BlockSpec((1, d), lambda i: (0, 0))
    return pl.pallas_call(
        body, name=name, grid=(tp // tr,),
        in_specs=[row, row, row, vec],
        out_specs=[row, row, vec],
        out_shape=[jax.ShapeDtypeStruct((tp, d), F32), jax.ShapeDtypeStruct((tp, d), BF16),
                   jax.ShapeDtypeStruct((1, d), F32)],
        compiler_params=_params(("arbitrary",)),
    )(h, dhn, dout, g.reshape(1, d))


def _rms_bwd_first(h, dhn, dout, g, *, n_meta, seq, name):
    tp, d = h.shape
    tr = _tile(seq, 512)
    assert seq % tr == 0 and tr % n_meta == 0 and tp >= seq + n_meta
    nt = seq // tr
    per = tr // n_meta

    def grads(hv, dn, do, gv):
        rstd = lax.rsqrt(jnp.mean(hv * hv, axis=-1, keepdims=True) + RMS_EPS)
        xhat = hv * rstd
        dxhat = dn * gv
        dh = do + rstd * (dxhat - xhat * jnp.mean(dxhat * xhat, axis=-1, keepdims=True))
        return dh, jnp.sum(dn * xhat, axis=0, keepdims=True)

    def body(h_ref, dhn_ref, dout_ref, hn_ref, dhnn_ref, doutn_ref, g_ref, gx_ref, dmeta_ref, dg_ref):
        i = pl.program_id(0)
        gv = g_ref[...]
        dh, part = grads(h_ref[...], dhn_ref[...], dout_ref[...], gv)
        dh_next, part_next = grads(hn_ref[...], dhnn_ref[...], doutn_ref[...], gv)
        gx_ref[...] = jnp.concatenate([dh[n_meta:], dh_next], axis=0)

        @pl.when(i == 0)
        def _():
            dmeta_ref[...] = dh[:n_meta]
            dg_ref[...] = part

        @pl.when(i > 0)
        def _():
            dg_ref[...] += part

        @pl.when(i == nt - 1)
        def _():
            dg_ref[...] += part_next

    row = pl.BlockSpec((tr, d), lambda i: (i, 0))
    nxt = pl.BlockSpec((n_meta, d), lambda i: ((i + 1) * per, 0))
    vec = pl.BlockSpec((1, d), lambda i: (0, 0))
    return pl.pallas_call(
        body, name=name, grid=(nt,),
        in_specs=[row, row, row, nxt, nxt, nxt, vec],
        out_specs=[pl.BlockSpec((None, tr, d), lambda i: (0, i, 0)), pl.BlockSpec((n_meta, d), lambda i: (0, 0)), vec],
        out_shape=[jax.ShapeDtypeStruct((1, seq, d), F32), jax.ShapeDtypeStruct((n_meta, d), F32),
                   jax.ShapeDtypeStruct((1, d), F32)],
        compiler_params=_params(("arbitrary",)),
    )(h, dhn, dout, h, dhn, dout, g.reshape(1, d))


def _loss_head(h, tgt, g, *, n_meta, n_tok, name):
    tp, d = h.shape
    seq = tgt.shape[0]
    tr = TOKEN_TILE
    assert tp % tr == 0 and tr % n_meta == 0
    per = tr // n_meta

    def body(h_ref, t_ref, tp_ref, g_ref, dh_ref, dhb_ref, dg_ref, loss_ref):
        i = pl.program_id(0)
        hv = h_ref[...]
        rstd = lax.rsqrt(jnp.mean(hv * hv, axis=-1, keepdims=True) + RMS_EPS)
        xhat = hv * rstd
        gv = g_ref[...]
        rows = i * tr + lax.broadcasted_iota(jnp.int32, (tr, 1), 0)
        valid = jnp.logical_and(rows >= n_meta, rows < n_tok)
        target = jnp.concatenate([tp_ref[...], t_ref[:tr - n_meta, :]], axis=0)
        err = jnp.where(valid, xhat * gv - target, 0.0)
        dy = err * (1.0 / d)
        dxhat = dy * gv
        dh = rstd * (dxhat - xhat * jnp.mean(dxhat * xhat, axis=-1, keepdims=True))
        dh_ref[...] = dh
        dhb_ref[...] = dh.astype(BF16)
        dg_part = jnp.sum(dy * xhat, axis=0, keepdims=True)
        per_row = jnp.sum(err * err, axis=-1, keepdims=True) * (1.0 / d)
        loss_part = jnp.broadcast_to(0.5 * jnp.sum(per_row, axis=0, keepdims=True), (SUBLANES, LANES))

        @pl.when(i == 0)
        def _():
            dg_ref[...] = dg_part
            loss_ref[...] = loss_part

        @pl.when(i > 0)
        def _():
            dg_ref[...] += dg_part
            loss_ref[...] += loss_part

    row = pl.BlockSpec((tr, d), lambda i: (i, 0))
    vec = pl.BlockSpec((1, d), lambda i: (0, 0))
    own = pl.BlockSpec((tr, d), lambda i: (jnp.minimum(i, -(-seq // tr) - 1), 0))
    before = pl.BlockSpec((n_meta, d), lambda i: (jnp.maximum(i * per - 1, 0), 0))
    return pl.pallas_call(
        body, name=name, grid=(tp // tr,),
        in_specs=[row, own, before, vec],
        out_specs=[row, row, vec, pl.BlockSpec((SUBLANES, LANES), lambda i: (0, 0))],
        out_shape=[jax.ShapeDtypeStruct((tp, d), F32), jax.ShapeDtypeStruct((tp, d), BF16),
                   jax.ShapeDtypeStruct((1, d), F32), jax.ShapeDtypeStruct((SUBLANES, LANES), F32)],
        compiler_params=_params(("arbitrary",)),
    )(h, tgt, tgt, g.reshape(1, d))


def _shift_down(halo, tile, s):
    if s == 0:
        return tile
    ext = jnp.concatenate([halo, tile], axis=0)
    return pltpu.roll(ext, s, 0)[SUBLANES:]


def _shift_up(tile, head, s):
    if s == 0:
        return tile
    ext = jnp.concatenate([tile, head], axis=0)
    n = ext.shape[0]
    return pltpu.roll(ext, n - s, 0)[: tile.shape[0]]


def _to_lane_blocks(ref, cols, val):
    for j in range(cols.start // LANES, cols.stop // LANES):
        ref[j] = val[:, j * LANES - cols.start:(j + 1) * LANES - cols.start]


def _from_lane_blocks(ref, cols):
    return jnp.concatenate([ref[j] for j in range(cols.start // LANES, cols.stop // LANES)], axis=1)


def _scan_tile(a_ref, b_ref, out_ref, carry, j, *, reverse):
    ng = a_ref.shape[1] // SUBLANES
    order = list(range(SUBLANES))[::-1] if reverse else list(range(SUBLANES))

    def rows(r):
        return pl.ds(r, ng, stride=SUBLANES)

    prod, loc = {}, {}
    prev = None
    for r in order:
        ar = a_ref[j, rows(r), :]
        br = b_ref[j, rows(r), :]
        prod[r] = ar if prev is None else ar * prod[prev]
        loc[r] = br if prev is None else ar * loc[prev] + br
        prev = r
    pg, lg = prod[prev], loc[prev]
    ones = jnp.ones((SUBLANES,) + pg.shape[1:], F32)
    zeros = jnp.zeros_like(ones)
    s = 1
    while s < ng:
        p_sh = _shift_up(pg, ones, s) if reverse else _shift_down(ones, pg, s)
        l_sh = _shift_up(lg, zeros, s) if reverse else _shift_down(zeros, lg, s)
        lg = pg * l_sh + lg
        pg = pg * p_sh
        s *= 2
    leaving = pg * carry[0:1, :] + lg
    entering = _shift_up(leaving, carry, 1) if reverse else _shift_down(carry, leaving, 1)
    for r in order:
        out_ref[j, rows(r), :] = loc[r] + prod[r] * entering
    last = leaving[0:1, :] if reverse else leaving[ng - 1:ng, :]
    return jnp.broadcast_to(last, carry.shape)


def _gates(ca, wr, wi, br, bi, sp):
    cab = ca.astype(BF16)
    r = _sigmoid(jnp.dot(cab, wr, preferred_element_type=F32) + br)
    ig = _sigmoid(jnp.dot(cab, wi, preferred_element_type=F32) + bi)
    la = -LRU_C * r * sp
    a = jnp.exp(la)
    mult = jnp.sqrt(-jnp.tanh(la) * (a * a + 1.0))
    return r, ig, a, mult


def _mixer_fwd(u, wa, ba, wr_blk, br, wi_blk, bi, lam, wb, *, name):
    tp, din = u.shape
    dl = din // 6
    tt = MIX_ROWS
    cw = GATE_BLOCK
    nch = dl // cw
    assert tp % tt == 0 and dl % cw == 0

    def body(u_ref, wa_ref, ba_ref, wr_ref, br_ref, wi_ref, bi_ref, lam_ref, wb_ref,
             s_ref, y_ref, xa_tail, v_tail, h_carry, a_s, b_s, h_s):
        @pl.when(pl.program_id(0) == 0)
        def _():
            xa_tail[...] = jnp.zeros_like(xa_tail)
            v_tail[...] = jnp.zeros_like(v_tail)
            h_carry[...] = jnp.zeros_like(h_carry)

        for sub in range(MIX_SUBTILES):
            rows = slice(sub * tt, (sub + 1) * tt)
            for ch in range(nch):
                cs = slice(ch * cw, (ch + 1) * cw)

                def seg(s):
                    return slice(s * dl + ch * cw, s * dl + (ch + 1) * cw)

                xa = u_ref[rows, seg(0)]
                halo = xa_tail[:, cs]
                ca = ba_ref[:, cs] + wa_ref[3:4, cs] * xa
                for kk in range(3):
                    ca = ca + wa_ref[kk:kk + 1, cs] * _shift_down(halo, xa, 3 - kk)
                xa_tail[:, cs] = xa[tt - SUBLANES:]
                s_ref[rows, cs] = ca
                sp = _softplus(-lam_ref[:, cs])
                _, ig, a, mult = _gates(ca, wr_ref[ch], wi_ref[ch], br_ref[:, cs], bi_ref[:, cs], sp)
                _to_lane_blocks(a_s, cs, a)
                _to_lane_blocks(b_s, cs, mult * (ig * ca))

                bv = u_ref[rows, seg(2)]
                v = u_ref[rows, seg(3)] * u_ref[rows, seg(4)]
                gb = u_ref[rows, seg(5)]
                vh = v_tail[:, cs]
                cb = wb_ref[2:3, cs] * v
                for kk in range(2):
                    cb = cb + wb_ref[kk:kk + 1, cs] * _shift_down(vh, v, 2 - kk)
                v_tail[:, cs] = v[tt - SUBLANES:]
                y_ref[rows, dl + ch * cw: dl + (ch + 1) * cw] = (bv * cb * (gb * _sigmoid(gb))).astype(BF16)

            for ch in range(nch):
                cs = slice(ch * cw, (ch + 1) * cw)
                for j in range(cs.start // LANES, cs.stop // LANES):
                    lanes = slice(j * LANES, (j + 1) * LANES)
                    h_carry[:, lanes] = _scan_tile(a_s, b_s, h_s, h_carry[:, lanes], j, reverse=False)
                hsv = _from_lane_blocks(h_s, cs)
                s_ref[rows, dl + ch * cw: dl + (ch + 1) * cw] = hsv
                ga = u_ref[rows, dl + ch * cw: dl + (ch + 1) * cw]
                y_ref[rows, cs] = (hsv * (ga * _sigmoid(ga))).astype(BF16)

    tb = tt * MIX_SUBTILES
    assert tp % tb == 0
    row = lambda w: pl.BlockSpec((tb, w), lambda i: (i, 0))
    full = lambda shp: pl.BlockSpec(shp, lambda i: tuple(0 for _ in shp))
    return pl.pallas_call(
        body, name=name, grid=(tp // tb,),
        in_specs=[row(din), full((4, dl)), full((1, dl)), full((nch, cw, cw)), full((1, dl)),
                  full((nch, cw, cw)), full((1, dl)), full((1, dl)), full((3, dl))],
        out_specs=[row(2 * dl), row(2 * dl)],
        out_shape=[jax.ShapeDtypeStruct((tp, 2 * dl), F32), jax.ShapeDtypeStruct((tp, 2 * dl), BF16)],
        scratch_shapes=[pltpu.VMEM((SUBLANES, dl), F32), pltpu.VMEM((SUBLANES, dl), F32),
                        pltpu.VMEM((SUBLANES, dl), F32)] + [pltpu.VMEM((dl // LANES, tt, LANES), F32)] * 3,
        compiler_params=_params(("arbitrary",)),
    )(u, wa, ba, wr_blk, br, wi_blk, bi, lam, wb)


SG_WA, SG_BA, SG_BR, SG_BI, SG_LAM, SG_WB, SG_ROWS = 0, 4, 5, 6, 7, 8, 16


def _mixer_bwd(u, saved, dy, wa, wr_blk, br, wi_blk, bi, lam, wb, *, name):
    tp, din = u.shape
    dl = din // 6
    tt = MIX_ROWS
    cw = GATE_BLOCK
    nch = dl // cw
    tb = tt * MIX_SUBTILES
    assert tp % tb == 0
    nt = tp // tb
    hb = tb // SUBLANES
    tn_dims = (((0,), (0,)), ((), ()))
    nt_dims = (((1,), (1,)), ((), ()))

    def body(u_ref, uh_ref, s_ref, sh_ref, dy_ref, wa_ref, wr_ref, br_ref, wi_ref, bi_ref, lam_ref, wb_ref,
             du_ref, sg_ref, dwr_ref, dwi_ref,
             g_carry, a_head, dca_head, dcb_head, r_s, i_s, a_s, an_s, d_s, g_s):
        i = pl.program_id(0)
        first_tile = i == nt - 1

        @pl.when(i == 0)
        def _():
            for ref in (g_carry, a_head, dca_head, dcb_head, sg_ref, dwr_ref, dwi_ref):
                ref[...] = jnp.zeros_like(ref)

        def halo_of(x):
            return jnp.where(first_tile, 0.0, x)

        for sub in reversed(range(MIX_SUBTILES)):
            rows = slice(sub * tt, (sub + 1) * tt)

            def before(ref, halo_ref, cols):
                if sub == 0:
                    return halo_of(halo_ref[:, cols])
                return ref[sub * tt - SUBLANES:sub * tt, cols]

            for ch in range(nch):
                cs = slice(ch * cw, (ch + 1) * cw)
                cav = s_ref[rows, cs]
                sp = _softplus(-lam_ref[:, cs])
                r, ig, a, _ = _gates(cav, wr_ref[ch], wi_ref[ch], br_ref[:, cs], bi_ref[:, cs], sp)
                r_s[:, cs] = r
                i_s[:, cs] = ig
                a_s[:, cs] = a
                _to_lane_blocks(an_s, cs, _shift_up(a, a_head[:, cs], 1))
                a_head[:, cs] = a[:SUBLANES]
                ga = u_ref[rows, dl + ch * cw: dl + (ch + 1) * cw]
                _to_lane_blocks(d_s, cs, dy_ref[rows, cs] * (ga * _sigmoid(ga)))

            for j in range(dl // LANES):
                lanes = slice(j * LANES, (j + 1) * LANES)
                g_carry[:, lanes] = _scan_tile(an_s, d_s, g_s, g_carry[:, lanes], j, reverse=True)

            for ch in range(nch):
                cs = slice(ch * cw, (ch + 1) * cw)

                def acc_row(r0, val):
                    sg_ref[r0:r0 + 1, cs] += jnp.sum(val, axis=0, keepdims=True)

                def seg(s):
                    return slice(s * dl + ch * cw, s * dl + (ch + 1) * cw)

                cav = s_ref[rows, cs]
                r = r_s[:, cs]
                ig = i_s[:, cs]
                a = a_s[:, cs]
                g = _from_lane_blocks(g_s, cs)
                hsv = s_ref[rows, dl + ch * cw: dl + (ch + 1) * cw]
                lamv = lam_ref[:, cs]
                sp = _softplus(-lamv)
                la = -LRU_C * r * sp
                e2 = a * a
                one_m_e2 = -jnp.tanh(la) * (e2 + 1.0)
                mult = jnp.sqrt(one_m_e2)
                hprev = _shift_down(before(s_ref, sh_ref, slice(dl + ch * cw, dl + (ch + 1) * cw)), hsv, 1)
                icav = ig * cav
                dla = g * (hprev * a - icav * (e2 * lax.rsqrt(one_m_e2)))
                gm = g * mult
                dzi = gm * icav * (1.0 - ig)
                dca = gm * ig
                dla_r = dla * r
                dzr = dla_r * (1.0 - r) * (-LRU_C * sp)
                sg_ref[SG_LAM:SG_LAM + 1, cs] += jnp.sum(dla_r, axis=0, keepdims=True) * (LRU_C * _sigmoid(-lamv))
                acc_row(SG_BR, dzr)
                acc_row(SG_BI, dzi)
                dzr_b = dzr.astype(BF16)
                dzi_b = dzi.astype(BF16)
                cab = cav.astype(BF16)
                dca = dca + lax.dot_general(dzr_b, wr_ref[ch], nt_dims, preferred_element_type=F32)
                dca = dca + lax.dot_general(dzi_b, wi_ref[ch], nt_dims, preferred_element_type=F32)
                dwr_ref[ch] += lax.dot_general(cab, dzr_b, tn_dims, preferred_element_type=F32)
                dwi_ref[ch] += lax.dot_general(cab, dzi_b, tn_dims, preferred_element_type=F32)
                acc_row(SG_BA, dca)
                xa = u_ref[rows, seg(0)]
                head = dca_head[:, cs]
                dxa = wa_ref[3:4, cs] * dca
                acc_row(SG_WA + 3, dca * xa)
                for kk in range(3):
                    later = _shift_up(dca, head, 3 - kk)
                    acc_row(SG_WA + kk, later * xa)
                    dxa = dxa + wa_ref[kk:kk + 1, cs] * later
                dca_head[:, cs] = dca[:SUBLANES]
                ga = u_ref[rows, seg(1)]
                sga = _sigmoid(ga)
                dga = dy_ref[rows, cs] * hsv * (sga + (ga * sga) * (1.0 - sga))
                du_ref[rows, seg(0)] = dxa.astype(BF16)
                du_ref[rows, seg(1)] = dga.astype(BF16)

                bv = u_ref[rows, seg(2)]
                cv = u_ref[rows, seg(3)]
                xb = u_ref[rows, seg(4)]
                gb = u_ref[rows, seg(5)]
                dyb = dy_ref[rows, dl + ch * cw: dl + (ch + 1) * cw]
                v = cv * xb
                vh = before(u_ref, uh_ref, seg(3)) * before(u_ref, uh_ref, seg(4))
                v1 = _shift_down(vh, v, 1)
                v2 = _shift_down(vh, v, 2)
                cb = wb_ref[2:3, cs] * v + wb_ref[1:2, cs] * v1 + wb_ref[0:1, cs] * v2
                sgb = _sigmoid(gb)
                sl = gb * sgb
                dyb_b = dyb * bv
                dyb_cb = dyb * cb
                dcb = dyb_b * sl
                du_ref[rows, seg(2)] = (dyb_cb * sl).astype(BF16)
                du_ref[rows, seg(5)] = (dyb_cb * bv * (sgb + sl * (1.0 - sgb))).astype(BF16)
                bhead = dcb_head[:, cs]
                dv = wb_ref[2:3, cs] * dcb
                acc_row(SG_WB + 2, dcb * v)
                for kk in range(2):
                    later = _shift_up(dcb, bhead, 2 - kk)
                    acc_row(SG_WB + kk, later * v)
                    dv = dv + wb_ref[kk:kk + 1, cs] * later
                dcb_head[:, cs] = dcb[:SUBLANES]
                du_ref[rows, seg(3)] = (dv * xb).astype(BF16)
                du_ref[rows, seg(4)] = (dv * cv).astype(BF16)

    rev = lambda w: pl.BlockSpec((tb, w), lambda i: (nt - 1 - i, 0))
    halo = lambda w: pl.BlockSpec((SUBLANES, w), lambda i: (jnp.maximum((nt - 1 - i) * hb - 1, 0), 0))
    full = lambda shp: pl.BlockSpec(shp, lambda i: tuple(0 for _ in shp))
    vm = lambda r: pltpu.VMEM((r, dl), F32)
    return pl.pallas_call(
        body, name=name, grid=(nt,),
        in_specs=[rev(din), halo(din), rev(2 * dl), halo(2 * dl), rev(2 * dl), full((4, dl)),
                  full((nch, cw, cw)), full((1, dl)), full((nch, cw, cw)), full((1, dl)), full((1, dl)), full((3, dl))],
        out_specs=[rev(din), full((SG_ROWS, dl)), full((nch, cw, cw)), full((nch, cw, cw))],
        out_shape=[jax.ShapeDtypeStruct((tp, din), BF16), jax.ShapeDtypeStruct((SG_ROWS, dl), F32),
                   jax.ShapeDtypeStruct((nch, cw, cw), F32), jax.ShapeDtypeStruct((nch, cw, cw), F32)],
        scratch_shapes=[vm(SUBLANES), vm(SUBLANES), vm(SUBLANES), vm(SUBLANES), vm(tt), vm(tt), vm(tt)]
        + [pltpu.VMEM((dl // LANES, tt, LANES), F32)] * 3,
        compiler_params=_params(("arbitrary",)),
    )(u, u, saved, saved, dy, wa, wr_blk, br, wi_blk, bi, lam, wb)


def _adamw(w, g, m, v, *, name, landed=None, layer=None, depth=None, into=None, row_off=0, own=None):
    r, c = w.shape[-2:]
    rows = g.shape[1] if own == "lead" else g.shape[0]
    tr = _tile(rows, 512, 2 * SUBLANES)
    assert row_off % tr == 0
    boff = row_off // tr
    bc1 = 1.0 - ADAM_B1 ** ADAM_STEP
    bc2 = 1.0 - ADAM_B2 ** ADAM_STEP
    slots = landed is not None

    def body(*refs):
        if own is not None:
            refs = refs[1:]
        if into is not None:
            refs = refs[:-8] + refs[-4:]
        if slots:
            w_ref, g_ref, l_ref, m_ref, v_ref, grad_ref, delta_ref, nm_ref, nv_ref = refs
            gv = g_ref[...].astype(F32)
            for s in range(N_DEV - 1):
                gv = gv + l_ref[s].astype(F32)
        else:
            w_ref, g_ref, m_ref, v_ref, grad_ref, delta_ref, nm_ref, nv_ref = refs
            gv = g_ref[...]
        wv = w_ref[...]
        mn = ADAM_B1 * m_ref[...] + (1.0 - ADAM_B1) * gv
        vn = ADAM_B2 * v_ref[...] + (1.0 - ADAM_B2) * (gv * gv)
        m_hat = mn / bc1
        v_hat = vn / bc2
        grad_ref[...] = gv
        delta_ref[...] = -ADAM_LR * (m_hat / (jnp.sqrt(v_hat) + ADAM_EPS) + ADAM_WD * wv)
        nm_ref[...] = mn
        nv_ref[...] = vn

    if depth is None:
        blk = pl.BlockSpec((tr, c), lambda i, *_: (i + boff, 0))
    else:
        blk = pl.BlockSpec((None, tr, c), lambda i, *_: (layer, i + boff, 0))
    if own == "cols":
        g_blk = pl.BlockSpec((tr, c), lambda i, me_ref: (i, me_ref[0]))
    elif own == "lead":
        g_blk = pl.BlockSpec((None, tr, c), lambda i, me_ref: (me_ref[0], i, 0))
    else:
        g_blk = pl.BlockSpec((tr, c), lambda i, *_: (i, 0))
    l_spec = [pl.BlockSpec((N_DEV - 1, tr, c), lambda i, *_: (0, i, 0))] if slots else []
    args = (w, g, landed, m, v) if slots else (w, g, m, v)
    in_specs = [blk, g_blk] + l_spec + [blk, blk]
    if depth is None:
        shp = jax.ShapeDtypeStruct((r, c), F32)
        out_blk = blk
    else:
        shp = jax.ShapeDtypeStruct((depth, r, c), F32)
        out_blk = pl.BlockSpec((None, tr, c), lambda i, *_: (layer, i + boff, 0))
    prefetch = () if own is None else (_mesh_pos()[3].astype(jnp.int32).reshape(1),)
    aliases = {}
    if into is not None:
        aliases = {len(prefetch) + len(args) + j: j for j in range(4)}
        in_specs = in_specs + [ANY] * 4
        args = args + tuple(into)
    return pl.pallas_call(
        body, name=name,
        grid_spec=pltpu.PrefetchScalarGridSpec(
            num_scalar_prefetch=len(prefetch), grid=(rows // tr,), in_specs=in_specs, out_specs=[out_blk] * 4),
        out_shape=[shp] * 4, input_output_aliases=aliases,
        compiler_params=_params(("parallel",)),
    )(*prefetch, *args)


def _slot_sum(g, *, name):
    _, r, c = g.shape
    tr = _tile(r, 512, SUBLANES)

    def body(g_ref, o_ref):
        gv = g_ref[0].astype(F32)
        for s in range(1, N_DEV):
            gv = gv + g_ref[s].astype(F32)
        o_ref[...] = gv

    return pl.pallas_call(
        body, name=name, grid=(r // tr,),
        in_specs=[pl.BlockSpec((N_DEV, tr, c), lambda i: (0, i, 0))],
        out_specs=pl.BlockSpec((tr, c), lambda i: (i, 0)),
        out_shape=jax.ShapeDtypeStruct((r, c), F32),
        compiler_params=_params(("parallel",)),
    )(g)


def _mesh_pos():
    x, y, c = lax.axis_index("x"), lax.axis_index("y"), lax.axis_index("c")
    return x, y, c, 4 * x + 2 * y + c


ANY = pl.BlockSpec(memory_space=pl.ANY)


GATHER_COPIES = 9


def _all_gather(srcs, out_shapes, views, *, name):
    n = len(srcs)
    SIB, X_OWN, Y_OWN, X_DIAG, Y_DIAG, SIB_X, SIB_Y, SIB_DIAG_TOP, SIB_DIAG_BOTTOM = range(GATHER_COPIES)

    def body(*refs):
        src = refs[:n]
        dst = refs[n:2 * n]
        send_sems, recv_sems, local_sems = refs[2 * n:]
        x, y, c, me = _mesh_pos()
        sibling, x_nbr, y_nbr = (x, y, 1 - c), (1 - x, y, c), (x, 1 - y, c)

        def block(a, px, py, pc, half=None):
            win = views[a](dst[a], 4 * px + 2 * py + pc)
            if half is None:
                return win
            rows = win.shape[0] // 2
            return win.at[pl.ds(half * rows, rows)]

        def copy(a, k, win, to, from_src=False):
            return pltpu.make_async_remote_copy(
                src_ref=src[a] if from_src else win, dst_ref=win,
                send_sem=send_sems.at[a * GATHER_COPIES + k], recv_sem=recv_sems.at[a * GATHER_COPIES + k],
                device_id=to, device_id_type=MESH)

        mine = [pltpu.make_async_copy(src[a], block(a, x, y, c), local_sems.at[a]) for a in range(n)]
        started = []

        def start(cp):
            cp.start()
            started.append(cp)

        for a in range(n):
            mine[a].start()
            own = block(a, x, y, c)
            start(copy(a, SIB, own, sibling, True))
            start(copy(a, X_OWN, own, x_nbr, True))
            start(copy(a, Y_OWN, own, y_nbr, True))
        for a in range(n):
            from_y = block(a, x, 1 - y, c)
            copy(a, Y_OWN, from_y, y_nbr).wait_recv()
            start(copy(a, X_DIAG, block(a, x, 1 - y, c, 0), x_nbr))
            start(copy(a, SIB_Y, from_y, sibling))
            from_x = block(a, 1 - x, y, c)
            copy(a, X_OWN, from_x, x_nbr).wait_recv()
            start(copy(a, Y_DIAG, block(a, 1 - x, y, c, 1), y_nbr))
            start(copy(a, SIB_X, from_x, sibling))
        for a in range(n):
            top = block(a, 1 - x, 1 - y, c, 0)
            copy(a, X_DIAG, top, x_nbr).wait_recv()
            start(copy(a, SIB_DIAG_TOP, top, sibling))
            bottom = block(a, 1 - x, 1 - y, c, 1)
            copy(a, Y_DIAG, bottom, y_nbr).wait_recv()
            start(copy(a, SIB_DIAG_BOTTOM, bottom, sibling))
        for a in range(n):
            copy(a, SIB, block(a, x, y, 1 - c), sibling).wait_recv()
            copy(a, SIB_X, block(a, 1 - x, y, 1 - c), sibling).wait_recv()
            copy(a, SIB_Y, block(a, x, 1 - y, 1 - c), sibling).wait_recv()
            copy(a, SIB_DIAG_TOP, block(a, 1 - x, 1 - y, 1 - c, 0), sibling).wait_recv()
            copy(a, SIB_DIAG_BOTTOM, block(a, 1 - x, 1 - y, 1 - c, 1), sibling).wait_recv()
        for cp in started:
            cp.wait_send()
        for cp in mine:
            cp.wait()

    return pl.pallas_call(
        body, name=name,
        in_specs=[ANY] * n, out_specs=[ANY] * n,
        out_shape=[jax.ShapeDtypeStruct(s, x.dtype) for s, x in zip(out_shapes, srcs)],
        scratch_shapes=[pltpu.SemaphoreType.DMA((GATHER_COPIES * n,)), pltpu.SemaphoreType.DMA((GATHER_COPIES * n,)),
                        pltpu.SemaphoreType.DMA((n,))],
    )(*srcs)


HBM = pl.BlockSpec(memory_space=pltpu.HBM)
SEM = pl.BlockSpec(memory_space=pltpu.SEMAPHORE)
EFFECT = pltpu.SideEffectType.DATAFLOW_SIDE_EFFECTING


def _peer_of(x, y, c, k):
    return (1 - x if k & 4 else x, 1 - y if k & 2 else y, 1 - c if k & 1 else c)


def _peer_copies(n, wins, src, land, send_sems, recv_sems):
    x, y, c, me = _mesh_pos()
    out = []
    for a in range(n):
        for k in range(1, N_DEV):
            px, py, pc = _peer_of(x, y, c, k)
            plan = wins[a](src[a], land[a], me, 4 * px + 2 * py + pc, k)
            if plan is None:
                continue
            target = _peer_of(x, y, c, plan[2]) if len(plan) == 3 else (px, py, pc)
            out.append(pltpu.make_async_remote_copy(
                src_ref=plan[0], dst_ref=plan[1],
                send_sem=send_sems.at[a * 7 + k - 1], recv_sem=recv_sems.at[a * 7 + k - 1],
                device_id=target, device_id_type=MESH))
    return out


def _push_start(srcs, lands, wins, *, name):
    n = len(srcs)

    def body(*refs):
        src = refs[:n]
        land = refs[n:2 * n]
        send_sems, recv_sems = refs[2 * n], refs[2 * n + 1]
        token = refs[-1]
        for cp in _peer_copies(n, wins, src, land, send_sems, recv_sems):
            cp.start()
        token[...] = jnp.zeros_like(token)

    bufs = (*srcs, *lands)
    return pl.pallas_call(
        body, name=name,
        out_shape=(pltpu.SemaphoreType.DMA((7 * n,)), pltpu.SemaphoreType.DMA((7 * n,)),
                   *[pltpu.HBM(v.shape, v.dtype) for v in bufs], jax.ShapeDtypeStruct((SUBLANES, LANES), F32)),
        in_specs=[HBM] * (2 * n),
        out_specs=(SEM, SEM, *[HBM] * (2 * n), pl.BlockSpec(memory_space=pltpu.VMEM)),
        input_output_aliases={i: 2 + i for i in range(2 * n)},
        compiler_params=pltpu.CompilerParams(has_side_effects=EFFECT),
    )(*[pltpu.with_memory_space_constraint(v, pltpu.HBM) for v in bufs])


def _push_wait(handle, wins, after, *, name):
    send_sems, recv_sems, *bufs, _ = handle
    n = len(bufs) // 2

    def body(*refs):
        src = refs[:n]
        land = refs[n:2 * n]
        for cp in _peer_copies(n, wins, src, land, refs[2 * n], refs[2 * n + 1]):
            cp.wait_send()
            cp.wait_recv()

    outs = pl.pallas_call(
        body, name=name,
        out_shape=tuple(pltpu.HBM(v.shape, v.dtype) for v in bufs),
        in_specs=[HBM] * (2 * n) + [SEM, SEM, ANY],
        out_specs=tuple([HBM] * (2 * n)),
        input_output_aliases={i: i for i in range(2 * n)},
        compiler_params=pltpu.CompilerParams(has_side_effects=EFFECT),
    )(*bufs, send_sems, recv_sems, after)
    return outs[:n], outs[n:]


def _gather_lead(src, land, me, peer, k):
    return src, land.at[me]


SAME_CORE_PEERS = (2, 4, 6)
SIBLING = 1


def _gather_cols(width, ks=range(1, N_DEV)):
    def win(src, land, me, peer, k):
        return (src, land.at[:, pl.ds(me * width, width)]) if k in ks else None
    return win


def _forward_cols(width):
    def win(src, land, me, peer, k):
        block = land.at[:, pl.ds(peer * width, width)]
        return (block, block, SIBLING) if k in SAME_CORE_PEERS else None
    return win


def _scatter_lead(src, land, me, peer, k):
    return src.at[peer], land.at[k - 1]


def _scatter_cols(width):
    def win(src, land, me, peer, k):
        return src.at[:, pl.ds(peer * width, width)], land.at[k - 1]
    return win


def _place_block(own, *, cols, name):
    rows, width = own.shape
    tr = _tile(rows, 512, 2 * SUBLANES)
    _, _, _, me = _mesh_pos()

    def body(me_ref, x_ref, o_ref):
        o_ref[...] = x_ref[...]

    if cols:
        out_spec = pl.BlockSpec((tr, width), lambda i, me_ref: (i, me_ref[0]))
        shape = (rows, N_DEV * width)
    else:
        out_spec = pl.BlockSpec((None, tr, width), lambda i, me_ref: (me_ref[0], i, 0))
        shape = (N_DEV, rows, width)
    return pl.pallas_call(
        body, name=name,
        grid_spec=pltpu.PrefetchScalarGridSpec(
            num_scalar_prefetch=1, grid=(rows // tr,),
            in_specs=[pl.BlockSpec((tr, width), lambda i, me_ref: (i, 0))], out_specs=out_spec),
        out_shape=jax.ShapeDtypeStruct(shape, own.dtype),
        compiler_params=_params(("arbitrary",)),
    )(me.astype(jnp.int32).reshape(1), own)


def _dep(x, token):
    return x + token[0, 0].astype(x.dtype)


def _lead(ref, d):
    return ref.at[d]


def _col_window(width):
    def view(ref, d):
        return ref.at[:, pl.ds(d * width, width)]
    return view


def _pack(arrs):
    flat = jnp.concatenate([a.reshape(-1).astype(F32) for a in arrs])
    n = flat.shape[0]
    rows = -(-n // (2 * SUBLANES * LANES)) * 2 * SUBLANES
    return jnp.pad(flat, (0, rows * LANES - n)).reshape(rows, LANES)


def _unpack(buf, shapes):
    flat = buf.reshape(-1)
    out, off = [], 0
    for s in shapes:
        n = 1
        for q in s:
            n *= q
        out.append(flat[off:off + n].reshape(s))
        off += n
    return out


def _blockdiag(w, cw):
    h, hd, _ = w.shape
    per = cw // hd
    wg = w.reshape(h // per, per, hd, hd)
    eye = jnp.eye(per, dtype=w.dtype)
    blk = jnp.einsum("gpij,pq->gpiqj", wg, eye)
    return blk.reshape(h // per, cw, cw).astype(BF16)


def _blockdiag_extract(g, hd):
    n, cw, _ = g.shape
    per = cw // hd
    g5 = g.reshape(n, per, hd, per, hd)
    idx = jnp.arange(per)
    return g5[:, idx, :, idx, :].transpose(1, 0, 2, 3).reshape(n * per, hd, hd)


def kernel(x, meta, norm_g, w_in, conv_a_w, conv_a_b, lru_wr, lru_br, lru_wi, lru_bi, lru_lambda, conv_b_w, w_out, final_g, loss_target, m_meta, m_norm_g, m_w_in, m_conv_a_w, m_conv_a_b, m_lru_wr, m_lru_br, m_lru_wi, m_lru_bi, m_lru_lambda, m_conv_b_w, m_w_out, m_final_g, v_meta, v_norm_g, v_w_in, v_conv_a_w, v_conv_a_b, v_lru_wr, v_lru_br, v_lru_wi, v_lru_bi, v_lru_lambda, v_conv_b_w, v_w_out, v_final_g):
    _, seq, d = x.shape
    n_meta = meta.shape[0]
    depth = w_in.shape[0]
    din = w_in.shape[2] * N_DEV
    dl = din // 6
    dmix = 2 * dl
    wcol = w_in.shape[2]
    wrow = w_out.shape[1]
    mcol = meta.shape[1]
    ccol = conv_a_w.shape[2]
    hd = lru_wr.shape[2]
    n_tok = n_meta + seq
    tp = -(-n_tok // TOKEN_TILE) * TOKEN_TILE
    me = 4 * lax.axis_index("x") + 2 * lax.axis_index("y") + lax.axis_index("c")

    bf = lambda a: a.astype(BF16)
    small_mine = _pack([meta, conv_a_w, conv_b_w])
    first = _all_gather([bf(w_in[0]), small_mine], [(d, din), (N_DEV,) + small_mine.shape],
                        [_col_window(wcol), _lead], name="gather_first")
    flat = first[1].reshape(N_DEV, -1)
    sizes = [meta.size, conv_a_w.size, conv_b_w.size]
    meta_full = jnp.moveaxis(flat[:, :sizes[0]].reshape(N_DEV, n_meta, mcol), 0, 1).reshape(n_meta, d)
    wa_full = jnp.moveaxis(flat[:, sizes[0]:sizes[0] + sizes[1]].reshape(N_DEV, depth, 4, ccol), 0, 2) \
        .reshape(depth, 4, dl)
    wb_full = jnp.moveaxis(flat[:, sizes[0] + sizes[1]:sum(sizes)].reshape(N_DEV, depth, 3, ccol), 0, 2) \
        .reshape(depth, 3, dl)
    w_in_full = [None] * depth
    w_out_full = [None] * depth

    push_out = [None] * depth
    push_in = [None] * depth
    w_in_full[0], src = lax.optimization_barrier((first[0], bf(w_out[0])))
    push_out[0] = _push_start([src], [_place_block(src, cols=False, name="place_wout_0")], [_gather_lead],
                              name="gather_wout_0_start")
    token = push_out[0][-1]
    first_hop = [_gather_cols(wcol, ks=(SIBLING,) + SAME_CORE_PEERS)]
    second_hop = [_forward_cols(wcol)]
    for l in range(1, depth):
        src = bf(_dep(w_in[l], token))
        push_in[l] = _push_start([src], [_place_block(src, cols=True, name=f"place_win_{l}")], first_hop,
                                 name=f"gather_win_{l}_start")
        src = bf(_dep(w_out[l], push_in[l][-1]))
        push_out[l] = _push_start([src], [_place_block(src, cols=False, name=f"place_wout_{l}")], [_gather_lead],
                                  name=f"gather_wout_{l}_start")
        token = push_out[l][-1]

    wr_blk = [_blockdiag(lru_wr[l], GATE_BLOCK) for l in range(depth)]
    wi_blk = [_blockdiag(lru_wi[l], GATE_BLOCK) for l in range(depth)]
    vec = lambda a: a.reshape(1, dl)

    tm = _tile(tp, 1408)
    saved = []
    for l in range(depth):
        if l == 0:
            h, hn = _rms_fwd_first(x[0], meta_full, _dep(norm_g[l], token), tp=tp, name=f"rms_fwd_{l}")
        else:
            hn = _rms_fwd(h, _dep(norm_g[l], push_in[l][-1]), name=f"rms_fwd_{l}")
        if l > 0:
            _, landed = _push_wait(push_in[l], second_hop, hn, name=f"forward_win_{l}_wait")
            w_in_full[l] = landed[0]
        u = _matmul(hn, w_in_full[l], tm=tm, tn=_tile(din, 1536), tk=d, name=f"mm_u_{l}")
        mixed, y = _mixer_fwd(u, wa_full[l], vec(conv_a_b[l]), wr_blk[l], vec(lru_br[l]), wi_blk[l], vec(lru_bi[l]),
                              vec(lru_lambda[l]), wb_full[l], name=f"mixer_fwd_{l}")
        _, landed = _push_wait(push_out[l], [_gather_lead], y, name=f"gather_wout_{l}_wait")
        w_out_full[l] = landed[0].reshape(dmix, d)
        h_next = _matmul(y, w_out_full[l], tm=tm, tn=_tile(d, 512), tk=dmix, add=h, name=f"mm_out_{l}")
        if l + 1 < depth:
            src, landed = _push_wait(push_in[l + 1], first_hop, h_next, name=f"gather_win_{l + 1}_wait")
            push_in[l + 1] = _push_start(src, landed, second_hop, name=f"forward_win_{l + 1}_start")
        saved.append((h, hn, u, mixed, y))
        h = h_next

    dh, dhb, dg_final, loss_part = _loss_head(h, loss_target[0], final_g, n_meta=n_meta, n_tok=n_tok,
                                              name="loss_head")

    small_grads = [None] * depth
    sent_out = [None] * depth
    sent_in = [None] * depth
    scatter_in = [_scatter_cols(wcol)]
    token = None
    dg_norms = []
    for l in reversed(range(depth)):
        h_in, hn, u, mixed, y = saved[l]
        dy = _matmul(dhb, w_out_full[l], tb=True, tm=tm, tn=_tile(dmix, 1024), tk=d, dep=token, name=f"mm_dy_{l}")
        dw_out = _matmul(y, dhb, ta=True, tm=_tile(dmix, 1024), tn=_tile(d, 1024), tk=tp, out_dtype=BF16,
                         name=f"mm_dwout_{l}")
        sent_out[l] = _push_start([dw_out.reshape(N_DEV, wrow, d)], [lax.empty((N_DEV - 1, wrow, d), BF16)],
                                  [_scatter_lead], name=f"scatter_wout_{l}_start")
        du, sg, dwr, dwi = _mixer_bwd(u, mixed, dy, wa_full[l], wr_blk[l], vec(lru_br[l]), wi_blk[l], vec(lru_bi[l]),
                                      vec(lru_lambda[l]), _dep(wb_full[l], sent_out[l][-1]), name=f"mixer_bwd_{l}")
        small_grads[l] = (sg, dwr, dwi)
        if l == 0:
            rows = jnp.stack([small_grads[j][0] for j in range(depth)])
            early = [_pack([
                rows[:, SG_BA], rows[:, SG_BR], rows[:, SG_BI], rows[:, SG_LAM], rows[:, SG_WA:SG_WA + 4],
                rows[:, SG_WB:SG_WB + 3], dg_final[0], *dg_norms]),
                _pack([jnp.stack([_blockdiag_extract(small_grads[j][1], hd) for j in range(depth)]),
                       jnp.stack([_blockdiag_extract(small_grads[j][2], hd) for j in range(depth)])]).astype(BF16)]
            early_land = [lax.dynamic_update_slice(lax.empty((N_DEV,) + a.shape, a.dtype), a[None], (me, 0, 0))
                          for a in early]
            sent_early = _push_start(early, early_land, [_gather_lead] * 2, name="gather_early_grads_start")
        parts = 2 if l == 0 else 1
        token = sent_early[-1] if l == 0 else None
        sent_in[l] = []
        for p in range(parts):
            dw_in = _matmul(hn, du, ta=True, tm=_tile(d // parts, 512), tn=_tile(din, 1536), tk=tp, out_dtype=BF16,
                            dep=token, m_part=(p, parts), name=f"mm_dwin_{l}_{p}")
            sent_in[l].append(_push_start([dw_in], [lax.empty((N_DEV - 1, d // parts, wcol), BF16)], scatter_in,
                                          name=f"scatter_win_{l}_{p}_start"))
            token = sent_in[l][-1][-1]
        dhn = _matmul(du, w_in_full[l], tb=True, tm=_tile(tp, 528, 2 * SUBLANES), tn=_tile(d, 1024), tk=din, dep=token,
                      name=f"mm_dhn_{l}")
        if l > 0:
            dh, dhb, dg_norm = _rms_bwd(h_in, dhn, dh, norm_g[l], name=f"rms_bwd_{l}")
            dg_norms.append(dg_norm[0])
        else:
            grad_x, d_meta, dg_norm = _rms_bwd_first(h_in, dhn, dh, norm_g[l], n_meta=n_meta, seq=seq,
                                                     name=f"rms_bwd_{l}")

    big = {"win": None, "wout": None}

    def big_adamw(l, after):
        src, landed = _push_wait(sent_out[l], [_scatter_lead], after, name=f"scatter_wout_{l}_wait")
        big["wout"] = _adamw(w_out, src[0], m_w_out, v_w_out, landed=landed[0], layer=l, depth=depth,
                             into=big["wout"], own="lead", name=f"adamw_w_out_{l}")
        after = big["wout"][0]
        for p, sent in enumerate(sent_in[l]):
            src, landed = _push_wait(sent, scatter_in, after, name=f"scatter_win_{l}_{p}_wait")
            big["win"] = _adamw(w_in, src[0], m_w_in, v_w_in, landed=landed[0], layer=l, depth=depth,
                                into=big["win"], row_off=p * src[0].shape[0], own="cols",
                                name=f"adamw_w_in_{l}_{p}")
            after = big["win"][0]
        return after

    after = dg_norm
    for l in reversed(range(1, depth)):
        after = big_adamw(l, after)

    late = _pack([dg_norm[0], d_meta, loss_part[0:1, 0:1]])
    if depth > 1:
        late, after = lax.optimization_barrier((late, after))
    late_all = _all_gather([late], [(N_DEV,) + late.shape], [_lead], name="gather_late_grads")[0]
    late_sum = _unpack(_slot_sum(late_all, name="sum_late_grads"), [(d,), (n_meta, d), ()])
    loss = late_sum[2]
    _, early_all = _push_wait(sent_early, [_gather_lead] * 2, late_sum[0], name="gather_early_grads_wait")
    vec_shapes = [conv_a_b.shape, lru_br.shape, lru_bi.shape, lru_lambda.shape, (depth, 4, dl), (depth, 3, dl),
                  final_g.shape] + [(d,)] * (depth - 1)
    e = _unpack(_slot_sum(early_all[0], name="sum_early_vectors"), vec_shapes)
    g_wr, g_wi = _unpack(_slot_sum(early_all[1], name="sum_early_maps"), [lru_wr.shape, lru_wi.shape])
    g_norm = jnp.stack([late_sum[0]] + e[7:][::-1])
    g_meta = lax.dynamic_slice_in_dim(late_sum[1], me * mcol, mcol, axis=1)
    g_wa = lax.dynamic_slice_in_dim(e[4], me * ccol, ccol, axis=2)
    g_wb = lax.dynamic_slice_in_dim(e[5], me * ccol, ccol, axis=2)

    small_w = [norm_g, conv_a_b, lru_wr, lru_br, lru_wi, lru_bi, lru_lambda, final_g, meta, conv_a_w, conv_b_w]
    small_m = [m_norm_g, m_conv_a_b, m_lru_wr, m_lru_br, m_lru_wi, m_lru_bi, m_lru_lambda, m_final_g, m_meta,
               m_conv_a_w, m_conv_b_w]
    small_v = [v_norm_g, v_conv_a_b, v_lru_wr, v_lru_br, v_lru_wi, v_lru_bi, v_lru_lambda, v_final_g, v_meta,
               v_conv_a_w, v_conv_b_w]
    small_g = [g_norm, e[0], g_wr, e[1], g_wi, e[2], e[3], e[6], g_meta, g_wa, g_wb]
    small_out = _adamw(_pack(small_w), _pack(small_g), _pack(small_m), _pack(small_v), name="adamw_small")
    small_shapes = [a.shape for a in small_w]
    s_grad, s_delta, s_m, s_v = [_unpack(o, small_shapes) for o in small_out]

    big_adamw(0, small_out[0])
    win_out, wout_out = big["win"], big["wout"]

    names = ["norm_g", "conv_a_b", "lru_wr", "lru_br", "lru_wi", "lru_bi", "lru_lambda", "final_g", "meta",
             "conv_a_w", "conv_b_w"]
    order = ["meta", "norm_g", "w_in", "conv_a_w", "conv_a_b", "lru_wr", "lru_br", "lru_wi", "lru_bi", "lru_lambda",
             "conv_b_w", "w_out", "final_g"]

    def family(idx, small):
        table = {nm: small[i] for i, nm in enumerate(names)}
        table["w_in"] = win_out[idx]
        table["w_out"] = wout_out[idx]
        return [table[nm] for nm in order]

    return (loss, grad_x, *family(0, s_grad), *family(1, s_delta), *family(2, s_m), *family(3, s_v))
```

```python
import jax
import jax.numpy as jnp
from jax import lax
from jax.experimental import pallas as pl
from jax.experimental.pallas import tpu as pltpu

F32 = jnp.float32
BF16 = jnp.bfloat16
MESH = pl.DeviceIdType.MESH

N_DEV = 8
RMS_EPS = 1e-6
LRU_C = 8.0
ADAM_LR = 0.001
ADAM_B1 = 0.9
ADAM_B2 = 0.999
ADAM_EPS = 1e-08
ADAM_WD = 0.01
ADAM_STEP = 10

V7X_VMEM_LIMIT = 52 * 1024 * 1024
LANES = 128
SUBLANES = 8
TOKEN_TILE = 384
MIX_ROWS = 128
MIX_SUBTILES = 3
GATE_BLOCK = 128


def _params(sem):
    return pltpu.CompilerParams(dimension_semantics=sem, vmem_limit_bytes=V7X_VMEM_LIMIT)


def _tile(n, target, align=LANES):
    best = None
    for t in range(align, min(n, target) + 1, align):
        if n % t == 0:
            best = t
    return n if best is None else best


def _sigmoid(z):
    return 0.5 * jnp.tanh(0.5 * z) + 0.5


def _softplus(z):
    e = jnp.exp(-jnp.abs(z))
    u = 1.0 + e
    l1p = jnp.where(u == 1.0, e, jnp.log(u) * e / jnp.where(u == 1.0, 1.0, u - 1.0))
    return jnp.maximum(z, 0.0) + l1p


def _matmul(a, b, *, ta=False, tb=False, tm, tn, tk, out_dtype=F32, add=None, dep=None, m_part=None, name):
    m, k = (a.shape[1], a.shape[0]) if ta else a.shape
    m_off = 0
    if m_part is not None:
        assert add is None and m % (m_part[1] * tm) == 0
        m //= m_part[1]
        m_off = m_part[0] * (m // tm)
    n, kb = b.shape if tb else b.shape[::-1]
    assert kb == k
    assert m % tm == 0 and n % tn == 0 and k % tk == 0, (m, n, k, tm, tn, tk)
    nk = k // tk
    a_spec = pl.BlockSpec((tk, tm), lambda i, j, q: (q, i + m_off)) if ta \
        else pl.BlockSpec((tm, tk), lambda i, j, q: (i + m_off, q))
    b_spec = pl.BlockSpec((tn, tk), lambda i, j, q: (j, q)) if tb else pl.BlockSpec((tk, tn), lambda i, j, q: (q, j))
    o_spec = pl.BlockSpec((tm, tn), lambda i, j, q: (i, j))
    o_shape = (m, n)
    dims = (((0 if ta else 1,), (1 if tb else 0,)), ((), ()))
    has_add = add is not None
    has_dep = dep is not None

    def body(*refs):
        if has_dep:
            refs = refs[:-3] + refs[-2:]
        if has_add:
            a_ref, b_ref, add_ref, o_ref, acc_ref = refs
        else:
            a_ref, b_ref, o_ref, acc_ref = refs
        q = pl.program_id(2)
        part = lax.dot_general(a_ref[...], b_ref[...], dims, preferred_element_type=F32)

        def finish(acc):
            if has_add:
                acc = acc + add_ref[...]
            o_ref[...] = acc.astype(out_dtype)

        if nk == 1:
            finish(part)
        else:
            @pl.when(q == 0)
            def _():
                acc_ref[...] = part

            @pl.when(jnp.logical_and(q > 0, q < nk - 1))
            def _():
                acc_ref[...] += part

            @pl.when(q == nk - 1)
            def _():
                finish(acc_ref[...] + part)

    in_specs = [a_spec, b_spec] + ([o_spec] if has_add else [])
    args = (a, b) + ((add,) if has_add else ())
    if has_dep:
        in_specs.append(pl.BlockSpec((SUBLANES, LANES), lambda i, j, q: (0, 0)))
        args += (dep,)
    acc_shape = (tm, tn) if nk > 1 else (SUBLANES, LANES)
    return pl.pallas_call(
        body, name=name,
        grid=(m // tm, n // tn, nk),
        in_specs=in_specs, out_specs=o_spec,
        out_shape=jax.ShapeDtypeStruct(o_shape, out_dtype),
        scratch_shapes=[pltpu.VMEM(acc_shape, F32)],
        compiler_params=_params(("parallel", "parallel", "arbitrary")),
    )(*args)


def _rms_fwd(h, g, *, name):
    tp, d = h.shape
    tr = _tile(tp, 1408, 2 * SUBLANES)

    def body(h_ref, g_ref, o_ref):
        hv = h_ref[...]
        rstd = lax.rsqrt(jnp.mean(hv * hv, axis=-1, keepdims=True) + RMS_EPS)
        o_ref[...] = (hv * rstd * g_ref[...]).astype(BF16)

    return pl.pallas_call(
        body, name=name, grid=(tp // tr,),
        in_specs=[pl.BlockSpec((tr, d), lambda i: (i, 0)), pl.BlockSpec((1, d), lambda i: (0, 0))],
        out_specs=pl.BlockSpec((tr, d), lambda i: (i, 0)),
        out_shape=jax.ShapeDtypeStruct((tp, d), BF16),
        compiler_params=_params(("parallel",)),
    )(h, g.reshape(1, d))


def _rms_fwd_first(x, meta, g, *, tp, name):
    seq, d = x.shape
    n_meta = meta.shape[0]
    n_tok = n_meta + seq
    tr = TOKEN_TILE
    assert tp % tr == 0 and tr % n_meta == 0
    per = tr // n_meta

    def body(x_ref, xp_ref, m_ref, g_ref, h_ref, o_ref):
        i = pl.program_id(0)
        head = jnp.where(i == 0, m_ref[...], xp_ref[...])
        rows = i * tr + lax.broadcasted_iota(jnp.int32, (tr, 1), 0)
        hv = jnp.where(rows < n_tok, jnp.concatenate([head, x_ref[:tr - n_meta, :]], axis=0), 0.0)
        h_ref[...] = hv
        rstd = lax.rsqrt(jnp.mean(hv * hv, axis=-1, keepdims=True) + RMS_EPS)
        o_ref[...] = (hv * rstd * g_ref[...]).astype(BF16)

    row = pl.BlockSpec((tr, d), lambda i: (i, 0))
    own = pl.BlockSpec((tr, d), lambda i: (jnp.minimum(i, -(-seq // tr) - 1), 0))
    before = pl.BlockSpec((n_meta, d), lambda i: (jnp.maximum(i * per - 1, 0), 0))
    return pl.pallas_call(
        body, name=name, grid=(tp // tr,),
        in_specs=[own, before, pl.BlockSpec((n_meta, d), lambda i: (0, 0)), pl.BlockSpec((1, d), lambda i: (0, 0))],
        out_specs=[row, row],
        out_shape=[jax.ShapeDtypeStruct((tp, d), F32), jax.ShapeDtypeStruct((tp, d), BF16)],
        compiler_params=_params(("parallel",)),
    )(x, x, meta, g.reshape(1, d))


def _rms_bwd(h, dhn, dout, g, *, name):
    tp, d = h.shape
    tr = _tile(tp, 528, 2 * SUBLANES)

    def body(h_ref, dhn_ref, dout_ref, g_ref, dh_ref, dhb_ref, dg_ref):
        hv = h_ref[...]
        rstd = lax.rsqrt(jnp.mean(hv * hv, axis=-1, keepdims=True) + RMS_EPS)
        xhat = hv * rstd
        dn = dhn_ref[...]
        dxhat = dn * g_ref[...]
        dh = dout_ref[...] + rstd * (dxhat - xhat * jnp.mean(dxhat * xhat, axis=-1, keepdims=True))
        dh_ref[...] = dh
        dhb_ref[...] = dh.astype(BF16)
        part = jnp.sum(dn * xhat, axis=0, keepdims=True)

        @pl.when(pl.program_id(0) == 0)
        def _():
            dg_ref[...] = part

        @pl.when(pl.program_id(0) > 0)
        def _():
            dg_ref[...] += part

    row = pl.BlockSpec((tr, d), lambda i: (i, 0))
    vec = pl.BlockSpec((1, d), lambda i: (0, 0))
    return pl.pallas_call(
        body, name=name, grid=(tp // tr,),
        in_specs=[row, row, row, vec],
        out_specs=[row, row, vec],
        out_shape=[jax.ShapeDtypeStruct((tp, d), F32), jax.ShapeDtypeStruct((tp, d), BF16),
                   jax.ShapeDtypeStruct((1, d), F32)],
        compiler_params=_params(("arbitrary",)),
    )(h, dhn, dout, g.reshape(1, d))


def _rms_bwd_first(h, dhn, dout, g, *, n_meta, seq, name):
    tp, d = h.shape
    tr = _tile(seq, 512)
    assert seq % tr == 0 and tr % n_meta == 0 and tp >= seq + n_meta
    nt = seq // tr
    per = tr // n_meta

    def grads(hv, dn, do, gv):
        rstd = lax.rsqrt(jnp.mean(hv * hv, axis=-1, keepdims=True) + RMS_EPS)
        xhat = hv * rstd
        dxhat = dn * gv
        dh = do + rstd * (dxhat - xhat * jnp.mean(dxhat * xhat, axis=-1, keepdims=True))
        return dh, jnp.sum(dn * xhat, axis=0, keepdims=True)

    def body(h_ref, dhn_ref, dout_ref, hn_ref, dhnn_ref, doutn_ref, g_ref, gx_ref, dmeta_ref, dg_ref):
        i = pl.program_id(0)
        gv = g_ref[...]
        dh, part = grads(h_ref[...], dhn_ref[...], dout_ref[...], gv)
        dh_next, part_next = grads(hn_ref[...], dhnn_ref[...], doutn_ref[...], gv)
        gx_ref[...] = jnp.concatenate([dh[n_meta:], dh_next], axis=0)

        @pl.when(i == 0)
        def _():
            dmeta_ref[...] = dh[:n_meta]
            dg_ref[...] = part

        @pl.when(i > 0)
        def _():
            dg_ref[...] += part

        @pl.when(i == nt - 1)
        def _():
            dg_ref[...] += part_next

    row = pl.BlockSpec((tr, d), lambda i: (i, 0))
    nxt = pl.BlockSpec((n_meta, d), lambda i: ((i + 1) * per, 0))
    vec = pl.BlockSpec((1, d), lambda i: (0, 0))
    return pl.pallas_call(
        body, name=name, grid=(nt,),
        in_specs=[row, row, row, nxt, nxt, nxt, vec],
        out_specs=[pl.BlockSpec((None, tr, d), lambda i: (0, i, 0)), pl.BlockSpec((n_meta, d), lambda i: (0, 0)), vec],
        out_shape=[jax.ShapeDtypeStruct((1, seq, d), F32), jax.ShapeDtypeStruct((n_meta, d), F32),
                   jax.ShapeDtypeStruct((1, d), F32)],
        compiler_params=_params(("arbitrary",)),
    )(h, dhn, dout, h, dhn, dout, g.reshape(1, d))


def _loss_head(h, tgt, g, *, n_meta, n_tok, name):
    tp, d = h.shape
    seq = tgt.shape[0]
    tr = TOKEN_TILE
    assert tp % tr == 0 and tr % n_meta == 0
    per = tr // n_meta

    def body(h_ref, t_ref, tp_ref, g_ref, dh_ref, dhb_ref, dg_ref, loss_ref):
        i = pl.program_id(0)
        hv = h_ref[...]
        rstd = lax.rsqrt(jnp.mean(hv * hv, axis=-1, keepdims=True) + RMS_EPS)
        xhat = hv * rstd
        gv = g_ref[...]
        rows = i * tr + lax.broadcasted_iota(jnp.int32, (tr, 1), 0)
        valid = jnp.logical_and(rows >= n_meta, rows < n_tok)
        target = jnp.concatenate([tp_ref[...], t_ref[:tr - n_meta, :]], axis=0)
        err = jnp.where(valid, xhat * gv - target, 0.0)
        dy = err * (1.0 / d)
        dxhat = dy * gv
        dh = rstd * (dxhat - xhat * jnp.mean(dxhat * xhat, axis=-1, keepdims=True))
        dh_ref[...] = dh
        dhb_ref[...] = dh.astype(BF16)
        dg_part = jnp.sum(dy * xhat, axis=0, keepdims=True)
        per_row = jnp.sum(err * err, axis=-1, keepdims=True) * (1.0 / d)
        loss_part = jnp.broadcast_to(0.5 * jnp.sum(per_row, axis=0, keepdims=True), (SUBLANES, LANES))

        @pl.when(i == 0)
        def _():
            dg_ref[...] = dg_part
            loss_ref[...] = loss_part

        @pl.when(i > 0)
        def _():
            dg_ref[...] += dg_part
            loss_ref[...] += loss_part

    row = pl.BlockSpec((tr, d), lambda i: (i, 0))
    vec = pl.BlockSpec((1, d), lambda i: (0, 0))
    own = pl.BlockSpec((tr, d), lambda i: (jnp.minimum(i, -(-seq // tr) - 1), 0))
    before = pl.BlockSpec((n_meta, d), lambda i: (jnp.maximum(i * per - 1, 0), 0))
    return pl.pallas_call(
        body, name=name, grid=(tp // tr,),
        in_specs=[row, own, before, vec],
        out_specs=[row, row, vec, pl.BlockSpec((SUBLANES, LANES), lambda i: (0, 0))],
        out_shape=[jax.ShapeDtypeStruct((tp, d), F32), jax.ShapeDtypeStruct((tp, d), BF16),
                   jax.ShapeDtypeStruct((1, d), F32), jax.ShapeDtypeStruct((SUBLANES, LANES), F32)],
        compiler_params=_params(("arbitrary",)),
    )(h, tgt, tgt, g.reshape(1, d))


def _shift_down(halo, tile, s):
    if s == 0:
        return tile
    ext = jnp.concatenate([halo, tile], axis=0)
    return pltpu.roll(ext, s, 0)[SUBLANES:]


def _shift_up(tile, head, s):
    if s == 0:
        return tile
    ext = jnp.concatenate([tile, head], axis=0)
    n = ext.shape[0]
    return pltpu.roll(ext, n - s, 0)[: tile.shape[0]]


def _to_lane_blocks(ref, cols, val):
    for j in range(cols.start // LANES, cols.stop // LANES):
        ref[j] = val[:, j * LANES - cols.start:(j + 1) * LANES - cols.start]


def _from_lane_blocks(ref, cols):
    return jnp.concatenate([ref[j] for j in range(cols.start // LANES, cols.stop // LANES)], axis=1)


def _scan_tile(a_ref, b_ref, out_ref, carry, j, *, reverse):
    ng = a_ref.shape[1] // SUBLANES
    order = list(range(SUBLANES))[::-1] if reverse else list(range(SUBLANES))

    def rows(r):
        return pl.ds(r, ng, stride=SUBLANES)

    prod, loc = {}, {}
    prev = None
    for r in order:
        ar = a_ref[j, rows(r), :]
        br = b_ref[j, rows(r), :]
        prod[r] = ar if prev is None else ar * prod[prev]
        loc[r] = br if prev is None else ar * loc[prev] + br
        prev = r
    pg, lg = prod[prev], loc[prev]
    ones = jnp.ones((SUBLANES,) + pg.shape[1:], F32)
    zeros = jnp.zeros_like(ones)
    s = 1
    while s < ng:
        p_sh = _shift_up(pg, ones, s) if reverse else _shift_down(ones, pg, s)
        l_sh = _shift_up(lg, zeros, s) if reverse else _shift_down(zeros, lg, s)
        lg = pg * l_sh + lg
        pg = pg * p_sh
        s *= 2
    leaving = pg * carry[0:1, :] + lg
    entering = _shift_up(leaving, carry, 1) if reverse else _shift_down(carry, leaving, 1)
    for r in order:
        out_ref[j, rows(r), :] = loc[r] + prod[r] * entering
    last = leaving[0:1, :] if reverse else leaving[ng - 1:ng, :]
    return jnp.broadcast_to(last, carry.shape)


def _gates(ca, wr, wi, br, bi, sp):
    cab = ca.astype(BF16)
    r = _sigmoid(jnp.dot(cab, wr, preferred_element_type=F32) + br)
    ig = _sigmoid(jnp.dot(cab, wi, preferred_element_type=F32) + bi)
    la = -LRU_C * r * sp
    a = jnp.exp(la)
    mult = jnp.sqrt(-jnp.tanh(la) * (a * a + 1.0))
    return r, ig, a, mult


def _mixer_fwd(u, wa, ba, wr_blk, br, wi_blk, bi, lam, wb, *, name):
    tp, din = u.shape
    dl = din // 6
    tt = MIX_ROWS
    cw = GATE_BLOCK
    nch = dl // cw
    assert tp % tt == 0 and dl % cw == 0

    def body(u_ref, wa_ref, ba_ref, wr_ref, br_ref, wi_ref, bi_ref, lam_ref, wb_ref,
             s_ref, y_ref, xa_tail, v_tail, h_carry, a_s, b_s, h_s):
        @pl.when(pl.program_id(0) == 0)
        def _():
            xa_tail[...] = jnp.zeros_like(xa_tail)
            v_tail[...] = jnp.zeros_like(v_tail)
            h_carry[...] = jnp.zeros_like(h_carry)

        for sub in range(MIX_SUBTILES):
            rows = slice(sub * tt, (sub + 1) * tt)
            for ch in range(nch):
                cs = slice(ch * cw, (ch + 1) * cw)

                def seg(s):
                    return slice(s * dl + ch * cw, s * dl + (ch + 1) * cw)

                xa = u_ref[rows, seg(0)]
                halo = xa_tail[:, cs]
                ca = ba_ref[:, cs] + wa_ref[3:4, cs] * xa
                for kk in range(3):
                    ca = ca + wa_ref[kk:kk + 1, cs] * _shift_down(halo, xa, 3 - kk)
                xa_tail[:, cs] = xa[tt - SUBLANES:]
                s_ref[rows, cs] = ca
                sp = _softplus(-lam_ref[:, cs])
                _, ig, a, mult = _gates(ca, wr_ref[ch], wi_ref[ch], br_ref[:, cs], bi_ref[:, cs], sp)
                _to_lane_blocks(a_s, cs, a)
                _to_lane_blocks(b_s, cs, mult * (ig * ca))

                bv = u_ref[rows, seg(2)]
                v = u_ref[rows, seg(3)] * u_ref[rows, seg(4)]
                gb = u_ref[rows, seg(5)]
                vh = v_tail[:, cs]
                cb = wb_ref[2:3, cs] * v
                for kk in range(2):
                    cb = cb + wb_ref[kk:kk + 1, cs] * _shift_down(vh, v, 2 - kk)
                v_tail[:, cs] = v[tt - SUBLANES:]
                y_ref[rows, dl + ch * cw: dl + (ch + 1) * cw] = (bv * cb * (gb * _sigmoid(gb))).astype(BF16)

            for ch in range(nch):
                cs = slice(ch * cw, (ch + 1) * cw)
                for j in range(cs.start // LANES, cs.stop // LANES):
                    lanes = slice(j * LANES, (j + 1) * LANES)
                    h_carry[:, lanes] = _scan_tile(a_s, b_s, h_s, h_carry[:, lanes], j, reverse=False)
                hsv = _from_lane_blocks(h_s, cs)
                s_ref[rows, dl + ch * cw: dl + (ch + 1) * cw] = hsv
                ga = u_ref[rows, dl + ch * cw: dl + (ch + 1) * cw]
                y_ref[rows, cs] = (hsv * (ga * _sigmoid(ga))).astype(BF16)

    tb = tt * MIX_SUBTILES
    assert tp % tb == 0
    row = lambda w: pl.BlockSpec((tb, w), lambda i: (i, 0))
    full = lambda shp: pl.BlockSpec(shp, lambda i: tuple(0 for _ in shp))
    return pl.pallas_call(
        body, name=name, grid=(tp // tb,),
        in_specs=[row(din), full((4, dl)), full((1, dl)), full((nch, cw, cw)), full((1, dl)),
                  full((nch, cw, cw)), full((1, dl)), full((1, dl)), full((3, dl))],
        out_specs=[row(2 * dl), row(2 * dl)],
        out_shape=[jax.ShapeDtypeStruct((tp, 2 * dl), F32), jax.ShapeDtypeStruct((tp, 2 * dl), BF16)],
        scratch_shapes=[pltpu.VMEM((SUBLANES, dl), F32), pltpu.VMEM((SUBLANES, dl), F32),
                        pltpu.VMEM((SUBLANES, dl), F32)] + [pltpu.VMEM((dl // LANES, tt, LANES), F32)] * 3,
        compiler_params=_params(("arbitrary",)),
    )(u, wa, ba, wr_blk, br, wi_blk, bi, lam, wb)


SG_WA, SG_BA, SG_BR, SG_BI, SG_LAM, SG_WB, SG_ROWS = 0, 4, 5, 6, 7, 8, 16


def _mixer_bwd(u, saved, dy, wa, wr_blk, br, wi_blk, bi, lam, wb, *, name):
    tp, din = u.shape
    dl = din // 6
    tt = MIX_ROWS
    cw = GATE_BLOCK
    nch = dl // cw
    tb = tt * MIX_SUBTILES
    assert tp % tb == 0
    nt = tp // tb
    hb = tb // SUBLANES
    tn_dims = (((0,), (0,)), ((), ()))
    nt_dims = (((1,), (1,)), ((), ()))

    def body(u_ref, uh_ref, s_ref, sh_ref, dy_ref, wa_ref, wr_ref, br_ref, wi_ref, bi_ref, lam_ref, wb_ref,
             du_ref, sg_ref, dwr_ref, dwi_ref,
             g_carry, a_head, dca_head, dcb_head, r_s, i_s, a_s, an_s, d_s, g_s):
        i = pl.program_id(0)
        first_tile = i == nt - 1

        @pl.when(i == 0)
        def _():
            for ref in (g_carry, a_head, dca_head, dcb_head, sg_ref, dwr_ref, dwi_ref):
                ref[...] = jnp.zeros_like(ref)

        def halo_of(x):
            return jnp.where(first_tile, 0.0, x)

        for sub in reversed(range(MIX_SUBTILES)):
            rows = slice(sub * tt, (sub + 1) * tt)

            def before(ref, halo_ref, cols):
                if sub == 0:
                    return halo_of(halo_ref[:, cols])
                return ref[sub * tt - SUBLANES:sub * tt, cols]

            for ch in range(nch):
                cs = slice(ch * cw, (ch + 1) * cw)
                cav = s_ref[rows, cs]
                sp = _softplus(-lam_ref[:, cs])
                r, ig, a, _ = _gates(cav, wr_ref[ch], wi_ref[ch], br_ref[:, cs], bi_ref[:, cs], sp)
                r_s[:, cs] = r
                i_s[:, cs] = ig
                a_s[:, cs] = a
                _to_lane_blocks(an_s, cs, _shift_up(a, a_head[:, cs], 1))
                a_head[:, cs] = a[:SUBLANES]
                ga = u_ref[rows, dl + ch * cw: dl + (ch + 1) * cw]
                _to_lane_blocks(d_s, cs, dy_ref[rows, cs] * (ga * _sigmoid(ga)))

            for j in range(dl // LANES):
                lanes = slice(j * LANES, (j + 1) * LANES)
                g_carry[:, lanes] = _scan_tile(an_s, d_s, g_s, g_carry[:, lanes], j, reverse=True)

            for ch in range(nch):
                cs = slice(ch * cw, (ch + 1) * cw)

                def acc_row(r0, val):
                    sg_ref[r0:r0 + 1, cs] += jnp.sum(val, axis=0, keepdims=True)

                def seg(s):
                    return slice(s * dl + ch * cw, s * dl + (ch + 1) * cw)

                cav = s_ref[rows, cs]
                r = r_s[:, cs]
                ig = i_s[:, cs]
                a = a_s[:, cs]
                g = _from_lane_blocks(g_s, cs)
                hsv = s_ref[rows, dl + ch * cw: dl + (ch + 1) * cw]
                lamv = lam_ref[:, cs]
                sp = _softplus(-lamv)
                la = -LRU_C * r * sp
                e2 = a * a
                one_m_e2 = -jnp.tanh(la) * (e2 + 1.0)
                mult = jnp.sqrt(one_m_e2)
                hprev = _shift_down(before(s_ref, sh_ref, slice(dl + ch * cw, dl + (ch + 1) * cw)), hsv, 1)
                icav = ig * cav
                dla = g * (hprev * a - icav * (e2 * lax.rsqrt(one_m_e2)))
                gm = g * mult
                dzi = gm * icav * (1.0 - ig)
                dca = gm * ig
                dla_r = dla * r
                dzr = dla_r * (1.0 - r) * (-LRU_C * sp)
                sg_ref[SG_LAM:SG_LAM + 1, cs] += jnp.sum(dla_r, axis=0, keepdims=True) * (LRU_C * _sigmoid(-lamv))
                acc_row(SG_BR, dzr)
                acc_row(SG_BI, dzi)
                dzr_b = dzr.astype(BF16)
                dzi_b = dzi.astype(BF16)
                cab = cav.astype(BF16)
                dca = dca + lax.dot_general(dzr_b, wr_ref[ch], nt_dims, preferred_element_type=F32)
                dca = dca + lax.dot_general(dzi_b, wi_ref[ch], nt_dims, preferred_element_type=F32)
                dwr_ref[ch] += lax.dot_general(cab, dzr_b, tn_dims, preferred_element_type=F32)
                dwi_ref[ch] += lax.dot_general(cab, dzi_b, tn_dims, preferred_element_type=F32)
                acc_row(SG_BA, dca)
                xa = u_ref[rows, seg(0)]
                head = dca_head[:, cs]
                dxa = wa_ref[3:4, cs] * dca
                acc_row(SG_WA + 3, dca * xa)
                for kk in range(3):
                    later = _shift_up(dca, head, 3 - kk)
                    acc_row(SG_WA + kk, later * xa)
                    dxa = dxa + wa_ref[kk:kk + 1, cs] * later
                dca_head[:, cs] = dca[:SUBLANES]
                ga = u_ref[rows, seg(1)]
                sga = _sigmoid(ga)
                dga = dy_ref[rows, cs] * hsv * (sga + (ga * sga) * (1.0 - sga))
                du_ref[rows, seg(0)] = dxa.astype(BF16)
                du_ref[rows, seg(1)] = dga.astype(BF16)

                bv = u_ref[rows, seg(2)]
                cv = u_ref[rows, seg(3)]
                xb = u_ref[rows, seg(4)]
                gb = u_ref[rows, seg(5)]
                dyb = dy_ref[rows, dl + ch * cw: dl + (ch + 1) * cw]
                v = cv * xb
                vh = before(u_ref, uh_ref, seg(3)) * before(u_ref, uh_ref, seg(4))
                v1 = _shift_down(vh, v, 1)
                v2 = _shift_down(vh, v, 2)
                cb = wb_ref[2:3, cs] * v + wb_ref[1:2, cs] * v1 + wb_ref[0:1, cs] * v2
                sgb = _sigmoid(gb)
                sl = gb * sgb
                dyb_b = dyb * bv
                dyb_cb = dyb * cb
                dcb = dyb_b * sl
                du_ref[rows, seg(2)] = (dyb_cb * sl).astype(BF16)
                du_ref[rows, seg(5)] = (dyb_cb * bv * (sgb + sl * (1.0 - sgb))).astype(BF16)
                bhead = dcb_head[:, cs]
                dv = wb_ref[2:3, cs] * dcb
                acc_row(SG_WB + 2, dcb * v)
                for kk in range(2):
                    later = _shift_up(dcb, bhead, 2 - kk)
                    acc_row(SG_WB + kk, later * v)
                    dv = dv + wb_ref[kk:kk + 1, cs] * later
                dcb_head[:, cs] = dcb[:SUBLANES]
                du_ref[rows, seg(3)] = (dv * xb).astype(BF16)
                du_ref[rows, seg(4)] = (dv * cv).astype(BF16)

    rev = lambda w: pl.BlockSpec((tb, w), lambda i: (nt - 1 - i, 0))
    halo = lambda w: pl.BlockSpec((SUBLANES, w), lambda i: (jnp.maximum((nt - 1 - i) * hb - 1, 0), 0))
    full = lambda shp: pl.BlockSpec(shp, lambda i: tuple(0 for _ in shp))
    vm = lambda r: pltpu.VMEM((r, dl), F32)
    return pl.pallas_call(
        body, name=name, grid=(nt,),
        in_specs=[rev(din), halo(din), rev(2 * dl), halo(2 * dl), rev(2 * dl), full((4, dl)),
                  full((nch, cw, cw)), full((1, dl)), full((nch, cw, cw)), full((1, dl)), full((1, dl)), full((3, dl))],
        out_specs=[rev(din), full((SG_ROWS, dl)), full((nch, cw, cw)), full((nch, cw, cw))],
        out_shape=[jax.ShapeDtypeStruct((tp, din), BF16), jax.ShapeDtypeStruct((SG_ROWS, dl), F32),
                   jax.ShapeDtypeStruct((nch, cw, cw), F32), jax.ShapeDtypeStruct((nch, cw, cw), F32)],
        scratch_shapes=[vm(SUBLANES), vm(SUBLANES), vm(SUBLANES), vm(SUBLANES), vm(tt), vm(tt), vm(tt)]
        + [pltpu.VMEM((dl // LANES, tt, LANES), F32)] * 3,
        compiler_params=_params(("arbitrary",)),
    )(u, u, saved, saved, dy, wa, wr_blk, br, wi_blk, bi, lam, wb)


def _adamw(w, g, m, v, *, name, landed=None, layer=None, depth=None, into=None, row_off=0, own=None):
    r, c = w.shape[-2:]
    rows = g.shape[1] if own == "lead" else g.shape[0]
    tr = _tile(rows, 512, 2 * SUBLANES)
    assert row_off % tr == 0
    boff = row_off // tr
    bc1 = 1.0 - ADAM_B1 ** ADAM_STEP
    bc2 = 1.0 - ADAM_B2 ** ADAM_STEP
    slots = landed is not None

    def body(*refs):
        if own is not None:
            refs = refs[1:]
        if into is not None:
            refs = refs[:-8] + refs[-4:]
        if slots:
            w_ref, g_ref, l_ref, m_ref, v_ref, grad_ref, delta_ref, nm_ref, nv_ref = refs
            gv = g_ref[...].astype(F32)
            for s in range(N_DEV - 1):
                gv = gv + l_ref[s].astype(F32)
        else:
            w_ref, g_ref, m_ref, v_ref, grad_ref, delta_ref, nm_ref, nv_ref = refs
            gv = g_ref[...]
        wv = w_ref[...]
        mn = ADAM_B1 * m_ref[...] + (1.0 - ADAM_B1) * gv
        vn = ADAM_B2 * v_ref[...] + (1.0 - ADAM_B2) * (gv * gv)
        m_hat = mn / bc1
        v_hat = vn / bc2
        grad_ref[...] = gv
        delta_ref[...] = -ADAM_LR * (m_hat / (jnp.sqrt(v_hat) + ADAM_EPS) + ADAM_WD * wv)
        nm_ref[...] = mn
        nv_ref[...] = vn

    if depth is None:
        blk = pl.BlockSpec((tr, c), lambda i, *_: (i + boff, 0))
    else:
        blk = pl.BlockSpec((None, tr, c), lambda i, *_: (layer, i + boff, 0))
    if own == "cols":
        g_blk = pl.BlockSpec((tr, c), lambda i, me_ref: (i, me_ref[0]))
    elif own == "lead":
        g_blk = pl.BlockSpec((None, tr, c), lambda i, me_ref: (me_ref[0], i, 0))
    else:
        g_blk = pl.BlockSpec((tr, c), lambda i, *_: (i, 0))
    l_spec = [pl.BlockSpec((N_DEV - 1, tr, c), lambda i, *_: (0, i, 0))] if slots else []
    args = (w, g, landed, m, v) if slots else (w, g, m, v)
    in_specs = [blk, g_blk] + l_spec + [blk, blk]
    if depth is None:
        shp = jax.ShapeDtypeStruct((r, c), F32)
        out_blk = blk
    else:
        shp = jax.ShapeDtypeStruct((depth, r, c), F32)
        out_blk = pl.BlockSpec((None, tr, c), lambda i, *_: (layer, i + boff, 0))
    prefetch = () if own is None else (_mesh_pos()[3].astype(jnp.int32).reshape(1),)
    aliases = {}
    if into is not None:
        aliases = {len(prefetch) + len(args) + j: j for j in range(4)}
        in_specs = in_specs + [ANY] * 4
        args = args + tuple(into)
    return pl.pallas_call(
        body, name=name,
        grid_spec=pltpu.PrefetchScalarGridSpec(
            num_scalar_prefetch=len(prefetch), grid=(rows // tr,), in_specs=in_specs, out_specs=[out_blk] * 4),
        out_shape=[shp] * 4, input_output_aliases=aliases,
        compiler_params=_params(("parallel",)),
    )(*prefetch, *args)


def _slot_sum(g, *, name):
    _, r, c = g.shape
    tr = _tile(r, 512, SUBLANES)

    def body(g_ref, o_ref):
        gv = g_ref[0].astype(F32)
        for s in range(1, N_DEV):
            gv = gv + g_ref[s].astype(F32)
        o_ref[...] = gv

    return pl.pallas_call(
        body, name=name, grid=(r // tr,),
        in_specs=[pl.BlockSpec((N_DEV, tr, c), lambda i: (0, i, 0))],
        out_specs=pl.BlockSpec((tr, c), lambda i: (i, 0)),
        out_shape=jax.ShapeDtypeStruct((r, c), F32),
        compiler_params=_params(("parallel",)),
    )(g)


def _mesh_pos():
    x, y, c = lax.axis_index("x"), lax.axis_index("y"), lax.axis_index("c")
    return x, y, c, 4 * x + 2 * y + c


ANY = pl.BlockSpec(memory_space=pl.ANY)


GATHER_COPIES = 9


def _all_gather(srcs, out_shapes, views, *, name):
    n = len(srcs)
    SIB, X_OWN, Y_OWN, X_DIAG, Y_DIAG, SIB_X, SIB_Y, SIB_DIAG_TOP, SIB_DIAG_BOTTOM = range(GATHER_COPIES)

    def body(*refs):
        src = refs[:n]
        dst = refs[n:2 * n]
        send_sems, recv_sems, local_sems = refs[2 * n:]
        x, y, c, me = _mesh_pos()
        sibling, x_nbr, y_nbr = (x, y, 1 - c), (1 - x, y, c), (x, 1 - y, c)

        def block(a, px, py, pc, half=None):
            win = views[a](dst[a], 4 * px + 2 * py + pc)
            if half is None:
                return win
            rows = win.shape[0] // 2
            return win.at[pl.ds(half * rows, rows)]

        def copy(a, k, win, to, from_src=False):
            return pltpu.make_async_remote_copy(
                src_ref=src[a] if from_src else win, dst_ref=win,
                send_sem=send_sems.at[a * GATHER_COPIES + k], recv_sem=recv_sems.at[a * GATHER_COPIES + k],
                device_id=to, device_id_type=MESH)

        mine = [pltpu.make_async_copy(src[a], block(a, x, y, c), local_sems.at[a]) for a in range(n)]
        started = []

        def start(cp):
            cp.start()
            started.append(cp)

        for a in range(n):
            mine[a].start()
            own = block(a, x, y, c)
            start(copy(a, SIB, own, sibling, True))
            start(copy(a, X_OWN, own, x_nbr, True))
            start(copy(a, Y_OWN, own, y_nbr, True))
        for a in range(n):
            from_y = block(a, x, 1 - y, c)
            copy(a, Y_OWN, from_y, y_nbr).wait_recv()
            start(copy(a, X_DIAG, block(a, x, 1 - y, c, 0), x_nbr))
            start(copy(a, SIB_Y, from_y, sibling))
            from_x = block(a, 1 - x, y, c)
            copy(a, X_OWN, from_x, x_nbr).wait_recv()
            start(copy(a, Y_DIAG, block(a, 1 - x, y, c, 1), y_nbr))
            start(copy(a, SIB_X, from_x, sibling))
        for a in range(n):
            top = block(a, 1 - x, 1 - y, c, 0)
            copy(a, X_DIAG, top, x_nbr).wait_recv()
            start(copy(a, SIB_DIAG_TOP, top, sibling))
            bottom = block(a, 1 - x, 1 - y, c, 1)
            copy(a, Y_DIAG, bottom, y_nbr).wait_recv()
            start(copy(a, SIB_DIAG_BOTTOM, bottom, sibling))
        for a in range(n):
            copy(a, SIB, block(a, x, y, 1 - c), sibling).wait_recv()
            copy(a, SIB_X, block(a, 1 - x, y, 1 - c), sibling).wait_recv()
            copy(a, SIB_Y, block(a, x, 1 - y, 1 - c), sibling).wait_recv()
            copy(a, SIB_DIAG_TOP, block(a, 1 - x, 1 - y, 1 - c, 0), sibling).wait_recv()
            copy(a, SIB_DIAG_BOTTOM, block(a, 1 - x, 1 - y, 1 - c, 1), sibling).wait_recv()
        for cp in started:
            cp.wait_send()
        for cp in mine:
            cp.wait()

    return pl.pallas_call(
        body, name=name,
        in_specs=[ANY] * n, out_specs=[ANY] * n,
        out_shape=[jax.ShapeDtypeStruct(s, x.dtype) for s, x in zip(out_shapes, srcs)],
        scratch_shapes=[pltpu.SemaphoreType.DMA((GATHER_COPIES * n,)), pltpu.SemaphoreType.DMA((GATHER_COPIES * n,)),
                        pltpu.SemaphoreType.DMA((n,))],
    )(*srcs)


def _all_gather_small(src, *, name):
    def body(src_ref, dst_ref, send_sems, recv_sems, local_sem):
        x, y, c, me = _mesh_pos()
        mine = pltpu.make_async_copy(src_ref, dst_ref.at[me], local_sem)
        mine.start()
        sent = []
        for k in range(1, N_DEV):
            cp = pltpu.make_async_remote_copy(
                src_ref=src_ref, dst_ref=dst_ref.at[me], send_sem=send_sems.at[k - 1], recv_sem=recv_sems.at[k - 1],
                device_id=_peer_of(x, y, c, k), device_id_type=MESH)
            cp.start()
            sent.append(cp)
        for k in range(1, N_DEV):
            px, py, pc = _peer_of(x, y, c, k)
            pltpu.make_async_remote_copy(
                src_ref=src_ref, dst_ref=dst_ref.at[4 * px + 2 * py + pc], send_sem=send_sems.at[k - 1],
                recv_sem=recv_sems.at[k - 1], device_id=(px, py, pc), device_id_type=MESH).wait_recv()
        for cp in sent:
            cp.wait_send()
        mine.wait()

    return pl.pallas_call(
        body, name=name, in_specs=[ANY], out_specs=ANY,
        out_shape=jax.ShapeDtypeStruct((N_DEV,) + src.shape, src.dtype),
        scratch_shapes=[pltpu.SemaphoreType.DMA((N_DEV - 1,)), pltpu.SemaphoreType.DMA((N_DEV - 1,)),
                        pltpu.SemaphoreType.DMA],
    )(src)


HBM = pl.BlockSpec(memory_space=pltpu.HBM)
SEM = pl.BlockSpec(memory_space=pltpu.SEMAPHORE)
EFFECT = pltpu.SideEffectType.DATAFLOW_SIDE_EFFECTING


def _peer_of(x, y, c, k):
    return (1 - x if k & 4 else x, 1 - y if k & 2 else y, 1 - c if k & 1 else c)


def _peer_copies(n, wins, src, land, send_sems, recv_sems):
    x, y, c, me = _mesh_pos()
    out = []
    for a in range(n):
        for k in range(1, N_DEV):
            px, py, pc = _peer_of(x, y, c, k)
            plan = wins[a](src[a], land[a], me, 4 * px + 2 * py + pc, k)
            if plan is None:
                continue
            target = _peer_of(x, y, c, plan[2]) if len(plan) == 3 else (px, py, pc)
            out.append(pltpu.make_async_remote_copy(
                src_ref=plan[0], dst_ref=plan[1],
                send_sem=send_sems.at[a * 7 + k - 1], recv_sem=recv_sems.at[a * 7 + k - 1],
                device_id=target, device_id_type=MESH))
    return out


def _push_start(srcs, lands, wins, *, name):
    n = len(srcs)

    def body(*refs):
        src = refs[:n]
        land = refs[n:2 * n]
        send_sems, recv_sems = refs[2 * n], refs[2 * n + 1]
        token = refs[-1]
        for cp in _peer_copies(n, wins, src, land, send_sems, recv_sems):
            cp.start()
        token[...] = jnp.zeros_like(token)

    bufs = (*srcs, *lands)
    return pl.pallas_call(
        body, name=name,
        out_shape=(pltpu.SemaphoreType.DMA((7 * n,)), pltpu.SemaphoreType.DMA((7 * n,)),
                   *[pltpu.HBM(v.shape, v.dtype) for v in bufs], jax.ShapeDtypeStruct((SUBLANES, LANES), F32)),
        in_specs=[HBM] * (2 * n),
        out_specs=(SEM, SEM, *[HBM] * (2 * n), pl.BlockSpec(memory_space=pltpu.VMEM)),
        input_output_aliases={i: 2 + i for i in range(2 * n)},
        compiler_params=pltpu.CompilerParams(has_side_effects=EFFECT),
    )(*[pltpu.with_memory_space_constraint(v, pltpu.HBM) for v in bufs])


def _push_wait(handle, wins, after, *, name):
    send_sems, recv_sems, *bufs, _ = handle
    n = len(bufs) // 2

    def body(*refs):
        src = refs[:n]
        land = refs[n:2 * n]
        for cp in _peer_copies(n, wins, src, land, refs[2 * n], refs[2 * n + 1]):
            cp.wait_send()
            cp.wait_recv()

    outs = pl.pallas_call(
        body, name=name,
        out_shape=tuple(pltpu.HBM(v.shape, v.dtype) for v in bufs),
        in_specs=[HBM] * (2 * n) + [SEM, SEM, ANY],
        out_specs=tuple([HBM] * (2 * n)),
        input_output_aliases={i: i for i in range(2 * n)},
        compiler_params=pltpu.CompilerParams(has_side_effects=EFFECT),
    )(*bufs, send_sems, recv_sems, after)
    return outs[:n], outs[n:]


def _gather_lead(src, land, me, peer, k):
    return src, land.at[me]


SAME_CORE_PEERS = (2, 4, 6)
SIBLING = 1


def _gather_cols(width, ks=range(1, N_DEV)):
    def win(src, land, me, peer, k):
        return (src, land.at[:, pl.ds(me * width, width)]) if k in ks else None
    return win


def _forward_cols(width):
    def win(src, land, me, peer, k):
        block = land.at[:, pl.ds(peer * width, width)]
        return (block, block, SIBLING) if k in SAME_CORE_PEERS else None
    return win


def _scatter_lead(src, land, me, peer, k):
    return src.at[peer], land.at[k - 1]


def _scatter_cols(width):
    def win(src, land, me, peer, k):
        return src.at[:, pl.ds(peer * width, width)], land.at[k - 1]
    return win


def _place_block(own, *, cols, name):
    rows, width = own.shape
    tr = _tile(rows, 512, 2 * SUBLANES)
    _, _, _, me = _mesh_pos()

    def body(me_ref, x_ref, o_ref):
        o_ref[...] = x_ref[...]

    if cols:
        out_spec = pl.BlockSpec((tr, width), lambda i, me_ref: (i, me_ref[0]))
        shape = (rows, N_DEV * width)
    else:
        out_spec = pl.BlockSpec((None, tr, width), lambda i, me_ref: (me_ref[0], i, 0))
        shape = (N_DEV, rows, width)
    return pl.pallas_call(
        body, name=name,
        grid_spec=pltpu.PrefetchScalarGridSpec(
            num_scalar_prefetch=1, grid=(rows // tr,),
            in_specs=[pl.BlockSpec((tr, width), lambda i, me_ref: (i, 0))], out_specs=out_spec),
        out_shape=jax.ShapeDtypeStruct(shape, own.dtype),
        compiler_params=_params(("arbitrary",)),
    )(me.astype(jnp.int32).reshape(1), own)


def _dep(x, token):
    return x + token[0, 0].astype(x.dtype)


def _lead(ref, d):
    return ref.at[d]


def _col_window(width):
    def view(ref, d):
        return ref.at[:, pl.ds(d * width, width)]
    return view


def _pack(arrs):
    flat = jnp.concatenate([a.reshape(-1).astype(F32) for a in arrs])
    n = flat.shape[0]
    rows = -(-n // (2 * SUBLANES * LANES)) * 2 * SUBLANES
    return jnp.pad(flat, (0, rows * LANES - n)).reshape(rows, LANES)


def _unpack(buf, shapes):
    flat = buf.reshape(-1)
    out, off = [], 0
    for s in shapes:
        n = 1
        for q in s:
            n *= q
        out.append(flat[off:off + n].reshape(s))
        off += n
    return out


def _blockdiag(w, cw):
    h, hd, _ = w.shape
    per = cw // hd
    wg = w.reshape(h // per, per, hd, hd)
    eye = jnp.eye(per, dtype=w.dtype)
    blk = jnp.einsum("gpij,pq->gpiqj", wg, eye)
    return blk.reshape(h // per, cw, cw).astype(BF16)


def _blockdiag_extract(g, hd):
    n, cw, _ = g.shape
    per = cw // hd
    g5 = g.reshape(n, per, hd, per, hd)
    idx = jnp.arange(per)
    return g5[:, idx, :, idx, :].transpose(1, 0, 2, 3).reshape(n * per, hd, hd)


def kernel(x, meta, norm_g, w_in, conv_a_w, conv_a_b, lru_wr, lru_br, lru_wi, lru_bi, lru_lambda, conv_b_w, w_out, final_g, loss_target, m_meta, m_norm_g, m_w_in, m_conv_a_w, m_conv_a_b, m_lru_wr, m_lru_br, m_lru_wi, m_lru_bi, m_lru_lambda, m_conv_b_w, m_w_out, m_final_g, v_meta, v_norm_g, v_w_in, v_conv_a_w, v_conv_a_b, v_lru_wr, v_lru_br, v_lru_wi, v_lru_bi, v_lru_lambda, v_conv_b_w, v_w_out, v_final_g):
    _, seq, d = x.shape
    n_meta = meta.shape[0]
    depth = w_in.shape[0]
    din = w_in.shape[2] * N_DEV
    dl = din // 6
    dmix = 2 * dl
    wcol = w_in.shape[2]
    wrow = w_out.shape[1]
    mcol = meta.shape[1]
    ccol = conv_a_w.shape[2]
    hd = lru_wr.shape[2]
    n_tok = n_meta + seq
    tp = -(-n_tok // TOKEN_TILE) * TOKEN_TILE
    me = 4 * lax.axis_index("x") + 2 * lax.axis_index("y") + lax.axis_index("c")

    bf = lambda a: a.astype(BF16)
    small_mine = _pack([meta, conv_a_w, conv_b_w])
    first = _all_gather([bf(w_in[0]), small_mine], [(d, din), (N_DEV,) + small_mine.shape],
                        [_col_window(wcol), _lead], name="gather_first")
    flat = first[1].reshape(N_DEV, -1)
    sizes = [meta.size, conv_a_w.size, conv_b_w.size]
    meta_full = jnp.moveaxis(flat[:, :sizes[0]].reshape(N_DEV, n_meta, mcol), 0, 1).reshape(n_meta, d)
    wa_full = jnp.moveaxis(flat[:, sizes[0]:sizes[0] + sizes[1]].reshape(N_DEV, depth, 4, ccol), 0, 2) \
        .reshape(depth, 4, dl)
    wb_full = jnp.moveaxis(flat[:, sizes[0] + sizes[1]:sum(sizes)].reshape(N_DEV, depth, 3, ccol), 0, 2) \
        .reshape(depth, 3, dl)
    w_in_full = [None] * depth
    w_out_full = [None] * depth

    push_out = [None] * depth
    push_in = [None] * depth
    w_in_full[0], src = lax.optimization_barrier((first[0], bf(w_out[0])))
    push_out[0] = _push_start([src], [_place_block(src, cols=False, name="place_wout_0")], [_gather_lead],
                              name="gather_wout_0_start")
    token = push_out[0][-1]
    first_hop = [_gather_cols(wcol, ks=(SIBLING,) + SAME_CORE_PEERS)]
    second_hop = [_forward_cols(wcol)]
    for l in range(1, depth):
        src = bf(_dep(w_in[l], token))
        push_in[l] = _push_start([src], [_place_block(src, cols=True, name=f"place_win_{l}")], first_hop,
                                 name=f"gather_win_{l}_start")
        src = bf(_dep(w_out[l], push_in[l][-1]))
        push_out[l] = _push_start([src], [_place_block(src, cols=False, name=f"place_wout_{l}")], [_gather_lead],
                                  name=f"gather_wout_{l}_start")
        token = push_out[l][-1]

    wr_blk = [_blockdiag(lru_wr[l], GATE_BLOCK) for l in range(depth)]
    wi_blk = [_blockdiag(lru_wi[l], GATE_BLOCK) for l in range(depth)]
    vec = lambda a: a.reshape(1, dl)

    tm = _tile(tp, 1408)
    saved = []
    for l in range(depth):
        if l == 0:
            h, hn = _rms_fwd_first(x[0], meta_full, _dep(norm_g[l], token), tp=tp, name=f"rms_fwd_{l}")
        else:
            hn = _rms_fwd(h, _dep(norm_g[l], push_in[l][-1]), name=f"rms_fwd_{l}")
        if l > 0:
            _, landed = _push_wait(push_in[l], second_hop, hn, name=f"forward_win_{l}_wait")
            w_in_full[l] = landed[0]
        u = _matmul(hn, w_in_full[l], tm=tm, tn=_tile(din, 1536), tk=d, name=f"mm_u_{l}")
        mixed, y = _mixer_fwd(u, wa_full[l], vec(conv_a_b[l]), wr_blk[l], vec(lru_br[l]), wi_blk[l], vec(lru_bi[l]),
                              vec(lru_lambda[l]), wb_full[l], name=f"mixer_fwd_{l}")
        _, landed = _push_wait(push_out[l], [_gather_lead], y, name=f"gather_wout_{l}_wait")
        w_out_full[l] = landed[0].reshape(dmix, d)
        h_next = _matmul(y, w_out_full[l], tm=tm, tn=_tile(d, 512), tk=dmix, add=h, name=f"mm_out_{l}")
        if l + 1 < depth:
            src, landed = _push_wait(push_in[l + 1], first_hop, h_next, name=f"gather_win_{l + 1}_wait")
            push_in[l + 1] = _push_start(src, landed, second_hop, name=f"forward_win_{l + 1}_start")
        saved.append((h, hn, u, mixed, y))
        h = h_next

    dh, dhb, dg_final, loss_part = _loss_head(h, loss_target[0], final_g, n_meta=n_meta, n_tok=n_tok,
                                              name="loss_head")

    small_grads = [None] * depth
    sent_out = [None] * depth
    sent_in = [None] * depth
    scatter_in = [_scatter_cols(wcol)]
    token = None
    dg_norms = []
    for l in reversed(range(depth)):
        h_in, hn, u, mixed, y = saved[l]
        dy = _matmul(dhb, w_out_full[l], tb=True, tm=tm, tn=_tile(dmix, 1024), tk=d, dep=token, name=f"mm_dy_{l}")
        dw_out = _matmul(y, dhb, ta=True, tm=_tile(dmix, 1024), tn=_tile(d, 1024), tk=tp, out_dtype=BF16,
                         name=f"mm_dwout_{l}")
        sent_out[l] = _push_start([dw_out.reshape(N_DEV, wrow, d)], [lax.empty((N_DEV - 1, wrow, d), BF16)],
                                  [_scatter_lead], name=f"scatter_wout_{l}_start")
        du, sg, dwr, dwi = _mixer_bwd(u, mixed, dy, wa_full[l], wr_blk[l], vec(lru_br[l]), wi_blk[l], vec(lru_bi[l]),
                                      vec(lru_lambda[l]), _dep(wb_full[l], sent_out[l][-1]), name=f"mixer_bwd_{l}")
        small_grads[l] = (sg, dwr, dwi)
        if l == 0:
            rows = jnp.stack([small_grads[j][0] for j in range(depth)])
            early = [_pack([
                rows[:, SG_BA], rows[:, SG_BR], rows[:, SG_BI], rows[:, SG_LAM], rows[:, SG_WA:SG_WA + 4],
                rows[:, SG_WB:SG_WB + 3], dg_final[0], *dg_norms]),
                _pack([jnp.stack([_blockdiag_extract(small_grads[j][1], hd) for j in range(depth)]),
                       jnp.stack([_blockdiag_extract(small_grads[j][2], hd) for j in range(depth)])]).astype(BF16)]
            early_land = [lax.dynamic_update_slice(lax.empty((N_DEV,) + a.shape, a.dtype), a[None], (me, 0, 0))
                          for a in early]
            sent_early = _push_start(early, early_land, [_gather_lead] * 2, name="gather_early_grads_start")
        parts = 2 if l == 0 else 1
        token = sent_early[-1] if l == 0 else None
        sent_in[l] = []
        for p in range(parts):
            dw_in = _matmul(hn, du, ta=True, tm=_tile(d // parts, 512), tn=_tile(din, 1536), tk=tp, out_dtype=BF16,
                            dep=token, m_part=(p, parts), name=f"mm_dwin_{l}_{p}")
            sent_in[l].append(_push_start([dw_in], [lax.empty((N_DEV - 1, d // parts, wcol), BF16)], scatter_in,
                                          name=f"scatter_win_{l}_{p}_start"))
            token = sent_in[l][-1][-1]
        dhn = _matmul(du, w_in_full[l], tb=True, tm=_tile(tp, 528, 2 * SUBLANES), tn=_tile(d, 1024), tk=din, dep=token,
                      name=f"mm_dhn_{l}")
        if l > 0:
            dh, dhb, dg_norm = _rms_bwd(h_in, dhn, dh, norm_g[l], name=f"rms_bwd_{l}")
            dg_norms.append(dg_norm[0])
        else:
            grad_x, d_meta, dg_norm = _rms_bwd_first(h_in, dhn, dh, norm_g[l], n_meta=n_meta, seq=seq,
                                                     name=f"rms_bwd_{l}")

    big = {"win": None, "wout": None}

    def big_adamw(l, after):
        src, landed = _push_wait(sent_out[l], [_scatter_lead], after, name=f"scatter_wout_{l}_wait")
        big["wout"] = _adamw(w_out, src[0], m_w_out, v_w_out, landed=landed[0], layer=l, depth=depth,
                             into=big["wout"], own="lead", name=f"adamw_w_out_{l}")
        after = big["wout"][0]
        for p, sent in enumerate(sent_in[l]):
            src, landed = _push_wait(sent, scatter_in, after, name=f"scatter_win_{l}_{p}_wait")
            big["win"] = _adamw(w_in, src[0], m_w_in, v_w_in, landed=landed[0], layer=l, depth=depth,
                                into=big["win"], row_off=p * src[0].shape[0], own="cols",
                                name=f"adamw_w_in_{l}_{p}")
            after = big["win"][0]
        return after

    after = dg_norm
    for l in reversed(range(1, depth)):
        after = big_adamw(l, after)

    late = _pack([dg_norm[0], d_meta, loss_part[0:1, 0:1]])
    if depth > 1:
        late, after = lax.optimization_barrier((late, after))
    late_all = _all_gather_small(late, name="gather_late_grads")
    late_sum = _unpack(_slot_sum(late_all, name="sum_late_grads"), [(d,), (n_meta, d), ()])
    loss = late_sum[2]
    _, early_all = _push_wait(sent_early, [_gather_lead] * 2, late_sum[0], name="gather_early_grads_wait")
    vec_shapes = [conv_a_b.shape, lru_br.shape, lru_bi.shape, lru_lambda.shape, (depth, 4, dl), (depth, 3, dl),
                  final_g.shape] + [(d,)] * (depth - 1)
    e = _unpack(_slot_sum(early_all[0], name="sum_early_vectors"), vec_shapes)
    g_wr, g_wi = _unpack(_slot_sum(early_all[1], name="sum_early_maps"), [lru_wr.shape, lru_wi.shape])
    g_norm = jnp.stack([late_sum[0]] + e[7:][::-1])
    g_meta = lax.dynamic_slice_in_dim(late_sum[1], me * mcol, mcol, axis=1)
    g_wa = lax.dynamic_slice_in_dim(e[4], me * ccol, ccol, axis=2)
    g_wb = lax.dynamic_slice_in_dim(e[5], me * ccol, ccol, axis=2)

    small_w = [norm_g, conv_a_b, lru_wr, lru_br, lru_wi, lru_bi, lru_lambda, final_g, meta, conv_a_w, conv_b_w]
    small_m = [m_norm_g, m_conv_a_b, m_lru_wr, m_lru_br, m_lru_wi, m_lru_bi, m_lru_lambda, m_final_g, m_meta,
               m_conv_a_w, m_conv_b_w]
    small_v = [v_norm_g, v_conv_a_b, v_lru_wr, v_lru_br, v_lru_wi, v_lru_bi, v_lru_lambda, v_final_g, v_meta,
               v_conv_a_w, v_conv_b_w]
    small_g = [g_norm, e[0], g_wr, e[1], g_wi, e[2], e[3], e[6], g_meta, g_wa, g_wb]
    small_out = _adamw(_pack(small_w), _pack(small_g), _pack(small_m), _pack(small_v), name="adamw_small")
    small_shapes = [a.shape for a in small_w]
    s_grad, s_delta, s_m, s_v = [_unpack(o, small_shapes) for o in small_out]

    big_adamw(0, small_out[0])
    win_out, wout_out = big["win"], big["wout"]

    names = ["norm_g", "conv_a_b", "lru_wr", "lru_br", "lru_wi", "lru_bi", "lru_lambda", "final_g", "meta",
             "conv_a_w", "conv_b_w"]
    order = ["meta", "norm_g", "w_in", "conv_a_w", "conv_a_b", "lru_wr", "lru_br", "lru_wi", "lru_bi", "lru_lambda",
             "conv_b_w", "w_out", "final_g"]

    def family(idx, small):
        table = {nm: small[i] for i, nm in enumerate(names)}
        table["w_in"] = win_out[idx]
        table["w_out"] = wout_out[idx]
        return [table[nm] for nm in order]

    return (loss, grad_x, *family(0, s_grad), *family(1, s_delta), *family(2, s_m), *family(3, s_v))
```

```python
import jax
import jax.numpy as jnp
from jax import lax
from jax.experimental import pallas as pl
from jax.experimental.pallas import tpu as pltpu

F32 = jnp.float32
BF16 = jnp.bfloat16
MESH = pl.DeviceIdType.MESH

N_DEV = 8
RMS_EPS = 1e-6
LRU_C = 8.0
ADAM_LR = 0.001
ADAM_B1 = 0.9
ADAM_B2 = 0.999
ADAM_EPS = 1e-08
ADAM_WD = 0.01
ADAM_STEP = 10

V7X_VMEM_LIMIT = 52 * 1024 * 1024
LANES = 128
SUBLANES = 8
TOKEN_TILE = 384
MIX_ROWS = 128
MIX_SUBTILES = 3
GATE_BLOCK = 128


def _params(sem):
    return pltpu.CompilerParams(dimension_semantics=sem, vmem_limit_bytes=V7X_VMEM_LIMIT)


def _tile(n, target, align=LANES):
    best = None
    for t in range(align, min(n, target) + 1, align):
        if n % t == 0:
            best = t
    return n if best is None else best


def _sigmoid(z):
    return 0.5 * jnp.tanh(0.5 * z) + 0.5


def _softplus(z):
    e = jnp.exp(-jnp.abs(z))
    u = 1.0 + e
    l1p = jnp.where(u == 1.0, e, jnp.log(u) * e / jnp.where(u == 1.0, 1.0, u - 1.0))
    return jnp.maximum(z, 0.0) + l1p


def _matmul(a, b, *, ta=False, tb=False, tm, tn, tk, out_dtype=F32, add=None, dep=None, m_part=None, name):
    m, k = (a.shape[1], a.shape[0]) if ta else a.shape
    m_off = 0
    if m_part is not None:
        assert add is None and m % (m_part[1] * tm) == 0
        m //= m_part[1]
        m_off = m_part[0] * (m // tm)
    n, kb = b.shape if tb else b.shape[::-1]
    assert kb == k
    assert m % tm == 0 and n % tn == 0 and k % tk == 0, (m, n, k, tm, tn, tk)
    nk = k // tk
    a_spec = pl.BlockSpec((tk, tm), lambda i, j, q: (q, i + m_off)) if ta \
        else pl.BlockSpec((tm, tk), lambda i, j, q: (i + m_off, q))
    b_spec = pl.BlockSpec((tn, tk), lambda i, j, q: (j, q)) if tb else pl.BlockSpec((tk, tn), lambda i, j, q: (q, j))
    o_spec = pl.BlockSpec((tm, tn), lambda i, j, q: (i, j))
    o_shape = (m, n)
    dims = (((0 if ta else 1,), (1 if tb else 0,)), ((), ()))
    has_add = add is not None
    has_dep = dep is not None

    def body(*refs):
        if has_dep:
            refs = refs[:-3] + refs[-2:]
        if has_add:
            a_ref, b_ref, add_ref, o_ref, acc_ref = refs
        else:
            a_ref, b_ref, o_ref, acc_ref = refs
        q = pl.program_id(2)
        part = lax.dot_general(a_ref[...], b_ref[...], dims, preferred_element_type=F32)

        def finish(acc):
            if has_add:
                acc = acc + add_ref[...]
            o_ref[...] = acc.astype(out_dtype)

        if nk == 1:
            finish(part)
        else:
            @pl.when(q == 0)
            def _():
                acc_ref[...] = part

            @pl.when(jnp.logical_and(q > 0, q < nk - 1))
            def _():
                acc_ref[...] += part

            @pl.when(q == nk - 1)
            def _():
                finish(acc_ref[...] + part)

    in_specs = [a_spec, b_spec] + ([o_spec] if has_add else [])
    args = (a, b) + ((add,) if has_add else ())
    if has_dep:
        in_specs.append(pl.BlockSpec((SUBLANES, LANES), lambda i, j, q: (0, 0)))
        args += (dep,)
    acc_shape = (tm, tn) if nk > 1 else (SUBLANES, LANES)
    return pl.pallas_call(
        body, name=name,
        grid=(m // tm, n // tn, nk),
        in_specs=in_specs, out_specs=o_spec,
        out_shape=jax.ShapeDtypeStruct(o_shape, out_dtype),
        scratch_shapes=[pltpu.VMEM(acc_shape, F32)],
        compiler_params=_params(("parallel", "parallel", "arbitrary")),
    )(*args)


def _rms_fwd(h, g, *, name):
    tp, d = h.shape
    tr = _tile(tp, 1408, 2 * SUBLANES)

    def body(h_ref, g_ref, o_ref):
        hv = h_ref[...]
        rstd = lax.rsqrt(jnp.mean(hv * hv, axis=-1, keepdims=True) + RMS_EPS)
        o_ref[...] = (hv * rstd * g_ref[...]).astype(BF16)

    return pl.pallas_call(
        body, name=name, grid=(tp // tr,),
        in_specs=[pl.BlockSpec((tr, d), lambda i: (i, 0)), pl.BlockSpec((1, d), lambda i: (0, 0))],
        out_specs=pl.BlockSpec((tr, d), lambda i: (i, 0)),
        out_shape=jax.ShapeDtypeStruct((tp, d), BF16),
        compiler_params=_params(("parallel",)),
    )(h, g.reshape(1, d))


def _rms_fwd_first(x, meta, g, *, tp, name):
    seq, d = x.shape
    n_meta = meta.shape[0]
    n_tok = n_meta + seq
    tr = TOKEN_TILE
    assert tp % tr == 0 and tr % n_meta == 0
    per = tr // n_meta

    def body(x_ref, xp_ref, m_ref, g_ref, h_ref, o_ref):
        i = pl.program_id(0)
        head = jnp.where(i == 0, m_ref[...], xp_ref[...])
        rows = i * tr + lax.broadcasted_iota(jnp.int32, (tr, 1), 0)
        hv = jnp.where(rows < n_tok, jnp.concatenate([head, x_ref[:tr - n_meta, :]], axis=0), 0.0)
        h_ref[...] = hv
        rstd = lax.rsqrt(jnp.mean(hv * hv, axis=-1, keepdims=True) + RMS_EPS)
        o_ref[...] = (hv * rstd * g_ref[...]).astype(BF16)

    row = pl.BlockSpec((tr, d), lambda i: (i, 0))
    own = pl.BlockSpec((tr, d), lambda i: (jnp.minimum(i, -(-seq // tr) - 1), 0))
    before = pl.BlockSpec((n_meta, d), lambda i: (jnp.maximum(i * per - 1, 0), 0))
    return pl.pallas_call(
        body, name=name, grid=(tp // tr,),
        in_specs=[own, before, pl.BlockSpec((n_meta, d), lambda i: (0, 0)), pl.BlockSpec((1, d), lambda i: (0, 0))],
        out_specs=[row, row],
        out_shape=[jax.ShapeDtypeStruct((tp, d), F32), jax.ShapeDtypeStruct((tp, d), BF16)],
        compiler_params=_params(("parallel",)),
    )(x, x, meta, g.reshape(1, d))


def _rms_bwd(h, dhn, dout, g, *, name):
    tp, d = h.shape
    tr = _tile(tp, 528, 2 * SUBLANES)

    def body(h_ref, dhn_ref, dout_ref, g_ref, dh_ref, dhb_ref, dg_ref):
        hv = h_ref[...]
        rstd = lax.rsqrt(jnp.mean(hv * hv, axis=-1, keepdims=True) + RMS_EPS)
        xhat = hv * rstd
        dn = dhn_ref[...]
        dxhat = dn * g_ref[...]
        dh = dout_ref[...] + rstd * (dxhat - xhat * jnp.mean(dxhat * xhat, axis=-1, keepdims=True))
        dh_ref[...] = dh
        dhb_ref[...] = dh.astype(BF16)
        part = jnp.sum(dn * xhat, axis=0, keepdims=True)

        @pl.when(pl.program_id(0) == 0)
        def _():
            dg_ref[...] = part

        @pl.when(pl.program_id(0) > 0)
        def _():
            dg_ref[...] += part

    row = pl.BlockSpec((tr, d), lambda i: (i, 0))
    vec = pl.BlockSpec((1, d), lambda i: (0, 0))
    return pl.pallas_call(
        body, name=name, grid=(tp // tr,),
        in_specs=[row, row, row, vec],
        out_specs=[row, row, vec],
        out_shape=[jax.ShapeDtypeStruct((tp, d), F32), jax.ShapeDtypeStruct((tp, d), BF16),
                   jax.ShapeDtypeStruct((1, d), F32)],
        compiler_params=_params(("arbitrary",)),
    )(h, dhn, dout, g.reshape(1, d))


def _rms_bwd_first(h, dhn, dout, g, *, n_meta, seq, name):
    tp, d = h.shape
    tr = _tile(seq, 512)
    assert seq % tr == 0 and tr % n_meta == 0 and tp >= seq + n_meta
    nt = seq // tr
    per = tr // n_meta

    def grads(hv, dn, do, gv):
        rstd = lax.rsqrt(jnp.mean(hv * hv, axis=-1, keepdims=True) + RMS_EPS)
        xhat = hv * rstd
        dxhat = dn * gv
        dh = do + rstd * (dxhat - xhat * jnp.mean(dxhat * xhat, axis=-1, keepdims=True))
        return dh, jnp.sum(dn * xhat, axis=0, keepdims=True)

    def body(h_ref, dhn_ref, dout_ref, hn_ref, dhnn_ref, doutn_ref, g_ref, gx_ref, dmeta_ref, dg_ref):
        i = pl.program_id(0)
        gv = g_ref[...]
        dh, part = grads(h_ref[...], dhn_ref[...], dout_ref[...], gv)
        dh_next, part_next = grads(hn_ref[...], dhnn_ref[...], doutn_ref[...], gv)
        gx_ref[...] = jnp.concatenate([dh[n_meta:], dh_next], axis=0)

        @pl.when(i == 0)
        def _():
            dmeta_ref[...] = dh[:n_meta]
            dg_ref[...] = part

        @pl.when(i > 0)
        def _():
            dg_ref[...] += part

        @pl.when(i == nt - 1)
        def _():
            dg_ref[...] += part_next

    row = pl.BlockSpec((tr, d), lambda i: (i, 0))
    nxt = pl.BlockSpec((n_meta, d), lambda i: ((i + 1) * per, 0))
    vec = pl.BlockSpec((1, d), lambda i: (0, 0))
    return pl.pallas_call(
        body, name=name, grid=(nt,),
        in_specs=[row, row, row, nxt, nxt, nxt, vec],
        out_specs=[pl.BlockSpec((None, tr, d), lambda i: (0, i, 0)), pl.BlockSpec((n_meta, d), lambda i: (0, 0)), vec],
        out_shape=[jax.ShapeDtypeStruct((1, seq, d), F32), jax.ShapeDtypeStruct((n_meta, d), F32),
                   jax.ShapeDtypeStruct((1, d), F32)],
        compiler_params=_params(("arbitrary",)),
    )(h, dhn, dout, h, dhn, dout, g.reshape(1, d))


def _loss_head(h, tgt, g, *, n_meta, n_tok, name):
    tp, d = h.shape
    seq = tgt.shape[0]
    tr = TOKEN_TILE
    assert tp % tr == 0 and tr % n_meta == 0
    per = tr // n_meta

    def body(h_ref, t_ref, tp_ref, g_ref, dh_ref, dhb_ref, dg_ref, loss_ref):
        i = pl.program_id(0)
        hv = h_ref[...]
        rstd = lax.rsqrt(jnp.mean(hv * hv, axis=-1, keepdims=True) + RMS_EPS)
        xhat = hv * rstd
        gv = g_ref[...]
        rows = i * tr + lax.broadcasted_iota(jnp.int32, (tr, 1), 0)
        valid = jnp.logical_and(rows >= n_meta, rows < n_tok)
        target = jnp.concatenate([tp_ref[...], t_ref[:tr - n_meta, :]], axis=0)
        err = jnp.where(valid, xhat * gv - target, 0.0)
        dy = err * (1.0 / d)
        dxhat = dy * gv
        dh = rstd * (dxhat - xhat * jnp.mean(dxhat * xhat, axis=-1, keepdims=True))
        dh_ref[...] = dh
        dhb_ref[...] = dh.astype(BF16)
        dg_part = jnp.sum(dy * xhat, axis=0, keepdims=True)
        per_row = jnp.sum(err * err, axis=-1, keepdims=True) * (1.0 / d)
        loss_part = jnp.broadcast_to(0.5 * jnp.sum(per_row, axis=0, keepdims=True), (SUBLANES, LANES))

        @pl.when(i == 0)
        def _():
            dg_ref[...] = dg_part
            loss_ref[...] = loss_part

        @pl.when(i > 0)
        def _():
            dg_ref[...] += dg_part
            loss_ref[...] += loss_part

    row = pl.BlockSpec((tr, d), lambda i: (i, 0))
    vec = pl.BlockSpec((1, d), lambda i: (0, 0))
    own = pl.BlockSpec((tr, d), lambda i: (jnp.minimum(i, -(-seq // tr) - 1), 0))
    before = pl.BlockSpec((n_meta, d), lambda i: (jnp.maximum(i * per - 1, 0), 0))
    return pl.pallas_call(
        body, name=name, grid=(tp // tr,),
        in_specs=[row, own, before, vec],
        out_specs=[row, row, vec, pl.BlockSpec((SUBLANES, LANES), lambda i: (0, 0))],
        out_shape=[jax.ShapeDtypeStruct((tp, d), F32), jax.ShapeDtypeStruct((tp, d), BF16),
                   jax.ShapeDtypeStruct((1, d), F32), jax.ShapeDtypeStruct((SUBLANES, LANES), F32)],
        compiler_params=_params(("arbitrary",)),
    )(h, tgt, tgt, g.reshape(1, d))


def _shift_down(halo, tile, s):
    if s == 0:
        return tile
    ext = jnp.concatenate([halo, tile], axis=0)
    return pltpu.roll(ext, s, 0)[SUBLANES:]


def _shift_up(tile, head, s):
    if s == 0:
        return tile
    ext = jnp.concatenate([tile, head], axis=0)
    n = ext.shape[0]
    return pltpu.roll(ext, n - s, 0)[: tile.shape[0]]


def _to_lane_blocks(ref, cols, val):
    for j in range(cols.start // LANES, cols.stop // LANES):
        ref[j] = val[:, j * LANES - cols.start:(j + 1) * LANES - cols.start]


def _from_lane_blocks(ref, cols):
    return jnp.concatenate([ref[j] for j in range(cols.start // LANES, cols.stop // LANES)], axis=1)


def _scan_tile(a_ref, b_ref, out_ref, carry, j, *, reverse):
    ng = a_ref.shape[1] // SUBLANES
    order = list(range(SUBLANES))[::-1] if reverse else list(range(SUBLANES))

    def rows(r):
        return pl.ds(r, ng, stride=SUBLANES)

    prod, loc = {}, {}
    prev = None
    for r in order:
        ar = a_ref[j, rows(r), :]
        br = b_ref[j, rows(r), :]
        prod[r] = ar if prev is None else ar * prod[prev]
        loc[r] = br if prev is None else ar * loc[prev] + br
        prev = r
    pg, lg = prod[prev], loc[prev]
    ones = jnp.ones((SUBLANES,) + pg.shape[1:], F32)
    zeros = jnp.zeros_like(ones)
    s = 1
    while s < ng:
        p_sh = _shift_up(pg, ones, s) if reverse else _shift_down(ones, pg, s)
        l_sh = _shift_up(lg, zeros, s) if reverse else _shift_down(zeros, lg, s)
        lg = pg * l_sh + lg
        pg = pg * p_sh
        s *= 2
    leaving = pg * carry[0:1, :] + lg
    entering = _shift_up(leaving, carry, 1) if reverse else _shift_down(carry, leaving, 1)
    for r in order:
        out_ref[j, rows(r), :] = loc[r] + prod[r] * entering
    last = leaving[0:1, :] if reverse else leaving[ng - 1:ng, :]
    return jnp.broadcast_to(last, carry.shape)


def _gates(ca, wr, wi, br, bi, sp):
    cab = ca.astype(BF16)
    r = _sigmoid(jnp.dot(cab, wr, preferred_element_type=F32) + br)
    ig = _sigmoid(jnp.dot(cab, wi, preferred_element_type=F32) + bi)
    la = -LRU_C * r * sp
    a = jnp.exp(la)
    mult = jnp.sqrt(-jnp.tanh(la) * (a * a + 1.0))
    return r, ig, a, mult


def _mixer_fwd(u, wa, ba, wr_blk, br, wi_blk, bi, lam, wb, *, name):
    tp, din = u.shape
    dl = din // 6
    tt = MIX_ROWS
    cw = GATE_BLOCK
    nch = dl // cw
    assert tp % tt == 0 and dl % cw == 0

    def body(u_ref, wa_ref, ba_ref, wr_ref, br_ref, wi_ref, bi_ref, lam_ref, wb_ref,
             s_ref, y_ref, xa_tail, v_tail, h_carry, a_s, b_s, h_s):
        @pl.when(pl.program_id(0) == 0)
        def _():
            xa_tail[...] = jnp.zeros_like(xa_tail)
            v_tail[...] = jnp.zeros_like(v_tail)
            h_carry[...] = jnp.zeros_like(h_carry)

        for sub in range(MIX_SUBTILES):
            rows = slice(sub * tt, (sub + 1) * tt)
            for ch in range(nch):
                cs = slice(ch * cw, (ch + 1) * cw)

                def seg(s):
                    return slice(s * dl + ch * cw, s * dl + (ch + 1) * cw)

                xa = u_ref[rows, seg(0)]
                halo = xa_tail[:, cs]
                ca = ba_ref[:, cs] + wa_ref[3:4, cs] * xa
                for kk in range(3):
                    ca = ca + wa_ref[kk:kk + 1, cs] * _shift_down(halo, xa, 3 - kk)
                xa_tail[:, cs] = xa[tt - SUBLANES:]
                s_ref[rows, cs] = ca
                sp = _softplus(-lam_ref[:, cs])
                _, ig, a, mult = _gates(ca, wr_ref[ch], wi_ref[ch], br_ref[:, cs], bi_ref[:, cs], sp)
                _to_lane_blocks(a_s, cs, a)
                _to_lane_blocks(b_s, cs, mult * (ig * ca))

                bv = u_ref[rows, seg(2)]
                v = u_ref[rows, seg(3)] * u_ref[rows, seg(4)]
                gb = u_ref[rows, seg(5)]
                vh = v_tail[:, cs]
                cb = wb_ref[2:3, cs] * v
                for kk in range(2):
                    cb = cb + wb_ref[kk:kk + 1, cs] * _shift_down(vh, v, 2 - kk)
                v_tail[:, cs] = v[tt - SUBLANES:]
                y_ref[rows, dl + ch * cw: dl + (ch + 1) * cw] = (bv * cb * (gb * _sigmoid(gb))).astype(BF16)

            for ch in range(nch):
                cs = slice(ch * cw, (ch + 1) * cw)
                for j in range(cs.start // LANES, cs.stop // LANES):
                    lanes = slice(j * LANES, (j + 1) * LANES)
                    h_carry[:, lanes] = _scan_tile(a_s, b_s, h_s, h_carry[:, lanes], j, reverse=False)
                hsv = _from_lane_blocks(h_s, cs)
                s_ref[rows, dl + ch * cw: dl + (ch + 1) * cw] = hsv
                ga = u_ref[rows, dl + ch * cw: dl + (ch + 1) * cw]
                y_ref[rows, cs] = (hsv * (ga * _sigmoid(ga))).astype(BF16)

    tb = tt * MIX_SUBTILES
    assert tp % tb == 0
    row = lambda w: pl.BlockSpec((tb, w), lambda i: (i, 0))
    full = lambda shp: pl.BlockSpec(shp, lambda i: tuple(0 for _ in shp))
    return pl.pallas_call(
        body, name=name, grid=(tp // tb,),
        in_specs=[row(din), full((4, dl)), full((1, dl)), full((nch, cw, cw)), full((1, dl)),
                  full((nch, cw, cw)), full((1, dl)), full((1, dl)), full((3, dl))],
        out_specs=[row(2 * dl), row(2 * dl)],
        out_shape=[jax.ShapeDtypeStruct((tp, 2 * dl), F32), jax.ShapeDtypeStruct((tp, 2 * dl), BF16)],
        scratch_shapes=[pltpu.VMEM((SUBLANES, dl), F32), pltpu.VMEM((SUBLANES, dl), F32),
                        pltpu.VMEM((SUBLANES, dl), F32)] + [pltpu.VMEM((dl // LANES, tt, LANES), F32)] * 3,
        compiler_params=_params(("arbitrary",)),
    )(u, wa, ba, wr_blk, br, wi_blk, bi, lam, wb)


SG_WA, SG_BA, SG_BR, SG_BI, SG_LAM, SG_WB, SG_ROWS = 0, 4, 5, 6, 7, 8, 16


def _mixer_bwd(u, saved, dy, wa, wr_blk, br, wi_blk, bi, lam, wb, *, name):
    tp, din = u.shape
    dl = din // 6
    tt = MIX_ROWS
    cw = GATE_BLOCK
    nch = dl // cw
    tb = tt * MIX_SUBTILES
    assert tp % tb == 0
    nt = tp // tb
    hb = tb // SUBLANES
    tn_dims = (((0,), (0,)), ((), ()))
    nt_dims = (((1,), (1,)), ((), ()))

    def body(u_ref, uh_ref, s_ref, sh_ref, dy_ref, wa_ref, wr_ref, br_ref, wi_ref, bi_ref, lam_ref, wb_ref,
             du_ref, sg_ref, dwr_ref, dwi_ref,
             g_carry, a_head, dca_head, dcb_head, r_s, i_s, a_s, an_s, d_s, g_s):
        i = pl.program_id(0)
        first_tile = i == nt - 1

        @pl.when(i == 0)
        def _():
            for ref in (g_carry, a_head, dca_head, dcb_head, sg_ref, dwr_ref, dwi_ref):
                ref[...] = jnp.zeros_like(ref)

        def halo_of(x):
            return jnp.where(first_tile, 0.0, x)

        for sub in reversed(range(MIX_SUBTILES)):
            rows = slice(sub * tt, (sub + 1) * tt)

            def before(ref, halo_ref, cols):
                if sub == 0:
                    return halo_of(halo_ref[:, cols])
                return ref[sub * tt - SUBLANES:sub * tt, cols]

            for ch in range(nch):
                cs = slice(ch * cw, (ch + 1) * cw)
                cav = s_ref[rows, cs]
                sp = _softplus(-lam_ref[:, cs])
                r, ig, a, _ = _gates(cav, wr_ref[ch], wi_ref[ch], br_ref[:, cs], bi_ref[:, cs], sp)
                r_s[:, cs] = r
                i_s[:, cs] = ig
                a_s[:, cs] = a
                _to_lane_blocks(an_s, cs, _shift_up(a, a_head[:, cs], 1))
                a_head[:, cs] = a[:SUBLANES]
                ga = u_ref[rows, dl + ch * cw: dl + (ch + 1) * cw]
                _to_lane_blocks(d_s, cs, dy_ref[rows, cs] * (ga * _sigmoid(ga)))

            for j in range(dl // LANES):
                lanes = slice(j * LANES, (j + 1) * LANES)
                g_carry[:, lanes] = _scan_tile(an_s, d_s, g_s, g_carry[:, lanes], j, reverse=True)

            for ch in range(nch):
                cs = slice(ch * cw, (ch + 1) * cw)

                def acc_row(r0, val):
                    sg_ref[r0:r0 + 1, cs] += jnp.sum(val, axis=0, keepdims=True)

                def seg(s):
                    return slice(s * dl + ch * cw, s * dl + (ch + 1) * cw)

                cav = s_ref[rows, cs]
                r = r_s[:, cs]
                ig = i_s[:, cs]
                a = a_s[:, cs]
                g = _from_lane_blocks(g_s, cs)
                hsv = s_ref[rows, dl + ch * cw: dl + (ch + 1) * cw]
                lamv = lam_ref[:, cs]
                sp = _softplus(-lamv)
                la = -LRU_C * r * sp
                e2 = a * a
                one_m_e2 = -jnp.tanh(la) * (e2 + 1.0)
                mult = jnp.sqrt(one_m_e2)
                hprev = _shift_down(before(s_ref, sh_ref, slice(dl + ch * cw, dl + (ch + 1) * cw)), hsv, 1)
                icav = ig * cav
                dla = g * (hprev * a - icav * (e2 * lax.rsqrt(one_m_e2)))
                gm = g * mult
                dzi = gm * icav * (1.0 - ig)
                dca = gm * ig
                dla_r = dla * r
                dzr = dla_r * (1.0 - r) * (-LRU_C * sp)
                sg_ref[SG_LAM:SG_LAM + 1, cs] += jnp.sum(dla_r, axis=0, keepdims=True) * (LRU_C * _sigmoid(-lamv))
                acc_row(SG_BR, dzr)
                acc_row(SG_BI, dzi)
                dzr_b = dzr.astype(BF16)
                dzi_b = dzi.astype(BF16)
                cab = cav.astype(BF16)
                dca = dca + lax.dot_general(dzr_b, wr_ref[ch], nt_dims, preferred_element_type=F32)
                dca = dca + lax.dot_general(dzi_b, wi_ref[ch], nt_dims, preferred_element_type=F32)
                dwr_ref[ch] += lax.dot_general(cab, dzr_b, tn_dims, preferred_element_type=F32)
                dwi_ref[ch] += lax.dot_general(cab, dzi_b, tn_dims, preferred_element_type=F32)
                acc_row(SG_BA, dca)
                xa = u_ref[rows, seg(0)]
                head = dca_head[:, cs]
                dxa = wa_ref[3:4, cs] * dca
                acc_row(SG_WA + 3, dca * xa)
                for kk in range(3):
                    later = _shift_up(dca, head, 3 - kk)
                    acc_row(SG_WA + kk, later * xa)
                    dxa = dxa + wa_ref[kk:kk + 1, cs] * later
                dca_head[:, cs] = dca[:SUBLANES]
                ga = u_ref[rows, seg(1)]
                sga = _sigmoid(ga)
                dga = dy_ref[rows, cs] * hsv * (sga + (ga * sga) * (1.0 - sga))
                du_ref[rows, seg(0)] = dxa.astype(BF16)
                du_ref[rows, seg(1)] = dga.astype(BF16)

                bv = u_ref[rows, seg(2)]
                cv = u_ref[rows, seg(3)]
                xb = u_ref[rows, seg(4)]
                gb = u_ref[rows, seg(5)]
                dyb = dy_ref[rows, dl + ch * cw: dl + (ch + 1) * cw]
                v = cv * xb
                vh = before(u_ref, uh_ref, seg(3)) * before(u_ref, uh_ref, seg(4))
                v1 = _shift_down(vh, v, 1)
                v2 = _shift_down(vh, v, 2)
                cb = wb_ref[2:3, cs] * v + wb_ref[1:2, cs] * v1 + wb_ref[0:1, cs] * v2
                sgb = _sigmoid(gb)
                sl = gb * sgb
                dyb_b = dyb * bv
                dyb_cb = dyb * cb
                dcb = dyb_b * sl
                du_ref[rows, seg(2)] = (dyb_cb * sl).astype(BF16)
                du_ref[rows, seg(5)] = (dyb_cb * bv * (sgb + sl * (1.0 - sgb))).astype(BF16)
                bhead = dcb_head[:, cs]
                dv = wb_ref[2:3, cs] * dcb
                acc_row(SG_WB + 2, dcb * v)
                for kk in range(2):
                    later = _shift_up(dcb, bhead, 2 - kk)
                    acc_row(SG_WB + kk, later * v)
                    dv = dv + wb_ref[kk:kk + 1, cs] * later
                dcb_head[:, cs] = dcb[:SUBLANES]
                du_ref[rows, seg(3)] = (dv * xb).astype(BF16)
                du_ref[rows, seg(4)] = (dv * cv).astype(BF16)

    rev = lambda w: pl.BlockSpec((tb, w), lambda i: (nt - 1 - i, 0))
    halo = lambda w: pl.BlockSpec((SUBLANES, w), lambda i: (jnp.maximum((nt - 1 - i) * hb - 1, 0), 0))
    full = lambda shp: pl.BlockSpec(shp, lambda i: tuple(0 for _ in shp))
    vm = lambda r: pltpu.VMEM((r, dl), F32)
    return pl.pallas_call(
        body, name=name, grid=(nt,),
        in_specs=[rev(din), halo(din), rev(2 * dl), halo(2 * dl), rev(2 * dl), full((4, dl)),
                  full((nch, cw, cw)), full((1, dl)), full((nch, cw, cw)), full((1, dl)), full((1, dl)), full((3, dl))],
        out_specs=[rev(din), full((SG_ROWS, dl)), full((nch, cw, cw)), full((nch, cw, cw))],
        out_shape=[jax.ShapeDtypeStruct((tp, din), BF16), jax.ShapeDtypeStruct((SG_ROWS, dl), F32),
                   jax.ShapeDtypeStruct((nch, cw, cw), F32), jax.ShapeDtypeStruct((nch, cw, cw), F32)],
        scratch_shapes=[vm(SUBLANES), vm(SUBLANES), vm(SUBLANES), vm(SUBLANES), vm(tt), vm(tt), vm(tt)]
        + [pltpu.VMEM((dl // LANES, tt, LANES), F32)] * 3,
        compiler_params=_params(("arbitrary",)),
    )(u, u, saved, saved, dy, wa, wr_blk, br, wi_blk, bi, lam, wb)


def _adamw(w, g, m, v, *, name, landed=None, layer=None, depth=None, into=None, row_off=0, own=None):
    r, c = w.shape[-2:]
    rows = g.shape[1] if own == "lead" else g.shape[0]
    tr = _tile(rows, 512, 2 * SUBLANES)
    assert row_off % tr == 0
    boff = row_off // tr
    bc1 = 1.0 - ADAM_B1 ** ADAM_STEP
    bc2 = 1.0 - ADAM_B2 ** ADAM_STEP
    slots = landed is not None

    def body(*refs):
        if own is not None:
            refs = refs[1:]
        if into is not None:
            refs = refs[:-8] + refs[-4:]
        if slots:
            w_ref, g_ref, l_ref, m_ref, v_ref, grad_ref, delta_ref, nm_ref, nv_ref = refs
            gv = g_ref[...].astype(F32)
            for s in range(N_DEV - 1):
                gv = gv + l_ref[s].astype(F32)
        else:
            w_ref, g_ref, m_ref, v_ref, grad_ref, delta_ref, nm_ref, nv_ref = refs
            gv = g_ref[...]
        wv = w_ref[...]
        mn = ADAM_B1 * m_ref[...] + (1.0 - ADAM_B1) * gv
        vn = ADAM_B2 * v_ref[...] + (1.0 - ADAM_B2) * (gv * gv)
        m_hat = mn / bc1
        v_hat = vn / bc2
        grad_ref[...] = gv
        delta_ref[...] = -ADAM_LR * (m_hat / (jnp.sqrt(v_hat) + ADAM_EPS) + ADAM_WD * wv)
        nm_ref[...] = mn
        nv_ref[...] = vn

    if depth is None:
        blk = pl.BlockSpec((tr, c), lambda i, *_: (i + boff, 0))
    else:
        blk = pl.BlockSpec((None, tr, c), lambda i, *_: (layer, i + boff, 0))
    if own == "cols":
        g_blk = pl.BlockSpec((tr, c), lambda i, me_ref: (i, me_ref[0]))
    elif own == "lead":
        g_blk = pl.BlockSpec((None, tr, c), lambda i, me_ref: (me_ref[0], i, 0))
    else:
        g_blk = pl.BlockSpec((tr, c), lambda i, *_: (i, 0))
    l_spec = [pl.BlockSpec((N_DEV - 1, tr, c), lambda i, *_: (0, i, 0))] if slots else []
    args = (w, g, landed, m, v) if slots else (w, g, m, v)
    in_specs = [blk, g_blk] + l_spec + [blk, blk]
    if depth is None:
        shp = jax.ShapeDtypeStruct((r, c), F32)
        out_blk = blk
    else:
        shp = jax.ShapeDtypeStruct((depth, r, c), F32)
        out_blk = pl.BlockSpec((None, tr, c), lambda i, *_: (layer, i + boff, 0))
    prefetch = () if own is None else (_mesh_pos()[3].astype(jnp.int32).reshape(1),)
    aliases = {}
    if into is not None:
        aliases = {len(prefetch) + len(args) + j: j for j in range(4)}
        in_specs = in_specs + [ANY] * 4
        args = args + tuple(into)
    return pl.pallas_call(
        body, name=name,
        grid_spec=pltpu.PrefetchScalarGridSpec(
            num_scalar_prefetch=len(prefetch), grid=(rows // tr,), in_specs=in_specs, out_specs=[out_blk] * 4),
        out_shape=[shp] * 4, input_output_aliases=aliases,
        compiler_params=_params(("parallel",)),
    )(*prefetch, *args)


def _slot_sum(g, *, name):
    _, r, c = g.shape
    tr = _tile(r, 512, SUBLANES)

    def body(g_ref, o_ref):
        gv = g_ref[0].astype(F32)
        for s in range(1, N_DEV):
            gv = gv + g_ref[s].astype(F32)
        o_ref[...] = gv

    return pl.pallas_call(
        body, name=name, grid=(r // tr,),
        in_specs=[pl.BlockSpec((N_DEV, tr, c), lambda i: (0, i, 0))],
        out_specs=pl.BlockSpec((tr, c), lambda i: (i, 0)),
        out_shape=jax.ShapeDtypeStruct((r, c), F32),
        compiler_params=_params(("parallel",)),
    )(g)


def _mesh_pos():
    x, y, c = lax.axis_index("x"), lax.axis_index("y"), lax.axis_index("c")
    return x, y, c, 4 * x + 2 * y + c


ANY = pl.BlockSpec(memory_space=pl.ANY)


GATHER_COPIES = 9


def _all_gather(srcs, out_shapes, views, *, name):
    n = len(srcs)
    SIB, X_OWN, Y_OWN, X_DIAG, Y_DIAG, SIB_X, SIB_Y, SIB_DIAG_TOP, SIB_DIAG_BOTTOM = range(GATHER_COPIES)

    def body(*refs):
        src = refs[:n]
        dst = refs[n:2 * n]
        send_sems, recv_sems, local_sems = refs[2 * n:]
        x, y, c, me = _mesh_pos()
        sibling, x_nbr, y_nbr = (x, y, 1 - c), (1 - x, y, c), (x, 1 - y, c)

        def block(a, px, py, pc, half=None):
            win = views[a](dst[a], 4 * px + 2 * py + pc)
            if half is None:
                return win
            rows = win.shape[0] // 2
            return win.at[pl.ds(half * rows, rows)]

        def copy(a, k, win, to, from_src=False):
            return pltpu.make_async_remote_copy(
                src_ref=src[a] if from_src else win, dst_ref=win,
                send_sem=send_sems.at[a * GATHER_COPIES + k], recv_sem=recv_sems.at[a * GATHER_COPIES + k],
                device_id=to, device_id_type=MESH)

        mine = [pltpu.make_async_copy(src[a], block(a, x, y, c), local_sems.at[a]) for a in range(n)]
        started = []

        def start(cp):
            cp.start()
            started.append(cp)

        for a in range(n):
            mine[a].start()
            own = block(a, x, y, c)
            start(copy(a, SIB, own, sibling, True))
            start(copy(a, X_OWN, own, x_nbr, True))
            start(copy(a, Y_OWN, own, y_nbr, True))
        for a in range(n):
            from_y = block(a, x, 1 - y, c)
            copy(a, Y_OWN, from_y, y_nbr).wait_recv()
            start(copy(a, X_DIAG, block(a, x, 1 - y, c, 0), x_nbr))
            start(copy(a, SIB_Y, from_y, sibling))
            from_x = block(a, 1 - x, y, c)
            copy(a, X_OWN, from_x, x_nbr).wait_recv()
            start(copy(a, Y_DIAG, block(a, 1 - x, y, c, 1), y_nbr))
            start(copy(a, SIB_X, from_x, sibling))
        for a in range(n):
            top = block(a, 1 - x, 1 - y, c, 0)
            copy(a, X_DIAG, top, x_nbr).wait_recv()
            start(copy(a, SIB_DIAG_TOP, top, sibling))
            bottom = block(a, 1 - x, 1 - y, c, 1)
            copy(a, Y_DIAG, bottom, y_nbr).wait_recv()
            start(copy(a, SIB_DIAG_BOTTOM, bottom, sibling))
        for a in range(n):
            copy(a, SIB, block(a, x, y, 1 - c), sibling).wait_recv()
            copy(a, SIB_X, block(a, 1 - x, y, 1 - c), sibling).wait_recv()
            copy(a, SIB_Y, block(a, x, 1 - y, 1 - c), sibling).wait_recv()
            copy(a, SIB_DIAG_TOP, block(a, 1 - x, 1 - y, 1 - c, 0), sibling).wait_recv()
            copy(a, SIB_DIAG_BOTTOM, block(a, 1 - x, 1 - y, 1 - c, 1), sibling).wait_recv()
        for cp in started:
            cp.wait_send()
        for cp in mine:
            cp.wait()

    return pl.pallas_call(
        body, name=name,
        in_specs=[ANY] * n, out_specs=[ANY] * n,
        out_shape=[jax.ShapeDtypeStruct(s, x.dtype) for s, x in zip(out_shapes, srcs)],
        scratch_shapes=[pltpu.SemaphoreType.DMA((GATHER_COPIES * n,)), pltpu.SemaphoreType.DMA((GATHER_COPIES * n,)),
                        pltpu.SemaphoreType.DMA((n,))],
    )(*srcs)


HBM = pl.BlockSpec(memory_space=pltpu.HBM)
SEM = pl.BlockSpec(memory_space=pltpu.SEMAPHORE)
EFFECT = pltpu.SideEffectType.DATAFLOW_SIDE_EFFECTING


def _peer_of(x, y, c, k):
    return (1 - x if k & 4 else x, 1 - y if k & 2 else y, 1 - c if k & 1 else c)


def _peer_copies(n, wins, src, land, send_sems, recv_sems):
    x, y, c, me = _mesh_pos()
    out = []
    for a in range(n):
        for k in range(1, N_DEV):
            px, py, pc = _peer_of(x, y, c, k)
            plan = wins[a](src[a], land[a], me, 4 * px + 2 * py + pc, k)
            if plan is None:
                continue
            target = _peer_of(x, y, c, plan[2]) if len(plan) == 3 else (px, py, pc)
            out.append(pltpu.make_async_remote_copy(
                src_ref=plan[0], dst_ref=plan[1],
                send_sem=send_sems.at[a * 7 + k - 1], recv_sem=recv_sems.at[a * 7 + k - 1],
                device_id=target, device_id_type=MESH))
    return out


def _push_start(srcs, lands, wins, *, name):
    n = len(srcs)

    def body(*refs):
        src = refs[:n]
        land = refs[n:2 * n]
        send_sems, recv_sems = refs[2 * n], refs[2 * n + 1]
        token = refs[-1]
        for cp in _peer_copies(n, wins, src, land, send_sems, recv_sems):
            cp.start()
        token[...] = jnp.zeros_like(token)

    bufs = (*srcs, *lands)
    return pl.pallas_call(
        body, name=name,
        out_shape=(pltpu.SemaphoreType.DMA((7 * n,)), pltpu.SemaphoreType.DMA((7 * n,)),
                   *[pltpu.HBM(v.shape, v.dtype) for v in bufs], jax.ShapeDtypeStruct((SUBLANES, LANES), F32)),
        in_specs=[HBM] * (2 * n),
        out_specs=(SEM, SEM, *[HBM] * (2 * n), pl.BlockSpec(memory_space=pltpu.VMEM)),
        input_output_aliases={i: 2 + i for i in range(2 * n)},
        compiler_params=pltpu.CompilerParams(has_side_effects=EFFECT),
    )(*[pltpu.with_memory_space_constraint(v, pltpu.HBM) for v in bufs])


def _push_wait(handle, wins, after, *, name):
    send_sems, recv_sems, *bufs, _ = handle
    n = len(bufs) // 2

    def body(*refs):
        src = refs[:n]
        land = refs[n:2 * n]
        for cp in _peer_copies(n, wins, src, land, refs[2 * n], refs[2 * n + 1]):
            cp.wait_send()
            cp.wait_recv()

    outs = pl.pallas_call(
        body, name=name,
        out_shape=tuple(pltpu.HBM(v.shape, v.dtype) for v in bufs),
        in_specs=[HBM] * (2 * n) + [SEM, SEM, ANY],
        out_specs=tuple([HBM] * (2 * n)),
        input_output_aliases={i: i for i in range(2 * n)},
        compiler_params=pltpu.CompilerParams(has_side_effects=EFFECT),
    )(*bufs, send_sems, recv_sems, after)
    return outs[:n], outs[n:]


def _gather_lead(src, land, me, peer, k):
    return src, land.at[me]


SAME_CORE_PEERS = (2, 4, 6)
SIBLING = 1


def _gather_cols(width, ks=range(1, N_DEV)):
    def win(src, land, me, peer, k):
        return (src, land.at[:, pl.ds(me * width, width)]) if k in ks else None
    return win


def _forward_cols(width):
    def win(src, land, me, peer, k):
        block = land.at[:, pl.ds(peer * width, width)]
        return (block, block, SIBLING) if k in SAME_CORE_PEERS else None
    return win


def _scatter_lead(src, land, me, peer, k):
    return src.at[peer], land.at[k - 1]


def _scatter_cols(width):
    def win(src, land, me, peer, k):
        return src.at[:, pl.ds(peer * width, width)], land.at[k - 1]
    return win


def _place_block(own, *, cols, name):
    rows, width = own.shape
    tr = _tile(rows, 512, 2 * SUBLANES)
    _, _, _, me = _mesh_pos()

    def body(me_ref, x_ref, o_ref):
        o_ref[...] = x_ref[...]

    if cols:
        out_spec = pl.BlockSpec((tr, width), lambda i, me_ref: (i, me_ref[0]))
        shape = (rows, N_DEV * width)
    else:
        out_spec = pl.BlockSpec((None, tr, width), lambda i, me_ref: (me_ref[0], i, 0))
        shape = (N_DEV, rows, width)
    return pl.pallas_call(
        body, name=name,
        grid_spec=pltpu.PrefetchScalarGridSpec(
            num_scalar_prefetch=1, grid=(rows // tr,),
            in_specs=[pl.BlockSpec((tr, width), lambda i, me_ref: (i, 0))], out_specs=out_spec),
        out_shape=jax.ShapeDtypeStruct(shape, own.dtype),
        compiler_params=_params(("arbitrary",)),
    )(me.astype(jnp.int32).reshape(1), own)


def _dep(x, token):
    return x + token[0, 0].astype(x.dtype)


def _lead(ref, d):
    return ref.at[d]


def _col_window(width):
    def view(ref, d):
        return ref.at[:, pl.ds(d * width, width)]
    return view


def _pack(arrs):
    flat = jnp.concatenate([a.reshape(-1).astype(F32) for a in arrs])
    n = flat.shape[0]
    rows = -(-n // (2 * SUBLANES * LANES)) * 2 * SUBLANES
    return jnp.pad(flat, (0, rows * LANES - n)).reshape(rows, LANES)


def _unpack(buf, shapes):
    flat = buf.reshape(-1)
    out, off = [], 0
    for s in shapes:
        n = 1
        for q in s:
            n *= q
        out.append(flat[off:off + n].reshape(s))
        off += n
    return out


def _blockdiag(w, cw):
    h, hd, _ = w.shape
    per = cw // hd
    wg = w.reshape(h // per, per, hd, hd)
    eye = jnp.eye(per, dtype=w.dtype)
    blk = jnp.einsum("gpij,pq->gpiqj", wg, eye)
    return blk.reshape(h // per, cw, cw).astype(BF16)


def _blockdiag_extract(g, hd):
    n, cw, _ = g.shape
    per = cw // hd
    g5 = g.reshape(n, per, hd, per, hd)
    idx = jnp.arange(per)
    return g5[:, idx, :, idx, :].transpose(1, 0, 2, 3).reshape(n * per, hd, hd)


def kernel(x, meta, norm_g, w_in, conv_a_w, conv_a_b, lru_wr, lru_br, lru_wi, lru_bi, lru_lambda, conv_b_w, w_out, final_g, loss_target, m_meta, m_norm_g, m_w_in, m_conv_a_w, m_conv_a_b, m_lru_wr, m_lru_br, m_lru_wi, m_lru_bi, m_lru_lambda, m_conv_b_w, m_w_out, m_final_g, v_meta, v_norm_g, v_w_in, v_conv_a_w, v_conv_a_b, v_lru_wr, v_lru_br, v_lru_wi, v_lru_bi, v_lru_lambda, v_conv_b_w, v_w_out, v_final_g):
    _, seq, d = x.shape
    n_meta = meta.shape[0]
    depth = w_in.shape[0]
    din = w_in.shape[2] * N_DEV
    dl = din // 6
    dmix = 2 * dl
    wcol = w_in.shape[2]
    wrow = w_out.shape[1]
    mcol = meta.shape[1]
    ccol = conv_a_w.shape[2]
    hd = lru_wr.shape[2]
    n_tok = n_meta + seq
    tp = -(-n_tok // TOKEN_TILE) * TOKEN_TILE
    me = 4 * lax.axis_index("x") + 2 * lax.axis_index("y") + lax.axis_index("c")

    bf = lambda a: a.astype(BF16)
    small_mine = _pack([meta, conv_a_w, conv_b_w])
    first = _all_gather([bf(w_in[0]), small_mine], [(d, din), (N_DEV,) + small_mine.shape],
                        [_col_window(wcol), _lead], name="gather_first")
    flat = first[1].reshape(N_DEV, -1)
    sizes = [meta.size, conv_a_w.size, conv_b_w.size]
    meta_full = jnp.moveaxis(flat[:, :sizes[0]].reshape(N_DEV, n_meta, mcol), 0, 1).reshape(n_meta, d)
    wa_full = jnp.moveaxis(flat[:, sizes[0]:sizes[0] + sizes[1]].reshape(N_DEV, depth, 4, ccol), 0, 2) \
        .reshape(depth, 4, dl)
    wb_full = jnp.moveaxis(flat[:, sizes[0] + sizes[1]:sum(sizes)].reshape(N_DEV, depth, 3, ccol), 0, 2) \
        .reshape(depth, 3, dl)
    w_in_full = [None] * depth
    w_out_full = [None] * depth

    push_out = [None] * depth
    push_in = [None] * depth
    w_in_full[0], src = lax.optimization_barrier((first[0], bf(w_out[0])))
    push_out[0] = _push_start([src], [_place_block(src, cols=False, name="place_wout_0")], [_gather_lead],
                              name="gather_wout_0_start")
    token = push_out[0][-1]
    first_hop = [_gather_cols(wcol, ks=(SIBLING,) + SAME_CORE_PEERS)]
    second_hop = [_forward_cols(wcol)]
    for l in range(1, depth):
        src = bf(_dep(w_in[l], token))
        push_in[l] = _push_start([src], [_place_block(src, cols=True, name=f"place_win_{l}")], first_hop,
                                 name=f"gather_win_{l}_start")
        src = bf(_dep(w_out[l], push_in[l][-1]))
        push_out[l] = _push_start([src], [_place_block(src, cols=False, name=f"place_wout_{l}")], [_gather_lead],
                                  name=f"gather_wout_{l}_start")
        token = push_out[l][-1]

    wr_blk = [_blockdiag(lru_wr[l], GATE_BLOCK) for l in range(depth)]
    wi_blk = [_blockdiag(lru_wi[l], GATE_BLOCK) for l in range(depth)]
    vec = lambda a: a.reshape(1, dl)

    tm = _tile(tp, 1408)
    saved = []
    for l in range(depth):
        if l == 0:
            h, hn = _rms_fwd_first(x[0], meta_full, _dep(norm_g[l], token), tp=tp, name=f"rms_fwd_{l}")
        else:
            hn = _rms_fwd(h, _dep(norm_g[l], push_in[l][-1]), name=f"rms_fwd_{l}")
        if l > 0:
            _, landed = _push_wait(push_in[l], second_hop, hn, name=f"forward_win_{l}_wait")
            w_in_full[l] = landed[0]
        u = _matmul(hn, w_in_full[l], tm=tm, tn=_tile(din, 1536), tk=d, name=f"mm_u_{l}")
        mixed, y = _mixer_fwd(u, wa_full[l], vec(conv_a_b[l]), wr_blk[l], vec(lru_br[l]), wi_blk[l], vec(lru_bi[l]),
                              vec(lru_lambda[l]), wb_full[l], name=f"mixer_fwd_{l}")
        _, landed = _push_wait(push_out[l], [_gather_lead], y, name=f"gather_wout_{l}_wait")
        w_out_full[l] = landed[0].reshape(dmix, d)
        h_next = _matmul(y, w_out_full[l], tm=tm, tn=_tile(d, 512), tk=dmix, add=h, name=f"mm_out_{l}")
        if l + 1 < depth:
            src, landed = _push_wait(push_in[l + 1], first_hop, h_next, name=f"gather_win_{l + 1}_wait")
            push_in[l + 1] = _push_start(src, landed, second_hop, name=f"forward_win_{l + 1}_start")
        saved.append((h, hn, u, mixed, y))
        h = h_next

    dh, dhb, dg_final, loss_part = _loss_head(h, loss_target[0], final_g, n_meta=n_meta, n_tok=n_tok,
                                              name="loss_head")

    small_grads = [None] * depth
    sent_out = [None] * depth
    sent_in = [None] * depth
    scatter_in = [_scatter_cols(wcol)]
    scatter_both = [_scatter_cols(wcol), _scatter_lead]
    token = None
    dg_norms = []
    for l in reversed(range(depth)):
        h_in, hn, u, mixed, y = saved[l]
        dy = _matmul(dhb, w_out_full[l], tb=True, tm=tm, tn=_tile(dmix, 1024), tk=d, dep=token, name=f"mm_dy_{l}")
        dw_out = _matmul(y, dhb, ta=True, tm=_tile(dmix, 1024), tn=_tile(d, 1024), tk=tp, out_dtype=BF16,
                         name=f"mm_dwout_{l}")
        dw_out = dw_out.reshape(N_DEV, wrow, d)
        wb_l = wb_full[l]
        if l == 0:
            sent_out[l] = _push_start([dw_out], [lax.empty((N_DEV - 1, wrow, d), BF16)],
                                      [_scatter_lead], name=f"scatter_wout_{l}_start")
            wb_l = _dep(wb_l, sent_out[l][-1])
        du, sg, dwr, dwi = _mixer_bwd(u, mixed, dy, wa_full[l], wr_blk[l], vec(lru_br[l]), wi_blk[l], vec(lru_bi[l]),
                                      vec(lru_lambda[l]), wb_l, name=f"mixer_bwd_{l}")
        small_grads[l] = (sg, dwr, dwi)
        if l == 0:
            rows = jnp.stack([small_grads[j][0] for j in range(depth)])
            early = [_pack([
                rows[:, SG_BA], rows[:, SG_BR], rows[:, SG_BI], rows[:, SG_LAM], rows[:, SG_WA:SG_WA + 4],
                rows[:, SG_WB:SG_WB + 3], dg_final[0], *dg_norms]),
                _pack([jnp.stack([_blockdiag_extract(small_grads[j][1], hd) for j in range(depth)]),
                       jnp.stack([_blockdiag_extract(small_grads[j][2], hd) for j in range(depth)])]).astype(BF16)]
            early_land = [lax.dynamic_update_slice(lax.empty((N_DEV,) + a.shape, a.dtype), a[None], (me, 0, 0))
                          for a in early]
            sent_early = _push_start(early, early_land, [_gather_lead] * 2, name="gather_early_grads_start")
        parts = 2 if l == 0 else 1
        token = sent_early[-1] if l == 0 else None
        sent_in[l] = []
        for p in range(parts):
            dw_in = _matmul(hn, du, ta=True, tm=_tile(d // parts, 512), tn=_tile(din, 1536), tk=tp, out_dtype=BF16,
                            dep=token, m_part=(p, parts), name=f"mm_dwin_{l}_{p}")
            if l == 0:
                sent_in[l].append(_push_start([dw_in], [lax.empty((N_DEV - 1, d // parts, wcol), BF16)], scatter_in,
                                              name=f"scatter_win_{l}_{p}_start"))
            else:
                sent_in[l].append(_push_start(
                    [dw_in, dw_out], [lax.empty((N_DEV - 1, d, wcol), BF16), lax.empty((N_DEV - 1, wrow, d), BF16)],
                    scatter_both, name=f"scatter_grads_{l}_start"))
            token = sent_in[l][-1][-1]
        dhn = _matmul(du, w_in_full[l], tb=True, tm=_tile(tp, 528, 2 * SUBLANES), tn=_tile(d, 1024), tk=din, dep=token,
                      name=f"mm_dhn_{l}")
        if l > 0:
            dh, dhb, dg_norm = _rms_bwd(h_in, dhn, dh, norm_g[l], name=f"rms_bwd_{l}")
            dg_norms.append(dg_norm[0])
        else:
            grad_x, d_meta, dg_norm = _rms_bwd_first(h_in, dhn, dh, norm_g[l], n_meta=n_meta, seq=seq,
                                                     name=f"rms_bwd_{l}")

    big = {"win": None, "wout": None}

    def big_adamw(l, after):
        if l > 0:
            src, landed = _push_wait(sent_in[l][0], scatter_both, after, name=f"scatter_grads_{l}_wait")
            big["wout"] = _adamw(w_out, src[1], m_w_out, v_w_out, landed=landed[1], layer=l, depth=depth,
                                 into=big["wout"], own="lead", name=f"adamw_w_out_{l}")
            big["win"] = _adamw(w_in, src[0], m_w_in, v_w_in, landed=landed[0], layer=l, depth=depth,
                                into=big["win"], own="cols", name=f"adamw_w_in_{l}_0")
            return big["win"][0]
        src, landed = _push_wait(sent_out[l], [_scatter_lead], after, name=f"scatter_wout_{l}_wait")
        big["wout"] = _adamw(w_out, src[0], m_w_out, v_w_out, landed=landed[0], layer=l, depth=depth,
                             into=big["wout"], own="lead", name=f"adamw_w_out_{l}")
        after = big["wout"][0]
        for p, sent in enumerate(sent_in[l]):
            src, landed = _push_wait(sent, scatter_in, after, name=f"scatter_win_{l}_{p}_wait")
            big["win"] = _adamw(w_in, src[0], m_w_in, v_w_in, landed=landed[0], layer=l, depth=depth,
                                into=big["win"], row_off=p * src[0].shape[0], own="cols",
                                name=f"adamw_w_in_{l}_{p}")
            after = big["win"][0]
        return after

    after = dg_norm
    for l in reversed(range(1, depth)):
        after = big_adamw(l, after)

    late = _pack([dg_norm[0], d_meta, loss_part[0:1, 0:1]])
    if depth > 1:
        late, after = lax.optimization_barrier((late, after))
    late_all = _all_gather([late], [(N_DEV,) + late.shape], [_lead], name="gather_late_grads")[0]
    late_sum = _unpack(_slot_sum(late_all, name="sum_late_grads"), [(d,), (n_meta, d), ()])
    loss = late_sum[2]
    _, early_all = _push_wait(sent_early, [_gather_lead] * 2, late_sum[0], name="gather_early_grads_wait")
    vec_shapes = [conv_a_b.shape, lru_br.shape, lru_bi.shape, lru_lambda.shape, (depth, 4, dl), (depth, 3, dl),
                  final_g.shape] + [(d,)] * (depth - 1)
    e = _unpack(_slot_sum(early_all[0], name="sum_early_vectors"), vec_shapes)
    g_wr, g_wi = _unpack(_slot_sum(early_all[1], name="sum_early_maps"), [lru_wr.shape, lru_wi.shape])
    g_norm = jnp.stack([late_sum[0]] + e[7:][::-1])
    g_meta = lax.dynamic_slice_in_dim(late_sum[1], me * mcol, mcol, axis=1)
    g_wa = lax.dynamic_slice_in_dim(e[4], me * ccol, ccol, axis=2)
    g_wb = lax.dynamic_slice_in_dim(e[5], me * ccol, ccol, axis=2)

    small_w = [norm_g, conv_a_b, lru_wr, lru_br, lru_wi, lru_bi, lru_lambda, final_g, meta, conv_a_w, conv_b_w]
    small_m = [m_norm_g, m_conv_a_b, m_lru_wr, m_lru_br, m_lru_wi, m_lru_bi, m_lru_lambda, m_final_g, m_meta,
               m_conv_a_w, m_conv_b_w]
    small_v = [v_norm_g, v_conv_a_b, v_lru_wr, v_lru_br, v_lru_wi, v_lru_bi, v_lru_lambda, v_final_g, v_meta,
               v_conv_a_w, v_conv_b_w]
    small_g = [g_norm, e[0], g_wr, e[1], g_wi, e[2], e[3], e[6], g_meta, g_wa, g_wb]
    small_out = _adamw(_pack(small_w), _pack(small_g), _pack(small_m), _pack(small_v), name="adamw_small")
    small_shapes = [a.shape for a in small_w]
    s_grad, s_delta, s_m, s_v = [_unpack(o, small_shapes) for o in small_out]

    big_adamw(0, small_out[0])
    win_out, wout_out = big["win"], big["wout"]

    names = ["norm_g", "conv_a_b", "lru_wr", "lru_br", "lru_wi", "lru_bi", "lru_lambda", "final_g", "meta",
             "conv_a_w", "conv_b_w"]
    order = ["meta", "norm_g", "w_in", "conv_a_w", "conv_a_b", "lru_wr", "lru_br", "lru_wi", "lru_bi", "lru_lambda",
             "conv_b_w", "w_out", "final_g"]

    def family(idx, small):
        table = {nm: small[i] for i, nm in enumerate(names)}
        table["w_in"] = win_out[idx]
        table["w_out"] = wout_out[idx]
        return [table[nm] for nm in order]

    return (loss, grad_x, *family(0, s_grad), *family(1, s_delta), *family(2, s_m), *family(3, s_v))
```
